```python
import math
import jax, jax.numpy as jnp
from jax import lax
import numpy as np

D_MODEL = 1024
BATCH = 16
SEQ = 4096
DEPTH = 4

N_MIXERS = 4
D_FF = 4 * D_MODEL
EPS = 1e-6

SSM_GROUP = 16
SSM_GROUPS = D_MODEL // SSM_GROUP
SSM_STATE = 64
DT_MIN = 1e-3
DT_MAX = 1e-1

CONV_WIDTH = 31

GMLP_CHUNK = 128
GMLP_HEADS = 4
GMLP_WIDTH = D_MODEL

ATT_CONFIGS = ((128, 1), (512, 4), (2048, 16))
ATT_GROUPS = len(ATT_CONFIGS)
ATT_HEADS = 8
HEAD_DIM = 64
ATT_GROUP_WIDTH = ATT_HEADS * HEAD_DIM

kernel_name = "interleaved_s5_conv_gmlp_dilated_attn_trunk"


def _rmsnorm(x, g):
    xf = x.astype(jnp.float32)
    y = xf * lax.rsqrt(jnp.mean(xf * xf, axis=-1, keepdims=True) + EPS)
    return (y * g.astype(jnp.float32)).astype(x.dtype)


def _layernorm(x, g, b):
    xf = x.astype(jnp.float32)
    mu = jnp.mean(xf, axis=-1, keepdims=True)
    var = jnp.mean(jnp.square(xf - mu), axis=-1, keepdims=True)
    y = (xf - mu) * lax.rsqrt(var + EPS)
    return (y * g.astype(jnp.float32) + b.astype(jnp.float32)).astype(x.dtype)


def _s5_mixer(h, a_re, a_im, b_re, b_im, c_re, c_im, d_skip, log_dt, w_glu):
    bsz, s, _ = h.shape
    f32 = jnp.float32
    u = h.astype(f32).reshape(bsz, s, SSM_GROUPS, SSM_GROUP)
    a = lax.complex(a_re.astype(f32), a_im.astype(f32))
    dt = jnp.exp(log_dt.astype(f32))[:, None]
    a_bar = jnp.exp(a * dt)
    b_mat = lax.complex(b_re.astype(f32), b_im.astype(f32))
    b_bar = ((a_bar - 1.0) / a)[..., None] * b_mat
    bu = jnp.einsum('bsgp,gnp->sbgn', u.astype(jnp.complex64), b_bar)
    a_elems = jnp.broadcast_to(a_bar[None, None], (s, 1, SSM_GROUPS, SSM_STATE))

    def combine(left, right):
        a_l, b_l = left
        a_r, b_r = right
        return a_r * a_l, a_r * b_l + b_r

    _, states = lax.associative_scan(combine, (a_elems, bu), axis=0)
    c_mat = lax.complex(c_re.astype(f32), c_im.astype(f32))
    y = jnp.real(jnp.einsum('sbgn,gpn->bsgp', states, c_mat))
    y = y + d_skip.astype(f32).reshape(SSM_GROUPS, SSM_GROUP) * u
    y = jax.nn.gelu(y.reshape(bsz, s, D_MODEL)).astype(h.dtype)
    z = y @ w_glu
    return z[..., :D_MODEL] * jax.nn.sigmoid(z[..., D_MODEL:])


def _conv_mixer(h, w_pw1, b_pw1, w_dw, b_dw, ln_g, ln_b, w_pw2, b_pw2):
    z = h @ w_pw1 + b_pw1
    z = z[..., :D_MODEL] * jax.nn.sigmoid(z[..., D_MODEL:])
    y = lax.conv_general_dilated(
        z, w_dw[:, None, :].astype(z.dtype), window_strides=(1,),
        padding=((CONV_WIDTH - 1, 0),),
        dimension_numbers=('NWC', 'WIO', 'NWC'),
        feature_group_count=D_MODEL) + b_dw
    y = jax.nn.silu(_layernorm(y, ln_g, ln_b))
    return y @ w_pw2 + b_pw2


def _gmlp_mixer(h, w_in, ln_g, ln_b, w_s, b_s, w_out):
    bsz, s, _ = h.shape
    z = jax.nn.gelu(h @ w_in)
    u, v = z[..., :GMLP_WIDTH], z[..., GMLP_WIDTH:]
    v = _layernorm(v, ln_g, ln_b)
    n_chunks = s // GMLP_CHUNK
    v = v.reshape(bsz, n_chunks, GMLP_CHUNK, GMLP_HEADS, GMLP_WIDTH // GMLP_HEADS)
    causal = jnp.tril(jnp.ones((GMLP_CHUNK, GMLP_CHUNK), dtype=bool))
    ws = jnp.where(causal[None], w_s, 0.0)
    v = jnp.einsum('hts,bcshe->bcthe', ws, v) + b_s.T[None, None, :, :, None]
    v = v.reshape(bsz, s, GMLP_WIDTH)
    return (u * v) @ w_out


def _dilated_window_attention(q, k, v, window, dil):
    bsz, s, nh, hd = q.shape
    steps = window // dil
    blk = steps
    span = dil * blk
    s_pad = -(-s // span) * span
    pad = ((0, 0), (0, s_pad - s), (0, 0), (0, 0))
    nb = s_pad // span

    def split(t):
        t = jnp.pad(t, pad)
        return t.reshape(bsz, nb, blk, dil, nh, hd).transpose(0, 3, 1, 2, 4, 5)

    def with_prev(t):
        prev = jnp.pad(t, ((0, 0), (0, 0), (1, 0), (0, 0), (0, 0), (0, 0)))[:, :, :-1]
        return jnp.concatenate([prev, t], axis=3)

    qb = split(q)
    kk = with_prev(split(k))
    vv = with_prev(split(v))
    scores = jnp.einsum('brnihd,brnjhd->brnhij', qb, kk,
                        preferred_element_type=jnp.float32) * (HEAD_DIM ** -0.5)
    i_idx = jnp.arange(blk)[:, None]
    j_idx = jnp.arange(2 * blk)[None, :]
    dist = i_idx + blk - j_idx
    band = (dist >= 0) & (dist <= steps)
    has_prev = (jnp.arange(nb) > 0)[:, None, None]
    valid = band[None] & (has_prev | (j_idx >= blk)[None])
    scores = jnp.where(valid[None, None, :, None], scores, -jnp.inf)
    lse = jax.nn.logsumexp(scores, axis=-1)
    probs = jnp.exp(scores - lse[..., None])
    out = jnp.einsum('brnhij,brnjhd->brnihd', probs, vv.astype(jnp.float32))
    out = out.transpose(0, 2, 3, 1, 4, 5).reshape(bsz, s_pad, nh, hd)[:, :s]
    lse = lse.transpose(0, 2, 4, 1, 3).reshape(bsz, s_pad, nh)[:, :s]
    return out, lse


def _attention_mixer(h, w_qkv, w_o):
    bsz, s, _ = h.shape
    qkv = (h @ w_qkv).reshape(bsz, s, 3, ATT_GROUPS, ATT_HEADS, HEAD_DIM)
    outs, lses = [], []
    for g, (window, dil) in enumerate(ATT_CONFIGS):
        o, l = _dilated_window_attention(qkv[:, :, 0, g], qkv[:, :, 1, g], qkv[:, :, 2, g], window, dil)
        outs.append(o)
        lses.append(l)
    outs = jnp.stack(outs, axis=0)
    weights = jax.nn.softmax(jnp.stack(lses, axis=0), axis=0)
    merged = jnp.sum(weights[..., None] * outs, axis=0)
    return merged.reshape(bsz, s, ATT_GROUP_WIDTH).astype(h.dtype) @ w_o


def _mlp(h, w_in, w_out):
    return jnp.square(jax.nn.relu(h @ w_in)) @ w_out


def _n_layers_of(m):
    return len(range(m, DEPTH, N_MIXERS))


def _fwd_setup_inputs(seed: int = 0) -> dict:
    key = jax.random.key(seed)
    ks = jax.random.split(key, 32)
    nrm = jax.random.normal
    f32 = jnp.float32
    D, G, N, P = D_MODEL, SSM_GROUPS, SSM_STATE, SSM_GROUP
    nA, nB, nC, nD = (_n_layers_of(m) for m in range(N_MIXERS))
    E, T = GMLP_WIDTH, GMLP_CHUNK
    qkv_width = 3 * ATT_GROUPS * ATT_GROUP_WIDTH
    n_idx = jnp.arange(N, dtype=f32)
    return {
        "x": nrm(ks[0], (BATCH, SEQ, D), f32),
        "norm_mix": 1.0 + 0.02 * nrm(ks[1], (DEPTH, D), f32),
        "norm_mlp": 1.0 + 0.02 * nrm(ks[2], (DEPTH, D), f32),
        "norm_final": 1.0 + 0.02 * nrm(ks[3], (D,), f32),
        "ssm_a_re": -0.5 + 0.01 * nrm(ks[4], (nA, G, N), f32),
        "ssm_a_im": math.pi * n_idx + 0.01 * nrm(ks[5], (nA, G, N), f32),
        "ssm_b_re": nrm(ks[6], (nA, G, N, P), f32) * (2 * P) ** -0.5,
        "ssm_b_im": nrm(ks[7], (nA, G, N, P), f32) * (2 * P) ** -0.5,
        "ssm_c_re": nrm(ks[8], (nA, G, P, N), f32) * (2 * N) ** -0.5,
        "ssm_c_im": nrm(ks[9], (nA, G, P, N), f32) * (2 * N) ** -0.5,
        "ssm_d": nrm(ks[10], (nA, D), f32),
        "ssm_log_dt": jax.random.uniform(ks[11], (nA, G), f32, math.log(DT_MIN), math.log(DT_MAX)),
        "ssm_w_glu": nrm(ks[12], (nA, D, 2 * D), f32) * D ** -0.5,
        "conv_w_pw1": nrm(ks[13], (nB, D, 2 * D), f32) * D ** -0.5,
        "conv_b_pw1": 0.01 * nrm(ks[14], (nB, 2 * D), f32),
        "conv_w_dw": nrm(ks[15], (nB, CONV_WIDTH, D), f32) * CONV_WIDTH ** -0.5,
        "conv_b_dw": 0.01 * nrm(ks[16], (nB, D), f32),
        "conv_ln_g": 1.0 + 0.02 * nrm(ks[17], (nB, D), f32),
        "conv_ln_b": 0.01 * nrm(ks[18], (nB, D), f32),
        "conv_w_pw2": nrm(ks[19], (nB, D, D), f32) * D ** -0.5,
        "conv_b_pw2": 0.01 * nrm(ks[20], (nB, D), f32),
        "gmlp_w_in": nrm(ks[21], (nC, D, 2 * E), f32) * D ** -0.5,
        "gmlp_ln_g": 1.0 + 0.02 * nrm(ks[22], (nC, E), f32),
        "gmlp_ln_b": 0.01 * nrm(ks[23], (nC, E), f32),
        "gmlp_w_s": nrm(ks[24], (nC, GMLP_HEADS, T, T), f32) * T ** -0.5,
        "gmlp_b_s": 1.0 + 0.02 * nrm(ks[25], (nC, GMLP_HEADS, T), f32),
        "gmlp_w_out": nrm(ks[26], (nC, E, D), f32) * E ** -0.5,
        "attn_w_qkv": nrm(ks[27], (nD, D, qkv_width), f32) * D ** -0.5,
        "attn_w_o": nrm(ks[28], (nD, ATT_GROUP_WIDTH, D), f32) * ATT_GROUP_WIDTH ** -0.5,
        "mlp_w_in": nrm(ks[29], (DEPTH, D, D_FF), f32) * D ** -0.5,
        "mlp_w_out": nrm(ks[30], (DEPTH, D_FF, D), f32) * D_FF ** -0.5,
    }


def _fwd_reference(x, norm_mix, norm_mlp, norm_final,
              ssm_a_re, ssm_a_im, ssm_b_re, ssm_b_im, ssm_c_re, ssm_c_im, ssm_d, ssm_log_dt, ssm_w_glu,
              conv_w_pw1, conv_b_pw1, conv_w_dw, conv_b_dw, conv_ln_g, conv_ln_b, conv_w_pw2, conv_b_pw2,
              gmlp_w_in, gmlp_ln_g, gmlp_ln_b, gmlp_w_s, gmlp_b_s, gmlp_w_out,
              attn_w_qkv, attn_w_o, mlp_w_in, mlp_w_out):
    for i in range(DEPTH):
        m, j = i % N_MIXERS, i // N_MIXERS
        h = _rmsnorm(x, norm_mix[i])
        if m == 0:
            y = _s5_mixer(h, ssm_a_re[j], ssm_a_im[j], ssm_b_re[j], ssm_b_im[j], ssm_c_re[j],
                          ssm_c_im[j], ssm_d[j], ssm_log_dt[j], ssm_w_glu[j])
        elif m == 1:
            y = _conv_mixer(h, conv_w_pw1[j], conv_b_pw1[j], conv_w_dw[j], conv_b_dw[j],
                            conv_ln_g[j], conv_ln_b[j], conv_w_pw2[j], conv_b_pw2[j])
        elif m == 2:
            y = _gmlp_mixer(h, gmlp_w_in[j], gmlp_ln_g[j], gmlp_ln_b[j], gmlp_w_s[j],
                            gmlp_b_s[j], gmlp_w_out[j])
        else:
            y = _attention_mixer(h, attn_w_qkv[j], attn_w_o[j])
        x = x + y.astype(x.dtype)
        x = x + _mlp(_rmsnorm(x, norm_mlp[i]), mlp_w_in[i], mlp_w_out[i]).astype(x.dtype)
    return _rmsnorm(x, norm_final)


import jax as _jax
import jax.numpy as _jnp

TWIN_FORMAT = 'train_step'
FWD_PARAMS = ['x', 'norm_mix', 'norm_mlp', 'norm_final', 'ssm_a_re', 'ssm_a_im', 'ssm_b_re', 'ssm_b_im', 'ssm_c_re', 'ssm_c_im', 'ssm_d', 'ssm_log_dt', 'ssm_w_glu', 'conv_w_pw1', 'conv_b_pw1', 'conv_w_dw', 'conv_b_dw', 'conv_ln_g', 'conv_ln_b', 'conv_w_pw2', 'conv_b_pw2', 'gmlp_w_in', 'gmlp_ln_g', 'gmlp_ln_b', 'gmlp_w_s', 'gmlp_b_s', 'gmlp_w_out', 'attn_w_qkv', 'attn_w_o', 'mlp_w_in', 'mlp_w_out']
TWIN_WEIGHTS = ['norm_mix', 'norm_mlp', 'norm_final', 'ssm_a_re', 'ssm_a_im', 'ssm_b_re', 'ssm_b_im', 'ssm_c_re', 'ssm_c_im', 'ssm_d', 'ssm_log_dt', 'ssm_w_glu', 'conv_w_pw1', 'conv_b_pw1', 'conv_w_dw', 'conv_b_dw', 'conv_ln_g', 'conv_ln_b', 'conv_w_pw2', 'conv_b_pw2', 'gmlp_w_in', 'gmlp_ln_g', 'gmlp_ln_b', 'gmlp_w_s', 'gmlp_b_s', 'gmlp_w_out', 'attn_w_qkv', 'attn_w_o', 'mlp_w_in', 'mlp_w_out']
TWIN_DIFF_INPUT = 'x'
TWIN_INPUTS = ['x', 'norm_mix', 'norm_mlp', 'norm_final', 'ssm_a_re', 'ssm_a_im', 'ssm_b_re', 'ssm_b_im', 'ssm_c_re', 'ssm_c_im', 'ssm_d', 'ssm_log_dt', 'ssm_w_glu', 'conv_w_pw1', 'conv_b_pw1', 'conv_w_dw', 'conv_b_dw', 'conv_ln_g', 'conv_ln_b', 'conv_w_pw2', 'conv_b_pw2', 'gmlp_w_in', 'gmlp_ln_g', 'gmlp_ln_b', 'gmlp_w_s', 'gmlp_b_s', 'gmlp_w_out', 'attn_w_qkv', 'attn_w_o', 'mlp_w_in', 'mlp_w_out', 'loss_target', 'm_norm_mix', 'm_norm_mlp', 'm_norm_final', 'm_ssm_a_re', 'm_ssm_a_im', 'm_ssm_b_re', 'm_ssm_b_im', 'm_ssm_c_re', 'm_ssm_c_im', 'm_ssm_d', 'm_ssm_log_dt', 'm_ssm_w_glu', 'm_conv_w_pw1', 'm_conv_b_pw1', 'm_conv_w_dw', 'm_conv_b_dw', 'm_conv_ln_g', 'm_conv_ln_b', 'm_conv_w_pw2', 'm_conv_b_pw2', 'm_gmlp_w_in', 'm_gmlp_ln_g', 'm_gmlp_ln_b', 'm_gmlp_w_s', 'm_gmlp_b_s', 'm_gmlp_w_out', 'm_attn_w_qkv', 'm_attn_w_o', 'm_mlp_w_in', 'm_mlp_w_out', 'v_norm_mix', 'v_norm_mlp', 'v_norm_final', 'v_ssm_a_re', 'v_ssm_a_im', 'v_ssm_b_re', 'v_ssm_b_im', 'v_ssm_c_re', 'v_ssm_c_im', 'v_ssm_d', 'v_ssm_log_dt', 'v_ssm_w_glu', 'v_conv_w_pw1', 'v_conv_b_pw1', 'v_conv_w_dw', 'v_conv_b_dw', 'v_conv_ln_g', 'v_conv_ln_b', 'v_conv_w_pw2', 'v_conv_b_pw2', 'v_gmlp_w_in', 'v_gmlp_ln_g', 'v_gmlp_ln_b', 'v_gmlp_w_s', 'v_gmlp_b_s', 'v_gmlp_w_out', 'v_attn_w_qkv', 'v_attn_w_o', 'v_mlp_w_in', 'v_mlp_w_out']
TWIN_OUTPUTS = ['loss', 'grad_x', 'grad_norm_mix', 'grad_norm_mlp', 'grad_norm_final', 'grad_ssm_a_re', 'grad_ssm_a_im', 'grad_ssm_b_re', 'grad_ssm_b_im', 'grad_ssm_c_re', 'grad_ssm_c_im', 'grad_ssm_d', 'grad_ssm_log_dt', 'grad_ssm_w_glu', 'grad_conv_w_pw1', 'grad_conv_b_pw1', 'grad_conv_w_dw', 'grad_conv_b_dw', 'grad_conv_ln_g', 'grad_conv_ln_b', 'grad_conv_w_pw2', 'grad_conv_b_pw2', 'grad_gmlp_w_in', 'grad_gmlp_ln_g', 'grad_gmlp_ln_b', 'grad_gmlp_w_s', 'grad_gmlp_b_s', 'grad_gmlp_w_out', 'grad_attn_w_qkv', 'grad_attn_w_o', 'grad_mlp_w_in', 'grad_mlp_w_out', 'delta_norm_mix', 'delta_norm_mlp', 'delta_norm_final', 'delta_ssm_a_re', 'delta_ssm_a_im', 'delta_ssm_b_re', 'delta_ssm_b_im', 'delta_ssm_c_re', 'delta_ssm_c_im', 'delta_ssm_d', 'delta_ssm_log_dt', 'delta_ssm_w_glu', 'delta_conv_w_pw1', 'delta_conv_b_pw1', 'delta_conv_w_dw', 'delta_conv_b_dw', 'delta_conv_ln_g', 'delta_conv_ln_b', 'delta_conv_w_pw2', 'delta_conv_b_pw2', 'delta_gmlp_w_in', 'delta_gmlp_ln_g', 'delta_gmlp_ln_b', 'delta_gmlp_w_s', 'delta_gmlp_b_s', 'delta_gmlp_w_out', 'delta_attn_w_qkv', 'delta_attn_w_o', 'delta_mlp_w_in', 'delta_mlp_w_out', 'new_m_norm_mix', 'new_m_norm_mlp', 'new_m_norm_final', 'new_m_ssm_a_re', 'new_m_ssm_a_im', 'new_m_ssm_b_re', 'new_m_ssm_b_im', 'new_m_ssm_c_re', 'new_m_ssm_c_im', 'new_m_ssm_d', 'new_m_ssm_log_dt', 'new_m_ssm_w_glu', 'new_m_conv_w_pw1', 'new_m_conv_b_pw1', 'new_m_conv_w_dw', 'new_m_conv_b_dw', 'new_m_conv_ln_g', 'new_m_conv_ln_b', 'new_m_conv_w_pw2', 'new_m_conv_b_pw2', 'new_m_gmlp_w_in', 'new_m_gmlp_ln_g', 'new_m_gmlp_ln_b', 'new_m_gmlp_w_s', 'new_m_gmlp_b_s', 'new_m_gmlp_w_out', 'new_m_attn_w_qkv', 'new_m_attn_w_o', 'new_m_mlp_w_in', 'new_m_mlp_w_out', 'new_v_norm_mix', 'new_v_norm_mlp', 'new_v_norm_final', 'new_v_ssm_a_re', 'new_v_ssm_a_im', 'new_v_ssm_b_re', 'new_v_ssm_b_im', 'new_v_ssm_c_re', 'new_v_ssm_c_im', 'new_v_ssm_d', 'new_v_ssm_log_dt', 'new_v_ssm_w_glu', 'new_v_conv_w_pw1', 'new_v_conv_b_pw1', 'new_v_conv_w_dw', 'new_v_conv_b_dw', 'new_v_conv_ln_g', 'new_v_conv_ln_b', 'new_v_conv_w_pw2', 'new_v_conv_b_pw2', 'new_v_gmlp_w_in', 'new_v_gmlp_ln_g', 'new_v_gmlp_ln_b', 'new_v_gmlp_w_s', 'new_v_gmlp_b_s', 'new_v_gmlp_w_out', 'new_v_attn_w_qkv', 'new_v_attn_w_o', 'new_v_mlp_w_in', 'new_v_mlp_w_out']
TWIN_LEAF_KINDS = {'loss': 'loss', 'grad_x': 'grad_x', 'grad_norm_mix': 'grad_w', 'grad_norm_mlp': 'grad_w', 'grad_norm_final': 'grad_w', 'grad_ssm_a_re': 'grad_w', 'grad_ssm_a_im': 'grad_w', 'grad_ssm_b_re': 'grad_w', 'grad_ssm_b_im': 'grad_w', 'grad_ssm_c_re': 'grad_w', 'grad_ssm_c_im': 'grad_w', 'grad_ssm_d': 'grad_w', 'grad_ssm_log_dt': 'grad_w', 'grad_ssm_w_glu': 'grad_w', 'grad_conv_w_pw1': 'grad_w', 'grad_conv_b_pw1': 'grad_w', 'grad_conv_w_dw': 'grad_w', 'grad_conv_b_dw': 'grad_w', 'grad_conv_ln_g': 'grad_w', 'grad_conv_ln_b': 'grad_w', 'grad_conv_w_pw2': 'grad_w', 'grad_conv_b_pw2': 'grad_w', 'grad_gmlp_w_in': 'grad_w', 'grad_gmlp_ln_g': 'grad_w', 'grad_gmlp_ln_b': 'grad_w', 'grad_gmlp_w_s': 'grad_w', 'grad_gmlp_b_s': 'grad_w', 'grad_gmlp_w_out': 'grad_w', 'grad_attn_w_qkv': 'grad_w', 'grad_attn_w_o': 'grad_w', 'grad_mlp_w_in': 'grad_w', 'grad_mlp_w_out': 'grad_w', 'delta_norm_mix': 'delta_w', 'delta_norm_mlp': 'delta_w', 'delta_norm_final': 'delta_w', 'delta_ssm_a_re': 'delta_w', 'delta_ssm_a_im': 'delta_w', 'delta_ssm_b_re': 'delta_w', 'delta_ssm_b_im': 'delta_w', 'delta_ssm_c_re': 'delta_w', 'delta_ssm_c_im': 'delta_w', 'delta_ssm_d': 'delta_w', 'delta_ssm_log_dt': 'delta_w', 'delta_ssm_w_glu': 'delta_w', 'delta_conv_w_pw1': 'delta_w', 'delta_conv_b_pw1': 'delta_w', 'delta_conv_w_dw': 'delta_w', 'delta_conv_b_dw': 'delta_w', 'delta_conv_ln_g': 'delta_w', 'delta_conv_ln_b': 'delta_w', 'delta_conv_w_pw2': 'delta_w', 'delta_conv_b_pw2': 'delta_w', 'delta_gmlp_w_in': 'delta_w', 'delta_gmlp_ln_g': 'delta_w', 'delta_gmlp_ln_b': 'delta_w', 'delta_gmlp_w_s': 'delta_w', 'delta_gmlp_b_s': 'delta_w', 'delta_gmlp_w_out': 'delta_w', 'delta_attn_w_qkv': 'delta_w', 'delta_attn_w_o': 'delta_w', 'delta_mlp_w_in': 'delta_w', 'delta_mlp_w_out': 'delta_w', 'new_m_norm_mix': 'new_m', 'new_m_norm_mlp': 'new_m', 'new_m_norm_final': 'new_m', 'new_m_ssm_a_re': 'new_m', 'new_m_ssm_a_im': 'new_m', 'new_m_ssm_b_re': 'new_m', 'new_m_ssm_b_im': 'new_m', 'new_m_ssm_c_re': 'new_m', 'new_m_ssm_c_im': 'new_m', 'new_m_ssm_d': 'new_m', 'new_m_ssm_log_dt': 'new_m', 'new_m_ssm_w_glu': 'new_m', 'new_m_conv_w_pw1': 'new_m', 'new_m_conv_b_pw1': 'new_m', 'new_m_conv_w_dw': 'new_m', 'new_m_conv_b_dw': 'new_m', 'new_m_conv_ln_g': 'new_m', 'new_m_conv_ln_b': 'new_m', 'new_m_conv_w_pw2': 'new_m', 'new_m_conv_b_pw2': 'new_m', 'new_m_gmlp_w_in': 'new_m', 'new_m_gmlp_ln_g': 'new_m', 'new_m_gmlp_ln_b': 'new_m', 'new_m_gmlp_w_s': 'new_m', 'new_m_gmlp_b_s': 'new_m', 'new_m_gmlp_w_out': 'new_m', 'new_m_attn_w_qkv': 'new_m', 'new_m_attn_w_o': 'new_m', 'new_m_mlp_w_in': 'new_m', 'new_m_mlp_w_out': 'new_m', 'new_v_norm_mix': 'new_v', 'new_v_norm_mlp': 'new_v', 'new_v_norm_final': 'new_v', 'new_v_ssm_a_re': 'new_v', 'new_v_ssm_a_im': 'new_v', 'new_v_ssm_b_re': 'new_v', 'new_v_ssm_b_im': 'new_v', 'new_v_ssm_c_re': 'new_v', 'new_v_ssm_c_im': 'new_v', 'new_v_ssm_d': 'new_v', 'new_v_ssm_log_dt': 'new_v', 'new_v_ssm_w_glu': 'new_v', 'new_v_conv_w_pw1': 'new_v', 'new_v_conv_b_pw1': 'new_v', 'new_v_conv_w_dw': 'new_v', 'new_v_conv_b_dw': 'new_v', 'new_v_conv_ln_g': 'new_v', 'new_v_conv_ln_b': 'new_v', 'new_v_conv_w_pw2': 'new_v', 'new_v_conv_b_pw2': 'new_v', 'new_v_gmlp_w_in': 'new_v', 'new_v_gmlp_ln_g': 'new_v', 'new_v_gmlp_ln_b': 'new_v', 'new_v_gmlp_w_s': 'new_v', 'new_v_gmlp_b_s': 'new_v', 'new_v_gmlp_w_out': 'new_v', 'new_v_attn_w_qkv': 'new_v', 'new_v_attn_w_o': 'new_v', 'new_v_mlp_w_in': 'new_v', 'new_v_mlp_w_out': 'new_v'}


def _forward(args):
    return _fwd_reference(*[args[k] for k in FWD_PARAMS])


def _output_shape():
    out = _jax.eval_shape(lambda: _forward(_fwd_setup_inputs(0)))
    return out.shape, out.dtype

N_MICROBATCH = 1
ADAM_LR = 0.001
ADAM_B1 = 0.9
ADAM_B2 = 0.999
ADAM_EPS = 1e-08
ADAM_WD = 0.01
ADAM_STEP = 10
PER_EXAMPLE_BATCH_AXIS = {'x': 0, 'loss_target': 0}
SHARED_INPUTS = []
_WEIGHT_DTYPES = {'norm_mix': _jnp.float32, 'norm_mlp': _jnp.float32, 'norm_final': _jnp.float32, 'ssm_a_re': _jnp.float32, 'ssm_a_im': _jnp.float32, 'ssm_b_re': _jnp.float32, 'ssm_b_im': _jnp.float32, 'ssm_c_re': _jnp.float32, 'ssm_c_im': _jnp.float32, 'ssm_d': _jnp.float32, 'ssm_log_dt': _jnp.float32, 'ssm_w_glu': _jnp.float32, 'conv_w_pw1': _jnp.float32, 'conv_b_pw1': _jnp.float32, 'conv_w_dw': _jnp.float32, 'conv_b_dw': _jnp.float32, 'conv_ln_g': _jnp.float32, 'conv_ln_b': _jnp.float32, 'conv_w_pw2': _jnp.float32, 'conv_b_pw2': _jnp.float32, 'gmlp_w_in': _jnp.float32, 'gmlp_ln_g': _jnp.float32, 'gmlp_ln_b': _jnp.float32, 'gmlp_w_s': _jnp.float32, 'gmlp_b_s': _jnp.float32, 'gmlp_w_out': _jnp.float32, 'attn_w_qkv': _jnp.float32, 'attn_w_o': _jnp.float32, 'mlp_w_in': _jnp.float32, 'mlp_w_out': _jnp.float32}
MOMENT_SCALE = {'norm_mix': 1.272906e-01, 'norm_mlp': 1.852125e-01, 'norm_final': 6.605171e+01, 'ssm_a_re': 7.468558e-03, 'ssm_a_im': 7.564631e-03, 'ssm_b_re': 3.958143e-03, 'ssm_b_im': 3.930943e-03, 'ssm_c_re': 7.789463e-03, 'ssm_c_im': 7.672637e-03, 'ssm_d': 1.306292e-01, 'ssm_log_dt': 5.072862e+00, 'ssm_w_glu': 8.537206e-02, 'conv_w_pw1': 8.365691e-02, 'conv_b_pw1': 1.152837e-01, 'conv_w_dw': 1.120682e-01, 'conv_b_dw': 2.459268e-01, 'conv_ln_g': 1.453633e-01, 'conv_ln_b': 1.615369e-01, 'conv_w_pw2': 1.135008e-01, 'conv_b_pw2': 2.788087e-01, 'gmlp_w_in': 8.686374e-02, 'gmlp_ln_g': 5.765163e-02, 'gmlp_ln_b': 5.677108e-02, 'gmlp_w_s': 7.744856e-02, 'gmlp_b_s': 1.165478e-01, 'gmlp_w_out': 1.189684e-01, 'attn_w_qkv': 2.422647e-02, 'attn_w_o': 4.524828e-02, 'mlp_w_in': 9.538579e-02, 'mlp_w_out': 1.927644e-01}


def _to_microbatches(a, axis):
    t = _jnp.moveaxis(a, axis, 0)
    t = t.reshape((N_MICROBATCH, t.shape[0] // N_MICROBATCH) + t.shape[1:])
    return _jnp.moveaxis(t, 1, axis + 1)


def setup_inputs(seed: int = 0) -> dict:
    inp = _fwd_setup_inputs(seed)
    key = _jax.random.fold_in(_jax.random.key(seed), 7919)
    shape, _ = _output_shape()
    out = dict(inp)
    out["loss_target"] = _jax.random.normal(_jax.random.fold_in(key, 0), shape, _jnp.float32)
    for i, name in enumerate(TWIN_WEIGHTS):
        w = inp[name].astype(_jnp.float32)
        if MOMENT_SCALE is None:
            s = _jnp.sqrt(_jnp.mean(_jnp.square(w)) + 1e-30)
        else:
            s = MOMENT_SCALE[name]
        km, kv = _jax.random.split(_jax.random.fold_in(key, i + 1))
        out[name] = w
        out["m_" + name] = s * _jax.random.normal(km, w.shape, _jnp.float32)
        out["v_" + name] = (s * s) * _jax.random.uniform(kv, w.shape, _jnp.float32, 0.5, 1.5)
    if N_MICROBATCH > 1:
        for name, axis in PER_EXAMPLE_BATCH_AXIS.items():
            out[name] = _to_microbatches(out[name], axis)
    return {'x': out['x'], 'norm_mix': out['norm_mix'], 'norm_mlp': out['norm_mlp'], 'norm_final': out['norm_final'], 'ssm_a_re': out['ssm_a_re'], 'ssm_a_im': out['ssm_a_im'], 'ssm_b_re': out['ssm_b_re'], 'ssm_b_im': out['ssm_b_im'], 'ssm_c_re': out['ssm_c_re'], 'ssm_c_im': out['ssm_c_im'], 'ssm_d': out['ssm_d'], 'ssm_log_dt': out['ssm_log_dt'], 'ssm_w_glu': out['ssm_w_glu'], 'conv_w_pw1': out['conv_w_pw1'], 'conv_b_pw1': out['conv_b_pw1'], 'conv_w_dw': out['conv_w_dw'], 'conv_b_dw': out['conv_b_dw'], 'conv_ln_g': out['conv_ln_g'], 'conv_ln_b': out['conv_ln_b'], 'conv_w_pw2': out['conv_w_pw2'], 'conv_b_pw2': out['conv_b_pw2'], 'gmlp_w_in': out['gmlp_w_in'], 'gmlp_ln_g': out['gmlp_ln_g'], 'gmlp_ln_b': out['gmlp_ln_b'], 'gmlp_w_s': out['gmlp_w_s'], 'gmlp_b_s': out['gmlp_b_s'], 'gmlp_w_out': out['gmlp_w_out'], 'attn_w_qkv': out['attn_w_qkv'], 'attn_w_o': out['attn_w_o'], 'mlp_w_in': out['mlp_w_in'], 'mlp_w_out': out['mlp_w_out'], 'loss_target': out['loss_target'], 'm_norm_mix': out['m_norm_mix'], 'm_norm_mlp': out['m_norm_mlp'], 'm_norm_final': out['m_norm_final'], 'm_ssm_a_re': out['m_ssm_a_re'], 'm_ssm_a_im': out['m_ssm_a_im'], 'm_ssm_b_re': out['m_ssm_b_re'], 'm_ssm_b_im': out['m_ssm_b_im'], 'm_ssm_c_re': out['m_ssm_c_re'], 'm_ssm_c_im': out['m_ssm_c_im'], 'm_ssm_d': out['m_ssm_d'], 'm_ssm_log_dt': out['m_ssm_log_dt'], 'm_ssm_w_glu': out['m_ssm_w_glu'], 'm_conv_w_pw1': out['m_conv_w_pw1'], 'm_conv_b_pw1': out['m_conv_b_pw1'], 'm_conv_w_dw': out['m_conv_w_dw'], 'm_conv_b_dw': out['m_conv_b_dw'], 'm_conv_ln_g': out['m_conv_ln_g'], 'm_conv_ln_b': out['m_conv_ln_b'], 'm_conv_w_pw2': out['m_conv_w_pw2'], 'm_conv_b_pw2': out['m_conv_b_pw2'], 'm_gmlp_w_in': out['m_gmlp_w_in'], 'm_gmlp_ln_g': out['m_gmlp_ln_g'], 'm_gmlp_ln_b': out['m_gmlp_ln_b'], 'm_gmlp_w_s': out['m_gmlp_w_s'], 'm_gmlp_b_s': out['m_gmlp_b_s'], 'm_gmlp_w_out': out['m_gmlp_w_out'], 'm_attn_w_qkv': out['m_attn_w_qkv'], 'm_attn_w_o': out['m_attn_w_o'], 'm_mlp_w_in': out['m_mlp_w_in'], 'm_mlp_w_out': out['m_mlp_w_out'], 'v_norm_mix': out['v_norm_mix'], 'v_norm_mlp': out['v_norm_mlp'], 'v_norm_final': out['v_norm_final'], 'v_ssm_a_re': out['v_ssm_a_re'], 'v_ssm_a_im': out['v_ssm_a_im'], 'v_ssm_b_re': out['v_ssm_b_re'], 'v_ssm_b_im': out['v_ssm_b_im'], 'v_ssm_c_re': out['v_ssm_c_re'], 'v_ssm_c_im': out['v_ssm_c_im'], 'v_ssm_d': out['v_ssm_d'], 'v_ssm_log_dt': out['v_ssm_log_dt'], 'v_ssm_w_glu': out['v_ssm_w_glu'], 'v_conv_w_pw1': out['v_conv_w_pw1'], 'v_conv_b_pw1': out['v_conv_b_pw1'], 'v_conv_w_dw': out['v_conv_w_dw'], 'v_conv_b_dw': out['v_conv_b_dw'], 'v_conv_ln_g': out['v_conv_ln_g'], 'v_conv_ln_b': out['v_conv_ln_b'], 'v_conv_w_pw2': out['v_conv_w_pw2'], 'v_conv_b_pw2': out['v_conv_b_pw2'], 'v_gmlp_w_in': out['v_gmlp_w_in'], 'v_gmlp_ln_g': out['v_gmlp_ln_g'], 'v_gmlp_ln_b': out['v_gmlp_ln_b'], 'v_gmlp_w_s': out['v_gmlp_w_s'], 'v_gmlp_b_s': out['v_gmlp_b_s'], 'v_gmlp_w_out': out['v_gmlp_w_out'], 'v_attn_w_qkv': out['v_attn_w_qkv'], 'v_attn_w_o': out['v_attn_w_o'], 'v_mlp_w_in': out['v_mlp_w_in'], 'v_mlp_w_out': out['v_mlp_w_out']}


def _loss(weights, diff, rest, loss_target):
    with _jax.named_scope("forward"):
        args = {**rest, TWIN_DIFF_INPUT: diff, **{k: w.astype(_WEIGHT_DTYPES[k]) for k, w in weights.items()}}
        y = _forward(args)
    with _jax.named_scope("loss_head"):
        err = _jnp.square(y.astype(_jnp.float32) - loss_target)
        return 0.5 * _jnp.sum(_jnp.mean(err, axis=-1)) if err.ndim else 0.5 * err


def _adamw(w, g, m, v):
    m = ADAM_B1 * m + (1.0 - ADAM_B1) * g
    v = ADAM_B2 * v + (1.0 - ADAM_B2) * _jnp.square(g)
    m_hat = m / (1.0 - ADAM_B1 ** ADAM_STEP)
    v_hat = v / (1.0 - ADAM_B2 ** ADAM_STEP)
    delta = -ADAM_LR * (m_hat / (_jnp.sqrt(v_hat) + ADAM_EPS) + ADAM_WD * w)
    return delta, m, v


def reference(x, norm_mix, norm_mlp, norm_final, ssm_a_re, ssm_a_im, ssm_b_re, ssm_b_im, ssm_c_re, ssm_c_im, ssm_d, ssm_log_dt, ssm_w_glu, conv_w_pw1, conv_b_pw1, conv_w_dw, conv_b_dw, conv_ln_g, conv_ln_b, conv_w_pw2, conv_b_pw2, gmlp_w_in, gmlp_ln_g, gmlp_ln_b, gmlp_w_s, gmlp_b_s, gmlp_w_out, attn_w_qkv, attn_w_o, mlp_w_in, mlp_w_out, loss_target, m_norm_mix, m_norm_mlp, m_norm_final, m_ssm_a_re, m_ssm_a_im, m_ssm_b_re, m_ssm_b_im, m_ssm_c_re, m_ssm_c_im, m_ssm_d, m_ssm_log_dt, m_ssm_w_glu, m_conv_w_pw1, m_conv_b_pw1, m_conv_w_dw, m_conv_b_dw, m_conv_ln_g, m_conv_ln_b, m_conv_w_pw2, m_conv_b_pw2, m_gmlp_w_in, m_gmlp_ln_g, m_gmlp_ln_b, m_gmlp_w_s, m_gmlp_b_s, m_gmlp_w_out, m_attn_w_qkv, m_attn_w_o, m_mlp_w_in, m_mlp_w_out, v_norm_mix, v_norm_mlp, v_norm_final, v_ssm_a_re, v_ssm_a_im, v_ssm_b_re, v_ssm_b_im, v_ssm_c_re, v_ssm_c_im, v_ssm_d, v_ssm_log_dt, v_ssm_w_glu, v_conv_w_pw1, v_conv_b_pw1, v_conv_w_dw, v_conv_b_dw, v_conv_ln_g, v_conv_ln_b, v_conv_w_pw2, v_conv_b_pw2, v_gmlp_w_in, v_gmlp_ln_g, v_gmlp_ln_b, v_gmlp_w_s, v_gmlp_b_s, v_gmlp_w_out, v_attn_w_qkv, v_attn_w_o, v_mlp_w_in, v_mlp_w_out):
    given = dict(x=x, norm_mix=norm_mix, norm_mlp=norm_mlp, norm_final=norm_final, ssm_a_re=ssm_a_re, ssm_a_im=ssm_a_im, ssm_b_re=ssm_b_re, ssm_b_im=ssm_b_im, ssm_c_re=ssm_c_re, ssm_c_im=ssm_c_im, ssm_d=ssm_d, ssm_log_dt=ssm_log_dt, ssm_w_glu=ssm_w_glu, conv_w_pw1=conv_w_pw1, conv_b_pw1=conv_b_pw1, conv_w_dw=conv_w_dw, conv_b_dw=conv_b_dw, conv_ln_g=conv_ln_g, conv_ln_b=conv_ln_b, conv_w_pw2=conv_w_pw2, conv_b_pw2=conv_b_pw2, gmlp_w_in=gmlp_w_in, gmlp_ln_g=gmlp_ln_g, gmlp_ln_b=gmlp_ln_b, gmlp_w_s=gmlp_w_s, gmlp_b_s=gmlp_b_s, gmlp_w_out=gmlp_w_out, attn_w_qkv=attn_w_qkv, attn_w_o=attn_w_o, mlp_w_in=mlp_w_in, mlp_w_out=mlp_w_out, loss_target=loss_target, m_norm_mix=m_norm_mix, m_norm_mlp=m_norm_mlp, m_norm_final=m_norm_final, m_ssm_a_re=m_ssm_a_re, m_ssm_a_im=m_ssm_a_im, m_ssm_b_re=m_ssm_b_re, m_ssm_b_im=m_ssm_b_im, m_ssm_c_re=m_ssm_c_re, m_ssm_c_im=m_ssm_c_im, m_ssm_d=m_ssm_d, m_ssm_log_dt=m_ssm_log_dt, m_ssm_w_glu=m_ssm_w_glu, m_conv_w_pw1=m_conv_w_pw1, m_conv_b_pw1=m_conv_b_pw1, m_conv_w_dw=m_conv_w_dw, m_conv_b_dw=m_conv_b_dw, m_conv_ln_g=m_conv_ln_g, m_conv_ln_b=m_conv_ln_b, m_conv_w_pw2=m_conv_w_pw2, m_conv_b_pw2=m_conv_b_pw2, m_gmlp_w_in=m_gmlp_w_in, m_gmlp_ln_g=m_gmlp_ln_g, m_gmlp_ln_b=m_gmlp_ln_b, m_gmlp_w_s=m_gmlp_w_s, m_gmlp_b_s=m_gmlp_b_s, m_gmlp_w_out=m_gmlp_w_out, m_attn_w_qkv=m_attn_w_qkv, m_attn_w_o=m_attn_w_o, m_mlp_w_in=m_mlp_w_in, m_mlp_w_out=m_mlp_w_out, v_norm_mix=v_norm_mix, v_norm_mlp=v_norm_mlp, v_norm_final=v_norm_final, v_ssm_a_re=v_ssm_a_re, v_ssm_a_im=v_ssm_a_im, v_ssm_b_re=v_ssm_b_re, v_ssm_b_im=v_ssm_b_im, v_ssm_c_re=v_ssm_c_re, v_ssm_c_im=v_ssm_c_im, v_ssm_d=v_ssm_d, v_ssm_log_dt=v_ssm_log_dt, v_ssm_w_glu=v_ssm_w_glu, v_conv_w_pw1=v_conv_w_pw1, v_conv_b_pw1=v_conv_b_pw1, v_conv_w_dw=v_conv_w_dw, v_conv_b_dw=v_conv_b_dw, v_conv_ln_g=v_conv_ln_g, v_conv_ln_b=v_conv_ln_b, v_conv_w_pw2=v_conv_w_pw2, v_conv_b_pw2=v_conv_b_pw2, v_gmlp_w_in=v_gmlp_w_in, v_gmlp_ln_g=v_gmlp_ln_g, v_gmlp_ln_b=v_gmlp_ln_b, v_gmlp_w_s=v_gmlp_w_s, v_gmlp_b_s=v_gmlp_b_s, v_gmlp_w_out=v_gmlp_w_out, v_attn_w_qkv=v_attn_w_qkv, v_attn_w_o=v_attn_w_o, v_mlp_w_in=v_mlp_w_in, v_mlp_w_out=v_mlp_w_out)
    weights = {n: given[n] for n in TWIN_WEIGHTS}
    shared = {n: given[n] for n in SHARED_INPUTS}
    per_example = {n: given[n] for n in ['x']}
    grad_fn = _jax.value_and_grad(_loss, argnums=(0, 1))

    def one_microbatch(ex, loss_target):
        ex = dict(ex)
        diff = ex.pop(TWIN_DIFF_INPUT)
        return grad_fn(weights, diff, {**shared, **ex}, loss_target)

    if N_MICROBATCH == 1:
        loss, (grad_w, grad_x) = one_microbatch(per_example, given["loss_target"])
    else:
        def body(carry, xs):
            loss_sum, grad_sum = carry
            l_k, (gw_k, gx_k) = one_microbatch(xs[0], xs[1])
            with _jax.named_scope("update"):
                return (loss_sum + l_k, _jax.tree.map(_jnp.add, grad_sum, gw_k)), gx_k

        init = (_jnp.zeros((), _jnp.float32), _jax.tree.map(_jnp.zeros_like, weights))
        (loss, grad_w), grad_x = _jax.lax.scan(body, init, (per_example, given["loss_target"]))
    with _jax.named_scope("update"):
        delta_w, new_m, new_v = {}, {}, {}
        for n in TWIN_WEIGHTS:
            delta_w[n], new_m[n], new_v[n] = _adamw(weights[n], grad_w[n], given["m_" + n], given["v_" + n])
    return (loss, grad_x, *[grad_w[n] for n in TWIN_WEIGHTS], *[delta_w[n] for n in TWIN_WEIGHTS],
            *[new_m[n] for n in TWIN_WEIGHTS], *[new_v[n] for n in TWIN_WEIGHTS])
```

```python
import functools
import math

import jax
import jax.numpy as jnp
from jax import lax
from jax.experimental import pallas as pl
from jax.experimental.pallas import tpu as pltpu

F32 = jnp.float32
BF16 = jnp.bfloat16

D_MODEL = 1024
DEPTH = 4
EPS = 1e-6
SSM_GROUP = 16
SSM_GROUPS = 64
SSM_STATE = 64
S5_GB = 8
S5_NGB = SSM_GROUPS // S5_GB
S5_CH = S5_GB * SSM_GROUP
S5_ST = S5_GB * SSM_STATE
S5_L = 128
CONV_WIDTH = 31
CONV_PAD = 32
CONV_TS = 256
CONV_CW = 256
GMLP_CHUNK = 128
GMLP_HEADS = 4
ATT_CONFIGS = ((128, 1), (512, 4), (2048, 16))
ATT_HEADS = 8
HEAD_DIM = 64
ATT_BLK = 128
ATT_TB = 8
ATT_W = ATT_HEADS * HEAD_DIM
N_DEV = 8
ADAM_LR = 0.001
ADAM_B1 = 0.9
ADAM_B2 = 0.999
ADAM_EPS = 1e-08
ADAM_WD = 0.01
ADAM_STEP = 10
VMEM_LIMIT = 56 * 1024 * 1024
PACK_C = 1024
MESH_AXES = ("x", "y", "c")
MESH = pl.DeviceIdType.MESH

WEIGHT_NAMES = ['norm_mix', 'norm_mlp', 'norm_final', 'ssm_a_re', 'ssm_a_im', 'ssm_b_re', 'ssm_b_im',
                'ssm_c_re', 'ssm_c_im', 'ssm_d', 'ssm_log_dt', 'ssm_w_glu', 'conv_w_pw1', 'conv_b_pw1',
                'conv_w_dw', 'conv_b_dw', 'conv_ln_g', 'conv_ln_b', 'conv_w_pw2', 'conv_b_pw2',
                'gmlp_w_in', 'gmlp_ln_g', 'gmlp_ln_b', 'gmlp_w_s', 'gmlp_b_s', 'gmlp_w_out',
                'attn_w_qkv', 'attn_w_o', 'mlp_w_in', 'mlp_w_out']
BIG = {'ssm_w_glu': 2, 'conv_w_pw1': 2, 'conv_w_pw2': 1, 'gmlp_w_in': 2, 'gmlp_w_out': 1,
       'attn_w_qkv': 2, 'attn_w_o': 2, 'mlp_w_in': 2, 'mlp_w_out': 1}
SMALL_SHARDED = {'conv_b_pw1': 1, 'conv_w_dw': 2, 'conv_b_dw': 1, 'conv_ln_g': 1, 'conv_ln_b': 1,
                 'conv_b_pw2': 1, 'gmlp_ln_g': 1, 'gmlp_ln_b': 1}
SMALL = [n for n in WEIGHT_NAMES if n not in BIG]


def _cparams(sem=None):
    return pltpu.CompilerParams(dimension_semantics=sem, vmem_limit_bytes=VMEM_LIMIT)


def _dot(a, b):
    return jnp.dot(a, b, preferred_element_type=F32)


def _dot_nt(a, b):
    return lax.dot_general(a, b, (((1,), (1,)), ((), ())), preferred_element_type=F32)


def _dot_tn(a, b):
    return lax.dot_general(a, b, (((0,), (0,)), ((), ())), preferred_element_type=F32)


def rowwise(name, fn, rows, params, row_out, acc_out=(), tr=256):
    T = rows[0].shape[0]
    tr = min(tr, T)
    while T % tr:
        tr //= 2
    assert tr % 8 == 0
    nr, npar, nro = len(rows), len(params), len(row_out)

    def body(*refs):
        ins = [r[...] for r in refs[:nr + npar]]
        outs = refs[nr + npar:]
        res = fn(*ins)
        if not isinstance(res, (tuple, list)):
            res = (res,)
        for k in range(nro):
            outs[k][...] = res[k].astype(outs[k].dtype)
        if acc_out:
            @pl.when(pl.program_id(0) == 0)
            def _():
                for k in range(nro, len(outs)):
                    outs[k][...] = jnp.zeros_like(outs[k])
            for k in range(nro, len(outs)):
                outs[k][...] += res[k].astype(outs[k].dtype)

    in_specs = [pl.BlockSpec((tr, r.shape[1]), lambda i: (i, 0)) for r in rows]
    in_specs += [pl.BlockSpec(p.shape, lambda i, nd=p.ndim: (0,) * nd) for p in params]
    out_shape = [jax.ShapeDtypeStruct((T, c), dt) for c, dt in row_out]
    out_specs = [pl.BlockSpec((tr, c), lambda i: (i, 0)) for c, dt in row_out]
    out_shape += [jax.ShapeDtypeStruct(s, dt) for s, dt in acc_out]
    out_specs += [pl.BlockSpec(s, lambda i, nd=len(s): (0,) * nd) for s, dt in acc_out]
    res = pl.pallas_call(body, grid=(T // tr,), in_specs=in_specs, out_specs=out_specs, out_shape=out_shape,
                         name=name, compiler_params=_cparams(("arbitrary",)))(*rows, *params)
    return res


def matmul(name, a, b, *, nt=False, epi=None, extras=(), out_dtypes=(F32,), out_cols=None, tn=512):
    M, K = a.shape
    N = b.shape[0] if nt else b.shape[1]
    tm = 1024
    while tm > 256 and tm * K * 2 > (4 << 20):
        tm //= 2
    tm = min(tm, M)
    tn = min(tn, N)
    assert M % tm == 0 and N % tn == 0, (M, N, tm, tn)
    nex = len(extras)

    def body(a_ref, b_ref, *rest):
        ex = [e[...] for e in rest[:nex]]
        outs = rest[nex:]
        acc = _dot_nt(a_ref[...], b_ref[...]) if nt else _dot(a_ref[...], b_ref[...])
        res = epi(acc, *ex) if epi is not None else (acc,)
        for o, r in zip(outs, res):
            o[...] = r.astype(o.dtype)

    in_specs = [pl.BlockSpec((tm, K), lambda j, i: (i, 0)),
                pl.BlockSpec((tn, K), lambda j, i: (j, 0)) if nt else pl.BlockSpec((K, tn), lambda j, i: (0, j))]
    for arr, kind in extras:
        if kind == 'tile':
            in_specs.append(pl.BlockSpec((tm, tn), lambda j, i: (i, j)))
        else:
            in_specs.append(pl.BlockSpec((1, tn), lambda j, i: (0, j)))
    out_shape = [jax.ShapeDtypeStruct((M, N), dt) for dt in out_dtypes]
    out_specs = [pl.BlockSpec((tm, tn), lambda j, i: (i, j)) for dt in out_dtypes]
    res = pl.pallas_call(body, grid=(N // tn, M // tm), in_specs=in_specs, out_specs=out_specs,
                         out_shape=out_shape, name=name,
                         compiler_params=_cparams(("arbitrary", "arbitrary")))(a, b, *[e[0] for e in extras])
    return res


def wgrad(name, a, g):
    M, K = a.shape
    N = g.shape[1]
    tk, tn, tm = min(K, 1024), min(N, 512), min(M, 1024)
    assert K % tk == 0 and N % tn == 0 and M % tm == 0

    def body(a_ref, g_ref, o_ref):
        @pl.when(pl.program_id(2) == 0)
        def _():
            o_ref[...] = jnp.zeros_like(o_ref)
        o_ref[...] += _dot_tn(a_ref[...], g_ref[...])

    return pl.pallas_call(
        body, grid=(K // tk, N // tn, M // tm),
        in_specs=[pl.BlockSpec((tm, tk), lambda k, n, m: (m, k)), pl.BlockSpec((tm, tn), lambda k, n, m: (m, n))],
        out_specs=pl.BlockSpec((tk, tn), lambda k, n, m: (k, n)),
        out_shape=jax.ShapeDtypeStruct((K, N), F32), name=name,
        compiler_params=_cparams(("arbitrary", "arbitrary", "arbitrary")))(a, g)


def _rms(x, g):
    x = x.astype(F32)
    return x * lax.rsqrt(jnp.mean(x * x, axis=-1, keepdims=True) + EPS) * g


def _ln(x, g, b):
    mu = jnp.mean(x, axis=-1, keepdims=True)
    var = jnp.mean(jnp.square(x - mu), axis=-1, keepdims=True)
    return (x - mu) * lax.rsqrt(var + EPS) * g + b


def _glu(z):
    d = z.shape[1] // 2
    return z[:, :d] * jax.nn.sigmoid(z[:, d:])


def _glu_bwd(z, dy):
    d = z.shape[1] // 2
    a, s = z[:, :d], jax.nn.sigmoid(z[:, d:])
    return jnp.concatenate([dy * s, dy * a * s * (1.0 - s)], axis=1)


def _colsum(v):
    return jnp.sum(v.astype(F32), axis=0, keepdims=True)


def rms_fwd(name, x, g, want_f32=False):
    def fn(xt, gt):
        h = _rms(xt, gt)
        return (h, h) if want_f32 else (h,)
    D = x.shape[1]
    outs = [(D, BF16)] + ([(D, F32)] if want_f32 else [])
    return rowwise(name, fn, [x], [g], outs)


def rms_bwd(name, x, dh, dres, g):
    def fn(xt, dht, drt, gt):
        _, vjp = jax.vjp(_rms, xt, gt)
        dx, dg = vjp(dht.astype(F32))
        dx = dx + drt
        return dx, dx, dg
    D = x.shape[1]
    return rowwise(name, fn, [x, dh, dres], [g], [(D, F32), (D, BF16)], [((1, D), F32)])


def s5_disc(a_re, a_im, log_dt, b_re, b_im):
    dt = jnp.exp(log_dt)[:, None]
    er = jnp.exp(a_re * dt)
    lam_re = er * jnp.cos(a_im * dt)
    lam_im = er * jnp.sin(a_im * dt)
    nr, ni = lam_re - 1.0, lam_im
    den = a_re * a_re + a_im * a_im
    f_re = (nr * a_re + ni * a_im) / den
    f_im = (ni * a_re - nr * a_im) / den
    bb_re = f_re[..., None] * b_re - f_im[..., None] * b_im
    bb_im = f_re[..., None] * b_im + f_im[..., None] * b_re
    return lam_re, lam_im, bb_re, bb_im


def _s5_blockdiag_b(bb):
    t = bb.reshape(S5_NGB, S5_GB, SSM_STATE, SSM_GROUP).transpose(0, 1, 3, 2)
    eye = jnp.eye(S5_GB, dtype=bb.dtype)
    return jnp.einsum('bgpn,gh->bgphn', t, eye).reshape(S5_NGB, S5_CH, S5_ST)


def _s5_blockdiag_b_inv(x):
    t = x.reshape(S5_NGB, S5_GB, SSM_GROUP, S5_GB, SSM_STATE)
    eye = jnp.eye(S5_GB, dtype=x.dtype)
    d = jnp.einsum('bgphn,gh->bgpn', t, eye)
    return d.transpose(0, 1, 3, 2).reshape(SSM_GROUPS, SSM_STATE, SSM_GROUP)


def _s5_blockdiag_c(c):
    t = c.reshape(S5_NGB, S5_GB, SSM_GROUP, SSM_STATE).transpose(0, 1, 3, 2)
    eye = jnp.eye(S5_GB, dtype=c.dtype)
    return jnp.einsum('bgnp,gh->bgnhp', t, eye).reshape(S5_NGB, S5_ST, S5_CH)


def _s5_blockdiag_c_inv(x):
    t = x.reshape(S5_NGB, S5_GB, SSM_STATE, S5_GB, SSM_GROUP)
    eye = jnp.eye(S5_GB, dtype=x.dtype)
    d = jnp.einsum('bgnhp,gh->bgnp', t, eye)
    return d.transpose(0, 1, 3, 2).reshape(SSM_GROUPS, SSM_GROUP, SSM_STATE)


def s5_tables(lam_re, lam_im, L):
    pr, pi = lam_re.reshape(1, -1), lam_im.reshape(1, -1)
    n = 1
    while n < L:
        lr, li = pr[n - 1:n], pi[n - 1:n]
        pr, pi = (jnp.concatenate([pr, pr * lr - pi * li], 0), jnp.concatenate([pi, pr * li + pi * lr], 0))
        n *= 2
    nk = int(math.log2(L))
    idx = [2 ** k - 1 for k in range(nk)] + [0] * (8 - nk)

    def blk(t):
        return t.reshape(t.shape[0], S5_NGB, S5_ST).transpose(1, 0, 2)

    def rows(t):
        return jnp.concatenate([t[j:j + 1] for j in idx], axis=0)
    return blk(pr), blk(pi), blk(rows(pr)), blk(rows(pi))


def _cscan(br, bi, a2r, a2i, L, reverse):
    row = lax.broadcasted_iota(jnp.int32, br.shape, 0)
    xr, xi = br, bi
    for k in range(int(math.log2(L))):
        s = 1 << k
        ar, ai = a2r[k:k + 1, :], a2i[k:k + 1, :]
        if reverse:
            sr, si = pltpu.roll(xr, L - s, 0), pltpu.roll(xi, L - s, 0)
            m = row < L - s
        else:
            sr, si = pltpu.roll(xr, s, 0), pltpu.roll(xi, s, 0)
            m = row >= s
        sr, si = jnp.where(m, sr, 0.0), jnp.where(m, si, 0.0)
        xr, xi = xr + ar * sr - ai * si, xi + ar * si + ai * sr
    return xr, xi


def s5_fwd(h, bre, bim, cre, cim, pwr, pwi, l2r, l2i, dskip, bsz):
    T, D = h.shape
    L = S5_L
    S = T // bsz
    NC = S // L

    def body(h_ref, bre_ref, bim_ref, cre_ref, cim_ref, pwr_ref, pwi_ref, l2r_ref, l2i_ref, d_ref,
             y_ref, gy_ref, xs_ref, car_r, car_i):
        @pl.when(pl.program_id(2) == 0)
        def _():
            car_r[...] = jnp.zeros_like(car_r)
            car_i[...] = jnp.zeros_like(car_i)
        u = h_ref[...]
        ub = u.astype(BF16)
        cr, ci = car_r[0:1, :], car_i[0:1, :]
        xs_ref[...] = jnp.zeros_like(xs_ref)
        xs_ref[0:1, :] = cr
        xs_ref[1:2, :] = ci
        xr, xi = _cscan(_dot(ub, bre_ref[...]), _dot(ub, bim_ref[...]), l2r_ref[...], l2i_ref[...], L, False)
        pr, pi = pwr_ref[...], pwi_ref[...]
        xr = xr + pr * cr - pi * ci
        xi = xi + pr * ci + pi * cr
        car_r[...] = jnp.broadcast_to(xr[L - 1:L, :], car_r.shape)
        car_i[...] = jnp.broadcast_to(xi[L - 1:L, :], car_i.shape)
        y = _dot(xr.astype(BF16), cre_ref[...]) - _dot(xi.astype(BF16), cim_ref[...]) + d_ref[...] * u
        y_ref[...] = y
        gy_ref[...] = jax.nn.gelu(y).astype(BF16)

    tok = lambda g, b, c: (b * NC + c, g)
    par = lambda g, b, c: (g, 0, 0)
    return pl.pallas_call(
        body, grid=(S5_NGB, bsz, NC),
        in_specs=[pl.BlockSpec((L, S5_CH), tok),
                  pl.BlockSpec((None, S5_CH, S5_ST), par), pl.BlockSpec((None, S5_CH, S5_ST), par),
                  pl.BlockSpec((None, S5_ST, S5_CH), par), pl.BlockSpec((None, S5_ST, S5_CH), par),
                  pl.BlockSpec((None, L, S5_ST), par), pl.BlockSpec((None, L, S5_ST), par),
                  pl.BlockSpec((None, 8, S5_ST), par), pl.BlockSpec((None, 8, S5_ST), par),
                  pl.BlockSpec((1, S5_CH), lambda g, b, c: (0, g))],
        out_specs=[pl.BlockSpec((L, S5_CH), tok), pl.BlockSpec((L, S5_CH), tok),
                   pl.BlockSpec((None, 8, S5_ST), lambda g, b, c: (b * NC + c, 0, g))],
        out_shape=[jax.ShapeDtypeStruct((T, D), F32), jax.ShapeDtypeStruct((T, D), BF16),
                   jax.ShapeDtypeStruct((bsz * NC, 8, S5_NGB * S5_ST), F32)],
        scratch_shapes=[pltpu.VMEM((8, S5_ST), F32), pltpu.VMEM((8, S5_ST), F32)],
        name="s5_fwd", compiler_params=_cparams(("arbitrary", "arbitrary", "arbitrary")),
    )(h, bre, bim, cre, cim, pwr, pwi, l2r, l2i, dskip)


def s5_bwd(h, dy, xs, bre, bim, cre, cim, pwr, pwi, pwr_rev, pwi_rev, l2r, l2i, dskip, bsz):
    T, D = h.shape
    L = S5_L
    S = T // bsz
    NC = S // L

    def body(h_ref, dy_ref, xs_ref, bre_ref, bim_ref, cre_ref, cim_ref, pwr_ref, pwi_ref, qr_ref, qi_ref,
             l2r_ref, l2i_ref, d_ref, du_ref, dbr_ref, dbi_ref, dcr_ref, dci_ref, dl_ref, dd_ref, car_r, car_i):
        first = (pl.program_id(1) == 0) & (pl.program_id(2) == 0)

        @pl.when(first)
        def _():
            for r in (dbr_ref, dbi_ref, dcr_ref, dci_ref, dl_ref, dd_ref):
                r[...] = jnp.zeros_like(r)

        @pl.when(pl.program_id(2) == 0)
        def _():
            car_r[...] = jnp.zeros_like(car_r)
            car_i[...] = jnp.zeros_like(car_i)

        u = h_ref[...]
        ub = u.astype(BF16)
        dyv = dy_ref[...]
        dyb = dyv.astype(BF16)
        l2r_v, l2i_v = l2r_ref[...], l2i_ref[...]
        x0r, x0i = xs_ref[0:1, :], xs_ref[1:2, :]
        xr, xi = _cscan(_dot(ub, bre_ref[...]), _dot(ub, bim_ref[...]), l2r_v, l2i_v, L, False)
        pr, pi = pwr_ref[...], pwi_ref[...]
        xr = xr + pr * x0r - pi * x0i
        xi = xi + pr * x0i + pi * x0r
        gr = _dot_nt(dyb, cre_ref[...])
        gi = -_dot_nt(dyb, cim_ref[...])
        dr, di = _cscan(gr, gi, l2r_v, -l2i_v, L, True)
        qr, qi = qr_ref[...], qi_ref[...]
        cr, ci = car_r[0:1, :], car_i[0:1, :]
        dr = dr + qr * cr + qi * ci
        di = di + qr * ci - qi * cr
        car_r[...] = jnp.broadcast_to(dr[0:1, :], car_r.shape)
        car_i[...] = jnp.broadcast_to(di[0:1, :], car_i.shape)
        row = lax.broadcasted_iota(jnp.int32, xr.shape, 0)
        xpr = jnp.where(row >= 1, pltpu.roll(xr, 1, 0), x0r)
        xpi = jnp.where(row >= 1, pltpu.roll(xi, 1, 0), x0i)
        dl_ref[0:1, :] += _colsum(dr * xpr + di * xpi)
        dl_ref[1:2, :] += _colsum(di * xpr - dr * xpi)
        drb, dib = dr.astype(BF16), di.astype(BF16)
        dcr_ref[...] += _dot_tn(xr.astype(BF16), dyb)
        dci_ref[...] -= _dot_tn(xi.astype(BF16), dyb)
        dbr_ref[...] += _dot_tn(ub, drb)
        dbi_ref[...] += _dot_tn(ub, dib)
        du_ref[...] = _dot_nt(drb, bre_ref[...]) + _dot_nt(dib, bim_ref[...]) + d_ref[...] * dyv
        dd_ref[0:1, :] += _colsum(dyv * u)

    tok = lambda g, b, c: (b * NC + (NC - 1 - c), g)
    par = lambda g, b, c: (g, 0, 0)
    return pl.pallas_call(
        body, grid=(S5_NGB, bsz, NC),
        in_specs=[pl.BlockSpec((L, S5_CH), tok), pl.BlockSpec((L, S5_CH), tok),
                  pl.BlockSpec((None, 8, S5_ST), lambda g, b, c: (b * NC + (NC - 1 - c), 0, g)),
                  pl.BlockSpec((None, S5_CH, S5_ST), par), pl.BlockSpec((None, S5_CH, S5_ST), par),
                  pl.BlockSpec((None, S5_ST, S5_CH), par), pl.BlockSpec((None, S5_ST, S5_CH), par),
                  pl.BlockSpec((None, L, S5_ST), par), pl.BlockSpec((None, L, S5_ST), par),
                  pl.BlockSpec((None, L, S5_ST), par), pl.BlockSpec((None, L, S5_ST), par),
                  pl.BlockSpec((None, 8, S5_ST), par), pl.BlockSpec((None, 8, S5_ST), par),
                  pl.BlockSpec((1, S5_CH), lambda g, b, c: (0, g))],
        out_specs=[pl.BlockSpec((L, S5_CH), tok),
                   pl.BlockSpec((None, S5_CH, S5_ST), par), pl.BlockSpec((None, S5_CH, S5_ST), par),
                   pl.BlockSpec((None, S5_ST, S5_CH), par), pl.BlockSpec((None, S5_ST, S5_CH), par),
                   pl.BlockSpec((None, 8, S5_ST), par),
                   pl.BlockSpec((8, S5_CH), lambda g, b, c: (0, g))],
        out_shape=[jax.ShapeDtypeStruct((T, D), F32),
                   jax.ShapeDtypeStruct((S5_NGB, S5_CH, S5_ST), F32), jax.ShapeDtypeStruct((S5_NGB, S5_CH, S5_ST), F32),
                   jax.ShapeDtypeStruct((S5_NGB, S5_ST, S5_CH), F32), jax.ShapeDtypeStruct((S5_NGB, S5_ST, S5_CH), F32),
                   jax.ShapeDtypeStruct((S5_NGB, 8, S5_ST), F32), jax.ShapeDtypeStruct((8, D), F32)],
        scratch_shapes=[pltpu.VMEM((8, S5_ST), F32), pltpu.VMEM((8, S5_ST), F32)],
        name="s5_bwd", compiler_params=_cparams(("arbitrary", "arbitrary", "arbitrary")),
    )(h, dy, xs, bre, bim, cre, cim, pwr, pwi, pwr_rev, pwi_rev, l2r, l2i, dskip)


def _shift_rows(win, off, n):
    if off == 0:
        return win[:n]
    return pltpu.roll(win, win.shape[0] - off, 0)[:n]


def dwconv_fwd(z, w, b, bsz):
    T, D = z.shape
    S = T // bsz
    TS, CW, PAD = CONV_TS, CONV_CW, CONV_PAD

    def body(z_ref, w_ref, b_ref, y_ref, zp):
        zp[0:PAD, :] = jnp.zeros((PAD, CW), F32)
        zp[PAD:, :] = z_ref[...]
        wv, bv = w_ref[...], b_ref[...]

        def step(t, carry):
            base = pl.multiple_of(t * TS, TS)
            win = zp[pl.ds(base, TS + PAD), :]
            acc = jnp.zeros((TS, CW), F32) + bv
            for k in range(CONV_WIDTH):
                acc = acc + wv[k:k + 1, :] * _shift_rows(win, PAD - (CONV_WIDTH - 1) + k, TS)
            y_ref[pl.ds(base, TS), :] = acc
            return carry
        lax.fori_loop(0, S // TS, step, 0)

    return pl.pallas_call(
        body, grid=(D // CW, bsz),
        in_specs=[pl.BlockSpec((S, CW), lambda c, bb: (bb, c)), pl.BlockSpec((32, CW), lambda c, bb: (0, c)),
                  pl.BlockSpec((1, CW), lambda c, bb: (0, c))],
        out_specs=pl.BlockSpec((S, CW), lambda c, bb: (bb, c)),
        out_shape=jax.ShapeDtypeStruct((T, D), F32),
        scratch_shapes=[pltpu.VMEM((S + PAD, CW), F32)],
        name="dwconv_fwd", compiler_params=_cparams(("arbitrary", "arbitrary")),
    )(z, w, b)


def dwconv_bwd(z, dy, w, bsz):
    T, D = z.shape
    S = T // bsz
    TS, CW, PAD = CONV_TS, CONV_CW, CONV_PAD

    def body(z_ref, dy_ref, w_ref, dz_ref, dw_ref, db_ref, zp, dyp):
        @pl.when(pl.program_id(1) == 0)
        def _():
            dw_ref[...] = jnp.zeros_like(dw_ref)
            db_ref[...] = jnp.zeros_like(db_ref)
        zp[0:PAD, :] = jnp.zeros((PAD, CW), F32)
        zp[PAD:, :] = z_ref[...]
        dyp[0:S, :] = dy_ref[...]
        dyp[S:, :] = jnp.zeros((PAD, CW), F32)
        wv = w_ref[...]

        def step(t, carry):
            base = pl.multiple_of(t * TS, TS)
            zwin = zp[pl.ds(base, TS + PAD), :]
            dwin = dyp[pl.ds(base, TS + PAD), :]
            dyt = dwin[:TS]
            acc = jnp.zeros((TS, CW), F32)
            for j in range(CONV_WIDTH):
                k = CONV_WIDTH - 1 - j
                acc = acc + wv[k:k + 1, :] * _shift_rows(dwin, j, TS)
            dz_ref[pl.ds(base, TS), :] = acc
            for k in range(CONV_WIDTH):
                dw_ref[k:k + 1, :] += _colsum(dyt * _shift_rows(zwin, PAD - (CONV_WIDTH - 1) + k, TS))
            db_ref[0:1, :] += _colsum(dyt)
            return carry
        lax.fori_loop(0, S // TS, step, 0)

    return pl.pallas_call(
        body, grid=(D // CW, bsz),
        in_specs=[pl.BlockSpec((S, CW), lambda c, bb: (bb, c)), pl.BlockSpec((S, CW), lambda c, bb: (bb, c)),
                  pl.BlockSpec((32, CW), lambda c, bb: (0, c))],
        out_specs=[pl.BlockSpec((S, CW), lambda c, bb: (bb, c)), pl.BlockSpec((32, CW), lambda c, bb: (0, c)),
                   pl.BlockSpec((8, CW), lambda c, bb: (0, c))],
        out_shape=[jax.ShapeDtypeStruct((T, D), F32), jax.ShapeDtypeStruct((32, D), F32),
                   jax.ShapeDtypeStruct((8, D), F32)],
        scratch_shapes=[pltpu.VMEM((S + PAD, CW), F32), pltpu.VMEM((S + PAD, CW), F32)],
        name="dwconv_bwd", compiler_params=_cparams(("arbitrary", "arbitrary")),
    )(z, dy, w)


def spatial_fwd(u, vln, ws, bias):
    T, E = u.shape
    C, H = GMLP_CHUNK, GMLP_HEADS
    hw = E // H

    def body(u_ref, v_ref, ws_ref, b_ref, o_ref):
        for hh in range(H):
            sl = slice(hh * hw, (hh + 1) * hw)
            vp = _dot(ws_ref[hh], v_ref[:, sl]) + b_ref[:, sl]
            o_ref[:, sl] = (u_ref[:, sl] * vp).astype(o_ref.dtype)

    return pl.pallas_call(
        body, grid=(T // C,),
        in_specs=[pl.BlockSpec((C, E), lambda i: (i, 0)), pl.BlockSpec((C, E), lambda i: (i, 0)),
                  pl.BlockSpec((H, C, C), lambda i: (0, 0, 0)), pl.BlockSpec((C, E), lambda i: (0, 0))],
        out_specs=pl.BlockSpec((C, E), lambda i: (i, 0)),
        out_shape=jax.ShapeDtypeStruct((T, E), BF16),
        name="spatial_fwd", compiler_params=_cparams(("arbitrary",)),
    )(u, vln, ws, bias)


def spatial_bwd(u, vln, dg, ws, bias):
    T, E = u.shape
    C, H = GMLP_CHUNK, GMLP_HEADS
    hw = E // H

    def body(u_ref, v_ref, dg_ref, ws_ref, b_ref, du_ref, dv_ref, dws_ref, db_ref):
        @pl.when(pl.program_id(0) == 0)
        def _():
            dws_ref[...] = jnp.zeros_like(dws_ref)
            db_ref[...] = jnp.zeros_like(db_ref)
        tril = (lax.broadcasted_iota(jnp.int32, (C, C), 1) <= lax.broadcasted_iota(jnp.int32, (C, C), 0))
        for hh in range(H):
            sl = slice(hh * hw, (hh + 1) * hw)
            v = v_ref[:, sl]
            w = ws_ref[hh]
            dgv = dg_ref[:, sl].astype(F32)
            vp = _dot(w, v) + b_ref[:, sl]
            du_ref[:, sl] = dgv * vp
            dvp = dgv * u_ref[:, sl]
            dvpb = dvp.astype(BF16)
            dv_ref[:, sl] = _dot_tn(w, dvpb)
            dws_ref[hh] += jnp.where(tril, _dot_nt(dvpb, v), 0.0)
            db_ref[:, sl] += dvp

    return pl.pallas_call(
        body, grid=(T // C,),
        in_specs=[pl.BlockSpec((C, E), lambda i: (i, 0)), pl.BlockSpec((C, E), lambda i: (i, 0)),
                  pl.BlockSpec((C, E), lambda i: (i, 0)),
                  pl.BlockSpec((H, C, C), lambda i: (0, 0, 0)), pl.BlockSpec((C, E), lambda i: (0, 0))],
        out_specs=[pl.BlockSpec((C, E), lambda i: (i, 0)), pl.BlockSpec((C, E), lambda i: (i, 0)),
                   pl.BlockSpec((H, C, C), lambda i: (0, 0, 0)), pl.BlockSpec((C, E), lambda i: (0, 0))],
        out_shape=[jax.ShapeDtypeStruct((T, E), F32), jax.ShapeDtypeStruct((T, E), F32),
                   jax.ShapeDtypeStruct((H, C, C), F32), jax.ShapeDtypeStruct((C, E), F32)],
        name="spatial_bwd", compiler_params=_cparams(("arbitrary",)),
    )(u, vln, dg, ws, bias)


def _att_masks():
    r = lax.broadcasted_iota(jnp.int32, (ATT_BLK, ATT_BLK), 0)
    c = lax.broadcasted_iota(jnp.int32, (ATT_BLK, ATT_BLK), 1)
    return c <= r, c >= r


NEG = -1e30
ATT_SCALE = HEAD_DIM ** -0.5


def attn_fwd(name, q, k, v, nb):
    NB = q.shape[0]
    TB = ATT_TB
    assert NB % TB == 0 and (nb % TB == 0 or TB % nb == 0)

    def body(q_ref, k_ref, v_ref, kp_ref, vp_ref, o_ref, l_ref):
        i = pl.program_id(0)
        mc, mp = _att_masks()
        for j in range(TB):
            qj, kc, vc = q_ref[j], k_ref[j], v_ref[j]
            kp = k_ref[j - 1] if j > 0 else kp_ref[0]
            vp = v_ref[j - 1] if j > 0 else vp_ref[0]
            hp = ((i * TB + j) % nb) != 0
            sc = jnp.where(mc, _dot_nt(qj, kc) * ATT_SCALE, NEG)
            sp = jnp.where(mp & hp, _dot_nt(qj, kp) * ATT_SCALE, NEG)
            m = jnp.maximum(jnp.max(sc, axis=1, keepdims=True), jnp.max(sp, axis=1, keepdims=True))
            pc, pp = jnp.exp(sc - m), jnp.exp(sp - m)
            l = jnp.sum(pc, axis=1, keepdims=True) + jnp.sum(pp, axis=1, keepdims=True)
            o = (_dot(pc.astype(BF16), vc) + _dot(pp.astype(BF16), vp)) / l
            o_ref[j] = o
            l_ref[j] = jnp.broadcast_to(m + jnp.log(l), (ATT_BLK, HEAD_DIM))

    blk = pl.BlockSpec((TB, ATT_BLK, HEAD_DIM), lambda i: (i, 0, 0))
    prev = pl.BlockSpec((1, ATT_BLK, HEAD_DIM), lambda i: (jnp.maximum(i * TB - 1, 0), 0, 0))
    return pl.pallas_call(
        body, grid=(NB // TB,), in_specs=[blk, blk, blk, prev, prev], out_specs=[blk, blk],
        out_shape=[jax.ShapeDtypeStruct(q.shape, F32), jax.ShapeDtypeStruct(q.shape, F32)],
        name=name, compiler_params=_cparams(("arbitrary",)),
    )(q, k, v, k, v)


def attn_bwd(name, q, k, v, do, mg, lse, nb):
    NB = q.shape[0]
    TB = ATT_TB

    def body(q_ref, k_ref, v_ref, do_ref, mg_ref, l_ref, kp_ref, vp_ref, qn_ref, don_ref, mgn_ref, ln_ref,
             dq_ref, dk_ref, dv_ref):
        i = pl.program_id(0)
        mc, mp = _att_masks()

        def probs(qj, kk, lse_col, mask):
            s = _dot_nt(qj, kk) * ATT_SCALE
            return jnp.where(mask, jnp.exp(s - lse_col), 0.0)

        def ds_of(p, doj, vv, delta):
            return (p * (_dot_nt(doj, vv) - delta) * ATT_SCALE).astype(BF16)

        dk = [None] * TB
        dv = [None] * TB
        for j in range(TB + 1):
            if j < TB:
                qj, doj, mgj, lj = q_ref[j], do_ref[j], mg_ref[j], l_ref[j]
                hp = ((i * TB + j) % nb) != 0
            else:
                qj, doj, mgj, lj = qn_ref[0], don_ref[0], mgn_ref[0], ln_ref[0]
                hp = (((i + 1) * TB) % nb != 0) & ((i + 1) * TB < NB)
            lse_col = lj[:, 0:1]
            delta = jnp.sum(doj.astype(F32) * mgj.astype(F32), axis=1, keepdims=True)
            if j > 0:
                kp, vp = k_ref[j - 1], v_ref[j - 1]
            else:
                kp, vp = kp_ref[0], vp_ref[0]
            pp = probs(qj, kp, lse_col, mp & hp)
            dsp = ds_of(pp, doj, vp, delta)
            if j > 0:
                dk[j - 1] = dk[j - 1] + _dot_tn(dsp, qj)
                dv[j - 1] = dv[j - 1] + _dot_tn(pp.astype(BF16), doj)
            if j < TB:
                kc, vc = k_ref[j], v_ref[j]
                pc = probs(qj, kc, lse_col, mc)
                dsc = ds_of(pc, doj, vc, delta)
                dq_ref[j] = (_dot(dsc, kc) + _dot(dsp, kp)).astype(dq_ref.dtype)
                dk[j] = _dot_tn(dsc, qj)
                dv[j] = _dot_tn(pc.astype(BF16), doj)
        for j in range(TB):
            dk_ref[j] = dk[j].astype(dk_ref.dtype)
            dv_ref[j] = dv[j].astype(dv_ref.dtype)

    blk = pl.BlockSpec((TB, ATT_BLK, HEAD_DIM), lambda i: (i, 0, 0))
    prev = pl.BlockSpec((1, ATT_BLK, HEAD_DIM), lambda i: (jnp.maximum(i * TB - 1, 0), 0, 0))
    nxt = pl.BlockSpec((1, ATT_BLK, HEAD_DIM), lambda i: (jnp.minimum((i + 1) * TB, NB - 1), 0, 0))
    return pl.pallas_call(
        body, grid=(NB // TB,), in_specs=[blk] * 6 + [prev, prev, nxt, nxt, nxt, nxt], out_specs=[blk, blk, blk],
        out_shape=[jax.ShapeDtypeStruct(q.shape, BF16)] * 3,
        name=name, compiler_params=_cparams(("arbitrary",)),
    )(q, k, v, do, mg, lse, k, v, q, do, mg, lse)


def _to_blocks(t, bsz, dil):
    T = t.shape[0]
    S = T // bsz
    nb = S // (ATT_BLK * dil)
    t = t.reshape(bsz, nb, ATT_BLK, dil, ATT_HEADS, HEAD_DIM).transpose(0, 3, 4, 1, 2, 5)
    return t.reshape(bsz * dil * ATT_HEADS * nb, ATT_BLK, HEAD_DIM)


def _from_blocks(t, bsz, dil):
    nb = t.shape[0] // (bsz * dil * ATT_HEADS)
    t = t.reshape(bsz, dil, ATT_HEADS, nb, ATT_BLK, HEAD_DIM).transpose(0, 3, 4, 1, 2, 5)
    return t.reshape(bsz * nb * ATT_BLK * dil, ATT_HEADS * HEAD_DIM)


def _coords():
    return lax.axis_index("x"), lax.axis_index("y"), lax.axis_index("c")


def all_gather(name, xs):
    def body(x_ref, out_ref, send_sems, recv_sems, local_sem):
        x, y, c = _coords()
        me, sibling = (x, y, c), (x, y, 1 - c)
        chips = [(1 - x, y), (x, 1 - y), (1 - x, 1 - y)]

        def slot(px, py, pc):
            return out_ref.at[4 * px + 2 * py + pc]

        def copy(k, block, to, src=None):
            return pltpu.make_async_remote_copy(
                src_ref=slot(*block) if src is None else src, dst_ref=slot(*block),
                send_sem=send_sems.at[k], recv_sem=recv_sems.at[k], device_id=to, device_id_type=MESH)

        mine = pltpu.make_async_copy(x_ref, slot(*me), local_sem)
        mine.start()
        first = [copy(0, me, sibling, src=x_ref)]
        first += [copy(1 + j, me, (*chip, c), src=x_ref) for j, chip in enumerate(chips)]
        for cp in first:
            cp.start()
        passed = [copy(4 + j, (*chip, c), sibling) for j, chip in enumerate(chips)]
        for j, chip in enumerate(chips):
            copy(1 + j, (*chip, c), me).wait_recv()
            passed[j].start()
        copy(0, sibling, me).wait_recv()
        for j, chip in enumerate(chips):
            copy(4 + j, (*chip, 1 - c), me).wait_recv()
        for cp in first + passed:
            cp.wait_send()
        mine.wait()

    return pl.pallas_call(
        body, out_shape=jax.ShapeDtypeStruct((N_DEV,) + xs.shape, xs.dtype),
        in_specs=[pl.BlockSpec(memory_space=pl.ANY)], out_specs=pl.BlockSpec(memory_space=pl.ANY),
        scratch_shapes=[pltpu.SemaphoreType.DMA((7,)), pltpu.SemaphoreType.DMA((7,)), pltpu.SemaphoreType.DMA],
        name=name,
    )(xs)


def exchange_sibling(name, g):
    _, R, C = g.shape

    def body(g_ref, out_ref, send_sems, recv_sems):
        x, y, c = _coords()
        sibling = (x, y, 1 - c)
        cps = []
        for q in range(4):
            cps.append(pltpu.make_async_remote_copy(
                src_ref=g_ref.at[2 * q + (1 - c)], dst_ref=out_ref.at[q],
                send_sem=send_sems.at[q], recv_sem=recv_sems.at[q], device_id=sibling, device_id_type=MESH))
        for cp in cps:
            cp.start()
        for cp in cps:
            cp.wait_recv()
        for cp in cps:
            cp.wait_send()

    return pl.pallas_call(
        body, out_shape=jax.ShapeDtypeStruct((4, R, C), g.dtype),
        in_specs=[pl.BlockSpec(memory_space=pl.ANY)], out_specs=pl.BlockSpec(memory_space=pl.ANY),
        scratch_shapes=[pltpu.SemaphoreType.DMA((4,)), pltpu.SemaphoreType.DMA((4,))],
        name=name,
    )(g)


def exchange_chips(name, p):
    _, R, C = p.shape

    def body(p_ref, out_ref, send_sems, recv_sems):
        x, y, c = _coords()
        chips = [(1 - x, y), (x, 1 - y), (1 - x, 1 - y)]
        cps = []
        for k, (px, py) in enumerate(chips):
            cps.append(pltpu.make_async_remote_copy(
                src_ref=p_ref.at[2 * px + py], dst_ref=out_ref.at[k],
                send_sem=send_sems.at[k], recv_sem=recv_sems.at[k], device_id=(px, py, c), device_id_type=MESH))
        for cp in cps:
            cp.start()
        for cp in cps:
            cp.wait_recv()
        for cp in cps:
            cp.wait_send()

    return pl.pallas_call(
        body, out_shape=jax.ShapeDtypeStruct((3, R, C), p.dtype),
        in_specs=[pl.BlockSpec(memory_space=pl.ANY)], out_specs=pl.BlockSpec(memory_space=pl.ANY),
        scratch_shapes=[pltpu.SemaphoreType.DMA((3,)), pltpu.SemaphoreType.DMA((3,))],
        name=name,
    )(p)


def add_sibling(g, recv, c_idx, tr):
    _, R, C = g.shape

    def body(c_ref, g_ref, r_ref, o_ref):
        o_ref[...] = g_ref[...] + r_ref[...]

    return pl.pallas_call(
        body,
        grid_spec=pltpu.PrefetchScalarGridSpec(
            num_scalar_prefetch=1, grid=(4, R // tr),
            in_specs=[pl.BlockSpec((None, tr, C), lambda q, i, cr: (2 * q + cr[0], i, 0)),
                      pl.BlockSpec((None, tr, C), lambda q, i, cr: (q, i, 0))],
            out_specs=pl.BlockSpec((None, tr, C), lambda q, i, cr: (q, i, 0))),
        out_shape=jax.ShapeDtypeStruct((4, R, C), F32), name="add_sibling",
        compiler_params=_cparams(("arbitrary", "arbitrary")),
    )(c_idx, g, recv)


def _adam_math(w, g, m, v):
    m = ADAM_B1 * m + (1.0 - ADAM_B1) * g
    v = ADAM_B2 * v + (1.0 - ADAM_B2) * jnp.square(g)
    m_hat = m / (1.0 - ADAM_B1 ** ADAM_STEP)
    v_hat = v / (1.0 - ADAM_B2 ** ADAM_STEP)
    delta = -ADAM_LR * (m_hat / (jnp.sqrt(v_hat) + ADAM_EPS) + ADAM_WD * w)
    return delta, m, v


def adam_big(p1, recv, w, m, v, chip_idx, tr):
    R, C = w.shape

    def body(q_ref, p_ref, r_ref, w_ref, m_ref, v_ref, g_ref, d_ref, nm_ref, nv_ref):
        g = ((p_ref[...] + r_ref[0]) + r_ref[1]) + r_ref[2]
        d, nm, nv = _adam_math(w_ref[...], g, m_ref[...], v_ref[...])
        g_ref[...] = g
        d_ref[...] = d
        nm_ref[...] = nm
        nv_ref[...] = nv

    row = pl.BlockSpec((tr, C), lambda i, qr: (i, 0))
    return pl.pallas_call(
        body,
        grid_spec=pltpu.PrefetchScalarGridSpec(
            num_scalar_prefetch=1, grid=(R // tr,),
            in_specs=[pl.BlockSpec((None, tr, C), lambda i, qr: (qr[0], i, 0)),
                      pl.BlockSpec((3, tr, C), lambda i, qr: (0, i, 0)), row, row, row],
            out_specs=[row, row, row, row]),
        out_shape=[jax.ShapeDtypeStruct((R, C), F32)] * 4, name="adam_big",
        compiler_params=_cparams(("arbitrary",)),
    )(chip_idx, p1, recv, w, m, v)


def sum8(parts):
    _, R, C = parts.shape

    def body(p_ref, o_ref):
        acc = p_ref[0]
        for k in range(1, N_DEV):
            acc = acc + p_ref[k]
        o_ref[...] = acc

    tr = 128
    while R % tr:
        tr //= 2
    assert tr % 8 == 0
    return pl.pallas_call(
        body, grid=(R // tr,), in_specs=[pl.BlockSpec((N_DEV, tr, C), lambda i: (0, i, 0))],
        out_specs=pl.BlockSpec((tr, C), lambda i: (i, 0)), out_shape=jax.ShapeDtypeStruct((R, C), F32),
        name="sum8", compiler_params=_cparams(("arbitrary",)),
    )(parts)


def adam_small(w, g, m, v):
    def fn(wt, gt, mt, vt):
        return _adam_math(wt, gt, mt, vt)
    C = w.shape[1]
    return rowwise("adam_small", fn, [w, g, m, v], [], [(C, F32)] * 3, tr=128)


def _pack(arrs, rows_mult=8):
    flat = jnp.concatenate([a.reshape(-1) for a in arrs])
    n = flat.shape[0]
    per = PACK_C * rows_mult
    pad = (-n) % per
    if pad:
        flat = jnp.concatenate([flat, jnp.zeros((pad,), flat.dtype)])
    return flat.reshape(-1, PACK_C)


def _unpack(buf, shapes):
    flat = buf.reshape(-1)
    out, off = [], 0
    for s in shapes:
        n = math.prod(s)
        out.append(flat[off:off + n].reshape(s))
        off += n
    return out


def _blocked(gfull, axis):
    shp = gfull.shape
    n = shp[axis] // N_DEV
    t = gfull.reshape(shp[:axis] + (N_DEV, n) + shp[axis + 1:])
    t = jnp.moveaxis(t, axis, 0)
    return t.reshape(N_DEV, -1)


def _unblocked(gathered, shard_shape, axis):
    t = jnp.moveaxis(gathered, 0, axis)
    shp = shard_shape[:axis] + (N_DEV * shard_shape[axis],) + shard_shape[axis + 1:]
    return t.reshape(shp)


def _relu2_epi(acc):
    r = jnp.maximum(acc, 0.0)
    return acc, r * r


def _step(x3, target3, W, Wfull):
    bsz, S, D = x3.shape
    T = bsz * S
    x = x3.reshape(T, D)
    target = target3.reshape(T, D)
    row = lambda v: v.reshape(1, -1)
    grads = {}

    s5p = (W['ssm_a_re'][0], W['ssm_a_im'][0], W['ssm_log_dt'][0], W['ssm_b_re'][0], W['ssm_b_im'][0])
    (lam_re, lam_im, bb_re, bb_im), s5_disc_vjp = jax.vjp(s5_disc, *s5p)
    pwr, pwi, l2r, l2i = s5_tables(lam_re, lam_im, S5_L)
    bre, bim = _s5_blockdiag_b(bb_re).astype(BF16), _s5_blockdiag_b(bb_im).astype(BF16)
    cre, cim = _s5_blockdiag_c(W['ssm_c_re'][0]).astype(BF16), _s5_blockdiag_c(W['ssm_c_im'][0]).astype(BF16)
    dskip = W['ssm_d']

    tril = jnp.tril(jnp.ones((GMLP_CHUNK, GMLP_CHUNK), bool))
    ws = jnp.where(tril[None], W['gmlp_w_s'][0], 0.0).astype(BF16)
    hw = D // GMLP_HEADS
    sbias = jnp.repeat(W['gmlp_b_s'][0].T, hw, axis=1)

    conv_w = jnp.concatenate([W['conv_w_dw'][0], jnp.zeros((1, D), F32)], axis=0)

    saved = []
    for i in range(DEPTH):
        sv = {'x': x}
        nm = W['norm_mix'][i:i + 1]
        if i == 0:
            h, hf = rms_fwd("rms_mix0", x, nm, want_f32=True)
            ypre, gy, xs = s5_fwd(hf, bre, bim, cre, cim, pwr, pwi, l2r, l2i, dskip, bsz)
            z, = matmul("s5_glu_mm", gy, Wfull['ssm_w_glu'])
            x1, = rowwise("s5_glu", lambda zt, xt: xt + _glu(zt), [z, x], [], [(D, F32)])
            sv.update(hf=hf, ypre=ypre, gy=gy, xs=xs, z=z)
        elif i == 1:
            h, = rms_fwd("rms_mix1", x, nm)
            z, = matmul("conv_pw1", h, Wfull['conv_w_pw1'], epi=lambda acc, b: (acc + b,),
                        extras=[(W['conv_b_pw1'], 'row')])
            zg, = rowwise("conv_glu", _glu, [z], [], [(D, F32)])
            yc = dwconv_fwd(zg, conv_w, W['conv_b_dw'], bsz)
            y2, = rowwise("conv_ln_silu", lambda t, g, b: jax.nn.silu(_ln(t, g, b)), [yc],
                          [W['conv_ln_g'], W['conv_ln_b']], [(D, BF16)])
            x1, = matmul("conv_pw2", y2, Wfull['conv_w_pw2'], epi=lambda acc, b, r: (acc + b + r,),
                         extras=[(W['conv_b_pw2'], 'row'), (x, 'tile')])
            sv.update(h=h, z=z, zg=zg, yc=yc, y2=y2)
        elif i == 2:
            h, = rms_fwd("rms_mix2", x, nm)
            zp, = matmul("gmlp_in", h, Wfull['gmlp_w_in'])

            def gm_pre(zt, g, b):
                a = jax.nn.gelu(zt)
                return a[:, :D], _ln(a[:, D:], g, b)
            u, vln = rowwise("gmlp_pre", gm_pre, [zp], [W['gmlp_ln_g'], W['gmlp_ln_b']], [(D, F32), (D, BF16)])
            gated = spatial_fwd(u, vln, ws, sbias)
            x1, = matmul("gmlp_out", gated, Wfull['gmlp_w_out'], epi=lambda acc, r: (acc + r,), extras=[(x, 'tile')])
            sv.update(h=h, zp=zp, u=u, vln=vln, gated=gated)
        else:
            h, = rms_fwd("rms_mix3", x, nm)
            qkv, = matmul("attn_qkv", h, Wfull['attn_w_qkv'], out_dtypes=(BF16,))
            qkv5 = qkv.reshape(T, 3, len(ATT_CONFIGS), ATT_W)
            outs, lses, blocks = [], [], []
            for gi, (window, dil) in enumerate(ATT_CONFIGS):
                nb = S // (ATT_BLK * dil)
                qb, kb, vb = (_to_blocks(qkv5[:, j, gi], bsz, dil) for j in range(3))
                ob, lb = attn_fwd("attn_fwd%d" % gi, qb, kb, vb, nb)
                blocks.append((qb, kb, vb, lb, nb, dil))
                outs.append(_from_blocks(ob, bsz, dil))
                lses.append(_from_blocks(lb, bsz, dil))

            def merge(o0, o1, o2, l0, l1, l2):
                m = jnp.maximum(jnp.maximum(l0, l1), l2)
                e0, e1, e2 = jnp.exp(l0 - m), jnp.exp(l1 - m), jnp.exp(l2 - m)
                inv = 1.0 / (e0 + e1 + e2)
                w0, w1, w2 = e0 * inv, e1 * inv, e2 * inv
                return w0 * o0 + w1 * o1 + w2 * o2, w0, w1, w2
            merged, w0, w1, w2 = rowwise("attn_merge", merge, outs + lses, [],
                                         [(ATT_W, BF16), (ATT_W, F32), (ATT_W, F32), (ATT_W, F32)])
            x1, = matmul("attn_o", merged, Wfull['attn_w_o'], epi=lambda acc, r: (acc + r,), extras=[(x, 'tile')])
            sv.update(h=h, blocks=blocks, merged=merged, wts=(w0, w1, w2))
        h2, = rms_fwd("rms_mlp%d" % i, x1, W['norm_mlp'][i:i + 1])
        a, act = matmul("mlp_in%d" % i, h2, Wfull['mlp_w_in'][i], epi=_relu2_epi, out_dtypes=(BF16, BF16))
        x2, = matmul("mlp_out%d" % i, act, Wfull['mlp_w_out'][i], epi=lambda acc, r: (acc + r,), extras=[(x1, 'tile')])
        sv.update(x1=x1, h2=h2, a=a, act=act)
        saved.append(sv)
        x = x2

    def loss_fn(xt, tt, g):
        y, vjp = jax.vjp(_rms, xt, g)
        err = y - tt
        dxx, dg = vjp(err * (1.0 / D))
        lval = jnp.sum(jnp.sum(err * err, axis=1, keepdims=True), axis=0, keepdims=True) * (0.5 / D)
        return dxx, dxx, jnp.broadcast_to(lval, (1, 128)), dg
    dx, dxb, lacc, dnf = rowwise("loss_head", loss_fn, [x, target], [row(W['norm_final'])],
                                 [(D, F32), (D, BF16)], [((1, 128), F32), ((1, D), F32)])
    loss_local = lacc[0, 0]
    grads['norm_final'] = dnf.reshape(-1)

    g_norm_mix, g_norm_mlp = [None] * DEPTH, [None] * DEPTH
    g_mlp_in, g_mlp_out = [None] * DEPTH, [None] * DEPTH
    for i in reversed(range(DEPTH)):
        sv = saved[i]
        da, = matmul("mlp_out_bwd%d" % i, dxb, Wfull['mlp_w_out'][i], nt=True,
                     epi=lambda acc, av: (acc * (2.0 * jnp.maximum(av.astype(F32), 0.0)),),
                     extras=[(sv['a'], 'tile')], out_dtypes=(BF16,))
        g_mlp_out[i] = wgrad("mlp_out_wg%d" % i, sv['act'], dxb)
        dh2, = matmul("mlp_in_bwd%d" % i, da, Wfull['mlp_w_in'][i], nt=True)
        g_mlp_in[i] = wgrad("mlp_in_wg%d" % i, sv['h2'], da)
        dx, dxb, dg = rms_bwd("rms_mlp_bwd%d" % i, sv['x1'], dh2, dx, W['norm_mlp'][i:i + 1])
        g_norm_mlp[i] = dg.reshape(-1)
        xin = sv['x']
        if i == 0:
            dz, = rowwise("s5_glu_bwd", _glu_bwd, [sv['z'], dx], [], [(2 * D, BF16)])
            dgy, = matmul("s5_glu_mm_bwd", dz, Wfull['ssm_w_glu'], nt=True)
            grads['ssm_w_glu'] = wgrad("s5_glu_wg", sv['gy'], dz)[None]

            def gelu_bwd(yt, dt):
                _, vjp = jax.vjp(jax.nn.gelu, yt)
                return vjp(dt)[0]
            dypre, = rowwise("s5_gelu_bwd", gelu_bwd, [sv['ypre'], dgy], [], [(D, F32)])
            du, dbr, dbi, dcr, dci, dl, dd = s5_bwd(sv['hf'], dypre, sv['xs'], bre, bim, cre, cim, pwr, pwi,
                                                    pwr[:, ::-1], pwi[:, ::-1], l2r, l2i, dskip, bsz)
            dlam_re = dl[:, 0, :].reshape(SSM_GROUPS, SSM_STATE)
            dlam_im = dl[:, 1, :].reshape(SSM_GROUPS, SSM_STATE)
            s5_cot = (dlam_re, dlam_im, _s5_blockdiag_b_inv(dbr), _s5_blockdiag_b_inv(dbi))
            grads['ssm_c_re'] = _s5_blockdiag_c_inv(dcr)[None]
            grads['ssm_c_im'] = _s5_blockdiag_c_inv(dci)[None]
            grads['ssm_d'] = dd[0:1]
            dh = du
        elif i == 1:
            dy2, = matmul("conv_pw2_bwd", dxb, Wfull['conv_w_pw2'], nt=True)
            grads['conv_w_pw2'] = wgrad("conv_pw2_wg", sv['y2'], dxb)[None]

            def ln_silu_bwd(yt, dt, dxt, g, b):
                _, vjp = jax.vjp(lambda t, gg, bb: jax.nn.silu(_ln(t, gg, bb)), yt, g, b)
                dyc, dgg, dbb = vjp(dt)
                return dyc, dgg, dbb, _colsum(dxt)
            dyc, dlg, dlb, dbp2 = rowwise("conv_ln_silu_bwd", ln_silu_bwd, [sv['yc'], dy2, dx],
                                          [W['conv_ln_g'], W['conv_ln_b']], [(D, F32)],
                                          [((1, D), F32), ((1, D), F32), ((1, D), F32)])
            grads['conv_ln_g'], grads['conv_ln_b'], grads['conv_b_pw2'] = dlg, dlb, dbp2
            dzg, dwd, dbd = dwconv_bwd(sv['zg'], dyc, conv_w, bsz)
            grads['conv_w_dw'] = dwd[None, :CONV_WIDTH]
            grads['conv_b_dw'] = dbd[0:1]

            def glu_bwd1(zt, dyt):
                dzt = _glu_bwd(zt, dyt)
                return dzt, _colsum(dzt)
            dz, dbp1 = rowwise("conv_glu_bwd", glu_bwd1, [sv['z'], dzg], [], [(2 * D, BF16)], [((1, 2 * D), F32)])
            grads['conv_b_pw1'] = dbp1
            dh, = matmul("conv_pw1_bwd", dz, Wfull['conv_w_pw1'], nt=True)
            grads['conv_w_pw1'] = wgrad("conv_pw1_wg", sv['h'], dz)[None]
        elif i == 2:
            dgt, = matmul("gmlp_out_bwd", dxb, Wfull['gmlp_w_out'], nt=True, out_dtypes=(BF16,))
            grads['gmlp_w_out'] = wgrad("gmlp_out_wg", sv['gated'], dxb)[None]
            du, dvln, dws, dsb = spatial_bwd(sv['u'], sv['vln'], dgt, ws, sbias)
            grads['gmlp_w_s'] = dws[None]
            grads['gmlp_b_s'] = dsb.reshape(GMLP_CHUNK, GMLP_HEADS, hw).sum(-1).T[None]

            def gm_pre_bwd(zt, dut, dvt, g, b):
                _, vjp_u = jax.vjp(jax.nn.gelu, zt[:, :D])
                _, vjp_v = jax.vjp(lambda zz, gg, bb: _ln(jax.nn.gelu(zz), gg, bb), zt[:, D:], g, b)
                dz2, dgg, dbb = vjp_v(dvt)
                return jnp.concatenate([vjp_u(dut)[0], dz2], axis=1), dgg, dbb
            dzp, dlg, dlb = rowwise("gmlp_pre_bwd", gm_pre_bwd, [sv['zp'], du, dvln],
                                    [W['gmlp_ln_g'], W['gmlp_ln_b']], [(2 * D, BF16)], [((1, D), F32), ((1, D), F32)])
            grads['gmlp_ln_g'], grads['gmlp_ln_b'] = dlg, dlb
            dh, = matmul("gmlp_in_bwd", dzp, Wfull['gmlp_w_in'], nt=True)
            grads['gmlp_w_in'] = wgrad("gmlp_in_wg", sv['h'], dzp)[None]
        else:
            dm, = matmul("attn_o_bwd", dxb, Wfull['attn_w_o'], nt=True)
            grads['attn_w_o'] = wgrad("attn_o_wg", sv['merged'], dxb)[None]
            w0, w1, w2 = sv['wts']
            do0, do1, do2 = rowwise("attn_merge_bwd", lambda d, a, b, c: (a * d, b * d, c * d), [dm, w0, w1, w2], [],
                                    [(ATT_W, BF16)] * 3)
            dparts = [[None] * 3 for _ in range(3)]
            for gi, (dog, (qb, kb, vb, lb, nb, dil)) in enumerate(zip((do0, do1, do2), sv['blocks'])):
                dob = _to_blocks(dog, bsz, dil)
                mgb = _to_blocks(sv['merged'], bsz, dil)
                dqb, dkb, dvb = attn_bwd("attn_bwd%d" % gi, qb, kb, vb, dob, mgb, lb, nb)
                for j, t in enumerate((dqb, dkb, dvb)):
                    dparts[j][gi] = _from_blocks(t, bsz, dil)
            dqkv = jnp.concatenate([dparts[j][gi] for j in range(3) for gi in range(3)], axis=1)
            dh, = matmul("attn_qkv_bwd", dqkv, Wfull['attn_w_qkv'], nt=True)
            grads['attn_w_qkv'] = wgrad("attn_qkv_wg", sv['h'], dqkv)[None]
        dx, dxb, dg = rms_bwd("rms_mix_bwd%d" % i, xin, dh, dx, W['norm_mix'][i:i + 1])
        g_norm_mix[i] = dg.reshape(-1)

    grads['norm_mix'] = jnp.stack(g_norm_mix)
    grads['norm_mlp'] = jnp.stack(g_norm_mlp)
    grads['mlp_w_in'] = jnp.stack(g_mlp_in)
    grads['mlp_w_out'] = jnp.stack(g_mlp_out)
    return loss_local, dx.reshape(bsz, S, D), grads, (s5_disc_vjp, s5_cot)


def kernel(x, norm_mix, norm_mlp, norm_final, ssm_a_re, ssm_a_im, ssm_b_re, ssm_b_im, ssm_c_re, ssm_c_im, ssm_d, ssm_log_dt, ssm_w_glu, conv_w_pw1, conv_b_pw1, conv_w_dw, conv_b_dw, conv_ln_g, conv_ln_b, conv_w_pw2, conv_b_pw2, gmlp_w_in, gmlp_ln_g, gmlp_ln_b, gmlp_w_s, gmlp_b_s, gmlp_w_out, attn_w_qkv, attn_w_o, mlp_w_in, mlp_w_out, loss_target, m_norm_mix, m_norm_mlp, m_norm_final, m_ssm_a_re, m_ssm_a_im, m_ssm_b_re, m_ssm_b_im, m_ssm_c_re, m_ssm_c_im, m_ssm_d, m_ssm_log_dt, m_ssm_w_glu, m_conv_w_pw1, m_conv_b_pw1, m_conv_w_dw, m_conv_b_dw, m_conv_ln_g, m_conv_ln_b, m_conv_w_pw2, m_conv_b_pw2, m_gmlp_w_in, m_gmlp_ln_g, m_gmlp_ln_b, m_gmlp_w_s, m_gmlp_b_s, m_gmlp_w_out, m_attn_w_qkv, m_attn_w_o, m_mlp_w_in, m_mlp_w_out, v_norm_mix, v_norm_mlp, v_norm_final, v_ssm_a_re, v_ssm_a_im, v_ssm_b_re, v_ssm_b_im, v_ssm_c_re, v_ssm_c_im, v_ssm_d, v_ssm_log_dt, v_ssm_w_glu, v_conv_w_pw1, v_conv_b_pw1, v_conv_w_dw, v_conv_b_dw, v_conv_ln_g, v_conv_ln_b, v_conv_w_pw2, v_conv_b_pw2, v_gmlp_w_in, v_gmlp_ln_g, v_gmlp_ln_b, v_gmlp_w_s, v_gmlp_b_s, v_gmlp_w_out, v_attn_w_qkv, v_attn_w_o, v_mlp_w_in, v_mlp_w_out):
    args = (norm_mix, norm_mlp, norm_final, ssm_a_re, ssm_a_im, ssm_b_re, ssm_b_im, ssm_c_re, ssm_c_im, ssm_d,
            ssm_log_dt, ssm_w_glu, conv_w_pw1, conv_b_pw1, conv_w_dw, conv_b_dw, conv_ln_g, conv_ln_b, conv_w_pw2,
            conv_b_pw2, gmlp_w_in, gmlp_ln_g, gmlp_ln_b, gmlp_w_s, gmlp_b_s, gmlp_w_out, attn_w_qkv, attn_w_o,
            mlp_w_in, mlp_w_out)
    margs = (m_norm_mix, m_norm_mlp, m_norm_final, m_ssm_a_re, m_ssm_a_im, m_ssm_b_re, m_ssm_b_im, m_ssm_c_re,
             m_ssm_c_im, m_ssm_d, m_ssm_log_dt, m_ssm_w_glu, m_conv_w_pw1, m_conv_b_pw1, m_conv_w_dw, m_conv_b_dw,
             m_conv_ln_g, m_conv_ln_b, m_conv_w_pw2, m_conv_b_pw2, m_gmlp_w_in, m_gmlp_ln_g, m_gmlp_ln_b,
             m_gmlp_w_s, m_gmlp_b_s, m_gmlp_w_out, m_attn_w_qkv, m_attn_w_o, m_mlp_w_in, m_mlp_w_out)
    vargs = (v_norm_mix, v_norm_mlp, v_norm_final, v_ssm_a_re, v_ssm_a_im, v_ssm_b_re, v_ssm_b_im, v_ssm_c_re,
             v_ssm_c_im, v_ssm_d, v_ssm_log_dt, v_ssm_w_glu, v_conv_w_pw1, v_conv_b_pw1, v_conv_w_dw, v_conv_b_dw,
             v_conv_ln_g, v_conv_ln_b, v_conv_w_pw2, v_conv_b_pw2, v_gmlp_w_in, v_gmlp_ln_g, v_gmlp_ln_b,
             v_gmlp_w_s, v_gmlp_b_s, v_gmlp_w_out, v_attn_w_qkv, v_attn_w_o, v_mlp_w_in, v_mlp_w_out)
    Wl = dict(zip(WEIGHT_NAMES, args))
    Ml = dict(zip(WEIGHT_NAMES, margs))
    Vl = dict(zip(WEIGHT_NAMES, vargs))
    cx, cy, cc = _coords()
    my_idx = 4 * cx + 2 * cy + cc

    big_names = list(BIG)
    wpack = _pack([Wl[n].astype(BF16) for n in big_names], rows_mult=16)
    wg = all_gather("gather_weights", wpack)
    parts = _unpack_gathered(wg, [Wl[n].shape for n in big_names])
    Wfull = {n: _unblocked(p, Wl[n].shape, BIG[n]) for n, p in zip(big_names, parts)}
    Wfull = {n: (w[0] if n not in ('mlp_w_in', 'mlp_w_out') else w) for n, w in Wfull.items()}
    ss_names = list(SMALL_SHARDED)
    spack = _pack([Wl[n] for n in ss_names])
    sg = all_gather("gather_small", spack)
    sparts = _unpack_gathered(sg, [Wl[n].shape for n in ss_names])
    W = {n: Wl[n] for n in SMALL if n not in SMALL_SHARDED}
    for n, p in zip(ss_names, sparts):
        W[n] = _unblocked(p, Wl[n].shape, SMALL_SHARDED[n])

    loss_local, grad_x, grads, (s5_disc_vjp, s5_cot) = _step(x, loss_target, W, Wfull)
    loss = lax.psum(loss_local, MESH_AXES)

    gblk = jnp.concatenate([_blocked(grads[n], BIG[n]) for n in big_names], axis=1)
    nsh = gblk.shape[1]
    R = nsh // PACK_C
    gblk = gblk.reshape(N_DEV, R, PACK_C)
    tr = 192 if R % 192 == 0 else 8
    recv1 = exchange_sibling("rs_sibling", gblk)
    p1 = add_sibling(gblk, recv1, cc.reshape(1).astype(jnp.int32), tr)
    recv2 = exchange_chips("rs_chips", p1)
    wp, mp_, vp = (_pack([d[n] for n in big_names]) for d in (Wl, Ml, Vl))
    chip_idx = (2 * cx + cy).reshape(1).astype(jnp.int32)
    gB, dB, mB, vB = adam_big(p1, recv2, wp, mp_, vp, chip_idx, tr)
    big_shapes = [Wl[n].shape for n in big_names]
    out_g = dict(zip(big_names, _unpack(gB, big_shapes)))
    out_d = dict(zip(big_names, _unpack(dB, big_shapes)))
    out_m = dict(zip(big_names, _unpack(mB, big_shapes)))
    out_v = dict(zip(big_names, _unpack(vB, big_shapes)))

    s5_lin = ['ssm_a_re', 'ssm_a_im', 'ssm_log_dt', 'ssm_b_re', 'ssm_b_im']
    direct = [n for n in SMALL if n not in s5_lin]
    full_shape = lambda n: W[n].shape
    small_parts = [grads[n].reshape(full_shape(n)) for n in direct] + list(s5_cot)
    gsum = sum8(all_gather("gather_small_grads", _pack(small_parts)))
    summed = _unpack(gsum, [p.shape for p in small_parts])
    gsmall = dict(zip(direct, summed[:len(direct)]))
    s5g = s5_disc_vjp(tuple(summed[len(direct):]))
    for n, gval in zip(s5_lin, s5g):
        gsmall[n] = gval[None]
    for n, ax in SMALL_SHARDED.items():
        gsmall[n] = lax.dynamic_slice_in_dim(gsmall[n], my_idx * Wl[n].shape[ax], Wl[n].shape[ax], axis=ax)
    sm_shapes = [Wl[n].shape for n in SMALL]
    dS, mS, vS = adam_small(_pack([Wl[n] for n in SMALL]), _pack([gsmall[n] for n in SMALL]),
                            _pack([Ml[n] for n in SMALL]), _pack([Vl[n] for n in SMALL]))
    for n, gval in zip(SMALL, [gsmall[n] for n in SMALL]):
        out_g[n] = gval.reshape(Wl[n].shape)
    out_d.update(zip(SMALL, _unpack(dS, sm_shapes)))
    out_m.update(zip(SMALL, _unpack(mS, sm_shapes)))
    out_v.update(zip(SMALL, _unpack(vS, sm_shapes)))

    return (loss, grad_x, *[out_g[n] for n in WEIGHT_NAMES], *[out_d[n] for n in WEIGHT_NAMES],
            *[out_m[n] for n in WEIGHT_NAMES], *[out_v[n] for n in WEIGHT_NAMES])


def _unpack_gathered(g, shard_shapes):
    flat = g.reshape(N_DEV, -1)
    out, off = [], 0
    for s in shard_shapes:
        n = math.prod(s)
        out.append(flat[:, off:off + n].reshape((N_DEV,) + tuple(s)))
        off += n
    return out
```

```python
import functools
import math

import jax
import jax.numpy as jnp
from jax import lax
from jax.experimental import pallas as pl
from jax.experimental.pallas import tpu as pltpu

F32 = jnp.float32
BF16 = jnp.bfloat16

D_MODEL = 1024
DEPTH = 4
EPS = 1e-6
SSM_GROUP = 16
SSM_GROUPS = 64
SSM_STATE = 64
S5_GB = 8
S5_NGB = SSM_GROUPS // S5_GB
S5_CH = S5_GB * SSM_GROUP
S5_ST = S5_GB * SSM_STATE
S5_L = 128
CONV_WIDTH = 31
CONV_PAD = 32
CONV_TS = 256
CONV_CW = 256
GMLP_CHUNK = 128
GMLP_HEADS = 4
ATT_CONFIGS = ((128, 1), (512, 4), (2048, 16))
ATT_HEADS = 8
HEAD_DIM = 64
ATT_BLK = 128
ATT_TB = 8
ATT_W = ATT_HEADS * HEAD_DIM
N_DEV = 8
ADAM_LR = 0.001
ADAM_B1 = 0.9
ADAM_B2 = 0.999
ADAM_EPS = 1e-08
ADAM_WD = 0.01
ADAM_STEP = 10
VMEM_LIMIT = 56 * 1024 * 1024
PACK_C = 1024
MESH_AXES = ("x", "y", "c")
MESH = pl.DeviceIdType.MESH

WEIGHT_NAMES = ['norm_mix', 'norm_mlp', 'norm_final', 'ssm_a_re', 'ssm_a_im', 'ssm_b_re', 'ssm_b_im',
                'ssm_c_re', 'ssm_c_im', 'ssm_d', 'ssm_log_dt', 'ssm_w_glu', 'conv_w_pw1', 'conv_b_pw1',
                'conv_w_dw', 'conv_b_dw', 'conv_ln_g', 'conv_ln_b', 'conv_w_pw2', 'conv_b_pw2',
                'gmlp_w_in', 'gmlp_ln_g', 'gmlp_ln_b', 'gmlp_w_s', 'gmlp_b_s', 'gmlp_w_out',
                'attn_w_qkv', 'attn_w_o', 'mlp_w_in', 'mlp_w_out']
BIG = {'ssm_w_glu': 2, 'conv_w_pw1': 2, 'conv_w_pw2': 1, 'gmlp_w_in': 2, 'gmlp_w_out': 1,
       'attn_w_qkv': 2, 'attn_w_o': 2, 'mlp_w_in': 2, 'mlp_w_out': 1}
SMALL_SHARDED = {'conv_b_pw1': 1, 'conv_w_dw': 2, 'conv_b_dw': 1, 'conv_ln_g': 1, 'conv_ln_b': 1,
                 'conv_b_pw2': 1, 'gmlp_ln_g': 1, 'gmlp_ln_b': 1}
SMALL = [n for n in WEIGHT_NAMES if n not in BIG]


def _cparams(sem=None):
    return pltpu.CompilerParams(dimension_semantics=sem, vmem_limit_bytes=VMEM_LIMIT)


def _dot(a, b):
    return jnp.dot(a, b, preferred_element_type=F32)


def _dot_nt(a, b):
    return lax.dot_general(a, b, (((1,), (1,)), ((), ())), preferred_element_type=F32)


def _dot_tn(a, b):
    return lax.dot_general(a, b, (((0,), (0,)), ((), ())), preferred_element_type=F32)


def rowwise(name, fn, rows, params, row_out, acc_out=(), tr=256):
    T = rows[0].shape[0]
    tr = min(tr, T)
    while T % tr:
        tr //= 2
    assert tr % 8 == 0
    nr, npar, nro = len(rows), len(params), len(row_out)

    def body(*refs):
        ins = [r[...] for r in refs[:nr + npar]]
        outs = refs[nr + npar:]
        res = fn(*ins)
        if not isinstance(res, (tuple, list)):
            res = (res,)
        for k in range(nro):
            outs[k][...] = res[k].astype(outs[k].dtype)
        if acc_out:
            @pl.when(pl.program_id(0) == 0)
            def _():
                for k in range(nro, len(outs)):
                    outs[k][...] = jnp.zeros_like(outs[k])
            for k in range(nro, len(outs)):
                outs[k][...] += res[k].astype(outs[k].dtype)

    in_specs = [pl.BlockSpec((tr, r.shape[1]), lambda i: (i, 0)) for r in rows]
    in_specs += [pl.BlockSpec(p.shape, lambda i, nd=p.ndim: (0,) * nd) for p in params]
    out_shape = [jax.ShapeDtypeStruct((T, c), dt) for c, dt in row_out]
    out_specs = [pl.BlockSpec((tr, c), lambda i: (i, 0)) for c, dt in row_out]
    out_shape += [jax.ShapeDtypeStruct(s, dt) for s, dt in acc_out]
    out_specs += [pl.BlockSpec(s, lambda i, nd=len(s): (0,) * nd) for s, dt in acc_out]
    res = pl.pallas_call(body, grid=(T // tr,), in_specs=in_specs, out_specs=out_specs, out_shape=out_shape,
                         name=name, compiler_params=_cparams(("arbitrary",)))(*rows, *params)
    return res


def _tile_m(M, K):
    tm = 1024
    while tm > 256 and tm * K * 2 > (4 << 20):
        tm //= 2
    return min(tm, M)


def matmul(name, a, b, *, mode='nn', epi=None, extras=(), out_dtypes=(F32,), out3=False):
    M, K = a.shape
    if mode == 'cb':
        nblk, _, tn = b.shape
        N = nblk * tn
    else:
        N = b.shape[0] if mode == 'nt' else b.shape[1]
        tn = min(512, N)
    tm = _tile_m(M, K)
    assert M % tm == 0 and N % tn == 0, (M, N, tm, tn)
    nex = len(extras)

    def body(a_ref, b_ref, *rest):
        ex = [e[...] for e in rest[:nex]]
        outs = rest[nex:]
        acc = _dot_nt(a_ref[...], b_ref[...]) if mode == 'nt' else _dot(a_ref[...], b_ref[...])
        res = epi(acc, *ex) if epi is not None else (acc,)
        for o, r in zip(outs, res):
            o[...] = r.astype(o.dtype)

    b_spec = {'nn': pl.BlockSpec((K, tn), lambda j, i: (0, j)), 'nt': pl.BlockSpec((tn, K), lambda j, i: (j, 0)),
              'cb': pl.BlockSpec((None, K, tn), lambda j, i: (j, 0, 0))}[mode]
    in_specs = [pl.BlockSpec((tm, K), lambda j, i: (i, 0)), b_spec]
    for arr, kind in extras:
        if kind == 'tile':
            in_specs.append(pl.BlockSpec((tm, tn), lambda j, i: (i, j)))
        else:
            in_specs.append(pl.BlockSpec((1, tn), lambda j, i: (0, j)))
    if out3:
        out_shape = [jax.ShapeDtypeStruct((N // tn, M, tn), dt) for dt in out_dtypes]
        out_specs = [pl.BlockSpec((None, tm, tn), lambda j, i: (j, i, 0)) for dt in out_dtypes]
    else:
        out_shape = [jax.ShapeDtypeStruct((M, N), dt) for dt in out_dtypes]
        out_specs = [pl.BlockSpec((tm, tn), lambda j, i: (i, j)) for dt in out_dtypes]
    return pl.pallas_call(body, grid=(N // tn, M // tm), in_specs=in_specs, out_specs=out_specs,
                          out_shape=out_shape, name=name,
                          compiler_params=_cparams(("arbitrary", "arbitrary")))(a, b, *[e[0] for e in extras])


def matmul_nt_cb(name, a, b, *, a3=False, epi=None, extras=(), out_dtypes=(F32,)):
    nblk, K, n = b.shape
    M = a.shape[1] if a3 else a.shape[0]
    tm = min(M, 1024)
    assert M % tm == 0
    nex = len(extras)

    def body(a_ref, b_ref, *rest):
        ex, outs, acc = rest[:nex], rest[nex:-1], rest[-1]
        j = pl.program_id(1)

        @pl.when(j == 0)
        def _():
            acc[...] = jnp.zeros_like(acc)
        acc[...] += _dot_nt(a_ref[...], b_ref[...])

        @pl.when(j == nblk - 1)
        def _():
            res = epi(acc[...], *[e[...] for e in ex]) if epi is not None else (acc[...],)
            for o, r in zip(outs, res):
                o[...] = r.astype(o.dtype)

    a_spec = pl.BlockSpec((None, tm, n), lambda i, j: (j, i, 0)) if a3 else pl.BlockSpec((tm, n), lambda i, j: (i, j))
    row = pl.BlockSpec((tm, K), lambda i, j: (i, 0))
    return pl.pallas_call(
        body, grid=(M // tm, nblk),
        in_specs=[a_spec, pl.BlockSpec((None, K, n), lambda i, j: (j, 0, 0))] + [row] * nex,
        out_specs=[row] * len(out_dtypes), out_shape=[jax.ShapeDtypeStruct((M, K), dt) for dt in out_dtypes],
        scratch_shapes=[pltpu.VMEM((tm, K), F32)], name=name,
        compiler_params=_cparams(("arbitrary", "arbitrary")))(a, b, *extras)


def wgrad(name, a, g, *, cb=False, g3=False):
    M, K = a.shape
    tm, tk = min(M, 1024), min(K, 1024)
    if cb:
        n = g.shape[2] if g3 else g.shape[1] // N_DEV
        grid = (K // tk, N_DEV, M // tm)
        g_spec = (pl.BlockSpec((None, tm, n), lambda k, j, m: (j, m, 0)) if g3
                  else pl.BlockSpec((tm, n), lambda k, j, m: (m, j)))
        o_spec = pl.BlockSpec((None, tk, n), lambda k, j, m: (j, k, 0))
        o_shape = (N_DEV, K, n)
    else:
        N = g.shape[1]
        tn = min(N, 512)
        grid = (K // tk, N // tn, M // tm)
        g_spec = pl.BlockSpec((tm, tn), lambda k, j, m: (m, j))
        o_spec = pl.BlockSpec((tk, tn), lambda k, j, m: (k, j))
        o_shape = (K, N)
    nm = M // tm

    def body(a_ref, g_ref, o_ref, o16_ref):
        m = pl.program_id(2)

        @pl.when(m == 0)
        def _():
            o_ref[...] = jnp.zeros_like(o_ref)
        o_ref[...] += _dot_tn(a_ref[...], g_ref[...])

        @pl.when(m == nm - 1)
        def _():
            o16_ref[...] = o_ref[...].astype(BF16)

    return pl.pallas_call(
        body, grid=grid, in_specs=[pl.BlockSpec((tm, tk), lambda k, j, m: (m, k)), g_spec],
        out_specs=[o_spec, o_spec],
        out_shape=[jax.ShapeDtypeStruct(o_shape, F32), jax.ShapeDtypeStruct(o_shape, BF16)], name=name,
        compiler_params=_cparams(("arbitrary", "arbitrary", "arbitrary")))(a, g)


def _rms(x, g):
    x = x.astype(F32)
    return x * lax.rsqrt(jnp.mean(x * x, axis=-1, keepdims=True) + EPS) * g


def _ln(x, g, b):
    mu = jnp.mean(x, axis=-1, keepdims=True)
    var = jnp.mean(jnp.square(x - mu), axis=-1, keepdims=True)
    return (x - mu) * lax.rsqrt(var + EPS) * g + b


def _glu(z):
    d = z.shape[1] // 2
    return z[:, :d] * jax.nn.sigmoid(z[:, d:])


def _glu_bwd(z, dy):
    d = z.shape[1] // 2
    a, s = z[:, :d], jax.nn.sigmoid(z[:, d:])
    return jnp.concatenate([dy * s, dy * a * s * (1.0 - s)], axis=1)


def _colsum(v):
    return jnp.sum(v.astype(F32), axis=0, keepdims=True)


def rms_fwd(name, x, g, want_f32=False):
    def fn(xt, gt):
        h = _rms(xt, gt)
        return (h, h) if want_f32 else (h,)
    D = x.shape[1]
    outs = [(D, BF16)] + ([(D, F32)] if want_f32 else [])
    return rowwise(name, fn, [x], [g], outs)


def rms_bwd(name, x, dh, dres, g):
    def fn(xt, dht, drt, gt):
        _, vjp = jax.vjp(_rms, xt, gt)
        dx, dg = vjp(dht.astype(F32))
        dx = dx + drt
        return dx, dx, dg
    D = x.shape[1]
    return rowwise(name, fn, [x, dh, dres], [g], [(D, F32), (D, BF16)], [((1, D), F32)])


def s5_disc(a_re, a_im, log_dt, b_re, b_im):
    dt = jnp.exp(log_dt)[:, None]
    er = jnp.exp(a_re * dt)
    lam_re = er * jnp.cos(a_im * dt)
    lam_im = er * jnp.sin(a_im * dt)
    nr, ni = lam_re - 1.0, lam_im
    den = a_re * a_re + a_im * a_im
    f_re = (nr * a_re + ni * a_im) / den
    f_im = (ni * a_re - nr * a_im) / den
    bb_re = f_re[..., None] * b_re - f_im[..., None] * b_im
    bb_im = f_re[..., None] * b_im + f_im[..., None] * b_re
    return lam_re, lam_im, bb_re, bb_im


def _s5_blockdiag_b(bb):
    t = bb.reshape(S5_NGB, S5_GB, SSM_STATE, SSM_GROUP).transpose(0, 1, 3, 2)
    eye = jnp.eye(S5_GB, dtype=bb.dtype)
    return jnp.einsum('bgpn,gh->bgphn', t, eye).reshape(S5_NGB, S5_CH, S5_ST)


def _s5_blockdiag_b_inv(x):
    t = x.reshape(S5_NGB, S5_GB, SSM_GROUP, S5_GB, SSM_STATE)
    eye = jnp.eye(S5_GB, dtype=x.dtype)
    d = jnp.einsum('bgphn,gh->bgpn', t, eye)
    return d.transpose(0, 1, 3, 2).reshape(SSM_GROUPS, SSM_STATE, SSM_GROUP)


def _s5_blockdiag_c(c):
    t = c.reshape(S5_NGB, S5_GB, SSM_GROUP, SSM_STATE).transpose(0, 1, 3, 2)
    eye = jnp.eye(S5_GB, dtype=c.dtype)
    return jnp.einsum('bgnp,gh->bgnhp', t, eye).reshape(S5_NGB, S5_ST, S5_CH)


def _s5_blockdiag_c_inv(x):
    t = x.reshape(S5_NGB, S5_GB, SSM_STATE, S5_GB, SSM_GROUP)
    eye = jnp.eye(S5_GB, dtype=x.dtype)
    d = jnp.einsum('bgnhp,gh->bgnp', t, eye)
    return d.transpose(0, 1, 3, 2).reshape(SSM_GROUPS, SSM_GROUP, SSM_STATE)


def s5_tables(lam_re, lam_im, L):
    pr, pi = lam_re.reshape(1, -1), lam_im.reshape(1, -1)
    n = 1
    while n < L:
        lr, li = pr[n - 1:n], pi[n - 1:n]
        pr, pi = (jnp.concatenate([pr, pr * lr - pi * li], 0), jnp.concatenate([pi, pr * li + pi * lr], 0))
        n *= 2
    nk = int(math.log2(L))
    idx = [2 ** k - 1 for k in range(nk)] + [0] * (8 - nk)

    def blk(t):
        return t.reshape(t.shape[0], S5_NGB, S5_ST).transpose(1, 0, 2)

    def rows(t):
        return jnp.concatenate([t[j:j + 1] for j in idx], axis=0)
    return blk(pr), blk(pi), blk(rows(pr)), blk(rows(pi))


def _cscan(br, bi, a2r, a2i, L, reverse):
    row = lax.broadcasted_iota(jnp.int32, br.shape, 0)
    xr, xi = br, bi
    for k in range(int(math.log2(L))):
        s = 1 << k
        ar, ai = a2r[k:k + 1, :], a2i[k:k + 1, :]
        if reverse:
            sr, si = pltpu.roll(xr, L - s, 0), pltpu.roll(xi, L - s, 0)
            m = row < L - s
        else:
            sr, si = pltpu.roll(xr, s, 0), pltpu.roll(xi, s, 0)
            m = row >= s
        sr, si = jnp.where(m, sr, 0.0), jnp.where(m, si, 0.0)
        xr, xi = xr + ar * sr - ai * si, xi + ar * si + ai * sr
    return xr, xi


def s5_fwd(h, bre, bim, cre, cim, pwr, pwi, l2r, l2i, dskip, bsz):
    T, D = h.shape
    L = S5_L
    S = T // bsz
    NC = S // L

    def body(h_ref, bre_ref, bim_ref, cre_ref, cim_ref, pwr_ref, pwi_ref, l2r_ref, l2i_ref, d_ref,
             y_ref, gy_ref, xs_ref, car_r, car_i):
        @pl.when(pl.program_id(2) == 0)
        def _():
            car_r[...] = jnp.zeros_like(car_r)
            car_i[...] = jnp.zeros_like(car_i)
        u = h_ref[...]
        ub = u.astype(BF16)
        cr, ci = car_r[0:1, :], car_i[0:1, :]
        xs_ref[...] = jnp.zeros_like(xs_ref)
        xs_ref[0:1, :] = cr
        xs_ref[1:2, :] = ci
        xr, xi = _cscan(_dot(ub, bre_ref[...]), _dot(ub, bim_ref[...]), l2r_ref[...], l2i_ref[...], L, False)
        pr, pi = pwr_ref[...], pwi_ref[...]
        xr = xr + pr * cr - pi * ci
        xi = xi + pr * ci + pi * cr
        car_r[...] = jnp.broadcast_to(xr[L - 1:L, :], car_r.shape)
        car_i[...] = jnp.broadcast_to(xi[L - 1:L, :], car_i.shape)
        y = _dot(xr.astype(BF16), cre_ref[...]) - _dot(xi.astype(BF16), cim_ref[...]) + d_ref[...] * u
        y_ref[...] = y
        gy_ref[...] = jax.nn.gelu(y).astype(BF16)

    tok = lambda g, b, c: (b * NC + c, g)
    par = lambda g, b, c: (g, 0, 0)
    return pl.pallas_call(
        body, grid=(S5_NGB, bsz, NC),
        in_specs=[pl.BlockSpec((L, S5_CH), tok),
                  pl.BlockSpec((None, S5_CH, S5_ST), par), pl.BlockSpec((None, S5_CH, S5_ST), par),
                  pl.BlockSpec((None, S5_ST, S5_CH), par), pl.BlockSpec((None, S5_ST, S5_CH), par),
                  pl.BlockSpec((None, L, S5_ST), par), pl.BlockSpec((None, L, S5_ST), par),
                  pl.BlockSpec((None, 8, S5_ST), par), pl.BlockSpec((None, 8, S5_ST), par),
                  pl.BlockSpec((1, S5_CH), lambda g, b, c: (0, g))],
        out_specs=[pl.BlockSpec((L, S5_CH), tok), pl.BlockSpec((L, S5_CH), tok),
                   pl.BlockSpec((None, 8, S5_ST), lambda g, b, c: (b * NC + c, 0, g))],
        out_shape=[jax.ShapeDtypeStruct((T, D), F32), jax.ShapeDtypeStruct((T, D), BF16),
                   jax.ShapeDtypeStruct((bsz * NC, 8, S5_NGB * S5_ST), F32)],
        scratch_shapes=[pltpu.VMEM((8, S5_ST), F32), pltpu.VMEM((8, S5_ST), F32)],
        name="s5_fwd", compiler_params=_cparams(("arbitrary", "arbitrary", "arbitrary")),
    )(h, bre, bim, cre, cim, pwr, pwi, l2r, l2i, dskip)


def s5_bwd(h, dy, xs, bre, bim, cre, cim, pwr, pwi, pwr_rev, pwi_rev, l2r, l2i, dskip, bsz):
    T, D = h.shape
    L = S5_L
    S = T // bsz
    NC = S // L

    def body(h_ref, dy_ref, xs_ref, bre_ref, bim_ref, cre_ref, cim_ref, pwr_ref, pwi_ref, qr_ref, qi_ref,
             l2r_ref, l2i_ref, d_ref, du_ref, dbr_ref, dbi_ref, dcr_ref, dci_ref, dl_ref, dd_ref, car_r, car_i):
        first = (pl.program_id(1) == 0) & (pl.program_id(2) == 0)

        @pl.when(first)
        def _():
            for r in (dbr_ref, dbi_ref, dcr_ref, dci_ref, dl_ref, dd_ref):
                r[...] = jnp.zeros_like(r)

        @pl.when(pl.program_id(2) == 0)
        def _():
            car_r[...] = jnp.zeros_like(car_r)
            car_i[...] = jnp.zeros_like(car_i)

        u = h_ref[...]
        ub = u.astype(BF16)
        dyv = dy_ref[...]
        dyb = dyv.astype(BF16)
        l2r_v, l2i_v = l2r_ref[...], l2i_ref[...]
        x0r, x0i = xs_ref[0:1, :], xs_ref[1:2, :]
        xr, xi = _cscan(_dot(ub, bre_ref[...]), _dot(ub, bim_ref[...]), l2r_v, l2i_v, L, False)
        pr, pi = pwr_ref[...], pwi_ref[...]
        xr = xr + pr * x0r - pi * x0i
        xi = xi + pr * x0i + pi * x0r
        gr = _dot_nt(dyb, cre_ref[...])
        gi = -_dot_nt(dyb, cim_ref[...])
        dr, di = _cscan(gr, gi, l2r_v, -l2i_v, L, True)
        qr, qi = qr_ref[...], qi_ref[...]
        cr, ci = car_r[0:1, :], car_i[0:1, :]
        dr = dr + qr * cr + qi * ci
        di = di + qr * ci - qi * cr
        car_r[...] = jnp.broadcast_to(dr[0:1, :], car_r.shape)
        car_i[...] = jnp.broadcast_to(di[0:1, :], car_i.shape)
        row = lax.broadcasted_iota(jnp.int32, xr.shape, 0)
        xpr = jnp.where(row >= 1, pltpu.roll(xr, 1, 0), x0r)
        xpi = jnp.where(row >= 1, pltpu.roll(xi, 1, 0), x0i)
        dl_ref[0:1, :] += _colsum(dr * xpr + di * xpi)
        dl_ref[1:2, :] += _colsum(di * xpr - dr * xpi)
        drb, dib = dr.astype(BF16), di.astype(BF16)
        dcr_ref[...] += _dot_tn(xr.astype(BF16), dyb)
        dci_ref[...] -= _dot_tn(xi.astype(BF16), dyb)
        dbr_ref[...] += _dot_tn(ub, drb)
        dbi_ref[...] += _dot_tn(ub, dib)
        du_ref[...] = _dot_nt(drb, bre_ref[...]) + _dot_nt(dib, bim_ref[...]) + d_ref[...] * dyv
        dd_ref[0:1, :] += _colsum(dyv * u)

    tok = lambda g, b, c: (b * NC + (NC - 1 - c), g)
    par = lambda g, b, c: (g, 0, 0)
    return pl.pallas_call(
        body, grid=(S5_NGB, bsz, NC),
        in_specs=[pl.BlockSpec((L, S5_CH), tok), pl.BlockSpec((L, S5_CH), tok),
                  pl.BlockSpec((None, 8, S5_ST), lambda g, b, c: (b * NC + (NC - 1 - c), 0, g)),
                  pl.BlockSpec((None, S5_CH, S5_ST), par), pl.BlockSpec((None, S5_CH, S5_ST), par),
                  pl.BlockSpec((None, S5_ST, S5_CH), par), pl.BlockSpec((None, S5_ST, S5_CH), par),
                  pl.BlockSpec((None, L, S5_ST), par), pl.BlockSpec((None, L, S5_ST), par),
                  pl.BlockSpec((None, L, S5_ST), par), pl.BlockSpec((None, L, S5_ST), par),
                  pl.BlockSpec((None, 8, S5_ST), par), pl.BlockSpec((None, 8, S5_ST), par),
                  pl.BlockSpec((1, S5_CH), lambda g, b, c: (0, g))],
        out_specs=[pl.BlockSpec((L, S5_CH), tok),
                   pl.BlockSpec((None, S5_CH, S5_ST), par), pl.BlockSpec((None, S5_CH, S5_ST), par),
                   pl.BlockSpec((None, S5_ST, S5_CH), par), pl.BlockSpec((None, S5_ST, S5_CH), par),
                   pl.BlockSpec((None, 8, S5_ST), par),
                   pl.BlockSpec((8, S5_CH), lambda g, b, c: (0, g))],
        out_shape=[jax.ShapeDtypeStruct((T, D), F32),
                   jax.ShapeDtypeStruct((S5_NGB, S5_CH, S5_ST), F32), jax.ShapeDtypeStruct((S5_NGB, S5_CH, S5_ST), F32),
                   jax.ShapeDtypeStruct((S5_NGB, S5_ST, S5_CH), F32), jax.ShapeDtypeStruct((S5_NGB, S5_ST, S5_CH), F32),
                   jax.ShapeDtypeStruct((S5_NGB, 8, S5_ST), F32), jax.ShapeDtypeStruct((8, D), F32)],
        scratch_shapes=[pltpu.VMEM((8, S5_ST), F32), pltpu.VMEM((8, S5_ST), F32)],
        name="s5_bwd", compiler_params=_cparams(("arbitrary", "arbitrary", "arbitrary")),
    )(h, dy, xs, bre, bim, cre, cim, pwr, pwi, pwr_rev, pwi_rev, l2r, l2i, dskip)


def _shift_rows(win, off, n):
    if off == 0:
        return win[:n]
    return pltpu.roll(win, win.shape[0] - off, 0)[:n]


def dwconv_fwd(z, w, b, bsz):
    T, D = z.shape
    S = T // bsz
    TS, CW, PAD = CONV_TS, CONV_CW, CONV_PAD

    def body(z_ref, w_ref, b_ref, y_ref, zp):
        zp[0:PAD, :] = jnp.zeros((PAD, CW), F32)
        zp[PAD:, :] = z_ref[...]
        wv, bv = w_ref[...], b_ref[...]

        def step(t, carry):
            base = pl.multiple_of(t * TS, TS)
            win = zp[pl.ds(base, TS + PAD), :]
            acc = jnp.zeros((TS, CW), F32) + bv
            for k in range(CONV_WIDTH):
                acc = acc + wv[k:k + 1, :] * _shift_rows(win, PAD - (CONV_WIDTH - 1) + k, TS)
            y_ref[pl.ds(base, TS), :] = acc
            return carry
        lax.fori_loop(0, S // TS, step, 0)

    return pl.pallas_call(
        body, grid=(D // CW, bsz),
        in_specs=[pl.BlockSpec((S, CW), lambda c, bb: (bb, c)), pl.BlockSpec((32, CW), lambda c, bb: (0, c)),
                  pl.BlockSpec((1, CW), lambda c, bb: (0, c))],
        out_specs=pl.BlockSpec((S, CW), lambda c, bb: (bb, c)),
        out_shape=jax.ShapeDtypeStruct((T, D), F32),
        scratch_shapes=[pltpu.VMEM((S + PAD, CW), F32)],
        name="dwconv_fwd", compiler_params=_cparams(("arbitrary", "arbitrary")),
    )(z, w, b)


def dwconv_bwd(z, dy, w, bsz):
    T, D = z.shape
    S = T // bsz
    TS, CW, PAD = CONV_TS, CONV_CW, CONV_PAD

    def body(z_ref, dy_ref, w_ref, dz_ref, dw_ref, db_ref, zp, dyp):
        @pl.when(pl.program_id(1) == 0)
        def _():
            dw_ref[...] = jnp.zeros_like(dw_ref)
            db_ref[...] = jnp.zeros_like(db_ref)
        zp[0:PAD, :] = jnp.zeros((PAD, CW), F32)
        zp[PAD:, :] = z_ref[...]
        dyp[0:S, :] = dy_ref[...]
        dyp[S:, :] = jnp.zeros((PAD, CW), F32)
        wv = w_ref[...]

        def step(t, carry):
            base = pl.multiple_of(t * TS, TS)
            zwin = zp[pl.ds(base, TS + PAD), :]
            dwin = dyp[pl.ds(base, TS + PAD), :]
            dyt = dwin[:TS]
            acc = jnp.zeros((TS, CW), F32)
            for j in range(CONV_WIDTH):
                k = CONV_WIDTH - 1 - j
                acc = acc + wv[k:k + 1, :] * _shift_rows(dwin, j, TS)
            dz_ref[pl.ds(base, TS), :] = acc
            for k in range(CONV_WIDTH):
                dw_ref[k:k + 1, :] += _colsum(dyt * _shift_rows(zwin, PAD - (CONV_WIDTH - 1) + k, TS))
            db_ref[0:1, :] += _colsum(dyt)
            return carry
        lax.fori_loop(0, S // TS, step, 0)

    return pl.pallas_call(
        body, grid=(D // CW, bsz),
        in_specs=[pl.BlockSpec((S, CW), lambda c, bb: (bb, c)), pl.BlockSpec((S, CW), lambda c, bb: (bb, c)),
                  pl.BlockSpec((32, CW), lambda c, bb: (0, c))],
        out_specs=[pl.BlockSpec((S, CW), lambda c, bb: (bb, c)), pl.BlockSpec((32, CW), lambda c, bb: (0, c)),
                   pl.BlockSpec((8, CW), lambda c, bb: (0, c))],
        out_shape=[jax.ShapeDtypeStruct((T, D), F32), jax.ShapeDtypeStruct((32, D), F32),
                   jax.ShapeDtypeStruct((8, D), F32)],
        scratch_shapes=[pltpu.VMEM((S + PAD, CW), F32), pltpu.VMEM((S + PAD, CW), F32)],
        name="dwconv_bwd", compiler_params=_cparams(("arbitrary", "arbitrary")),
    )(z, dy, w)


def spatial_fwd(u, vln, ws, bias):
    T, E = u.shape
    C, H = GMLP_CHUNK, GMLP_HEADS
    hw = E // H

    def body(u_ref, v_ref, ws_ref, b_ref, o_ref):
        for hh in range(H):
            sl = slice(hh * hw, (hh + 1) * hw)
            vp = _dot(ws_ref[hh], v_ref[:, sl]) + b_ref[:, sl]
            o_ref[:, sl] = (u_ref[:, sl] * vp).astype(o_ref.dtype)

    return pl.pallas_call(
        body, grid=(T // C,),
        in_specs=[pl.BlockSpec((C, E), lambda i: (i, 0)), pl.BlockSpec((C, E), lambda i: (i, 0)),
                  pl.BlockSpec((H, C, C), lambda i: (0, 0, 0)), pl.BlockSpec((C, E), lambda i: (0, 0))],
        out_specs=pl.BlockSpec((C, E), lambda i: (i, 0)),
        out_shape=jax.ShapeDtypeStruct((T, E), BF16),
        name="spatial_fwd", compiler_params=_cparams(("arbitrary",)),
    )(u, vln, ws, bias)


def spatial_bwd(u, vln, dg, ws, bias):
    T, E = u.shape
    C, H = GMLP_CHUNK, GMLP_HEADS
    hw = E // H

    def body(u_ref, v_ref, dg_ref, ws_ref, b_ref, du_ref, dv_ref, dws_ref, db_ref):
        @pl.when(pl.program_id(0) == 0)
        def _():
            dws_ref[...] = jnp.zeros_like(dws_ref)
            db_ref[...] = jnp.zeros_like(db_ref)
        tril = (lax.broadcasted_iota(jnp.int32, (C, C), 1) <= lax.broadcasted_iota(jnp.int32, (C, C), 0))
        for hh in range(H):
            sl = slice(hh * hw, (hh + 1) * hw)
            v = v_ref[:, sl]
            w = ws_ref[hh]
            dgv = dg_ref[:, sl].astype(F32)
            vp = _dot(w, v) + b_ref[:, sl]
            du_ref[:, sl] = dgv * vp
            dvp = dgv * u_ref[:, sl]
            dvpb = dvp.astype(BF16)
            dv_ref[:, sl] = _dot_tn(w, dvpb)
            dws_ref[hh] += jnp.where(tril, _dot_nt(dvpb, v), 0.0)
            db_ref[:, sl] += dvp

    return pl.pallas_call(
        body, grid=(T // C,),
        in_specs=[pl.BlockSpec((C, E), lambda i: (i, 0)), pl.BlockSpec((C, E), lambda i: (i, 0)),
                  pl.BlockSpec((C, E), lambda i: (i, 0)),
                  pl.BlockSpec((H, C, C), lambda i: (0, 0, 0)), pl.BlockSpec((C, E), lambda i: (0, 0))],
        out_specs=[pl.BlockSpec((C, E), lambda i: (i, 0)), pl.BlockSpec((C, E), lambda i: (i, 0)),
                   pl.BlockSpec((H, C, C), lambda i: (0, 0, 0)), pl.BlockSpec((C, E), lambda i: (0, 0))],
        out_shape=[jax.ShapeDtypeStruct((T, E), F32), jax.ShapeDtypeStruct((T, E), F32),
                   jax.ShapeDtypeStruct((H, C, C), F32), jax.ShapeDtypeStruct((C, E), F32)],
        name="spatial_bwd", compiler_params=_cparams(("arbitrary",)),
    )(u, vln, dg, ws, bias)


def _att_masks():
    r = lax.broadcasted_iota(jnp.int32, (ATT_BLK, ATT_BLK), 0)
    c = lax.broadcasted_iota(jnp.int32, (ATT_BLK, ATT_BLK), 1)
    return c <= r, c >= r


NEG = -1e30
ATT_SCALE = HEAD_DIM ** -0.5


def attn_fwd(name, q, k, v, nb):
    NB = q.shape[0]
    TB = ATT_TB
    assert NB % TB == 0 and (nb % TB == 0 or TB % nb == 0)

    def body(q_ref, k_ref, v_ref, kp_ref, vp_ref, o_ref, l_ref):
        i = pl.program_id(0)
        mc, mp = _att_masks()
        for j in range(TB):
            qj, kc, vc = q_ref[j], k_ref[j], v_ref[j]
            kp = k_ref[j - 1] if j > 0 else kp_ref[0]
            vp = v_ref[j - 1] if j > 0 else vp_ref[0]
            hp = ((i * TB + j) % nb) != 0
            sc = jnp.where(mc, _dot_nt(qj, kc) * ATT_SCALE, NEG)
            sp = jnp.where(mp & hp, _dot_nt(qj, kp) * ATT_SCALE, NEG)
            m = jnp.maximum(jnp.max(sc, axis=1, keepdims=True), jnp.max(sp, axis=1, keepdims=True))
            pc, pp = jnp.exp(sc - m), jnp.exp(sp - m)
            l = jnp.sum(pc, axis=1, keepdims=True) + jnp.sum(pp, axis=1, keepdims=True)
            o = (_dot(pc.astype(BF16), vc) + _dot(pp.astype(BF16), vp)) / l
            o_ref[j] = o
            l_ref[j] = jnp.broadcast_to(m + jnp.log(l), (ATT_BLK, HEAD_DIM))

    blk = pl.BlockSpec((TB, ATT_BLK, HEAD_DIM), lambda i: (i, 0, 0))
    prev = pl.BlockSpec((1, ATT_BLK, HEAD_DIM), lambda i: (jnp.maximum(i * TB - 1, 0), 0, 0))
    return pl.pallas_call(
        body, grid=(NB // TB,), in_specs=[blk, blk, blk, prev, prev], out_specs=[blk, blk],
        out_shape=[jax.ShapeDtypeStruct(q.shape, F32), jax.ShapeDtypeStruct(q.shape, F32)],
        name=name, compiler_params=_cparams(("arbitrary",)),
    )(q, k, v, k, v)


def attn_bwd(name, q, k, v, do, mg, lse, nb):
    NB = q.shape[0]
    TB = ATT_TB

    def body(q_ref, k_ref, v_ref, do_ref, mg_ref, l_ref, kp_ref, vp_ref, qn_ref, don_ref, mgn_ref, ln_ref,
             dq_ref, dk_ref, dv_ref):
        i = pl.program_id(0)
        mc, mp = _att_masks()

        def probs(qj, kk, lse_col, mask):
            s = _dot_nt(qj, kk) * ATT_SCALE
            return jnp.where(mask, jnp.exp(s - lse_col), 0.0)

        def ds_of(p, doj, vv, delta):
            return (p * (_dot_nt(doj, vv) - delta) * ATT_SCALE).astype(BF16)

        dk = [None] * TB
        dv = [None] * TB
        for j in range(TB + 1):
            if j < TB:
                qj, doj, mgj, lj = q_ref[j], do_ref[j], mg_ref[j], l_ref[j]
                hp = ((i * TB + j) % nb) != 0
            else:
                qj, doj, mgj, lj = qn_ref[0], don_ref[0], mgn_ref[0], ln_ref[0]
                hp = (((i + 1) * TB) % nb != 0) & ((i + 1) * TB < NB)
            lse_col = lj[:, 0:1]
            delta = jnp.sum(doj.astype(F32) * mgj.astype(F32), axis=1, keepdims=True)
            if j > 0:
                kp, vp = k_ref[j - 1], v_ref[j - 1]
            else:
                kp, vp = kp_ref[0], vp_ref[0]
            pp = probs(qj, kp, lse_col, mp & hp)
            dsp = ds_of(pp, doj, vp, delta)
            if j > 0:
                dk[j - 1] = dk[j - 1] + _dot_tn(dsp, qj)
                dv[j - 1] = dv[j - 1] + _dot_tn(pp.astype(BF16), doj)
            if j < TB:
                kc, vc = k_ref[j], v_ref[j]
                pc = probs(qj, kc, lse_col, mc)
                dsc = ds_of(pc, doj, vc, delta)
                dq_ref[j] = (_dot(dsc, kc) + _dot(dsp, kp)).astype(dq_ref.dtype)
                dk[j] = _dot_tn(dsc, qj)
                dv[j] = _dot_tn(pc.astype(BF16), doj)
        for j in range(TB):
            dk_ref[j] = dk[j].astype(dk_ref.dtype)
            dv_ref[j] = dv[j].astype(dv_ref.dtype)

    blk = pl.BlockSpec((TB, ATT_BLK, HEAD_DIM), lambda i: (i, 0, 0))
    prev = pl.BlockSpec((1, ATT_BLK, HEAD_DIM), lambda i: (jnp.maximum(i * TB - 1, 0), 0, 0))
    nxt = pl.BlockSpec((1, ATT_BLK, HEAD_DIM), lambda i: (jnp.minimum((i + 1) * TB, NB - 1), 0, 0))
    return pl.pallas_call(
        body, grid=(NB // TB,), in_specs=[blk] * 6 + [prev, prev, nxt, nxt, nxt, nxt], out_specs=[blk, blk, blk],
        out_shape=[jax.ShapeDtypeStruct(q.shape, BF16)] * 3,
        name=name, compiler_params=_cparams(("arbitrary",)),
    )(q, k, v, do, mg, lse, k, v, q, do, mg, lse)


def _to_blocks(t, bsz, dil):
    T = t.shape[0]
    S = T // bsz
    nb = S // (ATT_BLK * dil)
    t = t.reshape(bsz, nb, ATT_BLK, dil, ATT_HEADS, HEAD_DIM).transpose(0, 3, 4, 1, 2, 5)
    return t.reshape(bsz * dil * ATT_HEADS * nb, ATT_BLK, HEAD_DIM)


def _from_blocks(t, bsz, dil):
    nb = t.shape[0] // (bsz * dil * ATT_HEADS)
    t = t.reshape(bsz, dil, ATT_HEADS, nb, ATT_BLK, HEAD_DIM).transpose(0, 3, 4, 1, 2, 5)
    return t.reshape(bsz * nb * ATT_BLK * dil, ATT_HEADS * HEAD_DIM)


QKV_SLOTS = 3 * len(ATT_CONFIGS) * ATT_HEADS
SLOTS_PER_DEV = QKV_SLOTS // N_DEV


def _head_slots(t3):
    return [t3[s // SLOTS_PER_DEV][:, (s % SLOTS_PER_DEV) * HEAD_DIM:(s % SLOTS_PER_DEV + 1) * HEAD_DIM]
            for s in range(QKV_SLOTS)]


def _heads_of(slots, k):
    return jnp.concatenate(slots[k * ATT_HEADS:(k + 1) * ATT_HEADS], axis=1)


def _slots_to_blocked(slots):
    return jnp.stack([jnp.concatenate(slots[b * SLOTS_PER_DEV:(b + 1) * SLOTS_PER_DEV], axis=1)
                      for b in range(N_DEV)])


def _coords():
    return lax.axis_index("x"), lax.axis_index("y"), lax.axis_index("c")


def all_gather(name, xs):
    n = len(xs)

    def body(*refs):
        x_refs, out_refs = refs[:n], refs[n:2 * n]
        send_sems, recv_sems, local_sems = refs[2 * n:]
        x, y, c = _coords()
        me, sibling = (x, y, c), (x, y, 1 - c)
        chips = [(1 - x, y), (x, 1 - y), (1 - x, 1 - y)]

        def slot(a, px, py, pc):
            return out_refs[a].at[4 * px + 2 * py + pc]

        def copy(a, k, block, to, src=None):
            return pltpu.make_async_remote_copy(
                src_ref=slot(a, *block) if src is None else src, dst_ref=slot(a, *block),
                send_sem=send_sems.at[7 * a + k], recv_sem=recv_sems.at[7 * a + k],
                device_id=to, device_id_type=MESH)

        mine = [pltpu.make_async_copy(x_refs[a], slot(a, *me), local_sems.at[a]) for a in range(n)]
        for cp in mine:
            cp.start()
        first = []
        for a in range(n):
            first.append(copy(a, 0, me, sibling, src=x_refs[a]))
            first += [copy(a, 1 + j, me, (*chip, c), src=x_refs[a]) for j, chip in enumerate(chips)]
        for cp in first:
            cp.start()
        passed = []
        for j, chip in enumerate(chips):
            for a in range(n):
                copy(a, 1 + j, (*chip, c), me).wait_recv()
                cp = copy(a, 4 + j, (*chip, c), sibling)
                cp.start()
                passed.append(cp)
        for a in range(n):
            copy(a, 0, sibling, me).wait_recv()
            for j, chip in enumerate(chips):
                copy(a, 4 + j, (*chip, 1 - c), me).wait_recv()
        for cp in first + passed:
            cp.wait_send()
        for cp in mine:
            cp.wait()

    anyspec = pl.BlockSpec(memory_space=pl.ANY)
    return pl.pallas_call(
        body, out_shape=[jax.ShapeDtypeStruct((N_DEV,) + t.shape, t.dtype) for t in xs],
        in_specs=[anyspec] * n, out_specs=[anyspec] * n,
        scratch_shapes=[pltpu.SemaphoreType.DMA((7 * n,)), pltpu.SemaphoreType.DMA((7 * n,)),
                        pltpu.SemaphoreType.DMA((n,))],
        name=name,
    )(*xs)


def exchange_sibling(name, gs):
    n = len(gs)

    def body(*refs):
        g_refs, out_refs = refs[:n], refs[n:2 * n]
        send_sems, recv_sems = refs[2 * n:]
        x, y, c = _coords()
        sibling = (x, y, 1 - c)
        cps = []
        for a in range(n):
            for q in range(4):
                cps.append(pltpu.make_async_remote_copy(
                    src_ref=g_refs[a].at[2 * q + (1 - c)], dst_ref=out_refs[a].at[q],
                    send_sem=send_sems.at[4 * a + q], recv_sem=recv_sems.at[4 * a + q],
                    device_id=sibling, device_id_type=MESH))
        for cp in cps:
            cp.start()
        for cp in cps:
            cp.wait_recv()
        for cp in cps:
            cp.wait_send()

    anyspec = pl.BlockSpec(memory_space=pl.ANY)
    return pl.pallas_call(
        body, out_shape=[jax.ShapeDtypeStruct((4,) + g.shape[1:], g.dtype) for g in gs],
        in_specs=[anyspec] * n, out_specs=[anyspec] * n,
        scratch_shapes=[pltpu.SemaphoreType.DMA((4 * n,)), pltpu.SemaphoreType.DMA((4 * n,))],
        name=name,
    )(*gs)


def exchange_chips(name, ps):
    n = len(ps)

    def body(*refs):
        p_refs, out_refs = refs[:n], refs[n:2 * n]
        send_sems, recv_sems = refs[2 * n:]
        x, y, c = _coords()
        chips = [(1 - x, y), (x, 1 - y), (1 - x, 1 - y)]
        cps = []
        for a in range(n):
            for k, (px, py) in enumerate(chips):
                cps.append(pltpu.make_async_remote_copy(
                    src_ref=p_refs[a].at[2 * px + py], dst_ref=out_refs[a].at[k],
                    send_sem=send_sems.at[3 * a + k], recv_sem=recv_sems.at[3 * a + k],
                    device_id=(px, py, c), device_id_type=MESH))
        for cp in cps:
            cp.start()
        for cp in cps:
            cp.wait_recv()
        for cp in cps:
            cp.wait_send()

    anyspec = pl.BlockSpec(memory_space=pl.ANY)
    return pl.pallas_call(
        body, out_shape=[jax.ShapeDtypeStruct((3,) + p.shape[1:], p.dtype) for p in ps],
        in_specs=[anyspec] * n, out_specs=[anyspec] * n,
        scratch_shapes=[pltpu.SemaphoreType.DMA((3 * n,)), pltpu.SemaphoreType.DMA((3 * n,))],
        name=name,
    )(*ps)


def _row_tile(R):
    tr = 256
    while R % tr:
        tr //= 2
    assert tr % 8 == 0
    return tr


def add_sibling(name, g, recv, c_idx):
    _, R, C = g.shape
    tr = _row_tile(R)

    def body(c_ref, g_ref, r_ref, o_ref, o16_ref):
        s = g_ref[...] + r_ref[...].astype(F32)
        o_ref[...] = s
        o16_ref[...] = s.astype(BF16)

    out = pl.BlockSpec((None, tr, C), lambda q, i, cr: (q, i, 0))
    return pl.pallas_call(
        body,
        grid_spec=pltpu.PrefetchScalarGridSpec(
            num_scalar_prefetch=1, grid=(4, R // tr),
            in_specs=[pl.BlockSpec((None, tr, C), lambda q, i, cr: (2 * q + cr[0], i, 0)), out],
            out_specs=[out, out]),
        out_shape=[jax.ShapeDtypeStruct((4, R, C), F32), jax.ShapeDtypeStruct((4, R, C), BF16)], name=name,
        compiler_params=_cparams(("arbitrary", "arbitrary")),
    )(c_idx, g, recv)


def _adam_math(w, g, m, v):
    m = ADAM_B1 * m + (1.0 - ADAM_B1) * g
    v = ADAM_B2 * v + (1.0 - ADAM_B2) * jnp.square(g)
    m_hat = m / (1.0 - ADAM_B1 ** ADAM_STEP)
    v_hat = v / (1.0 - ADAM_B2 ** ADAM_STEP)
    delta = -ADAM_LR * (m_hat / (jnp.sqrt(v_hat) + ADAM_EPS) + ADAM_WD * w)
    return delta, m, v


def adam_big(name, p1, recv, w, m, v, chip_idx, layer=0):
    _, R, C = p1.shape
    tr = _row_tile(R)
    nt = R // tr

    def body(q_ref, p_ref, r_ref, w_ref, m_ref, v_ref, g_ref, d_ref, nm_ref, nv_ref):
        g = ((p_ref[...] + r_ref[0].astype(F32)) + r_ref[1].astype(F32)) + r_ref[2].astype(F32)
        d, nm, nv = _adam_math(w_ref[...], g, m_ref[...], v_ref[...])
        g_ref[...] = g
        d_ref[...] = d
        nm_ref[...] = nm
        nv_ref[...] = nv

    row_in = pl.BlockSpec((tr, C), lambda i, qr: (layer * nt + i, 0))
    row = pl.BlockSpec((tr, C), lambda i, qr: (i, 0))
    return pl.pallas_call(
        body,
        grid_spec=pltpu.PrefetchScalarGridSpec(
            num_scalar_prefetch=1, grid=(nt,),
            in_specs=[pl.BlockSpec((None, tr, C), lambda i, qr: (qr[0], i, 0)),
                      pl.BlockSpec((3, tr, C), lambda i, qr: (0, i, 0)), row_in, row_in, row_in],
            out_specs=[row, row, row, row]),
        out_shape=[jax.ShapeDtypeStruct((R, C), F32)] * 4, name=name,
        compiler_params=_cparams(("arbitrary",)),
    )(chip_idx, p1, recv, w, m, v)


def sum8(parts):
    _, R, C = parts.shape

    def body(p_ref, o_ref):
        acc = p_ref[0]
        for k in range(1, N_DEV):
            acc = acc + p_ref[k]
        o_ref[...] = acc

    tr = 128
    while R % tr:
        tr //= 2
    assert tr % 8 == 0
    return pl.pallas_call(
        body, grid=(R // tr,), in_specs=[pl.BlockSpec((N_DEV, tr, C), lambda i: (0, i, 0))],
        out_specs=pl.BlockSpec((tr, C), lambda i: (i, 0)), out_shape=jax.ShapeDtypeStruct((R, C), F32),
        name="sum8", compiler_params=_cparams(("arbitrary",)),
    )(parts)


def adam_small(w, g, m, v):
    def fn(wt, gt, mt, vt):
        return _adam_math(wt, gt, mt, vt)
    C = w.shape[1]
    return rowwise("adam_small", fn, [w, g, m, v], [], [(C, F32)] * 3, tr=128)


def _pack(arrs, rows_mult=8):
    flat = jnp.concatenate([a.reshape(-1) for a in arrs])
    n = flat.shape[0]
    per = PACK_C * rows_mult
    pad = (-n) % per
    if pad:
        flat = jnp.concatenate([flat, jnp.zeros((pad,), flat.dtype)])
    return flat.reshape(-1, PACK_C)


def _unpack(buf, shapes):
    flat = buf.reshape(-1)
    out, off = [], 0
    for s in shapes:
        n = math.prod(s)
        out.append(flat[off:off + n].reshape(s))
        off += n
    return out


def _blocked(gfull, axis):
    shp = gfull.shape
    n = shp[axis] // N_DEV
    t = gfull.reshape(shp[:axis] + (N_DEV, n) + shp[axis + 1:])
    t = jnp.moveaxis(t, axis, 0)
    return t.reshape(N_DEV, -1)


def _unblocked(gathered, shard_shape, axis):
    t = jnp.moveaxis(gathered, 0, axis)
    shp = shard_shape[:axis] + (N_DEV * shard_shape[axis],) + shard_shape[axis + 1:]
    return t.reshape(shp)


def _relu2_epi(acc):
    r = jnp.maximum(acc, 0.0)
    return acc, r * r


def _step(x3, target3, W, Wfull):
    bsz, S, D = x3.shape
    T = bsz * S
    x = x3.reshape(T, D)
    target = target3.reshape(T, D)
    row = lambda v: v.reshape(1, -1)
    grads = {}

    s5p = (W['ssm_a_re'][0], W['ssm_a_im'][0], W['ssm_log_dt'][0], W['ssm_b_re'][0], W['ssm_b_im'][0])
    (lam_re, lam_im, bb_re, bb_im), s5_disc_vjp = jax.vjp(s5_disc, *s5p)
    pwr, pwi, l2r, l2i = s5_tables(lam_re, lam_im, S5_L)
    bre, bim = _s5_blockdiag_b(bb_re).astype(BF16), _s5_blockdiag_b(bb_im).astype(BF16)
    cre, cim = _s5_blockdiag_c(W['ssm_c_re'][0]).astype(BF16), _s5_blockdiag_c(W['ssm_c_im'][0]).astype(BF16)
    dskip = W['ssm_d']

    tril = jnp.tril(jnp.ones((GMLP_CHUNK, GMLP_CHUNK), bool))
    ws = jnp.where(tril[None], W['gmlp_w_s'][0], 0.0).astype(BF16)
    hw = D // GMLP_HEADS
    sbias = jnp.repeat(W['gmlp_b_s'][0].T, hw, axis=1)

    conv_w = jnp.concatenate([W['conv_w_dw'][0], jnp.zeros((1, D), F32)], axis=0)

    saved = []
    for i in range(DEPTH):
        sv = {'x': x}
        nm = W['norm_mix'][i:i + 1]
        if i == 0:
            h, hf = rms_fwd("rms_mix0", x, nm, want_f32=True)
            ypre, gy, xs = s5_fwd(hf, bre, bim, cre, cim, pwr, pwi, l2r, l2i, dskip, bsz)
            z, = matmul("s5_glu_mm", gy, Wfull['ssm_w_glu'], mode='cb')
            x1, = rowwise("s5_glu", lambda zt, xt: xt + _glu(zt), [z, x], [], [(D, F32)])
            sv.update(hf=hf, ypre=ypre, gy=gy, xs=xs, z=z)
        elif i == 1:
            h, = rms_fwd("rms_mix1", x, nm)
            z, = matmul("conv_pw1", h, Wfull['conv_w_pw1'], mode='cb', epi=lambda acc, b: (acc + b,),
                        extras=[(W['conv_b_pw1'], 'row')])
            zg, = rowwise("conv_glu", _glu, [z], [], [(D, F32)])
            yc = dwconv_fwd(zg, conv_w, W['conv_b_dw'], bsz)
            y2, = rowwise("conv_ln_silu", lambda t, g, b: jax.nn.silu(_ln(t, g, b)), [yc],
                          [W['conv_ln_g'], W['conv_ln_b']], [(D, BF16)])
            x1, = matmul("conv_pw2", y2, Wfull['conv_w_pw2'], epi=lambda acc, b, r: (acc + b + r,),
                         extras=[(W['conv_b_pw2'], 'row'), (x, 'tile')])
            sv.update(h=h, z=z, zg=zg, yc=yc, y2=y2)
        elif i == 2:
            h, = rms_fwd("rms_mix2", x, nm)
            zp, = matmul("gmlp_in", h, Wfull['gmlp_w_in'], mode='cb')

            def gm_pre(zt, g, b):
                a = jax.nn.gelu(zt)
                return a[:, :D], _ln(a[:, D:], g, b)
            u, vln = rowwise("gmlp_pre", gm_pre, [zp], [W['gmlp_ln_g'], W['gmlp_ln_b']], [(D, F32), (D, BF16)])
            gated = spatial_fwd(u, vln, ws, sbias)
            x1, = matmul("gmlp_out", gated, Wfull['gmlp_w_out'], epi=lambda acc, r: (acc + r,), extras=[(x, 'tile')])
            sv.update(h=h, zp=zp, u=u, vln=vln, gated=gated)
        else:
            h, = rms_fwd("rms_mix3", x, nm)
            qkv3, = matmul("attn_qkv", h, Wfull['attn_w_qkv'], mode='cb', out_dtypes=(BF16,), out3=True)
            slots = _head_slots(qkv3)
            ng = len(ATT_CONFIGS)
            outs, lses, blocks = [], [], []
            for gi, (window, dil) in enumerate(ATT_CONFIGS):
                nb = S // (ATT_BLK * dil)
                qb, kb, vb = (_to_blocks(_heads_of(slots, j * ng + gi), bsz, dil) for j in range(3))
                ob, lb = attn_fwd("attn_fwd%d" % gi, qb, kb, vb, nb)
                blocks.append((qb, kb, vb, lb, nb, dil))
                outs.append(_from_blocks(ob, bsz, dil))
                lses.append(_from_blocks(lb, bsz, dil))

            def merge(o0, o1, o2, l0, l1, l2):
                m = jnp.maximum(jnp.maximum(l0, l1), l2)
                e0, e1, e2 = jnp.exp(l0 - m), jnp.exp(l1 - m), jnp.exp(l2 - m)
                inv = 1.0 / (e0 + e1 + e2)
                w0, w1, w2 = e0 * inv, e1 * inv, e2 * inv
                return w0 * o0 + w1 * o1 + w2 * o2, w0, w1, w2
            merged, w0, w1, w2 = rowwise("attn_merge", merge, outs + lses, [],
                                         [(ATT_W, BF16), (ATT_W, F32), (ATT_W, F32), (ATT_W, F32)])
            x1, = matmul("attn_o", merged, Wfull['attn_w_o'], mode='cb', epi=lambda acc, r: (acc + r,),
                         extras=[(x, 'tile')])
            sv.update(h=h, blocks=blocks, merged=merged, wts=(w0, w1, w2))
        h2, = rms_fwd("rms_mlp%d" % i, x1, W['norm_mlp'][i:i + 1])
        a, act = matmul("mlp_in%d" % i, h2, Wfull['mlp_w_in'][i], mode='cb', epi=_relu2_epi, out_dtypes=(BF16, BF16))
        x2, = matmul("mlp_out%d" % i, act, Wfull['mlp_w_out'][i], epi=lambda acc, r: (acc + r,), extras=[(x1, 'tile')])
        sv.update(x1=x1, h2=h2, a=a, act=act)
        saved.append(sv)
        x = x2

    def loss_fn(xt, tt, g):
        y, vjp = jax.vjp(_rms, xt, g)
        err = y - tt
        dxx, dg = vjp(err * (1.0 / D))
        lval = jnp.sum(jnp.sum(err * err, axis=1, keepdims=True), axis=0, keepdims=True) * (0.5 / D)
        return dxx, dxx, jnp.broadcast_to(lval, (1, 128)), dg
    dx, dxb, lacc, dnf = rowwise("loss_head", loss_fn, [x, target], [row(W['norm_final'])],
                                 [(D, F32), (D, BF16)], [((1, 128), F32), ((1, D), F32)])
    loss_local = lacc[0, 0]
    grads['norm_final'] = dnf.reshape(-1)

    g_norm_mix, g_norm_mlp = [None] * DEPTH, [None] * DEPTH
    g_mlp_in, g_mlp_out = [None] * DEPTH, [None] * DEPTH
    for i in reversed(range(DEPTH)):
        sv = saved[i]
        da, = matmul("mlp_out_bwd%d" % i, dxb, Wfull['mlp_w_out'][i], mode='nt',
                     epi=lambda acc, av: (acc * (2.0 * jnp.maximum(av.astype(F32), 0.0)),),
                     extras=[(sv['a'], 'tile')], out_dtypes=(BF16,))
        g_mlp_out[i] = _rows_blocked(wgrad("mlp_out_wg%d" % i, sv['act'], dxb))
        dh2, = matmul_nt_cb("mlp_in_bwd%d" % i, da, Wfull['mlp_w_in'][i])
        g_mlp_in[i] = wgrad("mlp_in_wg%d" % i, sv['h2'], da, cb=True)
        dx, dxb, dg = rms_bwd("rms_mlp_bwd%d" % i, sv['x1'], dh2, dx, W['norm_mlp'][i:i + 1])
        g_norm_mlp[i] = dg.reshape(-1)
        xin = sv['x']
        if i == 0:
            dz, = rowwise("s5_glu_bwd", _glu_bwd, [sv['z'], dx], [], [(2 * D, BF16)])
            dgy, = matmul_nt_cb("s5_glu_mm_bwd", dz, Wfull['ssm_w_glu'])
            grads['ssm_w_glu'] = wgrad("s5_glu_wg", sv['gy'], dz, cb=True)

            def gelu_bwd(yt, dt):
                _, vjp = jax.vjp(jax.nn.gelu, yt)
                return vjp(dt)[0]
            dypre, = rowwise("s5_gelu_bwd", gelu_bwd, [sv['ypre'], dgy], [], [(D, F32)])
            du, dbr, dbi, dcr, dci, dl, dd = s5_bwd(sv['hf'], dypre, sv['xs'], bre, bim, cre, cim, pwr, pwi,
                                                    pwr[:, ::-1], pwi[:, ::-1], l2r, l2i, dskip, bsz)
            dlam_re = dl[:, 0, :].reshape(SSM_GROUPS, SSM_STATE)
            dlam_im = dl[:, 1, :].reshape(SSM_GROUPS, SSM_STATE)
            s5_cot = (dlam_re, dlam_im, _s5_blockdiag_b_inv(dbr), _s5_blockdiag_b_inv(dbi))
            grads['ssm_c_re'] = _s5_blockdiag_c_inv(dcr)[None]
            grads['ssm_c_im'] = _s5_blockdiag_c_inv(dci)[None]
            grads['ssm_d'] = dd[0:1]
            dh = du
        elif i == 1:
            dy2, = matmul("conv_pw2_bwd", dxb, Wfull['conv_w_pw2'], mode='nt')
            grads['conv_w_pw2'] = _rows_blocked(wgrad("conv_pw2_wg", sv['y2'], dxb))

            def ln_silu_bwd(yt, dt, dxt, g, b):
                _, vjp = jax.vjp(lambda t, gg, bb: jax.nn.silu(_ln(t, gg, bb)), yt, g, b)
                dyc, dgg, dbb = vjp(dt)
                return dyc, dgg, dbb, _colsum(dxt)
            dyc, dlg, dlb, dbp2 = rowwise("conv_ln_silu_bwd", ln_silu_bwd, [sv['yc'], dy2, dx],
                                          [W['conv_ln_g'], W['conv_ln_b']], [(D, F32)],
                                          [((1, D), F32), ((1, D), F32), ((1, D), F32)])
            grads['conv_ln_g'], grads['conv_ln_b'], grads['conv_b_pw2'] = dlg, dlb, dbp2
            dzg, dwd, dbd = dwconv_bwd(sv['zg'], dyc, conv_w, bsz)
            grads['conv_w_dw'] = dwd[None, :CONV_WIDTH]
            grads['conv_b_dw'] = dbd[0:1]

            def glu_bwd1(zt, dyt):
                dzt = _glu_bwd(zt, dyt)
                return dzt, _colsum(dzt)
            dz, dbp1 = rowwise("conv_glu_bwd", glu_bwd1, [sv['z'], dzg], [], [(2 * D, BF16)], [((1, 2 * D), F32)])
            grads['conv_b_pw1'] = dbp1
            dh, = matmul_nt_cb("conv_pw1_bwd", dz, Wfull['conv_w_pw1'])
            grads['conv_w_pw1'] = wgrad("conv_pw1_wg", sv['h'], dz, cb=True)
        elif i == 2:
            dgt, = matmul("gmlp_out_bwd", dxb, Wfull['gmlp_w_out'], mode='nt', out_dtypes=(BF16,))
            grads['gmlp_w_out'] = _rows_blocked(wgrad("gmlp_out_wg", sv['gated'], dxb))
            du, dvln, dws, dsb = spatial_bwd(sv['u'], sv['vln'], dgt, ws, sbias)
            grads['gmlp_w_s'] = dws[None]
            grads['gmlp_b_s'] = dsb.reshape(GMLP_CHUNK, GMLP_HEADS, hw).sum(-1).T[None]

            def gm_pre_bwd(zt, dut, dvt, g, b):
                _, vjp_u = jax.vjp(jax.nn.gelu, zt[:, :D])
                _, vjp_v = jax.vjp(lambda zz, gg, bb: _ln(jax.nn.gelu(zz), gg, bb), zt[:, D:], g, b)
                dz2, dgg, dbb = vjp_v(dvt)
                return jnp.concatenate([vjp_u(dut)[0], dz2], axis=1), dgg, dbb
            dzp, dlg, dlb = rowwise("gmlp_pre_bwd", gm_pre_bwd, [sv['zp'], du, dvln],
                                    [W['gmlp_ln_g'], W['gmlp_ln_b']], [(2 * D, BF16)], [((1, D), F32), ((1, D), F32)])
            grads['gmlp_ln_g'], grads['gmlp_ln_b'] = dlg, dlb
            dh, = matmul_nt_cb("gmlp_in_bwd", dzp, Wfull['gmlp_w_in'])
            grads['gmlp_w_in'] = wgrad("gmlp_in_wg", sv['h'], dzp, cb=True)
        else:
            dm, = matmul_nt_cb("attn_o_bwd", dxb, Wfull['attn_w_o'])
            grads['attn_w_o'] = wgrad("attn_o_wg", sv['merged'], dxb, cb=True)
            w0, w1, w2 = sv['wts']
            do0, do1, do2 = rowwise("attn_merge_bwd", lambda d, a, b, c: (a * d, b * d, c * d), [dm, w0, w1, w2], [],
                                    [(ATT_W, BF16)] * 3)
            dparts = [[None] * 3 for _ in range(3)]
            for gi, (dog, (qb, kb, vb, lb, nb, dil)) in enumerate(zip((do0, do1, do2), sv['blocks'])):
                dob = _to_blocks(dog, bsz, dil)
                mgb = _to_blocks(sv['merged'], bsz, dil)
                dqb, dkb, dvb = attn_bwd("attn_bwd%d" % gi, qb, kb, vb, dob, mgb, lb, nb)
                for j, t in enumerate((dqb, dkb, dvb)):
                    dparts[j][gi] = _from_blocks(t, bsz, dil)
            dslots = [dparts[j][gi][:, hh * HEAD_DIM:(hh + 1) * HEAD_DIM]
                      for j in range(3) for gi in range(3) for hh in range(ATT_HEADS)]
            dqkv3 = _slots_to_blocked(dslots)
            dh, = matmul_nt_cb("attn_qkv_bwd", dqkv3, Wfull['attn_w_qkv'], a3=True)
            grads['attn_w_qkv'] = wgrad("attn_qkv_wg", sv['h'], dqkv3, cb=True, g3=True)
        dx, dxb, dg = rms_bwd("rms_mix_bwd%d" % i, xin, dh, dx, W['norm_mix'][i:i + 1])
        g_norm_mix[i] = dg.reshape(-1)

    grads['norm_mix'] = jnp.stack(g_norm_mix)
    grads['norm_mlp'] = jnp.stack(g_norm_mlp)
    grads['mlp_w_in'] = g_mlp_in
    grads['mlp_w_out'] = g_mlp_out
    return loss_local, dx.reshape(bsz, S, D), grads, (s5_disc_vjp, s5_cot)


def _rows_blocked(pair):
    return tuple(t.reshape(N_DEV, t.shape[0] // N_DEV, t.shape[1]) for t in pair)


def kernel(x, norm_mix, norm_mlp, norm_final, ssm_a_re, ssm_a_im, ssm_b_re, ssm_b_im, ssm_c_re, ssm_c_im, ssm_d, ssm_log_dt, ssm_w_glu, conv_w_pw1, conv_b_pw1, conv_w_dw, conv_b_dw, conv_ln_g, conv_ln_b, conv_w_pw2, conv_b_pw2, gmlp_w_in, gmlp_ln_g, gmlp_ln_b, gmlp_w_s, gmlp_b_s, gmlp_w_out, attn_w_qkv, attn_w_o, mlp_w_in, mlp_w_out, loss_target, m_norm_mix, m_norm_mlp, m_norm_final, m_ssm_a_re, m_ssm_a_im, m_ssm_b_re, m_ssm_b_im, m_ssm_c_re, m_ssm_c_im, m_ssm_d, m_ssm_log_dt, m_ssm_w_glu, m_conv_w_pw1, m_conv_b_pw1, m_conv_w_dw, m_conv_b_dw, m_conv_ln_g, m_conv_ln_b, m_conv_w_pw2, m_conv_b_pw2, m_gmlp_w_in, m_gmlp_ln_g, m_gmlp_ln_b, m_gmlp_w_s, m_gmlp_b_s, m_gmlp_w_out, m_attn_w_qkv, m_attn_w_o, m_mlp_w_in, m_mlp_w_out, v_norm_mix, v_norm_mlp, v_norm_final, v_ssm_a_re, v_ssm_a_im, v_ssm_b_re, v_ssm_b_im, v_ssm_c_re, v_ssm_c_im, v_ssm_d, v_ssm_log_dt, v_ssm_w_glu, v_conv_w_pw1, v_conv_b_pw1, v_conv_w_dw, v_conv_b_dw, v_conv_ln_g, v_conv_ln_b, v_conv_w_pw2, v_conv_b_pw2, v_gmlp_w_in, v_gmlp_ln_g, v_gmlp_ln_b, v_gmlp_w_s, v_gmlp_b_s, v_gmlp_w_out, v_attn_w_qkv, v_attn_w_o, v_mlp_w_in, v_mlp_w_out):
    args = (norm_mix, norm_mlp, norm_final, ssm_a_re, ssm_a_im, ssm_b_re, ssm_b_im, ssm_c_re, ssm_c_im, ssm_d,
            ssm_log_dt, ssm_w_glu, conv_w_pw1, conv_b_pw1, conv_w_dw, conv_b_dw, conv_ln_g, conv_ln_b, conv_w_pw2,
            conv_b_pw2, gmlp_w_in, gmlp_ln_g, gmlp_ln_b, gmlp_w_s, gmlp_b_s, gmlp_w_out, attn_w_qkv, attn_w_o,
            mlp_w_in, mlp_w_out)
    margs = (m_norm_mix, m_norm_mlp, m_norm_final, m_ssm_a_re, m_ssm_a_im, m_ssm_b_re, m_ssm_b_im, m_ssm_c_re,
             m_ssm_c_im, m_ssm_d, m_ssm_log_dt, m_ssm_w_glu, m_conv_w_pw1, m_conv_b_pw1, m_conv_w_dw, m_conv_b_dw,
             m_conv_ln_g, m_conv_ln_b, m_conv_w_pw2, m_conv_b_pw2, m_gmlp_w_in, m_gmlp_ln_g, m_gmlp_ln_b,
             m_gmlp_w_s, m_gmlp_b_s, m_gmlp_w_out, m_attn_w_qkv, m_attn_w_o, m_mlp_w_in, m_mlp_w_out)
    vargs = (v_norm_mix, v_norm_mlp, v_norm_final, v_ssm_a_re, v_ssm_a_im, v_ssm_b_re, v_ssm_b_im, v_ssm_c_re,
             v_ssm_c_im, v_ssm_d, v_ssm_log_dt, v_ssm_w_glu, v_conv_w_pw1, v_conv_b_pw1, v_conv_w_dw, v_conv_b_dw,
             v_conv_ln_g, v_conv_ln_b, v_conv_w_pw2, v_conv_b_pw2, v_gmlp_w_in, v_gmlp_ln_g, v_gmlp_ln_b,
             v_gmlp_w_s, v_gmlp_b_s, v_gmlp_w_out, v_attn_w_qkv, v_attn_w_o, v_mlp_w_in, v_mlp_w_out)
    Wl = dict(zip(WEIGHT_NAMES, args))
    Ml = dict(zip(WEIGHT_NAMES, margs))
    Vl = dict(zip(WEIGHT_NAMES, vargs))
    cx, cy, cc = _coords()
    my_idx = 4 * cx + 2 * cy + cc

    c_idx = cc.reshape(1).astype(jnp.int32)
    chip_idx = (2 * cx + cy).reshape(1).astype(jnp.int32)
    units = []
    for n in BIG:
        units += [(n, i) for i in range(DEPTH)] if Wl[n].shape[0] == DEPTH else [(n, None)]
    ss_names = list(SMALL_SHARDED)
    spack = _pack([Wl[n] for n in ss_names])
    gathered = all_gather("gather_weights", [Wl[n][0 if i is None else i].astype(BF16) for n, i in units] + [spack])
    Wfull = {}
    for (n, i), g in zip(units, gathered):
        w = g if BIG[n] == 2 else g.reshape(N_DEV * g.shape[1], g.shape[2])
        if i is None:
            Wfull[n] = w
        else:
            Wfull.setdefault(n, []).append(w)
    sparts = _unpack_gathered(gathered[-1], [Wl[n].shape for n in ss_names])
    W = {n: Wl[n] for n in SMALL if n not in SMALL_SHARDED}
    for n, p in zip(ss_names, sparts):
        W[n] = _unblocked(p, Wl[n].shape, SMALL_SHARDED[n])

    loss_local, grad_x, grads, (s5_disc_vjp, s5_cot) = _step(x, loss_target, W, Wfull)
    loss = lax.psum(loss_local, MESH_AXES)

    pairs = [grads[n] if i is None else grads[n][i] for n, i in units]
    tag = lambda n, i: n if i is None else "%s%d" % (n, i)
    recv1 = exchange_sibling("rs_sibling", [p[1] for p in pairs])
    p1 = [add_sibling("add_sibling_" + tag(n, i), p[0], r, c_idx) for (n, i), p, r in zip(units, pairs, recv1)]
    recv2 = exchange_chips("rs_chips", [p[1] for p in p1])
    outs4 = {}
    for (n, i), p, r in zip(units, p1, recv2):
        w2, m2, v2 = (d[n].reshape(-1, d[n].shape[-1]) for d in (Wl, Ml, Vl))
        res = adam_big("adam_" + tag(n, i), p[0], r, w2, m2, v2, chip_idx, layer=0 if i is None else i)
        if i is None:
            outs4[n] = [t.reshape(Wl[n].shape) for t in res]
        else:
            outs4.setdefault(n, []).append(res)
    for n in BIG:
        if Wl[n].shape[0] == DEPTH:
            outs4[n] = [jnp.stack([layer[k] for layer in outs4[n]]) for k in range(4)]
    out_g = {n: outs4[n][0] for n in BIG}
    out_d = {n: outs4[n][1] for n in BIG}
    out_m = {n: outs4[n][2] for n in BIG}
    out_v = {n: outs4[n][3] for n in BIG}

    s5_lin = ['ssm_a_re', 'ssm_a_im', 'ssm_log_dt', 'ssm_b_re', 'ssm_b_im']
    direct = [n for n in SMALL if n not in s5_lin]
    full_shape = lambda n: W[n].shape
    small_parts = [grads[n].reshape(full_shape(n)) for n in direct] + list(s5_cot)
    gsum = sum8(all_gather("gather_small_grads", [_pack(small_parts)])[0])
    summed = _unpack(gsum, [p.shape for p in small_parts])
    gsmall = dict(zip(direct, summed[:len(direct)]))
    s5g = s5_disc_vjp(tuple(summed[len(direct):]))
    for n, gval in zip(s5_lin, s5g):
        gsmall[n] = gval[None]
    for n, ax in SMALL_SHARDED.items():
        gsmall[n] = lax.dynamic_slice_in_dim(gsmall[n], my_idx * Wl[n].shape[ax], Wl[n].shape[ax], axis=ax)
    sm_shapes = [Wl[n].shape for n in SMALL]
    dS, mS, vS = adam_small(_pack([Wl[n] for n in SMALL]), _pack([gsmall[n] for n in SMALL]),
                            _pack([Ml[n] for n in SMALL]), _pack([Vl[n] for n in SMALL]))
    for n, gval in zip(SMALL, [gsmall[n] for n in SMALL]):
        out_g[n] = gval.reshape(Wl[n].shape)
    out_d.update(zip(SMALL, _unpack(dS, sm_shapes)))
    out_m.update(zip(SMALL, _unpack(mS, sm_shapes)))
    out_v.update(zip(SMALL, _unpack(vS, sm_shapes)))

    return (loss, grad_x, *[out_g[n] for n in WEIGHT_NAMES], *[out_d[n] for n in WEIGHT_NAMES],
            *[out_m[n] for n in WEIGHT_NAMES], *[out_v[n] for n in WEIGHT_NAMES])


def _unpack_gathered(g, shard_shapes):
    flat = g.reshape(N_DEV, -1)
    out, off = [], 0
    for s in shard_shapes:
        n = math.prod(s)
        out.append(flat[:, off:off + n].reshape((N_DEV,) + tuple(s)))
        off += n
    return out
```

```python
import functools
import math

import jax
import jax.numpy as jnp
from jax import lax
from jax.experimental import pallas as pl
from jax.experimental.pallas import tpu as pltpu

F32 = jnp.float32
BF16 = jnp.bfloat16

D_MODEL = 1024
DEPTH = 4
EPS = 1e-6
SSM_GROUP = 16
SSM_GROUPS = 64
SSM_STATE = 64
S5_GB = 8
S5_NGB = SSM_GROUPS // S5_GB
S5_CH = S5_GB * SSM_GROUP
S5_ST = S5_GB * SSM_STATE
S5_L = 128
CONV_WIDTH = 31
CONV_PAD = 32
CONV_TS = 256
CONV_CW = 256
GMLP_CHUNK = 128
GMLP_HEADS = 4
ATT_CONFIGS = ((128, 1), (512, 4), (2048, 16))
ATT_HEADS = 8
HEAD_DIM = 64
ATT_BLK = 128
ATT_TB = 2
ATT_W = ATT_HEADS * HEAD_DIM
N_DEV = 8
ADAM_LR = 0.001
ADAM_B1 = 0.9
ADAM_B2 = 0.999
ADAM_EPS = 1e-08
ADAM_WD = 0.01
ADAM_STEP = 10
VMEM_LIMIT = 56 * 1024 * 1024
PACK_C = 1024
MESH_AXES = ("x", "y", "c")
MESH = pl.DeviceIdType.MESH

WEIGHT_NAMES = ['norm_mix', 'norm_mlp', 'norm_final', 'ssm_a_re', 'ssm_a_im', 'ssm_b_re', 'ssm_b_im',
                'ssm_c_re', 'ssm_c_im', 'ssm_d', 'ssm_log_dt', 'ssm_w_glu', 'conv_w_pw1', 'conv_b_pw1',
                'conv_w_dw', 'conv_b_dw', 'conv_ln_g', 'conv_ln_b', 'conv_w_pw2', 'conv_b_pw2',
                'gmlp_w_in', 'gmlp_ln_g', 'gmlp_ln_b', 'gmlp_w_s', 'gmlp_b_s', 'gmlp_w_out',
                'attn_w_qkv', 'attn_w_o', 'mlp_w_in', 'mlp_w_out']
BIG = {'ssm_w_glu': 2, 'conv_w_pw1': 2, 'conv_w_pw2': 1, 'gmlp_w_in': 2, 'gmlp_w_out': 1,
       'attn_w_qkv': 2, 'attn_w_o': 2, 'mlp_w_in': 2, 'mlp_w_out': 1}
SMALL_SHARDED = {'conv_b_pw1': 1, 'conv_w_dw': 2, 'conv_b_dw': 1, 'conv_ln_g': 1, 'conv_ln_b': 1,
                 'conv_b_pw2': 1, 'gmlp_ln_g': 1, 'gmlp_ln_b': 1}
SMALL = [n for n in WEIGHT_NAMES if n not in BIG]


def _cparams(sem=None):
    return pltpu.CompilerParams(dimension_semantics=sem, vmem_limit_bytes=VMEM_LIMIT)


def _dot(a, b):
    return jnp.dot(a, b, preferred_element_type=F32)


def _dot_nt(a, b):
    return lax.dot_general(a, b, (((1,), (1,)), ((), ())), preferred_element_type=F32)


def _dot_tn(a, b):
    return lax.dot_general(a, b, (((0,), (0,)), ((), ())), preferred_element_type=F32)


def rowwise(name, fn, rows, params, row_out, acc_out=(), tr=256):
    T = rows[0].shape[0]
    tr = min(tr, T)
    while T % tr:
        tr //= 2
    assert tr % 8 == 0
    nr, npar, nro = len(rows), len(params), len(row_out)

    def body(*refs):
        ins = [r[...] for r in refs[:nr + npar]]
        outs = refs[nr + npar:]
        res = fn(*ins)
        if not isinstance(res, (tuple, list)):
            res = (res,)
        for k in range(nro):
            outs[k][...] = res[k].astype(outs[k].dtype)
        if acc_out:
            @pl.when(pl.program_id(0) == 0)
            def _():
                for k in range(nro, len(outs)):
                    outs[k][...] = jnp.zeros_like(outs[k])
            for k in range(nro, len(outs)):
                outs[k][...] += res[k].astype(outs[k].dtype)

    in_specs = [pl.BlockSpec((tr, r.shape[1]), lambda i: (i, 0)) for r in rows]
    in_specs += [pl.BlockSpec(p.shape, lambda i, nd=p.ndim: (0,) * nd) for p in params]
    out_shape = [jax.ShapeDtypeStruct((T, c), dt) for c, dt in row_out]
    out_specs = [pl.BlockSpec((tr, c), lambda i: (i, 0)) for c, dt in row_out]
    out_shape += [jax.ShapeDtypeStruct(s, dt) for s, dt in acc_out]
    out_specs += [pl.BlockSpec(s, lambda i, nd=len(s): (0,) * nd) for s, dt in acc_out]
    res = pl.pallas_call(body, grid=(T // tr,), in_specs=in_specs, out_specs=out_specs, out_shape=out_shape,
                         name=name, compiler_params=_cparams(("arbitrary",)))(*rows, *params)
    return res


def _tile_m(M, K):
    tm = 2048
    while tm > 256 and tm * K * 2 > (4 << 20):
        tm //= 2
    return min(tm, M)


def matmul(name, a, b, *, mode='nn', epi=None, extras=(), out_dtypes=(F32,), out3=False):
    M, K = a.shape
    if mode == 'cb':
        nblk, _, tn = b.shape
        N = nblk * tn
    else:
        N = b.shape[0] if mode == 'nt' else b.shape[1]
        tn = min(512, N)
    tm = _tile_m(M, K)
    assert M % tm == 0 and N % tn == 0, (M, N, tm, tn)
    nex = len(extras)

    def body(a_ref, b_ref, *rest):
        ex = [e[...] for e in rest[:nex]]
        outs = rest[nex:]
        acc = _dot_nt(a_ref[...], b_ref[...]) if mode == 'nt' else _dot(a_ref[...], b_ref[...])
        res = epi(acc, *ex) if epi is not None else (acc,)
        for o, r in zip(outs, res):
            o[...] = r.astype(o.dtype)

    b_spec = {'nn': pl.BlockSpec((K, tn), lambda i, j: (0, j)), 'nt': pl.BlockSpec((tn, K), lambda i, j: (j, 0)),
              'cb': pl.BlockSpec((None, K, tn), lambda i, j: (j, 0, 0))}[mode]
    in_specs = [pl.BlockSpec((tm, K), lambda i, j: (i, 0)), b_spec]
    for arr, kind in extras:
        if kind == 'tile':
            in_specs.append(pl.BlockSpec((tm, tn), lambda i, j: (i, j)))
        else:
            in_specs.append(pl.BlockSpec((1, tn), lambda i, j: (0, j)))
    if out3:
        out_shape = [jax.ShapeDtypeStruct((N // tn, M, tn), dt) for dt in out_dtypes]
        out_specs = [pl.BlockSpec((None, tm, tn), lambda i, j: (j, i, 0)) for dt in out_dtypes]
    else:
        out_shape = [jax.ShapeDtypeStruct((M, N), dt) for dt in out_dtypes]
        out_specs = [pl.BlockSpec((tm, tn), lambda i, j: (i, j)) for dt in out_dtypes]
    return pl.pallas_call(body, grid=(M // tm, N // tn), in_specs=in_specs, out_specs=out_specs,
                          out_shape=out_shape, name=name,
                          compiler_params=_cparams(("arbitrary", "arbitrary")))(a, b, *[e[0] for e in extras])


def matmul_nt_cb(name, a, b, *, a3=False, epi=None, extras=(), out_dtypes=(F32,)):
    nblk, K, n = b.shape
    M = a.shape[1] if a3 else a.shape[0]
    tm = _tile_m(M, nblk * n)
    assert M % tm == 0
    nex = len(extras)

    def body(a_ref, b_ref, *rest):
        ex, outs = rest[:nex], rest[nex:]
        acc = None
        for j in range(nblk):
            aj = a_ref[j] if a3 else a_ref[:, j * n:(j + 1) * n]
            part = _dot_nt(aj, b_ref[j])
            acc = part if acc is None else acc + part
        res = epi(acc, *[e[...] for e in ex]) if epi is not None else (acc,)
        for o, r in zip(outs, res):
            o[...] = r.astype(o.dtype)

    a_spec = (pl.BlockSpec((nblk, tm, n), lambda i: (0, i, 0)) if a3
              else pl.BlockSpec((tm, nblk * n), lambda i: (i, 0)))
    row = pl.BlockSpec((tm, K), lambda i: (i, 0))
    return pl.pallas_call(
        body, grid=(M // tm,),
        in_specs=[a_spec, pl.BlockSpec((nblk, K, n), lambda i: (0, 0, 0))] + [row] * nex,
        out_specs=[row] * len(out_dtypes), out_shape=[jax.ShapeDtypeStruct((M, K), dt) for dt in out_dtypes],
        name=name, compiler_params=_cparams(("arbitrary",)))(a, b, *extras)


def wgrad(name, a, g, *, cb=False, g3=False):
    M, K = a.shape
    tm, tk = min(M, 1024), min(K, 1024)
    if cb:
        n = g.shape[2] if g3 else g.shape[1] // N_DEV
        grid = (K // tk, N_DEV, M // tm)
        g_spec = (pl.BlockSpec((None, tm, n), lambda k, j, m: (j, m, 0)) if g3
                  else pl.BlockSpec((tm, n), lambda k, j, m: (m, j)))
        o_spec = pl.BlockSpec((None, tk, n), lambda k, j, m: (j, k, 0))
        o_shape = (N_DEV, K, n)
    else:
        N = g.shape[1]
        tn = min(N, 512)
        grid = (K // tk, N // tn, M // tm)
        g_spec = pl.BlockSpec((tm, tn), lambda k, j, m: (m, j))
        o_spec = pl.BlockSpec((tk, tn), lambda k, j, m: (k, j))
        o_shape = (K, N)
    nm = M // tm

    def body(a_ref, g_ref, o_ref, o16_ref):
        m = pl.program_id(2)

        @pl.when(m == 0)
        def _():
            o_ref[...] = jnp.zeros_like(o_ref)
        o_ref[...] += _dot_tn(a_ref[...], g_ref[...])

        @pl.when(m == nm - 1)
        def _():
            o16_ref[...] = o_ref[...].astype(BF16)

    return pl.pallas_call(
        body, grid=grid, in_specs=[pl.BlockSpec((tm, tk), lambda k, j, m: (m, k)), g_spec],
        out_specs=[o_spec, o_spec],
        out_shape=[jax.ShapeDtypeStruct(o_shape, F32), jax.ShapeDtypeStruct(o_shape, BF16)], name=name,
        compiler_params=_cparams(("arbitrary", "arbitrary", "arbitrary")))(a, g)


def _rms(x, g):
    x = x.astype(F32)
    return x * lax.rsqrt(jnp.mean(x * x, axis=-1, keepdims=True) + EPS) * g


def _ln(x, g, b):
    mu = jnp.mean(x, axis=-1, keepdims=True)
    var = jnp.mean(jnp.square(x - mu), axis=-1, keepdims=True)
    return (x - mu) * lax.rsqrt(var + EPS) * g + b


def _glu(z):
    d = z.shape[1] // 2
    return z[:, :d] * jax.nn.sigmoid(z[:, d:])


def _glu_bwd(z, dy):
    d = z.shape[1] // 2
    a, s = z[:, :d], jax.nn.sigmoid(z[:, d:])
    return jnp.concatenate([dy * s, dy * a * s * (1.0 - s)], axis=1)


def _colsum(v):
    return jnp.sum(v.astype(F32), axis=0, keepdims=True)


def rms_fwd(name, x, g, want_f32=False):
    def fn(xt, gt):
        h = _rms(xt, gt)
        return (h, h) if want_f32 else (h,)
    D = x.shape[1]
    outs = [(D, BF16)] + ([(D, F32)] if want_f32 else [])
    return rowwise(name, fn, [x], [g], outs)


def rms_bwd(name, x, dh, dres, g):
    def fn(xt, dht, drt, gt):
        _, vjp = jax.vjp(_rms, xt, gt)
        dx, dg = vjp(dht.astype(F32))
        dx = dx + drt
        return dx, dx, dg
    D = x.shape[1]
    return rowwise(name, fn, [x, dh, dres], [g], [(D, F32), (D, BF16)], [((1, D), F32)])


def s5_disc(a_re, a_im, log_dt, b_re, b_im):
    dt = jnp.exp(log_dt)[:, None]
    er = jnp.exp(a_re * dt)
    lam_re = er * jnp.cos(a_im * dt)
    lam_im = er * jnp.sin(a_im * dt)
    nr, ni = lam_re - 1.0, lam_im
    den = a_re * a_re + a_im * a_im
    f_re = (nr * a_re + ni * a_im) / den
    f_im = (ni * a_re - nr * a_im) / den
    bb_re = f_re[..., None] * b_re - f_im[..., None] * b_im
    bb_im = f_re[..., None] * b_im + f_im[..., None] * b_re
    return lam_re, lam_im, bb_re, bb_im


def _s5_blockdiag_b(bb):
    t = bb.reshape(S5_NGB, S5_GB, SSM_STATE, SSM_GROUP).transpose(0, 1, 3, 2)
    eye = jnp.eye(S5_GB, dtype=bb.dtype)
    return jnp.einsum('bgpn,gh->bgphn', t, eye).reshape(S5_NGB, S5_CH, S5_ST)


def _s5_blockdiag_b_inv(x):
    t = x.reshape(S5_NGB, S5_GB, SSM_GROUP, S5_GB, SSM_STATE)
    eye = jnp.eye(S5_GB, dtype=x.dtype)
    d = jnp.einsum('bgphn,gh->bgpn', t, eye)
    return d.transpose(0, 1, 3, 2).reshape(SSM_GROUPS, SSM_STATE, SSM_GROUP)


def _s5_blockdiag_c(c):
    t = c.reshape(S5_NGB, S5_GB, SSM_GROUP, SSM_STATE).transpose(0, 1, 3, 2)
    eye = jnp.eye(S5_GB, dtype=c.dtype)
    return jnp.einsum('bgnp,gh->bgnhp', t, eye).reshape(S5_NGB, S5_ST, S5_CH)


def _s5_blockdiag_c_inv(x):
    t = x.reshape(S5_NGB, S5_GB, SSM_STATE, S5_GB, SSM_GROUP)
    eye = jnp.eye(S5_GB, dtype=x.dtype)
    d = jnp.einsum('bgnhp,gh->bgnp', t, eye)
    return d.transpose(0, 1, 3, 2).reshape(SSM_GROUPS, SSM_GROUP, SSM_STATE)


def s5_tables(lam_re, lam_im, L):
    pr, pi = lam_re.reshape(1, -1), lam_im.reshape(1, -1)
    n = 1
    while n < L:
        lr, li = pr[n - 1:n], pi[n - 1:n]
        pr, pi = (jnp.concatenate([pr, pr * lr - pi * li], 0), jnp.concatenate([pi, pr * li + pi * lr], 0))
        n *= 2
    nk = int(math.log2(L))
    idx = [2 ** k - 1 for k in range(nk)] + [0] * (8 - nk)

    def blk(t):
        return t.reshape(t.shape[0], S5_NGB, S5_ST).transpose(1, 0, 2)

    def rows(t):
        return jnp.concatenate([t[j:j + 1] for j in idx], axis=0)
    return blk(pr), blk(pi), blk(rows(pr)), blk(rows(pi))


S5_SUB = 8


def _scan_tiles(br, bi, a2r, a2i, reverse):
    L = br.shape[0]
    sub = lax.broadcasted_iota(jnp.int32, br.shape, 0) & (S5_SUB - 1)
    xr, xi = br, bi
    for k in range(3):
        s = 1 << k
        ar, ai = a2r[k:k + 1, :], a2i[k:k + 1, :]
        if reverse:
            sr, si = pltpu.roll(xr, L - s, 0), pltpu.roll(xi, L - s, 0)
            m = sub < S5_SUB - s
        else:
            sr, si = pltpu.roll(xr, s, 0), pltpu.roll(xi, s, 0)
            m = sub >= s
        sr, si = jnp.where(m, sr, 0.0), jnp.where(m, si, 0.0)
        xr, xi = xr + ar * sr - ai * si, xi + ar * si + ai * sr
    return xr, xi


def _scan_chain(xr, xi, pr, pi, cr, ci, out_r, out_i, reverse):
    ntile = xr.shape[0] // S5_SUB
    for g in (reversed(range(ntile)) if reverse else range(ntile)):
        rs = slice(g * S5_SUB, (g + 1) * S5_SUB)
        if reverse:
            nr = xr[rs] + pr * cr + pi * ci
            ni = xi[rs] + pr * ci - pi * cr
            cr, ci = nr[0:1], ni[0:1]
        else:
            nr = xr[rs] + pr * cr - pi * ci
            ni = xi[rs] + pr * ci + pi * cr
            cr, ci = nr[S5_SUB - 1:S5_SUB], ni[S5_SUB - 1:S5_SUB]
        out_r[rs, :] = nr
        out_i[rs, :] = ni
    return cr, ci


def s5_fwd(h, bre, bim, cre, cim, pwr, pwi, l2r, l2i, dskip, bsz):
    T, D = h.shape
    L = S5_L
    S = T // bsz
    NC = S // L

    def body(h_ref, bre_ref, bim_ref, cre_ref, cim_ref, pwr_ref, pwi_ref, l2r_ref, l2i_ref, d_ref,
             y_ref, gy_ref, xs_ref, car_r, car_i, xr_s, xi_s):
        @pl.when(pl.program_id(2) == 0)
        def _():
            car_r[...] = jnp.zeros_like(car_r)
            car_i[...] = jnp.zeros_like(car_i)
        u = h_ref[...]
        ub = u.astype(BF16)
        cr, ci = car_r[0:1, :], car_i[0:1, :]
        xs_ref[...] = jnp.zeros_like(xs_ref)
        xs_ref[0:1, :] = cr
        xs_ref[1:2, :] = ci
        xr, xi = _scan_tiles(_dot(ub, bre_ref[...]), _dot(ub, bim_ref[...]), l2r_ref[...], l2i_ref[...], False)
        cr, ci = _scan_chain(xr, xi, pwr_ref[...], pwi_ref[...], cr, ci, xr_s, xi_s, False)
        car_r[...] = jnp.broadcast_to(cr, car_r.shape)
        car_i[...] = jnp.broadcast_to(ci, car_i.shape)
        y = (_dot(xr_s[...].astype(BF16), cre_ref[...]) - _dot(xi_s[...].astype(BF16), cim_ref[...])
             + d_ref[...] * u)
        y_ref[...] = y
        gy_ref[...] = jax.nn.gelu(y).astype(BF16)

    tok = lambda g, b, c: (b * NC + c, g)
    par = lambda g, b, c: (g, 0, 0)
    return pl.pallas_call(
        body, grid=(S5_NGB, bsz, NC),
        in_specs=[pl.BlockSpec((L, S5_CH), tok),
                  pl.BlockSpec((None, S5_CH, S5_ST), par), pl.BlockSpec((None, S5_CH, S5_ST), par),
                  pl.BlockSpec((None, S5_ST, S5_CH), par), pl.BlockSpec((None, S5_ST, S5_CH), par),
                  pl.BlockSpec((None, 8, S5_ST), par), pl.BlockSpec((None, 8, S5_ST), par),
                  pl.BlockSpec((None, 8, S5_ST), par), pl.BlockSpec((None, 8, S5_ST), par),
                  pl.BlockSpec((1, S5_CH), lambda g, b, c: (0, g))],
        out_specs=[pl.BlockSpec((L, S5_CH), tok), pl.BlockSpec((L, S5_CH), tok),
                   pl.BlockSpec((None, 8, S5_ST), lambda g, b, c: (b * NC + c, 0, g))],
        out_shape=[jax.ShapeDtypeStruct((T, D), F32), jax.ShapeDtypeStruct((T, D), BF16),
                   jax.ShapeDtypeStruct((bsz * NC, 8, S5_NGB * S5_ST), F32)],
        scratch_shapes=[pltpu.VMEM((8, S5_ST), F32), pltpu.VMEM((8, S5_ST), F32),
                        pltpu.VMEM((L, S5_ST), F32), pltpu.VMEM((L, S5_ST), F32)],
        name="s5_fwd", compiler_params=_cparams(("arbitrary", "arbitrary", "arbitrary")),
    )(h, bre, bim, cre, cim, pwr, pwi, l2r, l2i, dskip)


def s5_bwd(h, dy, xs, bre, bim, cre, cim, pwr, pwi, pwr_rev, pwi_rev, l2r, l2i, dskip, bsz):
    T, D = h.shape
    L = S5_L
    S = T // bsz
    NC = S // L

    def body(h_ref, dy_ref, xs_ref, bre_ref, bim_ref, cre_ref, cim_ref, pwr_ref, pwi_ref, qr_ref, qi_ref,
             l2r_ref, l2i_ref, d_ref, du_ref, dbr_ref, dbi_ref, dcr_ref, dci_ref, dl_ref, dd_ref, car_r, car_i,
             xr_s, xi_s, dr_s, di_s):
        first = (pl.program_id(1) == 0) & (pl.program_id(2) == 0)

        @pl.when(first)
        def _():
            for r in (dbr_ref, dbi_ref, dcr_ref, dci_ref, dl_ref, dd_ref):
                r[...] = jnp.zeros_like(r)

        @pl.when(pl.program_id(2) == 0)
        def _():
            car_r[...] = jnp.zeros_like(car_r)
            car_i[...] = jnp.zeros_like(car_i)

        u = h_ref[...]
        ub = u.astype(BF16)
        dyv = dy_ref[...]
        dyb = dyv.astype(BF16)
        l2r_v, l2i_v = l2r_ref[...], l2i_ref[...]
        x0r, x0i = xs_ref[0:1, :], xs_ref[1:2, :]
        xr, xi = _scan_tiles(_dot(ub, bre_ref[...]), _dot(ub, bim_ref[...]), l2r_v, l2i_v, False)
        _scan_chain(xr, xi, pwr_ref[...], pwi_ref[...], x0r, x0i, xr_s, xi_s, False)
        xr, xi = xr_s[...], xi_s[...]
        gr = _dot_nt(dyb, cre_ref[...])
        gi = -_dot_nt(dyb, cim_ref[...])
        dr, di = _scan_tiles(gr, gi, l2r_v, -l2i_v, True)
        cr, ci = _scan_chain(dr, di, qr_ref[...], qi_ref[...], car_r[0:1, :], car_i[0:1, :], dr_s, di_s, True)
        dr, di = dr_s[...], di_s[...]
        car_r[...] = jnp.broadcast_to(cr, car_r.shape)
        car_i[...] = jnp.broadcast_to(ci, car_i.shape)
        row = lax.broadcasted_iota(jnp.int32, xr.shape, 0)
        xpr = jnp.where(row >= 1, pltpu.roll(xr, 1, 0), x0r)
        xpi = jnp.where(row >= 1, pltpu.roll(xi, 1, 0), x0i)
        dl_ref[0:1, :] += _colsum(dr * xpr + di * xpi)
        dl_ref[1:2, :] += _colsum(di * xpr - dr * xpi)
        drb, dib = dr.astype(BF16), di.astype(BF16)
        dcr_ref[...] += _dot_tn(xr.astype(BF16), dyb)
        dci_ref[...] -= _dot_tn(xi.astype(BF16), dyb)
        dbr_ref[...] += _dot_tn(ub, drb)
        dbi_ref[...] += _dot_tn(ub, dib)
        du_ref[...] = _dot_nt(drb, bre_ref[...]) + _dot_nt(dib, bim_ref[...]) + d_ref[...] * dyv
        dd_ref[0:1, :] += _colsum(dyv * u)

    tok = lambda g, b, c: (b * NC + (NC - 1 - c), g)
    par = lambda g, b, c: (g, 0, 0)
    return pl.pallas_call(
        body, grid=(S5_NGB, bsz, NC),
        in_specs=[pl.BlockSpec((L, S5_CH), tok), pl.BlockSpec((L, S5_CH), tok),
                  pl.BlockSpec((None, 8, S5_ST), lambda g, b, c: (b * NC + (NC - 1 - c), 0, g)),
                  pl.BlockSpec((None, S5_CH, S5_ST), par), pl.BlockSpec((None, S5_CH, S5_ST), par),
                  pl.BlockSpec((None, S5_ST, S5_CH), par), pl.BlockSpec((None, S5_ST, S5_CH), par),
                  pl.BlockSpec((None, 8, S5_ST), par), pl.BlockSpec((None, 8, S5_ST), par),
                  pl.BlockSpec((None, 8, S5_ST), par), pl.BlockSpec((None, 8, S5_ST), par),
                  pl.BlockSpec((None, 8, S5_ST), par), pl.BlockSpec((None, 8, S5_ST), par),
                  pl.BlockSpec((1, S5_CH), lambda g, b, c: (0, g))],
        out_specs=[pl.BlockSpec((L, S5_CH), tok),
                   pl.BlockSpec((None, S5_CH, S5_ST), par), pl.BlockSpec((None, S5_CH, S5_ST), par),
                   pl.BlockSpec((None, S5_ST, S5_CH), par), pl.BlockSpec((None, S5_ST, S5_CH), par),
                   pl.BlockSpec((None, 8, S5_ST), par),
                   pl.BlockSpec((8, S5_CH), lambda g, b, c: (0, g))],
        out_shape=[jax.ShapeDtypeStruct((T, D), F32),
                   jax.ShapeDtypeStruct((S5_NGB, S5_CH, S5_ST), F32), jax.ShapeDtypeStruct((S5_NGB, S5_CH, S5_ST), F32),
                   jax.ShapeDtypeStruct((S5_NGB, S5_ST, S5_CH), F32), jax.ShapeDtypeStruct((S5_NGB, S5_ST, S5_CH), F32),
                   jax.ShapeDtypeStruct((S5_NGB, 8, S5_ST), F32), jax.ShapeDtypeStruct((8, D), F32)],
        scratch_shapes=[pltpu.VMEM((8, S5_ST), F32), pltpu.VMEM((8, S5_ST), F32)]
        + [pltpu.VMEM((L, S5_ST), F32)] * 4,
        name="s5_bwd", compiler_params=_cparams(("arbitrary", "arbitrary", "arbitrary")),
    )(h, dy, xs, bre, bim, cre, cim, pwr, pwi, pwr_rev, pwi_rev, l2r, l2i, dskip)


def _shift_rows(win, off, n):
    if off == 0:
        return win[:n]
    return pltpu.roll(win, win.shape[0] - off, 0)[:n]


def dwconv_fwd(z, w, b, bsz):
    T, D = z.shape
    S = T // bsz
    TS, CW, PAD = CONV_TS, CONV_CW, CONV_PAD

    def body(z_ref, w_ref, b_ref, y_ref, zp):
        zp[0:PAD, :] = jnp.zeros((PAD, CW), F32)
        zp[PAD:, :] = z_ref[...]
        wv, bv = w_ref[...], b_ref[...]

        def step(t, carry):
            base = pl.multiple_of(t * TS, TS)
            win = zp[pl.ds(base, TS + PAD), :]
            acc = jnp.zeros((TS, CW), F32) + bv
            for k in range(CONV_WIDTH):
                acc = acc + wv[k:k + 1, :] * _shift_rows(win, PAD - (CONV_WIDTH - 1) + k, TS)
            y_ref[pl.ds(base, TS), :] = acc
            return carry
        lax.fori_loop(0, S // TS, step, 0)

    return pl.pallas_call(
        body, grid=(D // CW, bsz),
        in_specs=[pl.BlockSpec((S, CW), lambda c, bb: (bb, c)), pl.BlockSpec((32, CW), lambda c, bb: (0, c)),
                  pl.BlockSpec((1, CW), lambda c, bb: (0, c))],
        out_specs=pl.BlockSpec((S, CW), lambda c, bb: (bb, c)),
        out_shape=jax.ShapeDtypeStruct((T, D), F32),
        scratch_shapes=[pltpu.VMEM((S + PAD, CW), F32)],
        name="dwconv_fwd", compiler_params=_cparams(("arbitrary", "arbitrary")),
    )(z, w, b)


def dwconv_bwd(z, dy, w, bsz):
    T, D = z.shape
    S = T // bsz
    TS, CW, PAD = CONV_TS, CONV_CW, CONV_PAD

    def body(z_ref, dy_ref, w_ref, dz_ref, dw_ref, db_ref, zp, dyp):
        @pl.when(pl.program_id(1) == 0)
        def _():
            dw_ref[...] = jnp.zeros_like(dw_ref)
            db_ref[...] = jnp.zeros_like(db_ref)
        zp[0:PAD, :] = jnp.zeros((PAD, CW), F32)
        zp[PAD:, :] = z_ref[...]
        dyp[0:S, :] = dy_ref[...]
        dyp[S:, :] = jnp.zeros((PAD, CW), F32)
        wv = w_ref[...]

        def step(t, carry):
            base = pl.multiple_of(t * TS, TS)
            zwin = zp[pl.ds(base, TS + PAD), :]
            dwin = dyp[pl.ds(base, TS + PAD), :]
            dyt = dwin[:TS]
            acc = jnp.zeros((TS, CW), F32)
            for j in range(CONV_WIDTH):
                k = CONV_WIDTH - 1 - j
                acc = acc + wv[k:k + 1, :] * _shift_rows(dwin, j, TS)
            dz_ref[pl.ds(base, TS), :] = acc
            for k in range(CONV_WIDTH):
                dw_ref[k:k + 1, :] += _colsum(dyt * _shift_rows(zwin, PAD - (CONV_WIDTH - 1) + k, TS))
            db_ref[0:1, :] += _colsum(dyt)
            return carry
        lax.fori_loop(0, S // TS, step, 0)

    return pl.pallas_call(
        body, grid=(D // CW, bsz),
        in_specs=[pl.BlockSpec((S, CW), lambda c, bb: (bb, c)), pl.BlockSpec((S, CW), lambda c, bb: (bb, c)),
                  pl.BlockSpec((32, CW), lambda c, bb: (0, c))],
        out_specs=[pl.BlockSpec((S, CW), lambda c, bb: (bb, c)), pl.BlockSpec((32, CW), lambda c, bb: (0, c)),
                   pl.BlockSpec((8, CW), lambda c, bb: (0, c))],
        out_shape=[jax.ShapeDtypeStruct((T, D), F32), jax.ShapeDtypeStruct((32, D), F32),
                   jax.ShapeDtypeStruct((8, D), F32)],
        scratch_shapes=[pltpu.VMEM((S + PAD, CW), F32), pltpu.VMEM((S + PAD, CW), F32)],
        name="dwconv_bwd", compiler_params=_cparams(("arbitrary", "arbitrary")),
    )(z, dy, w)


def spatial_fwd(u, vln, ws, bias):
    T, E = u.shape
    C, H = GMLP_CHUNK, GMLP_HEADS
    hw = E // H

    def body(u_ref, v_ref, ws_ref, b_ref, o_ref):
        for hh in range(H):
            sl = slice(hh * hw, (hh + 1) * hw)
            vp = _dot(ws_ref[hh], v_ref[:, sl]) + b_ref[:, sl]
            o_ref[:, sl] = (u_ref[:, sl] * vp).astype(o_ref.dtype)

    return pl.pallas_call(
        body, grid=(T // C,),
        in_specs=[pl.BlockSpec((C, E), lambda i: (i, 0)), pl.BlockSpec((C, E), lambda i: (i, 0)),
                  pl.BlockSpec((H, C, C), lambda i: (0, 0, 0)), pl.BlockSpec((C, E), lambda i: (0, 0))],
        out_specs=pl.BlockSpec((C, E), lambda i: (i, 0)),
        out_shape=jax.ShapeDtypeStruct((T, E), BF16),
        name="spatial_fwd", compiler_params=_cparams(("arbitrary",)),
    )(u, vln, ws, bias)


def spatial_bwd(u, vln, dg, ws, bias):
    T, E = u.shape
    C, H = GMLP_CHUNK, GMLP_HEADS
    hw = E // H

    def body(u_ref, v_ref, dg_ref, ws_ref, b_ref, du_ref, dv_ref, dws_ref, db_ref):
        @pl.when(pl.program_id(0) == 0)
        def _():
            dws_ref[...] = jnp.zeros_like(dws_ref)
            db_ref[...] = jnp.zeros_like(db_ref)
        tril = (lax.broadcasted_iota(jnp.int32, (C, C), 1) <= lax.broadcasted_iota(jnp.int32, (C, C), 0))
        for hh in range(H):
            sl = slice(hh * hw, (hh + 1) * hw)
            v = v_ref[:, sl]
            w = ws_ref[hh]
            dgv = dg_ref[:, sl].astype(F32)
            vp = _dot(w, v) + b_ref[:, sl]
            du_ref[:, sl] = dgv * vp
            dvp = dgv * u_ref[:, sl]
            dvpb = dvp.astype(BF16)
            dv_ref[:, sl] = _dot_tn(w, dvpb)
            dws_ref[hh] += jnp.where(tril, _dot_nt(dvpb, v), 0.0)
            db_ref[:, sl] += dvp

    return pl.pallas_call(
        body, grid=(T // C,),
        in_specs=[pl.BlockSpec((C, E), lambda i: (i, 0)), pl.BlockSpec((C, E), lambda i: (i, 0)),
                  pl.BlockSpec((C, E), lambda i: (i, 0)),
                  pl.BlockSpec((H, C, C), lambda i: (0, 0, 0)), pl.BlockSpec((C, E), lambda i: (0, 0))],
        out_specs=[pl.BlockSpec((C, E), lambda i: (i, 0)), pl.BlockSpec((C, E), lambda i: (i, 0)),
                   pl.BlockSpec((H, C, C), lambda i: (0, 0, 0)), pl.BlockSpec((C, E), lambda i: (0, 0))],
        out_shape=[jax.ShapeDtypeStruct((T, E), F32), jax.ShapeDtypeStruct((T, E), F32),
                   jax.ShapeDtypeStruct((H, C, C), F32), jax.ShapeDtypeStruct((C, E), F32)],
        name="spatial_bwd", compiler_params=_cparams(("arbitrary",)),
    )(u, vln, dg, ws, bias)


def _att_masks():
    r = lax.broadcasted_iota(jnp.int32, (ATT_BLK, ATT_BLK), 0)
    c = lax.broadcasted_iota(jnp.int32, (ATT_BLK, ATT_BLK), 1)
    return c <= r, c >= r


NEG = -1e30
ATT_SCALE = HEAD_DIM ** -0.5


def _att_view(t, dil):
    return t.reshape(t.shape[0] // dil, dil * t.shape[1])


def attn_fwd(name, q, k, v, dil, bsz):
    T, Wd = q.shape
    nb = T // (bsz * dil * ATT_BLK)
    TB = min(nb, ATT_TB)
    nsteps = nb // TB

    def body(q_ref, k_ref, v_ref, kp_ref, vp_ref, o_ref, l_ref):
        n = pl.program_id(2)
        mc, mp = _att_masks()
        for j in range(TB):
            rows = slice(j * ATT_BLK, (j + 1) * ATT_BLK)
            prow = slice((j - 1) * ATT_BLK, j * ATT_BLK)
            hp = (n * TB + j) > 0
            for hh in range(ATT_HEADS):
                ls = slice(hh * HEAD_DIM, (hh + 1) * HEAD_DIM)
                qj, kc, vc = q_ref[rows, ls], k_ref[rows, ls], v_ref[rows, ls]
                kp = k_ref[prow, ls] if j > 0 else kp_ref[:, ls]
                vp = v_ref[prow, ls] if j > 0 else vp_ref[:, ls]
                sc = jnp.where(mc, _dot_nt(qj, kc) * ATT_SCALE, NEG)
                sp = jnp.where(mp & hp, _dot_nt(qj, kp) * ATT_SCALE, NEG)
                m = jnp.maximum(jnp.max(sc, axis=1, keepdims=True), jnp.max(sp, axis=1, keepdims=True))
                pc, pp = jnp.exp(sc - m), jnp.exp(sp - m)
                l = jnp.sum(pc, axis=1, keepdims=True) + jnp.sum(pp, axis=1, keepdims=True)
                o_ref[rows, ls] = (_dot(pc.astype(BF16), vc) + _dot(pp.astype(BF16), vp)) / l
                l_ref[rows, ls] = jnp.broadcast_to(m + jnp.log(l), (ATT_BLK, HEAD_DIM))

    blk = pl.BlockSpec((TB * ATT_BLK, Wd), lambda b, r, n: (b * nsteps + n, r))
    prev = pl.BlockSpec((ATT_BLK, Wd), lambda b, r, n: (jnp.maximum(b * nb + n * TB - 1, 0), r))
    qv, kv, vv = (_att_view(t, dil) for t in (q, k, v))
    o, l = pl.pallas_call(
        body, grid=(bsz, dil, nsteps), in_specs=[blk, blk, blk, prev, prev], out_specs=[blk, blk],
        out_shape=[jax.ShapeDtypeStruct(qv.shape, F32), jax.ShapeDtypeStruct(qv.shape, F32)],
        name=name, compiler_params=_cparams(("arbitrary", "arbitrary", "arbitrary")),
    )(qv, kv, vv, kv, vv)
    return o.reshape(T, Wd), l.reshape(T, Wd)


def attn_bwd(name, q, k, v, do, mg, lse, dil, bsz):
    T, Wd = q.shape
    nb = T // (bsz * dil * ATT_BLK)
    TB = min(nb, ATT_TB)
    nsteps = nb // TB

    def body(q_ref, k_ref, v_ref, do_ref, mg_ref, l_ref, kp_ref, vp_ref, qn_ref, don_ref, mgn_ref, ln_ref,
             dq_ref, dk_ref, dv_ref):
        n = pl.program_id(2)
        mc, mp = _att_masks()

        def probs(qj, kk, lse_col, mask):
            s = _dot_nt(qj, kk) * ATT_SCALE
            return jnp.where(mask, jnp.exp(s - lse_col), 0.0)

        def ds_of(p, doj, vv, delta):
            return (p * (_dot_nt(doj, vv) - delta) * ATT_SCALE).astype(BF16)

        for hh in range(ATT_HEADS):
            ls = slice(hh * HEAD_DIM, (hh + 1) * HEAD_DIM)
            dk = [None] * TB
            dv = [None] * TB
            for j in range(TB + 1):
                rows = slice(j * ATT_BLK, (j + 1) * ATT_BLK)
                prow = slice((j - 1) * ATT_BLK, j * ATT_BLK)
                if j < TB:
                    qj, doj, mgj, lj = q_ref[rows, ls], do_ref[rows, ls], mg_ref[rows, ls], l_ref[rows, ls]
                    hp = (n * TB + j) > 0
                else:
                    qj, doj, mgj, lj = qn_ref[:, ls], don_ref[:, ls], mgn_ref[:, ls], ln_ref[:, ls]
                    hp = (n + 1) * TB < nb
                lse_col = lj[:, 0:1]
                delta = jnp.sum(doj.astype(F32) * mgj.astype(F32), axis=1, keepdims=True)
                if j > 0:
                    kp, vp = k_ref[prow, ls], v_ref[prow, ls]
                else:
                    kp, vp = kp_ref[:, ls], vp_ref[:, ls]
                pp = probs(qj, kp, lse_col, mp & hp)
                dsp = ds_of(pp, doj, vp, delta)
                if j > 0:
                    dk[j - 1] = dk[j - 1] + _dot_tn(dsp, qj)
                    dv[j - 1] = dv[j - 1] + _dot_tn(pp.astype(BF16), doj)
                if j < TB:
                    kc, vc = k_ref[rows, ls], v_ref[rows, ls]
                    pc = probs(qj, kc, lse_col, mc)
                    dsc = ds_of(pc, doj, vc, delta)
                    dq_ref[rows, ls] = (_dot(dsc, kc) + _dot(dsp, kp)).astype(dq_ref.dtype)
                    dk[j] = _dot_tn(dsc, qj)
                    dv[j] = _dot_tn(pc.astype(BF16), doj)
            for j in range(TB):
                rows = slice(j * ATT_BLK, (j + 1) * ATT_BLK)
                dk_ref[rows, ls] = dk[j].astype(dk_ref.dtype)
                dv_ref[rows, ls] = dv[j].astype(dv_ref.dtype)

    blk = pl.BlockSpec((TB * ATT_BLK, Wd), lambda b, r, n: (b * nsteps + n, r))
    prev = pl.BlockSpec((ATT_BLK, Wd), lambda b, r, n: (jnp.maximum(b * nb + n * TB - 1, 0), r))
    nxt = pl.BlockSpec((ATT_BLK, Wd), lambda b, r, n: (b * nb + jnp.minimum((n + 1) * TB, nb - 1), r))
    qv, kv, vv, dov, mgv, lv = (_att_view(t, dil) for t in (q, k, v, do, mg, lse))
    res = pl.pallas_call(
        body, grid=(bsz, dil, nsteps), in_specs=[blk] * 6 + [prev, prev, nxt, nxt, nxt, nxt],
        out_specs=[blk, blk, blk], out_shape=[jax.ShapeDtypeStruct(qv.shape, BF16)] * 3,
        name=name, compiler_params=_cparams(("arbitrary", "arbitrary", "arbitrary")),
    )(qv, kv, vv, dov, mgv, lv, kv, vv, qv, dov, mgv, lv)
    return [t.reshape(T, Wd) for t in res]


QKV_SLOTS = 3 * len(ATT_CONFIGS) * ATT_HEADS
SLOTS_PER_DEV = QKV_SLOTS // N_DEV


def _head_slots(t3):
    return [t3[s // SLOTS_PER_DEV][:, (s % SLOTS_PER_DEV) * HEAD_DIM:(s % SLOTS_PER_DEV + 1) * HEAD_DIM]
            for s in range(QKV_SLOTS)]


def _heads_of(slots, k):
    return jnp.concatenate(slots[k * ATT_HEADS:(k + 1) * ATT_HEADS], axis=1)


def _slots_to_blocked(slots):
    return jnp.stack([jnp.concatenate(slots[b * SLOTS_PER_DEV:(b + 1) * SLOTS_PER_DEV], axis=1)
                      for b in range(N_DEV)])


def _coords():
    return lax.axis_index("x"), lax.axis_index("y"), lax.axis_index("c")


def all_gather(name, xs):
    n = len(xs)

    def body(*refs):
        x_refs, out_refs = refs[:n], refs[n:2 * n]
        send_sems, recv_sems, local_sems = refs[2 * n:]
        x, y, c = _coords()
        me, sibling = (x, y, c), (x, y, 1 - c)
        chips = [(1 - x, y), (x, 1 - y), (1 - x, 1 - y)]

        def slot(a, px, py, pc):
            return out_refs[a].at[4 * px + 2 * py + pc]

        def copy(a, k, block, to, src=None):
            return pltpu.make_async_remote_copy(
                src_ref=slot(a, *block) if src is None else src, dst_ref=slot(a, *block),
                send_sem=send_sems.at[7 * a + k], recv_sem=recv_sems.at[7 * a + k],
                device_id=to, device_id_type=MESH)

        mine = [pltpu.make_async_copy(x_refs[a], slot(a, *me), local_sems.at[a]) for a in range(n)]
        for cp in mine:
            cp.start()
        first = []
        for a in range(n):
            first.append(copy(a, 0, me, sibling, src=x_refs[a]))
            first += [copy(a, 1 + j, me, (*chip, c), src=x_refs[a]) for j, chip in enumerate(chips)]
        for cp in first:
            cp.start()
        passed = []
        for j, chip in enumerate(chips):
            for a in range(n):
                copy(a, 1 + j, (*chip, c), me).wait_recv()
                cp = copy(a, 4 + j, (*chip, c), sibling)
                cp.start()
                passed.append(cp)
        for a in range(n):
            copy(a, 0, sibling, me).wait_recv()
            for j, chip in enumerate(chips):
                copy(a, 4 + j, (*chip, 1 - c), me).wait_recv()
        for cp in first + passed:
            cp.wait_send()
        for cp in mine:
            cp.wait()

    anyspec = pl.BlockSpec(memory_space=pl.ANY)
    return pl.pallas_call(
        body, out_shape=[jax.ShapeDtypeStruct((N_DEV,) + t.shape, t.dtype) for t in xs],
        in_specs=[anyspec] * n, out_specs=[anyspec] * n,
        scratch_shapes=[pltpu.SemaphoreType.DMA((7 * n,)), pltpu.SemaphoreType.DMA((7 * n,)),
                        pltpu.SemaphoreType.DMA((n,))],
        name=name,
    )(*xs)


def exchange_sibling(name, gs):
    n = len(gs)

    def body(*refs):
        g_refs, out_refs = refs[:n], refs[n:2 * n]
        send_sems, recv_sems = refs[2 * n:]
        x, y, c = _coords()
        sibling = (x, y, 1 - c)
        cps = []
        for a in range(n):
            for q in range(4):
                cps.append(pltpu.make_async_remote_copy(
                    src_ref=g_refs[a].at[2 * q + (1 - c)], dst_ref=out_refs[a].at[q],
                    send_sem=send_sems.at[4 * a + q], recv_sem=recv_sems.at[4 * a + q],
                    device_id=sibling, device_id_type=MESH))
        for cp in cps:
            cp.start()
        for cp in cps:
            cp.wait_recv()
        for cp in cps:
            cp.wait_send()

    anyspec = pl.BlockSpec(memory_space=pl.ANY)
    return pl.pallas_call(
        body, out_shape=[jax.ShapeDtypeStruct((4,) + g.shape[1:], g.dtype) for g in gs],
        in_specs=[anyspec] * n, out_specs=[anyspec] * n,
        scratch_shapes=[pltpu.SemaphoreType.DMA((4 * n,)), pltpu.SemaphoreType.DMA((4 * n,))],
        name=name,
    )(*gs)


def exchange_chips(name, ps):
    n = len(ps)

    def body(*refs):
        p_refs, out_refs = refs[:n], refs[n:2 * n]
        send_sems, recv_sems = refs[2 * n:]
        x, y, c = _coords()
        chips = [(1 - x, y), (x, 1 - y), (1 - x, 1 - y)]
        cps = []
        for a in range(n):
            for k, (px, py) in enumerate(chips):
                cps.append(pltpu.make_async_remote_copy(
                    src_ref=p_refs[a].at[2 * px + py], dst_ref=out_refs[a].at[k],
                    send_sem=send_sems.at[3 * a + k], recv_sem=recv_sems.at[3 * a + k],
                    device_id=(px, py, c), device_id_type=MESH))
        for cp in cps:
            cp.start()
        for cp in cps:
            cp.wait_recv()
        for cp in cps:
            cp.wait_send()

    anyspec = pl.BlockSpec(memory_space=pl.ANY)
    return pl.pallas_call(
        body, out_shape=[jax.ShapeDtypeStruct((3,) + p.shape[1:], p.dtype) for p in ps],
        in_specs=[anyspec] * n, out_specs=[anyspec] * n,
        scratch_shapes=[pltpu.SemaphoreType.DMA((3 * n,)), pltpu.SemaphoreType.DMA((3 * n,))],
        name=name,
    )(*ps)


def _row_tile(R):
    tr = 256
    while R % tr:
        tr //= 2
    assert tr % 8 == 0
    return tr


def add_sibling(name, g, recv, c_idx):
    _, R, C = g.shape
    tr = _row_tile(R)

    def body(c_ref, g_ref, r_ref, o_ref, o16_ref):
        s = g_ref[...] + r_ref[...].astype(F32)
        o_ref[...] = s
        o16_ref[...] = s.astype(BF16)

    out = pl.BlockSpec((None, tr, C), lambda q, i, cr: (q, i, 0))
    return pl.pallas_call(
        body,
        grid_spec=pltpu.PrefetchScalarGridSpec(
            num_scalar_prefetch=1, grid=(4, R // tr),
            in_specs=[pl.BlockSpec((None, tr, C), lambda q, i, cr: (2 * q + cr[0], i, 0)), out],
            out_specs=[out, out]),
        out_shape=[jax.ShapeDtypeStruct((4, R, C), F32), jax.ShapeDtypeStruct((4, R, C), BF16)], name=name,
        compiler_params=_cparams(("arbitrary", "arbitrary")),
    )(c_idx, g, recv)


def _adam_math(w, g, m, v):
    m = ADAM_B1 * m + (1.0 - ADAM_B1) * g
    v = ADAM_B2 * v + (1.0 - ADAM_B2) * jnp.square(g)
    m_hat = m / (1.0 - ADAM_B1 ** ADAM_STEP)
    v_hat = v / (1.0 - ADAM_B2 ** ADAM_STEP)
    delta = -ADAM_LR * (m_hat / (jnp.sqrt(v_hat) + ADAM_EPS) + ADAM_WD * w)
    return delta, m, v


def adam_big(name, p1, recv, w, m, v, chip_idx, layer=0):
    _, R, C = p1.shape
    tr = _row_tile(R)
    nt = R // tr

    def body(q_ref, p_ref, r_ref, w_ref, m_ref, v_ref, g_ref, d_ref, nm_ref, nv_ref):
        g = ((p_ref[...] + r_ref[0].astype(F32)) + r_ref[1].astype(F32)) + r_ref[2].astype(F32)
        d, nm, nv = _adam_math(w_ref[...], g, m_ref[...], v_ref[...])
        g_ref[...] = g
        d_ref[...] = d
        nm_ref[...] = nm
        nv_ref[...] = nv

    row_in = pl.BlockSpec((tr, C), lambda i, qr: (layer * nt + i, 0))
    row = pl.BlockSpec((tr, C), lambda i, qr: (i, 0))
    return pl.pallas_call(
        body,
        grid_spec=pltpu.PrefetchScalarGridSpec(
            num_scalar_prefetch=1, grid=(nt,),
            in_specs=[pl.BlockSpec((None, tr, C), lambda i, qr: (qr[0], i, 0)),
                      pl.BlockSpec((3, tr, C), lambda i, qr: (0, i, 0)), row_in, row_in, row_in],
            out_specs=[row, row, row, row]),
        out_shape=[jax.ShapeDtypeStruct((R, C), F32)] * 4, name=name,
        compiler_params=_cparams(("arbitrary",)),
    )(chip_idx, p1, recv, w, m, v)


def sum8(parts):
    _, R, C = parts.shape

    def body(p_ref, o_ref):
        acc = p_ref[0]
        for k in range(1, N_DEV):
            acc = acc + p_ref[k]
        o_ref[...] = acc

    tr = 128
    while R % tr:
        tr //= 2
    assert tr % 8 == 0
    return pl.pallas_call(
        body, grid=(R // tr,), in_specs=[pl.BlockSpec((N_DEV, tr, C), lambda i: (0, i, 0))],
        out_specs=pl.BlockSpec((tr, C), lambda i: (i, 0)), out_shape=jax.ShapeDtypeStruct((R, C), F32),
        name="sum8", compiler_params=_cparams(("arbitrary",)),
    )(parts)


def adam_small(w, g, m, v):
    def fn(wt, gt, mt, vt):
        return _adam_math(wt, gt, mt, vt)
    C = w.shape[1]
    return rowwise("adam_small", fn, [w, g, m, v], [], [(C, F32)] * 3, tr=128)


def _pack(arrs, rows_mult=8):
    flat = jnp.concatenate([a.reshape(-1) for a in arrs])
    n = flat.shape[0]
    per = PACK_C * rows_mult
    pad = (-n) % per
    if pad:
        flat = jnp.concatenate([flat, jnp.zeros((pad,), flat.dtype)])
    return flat.reshape(-1, PACK_C)


def _unpack(buf, shapes):
    flat = buf.reshape(-1)
    out, off = [], 0
    for s in shapes:
        n = math.prod(s)
        out.append(flat[off:off + n].reshape(s))
        off += n
    return out


def _blocked(gfull, axis):
    shp = gfull.shape
    n = shp[axis] // N_DEV
    t = gfull.reshape(shp[:axis] + (N_DEV, n) + shp[axis + 1:])
    t = jnp.moveaxis(t, axis, 0)
    return t.reshape(N_DEV, -1)


def _unblocked(gathered, shard_shape, axis):
    t = jnp.moveaxis(gathered, 0, axis)
    shp = shard_shape[:axis] + (N_DEV * shard_shape[axis],) + shard_shape[axis + 1:]
    return t.reshape(shp)


def _relu2_epi(acc):
    r = jnp.maximum(acc, 0.0)
    return acc, r * r


def _step(x3, target3, W, Wfull):
    bsz, S, D = x3.shape
    T = bsz * S
    x = x3.reshape(T, D)
    target = target3.reshape(T, D)
    row = lambda v: v.reshape(1, -1)
    grads = {}

    s5p = (W['ssm_a_re'][0], W['ssm_a_im'][0], W['ssm_log_dt'][0], W['ssm_b_re'][0], W['ssm_b_im'][0])
    (lam_re, lam_im, bb_re, bb_im), s5_disc_vjp = jax.vjp(s5_disc, *s5p)
    pwr, pwi, l2r, l2i = s5_tables(lam_re, lam_im, S5_SUB)
    bre, bim = _s5_blockdiag_b(bb_re).astype(BF16), _s5_blockdiag_b(bb_im).astype(BF16)
    cre, cim = _s5_blockdiag_c(W['ssm_c_re'][0]).astype(BF16), _s5_blockdiag_c(W['ssm_c_im'][0]).astype(BF16)
    dskip = W['ssm_d']

    tril = jnp.tril(jnp.ones((GMLP_CHUNK, GMLP_CHUNK), bool))
    ws = jnp.where(tril[None], W['gmlp_w_s'][0], 0.0).astype(BF16)
    hw = D // GMLP_HEADS
    sbias = jnp.repeat(W['gmlp_b_s'][0].T, hw, axis=1)

    conv_w = jnp.concatenate([W['conv_w_dw'][0], jnp.zeros((1, D), F32)], axis=0)

    saved = []
    for i in range(DEPTH):
        sv = {'x': x}
        nm = W['norm_mix'][i:i + 1]
        if i == 0:
            h, hf = rms_fwd("rms_mix0", x, nm, want_f32=True)
            ypre, gy, xs = s5_fwd(hf, bre, bim, cre, cim, pwr, pwi, l2r, l2i, dskip, bsz)
            z, = matmul("s5_glu_mm", gy, Wfull['ssm_w_glu'], mode='cb')
            x1, = rowwise("s5_glu", lambda zt, xt: xt + _glu(zt), [z, x], [], [(D, F32)])
            sv.update(hf=hf, ypre=ypre, gy=gy, xs=xs, z=z)
        elif i == 1:
            h, = rms_fwd("rms_mix1", x, nm)
            z, = matmul("conv_pw1", h, Wfull['conv_w_pw1'], mode='cb', epi=lambda acc, b: (acc + b,),
                        extras=[(W['conv_b_pw1'], 'row')])
            zg, = rowwise("conv_glu", _glu, [z], [], [(D, F32)])
            yc = dwconv_fwd(zg, conv_w, W['conv_b_dw'], bsz)
            y2, = rowwise("conv_ln_silu", lambda t, g, b: jax.nn.silu(_ln(t, g, b)), [yc],
                          [W['conv_ln_g'], W['conv_ln_b']], [(D, BF16)])
            x1, = matmul("conv_pw2", y2, Wfull['conv_w_pw2'], epi=lambda acc, b, r: (acc + b + r,),
                         extras=[(W['conv_b_pw2'], 'row'), (x, 'tile')])
            sv.update(h=h, z=z, zg=zg, yc=yc, y2=y2)
        elif i == 2:
            h, = rms_fwd("rms_mix2", x, nm)
            zp, = matmul("gmlp_in", h, Wfull['gmlp_w_in'], mode='cb')

            def gm_pre(zt, g, b):
                a = jax.nn.gelu(zt)
                return a[:, :D], _ln(a[:, D:], g, b)
            u, vln = rowwise("gmlp_pre", gm_pre, [zp], [W['gmlp_ln_g'], W['gmlp_ln_b']], [(D, F32), (D, BF16)])
            gated = spatial_fwd(u, vln, ws, sbias)
            x1, = matmul("gmlp_out", gated, Wfull['gmlp_w_out'], epi=lambda acc, r: (acc + r,), extras=[(x, 'tile')])
            sv.update(h=h, zp=zp, u=u, vln=vln, gated=gated)
        else:
            h, = rms_fwd("rms_mix3", x, nm)
            qkv3, = matmul("attn_qkv", h, Wfull['attn_w_qkv'], mode='cb', out_dtypes=(BF16,), out3=True)
            slots = _head_slots(qkv3)
            ng = len(ATT_CONFIGS)
            outs, lses, blocks = [], [], []
            for gi, (window, dil) in enumerate(ATT_CONFIGS):
                qb, kb, vb = (_heads_of(slots, j * ng + gi) for j in range(3))
                ob, lb = attn_fwd("attn_fwd%d" % gi, qb, kb, vb, dil, bsz)
                blocks.append((qb, kb, vb, lb, dil))
                outs.append(ob)
                lses.append(lb)

            def merge(o0, o1, o2, l0, l1, l2):
                m = jnp.maximum(jnp.maximum(l0, l1), l2)
                e0, e1, e2 = jnp.exp(l0 - m), jnp.exp(l1 - m), jnp.exp(l2 - m)
                inv = 1.0 / (e0 + e1 + e2)
                w0, w1, w2 = e0 * inv, e1 * inv, e2 * inv
                return w0 * o0 + w1 * o1 + w2 * o2, w0, w1, w2
            merged, w0, w1, w2 = rowwise("attn_merge", merge, outs + lses, [],
                                         [(ATT_W, BF16), (ATT_W, F32), (ATT_W, F32), (ATT_W, F32)])
            x1, = matmul("attn_o", merged, Wfull['attn_w_o'], mode='cb', epi=lambda acc, r: (acc + r,),
                         extras=[(x, 'tile')])
            sv.update(h=h, blocks=blocks, merged=merged, wts=(w0, w1, w2))
        h2, = rms_fwd("rms_mlp%d" % i, x1, W['norm_mlp'][i:i + 1])
        a, act = matmul("mlp_in%d" % i, h2, Wfull['mlp_w_in'][i], mode='cb', epi=_relu2_epi, out_dtypes=(BF16, BF16))
        x2, = matmul("mlp_out%d" % i, act, Wfull['mlp_w_out'][i], epi=lambda acc, r: (acc + r,), extras=[(x1, 'tile')])
        sv.update(x1=x1, h2=h2, a=a, act=act)
        saved.append(sv)
        x = x2

    def loss_fn(xt, tt, g):
        y, vjp = jax.vjp(_rms, xt, g)
        err = y - tt
        dxx, dg = vjp(err * (1.0 / D))
        lval = jnp.sum(jnp.sum(err * err, axis=1, keepdims=True), axis=0, keepdims=True) * (0.5 / D)
        return dxx, dxx, jnp.broadcast_to(lval, (1, 128)), dg
    dx, dxb, lacc, dnf = rowwise("loss_head", loss_fn, [x, target], [row(W['norm_final'])],
                                 [(D, F32), (D, BF16)], [((1, 128), F32), ((1, D), F32)])
    loss_local = lacc[0, 0]
    grads['norm_final'] = dnf.reshape(-1)

    g_norm_mix, g_norm_mlp = [None] * DEPTH, [None] * DEPTH
    g_mlp_in, g_mlp_out = [None] * DEPTH, [None] * DEPTH
    for i in reversed(range(DEPTH)):
        sv = saved[i]
        da, = matmul("mlp_out_bwd%d" % i, dxb, Wfull['mlp_w_out'][i], mode='nt',
                     epi=lambda acc, av: (acc * (2.0 * jnp.maximum(av.astype(F32), 0.0)),),
                     extras=[(sv['a'], 'tile')], out_dtypes=(BF16,))
        g_mlp_out[i] = _rows_blocked(wgrad("mlp_out_wg%d" % i, sv['act'], dxb))
        dh2, = matmul_nt_cb("mlp_in_bwd%d" % i, da, Wfull['mlp_w_in'][i])
        g_mlp_in[i] = wgrad("mlp_in_wg%d" % i, sv['h2'], da, cb=True)
        dx, dxb, dg = rms_bwd("rms_mlp_bwd%d" % i, sv['x1'], dh2, dx, W['norm_mlp'][i:i + 1])
        g_norm_mlp[i] = dg.reshape(-1)
        xin = sv['x']
        if i == 0:
            dz, = rowwise("s5_glu_bwd", _glu_bwd, [sv['z'], dx], [], [(2 * D, BF16)])
            dgy, = matmul_nt_cb("s5_glu_mm_bwd", dz, Wfull['ssm_w_glu'])
            grads['ssm_w_glu'] = wgrad("s5_glu_wg", sv['gy'], dz, cb=True)

            def gelu_bwd(yt, dt):
                _, vjp = jax.vjp(jax.nn.gelu, yt)
                return vjp(dt)[0]
            dypre, = rowwise("s5_gelu_bwd", gelu_bwd, [sv['ypre'], dgy], [], [(D, F32)])
            du, dbr, dbi, dcr, dci, dl, dd = s5_bwd(sv['hf'], dypre, sv['xs'], bre, bim, cre, cim, pwr, pwi,
                                                    pwr[:, ::-1], pwi[:, ::-1], l2r, l2i, dskip, bsz)
            dlam_re = dl[:, 0, :].reshape(SSM_GROUPS, SSM_STATE)
            dlam_im = dl[:, 1, :].reshape(SSM_GROUPS, SSM_STATE)
            s5_cot = (dlam_re, dlam_im, _s5_blockdiag_b_inv(dbr), _s5_blockdiag_b_inv(dbi))
            grads['ssm_c_re'] = _s5_blockdiag_c_inv(dcr)[None]
            grads['ssm_c_im'] = _s5_blockdiag_c_inv(dci)[None]
            grads['ssm_d'] = dd[0:1]
            dh = du
        elif i == 1:
            dy2, = matmul("conv_pw2_bwd", dxb, Wfull['conv_w_pw2'], mode='nt')
            grads['conv_w_pw2'] = _rows_blocked(wgrad("conv_pw2_wg", sv['y2'], dxb))

            def ln_silu_bwd(yt, dt, dxt, g, b):
                _, vjp = jax.vjp(lambda t, gg, bb: jax.nn.silu(_ln(t, gg, bb)), yt, g, b)
                dyc, dgg, dbb = vjp(dt)
                return dyc, dgg, dbb, _colsum(dxt)
            dyc, dlg, dlb, dbp2 = rowwise("conv_ln_silu_bwd", ln_silu_bwd, [sv['yc'], dy2, dx],
                                          [W['conv_ln_g'], W['conv_ln_b']], [(D, F32)],
                                          [((1, D), F32), ((1, D), F32), ((1, D), F32)])
            grads['conv_ln_g'], grads['conv_ln_b'], grads['conv_b_pw2'] = dlg, dlb, dbp2
            dzg, dwd, dbd = dwconv_bwd(sv['zg'], dyc, conv_w, bsz)
            grads['conv_w_dw'] = dwd[None, :CONV_WIDTH]
            grads['conv_b_dw'] = dbd[0:1]

            def glu_bwd1(zt, dyt):
                dzt = _glu_bwd(zt, dyt)
                return dzt, _colsum(dzt)
            dz, dbp1 = rowwise("conv_glu_bwd", glu_bwd1, [sv['z'], dzg], [], [(2 * D, BF16)], [((1, 2 * D), F32)])
            grads['conv_b_pw1'] = dbp1
            dh, = matmul_nt_cb("conv_pw1_bwd", dz, Wfull['conv_w_pw1'])
            grads['conv_w_pw1'] = wgrad("conv_pw1_wg", sv['h'], dz, cb=True)
        elif i == 2:
            dgt, = matmul("gmlp_out_bwd", dxb, Wfull['gmlp_w_out'], mode='nt', out_dtypes=(BF16,))
            grads['gmlp_w_out'] = _rows_blocked(wgrad("gmlp_out_wg", sv['gated'], dxb))
            du, dvln, dws, dsb = spatial_bwd(sv['u'], sv['vln'], dgt, ws, sbias)
            grads['gmlp_w_s'] = dws[None]
            grads['gmlp_b_s'] = dsb.reshape(GMLP_CHUNK, GMLP_HEADS, hw).sum(-1).T[None]

            def gm_pre_bwd(zt, dut, dvt, g, b):
                _, vjp_u = jax.vjp(jax.nn.gelu, zt[:, :D])
                _, vjp_v = jax.vjp(lambda zz, gg, bb: _ln(jax.nn.gelu(zz), gg, bb), zt[:, D:], g, b)
                dz2, dgg, dbb = vjp_v(dvt)
                return jnp.concatenate([vjp_u(dut)[0], dz2], axis=1), dgg, dbb
            dzp, dlg, dlb = rowwise("gmlp_pre_bwd", gm_pre_bwd, [sv['zp'], du, dvln],
                                    [W['gmlp_ln_g'], W['gmlp_ln_b']], [(2 * D, BF16)], [((1, D), F32), ((1, D), F32)])
            grads['gmlp_ln_g'], grads['gmlp_ln_b'] = dlg, dlb
            dh, = matmul_nt_cb("gmlp_in_bwd", dzp, Wfull['gmlp_w_in'])
            grads['gmlp_w_in'] = wgrad("gmlp_in_wg", sv['h'], dzp, cb=True)
        else:
            dm, = matmul_nt_cb("attn_o_bwd", dxb, Wfull['attn_w_o'])
            grads['attn_w_o'] = wgrad("attn_o_wg", sv['merged'], dxb, cb=True)
            w0, w1, w2 = sv['wts']
            do0, do1, do2 = rowwise("attn_merge_bwd", lambda d, a, b, c: (a * d, b * d, c * d), [dm, w0, w1, w2], [],
                                    [(ATT_W, BF16)] * 3)
            dparts = [[None] * 3 for _ in range(3)]
            for gi, (dog, (qb, kb, vb, lb, dil)) in enumerate(zip((do0, do1, do2), sv['blocks'])):
                dqb, dkb, dvb = attn_bwd("attn_bwd%d" % gi, qb, kb, vb, dog, sv['merged'], lb, dil, bsz)
                for j, t in enumerate((dqb, dkb, dvb)):
                    dparts[j][gi] = t
            dslots = [dparts[j][gi][:, hh * HEAD_DIM:(hh + 1) * HEAD_DIM]
                      for j in range(3) for gi in range(3) for hh in range(ATT_HEADS)]
            dqkv3 = _slots_to_blocked(dslots)
            dh, = matmul_nt_cb("attn_qkv_bwd", dqkv3, Wfull['attn_w_qkv'], a3=True)
            grads['attn_w_qkv'] = wgrad("attn_qkv_wg", sv['h'], dqkv3, cb=True, g3=True)
        dx, dxb, dg = rms_bwd("rms_mix_bwd%d" % i, xin, dh, dx, W['norm_mix'][i:i + 1])
        g_norm_mix[i] = dg.reshape(-1)

    grads['norm_mix'] = jnp.stack(g_norm_mix)
    grads['norm_mlp'] = jnp.stack(g_norm_mlp)
    grads['mlp_w_in'] = g_mlp_in
    grads['mlp_w_out'] = g_mlp_out
    return loss_local, dx.reshape(bsz, S, D), grads, (s5_disc_vjp, s5_cot)


def _rows_blocked(pair):
    return tuple(t.reshape(N_DEV, t.shape[0] // N_DEV, t.shape[1]) for t in pair)


def kernel(x, norm_mix, norm_mlp, norm_final, ssm_a_re, ssm_a_im, ssm_b_re, ssm_b_im, ssm_c_re, ssm_c_im, ssm_d, ssm_log_dt, ssm_w_glu, conv_w_pw1, conv_b_pw1, conv_w_dw, conv_b_dw, conv_ln_g, conv_ln_b, conv_w_pw2, conv_b_pw2, gmlp_w_in, gmlp_ln_g, gmlp_ln_b, gmlp_w_s, gmlp_b_s, gmlp_w_out, attn_w_qkv, attn_w_o, mlp_w_in, mlp_w_out, loss_target, m_norm_mix, m_norm_mlp, m_norm_final, m_ssm_a_re, m_ssm_a_im, m_ssm_b_re, m_ssm_b_im, m_ssm_c_re, m_ssm_c_im, m_ssm_d, m_ssm_log_dt, m_ssm_w_glu, m_conv_w_pw1, m_conv_b_pw1, m_conv_w_dw, m_conv_b_dw, m_conv_ln_g, m_conv_ln_b, m_conv_w_pw2, m_conv_b_pw2, m_gmlp_w_in, m_gmlp_ln_g, m_gmlp_ln_b, m_gmlp_w_s, m_gmlp_b_s, m_gmlp_w_out, m_attn_w_qkv, m_attn_w_o, m_mlp_w_in, m_mlp_w_out, v_norm_mix, v_norm_mlp, v_norm_final, v_ssm_a_re, v_ssm_a_im, v_ssm_b_re, v_ssm_b_im, v_ssm_c_re, v_ssm_c_im, v_ssm_d, v_ssm_log_dt, v_ssm_w_glu, v_conv_w_pw1, v_conv_b_pw1, v_conv_w_dw, v_conv_b_dw, v_conv_ln_g, v_conv_ln_b, v_conv_w_pw2, v_conv_b_pw2, v_gmlp_w_in, v_gmlp_ln_g, v_gmlp_ln_b, v_gmlp_w_s, v_gmlp_b_s, v_gmlp_w_out, v_attn_w_qkv, v_attn_w_o, v_mlp_w_in, v_mlp_w_out):
    args = (norm_mix, norm_mlp, norm_final, ssm_a_re, ssm_a_im, ssm_b_re, ssm_b_im, ssm_c_re, ssm_c_im, ssm_d,
            ssm_log_dt, ssm_w_glu, conv_w_pw1, conv_b_pw1, conv_w_dw, conv_b_dw, conv_ln_g, conv_ln_b, conv_w_pw2,
            conv_b_pw2, gmlp_w_in, gmlp_ln_g, gmlp_ln_b, gmlp_w_s, gmlp_b_s, gmlp_w_out, attn_w_qkv, attn_w_o,
            mlp_w_in, mlp_w_out)
    margs = (m_norm_mix, m_norm_mlp, m_norm_final, m_ssm_a_re, m_ssm_a_im, m_ssm_b_re, m_ssm_b_im, m_ssm_c_re,
             m_ssm_c_im, m_ssm_d, m_ssm_log_dt, m_ssm_w_glu, m_conv_w_pw1, m_conv_b_pw1, m_conv_w_dw, m_conv_b_dw,
             m_conv_ln_g, m_conv_ln_b, m_conv_w_pw2, m_conv_b_pw2, m_gmlp_w_in, m_gmlp_ln_g, m_gmlp_ln_b,
             m_gmlp_w_s, m_gmlp_b_s, m_gmlp_w_out, m_attn_w_qkv, m_attn_w_o, m_mlp_w_in, m_mlp_w_out)
    vargs = (v_norm_mix, v_norm_mlp, v_norm_final, v_ssm_a_re, v_ssm_a_im, v_ssm_b_re, v_ssm_b_im, v_ssm_c_re,
             v_ssm_c_im, v_ssm_d, v_ssm_log_dt, v_ssm_w_glu, v_conv_w_pw1, v_conv_b_pw1, v_conv_w_dw, v_conv_b_dw,
             v_conv_ln_g, v_conv_ln_b, v_conv_w_pw2, v_conv_b_pw2, v_gmlp_w_in, v_gmlp_ln_g, v_gmlp_ln_b,
             v_gmlp_w_s, v_gmlp_b_s, v_gmlp_w_out, v_attn_w_qkv, v_attn_w_o, v_mlp_w_in, v_mlp_w_out)
    Wl = dict(zip(WEIGHT_NAMES, args))
    Ml = dict(zip(WEIGHT_NAMES, margs))
    Vl = dict(zip(WEIGHT_NAMES, vargs))
    cx, cy, cc = _coords()
    my_idx = 4 * cx + 2 * cy + cc

    c_idx = cc.reshape(1).astype(jnp.int32)
    chip_idx = (2 * cx + cy).reshape(1).astype(jnp.int32)
    units = []
    for n in BIG:
        units += [(n, i) for i in range(DEPTH)] if Wl[n].shape[0] == DEPTH else [(n, None)]
    ss_names = list(SMALL_SHARDED)
    spack = _pack([Wl[n] for n in ss_names])
    gathered = all_gather("gather_weights", [Wl[n][0 if i is None else i].astype(BF16) for n, i in units] + [spack])
    Wfull = {}
    for (n, i), g in zip(units, gathered):
        w = g if BIG[n] == 2 else g.reshape(N_DEV * g.shape[1], g.shape[2])
        if i is None:
            Wfull[n] = w
        else:
            Wfull.setdefault(n, []).append(w)
    sparts = _unpack_gathered(gathered[-1], [Wl[n].shape for n in ss_names])
    W = {n: Wl[n] for n in SMALL if n not in SMALL_SHARDED}
    for n, p in zip(ss_names, sparts):
        W[n] = _unblocked(p, Wl[n].shape, SMALL_SHARDED[n])

    loss_local, grad_x, grads, (s5_disc_vjp, s5_cot) = _step(x, loss_target, W, Wfull)
    loss = lax.psum(loss_local, MESH_AXES)

    pairs = [grads[n] if i is None else grads[n][i] for n, i in units]
    tag = lambda n, i: n if i is None else "%s%d" % (n, i)
    recv1 = exchange_sibling("rs_sibling", [p[1] for p in pairs])
    p1 = [add_sibling("add_sibling_" + tag(n, i), p[0], r, c_idx) for (n, i), p, r in zip(units, pairs, recv1)]
    recv2 = exchange_chips("rs_chips", [p[1] for p in p1])
    outs4 = {}
    for (n, i), p, r in zip(units, p1, recv2):
        w2, m2, v2 = (d[n].reshape(-1, d[n].shape[-1]) for d in (Wl, Ml, Vl))
        res = adam_big("adam_" + tag(n, i), p[0], r, w2, m2, v2, chip_idx, layer=0 if i is None else i)
        if i is None:
            outs4[n] = [t.reshape(Wl[n].shape) for t in res]
        else:
            outs4.setdefault(n, []).append(res)
    for n in BIG:
        if Wl[n].shape[0] == DEPTH:
            outs4[n] = [jnp.stack([layer[k] for layer in outs4[n]]) for k in range(4)]
    out_g = {n: outs4[n][0] for n in BIG}
    out_d = {n: outs4[n][1] for n in BIG}
    out_m = {n: outs4[n][2] for n in BIG}
    out_v = {n: outs4[n][3] for n in BIG}

    s5_lin = ['ssm_a_re', 'ssm_a_im', 'ssm_log_dt', 'ssm_b_re', 'ssm_b_im']
    direct = [n for n in SMALL if n not in s5_lin]
    full_shape = lambda n: W[n].shape
    small_parts = [grads[n].reshape(full_shape(n)) for n in direct] + list(s5_cot)
    gsum = sum8(all_gather("gather_small_grads", [_pack(small_parts)])[0])
    summed = _unpack(gsum, [p.shape for p in small_parts])
    gsmall = dict(zip(direct, summed[:len(direct)]))
    s5g = s5_disc_vjp(tuple(summed[len(direct):]))
    for n, gval in zip(s5_lin, s5g):
        gsmall[n] = gval[None]
    for n, ax in SMALL_SHARDED.items():
        gsmall[n] = lax.dynamic_slice_in_dim(gsmall[n], my_idx * Wl[n].shape[ax], Wl[n].shape[ax], axis=ax)
    sm_shapes = [Wl[n].shape for n in SMALL]
    dS, mS, vS = adam_small(_pack([Wl[n] for n in SMALL]), _pack([gsmall[n] for n in SMALL]),
                            _pack([Ml[n] for n in SMALL]), _pack([Vl[n] for n in SMALL]))
    for n, gval in zip(SMALL, [gsmall[n] for n in SMALL]):
        out_g[n] = gval.reshape(Wl[n].shape)
    out_d.update(zip(SMALL, _unpack(dS, sm_shapes)))
    out_m.update(zip(SMALL, _unpack(mS, sm_shapes)))
    out_v.update(zip(SMALL, _unpack(vS, sm_shapes)))

    return (loss, grad_x, *[out_g[n] for n in WEIGHT_NAMES], *[out_d[n] for n in WEIGHT_NAMES],
            *[out_m[n] for n in WEIGHT_NAMES], *[out_v[n] for n in WEIGHT_NAMES])


def _unpack_gathered(g, shard_shapes):
    flat = g.reshape(N_DEV, -1)
    out, off = [], 0
    for s in shard_shapes:
        n = math.prod(s)
        out.append(flat[:, off:off + n].reshape((N_DEV,) + tuple(s)))
        off += n
    return out
```

```python
import functools
import math

import jax
import jax.numpy as jnp
from jax import lax
from jax.experimental import pallas as pl
from jax.experimental.pallas import tpu as pltpu

F32 = jnp.float32
BF16 = jnp.bfloat16

D_MODEL = 1024
DEPTH = 4
EPS = 1e-6
SSM_GROUP = 16
SSM_GROUPS = 64
SSM_STATE = 64
S5_GB = 8
S5_NGB = SSM_GROUPS // S5_GB
S5_CH = S5_GB * SSM_GROUP
S5_ST = S5_GB * SSM_STATE
S5_L = 128
CONV_WIDTH = 31
CONV_PAD = 32
CONV_TS = 256
CONV_CW = 256
GMLP_CHUNK = 128
GMLP_HEADS = 4
ATT_CONFIGS = ((128, 1), (512, 4), (2048, 16))
ATT_HEADS = 8
HEAD_DIM = 64
ATT_BLK = 128
ATT_TB = 2
ATT_W = ATT_HEADS * HEAD_DIM
N_DEV = 8
ADAM_LR = 0.001
ADAM_B1 = 0.9
ADAM_B2 = 0.999
ADAM_EPS = 1e-08
ADAM_WD = 0.01
ADAM_STEP = 10
VMEM_LIMIT = 56 * 1024 * 1024
PACK_C = 1024
MESH_AXES = ("x", "y", "c")
MESH = pl.DeviceIdType.MESH

WEIGHT_NAMES = ['norm_mix', 'norm_mlp', 'norm_final', 'ssm_a_re', 'ssm_a_im', 'ssm_b_re', 'ssm_b_im',
                'ssm_c_re', 'ssm_c_im', 'ssm_d', 'ssm_log_dt', 'ssm_w_glu', 'conv_w_pw1', 'conv_b_pw1',
                'conv_w_dw', 'conv_b_dw', 'conv_ln_g', 'conv_ln_b', 'conv_w_pw2', 'conv_b_pw2',
                'gmlp_w_in', 'gmlp_ln_g', 'gmlp_ln_b', 'gmlp_w_s', 'gmlp_b_s', 'gmlp_w_out',
                'attn_w_qkv', 'attn_w_o', 'mlp_w_in', 'mlp_w_out']
BIG = {'ssm_w_glu': 2, 'conv_w_pw1': 2, 'conv_w_pw2': 1, 'gmlp_w_in': 2, 'gmlp_w_out': 1,
       'attn_w_qkv': 2, 'attn_w_o': 2, 'mlp_w_in': 2, 'mlp_w_out': 1}
SMALL_SHARDED = {'conv_b_pw1': 1, 'conv_w_dw': 2, 'conv_b_dw': 1, 'conv_ln_g': 1, 'conv_ln_b': 1,
                 'conv_b_pw2': 1, 'gmlp_ln_g': 1, 'gmlp_ln_b': 1}
SMALL = [n for n in WEIGHT_NAMES if n not in BIG]


def _cparams(sem=None):
    return pltpu.CompilerParams(dimension_semantics=sem, vmem_limit_bytes=VMEM_LIMIT)


def _dot(a, b):
    return jnp.dot(a, b, preferred_element_type=F32)


def _dot_nt(a, b):
    return lax.dot_general(a, b, (((1,), (1,)), ((), ())), preferred_element_type=F32)


def _dot_tn(a, b):
    return lax.dot_general(a, b, (((0,), (0,)), ((), ())), preferred_element_type=F32)


ROW_TILE_BYTES = 10 << 20


def _rows_for(T, row_bytes, cap=1024):
    tr = min(cap, T)
    while tr > 8 and (T % tr or tr * row_bytes > ROW_TILE_BYTES):
        tr //= 2
    assert T % tr == 0 and tr % 8 == 0
    return tr


def rowwise(name, fn, rows, params, row_out, acc_out=(), tr=None):
    T = rows[0].shape[0]
    row_bytes = (sum(r.shape[1] * r.dtype.itemsize for r in rows)
                 + sum(c * jnp.dtype(dt).itemsize for c, dt in row_out))
    tr = _rows_for(T, row_bytes, cap=tr or 1024)
    nr, npar, nro = len(rows), len(params), len(row_out)

    def body(*refs):
        ins = [r[...] for r in refs[:nr + npar]]
        outs = refs[nr + npar:]
        res = fn(*ins)
        if not isinstance(res, (tuple, list)):
            res = (res,)
        for k in range(nro):
            outs[k][...] = res[k].astype(outs[k].dtype)
        if acc_out:
            @pl.when(pl.program_id(0) == 0)
            def _():
                for k in range(nro, len(outs)):
                    outs[k][...] = jnp.zeros_like(outs[k])
            for k in range(nro, len(outs)):
                outs[k][...] += res[k].astype(outs[k].dtype)

    in_specs = [pl.BlockSpec((tr, r.shape[1]), lambda i: (i, 0)) for r in rows]
    in_specs += [pl.BlockSpec(p.shape, lambda i, nd=p.ndim: (0,) * nd) for p in params]
    out_shape = [jax.ShapeDtypeStruct((T, c), dt) for c, dt in row_out]
    out_specs = [pl.BlockSpec((tr, c), lambda i: (i, 0)) for c, dt in row_out]
    out_shape += [jax.ShapeDtypeStruct(s, dt) for s, dt in acc_out]
    out_specs += [pl.BlockSpec(s, lambda i, nd=len(s): (0,) * nd) for s, dt in acc_out]
    res = pl.pallas_call(body, grid=(T // tr,), in_specs=in_specs, out_specs=out_specs, out_shape=out_shape,
                         name=name, compiler_params=_cparams(("arbitrary",)))(*rows, *params)
    return res


def _tile_m(M, K):
    tm = 2048
    while tm > 256 and tm * K * 2 > (4 << 20):
        tm //= 2
    return min(tm, M)


def matmul(name, a, b, *, mode='nn', epi=None, extras=(), out_dtypes=(F32,), out3=False):
    M, K = a.shape
    if mode == 'cb':
        nblk, _, tn = b.shape
        N = nblk * tn
    else:
        N = b.shape[0] if mode == 'nt' else b.shape[1]
        tn = min(512, N)
    row_bytes = (K * 2 + sum(N * jnp.dtype(dt).itemsize for dt in out_dtypes)
                 + sum(N * arr.dtype.itemsize for arr, kind in extras if kind == 'tile'))
    tm = _rows_for(M, row_bytes)
    assert N % tn == 0, (M, N, tm, tn)
    nex = len(extras)

    def body(a_ref, b_ref, *rest):
        ex_refs, outs = rest[:nex], rest[nex:]
        av = a_ref[...]
        for c in range(N // tn):
            cs = slice(c * tn, (c + 1) * tn)
            if mode == 'cb':
                acc = _dot(av, b_ref[c])
            elif mode == 'nt':
                acc = _dot_nt(av, b_ref[cs, :])
            else:
                acc = _dot(av, b_ref[:, cs])
            res = epi(acc, *[e[:, cs] for e in ex_refs]) if epi is not None else (acc,)
            for o, r in zip(outs, res):
                if out3:
                    o[c] = r.astype(o.dtype)
                else:
                    o[:, cs] = r.astype(o.dtype)

    in_specs = [pl.BlockSpec((tm, K), lambda i: (i, 0)), pl.BlockSpec(b.shape, lambda i, nd=b.ndim: (0,) * nd)]
    for arr, kind in extras:
        in_specs.append(pl.BlockSpec((tm, N), lambda i: (i, 0)) if kind == 'tile'
                        else pl.BlockSpec((1, N), lambda i: (0, 0)))
    if out3:
        out_shape = [jax.ShapeDtypeStruct((N // tn, M, tn), dt) for dt in out_dtypes]
        out_specs = [pl.BlockSpec((N // tn, tm, tn), lambda i: (0, i, 0)) for dt in out_dtypes]
    else:
        out_shape = [jax.ShapeDtypeStruct((M, N), dt) for dt in out_dtypes]
        out_specs = [pl.BlockSpec((tm, N), lambda i: (i, 0)) for dt in out_dtypes]
    return pl.pallas_call(body, grid=(M // tm,), in_specs=in_specs, out_specs=out_specs,
                          out_shape=out_shape, name=name,
                          compiler_params=_cparams(("arbitrary",)))(a, b, *[e[0] for e in extras])


def matmul_nt_cb(name, a, b, *, a3=False, epi=None, extras=(), out_dtypes=(F32,)):
    nblk, K, n = b.shape
    M = a.shape[1] if a3 else a.shape[0]
    tm = _tile_m(M, nblk * n)
    assert M % tm == 0
    nex = len(extras)

    def body(a_ref, b_ref, *rest):
        ex, outs = rest[:nex], rest[nex:]
        acc = None
        for j in range(nblk):
            aj = a_ref[j] if a3 else a_ref[:, j * n:(j + 1) * n]
            part = _dot_nt(aj, b_ref[j])
            acc = part if acc is None else acc + part
        res = epi(acc, *[e[...] for e in ex]) if epi is not None else (acc,)
        for o, r in zip(outs, res):
            o[...] = r.astype(o.dtype)

    a_spec = (pl.BlockSpec((nblk, tm, n), lambda i: (0, i, 0)) if a3
              else pl.BlockSpec((tm, nblk * n), lambda i: (i, 0)))
    row = pl.BlockSpec((tm, K), lambda i: (i, 0))
    return pl.pallas_call(
        body, grid=(M // tm,),
        in_specs=[a_spec, pl.BlockSpec((nblk, K, n), lambda i: (0, 0, 0))] + [row] * nex,
        out_specs=[row] * len(out_dtypes), out_shape=[jax.ShapeDtypeStruct((M, K), dt) for dt in out_dtypes],
        name=name, compiler_params=_cparams(("arbitrary",)))(a, b, *extras)


def wgrad(name, a, g, *, cb=False, g3=False):
    M, K = a.shape
    tm, tk = min(M, 1024), min(K, 1024)
    if cb:
        n = g.shape[2] if g3 else g.shape[1] // N_DEV
        nj = N_DEV
        while nj > 1 and nj * tk * n * 6 > (14 << 20):
            nj //= 2
        grid = (K // tk, N_DEV // nj, M // tm)
        g_spec = (pl.BlockSpec((nj, tm, n), lambda k, j, m: (j, m, 0)) if g3
                  else pl.BlockSpec((tm, nj * n), lambda k, j, m: (m, j)))
        o_spec = pl.BlockSpec((nj, tk, n), lambda k, j, m: (j, k, 0))
        o_shape = (N_DEV, K, n)
    else:
        N = g.shape[1]
        tn = min(N, 1024)
        nj = 1
        grid = (K // tk, N // tn, M // tm)
        g_spec = pl.BlockSpec((tm, tn), lambda k, j, m: (m, j))
        o_spec = pl.BlockSpec((tk, tn), lambda k, j, m: (k, j))
        o_shape = (K, N)
    nm = M // tm

    def body(a_ref, g_ref, o_ref, o16_ref):
        m = pl.program_id(2)

        @pl.when(m == 0)
        def _():
            o_ref[...] = jnp.zeros_like(o_ref)
        at = a_ref[...].T
        if cb:
            for jj in range(nj):
                gj = g_ref[jj] if g3 else g_ref[:, jj * n:(jj + 1) * n]
                o_ref[jj] += _dot(at, gj)
        else:
            o_ref[...] += _dot(at, g_ref[...])

        @pl.when(m == nm - 1)
        def _():
            o16_ref[...] = o_ref[...].astype(BF16)

    return pl.pallas_call(
        body, grid=grid, in_specs=[pl.BlockSpec((tm, tk), lambda k, j, m: (m, k)), g_spec],
        out_specs=[o_spec, o_spec],
        out_shape=[jax.ShapeDtypeStruct(o_shape, F32), jax.ShapeDtypeStruct(o_shape, BF16)], name=name,
        compiler_params=_cparams(("arbitrary", "arbitrary", "arbitrary")))(a, g)


def _rms(x, g):
    x = x.astype(F32)
    return x * lax.rsqrt(jnp.mean(x * x, axis=-1, keepdims=True) + EPS) * g


def _ln(x, g, b):
    mu = jnp.mean(x, axis=-1, keepdims=True)
    var = jnp.mean(jnp.square(x - mu), axis=-1, keepdims=True)
    return (x - mu) * lax.rsqrt(var + EPS) * g + b


def _glu(z):
    d = z.shape[1] // 2
    return z[:, :d] * jax.nn.sigmoid(z[:, d:])


def _glu_bwd(z, dy):
    d = z.shape[1] // 2
    a, s = z[:, :d], jax.nn.sigmoid(z[:, d:])
    return jnp.concatenate([dy * s, dy * a * s * (1.0 - s)], axis=1)


def _colsum(v):
    return jnp.sum(v.astype(F32), axis=0, keepdims=True)


def rms_fwd(name, x, g, want_f32=False):
    def fn(xt, gt):
        h = _rms(xt, gt)
        return (h, h) if want_f32 else (h,)
    D = x.shape[1]
    outs = [(D, BF16)] + ([(D, F32)] if want_f32 else [])
    return rowwise(name, fn, [x], [g], outs)


def rms_bwd(name, x, dh, dres, g):
    def fn(xt, dht, drt, gt):
        _, vjp = jax.vjp(_rms, xt, gt)
        dx, dg = vjp(dht.astype(F32))
        dx = dx + drt
        return dx, dx, dg
    D = x.shape[1]
    return rowwise(name, fn, [x, dh, dres], [g], [(D, F32), (D, BF16)], [((1, D), F32)])


def s5_disc(a_re, a_im, log_dt, b_re, b_im):
    dt = jnp.exp(log_dt)[:, None]
    er = jnp.exp(a_re * dt)
    lam_re = er * jnp.cos(a_im * dt)
    lam_im = er * jnp.sin(a_im * dt)
    nr, ni = lam_re - 1.0, lam_im
    den = a_re * a_re + a_im * a_im
    f_re = (nr * a_re + ni * a_im) / den
    f_im = (ni * a_re - nr * a_im) / den
    bb_re = f_re[..., None] * b_re - f_im[..., None] * b_im
    bb_im = f_re[..., None] * b_im + f_im[..., None] * b_re
    return lam_re, lam_im, bb_re, bb_im


def _s5_blockdiag_b(bb):
    t = bb.reshape(S5_NGB, S5_GB, SSM_STATE, SSM_GROUP).transpose(0, 1, 3, 2)
    eye = jnp.eye(S5_GB, dtype=bb.dtype)
    return jnp.einsum('bgpn,gh->bgphn', t, eye).reshape(S5_NGB, S5_CH, S5_ST)


def _s5_blockdiag_b_inv(x):
    t = x.reshape(S5_NGB, S5_GB, SSM_GROUP, S5_GB, SSM_STATE)
    eye = jnp.eye(S5_GB, dtype=x.dtype)
    d = jnp.einsum('bgphn,gh->bgpn', t, eye)
    return d.transpose(0, 1, 3, 2).reshape(SSM_GROUPS, SSM_STATE, SSM_GROUP)


def _s5_blockdiag_c(c):
    t = c.reshape(S5_NGB, S5_GB, SSM_GROUP, SSM_STATE).transpose(0, 1, 3, 2)
    eye = jnp.eye(S5_GB, dtype=c.dtype)
    return jnp.einsum('bgnp,gh->bgnhp', t, eye).reshape(S5_NGB, S5_ST, S5_CH)


def _s5_blockdiag_c_inv(x):
    t = x.reshape(S5_NGB, S5_GB, SSM_STATE, S5_GB, SSM_GROUP)
    eye = jnp.eye(S5_GB, dtype=x.dtype)
    d = jnp.einsum('bgnhp,gh->bgnp', t, eye)
    return d.transpose(0, 1, 3, 2).reshape(SSM_GROUPS, SSM_GROUP, SSM_STATE)


def s5_tables(lam_re, lam_im, L):
    pr, pi = lam_re.reshape(1, -1), lam_im.reshape(1, -1)
    n = 1
    while n < L:
        lr, li = pr[n - 1:n], pi[n - 1:n]
        pr, pi = (jnp.concatenate([pr, pr * lr - pi * li], 0), jnp.concatenate([pi, pr * li + pi * lr], 0))
        n *= 2
    nk = int(math.log2(L))
    idx = [2 ** k - 1 for k in range(nk)] + [0] * (8 - nk)

    def blk(t):
        return t.reshape(t.shape[0], S5_NGB, S5_ST).transpose(1, 0, 2)

    def rows(t):
        return jnp.concatenate([t[j:j + 1] for j in idx], axis=0)
    return blk(pr), blk(pi), blk(rows(pr)), blk(rows(pi))


S5_SUB = 8


def _scan_tiles(br, bi, a2r, a2i, reverse):
    L = br.shape[0]
    sub = lax.broadcasted_iota(jnp.int32, br.shape, 0) & (S5_SUB - 1)
    xr, xi = br, bi
    for k in range(3):
        s = 1 << k
        ar, ai = a2r[k:k + 1, :], a2i[k:k + 1, :]
        if reverse:
            sr, si = pltpu.roll(xr, L - s, 0), pltpu.roll(xi, L - s, 0)
            m = sub < S5_SUB - s
        else:
            sr, si = pltpu.roll(xr, s, 0), pltpu.roll(xi, s, 0)
            m = sub >= s
        sr, si = jnp.where(m, sr, 0.0), jnp.where(m, si, 0.0)
        xr, xi = xr + ar * sr - ai * si, xi + ar * si + ai * sr
    return xr, xi


def _scan_chain(xr, xi, pr, pi, cr, ci, out_r, out_i, reverse):
    ntile = xr.shape[0] // S5_SUB
    for g in (reversed(range(ntile)) if reverse else range(ntile)):
        rs = slice(g * S5_SUB, (g + 1) * S5_SUB)
        if reverse:
            nr = xr[rs] + pr * cr + pi * ci
            ni = xi[rs] + pr * ci - pi * cr
            cr, ci = nr[0:1], ni[0:1]
        else:
            nr = xr[rs] + pr * cr - pi * ci
            ni = xi[rs] + pr * ci + pi * cr
            cr, ci = nr[S5_SUB - 1:S5_SUB], ni[S5_SUB - 1:S5_SUB]
        out_r[rs, :] = nr
        out_i[rs, :] = ni
    return cr, ci


def s5_fwd(h, bre, bim, cre, cim, pwr, pwi, l2r, l2i, dskip, bsz):
    T, D = h.shape
    L = S5_L
    S = T // bsz
    NC = S // L

    def body(h_ref, bre_ref, bim_ref, cre_ref, cim_ref, pwr_ref, pwi_ref, l2r_ref, l2i_ref, d_ref,
             y_ref, gy_ref, xs_ref, car_r, car_i, xr_s, xi_s):
        @pl.when(pl.program_id(2) == 0)
        def _():
            car_r[...] = jnp.zeros_like(car_r)
            car_i[...] = jnp.zeros_like(car_i)
        u = h_ref[...]
        ub = u.astype(BF16)
        cr, ci = car_r[0:1, :], car_i[0:1, :]
        xs_ref[...] = jnp.zeros_like(xs_ref)
        xs_ref[0:1, :] = cr
        xs_ref[1:2, :] = ci
        xr, xi = _scan_tiles(_dot(ub, bre_ref[...]), _dot(ub, bim_ref[...]), l2r_ref[...], l2i_ref[...], False)
        cr, ci = _scan_chain(xr, xi, pwr_ref[...], pwi_ref[...], cr, ci, xr_s, xi_s, False)
        car_r[...] = jnp.broadcast_to(cr, car_r.shape)
        car_i[...] = jnp.broadcast_to(ci, car_i.shape)
        y = (_dot(xr_s[...].astype(BF16), cre_ref[...]) - _dot(xi_s[...].astype(BF16), cim_ref[...])
             + d_ref[...] * u)
        y_ref[...] = y
        gy_ref[...] = jax.nn.gelu(y).astype(BF16)

    tok = lambda g, b, c: (b * NC + c, g)
    par = lambda g, b, c: (g, 0, 0)
    return pl.pallas_call(
        body, grid=(S5_NGB, bsz, NC),
        in_specs=[pl.BlockSpec((L, S5_CH), tok),
                  pl.BlockSpec((None, S5_CH, S5_ST), par), pl.BlockSpec((None, S5_CH, S5_ST), par),
                  pl.BlockSpec((None, S5_ST, S5_CH), par), pl.BlockSpec((None, S5_ST, S5_CH), par),
                  pl.BlockSpec((None, 8, S5_ST), par), pl.BlockSpec((None, 8, S5_ST), par),
                  pl.BlockSpec((None, 8, S5_ST), par), pl.BlockSpec((None, 8, S5_ST), par),
                  pl.BlockSpec((1, S5_CH), lambda g, b, c: (0, g))],
        out_specs=[pl.BlockSpec((L, S5_CH), tok), pl.BlockSpec((L, S5_CH), tok),
                   pl.BlockSpec((None, 8, S5_ST), lambda g, b, c: (b * NC + c, 0, g))],
        out_shape=[jax.ShapeDtypeStruct((T, D), F32), jax.ShapeDtypeStruct((T, D), BF16),
                   jax.ShapeDtypeStruct((bsz * NC, 8, S5_NGB * S5_ST), F32)],
        scratch_shapes=[pltpu.VMEM((8, S5_ST), F32), pltpu.VMEM((8, S5_ST), F32),
                        pltpu.VMEM((L, S5_ST), F32), pltpu.VMEM((L, S5_ST), F32)],
        name="s5_fwd", compiler_params=_cparams(("arbitrary", "arbitrary", "arbitrary")),
    )(h, bre, bim, cre, cim, pwr, pwi, l2r, l2i, dskip)


def s5_bwd(h, dy, xs, bre, bim, cre, cim, pwr, pwi, pwr_rev, pwi_rev, l2r, l2i, dskip, bsz):
    T, D = h.shape
    L = S5_L
    S = T // bsz
    NC = S // L

    def body(h_ref, dy_ref, xs_ref, bre_ref, bim_ref, cre_ref, cim_ref, pwr_ref, pwi_ref, qr_ref, qi_ref,
             l2r_ref, l2i_ref, d_ref, du_ref, dbr_ref, dbi_ref, dcr_ref, dci_ref, dl_ref, dd_ref, car_r, car_i,
             xr_s, xi_s, dr_s, di_s):
        first = (pl.program_id(1) == 0) & (pl.program_id(2) == 0)

        @pl.when(first)
        def _():
            for r in (dbr_ref, dbi_ref, dcr_ref, dci_ref, dl_ref, dd_ref):
                r[...] = jnp.zeros_like(r)

        @pl.when(pl.program_id(2) == 0)
        def _():
            car_r[...] = jnp.zeros_like(car_r)
            car_i[...] = jnp.zeros_like(car_i)

        u = h_ref[...]
        ub = u.astype(BF16)
        dyv = dy_ref[...]
        dyb = dyv.astype(BF16)
        l2r_v, l2i_v = l2r_ref[...], l2i_ref[...]
        x0r, x0i = xs_ref[0:1, :], xs_ref[1:2, :]
        xr, xi = _scan_tiles(_dot(ub, bre_ref[...]), _dot(ub, bim_ref[...]), l2r_v, l2i_v, False)
        _scan_chain(xr, xi, pwr_ref[...], pwi_ref[...], x0r, x0i, xr_s, xi_s, False)
        xr, xi = xr_s[...], xi_s[...]
        gr = _dot_nt(dyb, cre_ref[...])
        gi = -_dot_nt(dyb, cim_ref[...])
        dr, di = _scan_tiles(gr, gi, l2r_v, -l2i_v, True)
        cr, ci = _scan_chain(dr, di, qr_ref[...], qi_ref[...], car_r[0:1, :], car_i[0:1, :], dr_s, di_s, True)
        dr, di = dr_s[...], di_s[...]
        car_r[...] = jnp.broadcast_to(cr, car_r.shape)
        car_i[...] = jnp.broadcast_to(ci, car_i.shape)
        row = lax.broadcasted_iota(jnp.int32, xr.shape, 0)
        xpr = jnp.where(row >= 1, pltpu.roll(xr, 1, 0), x0r)
        xpi = jnp.where(row >= 1, pltpu.roll(xi, 1, 0), x0i)
        dl_ref[0:1, :] += _colsum(dr * xpr + di * xpi)
        dl_ref[1:2, :] += _colsum(di * xpr - dr * xpi)
        drb, dib = dr.astype(BF16), di.astype(BF16)
        dcr_ref[...] += _dot_tn(xr.astype(BF16), dyb)
        dci_ref[...] -= _dot_tn(xi.astype(BF16), dyb)
        dbr_ref[...] += _dot_tn(ub, drb)
        dbi_ref[...] += _dot_tn(ub, dib)
        du_ref[...] = _dot_nt(drb, bre_ref[...]) + _dot_nt(dib, bim_ref[...]) + d_ref[...] * dyv
        dd_ref[0:1, :] += _colsum(dyv * u)

    tok = lambda g, b, c: (b * NC + (NC - 1 - c), g)
    par = lambda g, b, c: (g, 0, 0)
    return pl.pallas_call(
        body, grid=(S5_NGB, bsz, NC),
        in_specs=[pl.BlockSpec((L, S5_CH), tok), pl.BlockSpec((L, S5_CH), tok),
                  pl.BlockSpec((None, 8, S5_ST), lambda g, b, c: (b * NC + (NC - 1 - c), 0, g)),
                  pl.BlockSpec((None, S5_CH, S5_ST), par), pl.BlockSpec((None, S5_CH, S5_ST), par),
                  pl.BlockSpec((None, S5_ST, S5_CH), par), pl.BlockSpec((None, S5_ST, S5_CH), par),
                  pl.BlockSpec((None, 8, S5_ST), par), pl.BlockSpec((None, 8, S5_ST), par),
                  pl.BlockSpec((None, 8, S5_ST), par), pl.BlockSpec((None, 8, S5_ST), par),
                  pl.BlockSpec((None, 8, S5_ST), par), pl.BlockSpec((None, 8, S5_ST), par),
                  pl.BlockSpec((1, S5_CH), lambda g, b, c: (0, g))],
        out_specs=[pl.BlockSpec((L, S5_CH), tok),
                   pl.BlockSpec((None, S5_CH, S5_ST), par), pl.BlockSpec((None, S5_CH, S5_ST), par),
                   pl.BlockSpec((None, S5_ST, S5_CH), par), pl.BlockSpec((None, S5_ST, S5_CH), par),
                   pl.BlockSpec((None, 8, S5_ST), par),
                   pl.BlockSpec((8, S5_CH), lambda g, b, c: (0, g))],
        out_shape=[jax.ShapeDtypeStruct((T, D), F32),
                   jax.ShapeDtypeStruct((S5_NGB, S5_CH, S5_ST), F32), jax.ShapeDtypeStruct((S5_NGB, S5_CH, S5_ST), F32),
                   jax.ShapeDtypeStruct((S5_NGB, S5_ST, S5_CH), F32), jax.ShapeDtypeStruct((S5_NGB, S5_ST, S5_CH), F32),
                   jax.ShapeDtypeStruct((S5_NGB, 8, S5_ST), F32), jax.ShapeDtypeStruct((8, D), F32)],
        scratch_shapes=[pltpu.VMEM((8, S5_ST), F32), pltpu.VMEM((8, S5_ST), F32)]
        + [pltpu.VMEM((L, S5_ST), F32)] * 4,
        name="s5_bwd", compiler_params=_cparams(("arbitrary", "arbitrary", "arbitrary")),
    )(h, dy, xs, bre, bim, cre, cim, pwr, pwi, pwr_rev, pwi_rev, l2r, l2i, dskip)


def _shift_rows(win, off, n):
    if off == 0:
        return win[:n]
    return pltpu.roll(win, win.shape[0] - off, 0)[:n]


def dwconv_fwd(z, w, b, bsz):
    T, D = z.shape
    S = T // bsz
    TS, CW, PAD = CONV_TS, CONV_CW, CONV_PAD

    def body(z_ref, w_ref, b_ref, y_ref, zp):
        zp[0:PAD, :] = jnp.zeros((PAD, CW), F32)
        zp[PAD:, :] = z_ref[...]
        wv, bv = w_ref[...], b_ref[...]

        def step(t, carry):
            base = pl.multiple_of(t * TS, TS)
            win = zp[pl.ds(base, TS + PAD), :]
            acc = jnp.zeros((TS, CW), F32) + bv
            for k in range(CONV_WIDTH):
                acc = acc + wv[k:k + 1, :] * _shift_rows(win, PAD - (CONV_WIDTH - 1) + k, TS)
            y_ref[pl.ds(base, TS), :] = acc
            return carry
        lax.fori_loop(0, S // TS, step, 0)

    return pl.pallas_call(
        body, grid=(D // CW, bsz),
        in_specs=[pl.BlockSpec((S, CW), lambda c, bb: (bb, c)), pl.BlockSpec((32, CW), lambda c, bb: (0, c)),
                  pl.BlockSpec((1, CW), lambda c, bb: (0, c))],
        out_specs=pl.BlockSpec((S, CW), lambda c, bb: (bb, c)),
        out_shape=jax.ShapeDtypeStruct((T, D), F32),
        scratch_shapes=[pltpu.VMEM((S + PAD, CW), F32)],
        name="dwconv_fwd", compiler_params=_cparams(("arbitrary", "arbitrary")),
    )(z, w, b)


def dwconv_bwd(z, dy, w, bsz):
    T, D = z.shape
    S = T // bsz
    TS, CW, PAD = CONV_TS, CONV_CW, CONV_PAD

    def body(z_ref, dy_ref, w_ref, dz_ref, dw_ref, db_ref, zp, dyp):
        @pl.when(pl.program_id(1) == 0)
        def _():
            dw_ref[...] = jnp.zeros_like(dw_ref)
            db_ref[...] = jnp.zeros_like(db_ref)
        zp[0:PAD, :] = jnp.zeros((PAD, CW), F32)
        zp[PAD:, :] = z_ref[...]
        dyp[0:S, :] = dy_ref[...]
        dyp[S:, :] = jnp.zeros((PAD, CW), F32)
        wv = w_ref[...]

        def step(t, carry):
            base = pl.multiple_of(t * TS, TS)
            zwin = zp[pl.ds(base, TS + PAD), :]
            dwin = dyp[pl.ds(base, TS + PAD), :]
            dyt = dwin[:TS]
            acc = jnp.zeros((TS, CW), F32)
            for j in range(CONV_WIDTH):
                k = CONV_WIDTH - 1 - j
                acc = acc + wv[k:k + 1, :] * _shift_rows(dwin, j, TS)
            dz_ref[pl.ds(base, TS), :] = acc
            for k in range(CONV_WIDTH):
                dw_ref[k:k + 1, :] += _colsum(dyt * _shift_rows(zwin, PAD - (CONV_WIDTH - 1) + k, TS))
            db_ref[0:1, :] += _colsum(dyt)
            return carry
        lax.fori_loop(0, S // TS, step, 0)

    return pl.pallas_call(
        body, grid=(D // CW, bsz),
        in_specs=[pl.BlockSpec((S, CW), lambda c, bb: (bb, c)), pl.BlockSpec((S, CW), lambda c, bb: (bb, c)),
                  pl.BlockSpec((32, CW), lambda c, bb: (0, c))],
        out_specs=[pl.BlockSpec((S, CW), lambda c, bb: (bb, c)), pl.BlockSpec((32, CW), lambda c, bb: (0, c)),
                   pl.BlockSpec((8, CW), lambda c, bb: (0, c))],
        out_shape=[jax.ShapeDtypeStruct((T, D), F32), jax.ShapeDtypeStruct((32, D), F32),
                   jax.ShapeDtypeStruct((8, D), F32)],
        scratch_shapes=[pltpu.VMEM((S + PAD, CW), F32), pltpu.VMEM((S + PAD, CW), F32)],
        name="dwconv_bwd", compiler_params=_cparams(("arbitrary", "arbitrary")),
    )(z, dy, w)


def spatial_fwd(u, vln, ws, bias):
    T, E = u.shape
    C, H = GMLP_CHUNK, GMLP_HEADS
    hw = E // H

    def body(u_ref, v_ref, ws_ref, b_ref, o_ref):
        for hh in range(H):
            sl = slice(hh * hw, (hh + 1) * hw)
            vp = _dot(ws_ref[hh], v_ref[:, sl]) + b_ref[:, sl]
            o_ref[:, sl] = (u_ref[:, sl] * vp).astype(o_ref.dtype)

    return pl.pallas_call(
        body, grid=(T // C,),
        in_specs=[pl.BlockSpec((C, E), lambda i: (i, 0)), pl.BlockSpec((C, E), lambda i: (i, 0)),
                  pl.BlockSpec((H, C, C), lambda i: (0, 0, 0)), pl.BlockSpec((C, E), lambda i: (0, 0))],
        out_specs=pl.BlockSpec((C, E), lambda i: (i, 0)),
        out_shape=jax.ShapeDtypeStruct((T, E), BF16),
        name="spatial_fwd", compiler_params=_cparams(("arbitrary",)),
    )(u, vln, ws, bias)


def spatial_bwd(u, vln, dg, ws, bias):
    T, E = u.shape
    C, H = GMLP_CHUNK, GMLP_HEADS
    hw = E // H

    def body(u_ref, v_ref, dg_ref, ws_ref, b_ref, du_ref, dv_ref, dws_ref, db_ref):
        @pl.when(pl.program_id(0) == 0)
        def _():
            dws_ref[...] = jnp.zeros_like(dws_ref)
            db_ref[...] = jnp.zeros_like(db_ref)
        tril = (lax.broadcasted_iota(jnp.int32, (C, C), 1) <= lax.broadcasted_iota(jnp.int32, (C, C), 0))
        for hh in range(H):
            sl = slice(hh * hw, (hh + 1) * hw)
            v = v_ref[:, sl]
            w = ws_ref[hh]
            dgv = dg_ref[:, sl].astype(F32)
            vp = _dot(w, v) + b_ref[:, sl]
            du_ref[:, sl] = dgv * vp
            dvp = dgv * u_ref[:, sl]
            dvpb = dvp.astype(BF16)
            dv_ref[:, sl] = _dot_tn(w, dvpb)
            dws_ref[hh] += jnp.where(tril, _dot_nt(dvpb, v), 0.0)
            db_ref[:, sl] += dvp

    return pl.pallas_call(
        body, grid=(T // C,),
        in_specs=[pl.BlockSpec((C, E), lambda i: (i, 0)), pl.BlockSpec((C, E), lambda i: (i, 0)),
                  pl.BlockSpec((C, E), lambda i: (i, 0)),
                  pl.BlockSpec((H, C, C), lambda i: (0, 0, 0)), pl.BlockSpec((C, E), lambda i: (0, 0))],
        out_specs=[pl.BlockSpec((C, E), lambda i: (i, 0)), pl.BlockSpec((C, E), lambda i: (i, 0)),
                   pl.BlockSpec((H, C, C), lambda i: (0, 0, 0)), pl.BlockSpec((C, E), lambda i: (0, 0))],
        out_shape=[jax.ShapeDtypeStruct((T, E), F32), jax.ShapeDtypeStruct((T, E), F32),
                   jax.ShapeDtypeStruct((H, C, C), F32), jax.ShapeDtypeStruct((C, E), F32)],
        name="spatial_bwd", compiler_params=_cparams(("arbitrary",)),
    )(u, vln, dg, ws, bias)


def _att_masks():
    r = lax.broadcasted_iota(jnp.int32, (ATT_BLK, ATT_BLK), 0)
    c = lax.broadcasted_iota(jnp.int32, (ATT_BLK, ATT_BLK), 1)
    return c <= r, c >= r


NEG = -1e30
ATT_SCALE = HEAD_DIM ** -0.5


def _att_view(t, dil):
    return t.reshape(t.shape[0] // dil, dil * t.shape[1])


def attn_fwd(name, q, k, v, dil, bsz):
    T, Wd = q.shape
    nb = T // (bsz * dil * ATT_BLK)
    TB = min(nb, ATT_TB)
    nsteps = nb // TB

    def body(q_ref, k_ref, v_ref, kp_ref, vp_ref, o_ref, l_ref):
        n = pl.program_id(2)
        mc, mp = _att_masks()
        for j in range(TB):
            rows = slice(j * ATT_BLK, (j + 1) * ATT_BLK)
            prow = slice((j - 1) * ATT_BLK, j * ATT_BLK)
            hp = (n * TB + j) > 0
            for hh in range(ATT_HEADS):
                ls = slice(hh * HEAD_DIM, (hh + 1) * HEAD_DIM)
                qj, kc, vc = q_ref[rows, ls], k_ref[rows, ls], v_ref[rows, ls]
                kp = k_ref[prow, ls] if j > 0 else kp_ref[:, ls]
                vp = v_ref[prow, ls] if j > 0 else vp_ref[:, ls]
                sc = jnp.where(mc, _dot_nt(qj, kc) * ATT_SCALE, NEG)
                sp = jnp.where(mp & hp, _dot_nt(qj, kp) * ATT_SCALE, NEG)
                m = jnp.maximum(jnp.max(sc, axis=1, keepdims=True), jnp.max(sp, axis=1, keepdims=True))
                pc, pp = jnp.exp(sc - m), jnp.exp(sp - m)
                l = jnp.sum(pc, axis=1, keepdims=True) + jnp.sum(pp, axis=1, keepdims=True)
                o_ref[rows, ls] = (_dot(pc.astype(BF16), vc) + _dot(pp.astype(BF16), vp)) / l
                l_ref[rows, ls] = jnp.broadcast_to(m + jnp.log(l), (ATT_BLK, HEAD_DIM))

    blk = pl.BlockSpec((TB * ATT_BLK, Wd), lambda b, r, n: (b * nsteps + n, r))
    prev = pl.BlockSpec((ATT_BLK, Wd), lambda b, r, n: (jnp.maximum(b * nb + n * TB - 1, 0), r))
    qv, kv, vv = (_att_view(t, dil) for t in (q, k, v))
    o, l = pl.pallas_call(
        body, grid=(bsz, dil, nsteps), in_specs=[blk, blk, blk, prev, prev], out_specs=[blk, blk],
        out_shape=[jax.ShapeDtypeStruct(qv.shape, F32), jax.ShapeDtypeStruct(qv.shape, F32)],
        name=name, compiler_params=_cparams(("arbitrary", "arbitrary", "arbitrary")),
    )(qv, kv, vv, kv, vv)
    return o.reshape(T, Wd), l.reshape(T, Wd)


def attn_bwd(name, q, k, v, do, mg, lse, dil, bsz):
    T, Wd = q.shape
    nb = T // (bsz * dil * ATT_BLK)
    TB = min(nb, ATT_TB)
    nsteps = nb // TB

    def body(q_ref, k_ref, v_ref, do_ref, mg_ref, l_ref, kp_ref, vp_ref, qn_ref, don_ref, mgn_ref, ln_ref,
             dq_ref, dk_ref, dv_ref):
        n = pl.program_id(2)
        mc, mp = _att_masks()

        def probs(qj, kk, lse_col, mask):
            s = _dot_nt(qj, kk) * ATT_SCALE
            return jnp.where(mask, jnp.exp(s - lse_col), 0.0)

        def ds_of(p, doj, vv, delta):
            return (p * (_dot_nt(doj, vv) - delta) * ATT_SCALE).astype(BF16)

        for hh in range(ATT_HEADS):
            ls = slice(hh * HEAD_DIM, (hh + 1) * HEAD_DIM)
            dk = [None] * TB
            dv = [None] * TB
            for j in range(TB + 1):
                rows = slice(j * ATT_BLK, (j + 1) * ATT_BLK)
                prow = slice((j - 1) * ATT_BLK, j * ATT_BLK)
                if j < TB:
                    qj, doj, mgj, lj = q_ref[rows, ls], do_ref[rows, ls], mg_ref[rows, ls], l_ref[rows, ls]
                    hp = (n * TB + j) > 0
                else:
                    qj, doj, mgj, lj = qn_ref[:, ls], don_ref[:, ls], mgn_ref[:, ls], ln_ref[:, ls]
                    hp = (n + 1) * TB < nb
                lse_col = lj[:, 0:1]
                delta = jnp.sum(doj.astype(F32) * mgj.astype(F32), axis=1, keepdims=True)
                if j > 0:
                    kp, vp = k_ref[prow, ls], v_ref[prow, ls]
                else:
                    kp, vp = kp_ref[:, ls], vp_ref[:, ls]
                pp = probs(qj, kp, lse_col, mp & hp)
                dsp = ds_of(pp, doj, vp, delta)
                if j > 0:
                    dk[j - 1] = dk[j - 1] + _dot_tn(dsp, qj)
                    dv[j - 1] = dv[j - 1] + _dot_tn(pp.astype(BF16), doj)
                if j < TB:
                    kc, vc = k_ref[rows, ls], v_ref[rows, ls]
                    pc = probs(qj, kc, lse_col, mc)
                    dsc = ds_of(pc, doj, vc, delta)
                    dq_ref[rows, ls] = (_dot(dsc, kc) + _dot(dsp, kp)).astype(dq_ref.dtype)
                    dk[j] = _dot_tn(dsc, qj)
                    dv[j] = _dot_tn(pc.astype(BF16), doj)
            for j in range(TB):
                rows = slice(j * ATT_BLK, (j + 1) * ATT_BLK)
                dk_ref[rows, ls] = dk[j].astype(dk_ref.dtype)
                dv_ref[rows, ls] = dv[j].astype(dv_ref.dtype)

    blk = pl.BlockSpec((TB * ATT_BLK, Wd), lambda b, r, n: (b * nsteps + n, r))
    prev = pl.BlockSpec((ATT_BLK, Wd), lambda b, r, n: (jnp.maximum(b * nb + n * TB - 1, 0), r))
    nxt = pl.BlockSpec((ATT_BLK, Wd), lambda b, r, n: (b * nb + jnp.minimum((n + 1) * TB, nb - 1), r))
    qv, kv, vv, dov, mgv, lv = (_att_view(t, dil) for t in (q, k, v, do, mg, lse))
    res = pl.pallas_call(
        body, grid=(bsz, dil, nsteps), in_specs=[blk] * 6 + [prev, prev, nxt, nxt, nxt, nxt],
        out_specs=[blk, blk, blk], out_shape=[jax.ShapeDtypeStruct(qv.shape, BF16)] * 3,
        name=name, compiler_params=_cparams(("arbitrary", "arbitrary", "arbitrary")),
    )(qv, kv, vv, dov, mgv, lv, kv, vv, qv, dov, mgv, lv)
    return [t.reshape(T, Wd) for t in res]


QKV_SLOTS = 3 * len(ATT_CONFIGS) * ATT_HEADS
SLOTS_PER_DEV = QKV_SLOTS // N_DEV


def _head_slots(t3):
    return [t3[s // SLOTS_PER_DEV][:, (s % SLOTS_PER_DEV) * HEAD_DIM:(s % SLOTS_PER_DEV + 1) * HEAD_DIM]
            for s in range(QKV_SLOTS)]


def _heads_of(slots, k):
    return jnp.concatenate(slots[k * ATT_HEADS:(k + 1) * ATT_HEADS], axis=1)


def _slots_to_blocked(slots):
    return jnp.stack([jnp.concatenate(slots[b * SLOTS_PER_DEV:(b + 1) * SLOTS_PER_DEV], axis=1)
                      for b in range(N_DEV)])


def _coords():
    return lax.axis_index("x"), lax.axis_index("y"), lax.axis_index("c")


def all_gather(name, xs):
    n = len(xs)

    def body(*refs):
        x_refs, out_refs = refs[:n], refs[n:2 * n]
        send_sems, recv_sems, local_sems = refs[2 * n:]
        x, y, c = _coords()
        me, sibling = (x, y, c), (x, y, 1 - c)
        chips = [(1 - x, y), (x, 1 - y), (1 - x, 1 - y)]

        def slot(a, px, py, pc):
            return out_refs[a].at[4 * px + 2 * py + pc]

        def copy(a, k, block, to, src=None):
            return pltpu.make_async_remote_copy(
                src_ref=slot(a, *block) if src is None else src, dst_ref=slot(a, *block),
                send_sem=send_sems.at[7 * a + k], recv_sem=recv_sems.at[7 * a + k],
                device_id=to, device_id_type=MESH)

        mine = [pltpu.make_async_copy(x_refs[a], slot(a, *me), local_sems.at[a]) for a in range(n)]
        for cp in mine:
            cp.start()
        first = []
        for a in range(n):
            first.append(copy(a, 0, me, sibling, src=x_refs[a]))
            first += [copy(a, 1 + j, me, (*chip, c), src=x_refs[a]) for j, chip in enumerate(chips)]
        for cp in first:
            cp.start()
        passed = []
        for j, chip in enumerate(chips):
            for a in range(n):
                copy(a, 1 + j, (*chip, c), me).wait_recv()
                cp = copy(a, 4 + j, (*chip, c), sibling)
                cp.start()
                passed.append(cp)
        for a in range(n):
            copy(a, 0, sibling, me).wait_recv()
            for j, chip in enumerate(chips):
                copy(a, 4 + j, (*chip, 1 - c), me).wait_recv()
        for cp in first + passed:
            cp.wait_send()
        for cp in mine:
            cp.wait()

    anyspec = pl.BlockSpec(memory_space=pl.ANY)
    return pl.pallas_call(
        body, out_shape=[jax.ShapeDtypeStruct((N_DEV,) + t.shape, t.dtype) for t in xs],
        in_specs=[anyspec] * n, out_specs=[anyspec] * n,
        scratch_shapes=[pltpu.SemaphoreType.DMA((7 * n,)), pltpu.SemaphoreType.DMA((7 * n,)),
                        pltpu.SemaphoreType.DMA((n,))],
        name=name,
    )(*xs)


def exchange_sibling(name, gs):
    n = len(gs)

    def body(*refs):
        g_refs, out_refs = refs[:n], refs[n:2 * n]
        send_sems, recv_sems = refs[2 * n:]
        x, y, c = _coords()
        sibling = (x, y, 1 - c)
        cps = []
        for a in range(n):
            for q in range(4):
                cps.append(pltpu.make_async_remote_copy(
                    src_ref=g_refs[a].at[2 * q + (1 - c)], dst_ref=out_refs[a].at[q],
                    send_sem=send_sems.at[4 * a + q], recv_sem=recv_sems.at[4 * a + q],
                    device_id=sibling, device_id_type=MESH))
        for cp in cps:
            cp.start()
        for cp in cps:
            cp.wait_recv()
        for cp in cps:
            cp.wait_send()

    anyspec = pl.BlockSpec(memory_space=pl.ANY)
    return pl.pallas_call(
        body, out_shape=[jax.ShapeDtypeStruct((4,) + g.shape[1:], g.dtype) for g in gs],
        in_specs=[anyspec] * n, out_specs=[anyspec] * n,
        scratch_shapes=[pltpu.SemaphoreType.DMA((4 * n,)), pltpu.SemaphoreType.DMA((4 * n,))],
        name=name,
    )(*gs)


def exchange_chips(name, ps):
    n = len(ps)

    def body(*refs):
        p_refs, out_refs = refs[:n], refs[n:2 * n]
        send_sems, recv_sems = refs[2 * n:]
        x, y, c = _coords()
        chips = [(1 - x, y), (x, 1 - y), (1 - x, 1 - y)]
        cps = []
        for a in range(n):
            for k, (px, py) in enumerate(chips):
                cps.append(pltpu.make_async_remote_copy(
                    src_ref=p_refs[a].at[2 * px + py], dst_ref=out_refs[a].at[k],
                    send_sem=send_sems.at[3 * a + k], recv_sem=recv_sems.at[3 * a + k],
                    device_id=(px, py, c), device_id_type=MESH))
        for cp in cps:
            cp.start()
        for cp in cps:
            cp.wait_recv()
        for cp in cps:
            cp.wait_send()

    anyspec = pl.BlockSpec(memory_space=pl.ANY)
    return pl.pallas_call(
        body, out_shape=[jax.ShapeDtypeStruct((3,) + p.shape[1:], p.dtype) for p in ps],
        in_specs=[anyspec] * n, out_specs=[anyspec] * n,
        scratch_shapes=[pltpu.SemaphoreType.DMA((3 * n,)), pltpu.SemaphoreType.DMA((3 * n,))],
        name=name,
    )(*ps)


def _row_tile(R):
    tr = 256
    while R % tr:
        tr //= 2
    assert tr % 8 == 0
    return tr


def add_sibling(name, g, recv, c_idx):
    _, R, C = g.shape
    tr = _row_tile(R)

    def body(c_ref, g_ref, r_ref, o_ref, o16_ref):
        s = g_ref[...] + r_ref[...].astype(F32)
        o_ref[...] = s
        o16_ref[...] = s.astype(BF16)

    out = pl.BlockSpec((None, tr, C), lambda q, i, cr: (q, i, 0))
    return pl.pallas_call(
        body,
        grid_spec=pltpu.PrefetchScalarGridSpec(
            num_scalar_prefetch=1, grid=(4, R // tr),
            in_specs=[pl.BlockSpec((None, tr, C), lambda q, i, cr: (2 * q + cr[0], i, 0)), out],
            out_specs=[out, out]),
        out_shape=[jax.ShapeDtypeStruct((4, R, C), F32), jax.ShapeDtypeStruct((4, R, C), BF16)], name=name,
        compiler_params=_cparams(("arbitrary", "arbitrary")),
    )(c_idx, g, recv)


def _adam_math(w, g, m, v):
    m = ADAM_B1 * m + (1.0 - ADAM_B1) * g
    v = ADAM_B2 * v + (1.0 - ADAM_B2) * jnp.square(g)
    m_hat = m / (1.0 - ADAM_B1 ** ADAM_STEP)
    v_hat = v / (1.0 - ADAM_B2 ** ADAM_STEP)
    delta = -ADAM_LR * (m_hat / (jnp.sqrt(v_hat) + ADAM_EPS) + ADAM_WD * w)
    return delta, m, v


def adam_big(name, p1, recv, w, m, v, chip_idx, layer=0):
    _, R, C = p1.shape
    tr = _row_tile(R)
    nt = R // tr

    def body(q_ref, p_ref, r_ref, w_ref, m_ref, v_ref, g_ref, d_ref, nm_ref, nv_ref):
        g = ((p_ref[...] + r_ref[0].astype(F32)) + r_ref[1].astype(F32)) + r_ref[2].astype(F32)
        d, nm, nv = _adam_math(w_ref[...], g, m_ref[...], v_ref[...])
        g_ref[...] = g
        d_ref[...] = d
        nm_ref[...] = nm
        nv_ref[...] = nv

    row_in = pl.BlockSpec((tr, C), lambda i, qr: (layer * nt + i, 0))
    row = pl.BlockSpec((tr, C), lambda i, qr: (i, 0))
    return pl.pallas_call(
        body,
        grid_spec=pltpu.PrefetchScalarGridSpec(
            num_scalar_prefetch=1, grid=(nt,),
            in_specs=[pl.BlockSpec((None, tr, C), lambda i, qr: (qr[0], i, 0)),
                      pl.BlockSpec((3, tr, C), lambda i, qr: (0, i, 0)), row_in, row_in, row_in],
            out_specs=[row, row, row, row]),
        out_shape=[jax.ShapeDtypeStruct((R, C), F32)] * 4, name=name,
        compiler_params=_cparams(("arbitrary",)),
    )(chip_idx, p1, recv, w, m, v)


def sum8(parts):
    _, R, C = parts.shape

    def body(p_ref, o_ref):
        acc = p_ref[0]
        for k in range(1, N_DEV):
            acc = acc + p_ref[k]
        o_ref[...] = acc

    tr = 128
    while R % tr:
        tr //= 2
    assert tr % 8 == 0
    return pl.pallas_call(
        body, grid=(R // tr,), in_specs=[pl.BlockSpec((N_DEV, tr, C), lambda i: (0, i, 0))],
        out_specs=pl.BlockSpec((tr, C), lambda i: (i, 0)), out_shape=jax.ShapeDtypeStruct((R, C), F32),
        name="sum8", compiler_params=_cparams(("arbitrary",)),
    )(parts)


def adam_small(w, g, m, v):
    def fn(wt, gt, mt, vt):
        return _adam_math(wt, gt, mt, vt)
    C = w.shape[1]
    return rowwise("adam_small", fn, [w, g, m, v], [], [(C, F32)] * 3, tr=128)


def _pack(arrs, rows_mult=8):
    flat = jnp.concatenate([a.reshape(-1) for a in arrs])
    n = flat.shape[0]
    per = PACK_C * rows_mult
    pad = (-n) % per
    if pad:
        flat = jnp.concatenate([flat, jnp.zeros((pad,), flat.dtype)])
    return flat.reshape(-1, PACK_C)


def _unpack(buf, shapes):
    flat = buf.reshape(-1)
    out, off = [], 0
    for s in shapes:
        n = math.prod(s)
        out.append(flat[off:off + n].reshape(s))
        off += n
    return out


def _blocked(gfull, axis):
    shp = gfull.shape
    n = shp[axis] // N_DEV
    t = gfull.reshape(shp[:axis] + (N_DEV, n) + shp[axis + 1:])
    t = jnp.moveaxis(t, axis, 0)
    return t.reshape(N_DEV, -1)


def _unblocked(gathered, shard_shape, axis):
    t = jnp.moveaxis(gathered, 0, axis)
    shp = shard_shape[:axis] + (N_DEV * shard_shape[axis],) + shard_shape[axis + 1:]
    return t.reshape(shp)


def _relu2_epi(acc):
    r = jnp.maximum(acc, 0.0)
    return acc, r * r


def _step(x3, target3, W, Wfull):
    bsz, S, D = x3.shape
    T = bsz * S
    x = x3.reshape(T, D)
    target = target3.reshape(T, D)
    row = lambda v: v.reshape(1, -1)
    grads = {}

    s5p = (W['ssm_a_re'][0], W['ssm_a_im'][0], W['ssm_log_dt'][0], W['ssm_b_re'][0], W['ssm_b_im'][0])
    (lam_re, lam_im, bb_re, bb_im), s5_disc_vjp = jax.vjp(s5_disc, *s5p)
    pwr, pwi, l2r, l2i = s5_tables(lam_re, lam_im, S5_SUB)
    bre, bim = _s5_blockdiag_b(bb_re).astype(BF16), _s5_blockdiag_b(bb_im).astype(BF16)
    cre, cim = _s5_blockdiag_c(W['ssm_c_re'][0]).astype(BF16), _s5_blockdiag_c(W['ssm_c_im'][0]).astype(BF16)
    dskip = W['ssm_d']

    tril = jnp.tril(jnp.ones((GMLP_CHUNK, GMLP_CHUNK), bool))
    ws = jnp.where(tril[None], W['gmlp_w_s'][0], 0.0).astype(BF16)
    hw = D // GMLP_HEADS
    sbias = jnp.repeat(W['gmlp_b_s'][0].T, hw, axis=1)

    conv_w = jnp.concatenate([W['conv_w_dw'][0], jnp.zeros((1, D), F32)], axis=0)

    saved = []
    for i in range(DEPTH):
        sv = {'x': x}
        nm = W['norm_mix'][i:i + 1]
        if i == 0:
            h, hf = rms_fwd("rms_mix0", x, nm, want_f32=True)
            ypre, gy, xs = s5_fwd(hf, bre, bim, cre, cim, pwr, pwi, l2r, l2i, dskip, bsz)
            z, = matmul("s5_glu_mm", gy, Wfull['ssm_w_glu'], mode='cb')
            x1, = rowwise("s5_glu", lambda zt, xt: xt + _glu(zt), [z, x], [], [(D, F32)])
            sv.update(hf=hf, ypre=ypre, gy=gy, xs=xs, z=z)
        elif i == 1:
            h, = rms_fwd("rms_mix1", x, nm)
            z, = matmul("conv_pw1", h, Wfull['conv_w_pw1'], mode='cb', epi=lambda acc, b: (acc + b,),
                        extras=[(W['conv_b_pw1'], 'row')])
            zg, = rowwise("conv_glu", _glu, [z], [], [(D, F32)])
            yc = dwconv_fwd(zg, conv_w, W['conv_b_dw'], bsz)
            y2, = rowwise("conv_ln_silu", lambda t, g, b: jax.nn.silu(_ln(t, g, b)), [yc],
                          [W['conv_ln_g'], W['conv_ln_b']], [(D, BF16)])
            x1, = matmul("conv_pw2", y2, Wfull['conv_w_pw2'], epi=lambda acc, b, r: (acc + b + r,),
                         extras=[(W['conv_b_pw2'], 'row'), (x, 'tile')])
            sv.update(h=h, z=z, zg=zg, yc=yc, y2=y2)
        elif i == 2:
            h, = rms_fwd("rms_mix2", x, nm)
            zp, = matmul("gmlp_in", h, Wfull['gmlp_w_in'], mode='cb')

            def gm_pre(zt, g, b):
                a = jax.nn.gelu(zt)
                return a[:, :D], _ln(a[:, D:], g, b)
            u, vln = rowwise("gmlp_pre", gm_pre, [zp], [W['gmlp_ln_g'], W['gmlp_ln_b']], [(D, F32), (D, BF16)])
            gated = spatial_fwd(u, vln, ws, sbias)
            x1, = matmul("gmlp_out", gated, Wfull['gmlp_w_out'], epi=lambda acc, r: (acc + r,), extras=[(x, 'tile')])
            sv.update(h=h, zp=zp, u=u, vln=vln, gated=gated)
        else:
            h, = rms_fwd("rms_mix3", x, nm)
            qkv3, = matmul("attn_qkv", h, Wfull['attn_w_qkv'], mode='cb', out_dtypes=(BF16,), out3=True)
            slots = _head_slots(qkv3)
            ng = len(ATT_CONFIGS)
            outs, lses, blocks = [], [], []
            for gi, (window, dil) in enumerate(ATT_CONFIGS):
                qb, kb, vb = (_heads_of(slots, j * ng + gi) for j in range(3))
                ob, lb = attn_fwd("attn_fwd%d" % gi, qb, kb, vb, dil, bsz)
                blocks.append((qb, kb, vb, lb, dil))
                outs.append(ob)
                lses.append(lb)

            def merge(o0, o1, o2, l0, l1, l2):
                m = jnp.maximum(jnp.maximum(l0, l1), l2)
                e0, e1, e2 = jnp.exp(l0 - m), jnp.exp(l1 - m), jnp.exp(l2 - m)
                inv = 1.0 / (e0 + e1 + e2)
                w0, w1, w2 = e0 * inv, e1 * inv, e2 * inv
                return w0 * o0 + w1 * o1 + w2 * o2, w0, w1, w2
            merged, w0, w1, w2 = rowwise("attn_merge", merge, outs + lses, [],
                                         [(ATT_W, BF16), (ATT_W, F32), (ATT_W, F32), (ATT_W, F32)])
            x1, = matmul("attn_o", merged, Wfull['attn_w_o'], mode='cb', epi=lambda acc, r: (acc + r,),
                         extras=[(x, 'tile')])
            sv.update(h=h, blocks=blocks, merged=merged, wts=(w0, w1, w2))
        h2, = rms_fwd("rms_mlp%d" % i, x1, W['norm_mlp'][i:i + 1])
        a, act = matmul("mlp_in%d" % i, h2, Wfull['mlp_w_in'][i], mode='cb', epi=_relu2_epi, out_dtypes=(BF16, BF16))
        x2, = matmul("mlp_out%d" % i, act, Wfull['mlp_w_out'][i], epi=lambda acc, r: (acc + r,), extras=[(x1, 'tile')])
        sv.update(x1=x1, h2=h2, a=a, act=act)
        saved.append(sv)
        x = x2

    def loss_fn(xt, tt, g):
        y, vjp = jax.vjp(_rms, xt, g)
        err = y - tt
        dxx, dg = vjp(err * (1.0 / D))
        lval = jnp.sum(jnp.sum(err * err, axis=1, keepdims=True), axis=0, keepdims=True) * (0.5 / D)
        return dxx, dxx, jnp.broadcast_to(lval, (1, 128)), dg
    dx, dxb, lacc, dnf = rowwise("loss_head", loss_fn, [x, target], [row(W['norm_final'])],
                                 [(D, F32), (D, BF16)], [((1, 128), F32), ((1, D), F32)])
    loss_local = lacc[0, 0]
    grads['norm_final'] = dnf.reshape(-1)

    g_norm_mix, g_norm_mlp = [None] * DEPTH, [None] * DEPTH
    g_mlp_in, g_mlp_out = [None] * DEPTH, [None] * DEPTH
    for i in reversed(range(DEPTH)):
        sv = saved[i]
        da, = matmul("mlp_out_bwd%d" % i, dxb, Wfull['mlp_w_out'][i], mode='nt',
                     epi=lambda acc, av: (acc * (2.0 * jnp.maximum(av.astype(F32), 0.0)),),
                     extras=[(sv['a'], 'tile')], out_dtypes=(BF16,))
        g_mlp_out[i] = _rows_blocked(wgrad("mlp_out_wg%d" % i, sv['act'], dxb))
        dh2, = matmul_nt_cb("mlp_in_bwd%d" % i, da, Wfull['mlp_w_in'][i])
        g_mlp_in[i] = wgrad("mlp_in_wg%d" % i, sv['h2'], da, cb=True)
        dx, dxb, dg = rms_bwd("rms_mlp_bwd%d" % i, sv['x1'], dh2, dx, W['norm_mlp'][i:i + 1])
        g_norm_mlp[i] = dg.reshape(-1)
        xin = sv['x']
        if i == 0:
            dz, = rowwise("s5_glu_bwd", _glu_bwd, [sv['z'], dx], [], [(2 * D, BF16)])
            dgy, = matmul_nt_cb("s5_glu_mm_bwd", dz, Wfull['ssm_w_glu'])
            grads['ssm_w_glu'] = wgrad("s5_glu_wg", sv['gy'], dz, cb=True)

            def gelu_bwd(yt, dt):
                _, vjp = jax.vjp(jax.nn.gelu, yt)
                return vjp(dt)[0]
            dypre, = rowwise("s5_gelu_bwd", gelu_bwd, [sv['ypre'], dgy], [], [(D, F32)])
            du, dbr, dbi, dcr, dci, dl, dd = s5_bwd(sv['hf'], dypre, sv['xs'], bre, bim, cre, cim, pwr, pwi,
                                                    pwr[:, ::-1], pwi[:, ::-1], l2r, l2i, dskip, bsz)
            dlam_re = dl[:, 0, :].reshape(SSM_GROUPS, SSM_STATE)
            dlam_im = dl[:, 1, :].reshape(SSM_GROUPS, SSM_STATE)
            s5_cot = (dlam_re, dlam_im, _s5_blockdiag_b_inv(dbr), _s5_blockdiag_b_inv(dbi))
            grads['ssm_c_re'] = _s5_blockdiag_c_inv(dcr)[None]
            grads['ssm_c_im'] = _s5_blockdiag_c_inv(dci)[None]
            grads['ssm_d'] = dd[0:1]
            dh = du
        elif i == 1:
            dy2, = matmul("conv_pw2_bwd", dxb, Wfull['conv_w_pw2'], mode='nt')
            grads['conv_w_pw2'] = _rows_blocked(wgrad("conv_pw2_wg", sv['y2'], dxb))

            def ln_silu_bwd(yt, dt, dxt, g, b):
                _, vjp = jax.vjp(lambda t, gg, bb: jax.nn.silu(_ln(t, gg, bb)), yt, g, b)
                dyc, dgg, dbb = vjp(dt)
                return dyc, dgg, dbb, _colsum(dxt)
            dyc, dlg, dlb, dbp2 = rowwise("conv_ln_silu_bwd", ln_silu_bwd, [sv['yc'], dy2, dx],
                                          [W['conv_ln_g'], W['conv_ln_b']], [(D, F32)],
                                          [((1, D), F32), ((1, D), F32), ((1, D), F32)])
            grads['conv_ln_g'], grads['conv_ln_b'], grads['conv_b_pw2'] = dlg, dlb, dbp2
            dzg, dwd, dbd = dwconv_bwd(sv['zg'], dyc, conv_w, bsz)
            grads['conv_w_dw'] = dwd[None, :CONV_WIDTH]
            grads['conv_b_dw'] = dbd[0:1]

            def glu_bwd1(zt, dyt):
                dzt = _glu_bwd(zt, dyt)
                return dzt, _colsum(dzt)
            dz, dbp1 = rowwise("conv_glu_bwd", glu_bwd1, [sv['z'], dzg], [], [(2 * D, BF16)], [((1, 2 * D), F32)])
            grads['conv_b_pw1'] = dbp1
            dh, = matmul_nt_cb("conv_pw1_bwd", dz, Wfull['conv_w_pw1'])
            grads['conv_w_pw1'] = wgrad("conv_pw1_wg", sv['h'], dz, cb=True)
        elif i == 2:
            dgt, = matmul("gmlp_out_bwd", dxb, Wfull['gmlp_w_out'], mode='nt', out_dtypes=(BF16,))
            grads['gmlp_w_out'] = _rows_blocked(wgrad("gmlp_out_wg", sv['gated'], dxb))
            du, dvln, dws, dsb = spatial_bwd(sv['u'], sv['vln'], dgt, ws, sbias)
            grads['gmlp_w_s'] = dws[None]
            grads['gmlp_b_s'] = dsb.reshape(GMLP_CHUNK, GMLP_HEADS, hw).sum(-1).T[None]

            def gm_pre_bwd(zt, dut, dvt, g, b):
                _, vjp_u = jax.vjp(jax.nn.gelu, zt[:, :D])
                _, vjp_v = jax.vjp(lambda zz, gg, bb: _ln(jax.nn.gelu(zz), gg, bb), zt[:, D:], g, b)
                dz2, dgg, dbb = vjp_v(dvt)
                return jnp.concatenate([vjp_u(dut)[0], dz2], axis=1), dgg, dbb
            dzp, dlg, dlb = rowwise("gmlp_pre_bwd", gm_pre_bwd, [sv['zp'], du, dvln],
                                    [W['gmlp_ln_g'], W['gmlp_ln_b']], [(2 * D, BF16)], [((1, D), F32), ((1, D), F32)])
            grads['gmlp_ln_g'], grads['gmlp_ln_b'] = dlg, dlb
            dh, = matmul_nt_cb("gmlp_in_bwd", dzp, Wfull['gmlp_w_in'])
            grads['gmlp_w_in'] = wgrad("gmlp_in_wg", sv['h'], dzp, cb=True)
        else:
            dm, = matmul_nt_cb("attn_o_bwd", dxb, Wfull['attn_w_o'])
            grads['attn_w_o'] = wgrad("attn_o_wg", sv['merged'], dxb, cb=True)
            w0, w1, w2 = sv['wts']
            do0, do1, do2 = rowwise("attn_merge_bwd", lambda d, a, b, c: (a * d, b * d, c * d), [dm, w0, w1, w2], [],
                                    [(ATT_W, BF16)] * 3)
            dparts = [[None] * 3 for _ in range(3)]
            for gi, (dog, (qb, kb, vb, lb, dil)) in enumerate(zip((do0, do1, do2), sv['blocks'])):
                dqb, dkb, dvb = attn_bwd("attn_bwd%d" % gi, qb, kb, vb, dog, sv['merged'], lb, dil, bsz)
                for j, t in enumerate((dqb, dkb, dvb)):
                    dparts[j][gi] = t
            dslots = [dparts[j][gi][:, hh * HEAD_DIM:(hh + 1) * HEAD_DIM]
                      for j in range(3) for gi in range(3) for hh in range(ATT_HEADS)]
            dqkv3 = _slots_to_blocked(dslots)
            dh, = matmul_nt_cb("attn_qkv_bwd", dqkv3, Wfull['attn_w_qkv'], a3=True)
            grads['attn_w_qkv'] = wgrad("attn_qkv_wg", sv['h'], dqkv3, cb=True, g3=True)
        dx, dxb, dg = rms_bwd("rms_mix_bwd%d" % i, xin, dh, dx, W['norm_mix'][i:i + 1])
        g_norm_mix[i] = dg.reshape(-1)

    grads['norm_mix'] = jnp.stack(g_norm_mix)
    grads['norm_mlp'] = jnp.stack(g_norm_mlp)
    grads['mlp_w_in'] = g_mlp_in
    grads['mlp_w_out'] = g_mlp_out
    return loss_local, dx.reshape(bsz, S, D), grads, (s5_disc_vjp, s5_cot)


def _rows_blocked(pair):
    return tuple(t.reshape(N_DEV, t.shape[0] // N_DEV, t.shape[1]) for t in pair)


def kernel(x, norm_mix, norm_mlp, norm_final, ssm_a_re, ssm_a_im, ssm_b_re, ssm_b_im, ssm_c_re, ssm_c_im, ssm_d, ssm_log_dt, ssm_w_glu, conv_w_pw1, conv_b_pw1, conv_w_dw, conv_b_dw, conv_ln_g, conv_ln_b, conv_w_pw2, conv_b_pw2, gmlp_w_in, gmlp_ln_g, gmlp_ln_b, gmlp_w_s, gmlp_b_s, gmlp_w_out, attn_w_qkv, attn_w_o, mlp_w_in, mlp_w_out, loss_target, m_norm_mix, m_norm_mlp, m_norm_final, m_ssm_a_re, m_ssm_a_im, m_ssm_b_re, m_ssm_b_im, m_ssm_c_re, m_ssm_c_im, m_ssm_d, m_ssm_log_dt, m_ssm_w_glu, m_conv_w_pw1, m_conv_b_pw1, m_conv_w_dw, m_conv_b_dw, m_conv_ln_g, m_conv_ln_b, m_conv_w_pw2, m_conv_b_pw2, m_gmlp_w_in, m_gmlp_ln_g, m_gmlp_ln_b, m_gmlp_w_s, m_gmlp_b_s, m_gmlp_w_out, m_attn_w_qkv, m_attn_w_o, m_mlp_w_in, m_mlp_w_out, v_norm_mix, v_norm_mlp, v_norm_final, v_ssm_a_re, v_ssm_a_im, v_ssm_b_re, v_ssm_b_im, v_ssm_c_re, v_ssm_c_im, v_ssm_d, v_ssm_log_dt, v_ssm_w_glu, v_conv_w_pw1, v_conv_b_pw1, v_conv_w_dw, v_conv_b_dw, v_conv_ln_g, v_conv_ln_b, v_conv_w_pw2, v_conv_b_pw2, v_gmlp_w_in, v_gmlp_ln_g, v_gmlp_ln_b, v_gmlp_w_s, v_gmlp_b_s, v_gmlp_w_out, v_attn_w_qkv, v_attn_w_o, v_mlp_w_in, v_mlp_w_out):
    args = (norm_mix, norm_mlp, norm_final, ssm_a_re, ssm_a_im, ssm_b_re, ssm_b_im, ssm_c_re, ssm_c_im, ssm_d,
            ssm_log_dt, ssm_w_glu, conv_w_pw1, conv_b_pw1, conv_w_dw, conv_b_dw, conv_ln_g, conv_ln_b, conv_w_pw2,
            conv_b_pw2, gmlp_w_in, gmlp_ln_g, gmlp_ln_b, gmlp_w_s, gmlp_b_s, gmlp_w_out, attn_w_qkv, attn_w_o,
            mlp_w_in, mlp_w_out)
    margs = (m_norm_mix, m_norm_mlp, m_norm_final, m_ssm_a_re, m_ssm_a_im, m_ssm_b_re, m_ssm_b_im, m_ssm_c_re,
             m_ssm_c_im, m_ssm_d, m_ssm_log_dt, m_ssm_w_glu, m_conv_w_pw1, m_conv_b_pw1, m_conv_w_dw, m_conv_b_dw,
             m_conv_ln_g, m_conv_ln_b, m_conv_w_pw2, m_conv_b_pw2, m_gmlp_w_in, m_gmlp_ln_g, m_gmlp_ln_b,
             m_gmlp_w_s, m_gmlp_b_s, m_gmlp_w_out, m_attn_w_qkv, m_attn_w_o, m_mlp_w_in, m_mlp_w_out)
    vargs = (v_norm_mix, v_norm_mlp, v_norm_final, v_ssm_a_re, v_ssm_a_im, v_ssm_b_re, v_ssm_b_im, v_ssm_c_re,
             v_ssm_c_im, v_ssm_d, v_ssm_log_dt, v_ssm_w_glu, v_conv_w_pw1, v_conv_b_pw1, v_conv_w_dw, v_conv_b_dw,
             v_conv_ln_g, v_conv_ln_b, v_conv_w_pw2, v_conv_b_pw2, v_gmlp_w_in, v_gmlp_ln_g, v_gmlp_ln_b,
             v_gmlp_w_s, v_gmlp_b_s, v_gmlp_w_out, v_attn_w_qkv, v_attn_w_o, v_mlp_w_in, v_mlp_w_out)
    Wl = dict(zip(WEIGHT_NAMES, args))
    Ml = dict(zip(WEIGHT_NAMES, margs))
    Vl = dict(zip(WEIGHT_NAMES, vargs))
    cx, cy, cc = _coords()
    my_idx = 4 * cx + 2 * cy + cc

    c_idx = cc.reshape(1).astype(jnp.int32)
    chip_idx = (2 * cx + cy).reshape(1).astype(jnp.int32)
    units = []
    for n in BIG:
        units += [(n, i) for i in range(DEPTH)] if Wl[n].shape[0] == DEPTH else [(n, None)]
    ss_names = list(SMALL_SHARDED)
    spack = _pack([Wl[n] for n in ss_names])
    gathered = all_gather("gather_weights", [Wl[n][0 if i is None else i].astype(BF16) for n, i in units] + [spack])
    Wfull = {}
    for (n, i), g in zip(units, gathered):
        w = g if BIG[n] == 2 else g.reshape(N_DEV * g.shape[1], g.shape[2])
        if i is None:
            Wfull[n] = w
        else:
            Wfull.setdefault(n, []).append(w)
    sparts = _unpack_gathered(gathered[-1], [Wl[n].shape for n in ss_names])
    W = {n: Wl[n] for n in SMALL if n not in SMALL_SHARDED}
    for n, p in zip(ss_names, sparts):
        W[n] = _unblocked(p, Wl[n].shape, SMALL_SHARDED[n])

    loss_local, grad_x, grads, (s5_disc_vjp, s5_cot) = _step(x, loss_target, W, Wfull)
    loss = lax.psum(loss_local, MESH_AXES)

    pairs = [grads[n] if i is None else grads[n][i] for n, i in units]
    tag = lambda n, i: n if i is None else "%s%d" % (n, i)
    recv1 = exchange_sibling("rs_sibling", [p[1] for p in pairs])
    p1 = [add_sibling("add_sibling_" + tag(n, i), p[0], r, c_idx) for (n, i), p, r in zip(units, pairs, recv1)]
    recv2 = exchange_chips("rs_chips", [p[1] for p in p1])
    outs4 = {}
    for (n, i), p, r in zip(units, p1, recv2):
        w2, m2, v2 = (d[n].reshape(-1, d[n].shape[-1]) for d in (Wl, Ml, Vl))
        res = adam_big("adam_" + tag(n, i), p[0], r, w2, m2, v2, chip_idx, layer=0 if i is None else i)
        if i is None:
            outs4[n] = [t.reshape(Wl[n].shape) for t in res]
        else:
            outs4.setdefault(n, []).append(res)
    for n in BIG:
        if Wl[n].shape[0] == DEPTH:
            outs4[n] = [jnp.stack([layer[k] for layer in outs4[n]]) for k in range(4)]
    out_g = {n: outs4[n][0] for n in BIG}
    out_d = {n: outs4[n][1] for n in BIG}
    out_m = {n: outs4[n][2] for n in BIG}
    out_v = {n: outs4[n][3] for n in BIG}

    s5_lin = ['ssm_a_re', 'ssm_a_im', 'ssm_log_dt', 'ssm_b_re', 'ssm_b_im']
    direct = [n for n in SMALL if n not in s5_lin]
    full_shape = lambda n: W[n].shape
    small_parts = [grads[n].reshape(full_shape(n)) for n in direct] + list(s5_cot)
    gsum = sum8(all_gather("gather_small_grads", [_pack(small_parts)])[0])
    summed = _unpack(gsum, [p.shape for p in small_parts])
    gsmall = dict(zip(direct, summed[:len(direct)]))
    s5g = s5_disc_vjp(tuple(summed[len(direct):]))
    for n, gval in zip(s5_lin, s5g):
        gsmall[n] = gval[None]
    for n, ax in SMALL_SHARDED.items():
        gsmall[n] = lax.dynamic_slice_in_dim(gsmall[n], my_idx * Wl[n].shape[ax], Wl[n].shape[ax], axis=ax)
    sm_shapes = [Wl[n].shape for n in SMALL]
    dS, mS, vS = adam_small(_pack([Wl[n] for n in SMALL]), _pack([gsmall[n] for n in SMALL]),
                            _pack([Ml[n] for n in SMALL]), _pack([Vl[n] for n in SMALL]))
    for n, gval in zip(SMALL, [gsmall[n] for n in SMALL]):
        out_g[n] = gval.reshape(Wl[n].shape)
    out_d.update(zip(SMALL, _unpack(dS, sm_shapes)))
    out_m.update(zip(SMALL, _unpack(mS, sm_shapes)))
    out_v.update(zip(SMALL, _unpack(vS, sm_shapes)))

    return (loss, grad_x, *[out_g[n] for n in WEIGHT_NAMES], *[out_d[n] for n in WEIGHT_NAMES],
            *[out_m[n] for n in WEIGHT_NAMES], *[out_v[n] for n in WEIGHT_NAMES])


def _unpack_gathered(g, shard_shapes):
    flat = g.reshape(N_DEV, -1)
    out, off = [], 0
    for s in shard_shapes:
        n = math.prod(s)
        out.append(flat[:, off:off + n].reshape((N_DEV,) + tuple(s)))
        off += n
    return out
```

```python
import functools
import math

import jax
import jax.numpy as jnp
from jax import lax
from jax.experimental import pallas as pl
from jax.experimental.pallas import tpu as pltpu

F32 = jnp.float32
BF16 = jnp.bfloat16

D_MODEL = 1024
DEPTH = 4
EPS = 1e-6
SSM_GROUP = 16
SSM_GROUPS = 64
SSM_STATE = 64
S5_GB = 8
S5_NGB = SSM_GROUPS // S5_GB
S5_CH = S5_GB * SSM_GROUP
S5_ST = S5_GB * SSM_STATE
S5_L = 128
CONV_WIDTH = 31
CONV_PAD = 32
CONV_TS = 256
CONV_CW = 256
GMLP_CHUNK = 128
GMLP_HEADS = 4
ATT_CONFIGS = ((128, 1), (512, 4), (2048, 16))
ATT_HEADS = 8
HEAD_DIM = 64
ATT_BLK = 128
ATT_TB = 2
ATT_W = ATT_HEADS * HEAD_DIM
N_DEV = 8
ADAM_LR = 0.001
ADAM_B1 = 0.9
ADAM_B2 = 0.999
ADAM_EPS = 1e-08
ADAM_WD = 0.01
ADAM_STEP = 10
VMEM_LIMIT = 56 * 1024 * 1024
PACK_C = 1024
MESH_AXES = ("x", "y", "c")
MESH = pl.DeviceIdType.MESH

WEIGHT_NAMES = ['norm_mix', 'norm_mlp', 'norm_final', 'ssm_a_re', 'ssm_a_im', 'ssm_b_re', 'ssm_b_im',
                'ssm_c_re', 'ssm_c_im', 'ssm_d', 'ssm_log_dt', 'ssm_w_glu', 'conv_w_pw1', 'conv_b_pw1',
                'conv_w_dw', 'conv_b_dw', 'conv_ln_g', 'conv_ln_b', 'conv_w_pw2', 'conv_b_pw2',
                'gmlp_w_in', 'gmlp_ln_g', 'gmlp_ln_b', 'gmlp_w_s', 'gmlp_b_s', 'gmlp_w_out',
                'attn_w_qkv', 'attn_w_o', 'mlp_w_in', 'mlp_w_out']
BIG = {'ssm_w_glu': 2, 'conv_w_pw1': 2, 'conv_w_pw2': 1, 'gmlp_w_in': 2, 'gmlp_w_out': 1,
       'attn_w_qkv': 2, 'attn_w_o': 2, 'mlp_w_in': 2, 'mlp_w_out': 1}
SMALL_SHARDED = {'conv_b_pw1': 1, 'conv_w_dw': 2, 'conv_b_dw': 1, 'conv_ln_g': 1, 'conv_ln_b': 1,
                 'conv_b_pw2': 1, 'gmlp_ln_g': 1, 'gmlp_ln_b': 1}
SMALL = [n for n in WEIGHT_NAMES if n not in BIG]


def _cparams(sem=None):
    return pltpu.CompilerParams(dimension_semantics=sem, vmem_limit_bytes=VMEM_LIMIT)


def _dot(a, b):
    return jnp.dot(a, b, preferred_element_type=F32)


def _dot_nt(a, b):
    return lax.dot_general(a, b, (((1,), (1,)), ((), ())), preferred_element_type=F32)


def _dot_tn(a, b):
    return lax.dot_general(a, b, (((0,), (0,)), ((), ())), preferred_element_type=F32)


ROW_TILE_BYTES = 10 << 20


def _rows_for(T, row_bytes, cap=1024):
    tr = min(cap, T)
    while tr > 8 and (T % tr or tr * row_bytes > ROW_TILE_BYTES):
        tr //= 2
    assert T % tr == 0 and tr % 8 == 0
    return tr


def rowwise(name, fn, rows, params, row_out, acc_out=(), tr=None):
    T = rows[0].shape[0]
    row_bytes = (sum(r.shape[1] * r.dtype.itemsize for r in rows)
                 + sum(c * jnp.dtype(dt).itemsize for c, dt in row_out))
    tr = _rows_for(T, row_bytes, cap=tr or 1024)
    nr, npar, nro = len(rows), len(params), len(row_out)

    def body(*refs):
        ins = [r[...] for r in refs[:nr + npar]]
        outs = refs[nr + npar:]
        res = fn(*ins)
        if not isinstance(res, (tuple, list)):
            res = (res,)
        for k in range(nro):
            outs[k][...] = res[k].astype(outs[k].dtype)
        if acc_out:
            @pl.when(pl.program_id(0) == 0)
            def _():
                for k in range(nro, len(outs)):
                    outs[k][...] = jnp.zeros_like(outs[k])
            for k in range(nro, len(outs)):
                outs[k][...] += res[k].astype(outs[k].dtype)

    in_specs = [pl.BlockSpec((tr, r.shape[1]), lambda i: (i, 0)) for r in rows]
    in_specs += [pl.BlockSpec(p.shape, lambda i, nd=p.ndim: (0,) * nd) for p in params]
    out_shape = [jax.ShapeDtypeStruct((T, c), dt) for c, dt in row_out]
    out_specs = [pl.BlockSpec((tr, c), lambda i: (i, 0)) for c, dt in row_out]
    out_shape += [jax.ShapeDtypeStruct(s, dt) for s, dt in acc_out]
    out_specs += [pl.BlockSpec(s, lambda i, nd=len(s): (0,) * nd) for s, dt in acc_out]
    res = pl.pallas_call(body, grid=(T // tr,), in_specs=in_specs, out_specs=out_specs, out_shape=out_shape,
                         name=name, compiler_params=_cparams(("arbitrary",)))(*rows, *params)
    return res


def _tile_m(M, K):
    tm = 2048
    while tm > 256 and tm * K * 2 > (4 << 20):
        tm //= 2
    return min(tm, M)


def matmul(name, a, b, *, mode='nn', epi=None, extras=(), out_dtypes=(F32,), out3=False):
    M, K = a.shape
    if mode == 'cb':
        nblk, _, tn = b.shape
        N = nblk * tn
    else:
        N = b.shape[0] if mode == 'nt' else b.shape[1]
        tn = min(512, N)
    row_bytes = (K * 2 + sum(N * jnp.dtype(dt).itemsize for dt in out_dtypes)
                 + sum(N * arr.dtype.itemsize for arr, kind in extras if kind == 'tile'))
    tm = _rows_for(M, row_bytes)
    assert N % tn == 0, (M, N, tm, tn)
    nex = len(extras)

    def body(a_ref, b_ref, *rest):
        ex_refs, outs = rest[:nex], rest[nex:]
        av = a_ref[...]
        for c in range(N // tn):
            cs = slice(c * tn, (c + 1) * tn)
            if mode == 'cb':
                acc = _dot(av, b_ref[c])
            elif mode == 'nt':
                acc = _dot_nt(av, b_ref[cs, :])
            else:
                acc = _dot(av, b_ref[:, cs])
            res = epi(acc, *[e[:, cs] for e in ex_refs]) if epi is not None else (acc,)
            for o, r in zip(outs, res):
                if out3:
                    o[c] = r.astype(o.dtype)
                else:
                    o[:, cs] = r.astype(o.dtype)

    in_specs = [pl.BlockSpec((tm, K), lambda i: (i, 0)), pl.BlockSpec(b.shape, lambda i, nd=b.ndim: (0,) * nd)]
    for arr, kind in extras:
        in_specs.append(pl.BlockSpec((tm, N), lambda i: (i, 0)) if kind == 'tile'
                        else pl.BlockSpec((1, N), lambda i: (0, 0)))
    if out3:
        out_shape = [jax.ShapeDtypeStruct((N // tn, M, tn), dt) for dt in out_dtypes]
        out_specs = [pl.BlockSpec((N // tn, tm, tn), lambda i: (0, i, 0)) for dt in out_dtypes]
    else:
        out_shape = [jax.ShapeDtypeStruct((M, N), dt) for dt in out_dtypes]
        out_specs = [pl.BlockSpec((tm, N), lambda i: (i, 0)) for dt in out_dtypes]
    return pl.pallas_call(body, grid=(M // tm,), in_specs=in_specs, out_specs=out_specs,
                          out_shape=out_shape, name=name,
                          compiler_params=_cparams(("arbitrary",)))(a, b, *[e[0] for e in extras])


def matmul_nt_cb(name, a, b, *, a3=False, epi=None, extras=(), out_dtypes=(F32,)):
    nblk, K, n = b.shape
    M = a.shape[1] if a3 else a.shape[0]
    tm = _tile_m(M, nblk * n)
    assert M % tm == 0
    nex = len(extras)

    def body(a_ref, b_ref, *rest):
        ex, outs = rest[:nex], rest[nex:]
        acc = None
        for j in range(nblk):
            aj = a_ref[j] if a3 else a_ref[:, j * n:(j + 1) * n]
            part = _dot_nt(aj, b_ref[j])
            acc = part if acc is None else acc + part
        res = epi(acc, *[e[...] for e in ex]) if epi is not None else (acc,)
        for o, r in zip(outs, res):
            o[...] = r.astype(o.dtype)

    a_spec = (pl.BlockSpec((nblk, tm, n), lambda i: (0, i, 0)) if a3
              else pl.BlockSpec((tm, nblk * n), lambda i: (i, 0)))
    row = pl.BlockSpec((tm, K), lambda i: (i, 0))
    return pl.pallas_call(
        body, grid=(M // tm,),
        in_specs=[a_spec, pl.BlockSpec((nblk, K, n), lambda i: (0, 0, 0))] + [row] * nex,
        out_specs=[row] * len(out_dtypes), out_shape=[jax.ShapeDtypeStruct((M, K), dt) for dt in out_dtypes],
        name=name, compiler_params=_cparams(("arbitrary",)))(a, b, *extras)


def wgrad(name, a, g, *, cb=False, g3=False):
    M, K = a.shape
    tm, tk = min(M, 1024), min(K, 1024)
    if cb:
        n = g.shape[2] if g3 else g.shape[1] // N_DEV
        nj = N_DEV
        while nj > 1 and nj * tk * n * 6 > (14 << 20):
            nj //= 2
        grid = (K // tk, N_DEV // nj, M // tm)
        g_spec = (pl.BlockSpec((nj, tm, n), lambda k, j, m: (j, m, 0)) if g3
                  else pl.BlockSpec((tm, nj * n), lambda k, j, m: (m, j)))
        o_spec = pl.BlockSpec((nj, tk, n), lambda k, j, m: (j, k, 0))
        o_shape = (N_DEV, K, n)
    else:
        N = g.shape[1]
        tn = min(N, 1024)
        nj = 1
        grid = (K // tk, N // tn, M // tm)
        g_spec = pl.BlockSpec((tm, tn), lambda k, j, m: (m, j))
        o_spec = pl.BlockSpec((tk, tn), lambda k, j, m: (k, j))
        o_shape = (K, N)
    nm = M // tm

    def body(a_ref, g_ref, o_ref, o16_ref):
        m = pl.program_id(2)

        @pl.when(m == 0)
        def _():
            o_ref[...] = jnp.zeros_like(o_ref)
        at = a_ref[...].T
        if cb:
            for jj in range(nj):
                gj = g_ref[jj] if g3 else g_ref[:, jj * n:(jj + 1) * n]
                o_ref[jj] += _dot(at, gj)
        else:
            o_ref[...] += _dot(at, g_ref[...])

        @pl.when(m == nm - 1)
        def _():
            o16_ref[...] = o_ref[...].astype(BF16)

    return pl.pallas_call(
        body, grid=grid, in_specs=[pl.BlockSpec((tm, tk), lambda k, j, m: (m, k)), g_spec],
        out_specs=[o_spec, o_spec],
        out_shape=[jax.ShapeDtypeStruct(o_shape, F32), jax.ShapeDtypeStruct(o_shape, BF16)], name=name,
        compiler_params=_cparams(("arbitrary", "arbitrary", "arbitrary")))(a, g)


def _rms(x, g):
    x = x.astype(F32)
    return x * lax.rsqrt(jnp.mean(x * x, axis=-1, keepdims=True) + EPS) * g


def _ln(x, g, b):
    mu = jnp.mean(x, axis=-1, keepdims=True)
    var = jnp.mean(jnp.square(x - mu), axis=-1, keepdims=True)
    return (x - mu) * lax.rsqrt(var + EPS) * g + b


def _glu(z):
    d = z.shape[1] // 2
    return z[:, :d] * jax.nn.sigmoid(z[:, d:])


def _glu_bwd(z, dy):
    d = z.shape[1] // 2
    a, s = z[:, :d], jax.nn.sigmoid(z[:, d:])
    return jnp.concatenate([dy * s, dy * a * s * (1.0 - s)], axis=1)


def _colsum(v):
    return jnp.sum(v.astype(F32), axis=0, keepdims=True)


def rms_fwd(name, x, g, want_f32=False):
    def fn(xt, gt):
        h = _rms(xt, gt)
        return (h, h) if want_f32 else (h,)
    D = x.shape[1]
    outs = [(D, BF16)] + ([(D, F32)] if want_f32 else [])
    return rowwise(name, fn, [x], [g], outs)


def rms_bwd(name, x, dh, dres, g):
    def fn(xt, dht, drt, gt):
        _, vjp = jax.vjp(_rms, xt, gt)
        dx, dg = vjp(dht.astype(F32))
        dx = dx + drt
        return dx, dx, dg
    D = x.shape[1]
    return rowwise(name, fn, [x, dh, dres], [g], [(D, F32), (D, BF16)], [((1, D), F32)])


def s5_disc(a_re, a_im, log_dt, b_re, b_im):
    dt = jnp.exp(log_dt)[:, None]
    er = jnp.exp(a_re * dt)
    lam_re = er * jnp.cos(a_im * dt)
    lam_im = er * jnp.sin(a_im * dt)
    nr, ni = lam_re - 1.0, lam_im
    den = a_re * a_re + a_im * a_im
    f_re = (nr * a_re + ni * a_im) / den
    f_im = (ni * a_re - nr * a_im) / den
    bb_re = f_re[..., None] * b_re - f_im[..., None] * b_im
    bb_im = f_re[..., None] * b_im + f_im[..., None] * b_re
    return lam_re, lam_im, bb_re, bb_im


def _s5_blockdiag_b(bb):
    t = bb.reshape(S5_NGB, S5_GB, SSM_STATE, SSM_GROUP).transpose(0, 1, 3, 2)
    eye = jnp.eye(S5_GB, dtype=bb.dtype)
    return jnp.einsum('bgpn,gh->bgphn', t, eye).reshape(S5_NGB, S5_CH, S5_ST)


def _s5_blockdiag_b_inv(x):
    t = x.reshape(S5_NGB, S5_GB, SSM_GROUP, S5_GB, SSM_STATE)
    eye = jnp.eye(S5_GB, dtype=x.dtype)
    d = jnp.einsum('bgphn,gh->bgpn', t, eye)
    return d.transpose(0, 1, 3, 2).reshape(SSM_GROUPS, SSM_STATE, SSM_GROUP)


def _s5_blockdiag_c(c):
    t = c.reshape(S5_NGB, S5_GB, SSM_GROUP, SSM_STATE).transpose(0, 1, 3, 2)
    eye = jnp.eye(S5_GB, dtype=c.dtype)
    return jnp.einsum('bgnp,gh->bgnhp', t, eye).reshape(S5_NGB, S5_ST, S5_CH)


def _s5_blockdiag_c_inv(x):
    t = x.reshape(S5_NGB, S5_GB, SSM_STATE, S5_GB, SSM_GROUP)
    eye = jnp.eye(S5_GB, dtype=x.dtype)
    d = jnp.einsum('bgnhp,gh->bgnp', t, eye)
    return d.transpose(0, 1, 3, 2).reshape(SSM_GROUPS, SSM_GROUP, SSM_STATE)


def s5_tables(lam_re, lam_im, L):
    pr, pi = lam_re.reshape(1, -1), lam_im.reshape(1, -1)
    n = 1
    while n < L:
        lr, li = pr[n - 1:n], pi[n - 1:n]
        pr, pi = (jnp.concatenate([pr, pr * lr - pi * li], 0), jnp.concatenate([pi, pr * li + pi * lr], 0))
        n *= 2
    nk = int(math.log2(L))
    idx = [2 ** k - 1 for k in range(nk)] + [0] * (8 - nk)

    def blk(t):
        return t.reshape(t.shape[0], S5_NGB, S5_ST).transpose(1, 0, 2)

    def rows(t):
        return jnp.concatenate([t[j:j + 1] for j in idx], axis=0)
    return blk(pr), blk(pi), blk(rows(pr)), blk(rows(pi))


S5_SUB = 8


def _scan_tiles(br, bi, a2r, a2i, reverse):
    L = br.shape[0]
    sub = lax.broadcasted_iota(jnp.int32, br.shape, 0) & (S5_SUB - 1)
    xr, xi = br, bi
    for k in range(3):
        s = 1 << k
        ar, ai = a2r[k:k + 1, :], a2i[k:k + 1, :]
        if reverse:
            sr, si = pltpu.roll(xr, L - s, 0), pltpu.roll(xi, L - s, 0)
            m = sub < S5_SUB - s
        else:
            sr, si = pltpu.roll(xr, s, 0), pltpu.roll(xi, s, 0)
            m = sub >= s
        sr, si = jnp.where(m, sr, 0.0), jnp.where(m, si, 0.0)
        xr, xi = xr + ar * sr - ai * si, xi + ar * si + ai * sr
    return xr, xi


def _scan_chain(xr, xi, pr, pi, cr, ci, out_r, out_i, reverse):
    ntile = xr.shape[0] // S5_SUB
    for g in (reversed(range(ntile)) if reverse else range(ntile)):
        rs = slice(g * S5_SUB, (g + 1) * S5_SUB)
        if reverse:
            nr = xr[rs] + pr * cr + pi * ci
            ni = xi[rs] + pr * ci - pi * cr
            cr, ci = nr[0:1], ni[0:1]
        else:
            nr = xr[rs] + pr * cr - pi * ci
            ni = xi[rs] + pr * ci + pi * cr
            cr, ci = nr[S5_SUB - 1:S5_SUB], ni[S5_SUB - 1:S5_SUB]
        out_r[rs, :] = nr
        out_i[rs, :] = ni
    return cr, ci


def _grid_step(shape):
    s = 0
    for ax, n in enumerate(shape):
        s = s * n + pl.program_id(ax)
    return s


def s5_fwd(h, bre, bim, cre, cim, pwr, pwi, l2r, l2i, dskip, bsz, gather=()):
    T, D = h.shape
    L = S5_L
    S = T // bsz
    NC = S // L
    ng = len(gather)
    grid = (S5_NGB, bsz, NC)
    nsteps = S5_NGB * bsz * NC
    fwd_step = nsteps - max(1, nsteps // 32)

    def body(*refs):
        (h_ref, bre_ref, bim_ref, cre_ref, cim_ref, pwr_ref, pwi_ref, l2r_ref, l2i_ref, d_ref) = refs[:10]
        x_refs = refs[10:10 + ng]
        y_ref, gy_ref, xs_ref = refs[10 + ng:13 + ng]
        g_refs = refs[13 + ng:13 + 2 * ng]
        car_r, car_i, xr_s, xi_s = refs[13 + 2 * ng:17 + 2 * ng]
        if ng:
            start, forward, finish = _gather_phases(x_refs, g_refs, *refs[17 + 2 * ng:])
            step = _grid_step(grid)
            pl.when(step == 0)(start)
            pl.when(step == fwd_step)(forward)

        @pl.when(pl.program_id(2) == 0)
        def _():
            car_r[...] = jnp.zeros_like(car_r)
            car_i[...] = jnp.zeros_like(car_i)
        u = h_ref[...]
        ub = u.astype(BF16)
        cr, ci = car_r[0:1, :], car_i[0:1, :]
        xs_ref[...] = jnp.zeros_like(xs_ref)
        xs_ref[0:1, :] = cr
        xs_ref[1:2, :] = ci
        xr, xi = _scan_tiles(_dot(ub, bre_ref[...]), _dot(ub, bim_ref[...]), l2r_ref[...], l2i_ref[...], False)
        cr, ci = _scan_chain(xr, xi, pwr_ref[...], pwi_ref[...], cr, ci, xr_s, xi_s, False)
        car_r[...] = jnp.broadcast_to(cr, car_r.shape)
        car_i[...] = jnp.broadcast_to(ci, car_i.shape)
        y = (_dot(xr_s[...].astype(BF16), cre_ref[...]) - _dot(xi_s[...].astype(BF16), cim_ref[...])
             + d_ref[...] * u)
        y_ref[...] = y
        gy_ref[...] = jax.nn.gelu(y).astype(BF16)
        if ng:
            pl.when(step == nsteps - 1)(finish)

    tok = lambda g, b, c: (b * NC + c, g)
    par = lambda g, b, c: (g, 0, 0)
    anyspec = pl.BlockSpec(memory_space=pl.ANY)
    return pl.pallas_call(
        body, grid=grid,
        in_specs=[pl.BlockSpec((L, S5_CH), tok),
                  pl.BlockSpec((None, S5_CH, S5_ST), par), pl.BlockSpec((None, S5_CH, S5_ST), par),
                  pl.BlockSpec((None, S5_ST, S5_CH), par), pl.BlockSpec((None, S5_ST, S5_CH), par),
                  pl.BlockSpec((None, 8, S5_ST), par), pl.BlockSpec((None, 8, S5_ST), par),
                  pl.BlockSpec((None, 8, S5_ST), par), pl.BlockSpec((None, 8, S5_ST), par),
                  pl.BlockSpec((1, S5_CH), lambda g, b, c: (0, g))] + [anyspec] * ng,
        out_specs=[pl.BlockSpec((L, S5_CH), tok), pl.BlockSpec((L, S5_CH), tok),
                   pl.BlockSpec((None, 8, S5_ST), lambda g, b, c: (b * NC + c, 0, g))] + [anyspec] * ng,
        out_shape=[jax.ShapeDtypeStruct((T, D), F32), jax.ShapeDtypeStruct((T, D), BF16),
                   jax.ShapeDtypeStruct((bsz * NC, 8, S5_NGB * S5_ST), F32)] + _gather_out_shapes(gather),
        scratch_shapes=[pltpu.VMEM((8, S5_ST), F32), pltpu.VMEM((8, S5_ST), F32),
                        pltpu.VMEM((L, S5_ST), F32), pltpu.VMEM((L, S5_ST), F32)]
        + (_gather_sems(ng) if ng else []),
        name="s5_fwd", compiler_params=_cparams(("arbitrary", "arbitrary", "arbitrary")),
    )(h, bre, bim, cre, cim, pwr, pwi, l2r, l2i, dskip, *gather)


def s5_bwd(h, dy, xs, bre, bim, cre, cim, pwr, pwi, pwr_rev, pwi_rev, l2r, l2i, dskip, bsz, chips=()):
    T, D = h.shape
    L = S5_L
    S = T // bsz
    NC = S // L
    nc = len(chips)
    grid = (S5_NGB, bsz, NC)
    nsteps = S5_NGB * bsz * NC

    def body(*refs):
        (h_ref, dy_ref, xs_ref, bre_ref, bim_ref, cre_ref, cim_ref, pwr_ref, pwi_ref, qr_ref, qi_ref,
         l2r_ref, l2i_ref, d_ref) = refs[:14]
        p_refs = refs[14:14 + nc]
        du_ref, dbr_ref, dbi_ref, dcr_ref, dci_ref, dl_ref, dd_ref = refs[14 + nc:21 + nc]
        r_refs = refs[21 + nc:21 + 2 * nc]
        car_r, car_i, xr_s, xi_s, dr_s, di_s = refs[21 + 2 * nc:27 + 2 * nc]
        if nc:
            start, finish = _chips_phases(p_refs, r_refs, *refs[27 + 2 * nc:])
            step = _grid_step(grid)
            pl.when(step == 0)(start)
        first = (pl.program_id(1) == 0) & (pl.program_id(2) == 0)

        @pl.when(first)
        def _():
            for r in (dbr_ref, dbi_ref, dcr_ref, dci_ref, dl_ref, dd_ref):
                r[...] = jnp.zeros_like(r)

        @pl.when(pl.program_id(2) == 0)
        def _():
            car_r[...] = jnp.zeros_like(car_r)
            car_i[...] = jnp.zeros_like(car_i)

        u = h_ref[...]
        ub = u.astype(BF16)
        dyv = dy_ref[...]
        dyb = dyv.astype(BF16)
        l2r_v, l2i_v = l2r_ref[...], l2i_ref[...]
        x0r, x0i = xs_ref[0:1, :], xs_ref[1:2, :]
        xr, xi = _scan_tiles(_dot(ub, bre_ref[...]), _dot(ub, bim_ref[...]), l2r_v, l2i_v, False)
        _scan_chain(xr, xi, pwr_ref[...], pwi_ref[...], x0r, x0i, xr_s, xi_s, False)
        xr, xi = xr_s[...], xi_s[...]
        gr = _dot_nt(dyb, cre_ref[...])
        gi = -_dot_nt(dyb, cim_ref[...])
        dr, di = _scan_tiles(gr, gi, l2r_v, -l2i_v, True)
        cr, ci = _scan_chain(dr, di, qr_ref[...], qi_ref[...], car_r[0:1, :], car_i[0:1, :], dr_s, di_s, True)
        dr, di = dr_s[...], di_s[...]
        car_r[...] = jnp.broadcast_to(cr, car_r.shape)
        car_i[...] = jnp.broadcast_to(ci, car_i.shape)
        row = lax.broadcasted_iota(jnp.int32, xr.shape, 0)
        xpr = jnp.where(row >= 1, pltpu.roll(xr, 1, 0), x0r)
        xpi = jnp.where(row >= 1, pltpu.roll(xi, 1, 0), x0i)
        dl_ref[0:1, :] += _colsum(dr * xpr + di * xpi)
        dl_ref[1:2, :] += _colsum(di * xpr - dr * xpi)
        drb, dib = dr.astype(BF16), di.astype(BF16)
        dcr_ref[...] += _dot_tn(xr.astype(BF16), dyb)
        dci_ref[...] -= _dot_tn(xi.astype(BF16), dyb)
        dbr_ref[...] += _dot_tn(ub, drb)
        dbi_ref[...] += _dot_tn(ub, dib)
        du_ref[...] = _dot_nt(drb, bre_ref[...]) + _dot_nt(dib, bim_ref[...]) + d_ref[...] * dyv
        dd_ref[0:1, :] += _colsum(dyv * u)
        if nc:
            pl.when(step == nsteps - 1)(finish)

    tok = lambda g, b, c: (b * NC + (NC - 1 - c), g)
    par = lambda g, b, c: (g, 0, 0)
    anyspec = pl.BlockSpec(memory_space=pl.ANY)
    return pl.pallas_call(
        body, grid=grid,
        in_specs=[pl.BlockSpec((L, S5_CH), tok), pl.BlockSpec((L, S5_CH), tok),
                  pl.BlockSpec((None, 8, S5_ST), lambda g, b, c: (b * NC + (NC - 1 - c), 0, g)),
                  pl.BlockSpec((None, S5_CH, S5_ST), par), pl.BlockSpec((None, S5_CH, S5_ST), par),
                  pl.BlockSpec((None, S5_ST, S5_CH), par), pl.BlockSpec((None, S5_ST, S5_CH), par),
                  pl.BlockSpec((None, 8, S5_ST), par), pl.BlockSpec((None, 8, S5_ST), par),
                  pl.BlockSpec((None, 8, S5_ST), par), pl.BlockSpec((None, 8, S5_ST), par),
                  pl.BlockSpec((None, 8, S5_ST), par), pl.BlockSpec((None, 8, S5_ST), par),
                  pl.BlockSpec((1, S5_CH), lambda g, b, c: (0, g))] + [anyspec] * nc,
        out_specs=[pl.BlockSpec((L, S5_CH), tok),
                   pl.BlockSpec((None, S5_CH, S5_ST), par), pl.BlockSpec((None, S5_CH, S5_ST), par),
                   pl.BlockSpec((None, S5_ST, S5_CH), par), pl.BlockSpec((None, S5_ST, S5_CH), par),
                   pl.BlockSpec((None, 8, S5_ST), par),
                   pl.BlockSpec((8, S5_CH), lambda g, b, c: (0, g))] + [anyspec] * nc,
        out_shape=[jax.ShapeDtypeStruct((T, D), F32),
                   jax.ShapeDtypeStruct((S5_NGB, S5_CH, S5_ST), F32), jax.ShapeDtypeStruct((S5_NGB, S5_CH, S5_ST), F32),
                   jax.ShapeDtypeStruct((S5_NGB, S5_ST, S5_CH), F32), jax.ShapeDtypeStruct((S5_NGB, S5_ST, S5_CH), F32),
                   jax.ShapeDtypeStruct((S5_NGB, 8, S5_ST), F32), jax.ShapeDtypeStruct((8, D), F32)]
        + _chips_out_shapes(chips),
        scratch_shapes=[pltpu.VMEM((8, S5_ST), F32), pltpu.VMEM((8, S5_ST), F32)]
        + [pltpu.VMEM((L, S5_ST), F32)] * 4 + (_chips_sems(nc) if nc else []),
        name="s5_bwd", compiler_params=_cparams(("arbitrary", "arbitrary", "arbitrary")),
    )(h, dy, xs, bre, bim, cre, cim, pwr, pwi, pwr_rev, pwi_rev, l2r, l2i, dskip, *chips)


def _shift_rows(win, off, n):
    if off == 0:
        return win[:n]
    return pltpu.roll(win, win.shape[0] - off, 0)[:n]


def dwconv_fwd(z, w, b, bsz):
    T, D = z.shape
    S = T // bsz
    TS, CW, PAD = CONV_TS, CONV_CW, CONV_PAD

    def body(z_ref, w_ref, b_ref, y_ref, zp):
        zp[0:PAD, :] = jnp.zeros((PAD, CW), F32)
        zp[PAD:, :] = z_ref[...]
        wv, bv = w_ref[...], b_ref[...]

        def step(t, carry):
            base = pl.multiple_of(t * TS, TS)
            win = zp[pl.ds(base, TS + PAD), :]
            acc = jnp.zeros((TS, CW), F32) + bv
            for k in range(CONV_WIDTH):
                acc = acc + wv[k:k + 1, :] * _shift_rows(win, PAD - (CONV_WIDTH - 1) + k, TS)
            y_ref[pl.ds(base, TS), :] = acc
            return carry
        lax.fori_loop(0, S // TS, step, 0)

    return pl.pallas_call(
        body, grid=(D // CW, bsz),
        in_specs=[pl.BlockSpec((S, CW), lambda c, bb: (bb, c)), pl.BlockSpec((32, CW), lambda c, bb: (0, c)),
                  pl.BlockSpec((1, CW), lambda c, bb: (0, c))],
        out_specs=pl.BlockSpec((S, CW), lambda c, bb: (bb, c)),
        out_shape=jax.ShapeDtypeStruct((T, D), F32),
        scratch_shapes=[pltpu.VMEM((S + PAD, CW), F32)],
        name="dwconv_fwd", compiler_params=_cparams(("arbitrary", "arbitrary")),
    )(z, w, b)


def dwconv_bwd(z, dy, w, bsz):
    T, D = z.shape
    S = T // bsz
    TS, CW, PAD = CONV_TS, CONV_CW, CONV_PAD

    def body(z_ref, dy_ref, w_ref, dz_ref, dw_ref, db_ref, zp, dyp):
        @pl.when(pl.program_id(1) == 0)
        def _():
            dw_ref[...] = jnp.zeros_like(dw_ref)
            db_ref[...] = jnp.zeros_like(db_ref)
        zp[0:PAD, :] = jnp.zeros((PAD, CW), F32)
        zp[PAD:, :] = z_ref[...]
        dyp[0:S, :] = dy_ref[...]
        dyp[S:, :] = jnp.zeros((PAD, CW), F32)
        wv = w_ref[...]

        def step(t, carry):
            base = pl.multiple_of(t * TS, TS)
            zwin = zp[pl.ds(base, TS + PAD), :]
            dwin = dyp[pl.ds(base, TS + PAD), :]
            dyt = dwin[:TS]
            acc = jnp.zeros((TS, CW), F32)
            for j in range(CONV_WIDTH):
                k = CONV_WIDTH - 1 - j
                acc = acc + wv[k:k + 1, :] * _shift_rows(dwin, j, TS)
            dz_ref[pl.ds(base, TS), :] = acc
            for k in range(CONV_WIDTH):
                dw_ref[k:k + 1, :] += _colsum(dyt * _shift_rows(zwin, PAD - (CONV_WIDTH - 1) + k, TS))
            db_ref[0:1, :] += _colsum(dyt)
            return carry
        lax.fori_loop(0, S // TS, step, 0)

    return pl.pallas_call(
        body, grid=(D // CW, bsz),
        in_specs=[pl.BlockSpec((S, CW), lambda c, bb: (bb, c)), pl.BlockSpec((S, CW), lambda c, bb: (bb, c)),
                  pl.BlockSpec((32, CW), lambda c, bb: (0, c))],
        out_specs=[pl.BlockSpec((S, CW), lambda c, bb: (bb, c)), pl.BlockSpec((32, CW), lambda c, bb: (0, c)),
                   pl.BlockSpec((8, CW), lambda c, bb: (0, c))],
        out_shape=[jax.ShapeDtypeStruct((T, D), F32), jax.ShapeDtypeStruct((32, D), F32),
                   jax.ShapeDtypeStruct((8, D), F32)],
        scratch_shapes=[pltpu.VMEM((S + PAD, CW), F32), pltpu.VMEM((S + PAD, CW), F32)],
        name="dwconv_bwd", compiler_params=_cparams(("arbitrary", "arbitrary")),
    )(z, dy, w)


def spatial_fwd(u, vln, ws, bias):
    T, E = u.shape
    C, H = GMLP_CHUNK, GMLP_HEADS
    hw = E // H

    def body(u_ref, v_ref, ws_ref, b_ref, o_ref):
        for hh in range(H):
            sl = slice(hh * hw, (hh + 1) * hw)
            vp = _dot(ws_ref[hh], v_ref[:, sl]) + b_ref[:, sl]
            o_ref[:, sl] = (u_ref[:, sl] * vp).astype(o_ref.dtype)

    return pl.pallas_call(
        body, grid=(T // C,),
        in_specs=[pl.BlockSpec((C, E), lambda i: (i, 0)), pl.BlockSpec((C, E), lambda i: (i, 0)),
                  pl.BlockSpec((H, C, C), lambda i: (0, 0, 0)), pl.BlockSpec((C, E), lambda i: (0, 0))],
        out_specs=pl.BlockSpec((C, E), lambda i: (i, 0)),
        out_shape=jax.ShapeDtypeStruct((T, E), BF16),
        name="spatial_fwd", compiler_params=_cparams(("arbitrary",)),
    )(u, vln, ws, bias)


def spatial_bwd(u, vln, dg, ws, bias):
    T, E = u.shape
    C, H = GMLP_CHUNK, GMLP_HEADS
    hw = E // H

    def body(u_ref, v_ref, dg_ref, ws_ref, b_ref, du_ref, dv_ref, dws_ref, db_ref):
        @pl.when(pl.program_id(0) == 0)
        def _():
            dws_ref[...] = jnp.zeros_like(dws_ref)
            db_ref[...] = jnp.zeros_like(db_ref)
        tril = (lax.broadcasted_iota(jnp.int32, (C, C), 1) <= lax.broadcasted_iota(jnp.int32, (C, C), 0))
        for hh in range(H):
            sl = slice(hh * hw, (hh + 1) * hw)
            v = v_ref[:, sl]
            w = ws_ref[hh]
            dgv = dg_ref[:, sl].astype(F32)
            vp = _dot(w, v) + b_ref[:, sl]
            du_ref[:, sl] = dgv * vp
            dvp = dgv * u_ref[:, sl]
            dvpb = dvp.astype(BF16)
            dv_ref[:, sl] = _dot_tn(w, dvpb)
            dws_ref[hh] += jnp.where(tril, _dot_nt(dvpb, v), 0.0)
            db_ref[:, sl] += dvp

    return pl.pallas_call(
        body, grid=(T // C,),
        in_specs=[pl.BlockSpec((C, E), lambda i: (i, 0)), pl.BlockSpec((C, E), lambda i: (i, 0)),
                  pl.BlockSpec((C, E), lambda i: (i, 0)),
                  pl.BlockSpec((H, C, C), lambda i: (0, 0, 0)), pl.BlockSpec((C, E), lambda i: (0, 0))],
        out_specs=[pl.BlockSpec((C, E), lambda i: (i, 0)), pl.BlockSpec((C, E), lambda i: (i, 0)),
                   pl.BlockSpec((H, C, C), lambda i: (0, 0, 0)), pl.BlockSpec((C, E), lambda i: (0, 0))],
        out_shape=[jax.ShapeDtypeStruct((T, E), F32), jax.ShapeDtypeStruct((T, E), F32),
                   jax.ShapeDtypeStruct((H, C, C), F32), jax.ShapeDtypeStruct((C, E), F32)],
        name="spatial_bwd", compiler_params=_cparams(("arbitrary",)),
    )(u, vln, dg, ws, bias)


def _att_masks():
    r = lax.broadcasted_iota(jnp.int32, (ATT_BLK, ATT_BLK), 0)
    c = lax.broadcasted_iota(jnp.int32, (ATT_BLK, ATT_BLK), 1)
    return c <= r, c >= r


NEG = -1e30
ATT_SCALE = HEAD_DIM ** -0.5


def _att_view(t, dil):
    return t.reshape(t.shape[0] // dil, dil * t.shape[1])


def attn_fwd(name, q, k, v, dil, bsz):
    T, Wd = q.shape
    nb = T // (bsz * dil * ATT_BLK)
    TB = min(nb, ATT_TB)
    nsteps = nb // TB

    def body(q_ref, k_ref, v_ref, kp_ref, vp_ref, o_ref, l_ref):
        n = pl.program_id(2)
        mc, mp = _att_masks()
        for j in range(TB):
            rows = slice(j * ATT_BLK, (j + 1) * ATT_BLK)
            prow = slice((j - 1) * ATT_BLK, j * ATT_BLK)
            hp = (n * TB + j) > 0
            for hh in range(ATT_HEADS):
                ls = slice(hh * HEAD_DIM, (hh + 1) * HEAD_DIM)
                qj, kc, vc = q_ref[rows, ls], k_ref[rows, ls], v_ref[rows, ls]
                kp = k_ref[prow, ls] if j > 0 else kp_ref[:, ls]
                vp = v_ref[prow, ls] if j > 0 else vp_ref[:, ls]
                sc = jnp.where(mc, _dot_nt(qj, kc) * ATT_SCALE, NEG)
                sp = jnp.where(mp & hp, _dot_nt(qj, kp) * ATT_SCALE, NEG)
                m = jnp.maximum(jnp.max(sc, axis=1, keepdims=True), jnp.max(sp, axis=1, keepdims=True))
                pc, pp = jnp.exp(sc - m), jnp.exp(sp - m)
                l = jnp.sum(pc, axis=1, keepdims=True) + jnp.sum(pp, axis=1, keepdims=True)
                o_ref[rows, ls] = (_dot(pc.astype(BF16), vc) + _dot(pp.astype(BF16), vp)) / l
                l_ref[rows, ls] = jnp.broadcast_to(m + jnp.log(l), (ATT_BLK, HEAD_DIM))

    blk = pl.BlockSpec((TB * ATT_BLK, Wd), lambda b, r, n: (b * nsteps + n, r))
    prev = pl.BlockSpec((ATT_BLK, Wd), lambda b, r, n: (jnp.maximum(b * nb + n * TB - 1, 0), r))
    qv, kv, vv = (_att_view(t, dil) for t in (q, k, v))
    o, l = pl.pallas_call(
        body, grid=(bsz, dil, nsteps), in_specs=[blk, blk, blk, prev, prev], out_specs=[blk, blk],
        out_shape=[jax.ShapeDtypeStruct(qv.shape, F32), jax.ShapeDtypeStruct(qv.shape, F32)],
        name=name, compiler_params=_cparams(("arbitrary", "arbitrary", "arbitrary")),
    )(qv, kv, vv, kv, vv)
    return o.reshape(T, Wd), l.reshape(T, Wd)


def attn_bwd(name, q, k, v, do, mg, lse, dil, bsz):
    T, Wd = q.shape
    nb = T // (bsz * dil * ATT_BLK)
    TB = min(nb, ATT_TB)
    nsteps = nb // TB

    def body(q_ref, k_ref, v_ref, do_ref, mg_ref, l_ref, kp_ref, vp_ref, qn_ref, don_ref, mgn_ref, ln_ref,
             dq_ref, dk_ref, dv_ref):
        n = pl.program_id(2)
        mc, mp = _att_masks()

        def probs(qj, kk, lse_col, mask):
            s = _dot_nt(qj, kk) * ATT_SCALE
            return jnp.where(mask, jnp.exp(s - lse_col), 0.0)

        def ds_of(p, doj, vv, delta):
            return (p * (_dot_nt(doj, vv) - delta) * ATT_SCALE).astype(BF16)

        for hh in range(ATT_HEADS):
            ls = slice(hh * HEAD_DIM, (hh + 1) * HEAD_DIM)
            dk = [None] * TB
            dv = [None] * TB
            for j in range(TB + 1):
                rows = slice(j * ATT_BLK, (j + 1) * ATT_BLK)
                prow = slice((j - 1) * ATT_BLK, j * ATT_BLK)
                if j < TB:
                    qj, doj, mgj, lj = q_ref[rows, ls], do_ref[rows, ls], mg_ref[rows, ls], l_ref[rows, ls]
                    hp = (n * TB + j) > 0
                else:
                    qj, doj, mgj, lj = qn_ref[:, ls], don_ref[:, ls], mgn_ref[:, ls], ln_ref[:, ls]
                    hp = (n + 1) * TB < nb
                lse_col = lj[:, 0:1]
                delta = jnp.sum(doj.astype(F32) * mgj.astype(F32), axis=1, keepdims=True)
                if j > 0:
                    kp, vp = k_ref[prow, ls], v_ref[prow, ls]
                else:
                    kp, vp = kp_ref[:, ls], vp_ref[:, ls]
                pp = probs(qj, kp, lse_col, mp & hp)
                dsp = ds_of(pp, doj, vp, delta)
                if j > 0:
                    dk[j - 1] = dk[j - 1] + _dot_tn(dsp, qj)
                    dv[j - 1] = dv[j - 1] + _dot_tn(pp.astype(BF16), doj)
                if j < TB:
                    kc, vc = k_ref[rows, ls], v_ref[rows, ls]
                    pc = probs(qj, kc, lse_col, mc)
                    dsc = ds_of(pc, doj, vc, delta)
                    dq_ref[rows, ls] = (_dot(dsc, kc) + _dot(dsp, kp)).astype(dq_ref.dtype)
                    dk[j] = _dot_tn(dsc, qj)
                    dv[j] = _dot_tn(pc.astype(BF16), doj)
            for j in range(TB):
                rows = slice(j * ATT_BLK, (j + 1) * ATT_BLK)
                dk_ref[rows, ls] = dk[j].astype(dk_ref.dtype)
                dv_ref[rows, ls] = dv[j].astype(dv_ref.dtype)

    blk = pl.BlockSpec((TB * ATT_BLK, Wd), lambda b, r, n: (b * nsteps + n, r))
    prev = pl.BlockSpec((ATT_BLK, Wd), lambda b, r, n: (jnp.maximum(b * nb + n * TB - 1, 0), r))
    nxt = pl.BlockSpec((ATT_BLK, Wd), lambda b, r, n: (b * nb + jnp.minimum((n + 1) * TB, nb - 1), r))
    qv, kv, vv, dov, mgv, lv = (_att_view(t, dil) for t in (q, k, v, do, mg, lse))
    res = pl.pallas_call(
        body, grid=(bsz, dil, nsteps), in_specs=[blk] * 6 + [prev, prev, nxt, nxt, nxt, nxt],
        out_specs=[blk, blk, blk], out_shape=[jax.ShapeDtypeStruct(qv.shape, BF16)] * 3,
        name=name, compiler_params=_cparams(("arbitrary", "arbitrary", "arbitrary")),
    )(qv, kv, vv, dov, mgv, lv, kv, vv, qv, dov, mgv, lv)
    return [t.reshape(T, Wd) for t in res]


QKV_SLOTS = 3 * len(ATT_CONFIGS) * ATT_HEADS
SLOTS_PER_DEV = QKV_SLOTS // N_DEV


def _head_slots(t3):
    return [t3[s // SLOTS_PER_DEV][:, (s % SLOTS_PER_DEV) * HEAD_DIM:(s % SLOTS_PER_DEV + 1) * HEAD_DIM]
            for s in range(QKV_SLOTS)]


def _heads_of(slots, k):
    return jnp.concatenate(slots[k * ATT_HEADS:(k + 1) * ATT_HEADS], axis=1)


def _slots_to_blocked(slots):
    return jnp.stack([jnp.concatenate(slots[b * SLOTS_PER_DEV:(b + 1) * SLOTS_PER_DEV], axis=1)
                      for b in range(N_DEV)])


def _coords():
    return lax.axis_index("x"), lax.axis_index("y"), lax.axis_index("c")


def all_gather(name, xs):
    n = len(xs)

    def body(*refs):
        start, forward, finish = _gather_phases(refs[:n], refs[n:2 * n], *refs[2 * n:])
        start()
        forward()
        finish()

    anyspec = pl.BlockSpec(memory_space=pl.ANY)
    return pl.pallas_call(
        body, out_shape=_gather_out_shapes(xs), in_specs=[anyspec] * n, out_specs=[anyspec] * n,
        scratch_shapes=_gather_sems(n), name=name,
    )(*xs)


def _gather_out_shapes(xs):
    return [jax.ShapeDtypeStruct((N_DEV,) + t.shape, t.dtype) for t in xs]


def _gather_sems(n):
    return [pltpu.SemaphoreType.DMA((7 * n,)), pltpu.SemaphoreType.DMA((7 * n,)), pltpu.SemaphoreType.DMA((n,))]


def _gather_phases(x_refs, out_refs, send_sems, recv_sems, local_sems):
    n = len(x_refs)

    def parts():
        x, y, c = _coords()
        return (x, y, c), (x, y, 1 - c), [(1 - x, y), (x, 1 - y), (1 - x, 1 - y)], c

    def slot(a, px, py, pc):
        return out_refs[a].at[4 * px + 2 * py + pc]

    def copy(a, k, block, to, src=None):
        return pltpu.make_async_remote_copy(
            src_ref=slot(a, *block) if src is None else src, dst_ref=slot(a, *block),
            send_sem=send_sems.at[7 * a + k], recv_sem=recv_sems.at[7 * a + k],
            device_id=to, device_id_type=MESH)

    def mine(a, me):
        return pltpu.make_async_copy(x_refs[a], slot(a, *me), local_sems.at[a])

    def first(a, me, sibling, chips, c):
        return ([copy(a, 0, me, sibling, src=x_refs[a])]
                + [copy(a, 1 + j, me, (*chip, c), src=x_refs[a]) for j, chip in enumerate(chips)])

    def start():
        me, sibling, chips, c = parts()
        for a in range(n):
            mine(a, me).start()
        for a in range(n):
            for cp in first(a, me, sibling, chips, c):
                cp.start()

    def forward():
        me, sibling, chips, c = parts()
        for j, chip in enumerate(chips):
            for a in range(n):
                copy(a, 1 + j, (*chip, c), me).wait_recv()
                copy(a, 4 + j, (*chip, c), sibling).start()

    def finish():
        me, sibling, chips, c = parts()
        for a in range(n):
            copy(a, 0, sibling, me).wait_recv()
            for j, chip in enumerate(chips):
                copy(a, 4 + j, (*chip, 1 - c), me).wait_recv()
        for a in range(n):
            for cp in first(a, me, sibling, chips, c):
                cp.wait_send()
            for j, chip in enumerate(chips):
                copy(a, 4 + j, (*chip, c), sibling).wait_send()
            mine(a, me).wait()

    return start, forward, finish


def exchange_sibling(name, gs):
    n = len(gs)

    def body(*refs):
        g_refs, out_refs = refs[:n], refs[n:2 * n]
        send_sems, recv_sems = refs[2 * n:]
        x, y, c = _coords()
        sibling = (x, y, 1 - c)
        cps = []
        for a in range(n):
            for q in range(4):
                cps.append(pltpu.make_async_remote_copy(
                    src_ref=g_refs[a].at[2 * q + (1 - c)], dst_ref=out_refs[a].at[q],
                    send_sem=send_sems.at[4 * a + q], recv_sem=recv_sems.at[4 * a + q],
                    device_id=sibling, device_id_type=MESH))
        for cp in cps:
            cp.start()
        for cp in cps:
            cp.wait_recv()
        for cp in cps:
            cp.wait_send()

    anyspec = pl.BlockSpec(memory_space=pl.ANY)
    return pl.pallas_call(
        body, out_shape=[jax.ShapeDtypeStruct((4,) + g.shape[1:], g.dtype) for g in gs],
        in_specs=[anyspec] * n, out_specs=[anyspec] * n,
        scratch_shapes=[pltpu.SemaphoreType.DMA((4 * n,)), pltpu.SemaphoreType.DMA((4 * n,))],
        name=name,
    )(*gs)


def exchange_chips(name, ps):
    n = len(ps)

    def body(*refs):
        start, finish = _chips_phases(refs[:n], refs[n:2 * n], *refs[2 * n:])
        start()
        finish()

    anyspec = pl.BlockSpec(memory_space=pl.ANY)
    return pl.pallas_call(
        body, out_shape=_chips_out_shapes(ps), in_specs=[anyspec] * n, out_specs=[anyspec] * n,
        scratch_shapes=_chips_sems(n), name=name,
    )(*ps)


def _chips_out_shapes(ps):
    return [jax.ShapeDtypeStruct((3,) + p.shape[1:], p.dtype) for p in ps]


def _chips_sems(n):
    return [pltpu.SemaphoreType.DMA((3 * n,)), pltpu.SemaphoreType.DMA((3 * n,))]


def _chips_phases(p_refs, out_refs, send_sems, recv_sems):
    n = len(p_refs)

    def copies():
        x, y, c = _coords()
        chips = [(1 - x, y), (x, 1 - y), (1 - x, 1 - y)]
        return [pltpu.make_async_remote_copy(
            src_ref=p_refs[a].at[2 * px + py], dst_ref=out_refs[a].at[k],
            send_sem=send_sems.at[3 * a + k], recv_sem=recv_sems.at[3 * a + k],
            device_id=(px, py, c), device_id_type=MESH)
            for a in range(n) for k, (px, py) in enumerate(chips)]

    def start():
        for cp in copies():
            cp.start()

    def finish():
        cps = copies()
        for cp in cps:
            cp.wait_recv()
        for cp in cps:
            cp.wait_send()

    return start, finish


def _row_tile(R):
    tr = 256
    while R % tr:
        tr //= 2
    assert tr % 8 == 0
    return tr


def add_sibling(name, g, recv, c_idx):
    _, R, C = g.shape
    tr = _row_tile(R)

    def body(c_ref, g_ref, r_ref, o_ref, o16_ref):
        s = g_ref[...] + r_ref[...].astype(F32)
        o_ref[...] = s
        o16_ref[...] = s.astype(BF16)

    out = pl.BlockSpec((None, tr, C), lambda q, i, cr: (q, i, 0))
    return pl.pallas_call(
        body,
        grid_spec=pltpu.PrefetchScalarGridSpec(
            num_scalar_prefetch=1, grid=(4, R // tr),
            in_specs=[pl.BlockSpec((None, tr, C), lambda q, i, cr: (2 * q + cr[0], i, 0)), out],
            out_specs=[out, out]),
        out_shape=[jax.ShapeDtypeStruct((4, R, C), F32), jax.ShapeDtypeStruct((4, R, C), BF16)], name=name,
        compiler_params=_cparams(("arbitrary", "arbitrary")),
    )(c_idx, g, recv)


def _adam_math(w, g, m, v):
    m = ADAM_B1 * m + (1.0 - ADAM_B1) * g
    v = ADAM_B2 * v + (1.0 - ADAM_B2) * jnp.square(g)
    m_hat = m / (1.0 - ADAM_B1 ** ADAM_STEP)
    v_hat = v / (1.0 - ADAM_B2 ** ADAM_STEP)
    delta = -ADAM_LR * (m_hat / (jnp.sqrt(v_hat) + ADAM_EPS) + ADAM_WD * w)
    return delta, m, v


def adam_big(name, p1, recv, w, m, v, chip_idx, layer=0):
    _, R, C = p1.shape
    tr = _row_tile(R)
    nt = R // tr

    def body(q_ref, p_ref, r_ref, w_ref, m_ref, v_ref, g_ref, d_ref, nm_ref, nv_ref):
        g = ((p_ref[...] + r_ref[0].astype(F32)) + r_ref[1].astype(F32)) + r_ref[2].astype(F32)
        d, nm, nv = _adam_math(w_ref[...], g, m_ref[...], v_ref[...])
        g_ref[...] = g
        d_ref[...] = d
        nm_ref[...] = nm
        nv_ref[...] = nv

    row_in = pl.BlockSpec((tr, C), lambda i, qr: (layer * nt + i, 0))
    row = pl.BlockSpec((tr, C), lambda i, qr: (i, 0))
    return pl.pallas_call(
        body,
        grid_spec=pltpu.PrefetchScalarGridSpec(
            num_scalar_prefetch=1, grid=(nt,),
            in_specs=[pl.BlockSpec((None, tr, C), lambda i, qr: (qr[0], i, 0)),
                      pl.BlockSpec((3, tr, C), lambda i, qr: (0, i, 0)), row_in, row_in, row_in],
            out_specs=[row, row, row, row]),
        out_shape=[jax.ShapeDtypeStruct((R, C), F32)] * 4, name=name,
        compiler_params=_cparams(("arbitrary",)),
    )(chip_idx, p1, recv, w, m, v)


def sum8(parts):
    _, R, C = parts.shape

    def body(p_ref, o_ref):
        acc = p_ref[0]
        for k in range(1, N_DEV):
            acc = acc + p_ref[k]
        o_ref[...] = acc

    tr = 128
    while R % tr:
        tr //= 2
    assert tr % 8 == 0
    return pl.pallas_call(
        body, grid=(R // tr,), in_specs=[pl.BlockSpec((N_DEV, tr, C), lambda i: (0, i, 0))],
        out_specs=pl.BlockSpec((tr, C), lambda i: (i, 0)), out_shape=jax.ShapeDtypeStruct((R, C), F32),
        name="sum8", compiler_params=_cparams(("arbitrary",)),
    )(parts)


def adam_small(w, g, m, v):
    def fn(wt, gt, mt, vt):
        return _adam_math(wt, gt, mt, vt)
    C = w.shape[1]
    return rowwise("adam_small", fn, [w, g, m, v], [], [(C, F32)] * 3, tr=128)


def _pack(arrs, rows_mult=8):
    flat = jnp.concatenate([a.reshape(-1) for a in arrs])
    n = flat.shape[0]
    per = PACK_C * rows_mult
    pad = (-n) % per
    if pad:
        flat = jnp.concatenate([flat, jnp.zeros((pad,), flat.dtype)])
    return flat.reshape(-1, PACK_C)


def _unpack(buf, shapes):
    flat = buf.reshape(-1)
    out, off = [], 0
    for s in shapes:
        n = math.prod(s)
        out.append(flat[off:off + n].reshape(s))
        off += n
    return out


def _blocked(gfull, axis):
    shp = gfull.shape
    n = shp[axis] // N_DEV
    t = gfull.reshape(shp[:axis] + (N_DEV, n) + shp[axis + 1:])
    t = jnp.moveaxis(t, axis, 0)
    return t.reshape(N_DEV, -1)


def _unblocked(gathered, shard_shape, axis):
    t = jnp.moveaxis(gathered, 0, axis)
    shp = shard_shape[:axis] + (N_DEV * shard_shape[axis],) + shard_shape[axis + 1:]
    return t.reshape(shp)


def _relu2_epi(acc):
    r = jnp.maximum(acc, 0.0)
    return acc, r * r


def _step(x3, target3, W, comm):
    bsz, S, D = x3.shape
    T = bsz * S
    x = x3.reshape(T, D)
    target = target3.reshape(T, D)
    row = lambda v: v.reshape(1, -1)
    grads = {}

    s5p = (W['ssm_a_re'][0], W['ssm_a_im'][0], W['ssm_log_dt'][0], W['ssm_b_re'][0], W['ssm_b_im'][0])
    (lam_re, lam_im, bb_re, bb_im), s5_disc_vjp = jax.vjp(s5_disc, *s5p)
    pwr, pwi, l2r, l2i = s5_tables(lam_re, lam_im, S5_SUB)
    bre, bim = _s5_blockdiag_b(bb_re).astype(BF16), _s5_blockdiag_b(bb_im).astype(BF16)
    cre, cim = _s5_blockdiag_c(W['ssm_c_re'][0]).astype(BF16), _s5_blockdiag_c(W['ssm_c_im'][0]).astype(BF16)
    dskip = W['ssm_d']

    tril = jnp.tril(jnp.ones((GMLP_CHUNK, GMLP_CHUNK), bool))
    ws = jnp.where(tril[None], W['gmlp_w_s'][0], 0.0).astype(BF16)
    hw = D // GMLP_HEADS
    sbias = jnp.repeat(W['gmlp_b_s'][0].T, hw, axis=1)

    saved = []
    for i in range(DEPTH):
        sv = {'x': x}
        nm = W['norm_mix'][i:i + 1]
        if i == 0:
            h, hf = rms_fwd("rms_mix0", x, nm, want_f32=True)
            res = s5_fwd(hf, bre, bim, cre, cim, pwr, pwi, l2r, l2i, dskip, bsz, gather=comm.gather_list)
            ypre, gy, xs = res[:3]
            Wfull, Wsh = comm.weights(res[3:])
            W = {**W, **Wsh}
            conv_w = jnp.concatenate([W['conv_w_dw'][0], jnp.zeros((1, D), F32)], axis=0)
            z, = matmul("s5_glu_mm", gy, Wfull['ssm_w_glu'], mode='cb')
            x1, = rowwise("s5_glu", lambda zt, xt: xt + _glu(zt), [z, x], [], [(D, F32)])
            sv.update(hf=hf, ypre=ypre, gy=gy, xs=xs, z=z)
        elif i == 1:
            h, = rms_fwd("rms_mix1", x, nm)
            z, = matmul("conv_pw1", h, Wfull['conv_w_pw1'], mode='cb', epi=lambda acc, b: (acc + b,),
                        extras=[(W['conv_b_pw1'], 'row')])
            zg, = rowwise("conv_glu", _glu, [z], [], [(D, F32)])
            yc = dwconv_fwd(zg, conv_w, W['conv_b_dw'], bsz)
            y2, = rowwise("conv_ln_silu", lambda t, g, b: jax.nn.silu(_ln(t, g, b)), [yc],
                          [W['conv_ln_g'], W['conv_ln_b']], [(D, BF16)])
            x1, = matmul("conv_pw2", y2, Wfull['conv_w_pw2'], epi=lambda acc, b, r: (acc + b + r,),
                         extras=[(W['conv_b_pw2'], 'row'), (x, 'tile')])
            sv.update(h=h, z=z, zg=zg, yc=yc, y2=y2)
        elif i == 2:
            h, = rms_fwd("rms_mix2", x, nm)
            zp, = matmul("gmlp_in", h, Wfull['gmlp_w_in'], mode='cb')

            def gm_pre(zt, g, b):
                a = jax.nn.gelu(zt)
                return a[:, :D], _ln(a[:, D:], g, b)
            u, vln = rowwise("gmlp_pre", gm_pre, [zp], [W['gmlp_ln_g'], W['gmlp_ln_b']], [(D, F32), (D, BF16)])
            gated = spatial_fwd(u, vln, ws, sbias)
            x1, = matmul("gmlp_out", gated, Wfull['gmlp_w_out'], epi=lambda acc, r: (acc + r,), extras=[(x, 'tile')])
            sv.update(h=h, zp=zp, u=u, vln=vln, gated=gated)
        else:
            h, = rms_fwd("rms_mix3", x, nm)
            qkv3, = matmul("attn_qkv", h, Wfull['attn_w_qkv'], mode='cb', out_dtypes=(BF16,), out3=True)
            slots = _head_slots(qkv3)
            ng = len(ATT_CONFIGS)
            outs, lses, blocks = [], [], []
            for gi, (window, dil) in enumerate(ATT_CONFIGS):
                qb, kb, vb = (_heads_of(slots, j * ng + gi) for j in range(3))
                ob, lb = attn_fwd("attn_fwd%d" % gi, qb, kb, vb, dil, bsz)
                blocks.append((qb, kb, vb, lb, dil))
                outs.append(ob)
                lses.append(lb)

            def merge(o0, o1, o2, l0, l1, l2):
                m = jnp.maximum(jnp.maximum(l0, l1), l2)
                e0, e1, e2 = jnp.exp(l0 - m), jnp.exp(l1 - m), jnp.exp(l2 - m)
                inv = 1.0 / (e0 + e1 + e2)
                w0, w1, w2 = e0 * inv, e1 * inv, e2 * inv
                return w0 * o0 + w1 * o1 + w2 * o2, w0, w1, w2
            merged, w0, w1, w2 = rowwise("attn_merge", merge, outs + lses, [],
                                         [(ATT_W, BF16), (ATT_W, F32), (ATT_W, F32), (ATT_W, F32)])
            x1, = matmul("attn_o", merged, Wfull['attn_w_o'], mode='cb', epi=lambda acc, r: (acc + r,),
                         extras=[(x, 'tile')])
            sv.update(h=h, blocks=blocks, merged=merged, wts=(w0, w1, w2))
        h2, = rms_fwd("rms_mlp%d" % i, x1, W['norm_mlp'][i:i + 1])
        a, act = matmul("mlp_in%d" % i, h2, Wfull['mlp_w_in'][i], mode='cb', epi=_relu2_epi, out_dtypes=(BF16, BF16))
        x2, = matmul("mlp_out%d" % i, act, Wfull['mlp_w_out'][i], epi=lambda acc, r: (acc + r,), extras=[(x1, 'tile')])
        sv.update(x1=x1, h2=h2, a=a, act=act)
        saved.append(sv)
        x = x2

    def loss_fn(xt, tt, g):
        y, vjp = jax.vjp(_rms, xt, g)
        err = y - tt
        dxx, dg = vjp(err * (1.0 / D))
        lval = jnp.sum(jnp.sum(err * err, axis=1, keepdims=True), axis=0, keepdims=True) * (0.5 / D)
        return dxx, dxx, jnp.broadcast_to(lval, (1, 128)), dg
    dx, dxb, lacc, dnf = rowwise("loss_head", loss_fn, [x, target], [row(W['norm_final'])],
                                 [(D, F32), (D, BF16)], [((1, 128), F32), ((1, D), F32)])
    loss_local = lacc[0, 0]
    grads['norm_final'] = dnf.reshape(-1)

    g_norm_mix, g_norm_mlp = [None] * DEPTH, [None] * DEPTH
    g_mlp_in, g_mlp_out = [None] * DEPTH, [None] * DEPTH
    for i in reversed(range(DEPTH)):
        sv = saved[i]
        da, = matmul("mlp_out_bwd%d" % i, dxb, Wfull['mlp_w_out'][i], mode='nt',
                     epi=lambda acc, av: (acc * (2.0 * jnp.maximum(av.astype(F32), 0.0)),),
                     extras=[(sv['a'], 'tile')], out_dtypes=(BF16,))
        g_mlp_out[i] = _rows_blocked(wgrad("mlp_out_wg%d" % i, sv['act'], dxb))
        dh2, = matmul_nt_cb("mlp_in_bwd%d" % i, da, Wfull['mlp_w_in'][i])
        g_mlp_in[i] = wgrad("mlp_in_wg%d" % i, sv['h2'], da, cb=True)
        dx, dxb, dg = rms_bwd("rms_mlp_bwd%d" % i, sv['x1'], dh2, dx, W['norm_mlp'][i:i + 1])
        g_norm_mlp[i] = dg.reshape(-1)
        xin = sv['x']
        if i == 0:
            dz, = rowwise("s5_glu_bwd", _glu_bwd, [sv['z'], dx], [], [(2 * D, BF16)])
            dgy, = matmul_nt_cb("s5_glu_mm_bwd", dz, Wfull['ssm_w_glu'])
            grads['ssm_w_glu'] = wgrad("s5_glu_wg", sv['gy'], dz, cb=True)

            def gelu_bwd(yt, dt):
                _, vjp = jax.vjp(jax.nn.gelu, yt)
                return vjp(dt)[0]
            dypre, = rowwise("s5_gelu_bwd", gelu_bwd, [sv['ypre'], dgy], [], [(D, F32)])
            grads['mlp_w_in'], grads['mlp_w_out'] = g_mlp_in, g_mlp_out
            res = s5_bwd(sv['hf'], dypre, sv['xs'], bre, bim, cre, cim, pwr, pwi,
                         pwr[:, ::-1], pwi[:, ::-1], l2r, l2i, dskip, bsz, chips=comm.rs_front(grads))
            du, dbr, dbi, dcr, dci, dl, dd = res[:7]
            comm.recv2 = res[7:]
            dlam_re = dl[:, 0, :].reshape(SSM_GROUPS, SSM_STATE)
            dlam_im = dl[:, 1, :].reshape(SSM_GROUPS, SSM_STATE)
            s5_cot = (dlam_re, dlam_im, _s5_blockdiag_b_inv(dbr), _s5_blockdiag_b_inv(dbi))
            grads['ssm_c_re'] = _s5_blockdiag_c_inv(dcr)[None]
            grads['ssm_c_im'] = _s5_blockdiag_c_inv(dci)[None]
            grads['ssm_d'] = dd[0:1]
            dh = du
        elif i == 1:
            dy2, = matmul("conv_pw2_bwd", dxb, Wfull['conv_w_pw2'], mode='nt')
            grads['conv_w_pw2'] = _rows_blocked(wgrad("conv_pw2_wg", sv['y2'], dxb))

            def ln_silu_bwd(yt, dt, dxt, g, b):
                _, vjp = jax.vjp(lambda t, gg, bb: jax.nn.silu(_ln(t, gg, bb)), yt, g, b)
                dyc, dgg, dbb = vjp(dt)
                return dyc, dgg, dbb, _colsum(dxt)
            dyc, dlg, dlb, dbp2 = rowwise("conv_ln_silu_bwd", ln_silu_bwd, [sv['yc'], dy2, dx],
                                          [W['conv_ln_g'], W['conv_ln_b']], [(D, F32)],
                                          [((1, D), F32), ((1, D), F32), ((1, D), F32)])
            grads['conv_ln_g'], grads['conv_ln_b'], grads['conv_b_pw2'] = dlg, dlb, dbp2
            dzg, dwd, dbd = dwconv_bwd(sv['zg'], dyc, conv_w, bsz)
            grads['conv_w_dw'] = dwd[None, :CONV_WIDTH]
            grads['conv_b_dw'] = dbd[0:1]

            def glu_bwd1(zt, dyt):
                dzt = _glu_bwd(zt, dyt)
                return dzt, _colsum(dzt)
            dz, dbp1 = rowwise("conv_glu_bwd", glu_bwd1, [sv['z'], dzg], [], [(2 * D, BF16)], [((1, 2 * D), F32)])
            grads['conv_b_pw1'] = dbp1
            dh, = matmul_nt_cb("conv_pw1_bwd", dz, Wfull['conv_w_pw1'])
            grads['conv_w_pw1'] = wgrad("conv_pw1_wg", sv['h'], dz, cb=True)
        elif i == 2:
            dgt, = matmul("gmlp_out_bwd", dxb, Wfull['gmlp_w_out'], mode='nt', out_dtypes=(BF16,))
            grads['gmlp_w_out'] = _rows_blocked(wgrad("gmlp_out_wg", sv['gated'], dxb))
            du, dvln, dws, dsb = spatial_bwd(sv['u'], sv['vln'], dgt, ws, sbias)
            grads['gmlp_w_s'] = dws[None]
            grads['gmlp_b_s'] = dsb.reshape(GMLP_CHUNK, GMLP_HEADS, hw).sum(-1).T[None]

            def gm_pre_bwd(zt, dut, dvt, g, b):
                _, vjp_u = jax.vjp(jax.nn.gelu, zt[:, :D])
                _, vjp_v = jax.vjp(lambda zz, gg, bb: _ln(jax.nn.gelu(zz), gg, bb), zt[:, D:], g, b)
                dz2, dgg, dbb = vjp_v(dvt)
                return jnp.concatenate([vjp_u(dut)[0], dz2], axis=1), dgg, dbb
            dzp, dlg, dlb = rowwise("gmlp_pre_bwd", gm_pre_bwd, [sv['zp'], du, dvln],
                                    [W['gmlp_ln_g'], W['gmlp_ln_b']], [(2 * D, BF16)], [((1, D), F32), ((1, D), F32)])
            grads['gmlp_ln_g'], grads['gmlp_ln_b'] = dlg, dlb
            dh, = matmul_nt_cb("gmlp_in_bwd", dzp, Wfull['gmlp_w_in'])
            grads['gmlp_w_in'] = wgrad("gmlp_in_wg", sv['h'], dzp, cb=True)
        else:
            dm, = matmul_nt_cb("attn_o_bwd", dxb, Wfull['attn_w_o'])
            grads['attn_w_o'] = wgrad("attn_o_wg", sv['merged'], dxb, cb=True)
            w0, w1, w2 = sv['wts']
            do0, do1, do2 = rowwise("attn_merge_bwd", lambda d, a, b, c: (a * d, b * d, c * d), [dm, w0, w1, w2], [],
                                    [(ATT_W, BF16)] * 3)
            dparts = [[None] * 3 for _ in range(3)]
            for gi, (dog, (qb, kb, vb, lb, dil)) in enumerate(zip((do0, do1, do2), sv['blocks'])):
                dqb, dkb, dvb = attn_bwd("attn_bwd%d" % gi, qb, kb, vb, dog, sv['merged'], lb, dil, bsz)
                for j, t in enumerate((dqb, dkb, dvb)):
                    dparts[j][gi] = t
            dslots = [dparts[j][gi][:, hh * HEAD_DIM:(hh + 1) * HEAD_DIM]
                      for j in range(3) for gi in range(3) for hh in range(ATT_HEADS)]
            dqkv3 = _slots_to_blocked(dslots)
            dh, = matmul_nt_cb("attn_qkv_bwd", dqkv3, Wfull['attn_w_qkv'], a3=True)
            grads['attn_w_qkv'] = wgrad("attn_qkv_wg", sv['h'], dqkv3, cb=True, g3=True)
        dx, dxb, dg = rms_bwd("rms_mix_bwd%d" % i, xin, dh, dx, W['norm_mix'][i:i + 1])
        g_norm_mix[i] = dg.reshape(-1)

    grads['norm_mix'] = jnp.stack(g_norm_mix)
    grads['norm_mlp'] = jnp.stack(g_norm_mlp)
    grads['mlp_w_in'] = g_mlp_in
    grads['mlp_w_out'] = g_mlp_out
    return loss_local, dx.reshape(bsz, S, D), grads, (s5_disc_vjp, s5_cot)


class _StepComm:
    def __init__(self, Wl, c_idx):
        self.Wl, self.c_idx = Wl, c_idx
        self.units = []
        for n in BIG:
            self.units += [(n, i) for i in range(DEPTH)] if Wl[n].shape[0] == DEPTH else [(n, None)]
        self.ss_names = list(SMALL_SHARDED)
        spack = _pack([Wl[n] for n in self.ss_names])
        self.gather_list = [Wl[n][0 if i is None else i].astype(BF16) for n, i in self.units] + [spack]
        self.p1 = self.recv2 = None

    @staticmethod
    def tag(n, i):
        return n if i is None else "%s%d" % (n, i)

    def weights(self, gathered):
        Wl = self.Wl
        Wfull = {}
        for (n, i), g in zip(self.units, gathered):
            w = g if BIG[n] == 2 else g.reshape(N_DEV * g.shape[1], g.shape[2])
            if i is None:
                Wfull[n] = w
            else:
                Wfull.setdefault(n, []).append(w)
        sparts = _unpack_gathered(gathered[-1], [Wl[n].shape for n in self.ss_names])
        Wsh = {n: _unblocked(p, Wl[n].shape, SMALL_SHARDED[n]) for n, p in zip(self.ss_names, sparts)}
        return Wfull, Wsh

    def rs_front(self, grads):
        pairs = [grads[n] if i is None else grads[n][i] for n, i in self.units]
        recv1 = exchange_sibling("rs_sibling", [p[1] for p in pairs])
        self.p1 = [add_sibling("add_sibling_" + self.tag(n, i), p[0], r, self.c_idx)
                   for (n, i), p, r in zip(self.units, pairs, recv1)]
        return [p[1] for p in self.p1]


def _rows_blocked(pair):
    return tuple(t.reshape(N_DEV, t.shape[0] // N_DEV, t.shape[1]) for t in pair)


def kernel(x, norm_mix, norm_mlp, norm_final, ssm_a_re, ssm_a_im, ssm_b_re, ssm_b_im, ssm_c_re, ssm_c_im, ssm_d, ssm_log_dt, ssm_w_glu, conv_w_pw1, conv_b_pw1, conv_w_dw, conv_b_dw, conv_ln_g, conv_ln_b, conv_w_pw2, conv_b_pw2, gmlp_w_in, gmlp_ln_g, gmlp_ln_b, gmlp_w_s, gmlp_b_s, gmlp_w_out, attn_w_qkv, attn_w_o, mlp_w_in, mlp_w_out, loss_target, m_norm_mix, m_norm_mlp, m_norm_final, m_ssm_a_re, m_ssm_a_im, m_ssm_b_re, m_ssm_b_im, m_ssm_c_re, m_ssm_c_im, m_ssm_d, m_ssm_log_dt, m_ssm_w_glu, m_conv_w_pw1, m_conv_b_pw1, m_conv_w_dw, m_conv_b_dw, m_conv_ln_g, m_conv_ln_b, m_conv_w_pw2, m_conv_b_pw2, m_gmlp_w_in, m_gmlp_ln_g, m_gmlp_ln_b, m_gmlp_w_s, m_gmlp_b_s, m_gmlp_w_out, m_attn_w_qkv, m_attn_w_o, m_mlp_w_in, m_mlp_w_out, v_norm_mix, v_norm_mlp, v_norm_final, v_ssm_a_re, v_ssm_a_im, v_ssm_b_re, v_ssm_b_im, v_ssm_c_re, v_ssm_c_im, v_ssm_d, v_ssm_log_dt, v_ssm_w_glu, v_conv_w_pw1, v_conv_b_pw1, v_conv_w_dw, v_conv_b_dw, v_conv_ln_g, v_conv_ln_b, v_conv_w_pw2, v_conv_b_pw2, v_gmlp_w_in, v_gmlp_ln_g, v_gmlp_ln_b, v_gmlp_w_s, v_gmlp_b_s, v_gmlp_w_out, v_attn_w_qkv, v_attn_w_o, v_mlp_w_in, v_mlp_w_out):
    args = (norm_mix, norm_mlp, norm_final, ssm_a_re, ssm_a_im, ssm_b_re, ssm_b_im, ssm_c_re, ssm_c_im, ssm_d,
            ssm_log_dt, ssm_w_glu, conv_w_pw1, conv_b_pw1, conv_w_dw, conv_b_dw, conv_ln_g, conv_ln_b, conv_w_pw2,
            conv_b_pw2, gmlp_w_in, gmlp_ln_g, gmlp_ln_b, gmlp_w_s, gmlp_b_s, gmlp_w_out, attn_w_qkv, attn_w_o,
            mlp_w_in, mlp_w_out)
    margs = (m_norm_mix, m_norm_mlp, m_norm_final, m_ssm_a_re, m_ssm_a_im, m_ssm_b_re, m_ssm_b_im, m_ssm_c_re,
             m_ssm_c_im, m_ssm_d, m_ssm_log_dt, m_ssm_w_glu, m_conv_w_pw1, m_conv_b_pw1, m_conv_w_dw, m_conv_b_dw,
             m_conv_ln_g, m_conv_ln_b, m_conv_w_pw2, m_conv_b_pw2, m_gmlp_w_in, m_gmlp_ln_g, m_gmlp_ln_b,
             m_gmlp_w_s, m_gmlp_b_s, m_gmlp_w_out, m_attn_w_qkv, m_attn_w_o, m_mlp_w_in, m_mlp_w_out)
    vargs = (v_norm_mix, v_norm_mlp, v_norm_final, v_ssm_a_re, v_ssm_a_im, v_ssm_b_re, v_ssm_b_im, v_ssm_c_re,
             v_ssm_c_im, v_ssm_d, v_ssm_log_dt, v_ssm_w_glu, v_conv_w_pw1, v_conv_b_pw1, v_conv_w_dw, v_conv_b_dw,
             v_conv_ln_g, v_conv_ln_b, v_conv_w_pw2, v_conv_b_pw2, v_gmlp_w_in, v_gmlp_ln_g, v_gmlp_ln_b,
             v_gmlp_w_s, v_gmlp_b_s, v_gmlp_w_out, v_attn_w_qkv, v_attn_w_o, v_mlp_w_in, v_mlp_w_out)
    Wl = dict(zip(WEIGHT_NAMES, args))
    Ml = dict(zip(WEIGHT_NAMES, margs))
    Vl = dict(zip(WEIGHT_NAMES, vargs))
    cx, cy, cc = _coords()
    my_idx = 4 * cx + 2 * cy + cc

    c_idx = cc.reshape(1).astype(jnp.int32)
    chip_idx = (2 * cx + cy).reshape(1).astype(jnp.int32)
    comm = _StepComm(Wl, c_idx)
    units, tag = comm.units, comm.tag
    W = {n: Wl[n] for n in SMALL if n not in SMALL_SHARDED}
    loss_local, grad_x, grads, (s5_disc_vjp, s5_cot) = _step(x, loss_target, W, comm)
    loss = lax.psum(loss_local, MESH_AXES)

    outs4 = {}
    for (n, i), p, r in zip(units, comm.p1, comm.recv2):
        w2, m2, v2 = (d[n].reshape(-1, d[n].shape[-1]) for d in (Wl, Ml, Vl))
        res = adam_big("adam_" + tag(n, i), p[0], r, w2, m2, v2, chip_idx, layer=0 if i is None else i)
        if i is None:
            outs4[n] = [t.reshape(Wl[n].shape) for t in res]
        else:
            outs4.setdefault(n, []).append(res)
    for n in BIG:
        if Wl[n].shape[0] == DEPTH:
            outs4[n] = [jnp.stack([layer[k] for layer in outs4[n]]) for k in range(4)]
    out_g = {n: outs4[n][0] for n in BIG}
    out_d = {n: outs4[n][1] for n in BIG}
    out_m = {n: outs4[n][2] for n in BIG}
    out_v = {n: outs4[n][3] for n in BIG}

    s5_lin = ['ssm_a_re', 'ssm_a_im', 'ssm_log_dt', 'ssm_b_re', 'ssm_b_im']
    direct = [n for n in SMALL if n not in s5_lin]
    def full_shape(n):
        shp = list(Wl[n].shape)
        if n in SMALL_SHARDED:
            shp[SMALL_SHARDED[n]] *= N_DEV
        return tuple(shp)
    small_parts = [grads[n].reshape(full_shape(n)) for n in direct] + list(s5_cot)
    gsum = sum8(all_gather("gather_small_grads", [_pack(small_parts)])[0])
    summed = _unpack(gsum, [p.shape for p in small_parts])
    gsmall = dict(zip(direct, summed[:len(direct)]))
    s5g = s5_disc_vjp(tuple(summed[len(direct):]))
    for n, gval in zip(s5_lin, s5g):
        gsmall[n] = gval[None]
    for n, ax in SMALL_SHARDED.items():
        gsmall[n] = lax.dynamic_slice_in_dim(gsmall[n], my_idx * Wl[n].shape[ax], Wl[n].shape[ax], axis=ax)
    sm_shapes = [Wl[n].shape for n in SMALL]
    dS, mS, vS = adam_small(_pack([Wl[n] for n in SMALL]), _pack([gsmall[n] for n in SMALL]),
                            _pack([Ml[n] for n in SMALL]), _pack([Vl[n] for n in SMALL]))
    for n, gval in zip(SMALL, [gsmall[n] for n in SMALL]):
        out_g[n] = gval.reshape(Wl[n].shape)
    out_d.update(zip(SMALL, _unpack(dS, sm_shapes)))
    out_m.update(zip(SMALL, _unpack(mS, sm_shapes)))
    out_v.update(zip(SMALL, _unpack(vS, sm_shapes)))

    return (loss, grad_x, *[out_g[n] for n in WEIGHT_NAMES], *[out_d[n] for n in WEIGHT_NAMES],
            *[out_m[n] for n in WEIGHT_NAMES], *[out_v[n] for n in WEIGHT_NAMES])


def _unpack_gathered(g, shard_shapes):
    flat = g.reshape(N_DEV, -1)
    out, off = [], 0
    for s in shard_shapes:
        n = math.prod(s)
        out.append(flat[:, off:off + n].reshape((N_DEV,) + tuple(s)))
        off += n
    return out
```

```python
import functools
import math

import jax
import jax.numpy as jnp
from jax import lax
from jax.experimental import pallas as pl
from jax.experimental.pallas import tpu as pltpu

F32 = jnp.float32
BF16 = jnp.bfloat16

D_MODEL = 1024
DEPTH = 4
EPS = 1e-6
SSM_GROUP = 16
SSM_GROUPS = 64
SSM_STATE = 64
S5_GB = 8
S5_NGB = SSM_GROUPS // S5_GB
S5_CH = S5_GB * SSM_GROUP
S5_ST = S5_GB * SSM_STATE
S5_L = 128
CONV_WIDTH = 31
CONV_PAD = 32
CONV_TS = 256
CONV_CW = 256
GMLP_CHUNK = 128
GMLP_HEADS = 4
ATT_CONFIGS = ((128, 1), (512, 4), (2048, 16))
ATT_HEADS = 8
HEAD_DIM = 64
ATT_BLK = 128
ATT_TB = 2
ATT_W = ATT_HEADS * HEAD_DIM
N_DEV = 8
ADAM_LR = 0.001
ADAM_B1 = 0.9
ADAM_B2 = 0.999
ADAM_EPS = 1e-08
ADAM_WD = 0.01
ADAM_STEP = 10
VMEM_LIMIT = 56 * 1024 * 1024
PACK_C = 1024
MESH_AXES = ("x", "y", "c")
MESH = pl.DeviceIdType.MESH

WEIGHT_NAMES = ['norm_mix', 'norm_mlp', 'norm_final', 'ssm_a_re', 'ssm_a_im', 'ssm_b_re', 'ssm_b_im',
                'ssm_c_re', 'ssm_c_im', 'ssm_d', 'ssm_log_dt', 'ssm_w_glu', 'conv_w_pw1', 'conv_b_pw1',
                'conv_w_dw', 'conv_b_dw', 'conv_ln_g', 'conv_ln_b', 'conv_w_pw2', 'conv_b_pw2',
                'gmlp_w_in', 'gmlp_ln_g', 'gmlp_ln_b', 'gmlp_w_s', 'gmlp_b_s', 'gmlp_w_out',
                'attn_w_qkv', 'attn_w_o', 'mlp_w_in', 'mlp_w_out']
BIG = {'ssm_w_glu': 2, 'conv_w_pw1': 2, 'conv_w_pw2': 1, 'gmlp_w_in': 2, 'gmlp_w_out': 1,
       'attn_w_qkv': 2, 'attn_w_o': 2, 'mlp_w_in': 2, 'mlp_w_out': 1}
SMALL_SHARDED = {'conv_b_pw1': 1, 'conv_w_dw': 2, 'conv_b_dw': 1, 'conv_ln_g': 1, 'conv_ln_b': 1,
                 'conv_b_pw2': 1, 'gmlp_ln_g': 1, 'gmlp_ln_b': 1}
SMALL = [n for n in WEIGHT_NAMES if n not in BIG]


def _cparams(sem=None):
    return pltpu.CompilerParams(dimension_semantics=sem, vmem_limit_bytes=VMEM_LIMIT)


def _dot(a, b):
    return jnp.dot(a, b, preferred_element_type=F32)


def _dot_nt(a, b):
    return lax.dot_general(a, b, (((1,), (1,)), ((), ())), preferred_element_type=F32)


def _dot_tn(a, b):
    return lax.dot_general(a, b, (((0,), (0,)), ((), ())), preferred_element_type=F32)


ROW_TILE_BYTES = 10 << 20


def _rows_for(T, row_bytes, cap=1024):
    tr = min(cap, T)
    while tr > 8 and (T % tr or tr * row_bytes > ROW_TILE_BYTES):
        tr //= 2
    assert T % tr == 0 and tr % 8 == 0
    return tr


def rowwise(name, fn, rows, params, row_out, acc_out=(), tr=None):
    T = rows[0].shape[0]
    row_bytes = (sum(r.shape[1] * r.dtype.itemsize for r in rows)
                 + sum(c * jnp.dtype(dt).itemsize for c, dt in row_out))
    tr = _rows_for(T, row_bytes, cap=tr or 1024)
    nr, npar, nro = len(rows), len(params), len(row_out)

    def body(*refs):
        ins = [r[...] for r in refs[:nr + npar]]
        outs = refs[nr + npar:]
        res = fn(*ins)
        if not isinstance(res, (tuple, list)):
            res = (res,)
        for k in range(nro):
            outs[k][...] = res[k].astype(outs[k].dtype)
        if acc_out:
            @pl.when(pl.program_id(0) == 0)
            def _():
                for k in range(nro, len(outs)):
                    outs[k][...] = jnp.zeros_like(outs[k])
            for k in range(nro, len(outs)):
                outs[k][...] += res[k].astype(outs[k].dtype)

    in_specs = [pl.BlockSpec((tr, r.shape[1]), lambda i: (i, 0)) for r in rows]
    in_specs += [pl.BlockSpec(p.shape, lambda i, nd=p.ndim: (0,) * nd) for p in params]
    out_shape = [jax.ShapeDtypeStruct((T, c), dt) for c, dt in row_out]
    out_specs = [pl.BlockSpec((tr, c), lambda i: (i, 0)) for c, dt in row_out]
    out_shape += [jax.ShapeDtypeStruct(s, dt) for s, dt in acc_out]
    out_specs += [pl.BlockSpec(s, lambda i, nd=len(s): (0,) * nd) for s, dt in acc_out]
    res = pl.pallas_call(body, grid=(T // tr,), in_specs=in_specs, out_specs=out_specs, out_shape=out_shape,
                         name=name, compiler_params=_cparams(("arbitrary",)))(*rows, *params)
    return res


def _tile_m(M, K):
    tm = 2048
    while tm > 256 and tm * K * 2 > (4 << 20):
        tm //= 2
    return min(tm, M)


def matmul(name, a, b, *, mode='nn', epi=None, extras=(), out_dtypes=(F32,), out3=False, whole_rows=False):
    M, K = a.shape
    if mode == 'cb':
        nblk, _, tn = b.shape
        N = nblk * tn
    else:
        N = b.shape[0] if mode == 'nt' else b.shape[1]
        tn = N if whole_rows else min(512, N)
    row_bytes = (K * 2 + sum(N * jnp.dtype(dt).itemsize for dt in out_dtypes)
                 + sum(N * arr.dtype.itemsize for arr, kind in extras if kind == 'tile'))
    tm = _rows_for(M, row_bytes)
    assert N % tn == 0, (M, N, tm, tn)
    nex = len(extras)

    def body(a_ref, b_ref, *rest):
        ex_refs, outs = rest[:nex], rest[nex:]
        av = a_ref[...]
        for c in range(N // tn):
            cs = slice(c * tn, (c + 1) * tn)
            if mode == 'cb':
                acc = _dot(av, b_ref[c])
            elif mode == 'nt':
                acc = _dot_nt(av, b_ref[cs, :])
            else:
                acc = _dot(av, b_ref[:, cs])
            res = epi(acc, *[e[:, cs] for e in ex_refs]) if epi is not None else (acc,)
            for o, r in zip(outs, res):
                if out3:
                    o[c] = r.astype(o.dtype)
                else:
                    o[:, cs] = r.astype(o.dtype)

    in_specs = [pl.BlockSpec((tm, K), lambda i: (i, 0)), pl.BlockSpec(b.shape, lambda i, nd=b.ndim: (0,) * nd)]
    for arr, kind in extras:
        in_specs.append(pl.BlockSpec((tm, N), lambda i: (i, 0)) if kind == 'tile'
                        else pl.BlockSpec((1, N), lambda i: (0, 0)))
    if out3:
        out_shape = [jax.ShapeDtypeStruct((N // tn, M, tn), dt) for dt in out_dtypes]
        out_specs = [pl.BlockSpec((N // tn, tm, tn), lambda i: (0, i, 0)) for dt in out_dtypes]
    else:
        out_shape = [jax.ShapeDtypeStruct((M, N), dt) for dt in out_dtypes]
        out_specs = [pl.BlockSpec((tm, N), lambda i: (i, 0)) for dt in out_dtypes]
    return pl.pallas_call(body, grid=(M // tm,), in_specs=in_specs, out_specs=out_specs,
                          out_shape=out_shape, name=name,
                          compiler_params=_cparams(("arbitrary",)))(a, b, *[e[0] for e in extras])


def matmul_nt_cb(name, a, b, *, a3=False, epi=None, extras=(), params=(), out_dtypes=(F32,), acc_out=()):
    nblk, K, n = b.shape
    M = a.shape[1] if a3 else a.shape[0]
    tm = _tile_m(M, nblk * n)
    assert M % tm == 0
    nex, npar, nro = len(extras), len(params), len(out_dtypes)

    def body(a_ref, b_ref, *rest):
        ex, outs = rest[:nex + npar], rest[nex + npar:]
        acc = None
        for j in range(nblk):
            aj = a_ref[j] if a3 else a_ref[:, j * n:(j + 1) * n]
            part = _dot_nt(aj, b_ref[j])
            acc = part if acc is None else acc + part
        res = epi(acc, *[e[...] for e in ex]) if epi is not None else (acc,)
        for o, r in zip(outs[:nro], res[:nro]):
            o[...] = r.astype(o.dtype)
        if acc_out:
            @pl.when(pl.program_id(0) == 0)
            def _():
                for o in outs[nro:]:
                    o[...] = jnp.zeros_like(o)
            for o, r in zip(outs[nro:], res[nro:]):
                o[...] += r.astype(o.dtype)

    a_spec = (pl.BlockSpec((nblk, tm, n), lambda i: (0, i, 0)) if a3
              else pl.BlockSpec((tm, nblk * n), lambda i: (i, 0)))
    row = pl.BlockSpec((tm, K), lambda i: (i, 0))
    const = lambda shp: pl.BlockSpec(shp, lambda i, nd=len(shp): (0,) * nd)
    return pl.pallas_call(
        body, grid=(M // tm,),
        in_specs=[a_spec, pl.BlockSpec((nblk, K, n), lambda i: (0, 0, 0))] + [row] * nex
        + [const(p.shape) for p in params],
        out_specs=[row] * nro + [const(s) for s, dt in acc_out],
        out_shape=[jax.ShapeDtypeStruct((M, K), dt) for dt in out_dtypes]
        + [jax.ShapeDtypeStruct(s, dt) for s, dt in acc_out],
        name=name, compiler_params=_cparams(("arbitrary",)))(a, b, *extras, *params)


def wgrad(name, a, g, *, cb=False, g3=False):
    M, K = a.shape
    tm, tk = min(M, 1024), min(K, 1024)
    if cb:
        n = g.shape[2] if g3 else g.shape[1] // N_DEV
        nj = N_DEV
        while nj > 1 and nj * tk * n * 6 > (14 << 20):
            nj //= 2
        grid = (K // tk, N_DEV // nj, M // tm)
        g_spec = (pl.BlockSpec((nj, tm, n), lambda k, j, m: (j, m, 0)) if g3
                  else pl.BlockSpec((tm, nj * n), lambda k, j, m: (m, j)))
        o_spec = pl.BlockSpec((nj, tk, n), lambda k, j, m: (j, k, 0))
        o_shape = (N_DEV, K, n)
    else:
        N = g.shape[1]
        tn = min(N, 1024)
        nj = 1
        grid = (K // tk, N // tn, M // tm)
        g_spec = pl.BlockSpec((tm, tn), lambda k, j, m: (m, j))
        o_spec = pl.BlockSpec((tk, tn), lambda k, j, m: (k, j))
        o_shape = (K, N)
    nm = M // tm

    def body(a_ref, g_ref, o_ref, o16_ref):
        m = pl.program_id(2)

        @pl.when(m == 0)
        def _():
            o_ref[...] = jnp.zeros_like(o_ref)
        at = a_ref[...].T
        if cb:
            for jj in range(nj):
                gj = g_ref[jj] if g3 else g_ref[:, jj * n:(jj + 1) * n]
                o_ref[jj] += _dot(at, gj)
        else:
            o_ref[...] += _dot(at, g_ref[...])

        @pl.when(m == nm - 1)
        def _():
            o16_ref[...] = o_ref[...].astype(BF16)

    return pl.pallas_call(
        body, grid=grid, in_specs=[pl.BlockSpec((tm, tk), lambda k, j, m: (m, k)), g_spec],
        out_specs=[o_spec, o_spec],
        out_shape=[jax.ShapeDtypeStruct(o_shape, F32), jax.ShapeDtypeStruct(o_shape, BF16)], name=name,
        compiler_params=_cparams(("arbitrary", "arbitrary", "arbitrary")))(a, g)


def _rms(x, g):
    x = x.astype(F32)
    return x * lax.rsqrt(jnp.mean(x * x, axis=-1, keepdims=True) + EPS) * g


def _ln(x, g, b):
    mu = jnp.mean(x, axis=-1, keepdims=True)
    var = jnp.mean(jnp.square(x - mu), axis=-1, keepdims=True)
    return (x - mu) * lax.rsqrt(var + EPS) * g + b


def _glu(z):
    d = z.shape[1] // 2
    return z[:, :d] * jax.nn.sigmoid(z[:, d:])


def _glu_bwd(z, dy):
    d = z.shape[1] // 2
    a, s = z[:, :d], jax.nn.sigmoid(z[:, d:])
    return jnp.concatenate([dy * s, dy * a * s * (1.0 - s)], axis=1)


def _colsum(v):
    return jnp.sum(v.astype(F32), axis=0, keepdims=True)


def rms_fwd(name, x, g, want_f32=False):
    def fn(xt, gt):
        h = _rms(xt, gt)
        return (h, h) if want_f32 else (h,)
    D = x.shape[1]
    outs = [(D, BF16)] + ([(D, F32)] if want_f32 else [])
    return rowwise(name, fn, [x], [g], outs)


def rms_bwd(name, x, dh, dres, g):
    def fn(xt, dht, drt, gt):
        _, vjp = jax.vjp(_rms, xt, gt)
        dx, dg = vjp(dht.astype(F32))
        dx = dx + drt
        return dx, dx, dg
    D = x.shape[1]
    return rowwise(name, fn, [x, dh, dres], [g], [(D, F32), (D, BF16)], [((1, D), F32)])


def s5_disc(a_re, a_im, log_dt, b_re, b_im):
    dt = jnp.exp(log_dt)[:, None]
    er = jnp.exp(a_re * dt)
    lam_re = er * jnp.cos(a_im * dt)
    lam_im = er * jnp.sin(a_im * dt)
    nr, ni = lam_re - 1.0, lam_im
    den = a_re * a_re + a_im * a_im
    f_re = (nr * a_re + ni * a_im) / den
    f_im = (ni * a_re - nr * a_im) / den
    bb_re = f_re[..., None] * b_re - f_im[..., None] * b_im
    bb_im = f_re[..., None] * b_im + f_im[..., None] * b_re
    return lam_re, lam_im, bb_re, bb_im


def _s5_blockdiag_b(bb):
    t = bb.reshape(S5_NGB, S5_GB, SSM_STATE, SSM_GROUP).transpose(0, 1, 3, 2)
    eye = jnp.eye(S5_GB, dtype=bb.dtype)
    return jnp.einsum('bgpn,gh->bgphn', t, eye).reshape(S5_NGB, S5_CH, S5_ST)


def _s5_blockdiag_b_inv(x):
    t = x.reshape(S5_NGB, S5_GB, SSM_GROUP, S5_GB, SSM_STATE)
    eye = jnp.eye(S5_GB, dtype=x.dtype)
    d = jnp.einsum('bgphn,gh->bgpn', t, eye)
    return d.transpose(0, 1, 3, 2).reshape(SSM_GROUPS, SSM_STATE, SSM_GROUP)


def _s5_blockdiag_c(c):
    t = c.reshape(S5_NGB, S5_GB, SSM_GROUP, SSM_STATE).transpose(0, 1, 3, 2)
    eye = jnp.eye(S5_GB, dtype=c.dtype)
    return jnp.einsum('bgnp,gh->bgnhp', t, eye).reshape(S5_NGB, S5_ST, S5_CH)


def _s5_blockdiag_c_inv(x):
    t = x.reshape(S5_NGB, S5_GB, SSM_STATE, S5_GB, SSM_GROUP)
    eye = jnp.eye(S5_GB, dtype=x.dtype)
    d = jnp.einsum('bgnhp,gh->bgnp', t, eye)
    return d.transpose(0, 1, 3, 2).reshape(SSM_GROUPS, SSM_GROUP, SSM_STATE)


def s5_tables(lam_re, lam_im, L):
    pr, pi = lam_re.reshape(1, -1), lam_im.reshape(1, -1)
    n = 1
    while n < L:
        lr, li = pr[n - 1:n], pi[n - 1:n]
        pr, pi = (jnp.concatenate([pr, pr * lr - pi * li], 0), jnp.concatenate([pi, pr * li + pi * lr], 0))
        n *= 2
    nk = int(math.log2(L))
    idx = [2 ** k - 1 for k in range(nk)] + [0] * (8 - nk)

    def blk(t):
        return t.reshape(t.shape[0], S5_NGB, S5_ST).transpose(1, 0, 2)

    def rows(t):
        return jnp.concatenate([t[j:j + 1] for j in idx], axis=0)
    return blk(pr), blk(pi), blk(rows(pr)), blk(rows(pi))


S5_SUB = 8


def _scan_tiles(br, bi, a2r, a2i, reverse):
    L = br.shape[0]
    sub = lax.broadcasted_iota(jnp.int32, br.shape, 0) & (S5_SUB - 1)
    xr, xi = br, bi
    for k in range(3):
        s = 1 << k
        ar, ai = a2r[k:k + 1, :], a2i[k:k + 1, :]
        if reverse:
            sr, si = pltpu.roll(xr, L - s, 0), pltpu.roll(xi, L - s, 0)
            m = sub < S5_SUB - s
        else:
            sr, si = pltpu.roll(xr, s, 0), pltpu.roll(xi, s, 0)
            m = sub >= s
        sr, si = jnp.where(m, sr, 0.0), jnp.where(m, si, 0.0)
        xr, xi = xr + ar * sr - ai * si, xi + ar * si + ai * sr
    return xr, xi


def _scan_chain(xr, xi, pr, pi, cr, ci, out_r, out_i, reverse):
    ntile = xr.shape[0] // S5_SUB
    for g in (reversed(range(ntile)) if reverse else range(ntile)):
        rs = slice(g * S5_SUB, (g + 1) * S5_SUB)
        if reverse:
            nr = xr[rs] + pr * cr + pi * ci
            ni = xi[rs] + pr * ci - pi * cr
            cr, ci = nr[0:1], ni[0:1]
        else:
            nr = xr[rs] + pr * cr - pi * ci
            ni = xi[rs] + pr * ci + pi * cr
            cr, ci = nr[S5_SUB - 1:S5_SUB], ni[S5_SUB - 1:S5_SUB]
        out_r[rs, :] = nr
        out_i[rs, :] = ni
    return cr, ci


def _grid_step(shape):
    s = 0
    for ax, n in enumerate(shape):
        s = s * n + pl.program_id(ax)
    return s


def s5_fwd(h, bre, bim, cre, cim, pwr, pwi, l2r, l2i, dskip, bsz, gather=()):
    T, D = h.shape
    L = S5_L
    S = T // bsz
    NC = S // L
    ng = len(gather)
    grid = (S5_NGB, bsz, NC)
    nsteps = S5_NGB * bsz * NC
    fwd_step = nsteps - max(1, nsteps // 32)

    def body(*refs):
        (h_ref, bre_ref, bim_ref, cre_ref, cim_ref, pwr_ref, pwi_ref, l2r_ref, l2i_ref, d_ref) = refs[:10]
        x_refs = refs[10:10 + ng]
        y_ref, gy_ref, xs_ref, xr_s, xi_s = refs[10 + ng:15 + ng]
        g_refs = refs[15 + ng:15 + 2 * ng]
        car_r, car_i = refs[15 + 2 * ng:17 + 2 * ng]
        if ng:
            start, forward, finish = _gather_phases(x_refs, g_refs, *refs[17 + 2 * ng:])
            step = _grid_step(grid)
            pl.when(step == 0)(start)
            pl.when(step == fwd_step)(forward)

        @pl.when(pl.program_id(2) == 0)
        def _():
            car_r[...] = jnp.zeros_like(car_r)
            car_i[...] = jnp.zeros_like(car_i)
        u = h_ref[...]
        ub = u.astype(BF16)
        cr, ci = car_r[0:1, :], car_i[0:1, :]
        xs_ref[...] = jnp.zeros_like(xs_ref)
        xs_ref[0:1, :] = cr
        xs_ref[1:2, :] = ci
        xr, xi = _scan_tiles(_dot(ub, bre_ref[...]), _dot(ub, bim_ref[...]), l2r_ref[...], l2i_ref[...], False)
        cr, ci = _scan_chain(xr, xi, pwr_ref[...], pwi_ref[...], cr, ci, xr_s, xi_s, False)
        car_r[...] = jnp.broadcast_to(cr, car_r.shape)
        car_i[...] = jnp.broadcast_to(ci, car_i.shape)
        y = (_dot(xr_s[...].astype(BF16), cre_ref[...]) - _dot(xi_s[...].astype(BF16), cim_ref[...])
             + d_ref[...] * u)
        y_ref[...] = y
        gy_ref[...] = jax.nn.gelu(y).astype(BF16)
        if ng:
            pl.when(step == nsteps - 1)(finish)

    tok = lambda g, b, c: (b * NC + c, g)
    par = lambda g, b, c: (g, 0, 0)
    anyspec = pl.BlockSpec(memory_space=pl.ANY)
    return pl.pallas_call(
        body, grid=grid,
        in_specs=[pl.BlockSpec((L, S5_CH), tok),
                  pl.BlockSpec((None, S5_CH, S5_ST), par), pl.BlockSpec((None, S5_CH, S5_ST), par),
                  pl.BlockSpec((None, S5_ST, S5_CH), par), pl.BlockSpec((None, S5_ST, S5_CH), par),
                  pl.BlockSpec((None, 8, S5_ST), par), pl.BlockSpec((None, 8, S5_ST), par),
                  pl.BlockSpec((None, 8, S5_ST), par), pl.BlockSpec((None, 8, S5_ST), par),
                  pl.BlockSpec((1, S5_CH), lambda g, b, c: (0, g))] + [anyspec] * ng,
        out_specs=[pl.BlockSpec((L, S5_CH), tok), pl.BlockSpec((L, S5_CH), tok),
                   pl.BlockSpec((None, 8, S5_ST), lambda g, b, c: (b * NC + c, 0, g)),
                   pl.BlockSpec((L, S5_ST), tok), pl.BlockSpec((L, S5_ST), tok)] + [anyspec] * ng,
        out_shape=[jax.ShapeDtypeStruct((T, D), F32), jax.ShapeDtypeStruct((T, D), BF16),
                   jax.ShapeDtypeStruct((bsz * NC, 8, S5_NGB * S5_ST), F32),
                   jax.ShapeDtypeStruct((T, S5_NGB * S5_ST), F32), jax.ShapeDtypeStruct((T, S5_NGB * S5_ST), F32)]
        + _gather_out_shapes(gather),
        scratch_shapes=[pltpu.VMEM((8, S5_ST), F32), pltpu.VMEM((8, S5_ST), F32)]
        + (_gather_sems(ng) if ng else []),
        name="s5_fwd", compiler_params=_cparams(("arbitrary", "arbitrary", "arbitrary")),
    )(h, bre, bim, cre, cim, pwr, pwi, l2r, l2i, dskip, *gather)


def s5_bwd(h, dy, xs, xr, xi, bre, bim, cre, cim, pwr_rev, pwi_rev, l2r, l2i, dskip, bsz, chips=()):
    T, D = h.shape
    L = S5_L
    S = T // bsz
    NC = S // L
    nc = len(chips)
    grid = (S5_NGB, bsz, NC)
    nsteps = S5_NGB * bsz * NC

    def body(*refs):
        (h_ref, dy_ref, xs_ref, xr_ref, xi_ref, bre_ref, bim_ref, cre_ref, cim_ref, qr_ref, qi_ref,
         l2r_ref, l2i_ref, d_ref) = refs[:14]
        p_refs = refs[14:14 + nc]
        du_ref, dbr_ref, dbi_ref, dcr_ref, dci_ref, dl_ref, dd_ref = refs[14 + nc:21 + nc]
        r_refs = refs[21 + nc:21 + 2 * nc]
        car_r, car_i, dr_s, di_s = refs[21 + 2 * nc:25 + 2 * nc]
        if nc:
            start, finish = _chips_phases(p_refs, r_refs, *refs[25 + 2 * nc:])
            step = _grid_step(grid)
            pl.when(step == 0)(start)
        first = (pl.program_id(1) == 0) & (pl.program_id(2) == 0)

        @pl.when(first)
        def _():
            for r in (dbr_ref, dbi_ref, dcr_ref, dci_ref, dl_ref, dd_ref):
                r[...] = jnp.zeros_like(r)

        @pl.when(pl.program_id(2) == 0)
        def _():
            car_r[...] = jnp.zeros_like(car_r)
            car_i[...] = jnp.zeros_like(car_i)

        u = h_ref[...]
        ub = u.astype(BF16)
        dyv = dy_ref[...]
        dyb = dyv.astype(BF16)
        l2r_v, l2i_v = l2r_ref[...], l2i_ref[...]
        x0r, x0i = xs_ref[0:1, :], xs_ref[1:2, :]
        xr, xi = xr_ref[...], xi_ref[...]
        gr = _dot_nt(dyb, cre_ref[...])
        gi = -_dot_nt(dyb, cim_ref[...])
        dr, di = _scan_tiles(gr, gi, l2r_v, -l2i_v, True)
        cr, ci = _scan_chain(dr, di, qr_ref[...], qi_ref[...], car_r[0:1, :], car_i[0:1, :], dr_s, di_s, True)
        dr, di = dr_s[...], di_s[...]
        car_r[...] = jnp.broadcast_to(cr, car_r.shape)
        car_i[...] = jnp.broadcast_to(ci, car_i.shape)
        row = lax.broadcasted_iota(jnp.int32, xr.shape, 0)
        xpr = jnp.where(row >= 1, pltpu.roll(xr, 1, 0), x0r)
        xpi = jnp.where(row >= 1, pltpu.roll(xi, 1, 0), x0i)
        dl_ref[0:1, :] += _colsum(dr * xpr + di * xpi)
        dl_ref[1:2, :] += _colsum(di * xpr - dr * xpi)
        drb, dib = dr.astype(BF16), di.astype(BF16)
        dcr_ref[...] += _dot_tn(xr.astype(BF16), dyb)
        dci_ref[...] -= _dot_tn(xi.astype(BF16), dyb)
        dbr_ref[...] += _dot_tn(ub, drb)
        dbi_ref[...] += _dot_tn(ub, dib)
        du_ref[...] = _dot_nt(drb, bre_ref[...]) + _dot_nt(dib, bim_ref[...]) + d_ref[...] * dyv
        dd_ref[0:1, :] += _colsum(dyv * u)
        if nc:
            pl.when(step == nsteps - 1)(finish)

    tok = lambda g, b, c: (b * NC + (NC - 1 - c), g)
    par = lambda g, b, c: (g, 0, 0)
    anyspec = pl.BlockSpec(memory_space=pl.ANY)
    return pl.pallas_call(
        body, grid=grid,
        in_specs=[pl.BlockSpec((L, S5_CH), tok), pl.BlockSpec((L, S5_CH), tok),
                  pl.BlockSpec((None, 8, S5_ST), lambda g, b, c: (b * NC + (NC - 1 - c), 0, g)),
                  pl.BlockSpec((L, S5_ST), tok), pl.BlockSpec((L, S5_ST), tok),
                  pl.BlockSpec((None, S5_CH, S5_ST), par), pl.BlockSpec((None, S5_CH, S5_ST), par),
                  pl.BlockSpec((None, S5_ST, S5_CH), par), pl.BlockSpec((None, S5_ST, S5_CH), par),
                  pl.BlockSpec((None, 8, S5_ST), par), pl.BlockSpec((None, 8, S5_ST), par),
                  pl.BlockSpec((None, 8, S5_ST), par), pl.BlockSpec((None, 8, S5_ST), par),
                  pl.BlockSpec((1, S5_CH), lambda g, b, c: (0, g))] + [anyspec] * nc,
        out_specs=[pl.BlockSpec((L, S5_CH), tok),
                   pl.BlockSpec((None, S5_CH, S5_ST), par), pl.BlockSpec((None, S5_CH, S5_ST), par),
                   pl.BlockSpec((None, S5_ST, S5_CH), par), pl.BlockSpec((None, S5_ST, S5_CH), par),
                   pl.BlockSpec((None, 8, S5_ST), par),
                   pl.BlockSpec((8, S5_CH), lambda g, b, c: (0, g))] + [anyspec] * nc,
        out_shape=[jax.ShapeDtypeStruct((T, D), F32),
                   jax.ShapeDtypeStruct((S5_NGB, S5_CH, S5_ST), F32), jax.ShapeDtypeStruct((S5_NGB, S5_CH, S5_ST), F32),
                   jax.ShapeDtypeStruct((S5_NGB, S5_ST, S5_CH), F32), jax.ShapeDtypeStruct((S5_NGB, S5_ST, S5_CH), F32),
                   jax.ShapeDtypeStruct((S5_NGB, 8, S5_ST), F32), jax.ShapeDtypeStruct((8, D), F32)]
        + _chips_out_shapes(chips),
        scratch_shapes=[pltpu.VMEM((8, S5_ST), F32), pltpu.VMEM((8, S5_ST), F32)]
        + [pltpu.VMEM((L, S5_ST), F32)] * 2 + (_chips_sems(nc) if nc else []),
        name="s5_bwd", compiler_params=_cparams(("arbitrary", "arbitrary", "arbitrary")),
    )(h, dy, xs, xr, xi, bre, bim, cre, cim, pwr_rev, pwi_rev, l2r, l2i, dskip, *chips)


def _shift_rows(win, off, n):
    if off == 0:
        return win[:n]
    return pltpu.roll(win, win.shape[0] - off, 0)[:n]


def dwconv_fwd(z, w, b, bsz):
    T, D = z.shape
    S = T // bsz
    TS, CW, PAD = CONV_TS, CONV_CW, CONV_PAD

    def body(z_ref, w_ref, b_ref, y_ref, zp):
        zp[0:PAD, :] = jnp.zeros((PAD, CW), F32)
        zp[PAD:, :] = z_ref[...]
        wv, bv = w_ref[...], b_ref[...]

        def step(t, carry):
            base = pl.multiple_of(t * TS, TS)
            win = zp[pl.ds(base, TS + PAD), :]
            acc = jnp.zeros((TS, CW), F32) + bv
            for k in range(CONV_WIDTH):
                acc = acc + wv[k:k + 1, :] * _shift_rows(win, PAD - (CONV_WIDTH - 1) + k, TS)
            y_ref[pl.ds(base, TS), :] = acc
            return carry
        lax.fori_loop(0, S // TS, step, 0)

    return pl.pallas_call(
        body, grid=(D // CW, bsz),
        in_specs=[pl.BlockSpec((S, CW), lambda c, bb: (bb, c)), pl.BlockSpec((32, CW), lambda c, bb: (0, c)),
                  pl.BlockSpec((1, CW), lambda c, bb: (0, c))],
        out_specs=pl.BlockSpec((S, CW), lambda c, bb: (bb, c)),
        out_shape=jax.ShapeDtypeStruct((T, D), F32),
        scratch_shapes=[pltpu.VMEM((S + PAD, CW), F32)],
        name="dwconv_fwd", compiler_params=_cparams(("arbitrary", "arbitrary")),
    )(z, w, b)


def dwconv_bwd(z, dy, w, bsz):
    T, D = z.shape
    S = T // bsz
    TS, CW, PAD = CONV_TS, CONV_CW, CONV_PAD

    def body(z_ref, dy_ref, w_ref, dz_ref, dw_ref, db_ref, zp, dyp):
        @pl.when(pl.program_id(1) == 0)
        def _():
            dw_ref[...] = jnp.zeros_like(dw_ref)
            db_ref[...] = jnp.zeros_like(db_ref)
        zp[0:PAD, :] = jnp.zeros((PAD, CW), F32)
        zp[PAD:, :] = z_ref[...]
        dyp[0:S, :] = dy_ref[...]
        dyp[S:, :] = jnp.zeros((PAD, CW), F32)
        wv = w_ref[...]

        def step(t, carry):
            base = pl.multiple_of(t * TS, TS)
            zwin = zp[pl.ds(base, TS + PAD), :]
            dwin = dyp[pl.ds(base, TS + PAD), :]
            dyt = dwin[:TS]
            acc = jnp.zeros((TS, CW), F32)
            for j in range(CONV_WIDTH):
                k = CONV_WIDTH - 1 - j
                acc = acc + wv[k:k + 1, :] * _shift_rows(dwin, j, TS)
            dz_ref[pl.ds(base, TS), :] = acc
            for k in range(CONV_WIDTH):
                dw_ref[k:k + 1, :] += _colsum(dyt * _shift_rows(zwin, PAD - (CONV_WIDTH - 1) + k, TS))
            db_ref[0:1, :] += _colsum(dyt)
            return carry
        lax.fori_loop(0, S // TS, step, 0)

    return pl.pallas_call(
        body, grid=(D // CW, bsz),
        in_specs=[pl.BlockSpec((S, CW), lambda c, bb: (bb, c)), pl.BlockSpec((S, CW), lambda c, bb: (bb, c)),
                  pl.BlockSpec((32, CW), lambda c, bb: (0, c))],
        out_specs=[pl.BlockSpec((S, CW), lambda c, bb: (bb, c)), pl.BlockSpec((32, CW), lambda c, bb: (0, c)),
                   pl.BlockSpec((8, CW), lambda c, bb: (0, c))],
        out_shape=[jax.ShapeDtypeStruct((T, D), F32), jax.ShapeDtypeStruct((32, D), F32),
                   jax.ShapeDtypeStruct((8, D), F32)],
        scratch_shapes=[pltpu.VMEM((S + PAD, CW), F32), pltpu.VMEM((S + PAD, CW), F32)],
        name="dwconv_bwd", compiler_params=_cparams(("arbitrary", "arbitrary")),
    )(z, dy, w)


def spatial_fwd(u, vln, ws, bias):
    T, E = u.shape
    C, H = GMLP_CHUNK, GMLP_HEADS
    hw = E // H

    def body(u_ref, v_ref, ws_ref, b_ref, o_ref):
        for hh in range(H):
            sl = slice(hh * hw, (hh + 1) * hw)
            vp = _dot(ws_ref[hh], v_ref[:, sl]) + b_ref[:, sl]
            o_ref[:, sl] = (u_ref[:, sl] * vp).astype(o_ref.dtype)

    return pl.pallas_call(
        body, grid=(T // C,),
        in_specs=[pl.BlockSpec((C, E), lambda i: (i, 0)), pl.BlockSpec((C, E), lambda i: (i, 0)),
                  pl.BlockSpec((H, C, C), lambda i: (0, 0, 0)), pl.BlockSpec((C, E), lambda i: (0, 0))],
        out_specs=pl.BlockSpec((C, E), lambda i: (i, 0)),
        out_shape=jax.ShapeDtypeStruct((T, E), BF16),
        name="spatial_fwd", compiler_params=_cparams(("arbitrary",)),
    )(u, vln, ws, bias)


def spatial_bwd(u, vln, dg, ws, bias):
    T, E = u.shape
    C, H = GMLP_CHUNK, GMLP_HEADS
    hw = E // H

    def body(u_ref, v_ref, dg_ref, ws_ref, b_ref, du_ref, dv_ref, dws_ref, db_ref):
        @pl.when(pl.program_id(0) == 0)
        def _():
            dws_ref[...] = jnp.zeros_like(dws_ref)
            db_ref[...] = jnp.zeros_like(db_ref)
        tril = (lax.broadcasted_iota(jnp.int32, (C, C), 1) <= lax.broadcasted_iota(jnp.int32, (C, C), 0))
        for hh in range(H):
            sl = slice(hh * hw, (hh + 1) * hw)
            v = v_ref[:, sl]
            w = ws_ref[hh]
            dgv = dg_ref[:, sl].astype(F32)
            vp = _dot(w, v) + b_ref[:, sl]
            du_ref[:, sl] = dgv * vp
            dvp = dgv * u_ref[:, sl]
            dvpb = dvp.astype(BF16)
            dv_ref[:, sl] = _dot_tn(w, dvpb)
            dws_ref[hh] += jnp.where(tril, _dot_nt(dvpb, v), 0.0)
            db_ref[:, sl] += dvp

    return pl.pallas_call(
        body, grid=(T // C,),
        in_specs=[pl.BlockSpec((C, E), lambda i: (i, 0)), pl.BlockSpec((C, E), lambda i: (i, 0)),
                  pl.BlockSpec((C, E), lambda i: (i, 0)),
                  pl.BlockSpec((H, C, C), lambda i: (0, 0, 0)), pl.BlockSpec((C, E), lambda i: (0, 0))],
        out_specs=[pl.BlockSpec((C, E), lambda i: (i, 0)), pl.BlockSpec((C, E), lambda i: (i, 0)),
                   pl.BlockSpec((H, C, C), lambda i: (0, 0, 0)), pl.BlockSpec((C, E), lambda i: (0, 0))],
        out_shape=[jax.ShapeDtypeStruct((T, E), F32), jax.ShapeDtypeStruct((T, E), F32),
                   jax.ShapeDtypeStruct((H, C, C), F32), jax.ShapeDtypeStruct((C, E), F32)],
        name="spatial_bwd", compiler_params=_cparams(("arbitrary",)),
    )(u, vln, dg, ws, bias)


def _att_masks():
    r = lax.broadcasted_iota(jnp.int32, (ATT_BLK, ATT_BLK), 0)
    c = lax.broadcasted_iota(jnp.int32, (ATT_BLK, ATT_BLK), 1)
    return c <= r, c >= r


NEG = -1e30
ATT_SCALE = HEAD_DIM ** -0.5


def _att_view(t, dil):
    return t.reshape(t.shape[0] // dil, dil * t.shape[1])


def attn_fwd(name, q, k, v, dil, bsz):
    T, Wd = q.shape
    nb = T // (bsz * dil * ATT_BLK)
    TB = min(nb, ATT_TB)
    nsteps = nb // TB

    def body(q_ref, k_ref, v_ref, kp_ref, vp_ref, o_ref, l_ref):
        n = pl.program_id(2)
        mc, mp = _att_masks()
        for j in range(TB):
            rows = slice(j * ATT_BLK, (j + 1) * ATT_BLK)
            prow = slice((j - 1) * ATT_BLK, j * ATT_BLK)
            hp = (n * TB + j) > 0
            for hh in range(ATT_HEADS):
                ls = slice(hh * HEAD_DIM, (hh + 1) * HEAD_DIM)
                qj, kc, vc = q_ref[rows, ls], k_ref[rows, ls], v_ref[rows, ls]
                kp = k_ref[prow, ls] if j > 0 else kp_ref[:, ls]
                vp = v_ref[prow, ls] if j > 0 else vp_ref[:, ls]
                sc = jnp.where(mc, _dot_nt(qj, kc) * ATT_SCALE, NEG)
                sp = jnp.where(mp & hp, _dot_nt(qj, kp) * ATT_SCALE, NEG)
                m = jnp.maximum(jnp.max(sc, axis=1, keepdims=True), jnp.max(sp, axis=1, keepdims=True))
                pc, pp = jnp.exp(sc - m), jnp.exp(sp - m)
                l = jnp.sum(pc, axis=1, keepdims=True) + jnp.sum(pp, axis=1, keepdims=True)
                o_ref[rows, ls] = (_dot(pc.astype(BF16), vc) + _dot(pp.astype(BF16), vp)) / l
                l_ref[rows, ls] = jnp.broadcast_to(m + jnp.log(l), (ATT_BLK, HEAD_DIM))

    blk = pl.BlockSpec((TB * ATT_BLK, Wd), lambda b, r, n: (b * nsteps + n, r))
    prev = pl.BlockSpec((ATT_BLK, Wd), lambda b, r, n: (jnp.maximum(b * nb + n * TB - 1, 0), r))
    qv, kv, vv = (_att_view(t, dil) for t in (q, k, v))
    o, l = pl.pallas_call(
        body, grid=(bsz, dil, nsteps), in_specs=[blk, blk, blk, prev, prev], out_specs=[blk, blk],
        out_shape=[jax.ShapeDtypeStruct(qv.shape, F32), jax.ShapeDtypeStruct(qv.shape, F32)],
        name=name, compiler_params=_cparams(("arbitrary", "arbitrary", "arbitrary")),
    )(qv, kv, vv, kv, vv)
    return o.reshape(T, Wd), l.reshape(T, Wd)


def attn_bwd(name, q, k, v, do, mg, lse, dil, bsz):
    T, Wd = q.shape
    nb = T // (bsz * dil * ATT_BLK)
    TB = min(nb, ATT_TB)
    nsteps = nb // TB

    def body(q_ref, k_ref, v_ref, do_ref, mg_ref, l_ref, kp_ref, vp_ref, qn_ref, don_ref, mgn_ref, ln_ref,
             dq_ref, dk_ref, dv_ref):
        n = pl.program_id(2)
        mc, mp = _att_masks()

        def probs(qj, kk, lse_col, mask):
            s = _dot_nt(qj, kk) * ATT_SCALE
            return jnp.where(mask, jnp.exp(s - lse_col), 0.0)

        def ds_of(p, doj, vv, delta):
            return (p * (_dot_nt(doj, vv) - delta) * ATT_SCALE).astype(BF16)

        for hh in range(ATT_HEADS):
            ls = slice(hh * HEAD_DIM, (hh + 1) * HEAD_DIM)
            dk = [None] * TB
            dv = [None] * TB
            for j in range(TB + 1):
                rows = slice(j * ATT_BLK, (j + 1) * ATT_BLK)
                prow = slice((j - 1) * ATT_BLK, j * ATT_BLK)
                if j < TB:
                    qj, doj, mgj, lj = q_ref[rows, ls], do_ref[rows, ls], mg_ref[rows, ls], l_ref[rows, ls]
                    hp = (n * TB + j) > 0
                else:
                    qj, doj, mgj, lj = qn_ref[:, ls], don_ref[:, ls], mgn_ref[:, ls], ln_ref[:, ls]
                    hp = (n + 1) * TB < nb
                lse_col = lj[:, 0:1]
                delta = jnp.sum(doj.astype(F32) * mgj.astype(F32), axis=1, keepdims=True)
                if j > 0:
                    kp, vp = k_ref[prow, ls], v_ref[prow, ls]
                else:
                    kp, vp = kp_ref[:, ls], vp_ref[:, ls]
                pp = probs(qj, kp, lse_col, mp & hp)
                dsp = ds_of(pp, doj, vp, delta)
                if j > 0:
                    dk[j - 1] = dk[j - 1] + _dot_tn(dsp, qj)
                    dv[j - 1] = dv[j - 1] + _dot_tn(pp.astype(BF16), doj)
                if j < TB:
                    kc, vc = k_ref[rows, ls], v_ref[rows, ls]
                    pc = probs(qj, kc, lse_col, mc)
                    dsc = ds_of(pc, doj, vc, delta)
                    dq_ref[rows, ls] = (_dot(dsc, kc) + _dot(dsp, kp)).astype(dq_ref.dtype)
                    dk[j] = _dot_tn(dsc, qj)
                    dv[j] = _dot_tn(pc.astype(BF16), doj)
            for j in range(TB):
                rows = slice(j * ATT_BLK, (j + 1) * ATT_BLK)
                dk_ref[rows, ls] = dk[j].astype(dk_ref.dtype)
                dv_ref[rows, ls] = dv[j].astype(dv_ref.dtype)

    blk = pl.BlockSpec((TB * ATT_BLK, Wd), lambda b, r, n: (b * nsteps + n, r))
    prev = pl.BlockSpec((ATT_BLK, Wd), lambda b, r, n: (jnp.maximum(b * nb + n * TB - 1, 0), r))
    nxt = pl.BlockSpec((ATT_BLK, Wd), lambda b, r, n: (b * nb + jnp.minimum((n + 1) * TB, nb - 1), r))
    qv, kv, vv, dov, mgv, lv = (_att_view(t, dil) for t in (q, k, v, do, mg, lse))
    res = pl.pallas_call(
        body, grid=(bsz, dil, nsteps), in_specs=[blk] * 6 + [prev, prev, nxt, nxt, nxt, nxt],
        out_specs=[blk, blk, blk], out_shape=[jax.ShapeDtypeStruct(qv.shape, BF16)] * 3,
        name=name, compiler_params=_cparams(("arbitrary", "arbitrary", "arbitrary")),
    )(qv, kv, vv, dov, mgv, lv, kv, vv, qv, dov, mgv, lv)
    return [t.reshape(T, Wd) for t in res]


QKV_SLOTS = 3 * len(ATT_CONFIGS) * ATT_HEADS
SLOTS_PER_DEV = QKV_SLOTS // N_DEV


def _head_slots(t3):
    return [t3[s // SLOTS_PER_DEV][:, (s % SLOTS_PER_DEV) * HEAD_DIM:(s % SLOTS_PER_DEV + 1) * HEAD_DIM]
            for s in range(QKV_SLOTS)]


def _heads_of(slots, k):
    return jnp.concatenate(slots[k * ATT_HEADS:(k + 1) * ATT_HEADS], axis=1)


def _slots_to_blocked(slots):
    return jnp.stack([jnp.concatenate(slots[b * SLOTS_PER_DEV:(b + 1) * SLOTS_PER_DEV], axis=1)
                      for b in range(N_DEV)])


def _coords():
    return lax.axis_index("x"), lax.axis_index("y"), lax.axis_index("c")


def all_gather(name, xs):
    n = len(xs)

    def body(*refs):
        start, forward, finish = _gather_phases(refs[:n], refs[n:2 * n], *refs[2 * n:])
        start()
        forward()
        finish()

    anyspec = pl.BlockSpec(memory_space=pl.ANY)
    return pl.pallas_call(
        body, out_shape=_gather_out_shapes(xs), in_specs=[anyspec] * n, out_specs=[anyspec] * n,
        scratch_shapes=_gather_sems(n), name=name,
    )(*xs)


def _gather_out_shapes(xs):
    return [jax.ShapeDtypeStruct((N_DEV,) + t.shape, t.dtype) for t in xs]


def _gather_sems(n):
    return [pltpu.SemaphoreType.DMA((7 * n,)), pltpu.SemaphoreType.DMA((7 * n,)), pltpu.SemaphoreType.DMA((n,))]


def _gather_phases(x_refs, out_refs, send_sems, recv_sems, local_sems):
    n = len(x_refs)

    def parts():
        x, y, c = _coords()
        return (x, y, c), (x, y, 1 - c), [(1 - x, y), (x, 1 - y), (1 - x, 1 - y)], c

    def slot(a, px, py, pc):
        return out_refs[a].at[4 * px + 2 * py + pc]

    def copy(a, k, block, to, src=None):
        return pltpu.make_async_remote_copy(
            src_ref=slot(a, *block) if src is None else src, dst_ref=slot(a, *block),
            send_sem=send_sems.at[7 * a + k], recv_sem=recv_sems.at[7 * a + k],
            device_id=to, device_id_type=MESH)

    def mine(a, me):
        return pltpu.make_async_copy(x_refs[a], slot(a, *me), local_sems.at[a])

    def first(a, me, sibling, chips, c):
        return ([copy(a, 0, me, sibling, src=x_refs[a])]
                + [copy(a, 1 + j, me, (*chip, c), src=x_refs[a]) for j, chip in enumerate(chips)])

    def start():
        me, sibling, chips, c = parts()
        for a in range(n):
            mine(a, me).start()
        for a in range(n):
            for cp in first(a, me, sibling, chips, c):
                cp.start()

    def forward():
        me, sibling, chips, c = parts()
        for j, chip in enumerate(chips):
            for a in range(n):
                copy(a, 1 + j, (*chip, c), me).wait_recv()
                copy(a, 4 + j, (*chip, c), sibling).start()

    def finish():
        me, sibling, chips, c = parts()
        for a in range(n):
            copy(a, 0, sibling, me).wait_recv()
            for j, chip in enumerate(chips):
                copy(a, 4 + j, (*chip, 1 - c), me).wait_recv()
        for a in range(n):
            for cp in first(a, me, sibling, chips, c):
                cp.wait_send()
            for j, chip in enumerate(chips):
                copy(a, 4 + j, (*chip, c), sibling).wait_send()
            mine(a, me).wait()

    return start, forward, finish


def exchange_sibling(name, gs):
    n = len(gs)

    def body(*refs):
        g_refs, out_refs = refs[:n], refs[n:2 * n]
        send_sems, recv_sems = refs[2 * n:]
        x, y, c = _coords()
        sibling = (x, y, 1 - c)
        cps = []
        for a in range(n):
            for q in range(4):
                cps.append(pltpu.make_async_remote_copy(
                    src_ref=g_refs[a].at[2 * q + (1 - c)], dst_ref=out_refs[a].at[q],
                    send_sem=send_sems.at[4 * a + q], recv_sem=recv_sems.at[4 * a + q],
                    device_id=sibling, device_id_type=MESH))
        for cp in cps:
            cp.start()
        for cp in cps:
            cp.wait_recv()
        for cp in cps:
            cp.wait_send()

    anyspec = pl.BlockSpec(memory_space=pl.ANY)
    return pl.pallas_call(
        body, out_shape=[jax.ShapeDtypeStruct((4,) + g.shape[1:], g.dtype) for g in gs],
        in_specs=[anyspec] * n, out_specs=[anyspec] * n,
        scratch_shapes=[pltpu.SemaphoreType.DMA((4 * n,)), pltpu.SemaphoreType.DMA((4 * n,))],
        name=name,
    )(*gs)


def exchange_chips(name, ps):
    n = len(ps)

    def body(*refs):
        start, finish = _chips_phases(refs[:n], refs[n:2 * n], *refs[2 * n:])
        start()
        finish()

    anyspec = pl.BlockSpec(memory_space=pl.ANY)
    return pl.pallas_call(
        body, out_shape=_chips_out_shapes(ps), in_specs=[anyspec] * n, out_specs=[anyspec] * n,
        scratch_shapes=_chips_sems(n), name=name,
    )(*ps)


def _chips_out_shapes(ps):
    return [jax.ShapeDtypeStruct((3,) + p.shape[1:], p.dtype) for p in ps]


def _chips_sems(n):
    return [pltpu.SemaphoreType.DMA((3 * n,)), pltpu.SemaphoreType.DMA((3 * n,))]


def _chips_phases(p_refs, out_refs, send_sems, recv_sems):
    n = len(p_refs)

    def copies():
        x, y, c = _coords()
        chips = [(1 - x, y), (x, 1 - y), (1 - x, 1 - y)]
        return [pltpu.make_async_remote_copy(
            src_ref=p_refs[a].at[2 * px + py], dst_ref=out_refs[a].at[k],
            send_sem=send_sems.at[3 * a + k], recv_sem=recv_sems.at[3 * a + k],
            device_id=(px, py, c), device_id_type=MESH)
            for a in range(n) for k, (px, py) in enumerate(chips)]

    def start():
        for cp in copies():
            cp.start()

    def finish():
        cps = copies()
        for cp in cps:
            cp.wait_recv()
        for cp in cps:
            cp.wait_send()

    return start, finish


def _row_tile(R):
    tr = 256
    while R % tr:
        tr //= 2
    assert tr % 8 == 0
    return tr


def add_sibling(name, g, recv, c_idx):
    _, R, C = g.shape
    tr = _row_tile(R)

    def body(c_ref, g_ref, r_ref, o_ref, o16_ref):
        s = g_ref[...] + r_ref[...].astype(F32)
        o_ref[...] = s
        o16_ref[...] = s.astype(BF16)

    out = pl.BlockSpec((None, tr, C), lambda q, i, cr: (q, i, 0))
    return pl.pallas_call(
        body,
        grid_spec=pltpu.PrefetchScalarGridSpec(
            num_scalar_prefetch=1, grid=(4, R // tr),
            in_specs=[pl.BlockSpec((None, tr, C), lambda q, i, cr: (2 * q + cr[0], i, 0)), out],
            out_specs=[out, out]),
        out_shape=[jax.ShapeDtypeStruct((4, R, C), F32), jax.ShapeDtypeStruct((4, R, C), BF16)], name=name,
        compiler_params=_cparams(("arbitrary", "arbitrary")),
    )(c_idx, g, recv)


def _adam_math(w, g, m, v):
    m = ADAM_B1 * m + (1.0 - ADAM_B1) * g
    v = ADAM_B2 * v + (1.0 - ADAM_B2) * jnp.square(g)
    m_hat = m / (1.0 - ADAM_B1 ** ADAM_STEP)
    v_hat = v / (1.0 - ADAM_B2 ** ADAM_STEP)
    delta = -ADAM_LR * (m_hat / (jnp.sqrt(v_hat) + ADAM_EPS) + ADAM_WD * w)
    return delta, m, v


def adam_big(name, p1, recv, w, m, v, chip_idx, layer=0):
    _, R, C = p1.shape
    tr = _row_tile(R)
    nt = R // tr

    def body(q_ref, p_ref, r_ref, w_ref, m_ref, v_ref, g_ref, d_ref, nm_ref, nv_ref):
        g = ((p_ref[...] + r_ref[0].astype(F32)) + r_ref[1].astype(F32)) + r_ref[2].astype(F32)
        d, nm, nv = _adam_math(w_ref[...], g, m_ref[...], v_ref[...])
        g_ref[...] = g
        d_ref[...] = d
        nm_ref[...] = nm
        nv_ref[...] = nv

    row_in = pl.BlockSpec((tr, C), lambda i, qr: (layer * nt + i, 0))
    row = pl.BlockSpec((tr, C), lambda i, qr: (i, 0))
    return pl.pallas_call(
        body,
        grid_spec=pltpu.PrefetchScalarGridSpec(
            num_scalar_prefetch=1, grid=(nt,),
            in_specs=[pl.BlockSpec((None, tr, C), lambda i, qr: (qr[0], i, 0)),
                      pl.BlockSpec((3, tr, C), lambda i, qr: (0, i, 0)), row_in, row_in, row_in],
            out_specs=[row, row, row, row]),
        out_shape=[jax.ShapeDtypeStruct((R, C), F32)] * 4, name=name,
        compiler_params=_cparams(("arbitrary",)),
    )(chip_idx, p1, recv, w, m, v)


def sum8(parts):
    _, R, C = parts.shape

    def body(p_ref, o_ref):
        acc = p_ref[0]
        for k in range(1, N_DEV):
            acc = acc + p_ref[k]
        o_ref[...] = acc

    tr = 128
    while R % tr:
        tr //= 2
    assert tr % 8 == 0
    return pl.pallas_call(
        body, grid=(R // tr,), in_specs=[pl.BlockSpec((N_DEV, tr, C), lambda i: (0, i, 0))],
        out_specs=pl.BlockSpec((tr, C), lambda i: (i, 0)), out_shape=jax.ShapeDtypeStruct((R, C), F32),
        name="sum8", compiler_params=_cparams(("arbitrary",)),
    )(parts)


def adam_small(w, g, m, v):
    def fn(wt, gt, mt, vt):
        return _adam_math(wt, gt, mt, vt)
    C = w.shape[1]
    return rowwise("adam_small", fn, [w, g, m, v], [], [(C, F32)] * 3, tr=128)


def _pack(arrs, rows_mult=8):
    flat = jnp.concatenate([a.reshape(-1) for a in arrs])
    n = flat.shape[0]
    per = PACK_C * rows_mult
    pad = (-n) % per
    if pad:
        flat = jnp.concatenate([flat, jnp.zeros((pad,), flat.dtype)])
    return flat.reshape(-1, PACK_C)


def _unpack(buf, shapes):
    flat = buf.reshape(-1)
    out, off = [], 0
    for s in shapes:
        n = math.prod(s)
        out.append(flat[off:off + n].reshape(s))
        off += n
    return out


def _blocked(gfull, axis):
    shp = gfull.shape
    n = shp[axis] // N_DEV
    t = gfull.reshape(shp[:axis] + (N_DEV, n) + shp[axis + 1:])
    t = jnp.moveaxis(t, axis, 0)
    return t.reshape(N_DEV, -1)


def _unblocked(gathered, shard_shape, axis):
    t = jnp.moveaxis(gathered, 0, axis)
    shp = shard_shape[:axis] + (N_DEV * shard_shape[axis],) + shard_shape[axis + 1:]
    return t.reshape(shp)


def _relu2_epi(acc):
    r = jnp.maximum(acc, 0.0)
    return acc, r * r


def _step(x3, target3, W, comm):
    bsz, S, D = x3.shape
    T = bsz * S
    x = x3.reshape(T, D)
    target = target3.reshape(T, D)
    row = lambda v: v.reshape(1, -1)
    grads = {}

    s5p = (W['ssm_a_re'][0], W['ssm_a_im'][0], W['ssm_log_dt'][0], W['ssm_b_re'][0], W['ssm_b_im'][0])
    (lam_re, lam_im, bb_re, bb_im), s5_disc_vjp = jax.vjp(s5_disc, *s5p)
    pwr, pwi, l2r, l2i = s5_tables(lam_re, lam_im, S5_SUB)
    bre, bim = _s5_blockdiag_b(bb_re).astype(BF16), _s5_blockdiag_b(bb_im).astype(BF16)
    cre, cim = _s5_blockdiag_c(W['ssm_c_re'][0]).astype(BF16), _s5_blockdiag_c(W['ssm_c_im'][0]).astype(BF16)
    dskip = W['ssm_d']

    tril = jnp.tril(jnp.ones((GMLP_CHUNK, GMLP_CHUNK), bool))
    ws = jnp.where(tril[None], W['gmlp_w_s'][0], 0.0).astype(BF16)
    hw = D // GMLP_HEADS
    sbias = jnp.repeat(W['gmlp_b_s'][0].T, hw, axis=1)

    saved = []
    def add_norm(acc, *ex):
        xn = acc + ex[0] + ex[1] if len(ex) == 3 else acc + ex[0]
        return xn, _rms(xn, ex[-1])

    for i in range(DEPTH):
        sv = {'x': x}
        nm = W['norm_mix'][i:i + 1]
        nl = W['norm_mlp'][i:i + 1]
        if i == 0:
            h, hf = rms_fwd("rms_mix0", x, nm, want_f32=True)
            res = s5_fwd(hf, bre, bim, cre, cim, pwr, pwi, l2r, l2i, dskip, bsz, gather=comm.gather_list)
            ypre, gy, xs, xr_all, xi_all = res[:5]
            Wfull, Wsh = comm.weights(res[5:])
            W = {**W, **Wsh}
            conv_w = jnp.concatenate([W['conv_w_dw'][0], jnp.zeros((1, D), F32)], axis=0)
            z, = matmul("s5_glu_mm", gy, Wfull['ssm_w_glu'], mode='cb')
            def s5_glu(zt, xt, g):
                xn = xt + _glu(zt)
                return xn, _rms(xn, g)
            x1, h2 = rowwise("s5_glu", s5_glu, [z, x], [nl], [(D, F32), (D, BF16)])
            sv.update(hf=hf, ypre=ypre, gy=gy, xs=xs, xr=xr_all, xi=xi_all, z=z)
        elif i == 1:
            z, = matmul("conv_pw1", h, Wfull['conv_w_pw1'], mode='cb', epi=lambda acc, b: (acc + b,),
                        extras=[(W['conv_b_pw1'], 'row')])
            zg, = rowwise("conv_glu", _glu, [z], [], [(D, F32)])
            yc = dwconv_fwd(zg, conv_w, W['conv_b_dw'], bsz)
            y2, = rowwise("conv_ln_silu", lambda t, g, b: jax.nn.silu(_ln(t, g, b)), [yc],
                          [W['conv_ln_g'], W['conv_ln_b']], [(D, BF16)])
            x1, h2 = matmul("conv_pw2", y2, Wfull['conv_w_pw2'], epi=add_norm, whole_rows=True,
                            extras=[(W['conv_b_pw2'], 'row'), (x, 'tile'), (nl, 'row')], out_dtypes=(F32, BF16))
            sv.update(h=h, z=z, zg=zg, yc=yc, y2=y2)
        elif i == 2:
            zp, = matmul("gmlp_in", h, Wfull['gmlp_w_in'], mode='cb')

            def gm_pre(zt, g, b):
                a = jax.nn.gelu(zt)
                return a[:, :D], _ln(a[:, D:], g, b)
            u, vln = rowwise("gmlp_pre", gm_pre, [zp], [W['gmlp_ln_g'], W['gmlp_ln_b']], [(D, F32), (D, BF16)])
            gated = spatial_fwd(u, vln, ws, sbias)
            x1, h2 = matmul("gmlp_out", gated, Wfull['gmlp_w_out'], epi=add_norm, whole_rows=True,
                            extras=[(x, 'tile'), (nl, 'row')], out_dtypes=(F32, BF16))
            sv.update(h=h, zp=zp, u=u, vln=vln, gated=gated)
        else:
            qkv3, = matmul("attn_qkv", h, Wfull['attn_w_qkv'], mode='cb', out_dtypes=(BF16,), out3=True)
            slots = _head_slots(qkv3)
            ng = len(ATT_CONFIGS)
            outs, lses, blocks = [], [], []
            for gi, (window, dil) in enumerate(ATT_CONFIGS):
                qb, kb, vb = (_heads_of(slots, j * ng + gi) for j in range(3))
                ob, lb = attn_fwd("attn_fwd%d" % gi, qb, kb, vb, dil, bsz)
                blocks.append((qb, kb, vb, lb, dil))
                outs.append(ob)
                lses.append(lb)

            def merge(o0, o1, o2, l0, l1, l2):
                m = jnp.maximum(jnp.maximum(l0, l1), l2)
                e0, e1, e2 = jnp.exp(l0 - m), jnp.exp(l1 - m), jnp.exp(l2 - m)
                inv = 1.0 / (e0 + e1 + e2)
                w0, w1, w2 = e0 * inv, e1 * inv, e2 * inv
                return w0 * o0 + w1 * o1 + w2 * o2, w0, w1, w2
            merged, w0, w1, w2 = rowwise("attn_merge", merge, outs + lses, [],
                                         [(ATT_W, BF16), (ATT_W, F32), (ATT_W, F32), (ATT_W, F32)])
            wo = Wfull['attn_w_o']
            wo_nat = wo.transpose(1, 0, 2).reshape(wo.shape[1], N_DEV * wo.shape[2])
            x1, h2 = matmul("attn_o", merged, wo_nat, epi=add_norm, whole_rows=True,
                            extras=[(x, 'tile'), (nl, 'row')], out_dtypes=(F32, BF16))
            sv.update(h=h, blocks=blocks, merged=merged, wts=(w0, w1, w2))
        a, act = matmul("mlp_in%d" % i, h2, Wfull['mlp_w_in'][i], mode='cb', epi=_relu2_epi, out_dtypes=(BF16, BF16))
        if i + 1 < DEPTH:
            x2, h = matmul("mlp_out%d" % i, act, Wfull['mlp_w_out'][i], epi=add_norm, whole_rows=True,
                           extras=[(x1, 'tile'), (W['norm_mix'][i + 1:i + 2], 'row')], out_dtypes=(F32, BF16))
        else:
            x2, = matmul("mlp_out%d" % i, act, Wfull['mlp_w_out'][i], epi=lambda acc, r: (acc + r,),
                         extras=[(x1, 'tile')])
        sv.update(x1=x1, h2=h2, a=a, act=act)
        saved.append(sv)
        x = x2

    def loss_fn(xt, tt, g):
        y, vjp = jax.vjp(_rms, xt, g)
        err = y - tt
        dxx, dg = vjp(err * (1.0 / D))
        lval = jnp.sum(jnp.sum(err * err, axis=1, keepdims=True), axis=0, keepdims=True) * (0.5 / D)
        return dxx, dxx, jnp.broadcast_to(lval, (1, 128)), dg
    dx, dxb, lacc, dnf = rowwise("loss_head", loss_fn, [x, target], [row(W['norm_final'])],
                                 [(D, F32), (D, BF16)], [((1, 128), F32), ((1, D), F32)])
    loss_local = lacc[0, 0]
    grads['norm_final'] = dnf.reshape(-1)

    g_norm_mix, g_norm_mlp = [None] * DEPTH, [None] * DEPTH
    g_mlp_in, g_mlp_out = [None] * DEPTH, [None] * DEPTH
    nl_all = [W['norm_mlp'][i:i + 1] for i in range(DEPTH)]
    nm_all = [W['norm_mix'][i:i + 1] for i in range(DEPTH)]

    def norm_bwd(xt, dres, g):
        def epi(dh, xv, dr, gv):
            _, vjp = jax.vjp(_rms, xv, gv)
            dxv, dgv = vjp(dh)
            dxv = dxv + dr
            return dxv, dxv, dgv
        return dict(epi=epi, extras=[xt, dres], params=[g], out_dtypes=(F32, BF16), acc_out=[((1, D), F32)])

    for i in reversed(range(DEPTH)):
        sv = saved[i]
        da, = matmul("mlp_out_bwd%d" % i, dxb, Wfull['mlp_w_out'][i], mode='nt',
                     epi=lambda acc, av: (acc * (2.0 * jnp.maximum(av.astype(F32), 0.0)),),
                     extras=[(sv['a'], 'tile')], out_dtypes=(BF16,))
        g_mlp_out[i] = _rows_blocked(wgrad("mlp_out_wg%d" % i, sv['act'], dxb))
        dx, dxb, dg = matmul_nt_cb("mlp_in_bwd%d" % i, da, Wfull['mlp_w_in'][i], **norm_bwd(sv['x1'], dx, nl_all[i]))
        g_mlp_in[i] = wgrad("mlp_in_wg%d" % i, sv['h2'], da, cb=True)
        g_norm_mlp[i] = dg.reshape(-1)
        xin = sv['x']
        if i == 0:
            dz, = rowwise("s5_glu_bwd", _glu_bwd, [sv['z'], dx], [], [(2 * D, BF16)])
            dgy, = matmul_nt_cb("s5_glu_mm_bwd", dz, Wfull['ssm_w_glu'])
            grads['ssm_w_glu'] = wgrad("s5_glu_wg", sv['gy'], dz, cb=True)

            def gelu_bwd(yt, dt):
                _, vjp = jax.vjp(jax.nn.gelu, yt)
                return vjp(dt)[0]
            dypre, = rowwise("s5_gelu_bwd", gelu_bwd, [sv['ypre'], dgy], [], [(D, F32)])
            grads['mlp_w_in'], grads['mlp_w_out'] = g_mlp_in, g_mlp_out
            res = s5_bwd(sv['hf'], dypre, sv['xs'], sv['xr'], sv['xi'], bre, bim, cre, cim,
                         pwr[:, ::-1], pwi[:, ::-1], l2r, l2i, dskip, bsz, chips=comm.rs_front(grads))
            du, dbr, dbi, dcr, dci, dl, dd = res[:7]
            comm.recv2 = res[7:]
            dlam_re = dl[:, 0, :].reshape(SSM_GROUPS, SSM_STATE)
            dlam_im = dl[:, 1, :].reshape(SSM_GROUPS, SSM_STATE)
            s5_cot = (dlam_re, dlam_im, _s5_blockdiag_b_inv(dbr), _s5_blockdiag_b_inv(dbi))
            grads['ssm_c_re'] = _s5_blockdiag_c_inv(dcr)[None]
            grads['ssm_c_im'] = _s5_blockdiag_c_inv(dci)[None]
            grads['ssm_d'] = dd[0:1]
            dx, dxb, dg = rms_bwd("rms_mix_bwd0", xin, du, dx, nm_all[0])
        elif i == 1:
            dy2, = matmul("conv_pw2_bwd", dxb, Wfull['conv_w_pw2'], mode='nt')
            grads['conv_w_pw2'] = _rows_blocked(wgrad("conv_pw2_wg", sv['y2'], dxb))

            def ln_silu_bwd(yt, dt, dxt, g, b):
                _, vjp = jax.vjp(lambda t, gg, bb: jax.nn.silu(_ln(t, gg, bb)), yt, g, b)
                dyc, dgg, dbb = vjp(dt)
                return dyc, dgg, dbb, _colsum(dxt)
            dyc, dlg, dlb, dbp2 = rowwise("conv_ln_silu_bwd", ln_silu_bwd, [sv['yc'], dy2, dx],
                                          [W['conv_ln_g'], W['conv_ln_b']], [(D, F32)],
                                          [((1, D), F32), ((1, D), F32), ((1, D), F32)])
            grads['conv_ln_g'], grads['conv_ln_b'], grads['conv_b_pw2'] = dlg, dlb, dbp2
            dzg, dwd, dbd = dwconv_bwd(sv['zg'], dyc, conv_w, bsz)
            grads['conv_w_dw'] = dwd[None, :CONV_WIDTH]
            grads['conv_b_dw'] = dbd[0:1]

            def glu_bwd1(zt, dyt):
                dzt = _glu_bwd(zt, dyt)
                return dzt, _colsum(dzt)
            dz, dbp1 = rowwise("conv_glu_bwd", glu_bwd1, [sv['z'], dzg], [], [(2 * D, BF16)], [((1, 2 * D), F32)])
            grads['conv_b_pw1'] = dbp1
            dx, dxb, dg = matmul_nt_cb("conv_pw1_bwd", dz, Wfull['conv_w_pw1'], **norm_bwd(xin, dx, nm_all[i]))
            grads['conv_w_pw1'] = wgrad("conv_pw1_wg", sv['h'], dz, cb=True)
        elif i == 2:
            dgt, = matmul("gmlp_out_bwd", dxb, Wfull['gmlp_w_out'], mode='nt', out_dtypes=(BF16,))
            grads['gmlp_w_out'] = _rows_blocked(wgrad("gmlp_out_wg", sv['gated'], dxb))
            du, dvln, dws, dsb = spatial_bwd(sv['u'], sv['vln'], dgt, ws, sbias)
            grads['gmlp_w_s'] = dws[None]
            grads['gmlp_b_s'] = dsb.reshape(GMLP_CHUNK, GMLP_HEADS, hw).sum(-1).T[None]

            def gm_pre_bwd(zt, dut, dvt, g, b):
                _, vjp_u = jax.vjp(jax.nn.gelu, zt[:, :D])
                _, vjp_v = jax.vjp(lambda zz, gg, bb: _ln(jax.nn.gelu(zz), gg, bb), zt[:, D:], g, b)
                dz2, dgg, dbb = vjp_v(dvt)
                return jnp.concatenate([vjp_u(dut)[0], dz2], axis=1), dgg, dbb
            dzp, dlg, dlb = rowwise("gmlp_pre_bwd", gm_pre_bwd, [sv['zp'], du, dvln],
                                    [W['gmlp_ln_g'], W['gmlp_ln_b']], [(2 * D, BF16)], [((1, D), F32), ((1, D), F32)])
            grads['gmlp_ln_g'], grads['gmlp_ln_b'] = dlg, dlb
            dx, dxb, dg = matmul_nt_cb("gmlp_in_bwd", dzp, Wfull['gmlp_w_in'], **norm_bwd(xin, dx, nm_all[i]))
            grads['gmlp_w_in'] = wgrad("gmlp_in_wg", sv['h'], dzp, cb=True)
        else:
            dm, = matmul_nt_cb("attn_o_bwd", dxb, Wfull['attn_w_o'])
            grads['attn_w_o'] = wgrad("attn_o_wg", sv['merged'], dxb, cb=True)
            w0, w1, w2 = sv['wts']
            do0, do1, do2 = rowwise("attn_merge_bwd", lambda d, a, b, c: (a * d, b * d, c * d), [dm, w0, w1, w2], [],
                                    [(ATT_W, BF16)] * 3)
            dparts = [[None] * 3 for _ in range(3)]
            for gi, (dog, (qb, kb, vb, lb, dil)) in enumerate(zip((do0, do1, do2), sv['blocks'])):
                dqb, dkb, dvb = attn_bwd("attn_bwd%d" % gi, qb, kb, vb, dog, sv['merged'], lb, dil, bsz)
                for j, t in enumerate((dqb, dkb, dvb)):
                    dparts[j][gi] = t
            dslots = [dparts[j][gi][:, hh * HEAD_DIM:(hh + 1) * HEAD_DIM]
                      for j in range(3) for gi in range(3) for hh in range(ATT_HEADS)]
            dqkv3 = _slots_to_blocked(dslots)
            dx, dxb, dg = matmul_nt_cb("attn_qkv_bwd", dqkv3, Wfull['attn_w_qkv'], a3=True,
                                       **norm_bwd(xin, dx, nm_all[i]))
            grads['attn_w_qkv'] = wgrad("attn_qkv_wg", sv['h'], dqkv3, cb=True, g3=True)
        g_norm_mix[i] = dg.reshape(-1)

    grads['norm_mix'] = jnp.stack(g_norm_mix)
    grads['norm_mlp'] = jnp.stack(g_norm_mlp)
    grads['mlp_w_in'] = g_mlp_in
    grads['mlp_w_out'] = g_mlp_out
    return loss_local, dx.reshape(bsz, S, D), grads, (s5_disc_vjp, s5_cot)


class _StepComm:
    def __init__(self, Wl, c_idx):
        self.Wl, self.c_idx = Wl, c_idx
        self.units = []
        for n in BIG:
            self.units += [(n, i) for i in range(DEPTH)] if Wl[n].shape[0] == DEPTH else [(n, None)]
        self.ss_names = list(SMALL_SHARDED)
        spack = _pack([Wl[n] for n in self.ss_names])
        self.gather_list = [Wl[n][0 if i is None else i].astype(BF16) for n, i in self.units] + [spack]
        self.p1 = self.recv2 = None

    @staticmethod
    def tag(n, i):
        return n if i is None else "%s%d" % (n, i)

    def weights(self, gathered):
        Wl = self.Wl
        Wfull = {}
        for (n, i), g in zip(self.units, gathered):
            w = g if BIG[n] == 2 else g.reshape(N_DEV * g.shape[1], g.shape[2])
            if i is None:
                Wfull[n] = w
            else:
                Wfull.setdefault(n, []).append(w)
        sparts = _unpack_gathered(gathered[-1], [Wl[n].shape for n in self.ss_names])
        Wsh = {n: _unblocked(p, Wl[n].shape, SMALL_SHARDED[n]) for n, p in zip(self.ss_names, sparts)}
        return Wfull, Wsh

    def rs_front(self, grads):
        pairs = [grads[n] if i is None else grads[n][i] for n, i in self.units]
        recv1 = exchange_sibling("rs_sibling", [p[1] for p in pairs])
        self.p1 = [add_sibling("add_sibling_" + self.tag(n, i), p[0], r, self.c_idx)
                   for (n, i), p, r in zip(self.units, pairs, recv1)]
        return [p[1] for p in self.p1]


def _rows_blocked(pair):
    return tuple(t.reshape(N_DEV, t.shape[0] // N_DEV, t.shape[1]) for t in pair)


def kernel(x, norm_mix, norm_mlp, norm_final, ssm_a_re, ssm_a_im, ssm_b_re, ssm_b_im, ssm_c_re, ssm_c_im, ssm_d, ssm_log_dt, ssm_w_glu, conv_w_pw1, conv_b_pw1, conv_w_dw, conv_b_dw, conv_ln_g, conv_ln_b, conv_w_pw2, conv_b_pw2, gmlp_w_in, gmlp_ln_g, gmlp_ln_b, gmlp_w_s, gmlp_b_s, gmlp_w_out, attn_w_qkv, attn_w_o, mlp_w_in, mlp_w_out, loss_target, m_norm_mix, m_norm_mlp, m_norm_final, m_ssm_a_re, m_ssm_a_im, m_ssm_b_re, m_ssm_b_im, m_ssm_c_re, m_ssm_c_im, m_ssm_d, m_ssm_log_dt, m_ssm_w_glu, m_conv_w_pw1, m_conv_b_pw1, m_conv_w_dw, m_conv_b_dw, m_conv_ln_g, m_conv_ln_b, m_conv_w_pw2, m_conv_b_pw2, m_gmlp_w_in, m_gmlp_ln_g, m_gmlp_ln_b, m_gmlp_w_s, m_gmlp_b_s, m_gmlp_w_out, m_attn_w_qkv, m_attn_w_o, m_mlp_w_in, m_mlp_w_out, v_norm_mix, v_norm_mlp, v_norm_final, v_ssm_a_re, v_ssm_a_im, v_ssm_b_re, v_ssm_b_im, v_ssm_c_re, v_ssm_c_im, v_ssm_d, v_ssm_log_dt, v_ssm_w_glu, v_conv_w_pw1, v_conv_b_pw1, v_conv_w_dw, v_conv_b_dw, v_conv_ln_g, v_conv_ln_b, v_conv_w_pw2, v_conv_b_pw2, v_gmlp_w_in, v_gmlp_ln_g, v_gmlp_ln_b, v_gmlp_w_s, v_gmlp_b_s, v_gmlp_w_out, v_attn_w_qkv, v_attn_w_o, v_mlp_w_in, v_mlp_w_out):
    args = (norm_mix, norm_mlp, norm_final, ssm_a_re, ssm_a_im, ssm_b_re, ssm_b_im, ssm_c_re, ssm_c_im, ssm_d,
            ssm_log_dt, ssm_w_glu, conv_w_pw1, conv_b_pw1, conv_w_dw, conv_b_dw, conv_ln_g, conv_ln_b, conv_w_pw2,
            conv_b_pw2, gmlp_w_in, gmlp_ln_g, gmlp_ln_b, gmlp_w_s, gmlp_b_s, gmlp_w_out, attn_w_qkv, attn_w_o,
            mlp_w_in, mlp_w_out)
    margs = (m_norm_mix, m_norm_mlp, m_norm_final, m_ssm_a_re, m_ssm_a_im, m_ssm_b_re, m_ssm_b_im, m_ssm_c_re,
             m_ssm_c_im, m_ssm_d, m_ssm_log_dt, m_ssm_w_glu, m_conv_w_pw1, m_conv_b_pw1, m_conv_w_dw, m_conv_b_dw,
             m_conv_ln_g, m_conv_ln_b, m_conv_w_pw2, m_conv_b_pw2, m_gmlp_w_in, m_gmlp_ln_g, m_gmlp_ln_b,
             m_gmlp_w_s, m_gmlp_b_s, m_gmlp_w_out, m_attn_w_qkv, m_attn_w_o, m_mlp_w_in, m_mlp_w_out)
    vargs = (v_norm_mix, v_norm_mlp, v_norm_final, v_ssm_a_re, v_ssm_a_im, v_ssm_b_re, v_ssm_b_im, v_ssm_c_re,
             v_ssm_c_im, v_ssm_d, v_ssm_log_dt, v_ssm_w_glu, v_conv_w_pw1, v_conv_b_pw1, v_conv_w_dw, v_conv_b_dw,
             v_conv_ln_g, v_conv_ln_b, v_conv_w_pw2, v_conv_b_pw2, v_gmlp_w_in, v_gmlp_ln_g, v_gmlp_ln_b,
             v_gmlp_w_s, v_gmlp_b_s, v_gmlp_w_out, v_attn_w_qkv, v_attn_w_o, v_mlp_w_in, v_mlp_w_out)
    Wl = dict(zip(WEIGHT_NAMES, args))
    Ml = dict(zip(WEIGHT_NAMES, margs))
    Vl = dict(zip(WEIGHT_NAMES, vargs))
    cx, cy, cc = _coords()
    my_idx = 4 * cx + 2 * cy + cc

    c_idx = cc.reshape(1).astype(jnp.int32)
    chip_idx = (2 * cx + cy).reshape(1).astype(jnp.int32)
    comm = _StepComm(Wl, c_idx)
    units, tag = comm.units, comm.tag
    W = {n: Wl[n] for n in SMALL if n not in SMALL_SHARDED}
    loss_local, grad_x, grads, (s5_disc_vjp, s5_cot) = _step(x, loss_target, W, comm)
    loss = lax.psum(loss_local, MESH_AXES)

    outs4 = {}
    for (n, i), p, r in zip(units, comm.p1, comm.recv2):
        w2, m2, v2 = (d[n].reshape(-1, d[n].shape[-1]) for d in (Wl, Ml, Vl))
        res = adam_big("adam_" + tag(n, i), p[0], r, w2, m2, v2, chip_idx, layer=0 if i is None else i)
        if i is None:
            outs4[n] = [t.reshape(Wl[n].shape) for t in res]
        else:
            outs4.setdefault(n, []).append(res)
    for n in BIG:
        if Wl[n].shape[0] == DEPTH:
            outs4[n] = [jnp.stack([layer[k] for layer in outs4[n]]) for k in range(4)]
    out_g = {n: outs4[n][0] for n in BIG}
    out_d = {n: outs4[n][1] for n in BIG}
    out_m = {n: outs4[n][2] for n in BIG}
    out_v = {n: outs4[n][3] for n in BIG}

    s5_lin = ['ssm_a_re', 'ssm_a_im', 'ssm_log_dt', 'ssm_b_re', 'ssm_b_im']
    direct = [n for n in SMALL if n not in s5_lin]
    def full_shape(n):
        shp = list(Wl[n].shape)
        if n in SMALL_SHARDED:
            shp[SMALL_SHARDED[n]] *= N_DEV
        return tuple(shp)
    small_parts = [grads[n].reshape(full_shape(n)) for n in direct] + list(s5_cot)
    gsum = sum8(all_gather("gather_small_grads", [_pack(small_parts)])[0])
    summed = _unpack(gsum, [p.shape for p in small_parts])
    gsmall = dict(zip(direct, summed[:len(direct)]))
    s5g = s5_disc_vjp(tuple(summed[len(direct):]))
    for n, gval in zip(s5_lin, s5g):
        gsmall[n] = gval[None]
    for n, ax in SMALL_SHARDED.items():
        gsmall[n] = lax.dynamic_slice_in_dim(gsmall[n], my_idx * Wl[n].shape[ax], Wl[n].shape[ax], axis=ax)
    sm_shapes = [Wl[n].shape for n in SMALL]
    dS, mS, vS = adam_small(_pack([Wl[n] for n in SMALL]), _pack([gsmall[n] for n in SMALL]),
                            _pack([Ml[n] for n in SMALL]), _pack([Vl[n] for n in SMALL]))
    for n, gval in zip(SMALL, [gsmall[n] for n in SMALL]):
        out_g[n] = gval.reshape(Wl[n].shape)
    out_d.update(zip(SMALL, _unpack(dS, sm_shapes)))
    out_m.update(zip(SMALL, _unpack(mS, sm_shapes)))
    out_v.update(zip(SMALL, _unpack(vS, sm_shapes)))

    return (loss, grad_x, *[out_g[n] for n in WEIGHT_NAMES], *[out_d[n] for n in WEIGHT_NAMES],
            *[out_m[n] for n in WEIGHT_NAMES], *[out_v[n] for n in WEIGHT_NAMES])


def _unpack_gathered(g, shard_shapes):
    flat = g.reshape(N_DEV, -1)
    out, off = [], 0
    for s in shard_shapes:
        n = math.prod(s)
        out.append(flat[:, off:off + n].reshape((N_DEV,) + tuple(s)))
        off += n
    return out
```

```python
import functools
import math

import jax
import jax.numpy as jnp
from jax import lax
from jax.experimental import pallas as pl
from jax.experimental.pallas import tpu as pltpu

F32 = jnp.float32
BF16 = jnp.bfloat16

D_MODEL = 1024
DEPTH = 4
EPS = 1e-6
SSM_GROUP = 16
SSM_GROUPS = 64
SSM_STATE = 64
S5_GB = 8
S5_NGB = SSM_GROUPS // S5_GB
S5_CH = S5_GB * SSM_GROUP
S5_ST = S5_GB * SSM_STATE
S5_L = 128
CONV_WIDTH = 31
CONV_PAD = 32
CONV_TS = 256
CONV_CW = 256
GMLP_CHUNK = 128
GMLP_HEADS = 4
ATT_CONFIGS = ((128, 1), (512, 4), (2048, 16))
ATT_HEADS = 8
HEAD_DIM = 64
ATT_BLK = 128
ATT_TB = 2
ATT_W = ATT_HEADS * HEAD_DIM
N_DEV = 8
ADAM_LR = 0.001
ADAM_B1 = 0.9
ADAM_B2 = 0.999
ADAM_EPS = 1e-08
ADAM_WD = 0.01
ADAM_STEP = 10
VMEM_LIMIT = 56 * 1024 * 1024
PACK_C = 1024
MESH_AXES = ("x", "y", "c")
MESH = pl.DeviceIdType.MESH

WEIGHT_NAMES = ['norm_mix', 'norm_mlp', 'norm_final', 'ssm_a_re', 'ssm_a_im', 'ssm_b_re', 'ssm_b_im',
                'ssm_c_re', 'ssm_c_im', 'ssm_d', 'ssm_log_dt', 'ssm_w_glu', 'conv_w_pw1', 'conv_b_pw1',
                'conv_w_dw', 'conv_b_dw', 'conv_ln_g', 'conv_ln_b', 'conv_w_pw2', 'conv_b_pw2',
                'gmlp_w_in', 'gmlp_ln_g', 'gmlp_ln_b', 'gmlp_w_s', 'gmlp_b_s', 'gmlp_w_out',
                'attn_w_qkv', 'attn_w_o', 'mlp_w_in', 'mlp_w_out']
BIG = {'ssm_w_glu': 2, 'conv_w_pw1': 2, 'conv_w_pw2': 1, 'gmlp_w_in': 2, 'gmlp_w_out': 1,
       'attn_w_qkv': 2, 'attn_w_o': 2, 'mlp_w_in': 2, 'mlp_w_out': 1}
SMALL_SHARDED = {'conv_b_pw1': 1, 'conv_w_dw': 2, 'conv_b_dw': 1, 'conv_ln_g': 1, 'conv_ln_b': 1,
                 'conv_b_pw2': 1, 'gmlp_ln_g': 1, 'gmlp_ln_b': 1}
SMALL = [n for n in WEIGHT_NAMES if n not in BIG]


def _cparams(sem=None):
    return pltpu.CompilerParams(dimension_semantics=sem, vmem_limit_bytes=VMEM_LIMIT)


def _dot(a, b):
    return jnp.dot(a, b, preferred_element_type=F32)


def _dot_nt(a, b):
    return lax.dot_general(a, b, (((1,), (1,)), ((), ())), preferred_element_type=F32)


def _dot_tn(a, b):
    return lax.dot_general(a, b, (((0,), (0,)), ((), ())), preferred_element_type=F32)


ROW_TILE_BYTES = 10 << 20


def _rows_for(T, row_bytes, cap=1024):
    tr = min(cap, T)
    while tr > 8 and (T % tr or tr * row_bytes > ROW_TILE_BYTES):
        tr //= 2
    assert T % tr == 0 and tr % 8 == 0
    return tr


def rowwise(name, fn, rows, params, row_out, acc_out=(), tr=None):
    T = rows[0].shape[0]
    row_bytes = (sum(r.shape[1] * r.dtype.itemsize for r in rows)
                 + sum(c * jnp.dtype(dt).itemsize for c, dt in row_out))
    tr = _rows_for(T, row_bytes, cap=tr or 1024)
    nr, npar, nro = len(rows), len(params), len(row_out)

    def body(*refs):
        ins = [r[...] for r in refs[:nr + npar]]
        outs = refs[nr + npar:]
        res = fn(*ins)
        if not isinstance(res, (tuple, list)):
            res = (res,)
        for k in range(nro):
            outs[k][...] = res[k].astype(outs[k].dtype)
        if acc_out:
            @pl.when(pl.program_id(0) == 0)
            def _():
                for k in range(nro, len(outs)):
                    outs[k][...] = jnp.zeros_like(outs[k])
            for k in range(nro, len(outs)):
                outs[k][...] += res[k].astype(outs[k].dtype)

    in_specs = [pl.BlockSpec((tr, r.shape[1]), lambda i: (i, 0)) for r in rows]
    in_specs += [pl.BlockSpec(p.shape, lambda i, nd=p.ndim: (0,) * nd) for p in params]
    out_shape = [jax.ShapeDtypeStruct((T, c), dt) for c, dt in row_out]
    out_specs = [pl.BlockSpec((tr, c), lambda i: (i, 0)) for c, dt in row_out]
    out_shape += [jax.ShapeDtypeStruct(s, dt) for s, dt in acc_out]
    out_specs += [pl.BlockSpec(s, lambda i, nd=len(s): (0,) * nd) for s, dt in acc_out]
    res = pl.pallas_call(body, grid=(T // tr,), in_specs=in_specs, out_specs=out_specs, out_shape=out_shape,
                         name=name, compiler_params=_cparams(("arbitrary",)))(*rows, *params)
    return res


def _tile_m(M, K):
    tm = 2048
    while tm > 256 and tm * K * 2 > (4 << 20):
        tm //= 2
    return min(tm, M)


def matmul(name, a, b, *, mode='nn', epi=None, extras=(), out_dtypes=(F32,), out3=False, whole_rows=False):
    M, K = a.shape
    if mode == 'cb':
        nblk, _, tn = b.shape
        N = nblk * tn
    else:
        N = b.shape[0] if mode == 'nt' else b.shape[1]
        tn = N if whole_rows else min(512, N)
    row_bytes = (K * 2 + sum(N * jnp.dtype(dt).itemsize for dt in out_dtypes)
                 + sum(N * arr.dtype.itemsize for arr, kind in extras if kind == 'tile'))
    tm = _rows_for(M, row_bytes)
    assert N % tn == 0, (M, N, tm, tn)
    nex = len(extras)

    def body(a_ref, b_ref, *rest):
        ex_refs, outs = rest[:nex], rest[nex:]
        av = a_ref[...]
        for c in range(N // tn):
            cs = slice(c * tn, (c + 1) * tn)
            if mode == 'cb':
                acc = _dot(av, b_ref[c])
            elif mode == 'nt':
                acc = _dot_nt(av, b_ref[cs, :])
            else:
                acc = _dot(av, b_ref[:, cs])
            res = epi(acc, *[e[:, cs] for e in ex_refs]) if epi is not None else (acc,)
            for o, r in zip(outs, res):
                if out3:
                    o[c] = r.astype(o.dtype)
                else:
                    o[:, cs] = r.astype(o.dtype)

    in_specs = [pl.BlockSpec((tm, K), lambda i: (i, 0)), pl.BlockSpec(b.shape, lambda i, nd=b.ndim: (0,) * nd)]
    for arr, kind in extras:
        in_specs.append(pl.BlockSpec((tm, N), lambda i: (i, 0)) if kind == 'tile'
                        else pl.BlockSpec((1, N), lambda i: (0, 0)))
    if out3:
        out_shape = [jax.ShapeDtypeStruct((N // tn, M, tn), dt) for dt in out_dtypes]
        out_specs = [pl.BlockSpec((N // tn, tm, tn), lambda i: (0, i, 0)) for dt in out_dtypes]
    else:
        out_shape = [jax.ShapeDtypeStruct((M, N), dt) for dt in out_dtypes]
        out_specs = [pl.BlockSpec((tm, N), lambda i: (i, 0)) for dt in out_dtypes]
    return pl.pallas_call(body, grid=(M // tm,), in_specs=in_specs, out_specs=out_specs,
                          out_shape=out_shape, name=name,
                          compiler_params=_cparams(("arbitrary",)))(a, b, *[e[0] for e in extras])


def matmul_nt_cb(name, a, b, *, a3=False, epi=None, extras=(), params=(), out_dtypes=(F32,), acc_out=()):
    nblk, K, n = b.shape
    M = a.shape[1] if a3 else a.shape[0]
    tm = _tile_m(M, nblk * n)
    assert M % tm == 0
    nex, npar, nro = len(extras), len(params), len(out_dtypes)

    def body(a_ref, b_ref, *rest):
        ex, outs = rest[:nex + npar], rest[nex + npar:]
        acc = None
        for j in range(nblk):
            aj = a_ref[j] if a3 else a_ref[:, j * n:(j + 1) * n]
            part = _dot_nt(aj, b_ref[j])
            acc = part if acc is None else acc + part
        res = epi(acc, *[e[...] for e in ex]) if epi is not None else (acc,)
        for o, r in zip(outs[:nro], res[:nro]):
            o[...] = r.astype(o.dtype)
        if acc_out:
            @pl.when(pl.program_id(0) == 0)
            def _():
                for o in outs[nro:]:
                    o[...] = jnp.zeros_like(o)
            for o, r in zip(outs[nro:], res[nro:]):
                o[...] += r.astype(o.dtype)

    a_spec = (pl.BlockSpec((nblk, tm, n), lambda i: (0, i, 0)) if a3
              else pl.BlockSpec((tm, nblk * n), lambda i: (i, 0)))
    row = pl.BlockSpec((tm, K), lambda i: (i, 0))
    const = lambda shp: pl.BlockSpec(shp, lambda i, nd=len(shp): (0,) * nd)
    return pl.pallas_call(
        body, grid=(M // tm,),
        in_specs=[a_spec, pl.BlockSpec((nblk, K, n), lambda i: (0, 0, 0))] + [row] * nex
        + [const(p.shape) for p in params],
        out_specs=[row] * nro + [const(s) for s, dt in acc_out],
        out_shape=[jax.ShapeDtypeStruct((M, K), dt) for dt in out_dtypes]
        + [jax.ShapeDtypeStruct(s, dt) for s, dt in acc_out],
        name=name, compiler_params=_cparams(("arbitrary",)))(a, b, *extras, *params)


def wgrad(name, a, g, *, cb=False, g3=False):
    M, K = a.shape
    tm, tk = min(M, 1024), min(K, 1024)
    if cb:
        n = g.shape[2] if g3 else g.shape[1] // N_DEV
        nj = N_DEV
        while nj > 1 and nj * tk * n * 6 > (14 << 20):
            nj //= 2
        grid = (K // tk, N_DEV // nj, M // tm)
        g_spec = (pl.BlockSpec((nj, tm, n), lambda k, j, m: (j, m, 0)) if g3
                  else pl.BlockSpec((tm, nj * n), lambda k, j, m: (m, j)))
        o_spec = pl.BlockSpec((nj, tk, n), lambda k, j, m: (j, k, 0))
        o_shape = (N_DEV, K, n)
    else:
        N = g.shape[1]
        tn = min(N, 1024)
        nj = 1
        grid = (K // tk, N // tn, M // tm)
        g_spec = pl.BlockSpec((tm, tn), lambda k, j, m: (m, j))
        o_spec = pl.BlockSpec((tk, tn), lambda k, j, m: (k, j))
        o_shape = (K, N)
    nm = M // tm

    def body(a_ref, g_ref, o_ref, o16_ref):
        m = pl.program_id(2)

        @pl.when(m == 0)
        def _():
            o_ref[...] = jnp.zeros_like(o_ref)
        at = a_ref[...].T
        if cb:
            for jj in range(nj):
                gj = g_ref[jj] if g3 else g_ref[:, jj * n:(jj + 1) * n]
                o_ref[jj] += _dot(at, gj)
        else:
            o_ref[...] += _dot(at, g_ref[...])

        @pl.when(m == nm - 1)
        def _():
            o16_ref[...] = o_ref[...].astype(BF16)

    return pl.pallas_call(
        body, grid=grid, in_specs=[pl.BlockSpec((tm, tk), lambda k, j, m: (m, k)), g_spec],
        out_specs=[o_spec, o_spec],
        out_shape=[jax.ShapeDtypeStruct(o_shape, F32), jax.ShapeDtypeStruct(o_shape, BF16)], name=name,
        compiler_params=_cparams(("arbitrary", "arbitrary", "arbitrary")))(a, g)


def _rms(x, g):
    x = x.astype(F32)
    return x * lax.rsqrt(jnp.mean(x * x, axis=-1, keepdims=True) + EPS) * g


def _ln(x, g, b):
    mu = jnp.mean(x, axis=-1, keepdims=True)
    var = jnp.mean(jnp.square(x - mu), axis=-1, keepdims=True)
    return (x - mu) * lax.rsqrt(var + EPS) * g + b


def _glu(z):
    d = z.shape[1] // 2
    return z[:, :d] * jax.nn.sigmoid(z[:, d:])


def _glu_bwd(z, dy):
    d = z.shape[1] // 2
    a, s = z[:, :d], jax.nn.sigmoid(z[:, d:])
    return jnp.concatenate([dy * s, dy * a * s * (1.0 - s)], axis=1)


def _colsum(v):
    return jnp.sum(v.astype(F32), axis=0, keepdims=True)


def rms_fwd(name, x, g, want_f32=False):
    def fn(xt, gt):
        h = _rms(xt, gt)
        return (h, h) if want_f32 else (h,)
    D = x.shape[1]
    outs = [(D, BF16)] + ([(D, F32)] if want_f32 else [])
    return rowwise(name, fn, [x], [g], outs)


def rms_bwd(name, x, dh, dres, g):
    def fn(xt, dht, drt, gt):
        _, vjp = jax.vjp(_rms, xt, gt)
        dx, dg = vjp(dht.astype(F32))
        dx = dx + drt
        return dx, dx, dg
    D = x.shape[1]
    return rowwise(name, fn, [x, dh, dres], [g], [(D, F32), (D, BF16)], [((1, D), F32)])


def s5_disc(a_re, a_im, log_dt, b_re, b_im):
    dt = jnp.exp(log_dt)[:, None]
    er = jnp.exp(a_re * dt)
    lam_re = er * jnp.cos(a_im * dt)
    lam_im = er * jnp.sin(a_im * dt)
    nr, ni = lam_re - 1.0, lam_im
    den = a_re * a_re + a_im * a_im
    f_re = (nr * a_re + ni * a_im) / den
    f_im = (ni * a_re - nr * a_im) / den
    bb_re = f_re[..., None] * b_re - f_im[..., None] * b_im
    bb_im = f_re[..., None] * b_im + f_im[..., None] * b_re
    return lam_re, lam_im, bb_re, bb_im


def _s5_blockdiag_b(bb):
    t = bb.reshape(S5_NGB, S5_GB, SSM_STATE, SSM_GROUP).transpose(0, 1, 3, 2)
    eye = jnp.eye(S5_GB, dtype=bb.dtype)
    return jnp.einsum('bgpn,gh->bgphn', t, eye).reshape(S5_NGB, S5_CH, S5_ST)


def _s5_blockdiag_b_inv(x):
    t = x.reshape(S5_NGB, S5_GB, SSM_GROUP, S5_GB, SSM_STATE)
    eye = jnp.eye(S5_GB, dtype=x.dtype)
    d = jnp.einsum('bgphn,gh->bgpn', t, eye)
    return d.transpose(0, 1, 3, 2).reshape(SSM_GROUPS, SSM_STATE, SSM_GROUP)


def _s5_blockdiag_c(c):
    t = c.reshape(S5_NGB, S5_GB, SSM_GROUP, SSM_STATE).transpose(0, 1, 3, 2)
    eye = jnp.eye(S5_GB, dtype=c.dtype)
    return jnp.einsum('bgnp,gh->bgnhp', t, eye).reshape(S5_NGB, S5_ST, S5_CH)


def _s5_blockdiag_c_inv(x):
    t = x.reshape(S5_NGB, S5_GB, SSM_STATE, S5_GB, SSM_GROUP)
    eye = jnp.eye(S5_GB, dtype=x.dtype)
    d = jnp.einsum('bgnhp,gh->bgnp', t, eye)
    return d.transpose(0, 1, 3, 2).reshape(SSM_GROUPS, SSM_GROUP, SSM_STATE)


def s5_tables(lam_re, lam_im, L):
    pr, pi = lam_re.reshape(1, -1), lam_im.reshape(1, -1)
    n = 1
    while n < L:
        lr, li = pr[n - 1:n], pi[n - 1:n]
        pr, pi = (jnp.concatenate([pr, pr * lr - pi * li], 0), jnp.concatenate([pi, pr * li + pi * lr], 0))
        n *= 2
    nk = int(math.log2(L))
    idx = [2 ** k - 1 for k in range(nk)] + [0] * (8 - nk)

    def blk(t):
        return t.reshape(t.shape[0], S5_NGB, S5_ST).transpose(1, 0, 2)

    def rows(t):
        return jnp.concatenate([t[j:j + 1] for j in idx], axis=0)
    return blk(pr), blk(pi), blk(rows(pr)), blk(rows(pi))


S5_SUB = 8


def _scan_tiles(br, bi, a2r, a2i, reverse):
    L = br.shape[0]
    sub = lax.broadcasted_iota(jnp.int32, br.shape, 0) & (S5_SUB - 1)
    xr, xi = br, bi
    for k in range(3):
        s = 1 << k
        ar, ai = a2r[k:k + 1, :], a2i[k:k + 1, :]
        if reverse:
            sr, si = pltpu.roll(xr, L - s, 0), pltpu.roll(xi, L - s, 0)
            m = sub < S5_SUB - s
        else:
            sr, si = pltpu.roll(xr, s, 0), pltpu.roll(xi, s, 0)
            m = sub >= s
        sr, si = jnp.where(m, sr, 0.0), jnp.where(m, si, 0.0)
        xr, xi = xr + ar * sr - ai * si, xi + ar * si + ai * sr
    return xr, xi


def _scan_chain(xr, xi, pr, pi, cr, ci, out_r, out_i, reverse):
    ntile = xr.shape[0] // S5_SUB
    for g in (reversed(range(ntile)) if reverse else range(ntile)):
        rs = slice(g * S5_SUB, (g + 1) * S5_SUB)
        if reverse:
            nr = xr[rs] + pr * cr + pi * ci
            ni = xi[rs] + pr * ci - pi * cr
            cr, ci = nr[0:1], ni[0:1]
        else:
            nr = xr[rs] + pr * cr - pi * ci
            ni = xi[rs] + pr * ci + pi * cr
            cr, ci = nr[S5_SUB - 1:S5_SUB], ni[S5_SUB - 1:S5_SUB]
        out_r[rs, :] = nr
        out_i[rs, :] = ni
    return cr, ci


def _grid_step(shape):
    s = 0
    for ax, n in enumerate(shape):
        s = s * n + pl.program_id(ax)
    return s


def s5_fwd(h, bre, bim, cre, cim, pwr, pwi, l2r, l2i, dskip, bsz, gather=()):
    T, D = h.shape
    L = S5_L
    S = T // bsz
    NC = S // L
    ng = len(gather)
    grid = (S5_NGB, bsz, NC)
    nsteps = S5_NGB * bsz * NC
    fwd_step = nsteps - max(1, nsteps // 32)

    def body(*refs):
        (h_ref, bre_ref, bim_ref, cre_ref, cim_ref, pwr_ref, pwi_ref, l2r_ref, l2i_ref, d_ref) = refs[:10]
        x_refs = refs[10:10 + ng]
        y_ref, gy_ref, xs_ref, xr_s, xi_s = refs[10 + ng:15 + ng]
        g_refs = refs[15 + ng:15 + 2 * ng]
        car_r, car_i = refs[15 + 2 * ng:17 + 2 * ng]
        if ng:
            start, forward, finish = _gather_phases(x_refs, g_refs, *refs[17 + 2 * ng:])
            step = _grid_step(grid)
            pl.when(step == 0)(start)
            pl.when(step == fwd_step)(forward)

        @pl.when(pl.program_id(2) == 0)
        def _():
            car_r[...] = jnp.zeros_like(car_r)
            car_i[...] = jnp.zeros_like(car_i)
        u = h_ref[...]
        ub = u.astype(BF16)
        cr, ci = car_r[0:1, :], car_i[0:1, :]
        xs_ref[...] = jnp.zeros_like(xs_ref)
        xs_ref[0:1, :] = cr
        xs_ref[1:2, :] = ci
        xr, xi = _scan_tiles(_dot(ub, bre_ref[...]), _dot(ub, bim_ref[...]), l2r_ref[...], l2i_ref[...], False)
        cr, ci = _scan_chain(xr, xi, pwr_ref[...], pwi_ref[...], cr, ci, xr_s, xi_s, False)
        car_r[...] = jnp.broadcast_to(cr, car_r.shape)
        car_i[...] = jnp.broadcast_to(ci, car_i.shape)
        y = (_dot(xr_s[...].astype(BF16), cre_ref[...]) - _dot(xi_s[...].astype(BF16), cim_ref[...])
             + d_ref[...] * u)
        y_ref[...] = y
        gy_ref[...] = jax.nn.gelu(y).astype(BF16)
        if ng:
            pl.when(step == nsteps - 1)(finish)

    tok = lambda g, b, c: (b * NC + c, g)
    par = lambda g, b, c: (g, 0, 0)
    anyspec = pl.BlockSpec(memory_space=pl.ANY)
    return pl.pallas_call(
        body, grid=grid,
        in_specs=[pl.BlockSpec((L, S5_CH), tok),
                  pl.BlockSpec((None, S5_CH, S5_ST), par), pl.BlockSpec((None, S5_CH, S5_ST), par),
                  pl.BlockSpec((None, S5_ST, S5_CH), par), pl.BlockSpec((None, S5_ST, S5_CH), par),
                  pl.BlockSpec((None, 8, S5_ST), par), pl.BlockSpec((None, 8, S5_ST), par),
                  pl.BlockSpec((None, 8, S5_ST), par), pl.BlockSpec((None, 8, S5_ST), par),
                  pl.BlockSpec((1, S5_CH), lambda g, b, c: (0, g))] + [anyspec] * ng,
        out_specs=[pl.BlockSpec((L, S5_CH), tok), pl.BlockSpec((L, S5_CH), tok),
                   pl.BlockSpec((None, 8, S5_ST), lambda g, b, c: (b * NC + c, 0, g)),
                   pl.BlockSpec((L, S5_ST), tok), pl.BlockSpec((L, S5_ST), tok)] + [anyspec] * ng,
        out_shape=[jax.ShapeDtypeStruct((T, D), F32), jax.ShapeDtypeStruct((T, D), BF16),
                   jax.ShapeDtypeStruct((bsz * NC, 8, S5_NGB * S5_ST), F32),
                   jax.ShapeDtypeStruct((T, S5_NGB * S5_ST), F32), jax.ShapeDtypeStruct((T, S5_NGB * S5_ST), F32)]
        + _gather_out_shapes(gather),
        scratch_shapes=[pltpu.VMEM((8, S5_ST), F32), pltpu.VMEM((8, S5_ST), F32)]
        + (_gather_sems(ng) if ng else []),
        name="s5_fwd", compiler_params=_cparams(("arbitrary", "arbitrary", "arbitrary")),
    )(h, bre, bim, cre, cim, pwr, pwi, l2r, l2i, dskip, *gather)


def s5_bwd(h, dy, xs, xr, xi, bre, bim, cre, cim, pwr_rev, pwi_rev, l2r, l2i, dskip, bsz, chips=()):
    T, D = h.shape
    L = S5_L
    S = T // bsz
    NC = S // L
    nc = len(chips)
    grid = (S5_NGB, bsz, NC)
    nsteps = S5_NGB * bsz * NC

    def body(*refs):
        (h_ref, dy_ref, xs_ref, xr_ref, xi_ref, bre_ref, bim_ref, cre_ref, cim_ref, qr_ref, qi_ref,
         l2r_ref, l2i_ref, d_ref) = refs[:14]
        p_refs = refs[14:14 + nc]
        du_ref, dbr_ref, dbi_ref, dcr_ref, dci_ref, dl_ref, dd_ref = refs[14 + nc:21 + nc]
        r_refs = refs[21 + nc:21 + 2 * nc]
        car_r, car_i, dr_s, di_s = refs[21 + 2 * nc:25 + 2 * nc]
        if nc:
            start, finish = _chips_phases(p_refs, r_refs, *refs[25 + 2 * nc:])
            step = _grid_step(grid)
            pl.when(step == 0)(start)
        first = (pl.program_id(1) == 0) & (pl.program_id(2) == 0)

        @pl.when(first)
        def _():
            for r in (dbr_ref, dbi_ref, dcr_ref, dci_ref, dl_ref, dd_ref):
                r[...] = jnp.zeros_like(r)

        @pl.when(pl.program_id(2) == 0)
        def _():
            car_r[...] = jnp.zeros_like(car_r)
            car_i[...] = jnp.zeros_like(car_i)

        u = h_ref[...]
        ub = u.astype(BF16)
        dyv = dy_ref[...]
        dyb = dyv.astype(BF16)
        l2r_v, l2i_v = l2r_ref[...], l2i_ref[...]
        x0r, x0i = xs_ref[0:1, :], xs_ref[1:2, :]
        xr, xi = xr_ref[...], xi_ref[...]
        gr = _dot_nt(dyb, cre_ref[...])
        gi = -_dot_nt(dyb, cim_ref[...])
        dr, di = _scan_tiles(gr, gi, l2r_v, -l2i_v, True)
        cr, ci = _scan_chain(dr, di, qr_ref[...], qi_ref[...], car_r[0:1, :], car_i[0:1, :], dr_s, di_s, True)
        dr, di = dr_s[...], di_s[...]
        car_r[...] = jnp.broadcast_to(cr, car_r.shape)
        car_i[...] = jnp.broadcast_to(ci, car_i.shape)
        row = lax.broadcasted_iota(jnp.int32, xr.shape, 0)
        xpr = jnp.where(row >= 1, pltpu.roll(xr, 1, 0), x0r)
        xpi = jnp.where(row >= 1, pltpu.roll(xi, 1, 0), x0i)
        dl_ref[0:1, :] += _colsum(dr * xpr + di * xpi)
        dl_ref[1:2, :] += _colsum(di * xpr - dr * xpi)
        drb, dib = dr.astype(BF16), di.astype(BF16)
        dcr_ref[...] += _dot_tn(xr.astype(BF16), dyb)
        dci_ref[...] -= _dot_tn(xi.astype(BF16), dyb)
        dbr_ref[...] += _dot_tn(ub, drb)
        dbi_ref[...] += _dot_tn(ub, dib)
        du_ref[...] = _dot_nt(drb, bre_ref[...]) + _dot_nt(dib, bim_ref[...]) + d_ref[...] * dyv
        dd_ref[0:1, :] += _colsum(dyv * u)
        if nc:
            pl.when(step == nsteps - 1)(finish)

    tok = lambda g, b, c: (b * NC + (NC - 1 - c), g)
    par = lambda g, b, c: (g, 0, 0)
    anyspec = pl.BlockSpec(memory_space=pl.ANY)
    return pl.pallas_call(
        body, grid=grid,
        in_specs=[pl.BlockSpec((L, S5_CH), tok), pl.BlockSpec((L, S5_CH), tok),
                  pl.BlockSpec((None, 8, S5_ST), lambda g, b, c: (b * NC + (NC - 1 - c), 0, g)),
                  pl.BlockSpec((L, S5_ST), tok), pl.BlockSpec((L, S5_ST), tok),
                  pl.BlockSpec((None, S5_CH, S5_ST), par), pl.BlockSpec((None, S5_CH, S5_ST), par),
                  pl.BlockSpec((None, S5_ST, S5_CH), par), pl.BlockSpec((None, S5_ST, S5_CH), par),
                  pl.BlockSpec((None, 8, S5_ST), par), pl.BlockSpec((None, 8, S5_ST), par),
                  pl.BlockSpec((None, 8, S5_ST), par), pl.BlockSpec((None, 8, S5_ST), par),
                  pl.BlockSpec((1, S5_CH), lambda g, b, c: (0, g))] + [anyspec] * nc,
        out_specs=[pl.BlockSpec((L, S5_CH), tok),
                   pl.BlockSpec((None, S5_CH, S5_ST), par), pl.BlockSpec((None, S5_CH, S5_ST), par),
                   pl.BlockSpec((None, S5_ST, S5_CH), par), pl.BlockSpec((None, S5_ST, S5_CH), par),
                   pl.BlockSpec((None, 8, S5_ST), par),
                   pl.BlockSpec((8, S5_CH), lambda g, b, c: (0, g))] + [anyspec] * nc,
        out_shape=[jax.ShapeDtypeStruct((T, D), F32),
                   jax.ShapeDtypeStruct((S5_NGB, S5_CH, S5_ST), F32), jax.ShapeDtypeStruct((S5_NGB, S5_CH, S5_ST), F32),
                   jax.ShapeDtypeStruct((S5_NGB, S5_ST, S5_CH), F32), jax.ShapeDtypeStruct((S5_NGB, S5_ST, S5_CH), F32),
                   jax.ShapeDtypeStruct((S5_NGB, 8, S5_ST), F32), jax.ShapeDtypeStruct((8, D), F32)]
        + _chips_out_shapes(chips),
        scratch_shapes=[pltpu.VMEM((8, S5_ST), F32), pltpu.VMEM((8, S5_ST), F32)]
        + [pltpu.VMEM((L, S5_ST), F32)] * 2 + (_chips_sems(nc) if nc else []),
        name="s5_bwd", compiler_params=_cparams(("arbitrary", "arbitrary", "arbitrary")),
    )(h, dy, xs, xr, xi, bre, bim, cre, cim, pwr_rev, pwi_rev, l2r, l2i, dskip, *chips)


def _shift_rows(win, off, n):
    if off == 0:
        return win[:n]
    return pltpu.roll(win, win.shape[0] - off, 0)[:n]


def dwconv_fwd(z, w, b, bsz):
    T, D = z.shape
    S = T // bsz
    TS, CW, PAD = CONV_TS, CONV_CW, CONV_PAD

    def body(z_ref, w_ref, b_ref, y_ref, zp):
        zp[0:PAD, :] = jnp.zeros((PAD, CW), F32)
        zp[PAD:, :] = z_ref[...]
        wv, bv = w_ref[...], b_ref[...]

        def step(t, carry):
            base = pl.multiple_of(t * TS, TS)
            win = zp[pl.ds(base, TS + PAD), :]
            acc = jnp.zeros((TS, CW), F32) + bv
            for k in range(CONV_WIDTH):
                acc = acc + wv[k:k + 1, :] * _shift_rows(win, PAD - (CONV_WIDTH - 1) + k, TS)
            y_ref[pl.ds(base, TS), :] = acc
            return carry
        lax.fori_loop(0, S // TS, step, 0)

    return pl.pallas_call(
        body, grid=(D // CW, bsz),
        in_specs=[pl.BlockSpec((S, CW), lambda c, bb: (bb, c)), pl.BlockSpec((32, CW), lambda c, bb: (0, c)),
                  pl.BlockSpec((1, CW), lambda c, bb: (0, c))],
        out_specs=pl.BlockSpec((S, CW), lambda c, bb: (bb, c)),
        out_shape=jax.ShapeDtypeStruct((T, D), F32),
        scratch_shapes=[pltpu.VMEM((S + PAD, CW), F32)],
        name="dwconv_fwd", compiler_params=_cparams(("arbitrary", "arbitrary")),
    )(z, w, b)


def dwconv_bwd(z, dy, w, bsz):
    T, D = z.shape
    S = T // bsz
    TS, CW, PAD = CONV_TS, CONV_CW, CONV_PAD

    def body(z_ref, dy_ref, w_ref, dz_ref, dw_ref, db_ref, zp, dyp):
        @pl.when(pl.program_id(1) == 0)
        def _():
            dw_ref[...] = jnp.zeros_like(dw_ref)
            db_ref[...] = jnp.zeros_like(db_ref)
        zp[0:PAD, :] = jnp.zeros((PAD, CW), F32)
        zp[PAD:, :] = z_ref[...]
        dyp[0:S, :] = dy_ref[...]
        dyp[S:, :] = jnp.zeros((PAD, CW), F32)
        wv = w_ref[...]

        def step(t, carry):
            base = pl.multiple_of(t * TS, TS)
            zwin = zp[pl.ds(base, TS + PAD), :]
            dwin = dyp[pl.ds(base, TS + PAD), :]
            dyt = dwin[:TS]
            acc = jnp.zeros((TS, CW), F32)
            for j in range(CONV_WIDTH):
                k = CONV_WIDTH - 1 - j
                acc = acc + wv[k:k + 1, :] * _shift_rows(dwin, j, TS)
            dz_ref[pl.ds(base, TS), :] = acc
            for k in range(CONV_WIDTH):
                prod = dyt * _shift_rows(zwin, PAD - (CONV_WIDTH - 1) + k, TS)
                dw_ref[8 * k:8 * k + 8, :] += jnp.sum(prod.reshape(TS // 8, 8, CW), axis=0)
            db_ref[...] += jnp.sum(dyt.reshape(TS // 8, 8, CW), axis=0)
            return carry
        lax.fori_loop(0, S // TS, step, 0)

    dz, dw, db = pl.pallas_call(
        body, grid=(D // CW, bsz),
        in_specs=[pl.BlockSpec((S, CW), lambda c, bb: (bb, c)), pl.BlockSpec((S, CW), lambda c, bb: (bb, c)),
                  pl.BlockSpec((32, CW), lambda c, bb: (0, c))],
        out_specs=[pl.BlockSpec((S, CW), lambda c, bb: (bb, c)), pl.BlockSpec((8 * 32, CW), lambda c, bb: (0, c)),
                   pl.BlockSpec((8, CW), lambda c, bb: (0, c))],
        out_shape=[jax.ShapeDtypeStruct((T, D), F32), jax.ShapeDtypeStruct((8 * 32, D), F32),
                   jax.ShapeDtypeStruct((8, D), F32)],
        scratch_shapes=[pltpu.VMEM((S + PAD, CW), F32), pltpu.VMEM((S + PAD, CW), F32)],
        name="dwconv_bwd", compiler_params=_cparams(("arbitrary", "arbitrary")),
    )(z, dy, w)
    return dz, dw.reshape(32, 8, D).sum(axis=1), db.sum(axis=0, keepdims=True)


def spatial_fwd(u, vln, ws, bias):
    T, E = u.shape
    C, H = GMLP_CHUNK, GMLP_HEADS
    hw = E // H

    def body(u_ref, v_ref, ws_ref, b_ref, o_ref):
        for hh in range(H):
            sl = slice(hh * hw, (hh + 1) * hw)
            vp = _dot(ws_ref[hh], v_ref[:, sl]) + b_ref[:, sl]
            o_ref[:, sl] = (u_ref[:, sl] * vp).astype(o_ref.dtype)

    return pl.pallas_call(
        body, grid=(T // C,),
        in_specs=[pl.BlockSpec((C, E), lambda i: (i, 0)), pl.BlockSpec((C, E), lambda i: (i, 0)),
                  pl.BlockSpec((H, C, C), lambda i: (0, 0, 0)), pl.BlockSpec((C, E), lambda i: (0, 0))],
        out_specs=pl.BlockSpec((C, E), lambda i: (i, 0)),
        out_shape=jax.ShapeDtypeStruct((T, E), BF16),
        name="spatial_fwd", compiler_params=_cparams(("arbitrary",)),
    )(u, vln, ws, bias)


def spatial_bwd(u, vln, dg, ws, bias):
    T, E = u.shape
    C, H = GMLP_CHUNK, GMLP_HEADS
    hw = E // H

    def body(u_ref, v_ref, dg_ref, ws_ref, b_ref, du_ref, dv_ref, dws_ref, db_ref):
        @pl.when(pl.program_id(0) == 0)
        def _():
            dws_ref[...] = jnp.zeros_like(dws_ref)
            db_ref[...] = jnp.zeros_like(db_ref)
        tril = (lax.broadcasted_iota(jnp.int32, (C, C), 1) <= lax.broadcasted_iota(jnp.int32, (C, C), 0))
        for hh in range(H):
            sl = slice(hh * hw, (hh + 1) * hw)
            v = v_ref[:, sl]
            w = ws_ref[hh]
            dgv = dg_ref[:, sl].astype(F32)
            vp = _dot(w, v) + b_ref[:, sl]
            du_ref[:, sl] = dgv * vp
            dvp = dgv * u_ref[:, sl]
            dvpb = dvp.astype(BF16)
            dv_ref[:, sl] = _dot_tn(w, dvpb)
            dws_ref[hh] += jnp.where(tril, _dot_nt(dvpb, v), 0.0)
            db_ref[:, sl] += dvp

    return pl.pallas_call(
        body, grid=(T // C,),
        in_specs=[pl.BlockSpec((C, E), lambda i: (i, 0)), pl.BlockSpec((C, E), lambda i: (i, 0)),
                  pl.BlockSpec((C, E), lambda i: (i, 0)),
                  pl.BlockSpec((H, C, C), lambda i: (0, 0, 0)), pl.BlockSpec((C, E), lambda i: (0, 0))],
        out_specs=[pl.BlockSpec((C, E), lambda i: (i, 0)), pl.BlockSpec((C, E), lambda i: (i, 0)),
                   pl.BlockSpec((H, C, C), lambda i: (0, 0, 0)), pl.BlockSpec((C, E), lambda i: (0, 0))],
        out_shape=[jax.ShapeDtypeStruct((T, E), F32), jax.ShapeDtypeStruct((T, E), F32),
                   jax.ShapeDtypeStruct((H, C, C), F32), jax.ShapeDtypeStruct((C, E), F32)],
        name="spatial_bwd", compiler_params=_cparams(("arbitrary",)),
    )(u, vln, dg, ws, bias)


def _att_masks():
    r = lax.broadcasted_iota(jnp.int32, (ATT_BLK, ATT_BLK), 0)
    c = lax.broadcasted_iota(jnp.int32, (ATT_BLK, ATT_BLK), 1)
    return c <= r, c >= r


NEG = -1e30
ATT_SCALE = HEAD_DIM ** -0.5


def _att_view(t, dil):
    return t.reshape(t.shape[0] // dil, dil * t.shape[1])


def attn_fwd(name, q, k, v, dil, bsz):
    T, Wd = q.shape
    nb = T // (bsz * dil * ATT_BLK)
    TB = min(nb, ATT_TB)
    nsteps = nb // TB

    def body(q_ref, k_ref, v_ref, kp_ref, vp_ref, o_ref, l_ref):
        n = pl.program_id(2)
        mc, mp = _att_masks()
        for j in range(TB):
            rows = slice(j * ATT_BLK, (j + 1) * ATT_BLK)
            prow = slice((j - 1) * ATT_BLK, j * ATT_BLK)
            hp = (n * TB + j) > 0
            H = range(ATT_HEADS)
            ls = [slice(hh * HEAD_DIM, (hh + 1) * HEAD_DIM) for hh in H]
            qj = [q_ref[rows, ls[hh]] for hh in H]
            kc = [k_ref[rows, ls[hh]] for hh in H]
            kp = [k_ref[prow, ls[hh]] if j > 0 else kp_ref[:, ls[hh]] for hh in H]
            sc = [jnp.where(mc, _dot_nt(qj[hh], kc[hh]) * ATT_SCALE, NEG) for hh in H]
            sp = [jnp.where(mp & hp, _dot_nt(qj[hh], kp[hh]) * ATT_SCALE, NEG) for hh in H]
            m = [jnp.maximum(jnp.max(sc[hh], axis=1, keepdims=True), jnp.max(sp[hh], axis=1, keepdims=True))
                 for hh in H]
            pc = [jnp.exp(sc[hh] - m[hh]) for hh in H]
            pp = [jnp.exp(sp[hh] - m[hh]) for hh in H]
            l = [jnp.sum(pc[hh], axis=1, keepdims=True) + jnp.sum(pp[hh], axis=1, keepdims=True) for hh in H]
            vc = [v_ref[rows, ls[hh]] for hh in H]
            vp = [v_ref[prow, ls[hh]] if j > 0 else vp_ref[:, ls[hh]] for hh in H]
            for hh in H:
                o_ref[rows, ls[hh]] = (_dot(pc[hh].astype(BF16), vc[hh]) + _dot(pp[hh].astype(BF16), vp[hh])) / l[hh]
                l_ref[rows, ls[hh]] = jnp.broadcast_to(m[hh] + jnp.log(l[hh]), (ATT_BLK, HEAD_DIM))

    blk = pl.BlockSpec((TB * ATT_BLK, Wd), lambda b, r, n: (b * nsteps + n, r))
    prev = pl.BlockSpec((ATT_BLK, Wd), lambda b, r, n: (jnp.maximum(b * nb + n * TB - 1, 0), r))
    qv, kv, vv = (_att_view(t, dil) for t in (q, k, v))
    o, l = pl.pallas_call(
        body, grid=(bsz, dil, nsteps), in_specs=[blk, blk, blk, prev, prev], out_specs=[blk, blk],
        out_shape=[jax.ShapeDtypeStruct(qv.shape, F32), jax.ShapeDtypeStruct(qv.shape, F32)],
        name=name, compiler_params=_cparams(("arbitrary", "arbitrary", "arbitrary")),
    )(qv, kv, vv, kv, vv)
    return o.reshape(T, Wd), l.reshape(T, Wd)


def attn_bwd(name, q, k, v, do, mg, lse, dil, bsz):
    T, Wd = q.shape
    nb = T // (bsz * dil * ATT_BLK)
    TB = min(nb, ATT_TB)
    nsteps = nb // TB

    def body(q_ref, k_ref, v_ref, do_ref, mg_ref, l_ref, kp_ref, vp_ref, qn_ref, don_ref, mgn_ref, ln_ref,
             dq_ref, dk_ref, dv_ref):
        n = pl.program_id(2)
        mc, mp = _att_masks()

        def probs_all(qs, ks, lse_cols, mask):
            s = [_dot_nt(qh, kh) * ATT_SCALE for qh, kh in zip(qs, ks)]
            return [jnp.where(mask, jnp.exp(sh - lc), 0.0) for sh, lc in zip(s, lse_cols)]

        def ds_all(ps, dos, vs, deltas):
            dp = [_dot_nt(dh, vh) for dh, vh in zip(dos, vs)]
            return [(ph * (dph - dl) * ATT_SCALE).astype(BF16) for ph, dph, dl in zip(ps, dp, deltas)]

        H = range(ATT_HEADS)
        ls = [slice(hh * HEAD_DIM, (hh + 1) * HEAD_DIM) for hh in H]
        dk = [[None] * TB for _ in H]
        dv = [[None] * TB for _ in H]
        for j in range(TB + 1):
            rows = slice(j * ATT_BLK, (j + 1) * ATT_BLK)
            prow = slice((j - 1) * ATT_BLK, j * ATT_BLK)
            if j < TB:
                srcs = (q_ref, do_ref, mg_ref, l_ref)
                qj, doj, mgj, lj = ([r[rows, ls[hh]] for hh in H] for r in srcs)
                hp = (n * TB + j) > 0
            else:
                srcs = (qn_ref, don_ref, mgn_ref, ln_ref)
                qj, doj, mgj, lj = ([r[:, ls[hh]] for hh in H] for r in srcs)
                hp = (n + 1) * TB < nb
            lse_col = [lj[hh][:, 0:1] for hh in H]
            delta = [jnp.sum(doj[hh].astype(F32) * mgj[hh].astype(F32), axis=1, keepdims=True) for hh in H]
            if j > 0:
                kp = [k_ref[prow, ls[hh]] for hh in H]
                vp = [v_ref[prow, ls[hh]] for hh in H]
            else:
                kp = [kp_ref[:, ls[hh]] for hh in H]
                vp = [vp_ref[:, ls[hh]] for hh in H]
            pp = probs_all(qj, kp, lse_col, mp & hp)
            dsp = ds_all(pp, doj, vp, delta)
            if j > 0:
                for hh in H:
                    dk[hh][j - 1] = dk[hh][j - 1] + _dot_tn(dsp[hh], qj[hh])
                    dv[hh][j - 1] = dv[hh][j - 1] + _dot_tn(pp[hh].astype(BF16), doj[hh])
            if j < TB:
                kc = [k_ref[rows, ls[hh]] for hh in H]
                vc = [v_ref[rows, ls[hh]] for hh in H]
                pc = probs_all(qj, kc, lse_col, mc)
                dsc = ds_all(pc, doj, vc, delta)
                for hh in H:
                    dq_ref[rows, ls[hh]] = (_dot(dsc[hh], kc[hh]) + _dot(dsp[hh], kp[hh])).astype(dq_ref.dtype)
                for hh in H:
                    dk[hh][j] = _dot_tn(dsc[hh], qj[hh])
                    dv[hh][j] = _dot_tn(pc[hh].astype(BF16), doj[hh])
        for j in range(TB):
            rows = slice(j * ATT_BLK, (j + 1) * ATT_BLK)
            for hh in H:
                dk_ref[rows, ls[hh]] = dk[hh][j].astype(dk_ref.dtype)
                dv_ref[rows, ls[hh]] = dv[hh][j].astype(dv_ref.dtype)

    blk = pl.BlockSpec((TB * ATT_BLK, Wd), lambda b, r, n: (b * nsteps + n, r))
    prev = pl.BlockSpec((ATT_BLK, Wd), lambda b, r, n: (jnp.maximum(b * nb + n * TB - 1, 0), r))
    nxt = pl.BlockSpec((ATT_BLK, Wd), lambda b, r, n: (b * nb + jnp.minimum((n + 1) * TB, nb - 1), r))
    qv, kv, vv, dov, mgv, lv = (_att_view(t, dil) for t in (q, k, v, do, mg, lse))
    res = pl.pallas_call(
        body, grid=(bsz, dil, nsteps), in_specs=[blk] * 6 + [prev, prev, nxt, nxt, nxt, nxt],
        out_specs=[blk, blk, blk], out_shape=[jax.ShapeDtypeStruct(qv.shape, BF16)] * 3,
        name=name, compiler_params=_cparams(("arbitrary", "arbitrary", "arbitrary")),
    )(qv, kv, vv, dov, mgv, lv, kv, vv, qv, dov, mgv, lv)
    return [t.reshape(T, Wd) for t in res]


QKV_SLOTS = 3 * len(ATT_CONFIGS) * ATT_HEADS
SLOTS_PER_DEV = QKV_SLOTS // N_DEV


def _head_slots(t3):
    return [t3[s // SLOTS_PER_DEV][:, (s % SLOTS_PER_DEV) * HEAD_DIM:(s % SLOTS_PER_DEV + 1) * HEAD_DIM]
            for s in range(QKV_SLOTS)]


def _heads_of(slots, k):
    return jnp.concatenate(slots[k * ATT_HEADS:(k + 1) * ATT_HEADS], axis=1)


def _slots_to_blocked(slots):
    return jnp.stack([jnp.concatenate(slots[b * SLOTS_PER_DEV:(b + 1) * SLOTS_PER_DEV], axis=1)
                      for b in range(N_DEV)])


def _coords():
    return lax.axis_index("x"), lax.axis_index("y"), lax.axis_index("c")


def all_gather(name, xs):
    n = len(xs)

    def body(*refs):
        start, forward, finish = _gather_phases(refs[:n], refs[n:2 * n], *refs[2 * n:])
        start()
        forward()
        finish()

    anyspec = pl.BlockSpec(memory_space=pl.ANY)
    return pl.pallas_call(
        body, out_shape=_gather_out_shapes(xs), in_specs=[anyspec] * n, out_specs=[anyspec] * n,
        scratch_shapes=_gather_sems(n), name=name,
    )(*xs)


def _gather_out_shapes(xs):
    return [jax.ShapeDtypeStruct((N_DEV,) + t.shape, t.dtype) for t in xs]


def _gather_sems(n):
    return [pltpu.SemaphoreType.DMA((7 * n,)), pltpu.SemaphoreType.DMA((7 * n,)), pltpu.SemaphoreType.DMA((n,))]


def _gather_phases(x_refs, out_refs, send_sems, recv_sems, local_sems):
    n = len(x_refs)

    def parts():
        x, y, c = _coords()
        return (x, y, c), (x, y, 1 - c), [(1 - x, y), (x, 1 - y), (1 - x, 1 - y)], c

    def slot(a, px, py, pc):
        return out_refs[a].at[4 * px + 2 * py + pc]

    def copy(a, k, block, to, src=None):
        return pltpu.make_async_remote_copy(
            src_ref=slot(a, *block) if src is None else src, dst_ref=slot(a, *block),
            send_sem=send_sems.at[7 * a + k], recv_sem=recv_sems.at[7 * a + k],
            device_id=to, device_id_type=MESH)

    def mine(a, me):
        return pltpu.make_async_copy(x_refs[a], slot(a, *me), local_sems.at[a])

    def first(a, me, sibling, chips, c):
        return ([copy(a, 0, me, sibling, src=x_refs[a])]
                + [copy(a, 1 + j, me, (*chip, c), src=x_refs[a]) for j, chip in enumerate(chips)])

    def start():
        me, sibling, chips, c = parts()
        for a in range(n):
            mine(a, me).start()
        for a in range(n):
            for cp in first(a, me, sibling, chips, c):
                cp.start()

    def forward():
        me, sibling, chips, c = parts()
        for j, chip in enumerate(chips):
            for a in range(n):
                copy(a, 1 + j, (*chip, c), me).wait_recv()
                copy(a, 4 + j, (*chip, c), sibling).start()

    def finish():
        me, sibling, chips, c = parts()
        for a in range(n):
            copy(a, 0, sibling, me).wait_recv()
            for j, chip in enumerate(chips):
                copy(a, 4 + j, (*chip, 1 - c), me).wait_recv()
        for a in range(n):
            for cp in first(a, me, sibling, chips, c):
                cp.wait_send()
            for j, chip in enumerate(chips):
                copy(a, 4 + j, (*chip, c), sibling).wait_send()
            mine(a, me).wait()

    return start, forward, finish


def exchange_sibling(name, gs):
    n = len(gs)

    def body(*refs):
        g_refs, out_refs = refs[:n], refs[n:2 * n]
        send_sems, recv_sems = refs[2 * n:]
        x, y, c = _coords()
        sibling = (x, y, 1 - c)
        cps = []
        for a in range(n):
            for q in range(4):
                cps.append(pltpu.make_async_remote_copy(
                    src_ref=g_refs[a].at[2 * q + (1 - c)], dst_ref=out_refs[a].at[q],
                    send_sem=send_sems.at[4 * a + q], recv_sem=recv_sems.at[4 * a + q],
                    device_id=sibling, device_id_type=MESH))
        for cp in cps:
            cp.start()
        for cp in cps:
            cp.wait_recv()
        for cp in cps:
            cp.wait_send()

    anyspec = pl.BlockSpec(memory_space=pl.ANY)
    return pl.pallas_call(
        body, out_shape=[jax.ShapeDtypeStruct((4,) + g.shape[1:], g.dtype) for g in gs],
        in_specs=[anyspec] * n, out_specs=[anyspec] * n,
        scratch_shapes=[pltpu.SemaphoreType.DMA((4 * n,)), pltpu.SemaphoreType.DMA((4 * n,))],
        name=name,
    )(*gs)


def exchange_chips(name, ps):
    n = len(ps)

    def body(*refs):
        start, finish = _chips_phases(refs[:n], refs[n:2 * n], *refs[2 * n:])
        start()
        finish()

    anyspec = pl.BlockSpec(memory_space=pl.ANY)
    return pl.pallas_call(
        body, out_shape=_chips_out_shapes(ps), in_specs=[anyspec] * n, out_specs=[anyspec] * n,
        scratch_shapes=_chips_sems(n), name=name,
    )(*ps)


def _chips_out_shapes(ps):
    return [jax.ShapeDtypeStruct((3,) + p.shape[1:], p.dtype) for p in ps]


def _chips_sems(n):
    return [pltpu.SemaphoreType.DMA((3 * n,)), pltpu.SemaphoreType.DMA((3 * n,))]


def _chips_phases(p_refs, out_refs, send_sems, recv_sems):
    n = len(p_refs)

    def copies():
        x, y, c = _coords()
        chips = [(1 - x, y), (x, 1 - y), (1 - x, 1 - y)]
        return [pltpu.make_async_remote_copy(
            src_ref=p_refs[a].at[2 * px + py], dst_ref=out_refs[a].at[k],
            send_sem=send_sems.at[3 * a + k], recv_sem=recv_sems.at[3 * a + k],
            device_id=(px, py, c), device_id_type=MESH)
            for a in range(n) for k, (px, py) in enumerate(chips)]

    def start():
        for cp in copies():
            cp.start()

    def finish():
        cps = copies()
        for cp in cps:
            cp.wait_recv()
        for cp in cps:
            cp.wait_send()

    return start, finish


def _row_tile(R):
    tr = 256
    while R % tr:
        tr //= 2
    assert tr % 8 == 0
    return tr


def add_sibling(name, g, recv, c_idx):
    _, R, C = g.shape
    tr = _row_tile(R)

    def body(c_ref, g_ref, r_ref, o_ref, o16_ref):
        s = g_ref[...] + r_ref[...].astype(F32)
        o_ref[...] = s
        o16_ref[...] = s.astype(BF16)

    out = pl.BlockSpec((None, tr, C), lambda q, i, cr: (q, i, 0))
    return pl.pallas_call(
        body,
        grid_spec=pltpu.PrefetchScalarGridSpec(
            num_scalar_prefetch=1, grid=(4, R // tr),
            in_specs=[pl.BlockSpec((None, tr, C), lambda q, i, cr: (2 * q + cr[0], i, 0)), out],
            out_specs=[out, out]),
        out_shape=[jax.ShapeDtypeStruct((4, R, C), F32), jax.ShapeDtypeStruct((4, R, C), BF16)], name=name,
        compiler_params=_cparams(("arbitrary", "arbitrary")),
    )(c_idx, g, recv)


def _adam_math(w, g, m, v):
    m = ADAM_B1 * m + (1.0 - ADAM_B1) * g
    v = ADAM_B2 * v + (1.0 - ADAM_B2) * jnp.square(g)
    m_hat = m / (1.0 - ADAM_B1 ** ADAM_STEP)
    v_hat = v / (1.0 - ADAM_B2 ** ADAM_STEP)
    delta = -ADAM_LR * (m_hat / (jnp.sqrt(v_hat) + ADAM_EPS) + ADAM_WD * w)
    return delta, m, v


def adam_big(name, p1, recv, w, m, v, chip_idx, layer=0):
    _, R, C = p1.shape
    tr = _row_tile(R)
    nt = R // tr

    def body(q_ref, p_ref, r_ref, w_ref, m_ref, v_ref, g_ref, d_ref, nm_ref, nv_ref):
        g = ((p_ref[...] + r_ref[0].astype(F32)) + r_ref[1].astype(F32)) + r_ref[2].astype(F32)
        d, nm, nv = _adam_math(w_ref[...], g, m_ref[...], v_ref[...])
        g_ref[...] = g
        d_ref[...] = d
        nm_ref[...] = nm
        nv_ref[...] = nv

    row_in = pl.BlockSpec((tr, C), lambda i, qr: (layer * nt + i, 0))
    row = pl.BlockSpec((tr, C), lambda i, qr: (i, 0))
    return pl.pallas_call(
        body,
        grid_spec=pltpu.PrefetchScalarGridSpec(
            num_scalar_prefetch=1, grid=(nt,),
            in_specs=[pl.BlockSpec((None, tr, C), lambda i, qr: (qr[0], i, 0)),
                      pl.BlockSpec((3, tr, C), lambda i, qr: (0, i, 0)), row_in, row_in, row_in],
            out_specs=[row, row, row, row]),
        out_shape=[jax.ShapeDtypeStruct((R, C), F32)] * 4, name=name,
        compiler_params=_cparams(("arbitrary",)),
    )(chip_idx, p1, recv, w, m, v)


def sum8(parts):
    _, R, C = parts.shape

    def body(p_ref, o_ref):
        acc = p_ref[0]
        for k in range(1, N_DEV):
            acc = acc + p_ref[k]
        o_ref[...] = acc

    tr = 128
    while R % tr:
        tr //= 2
    assert tr % 8 == 0
    return pl.pallas_call(
        body, grid=(R // tr,), in_specs=[pl.BlockSpec((N_DEV, tr, C), lambda i: (0, i, 0))],
        out_specs=pl.BlockSpec((tr, C), lambda i: (i, 0)), out_shape=jax.ShapeDtypeStruct((R, C), F32),
        name="sum8", compiler_params=_cparams(("arbitrary",)),
    )(parts)


def adam_small(w, g, m, v):
    def fn(wt, gt, mt, vt):
        return _adam_math(wt, gt, mt, vt)
    C = w.shape[1]
    return rowwise("adam_small", fn, [w, g, m, v], [], [(C, F32)] * 3, tr=128)


def _pack(arrs, rows_mult=8):
    flat = jnp.concatenate([a.reshape(-1) for a in arrs])
    n = flat.shape[0]
    per = PACK_C * rows_mult
    pad = (-n) % per
    if pad:
        flat = jnp.concatenate([flat, jnp.zeros((pad,), flat.dtype)])
    return flat.reshape(-1, PACK_C)


def _unpack(buf, shapes):
    flat = buf.reshape(-1)
    out, off = [], 0
    for s in shapes:
        n = math.prod(s)
        out.append(flat[off:off + n].reshape(s))
        off += n
    return out


def _blocked(gfull, axis):
    shp = gfull.shape
    n = shp[axis] // N_DEV
    t = gfull.reshape(shp[:axis] + (N_DEV, n) + shp[axis + 1:])
    t = jnp.moveaxis(t, axis, 0)
    return t.reshape(N_DEV, -1)


def _unblocked(gathered, shard_shape, axis):
    t = jnp.moveaxis(gathered, 0, axis)
    shp = shard_shape[:axis] + (N_DEV * shard_shape[axis],) + shard_shape[axis + 1:]
    return t.reshape(shp)


def _relu2_epi(acc):
    r = jnp.maximum(acc, 0.0)
    return acc, r * r


def _step(x3, target3, W, comm):
    bsz, S, D = x3.shape
    T = bsz * S
    x = x3.reshape(T, D)
    target = target3.reshape(T, D)
    row = lambda v: v.reshape(1, -1)
    grads = {}

    s5p = (W['ssm_a_re'][0], W['ssm_a_im'][0], W['ssm_log_dt'][0], W['ssm_b_re'][0], W['ssm_b_im'][0])
    (lam_re, lam_im, bb_re, bb_im), s5_disc_vjp = jax.vjp(s5_disc, *s5p)
    pwr, pwi, l2r, l2i = s5_tables(lam_re, lam_im, S5_SUB)
    bre, bim = _s5_blockdiag_b(bb_re).astype(BF16), _s5_blockdiag_b(bb_im).astype(BF16)
    cre, cim = _s5_blockdiag_c(W['ssm_c_re'][0]).astype(BF16), _s5_blockdiag_c(W['ssm_c_im'][0]).astype(BF16)
    dskip = W['ssm_d']

    tril = jnp.tril(jnp.ones((GMLP_CHUNK, GMLP_CHUNK), bool))
    ws = jnp.where(tril[None], W['gmlp_w_s'][0], 0.0).astype(BF16)
    hw = D // GMLP_HEADS
    sbias = jnp.repeat(W['gmlp_b_s'][0].T, hw, axis=1)

    saved = []
    def add_norm(acc, *ex):
        xn = acc + ex[0] + ex[1] if len(ex) == 3 else acc + ex[0]
        return xn, _rms(xn, ex[-1])

    for i in range(DEPTH):
        sv = {'x': x}
        nm = W['norm_mix'][i:i + 1]
        nl = W['norm_mlp'][i:i + 1]
        if i == 0:
            h, hf = rms_fwd("rms_mix0", x, nm, want_f32=True)
            res = s5_fwd(hf, bre, bim, cre, cim, pwr, pwi, l2r, l2i, dskip, bsz, gather=comm.gather_list)
            ypre, gy, xs, xr_all, xi_all = res[:5]
            Wfull, Wsh = comm.weights(res[5:])
            W = {**W, **Wsh}
            conv_w = jnp.concatenate([W['conv_w_dw'][0], jnp.zeros((1, D), F32)], axis=0)
            z, = matmul("s5_glu_mm", gy, Wfull['ssm_w_glu'], mode='cb')
            def s5_glu(zt, xt, g):
                xn = xt + _glu(zt)
                return xn, _rms(xn, g)
            x1, h2 = rowwise("s5_glu", s5_glu, [z, x], [nl], [(D, F32), (D, BF16)])
            sv.update(hf=hf, ypre=ypre, gy=gy, xs=xs, xr=xr_all, xi=xi_all, z=z)
        elif i == 1:
            z, = matmul("conv_pw1", h, Wfull['conv_w_pw1'], mode='cb', epi=lambda acc, b: (acc + b,),
                        extras=[(W['conv_b_pw1'], 'row')])
            zg, = rowwise("conv_glu", _glu, [z], [], [(D, F32)])
            yc = dwconv_fwd(zg, conv_w, W['conv_b_dw'], bsz)
            y2, = rowwise("conv_ln_silu", lambda t, g, b: jax.nn.silu(_ln(t, g, b)), [yc],
                          [W['conv_ln_g'], W['conv_ln_b']], [(D, BF16)])
            x1, h2 = matmul("conv_pw2", y2, Wfull['conv_w_pw2'], epi=add_norm, whole_rows=True,
                            extras=[(W['conv_b_pw2'], 'row'), (x, 'tile'), (nl, 'row')], out_dtypes=(F32, BF16))
            sv.update(h=h, z=z, zg=zg, yc=yc, y2=y2)
        elif i == 2:
            zp, = matmul("gmlp_in", h, Wfull['gmlp_w_in'], mode='cb')

            def gm_pre(zt, g, b):
                a = jax.nn.gelu(zt)
                return a[:, :D], _ln(a[:, D:], g, b)
            u, vln = rowwise("gmlp_pre", gm_pre, [zp], [W['gmlp_ln_g'], W['gmlp_ln_b']], [(D, F32), (D, BF16)])
            gated = spatial_fwd(u, vln, ws, sbias)
            x1, h2 = matmul("gmlp_out", gated, Wfull['gmlp_w_out'], epi=add_norm, whole_rows=True,
                            extras=[(x, 'tile'), (nl, 'row')], out_dtypes=(F32, BF16))
            sv.update(h=h, zp=zp, u=u, vln=vln, gated=gated)
        else:
            qkv3, = matmul("attn_qkv", h, Wfull['attn_w_qkv'], mode='cb', out_dtypes=(BF16,), out3=True)
            slots = _head_slots(qkv3)
            ng = len(ATT_CONFIGS)
            outs, lses, blocks = [], [], []
            for gi, (window, dil) in enumerate(ATT_CONFIGS):
                qb, kb, vb = (_heads_of(slots, j * ng + gi) for j in range(3))
                ob, lb = attn_fwd("attn_fwd%d" % gi, qb, kb, vb, dil, bsz)
                blocks.append((qb, kb, vb, lb, dil))
                outs.append(ob)
                lses.append(lb)

            def merge(o0, o1, o2, l0, l1, l2):
                m = jnp.maximum(jnp.maximum(l0, l1), l2)
                e0, e1, e2 = jnp.exp(l0 - m), jnp.exp(l1 - m), jnp.exp(l2 - m)
                inv = 1.0 / (e0 + e1 + e2)
                w0, w1, w2 = e0 * inv, e1 * inv, e2 * inv
                return w0 * o0 + w1 * o1 + w2 * o2, w0, w1, w2
            merged, w0, w1, w2 = rowwise("attn_merge", merge, outs + lses, [],
                                         [(ATT_W, BF16), (ATT_W, F32), (ATT_W, F32), (ATT_W, F32)])
            wo = Wfull['attn_w_o']
            wo_nat = wo.transpose(1, 0, 2).reshape(wo.shape[1], N_DEV * wo.shape[2])
            x1, h2 = matmul("attn_o", merged, wo_nat, epi=add_norm, whole_rows=True,
                            extras=[(x, 'tile'), (nl, 'row')], out_dtypes=(F32, BF16))
            sv.update(h=h, blocks=blocks, merged=merged, wts=(w0, w1, w2))
        a, act = matmul("mlp_in%d" % i, h2, Wfull['mlp_w_in'][i], mode='cb', epi=_relu2_epi, out_dtypes=(BF16, BF16))
        if i + 1 < DEPTH:
            x2, h = matmul("mlp_out%d" % i, act, Wfull['mlp_w_out'][i], epi=add_norm, whole_rows=True,
                           extras=[(x1, 'tile'), (W['norm_mix'][i + 1:i + 2], 'row')], out_dtypes=(F32, BF16))
        else:
            x2, = matmul("mlp_out%d" % i, act, Wfull['mlp_w_out'][i], epi=lambda acc, r: (acc + r,),
                         extras=[(x1, 'tile')])
        sv.update(x1=x1, h2=h2, a=a, act=act)
        saved.append(sv)
        x = x2

    def loss_fn(xt, tt, g):
        y, vjp = jax.vjp(_rms, xt, g)
        err = y - tt
        dxx, dg = vjp(err * (1.0 / D))
        lval = jnp.sum(jnp.sum(err * err, axis=1, keepdims=True), axis=0, keepdims=True) * (0.5 / D)
        return dxx, dxx, jnp.broadcast_to(lval, (1, 128)), dg
    dx, dxb, lacc, dnf = rowwise("loss_head", loss_fn, [x, target], [row(W['norm_final'])],
                                 [(D, F32), (D, BF16)], [((1, 128), F32), ((1, D), F32)])
    loss_local = lacc[0, 0]
    grads['norm_final'] = dnf.reshape(-1)

    g_norm_mix, g_norm_mlp = [None] * DEPTH, [None] * DEPTH
    g_mlp_in, g_mlp_out = [None] * DEPTH, [None] * DEPTH
    nl_all = [W['norm_mlp'][i:i + 1] for i in range(DEPTH)]
    nm_all = [W['norm_mix'][i:i + 1] for i in range(DEPTH)]

    def norm_bwd(xt, dres, g):
        def epi(dh, xv, dr, gv):
            _, vjp = jax.vjp(_rms, xv, gv)
            dxv, dgv = vjp(dh)
            dxv = dxv + dr
            return dxv, dxv, dgv
        return dict(epi=epi, extras=[xt, dres], params=[g], out_dtypes=(F32, BF16), acc_out=[((1, D), F32)])

    for i in reversed(range(DEPTH)):
        sv = saved[i]
        da, = matmul("mlp_out_bwd%d" % i, dxb, Wfull['mlp_w_out'][i], mode='nt',
                     epi=lambda acc, av: (acc * (2.0 * jnp.maximum(av.astype(F32), 0.0)),),
                     extras=[(sv['a'], 'tile')], out_dtypes=(BF16,))
        g_mlp_out[i] = _rows_blocked(wgrad("mlp_out_wg%d" % i, sv['act'], dxb))
        dx, dxb, dg = matmul_nt_cb("mlp_in_bwd%d" % i, da, Wfull['mlp_w_in'][i], **norm_bwd(sv['x1'], dx, nl_all[i]))
        g_mlp_in[i] = wgrad("mlp_in_wg%d" % i, sv['h2'], da, cb=True)
        g_norm_mlp[i] = dg.reshape(-1)
        xin = sv['x']
        if i == 0:
            dz, = rowwise("s5_glu_bwd", _glu_bwd, [sv['z'], dx], [], [(2 * D, BF16)])
            dgy, = matmul_nt_cb("s5_glu_mm_bwd", dz, Wfull['ssm_w_glu'])
            grads['ssm_w_glu'] = wgrad("s5_glu_wg", sv['gy'], dz, cb=True)

            def gelu_bwd(yt, dt):
                _, vjp = jax.vjp(jax.nn.gelu, yt)
                return vjp(dt)[0]
            dypre, = rowwise("s5_gelu_bwd", gelu_bwd, [sv['ypre'], dgy], [], [(D, F32)])
            grads['mlp_w_in'], grads['mlp_w_out'] = g_mlp_in, g_mlp_out
            res = s5_bwd(sv['hf'], dypre, sv['xs'], sv['xr'], sv['xi'], bre, bim, cre, cim,
                         pwr[:, ::-1], pwi[:, ::-1], l2r, l2i, dskip, bsz, chips=comm.rs_front(grads))
            du, dbr, dbi, dcr, dci, dl, dd = res[:7]
            comm.recv2 = res[7:]
            dlam_re = dl[:, 0, :].reshape(SSM_GROUPS, SSM_STATE)
            dlam_im = dl[:, 1, :].reshape(SSM_GROUPS, SSM_STATE)
            s5_cot = (dlam_re, dlam_im, _s5_blockdiag_b_inv(dbr), _s5_blockdiag_b_inv(dbi))
            grads['ssm_c_re'] = _s5_blockdiag_c_inv(dcr)[None]
            grads['ssm_c_im'] = _s5_blockdiag_c_inv(dci)[None]
            grads['ssm_d'] = dd[0:1]
            dx, dxb, dg = rms_bwd("rms_mix_bwd0", xin, du, dx, nm_all[0])
        elif i == 1:
            dy2, = matmul("conv_pw2_bwd", dxb, Wfull['conv_w_pw2'], mode='nt')
            grads['conv_w_pw2'] = _rows_blocked(wgrad("conv_pw2_wg", sv['y2'], dxb))

            def ln_silu_bwd(yt, dt, dxt, g, b):
                _, vjp = jax.vjp(lambda t, gg, bb: jax.nn.silu(_ln(t, gg, bb)), yt, g, b)
                dyc, dgg, dbb = vjp(dt)
                return dyc, dgg, dbb, _colsum(dxt)
            dyc, dlg, dlb, dbp2 = rowwise("conv_ln_silu_bwd", ln_silu_bwd, [sv['yc'], dy2, dx],
                                          [W['conv_ln_g'], W['conv_ln_b']], [(D, F32)],
                                          [((1, D), F32), ((1, D), F32), ((1, D), F32)])
            grads['conv_ln_g'], grads['conv_ln_b'], grads['conv_b_pw2'] = dlg, dlb, dbp2
            dzg, dwd, dbd = dwconv_bwd(sv['zg'], dyc, conv_w, bsz)
            grads['conv_w_dw'] = dwd[None, :CONV_WIDTH]
            grads['conv_b_dw'] = dbd

            def glu_bwd1(zt, dyt):
                dzt = _glu_bwd(zt, dyt)
                return dzt, _colsum(dzt)
            dz, dbp1 = rowwise("conv_glu_bwd", glu_bwd1, [sv['z'], dzg], [], [(2 * D, BF16)], [((1, 2 * D), F32)])
            grads['conv_b_pw1'] = dbp1
            dx, dxb, dg = matmul_nt_cb("conv_pw1_bwd", dz, Wfull['conv_w_pw1'], **norm_bwd(xin, dx, nm_all[i]))
            grads['conv_w_pw1'] = wgrad("conv_pw1_wg", sv['h'], dz, cb=True)
        elif i == 2:
            dgt, = matmul("gmlp_out_bwd", dxb, Wfull['gmlp_w_out'], mode='nt', out_dtypes=(BF16,))
            grads['gmlp_w_out'] = _rows_blocked(wgrad("gmlp_out_wg", sv['gated'], dxb))
            du, dvln, dws, dsb = spatial_bwd(sv['u'], sv['vln'], dgt, ws, sbias)
            grads['gmlp_w_s'] = dws[None]
            grads['gmlp_b_s'] = dsb.reshape(GMLP_CHUNK, GMLP_HEADS, hw).sum(-1).T[None]

            def gm_pre_bwd(zt, dut, dvt, g, b):
                _, vjp_u = jax.vjp(jax.nn.gelu, zt[:, :D])
                _, vjp_v = jax.vjp(lambda zz, gg, bb: _ln(jax.nn.gelu(zz), gg, bb), zt[:, D:], g, b)
                dz2, dgg, dbb = vjp_v(dvt)
                return jnp.concatenate([vjp_u(dut)[0], dz2], axis=1), dgg, dbb
            dzp, dlg, dlb = rowwise("gmlp_pre_bwd", gm_pre_bwd, [sv['zp'], du, dvln],
                                    [W['gmlp_ln_g'], W['gmlp_ln_b']], [(2 * D, BF16)], [((1, D), F32), ((1, D), F32)])
            grads['gmlp_ln_g'], grads['gmlp_ln_b'] = dlg, dlb
            dx, dxb, dg = matmul_nt_cb("gmlp_in_bwd", dzp, Wfull['gmlp_w_in'], **norm_bwd(xin, dx, nm_all[i]))
            grads['gmlp_w_in'] = wgrad("gmlp_in_wg", sv['h'], dzp, cb=True)
        else:
            dm, = matmul_nt_cb("attn_o_bwd", dxb, Wfull['attn_w_o'])
            grads['attn_w_o'] = wgrad("attn_o_wg", sv['merged'], dxb, cb=True)
            w0, w1, w2 = sv['wts']
            do0, do1, do2 = rowwise("attn_merge_bwd", lambda d, a, b, c: (a * d, b * d, c * d), [dm, w0, w1, w2], [],
                                    [(ATT_W, BF16)] * 3)
            dparts = [[None] * 3 for _ in range(3)]
            for gi, (dog, (qb, kb, vb, lb, dil)) in enumerate(zip((do0, do1, do2), sv['blocks'])):
                dqb, dkb, dvb = attn_bwd("attn_bwd%d" % gi, qb, kb, vb, dog, sv['merged'], lb, dil, bsz)
                for j, t in enumerate((dqb, dkb, dvb)):
                    dparts[j][gi] = t
            dslots = [dparts[j][gi][:, hh * HEAD_DIM:(hh + 1) * HEAD_DIM]
                      for j in range(3) for gi in range(3) for hh in range(ATT_HEADS)]
            dqkv3 = _slots_to_blocked(dslots)
            dx, dxb, dg = matmul_nt_cb("attn_qkv_bwd", dqkv3, Wfull['attn_w_qkv'], a3=True,
                                       **norm_bwd(xin, dx, nm_all[i]))
            grads['attn_w_qkv'] = wgrad("attn_qkv_wg", sv['h'], dqkv3, cb=True, g3=True)
        g_norm_mix[i] = dg.reshape(-1)

    grads['norm_mix'] = jnp.stack(g_norm_mix)
    grads['norm_mlp'] = jnp.stack(g_norm_mlp)
    grads['mlp_w_in'] = g_mlp_in
    grads['mlp_w_out'] = g_mlp_out
    return loss_local, dx.reshape(bsz, S, D), grads, (s5_disc_vjp, s5_cot)


class _StepComm:
    def __init__(self, Wl, c_idx):
        self.Wl, self.c_idx = Wl, c_idx
        self.units = []
        for n in BIG:
            self.units += [(n, i) for i in range(DEPTH)] if Wl[n].shape[0] == DEPTH else [(n, None)]
        self.ss_names = list(SMALL_SHARDED)
        spack = _pack([Wl[n] for n in self.ss_names])
        self.gather_list = [Wl[n][0 if i is None else i].astype(BF16) for n, i in self.units] + [spack]
        self.p1 = self.recv2 = None

    @staticmethod
    def tag(n, i):
        return n if i is None else "%s%d" % (n, i)

    def weights(self, gathered):
        Wl = self.Wl
        Wfull = {}
        for (n, i), g in zip(self.units, gathered):
            w = g if BIG[n] == 2 else g.reshape(N_DEV * g.shape[1], g.shape[2])
            if i is None:
                Wfull[n] = w
            else:
                Wfull.setdefault(n, []).append(w)
        sparts = _unpack_gathered(gathered[-1], [Wl[n].shape for n in self.ss_names])
        Wsh = {n: _unblocked(p, Wl[n].shape, SMALL_SHARDED[n]) for n, p in zip(self.ss_names, sparts)}
        return Wfull, Wsh

    def rs_front(self, grads):
        pairs = [grads[n] if i is None else grads[n][i] for n, i in self.units]
        recv1 = exchange_sibling("rs_sibling", [p[1] for p in pairs])
        self.p1 = [add_sibling("add_sibling_" + self.tag(n, i), p[0], r, self.c_idx)
                   for (n, i), p, r in zip(self.units, pairs, recv1)]
        return [p[1] for p in self.p1]


def _rows_blocked(pair):
    return tuple(t.reshape(N_DEV, t.shape[0] // N_DEV, t.shape[1]) for t in pair)


def kernel(x, norm_mix, norm_mlp, norm_final, ssm_a_re, ssm_a_im, ssm_b_re, ssm_b_im, ssm_c_re, ssm_c_im, ssm_d, ssm_log_dt, ssm_w_glu, conv_w_pw1, conv_b_pw1, conv_w_dw, conv_b_dw, conv_ln_g, conv_ln_b, conv_w_pw2, conv_b_pw2, gmlp_w_in, gmlp_ln_g, gmlp_ln_b, gmlp_w_s, gmlp_b_s, gmlp_w_out, attn_w_qkv, attn_w_o, mlp_w_in, mlp_w_out, loss_target, m_norm_mix, m_norm_mlp, m_norm_final, m_ssm_a_re, m_ssm_a_im, m_ssm_b_re, m_ssm_b_im, m_ssm_c_re, m_ssm_c_im, m_ssm_d, m_ssm_log_dt, m_ssm_w_glu, m_conv_w_pw1, m_conv_b_pw1, m_conv_w_dw, m_conv_b_dw, m_conv_ln_g, m_conv_ln_b, m_conv_w_pw2, m_conv_b_pw2, m_gmlp_w_in, m_gmlp_ln_g, m_gmlp_ln_b, m_gmlp_w_s, m_gmlp_b_s, m_gmlp_w_out, m_attn_w_qkv, m_attn_w_o, m_mlp_w_in, m_mlp_w_out, v_norm_mix, v_norm_mlp, v_norm_final, v_ssm_a_re, v_ssm_a_im, v_ssm_b_re, v_ssm_b_im, v_ssm_c_re, v_ssm_c_im, v_ssm_d, v_ssm_log_dt, v_ssm_w_glu, v_conv_w_pw1, v_conv_b_pw1, v_conv_w_dw, v_conv_b_dw, v_conv_ln_g, v_conv_ln_b, v_conv_w_pw2, v_conv_b_pw2, v_gmlp_w_in, v_gmlp_ln_g, v_gmlp_ln_b, v_gmlp_w_s, v_gmlp_b_s, v_gmlp_w_out, v_attn_w_qkv, v_attn_w_o, v_mlp_w_in, v_mlp_w_out):
    args = (norm_mix, norm_mlp, norm_final, ssm_a_re, ssm_a_im, ssm_b_re, ssm_b_im, ssm_c_re, ssm_c_im, ssm_d,
            ssm_log_dt, ssm_w_glu, conv_w_pw1, conv_b_pw1, conv_w_dw, conv_b_dw, conv_ln_g, conv_ln_b, conv_w_pw2,
            conv_b_pw2, gmlp_w_in, gmlp_ln_g, gmlp_ln_b, gmlp_w_s, gmlp_b_s, gmlp_w_out, attn_w_qkv, attn_w_o,
            mlp_w_in, mlp_w_out)
    margs = (m_norm_mix, m_norm_mlp, m_norm_final, m_ssm_a_re, m_ssm_a_im, m_ssm_b_re, m_ssm_b_im, m_ssm_c_re,
             m_ssm_c_im, m_ssm_d, m_ssm_log_dt, m_ssm_w_glu, m_conv_w_pw1, m_conv_b_pw1, m_conv_w_dw, m_conv_b_dw,
             m_conv_ln_g, m_conv_ln_b, m_conv_w_pw2, m_conv_b_pw2, m_gmlp_w_in, m_gmlp_ln_g, m_gmlp_ln_b,
             m_gmlp_w_s, m_gmlp_b_s, m_gmlp_w_out, m_attn_w_qkv, m_attn_w_o, m_mlp_w_in, m_mlp_w_out)
    vargs = (v_norm_mix, v_norm_mlp, v_norm_final, v_ssm_a_re, v_ssm_a_im, v_ssm_b_re, v_ssm_b_im, v_ssm_c_re,
             v_ssm_c_im, v_ssm_d, v_ssm_log_dt, v_ssm_w_glu, v_conv_w_pw1, v_conv_b_pw1, v_conv_w_dw, v_conv_b_dw,
             v_conv_ln_g, v_conv_ln_b, v_conv_w_pw2, v_conv_b_pw2, v_gmlp_w_in, v_gmlp_ln_g, v_gmlp_ln_b,
             v_gmlp_w_s, v_gmlp_b_s, v_gmlp_w_out, v_attn_w_qkv, v_attn_w_o, v_mlp_w_in, v_mlp_w_out)
    Wl = dict(zip(WEIGHT_NAMES, args))
    Ml = dict(zip(WEIGHT_NAMES, margs))
    Vl = dict(zip(WEIGHT_NAMES, vargs))
    cx, cy, cc = _coords()
    my_idx = 4 * cx + 2 * cy + cc

    c_idx = cc.reshape(1).astype(jnp.int32)
    chip_idx = (2 * cx + cy).reshape(1).astype(jnp.int32)
    comm = _StepComm(Wl, c_idx)
    units, tag = comm.units, comm.tag
    W = {n: Wl[n] for n in SMALL if n not in SMALL_SHARDED}
    loss_local, grad_x, grads, (s5_disc_vjp, s5_cot) = _step(x, loss_target, W, comm)
    loss = lax.psum(loss_local, MESH_AXES)

    outs4 = {}
    for (n, i), p, r in zip(units, comm.p1, comm.recv2):
        w2, m2, v2 = (d[n].reshape(-1, d[n].shape[-1]) for d in (Wl, Ml, Vl))
        res = adam_big("adam_" + tag(n, i), p[0], r, w2, m2, v2, chip_idx, layer=0 if i is None else i)
        if i is None:
            outs4[n] = [t.reshape(Wl[n].shape) for t in res]
        else:
            outs4.setdefault(n, []).append(res)
    for n in BIG:
        if Wl[n].shape[0] == DEPTH:
            outs4[n] = [jnp.stack([layer[k] for layer in outs4[n]]) for k in range(4)]
    out_g = {n: outs4[n][0] for n in BIG}
    out_d = {n: outs4[n][1] for n in BIG}
    out_m = {n: outs4[n][2] for n in BIG}
    out_v = {n: outs4[n][3] for n in BIG}

    s5_lin = ['ssm_a_re', 'ssm_a_im', 'ssm_log_dt', 'ssm_b_re', 'ssm_b_im']
    direct = [n for n in SMALL if n not in s5_lin]
    def full_shape(n):
        shp = list(Wl[n].shape)
        if n in SMALL_SHARDED:
            shp[SMALL_SHARDED[n]] *= N_DEV
        return tuple(shp)
    small_parts = [grads[n].reshape(full_shape(n)) for n in direct] + list(s5_cot)
    gsum = sum8(all_gather("gather_small_grads", [_pack(small_parts)])[0])
    summed = _unpack(gsum, [p.shape for p in small_parts])
    gsmall = dict(zip(direct, summed[:len(direct)]))
    s5g = s5_disc_vjp(tuple(summed[len(direct):]))
    for n, gval in zip(s5_lin, s5g):
        gsmall[n] = gval[None]
    for n, ax in SMALL_SHARDED.items():
        gsmall[n] = lax.dynamic_slice_in_dim(gsmall[n], my_idx * Wl[n].shape[ax], Wl[n].shape[ax], axis=ax)
    sm_shapes = [Wl[n].shape for n in SMALL]
    dS, mS, vS = adam_small(_pack([Wl[n] for n in SMALL]), _pack([gsmall[n] for n in SMALL]),
                            _pack([Ml[n] for n in SMALL]), _pack([Vl[n] for n in SMALL]))
    for n, gval in zip(SMALL, [gsmall[n] for n in SMALL]):
        out_g[n] = gval.reshape(Wl[n].shape)
    out_d.update(zip(SMALL, _unpack(dS, sm_shapes)))
    out_m.update(zip(SMALL, _unpack(mS, sm_shapes)))
    out_v.update(zip(SMALL, _unpack(vS, sm_shapes)))

    return (loss, grad_x, *[out_g[n] for n in WEIGHT_NAMES], *[out_d[n] for n in WEIGHT_NAMES],
            *[out_m[n] for n in WEIGHT_NAMES], *[out_v[n] for n in WEIGHT_NAMES])


def _unpack_gathered(g, shard_shapes):
    flat = g.reshape(N_DEV, -1)
    out, off = [], 0
    for s in shard_shapes:
        n = math.prod(s)
        out.append(flat[:, off:off + n].reshape((N_DEV,) + tuple(s)))
        off += n
    return out
```

```python
import functools
import math

import jax
import jax.numpy as jnp
from jax import lax
from jax.experimental import pallas as pl
from jax.experimental.pallas import tpu as pltpu

F32 = jnp.float32
BF16 = jnp.bfloat16

D_MODEL = 1024
DEPTH = 4
EPS = 1e-6
SSM_GROUP = 16
SSM_GROUPS = 64
SSM_STATE = 64
S5_GB = 8
S5_NGB = SSM_GROUPS // S5_GB
S5_CH = S5_GB * SSM_GROUP
S5_ST = S5_GB * SSM_STATE
S5_L = 128
CONV_WIDTH = 31
CONV_PAD = 32
CONV_TS = 256
CONV_CW = 256
GMLP_CHUNK = 128
GMLP_HEADS = 4
ATT_CONFIGS = ((128, 1), (512, 4), (2048, 16))
ATT_HEADS = 8
HEAD_DIM = 64
ATT_BLK = 128
ATT_TB = 2
ATT_W = ATT_HEADS * HEAD_DIM
N_DEV = 8
ADAM_LR = 0.001
ADAM_B1 = 0.9
ADAM_B2 = 0.999
ADAM_EPS = 1e-08
ADAM_WD = 0.01
ADAM_STEP = 10
VMEM_LIMIT = 56 * 1024 * 1024
PACK_C = 1024
MESH_AXES = ("x", "y", "c")
MESH = pl.DeviceIdType.MESH

WEIGHT_NAMES = ['norm_mix', 'norm_mlp', 'norm_final', 'ssm_a_re', 'ssm_a_im', 'ssm_b_re', 'ssm_b_im',
                'ssm_c_re', 'ssm_c_im', 'ssm_d', 'ssm_log_dt', 'ssm_w_glu', 'conv_w_pw1', 'conv_b_pw1',
                'conv_w_dw', 'conv_b_dw', 'conv_ln_g', 'conv_ln_b', 'conv_w_pw2', 'conv_b_pw2',
                'gmlp_w_in', 'gmlp_ln_g', 'gmlp_ln_b', 'gmlp_w_s', 'gmlp_b_s', 'gmlp_w_out',
                'attn_w_qkv', 'attn_w_o', 'mlp_w_in', 'mlp_w_out']
BIG = {'ssm_w_glu': 2, 'conv_w_pw1': 2, 'conv_w_pw2': 1, 'gmlp_w_in': 2, 'gmlp_w_out': 1,
       'attn_w_qkv': 2, 'attn_w_o': 2, 'mlp_w_in': 2, 'mlp_w_out': 1}
SMALL_SHARDED = {'conv_b_pw1': 1, 'conv_w_dw': 2, 'conv_b_dw': 1, 'conv_ln_g': 1, 'conv_ln_b': 1,
                 'conv_b_pw2': 1, 'gmlp_ln_g': 1, 'gmlp_ln_b': 1}
SMALL = [n for n in WEIGHT_NAMES if n not in BIG]


def _cparams(sem=None):
    return pltpu.CompilerParams(dimension_semantics=sem, vmem_limit_bytes=VMEM_LIMIT)


def _dot(a, b):
    return jnp.dot(a, b, preferred_element_type=F32)


def _dot_nt(a, b):
    return lax.dot_general(a, b, (((1,), (1,)), ((), ())), preferred_element_type=F32)


def _dot_tn(a, b):
    return lax.dot_general(a, b, (((0,), (0,)), ((), ())), preferred_element_type=F32)


ROW_TILE_BYTES = 10 << 20


def _rows_for(T, row_bytes, cap=1024):
    tr = min(cap, T)
    while tr > 8 and (T % tr or tr * row_bytes > ROW_TILE_BYTES):
        tr //= 2
    assert T % tr == 0 and tr % 8 == 0
    return tr


def rowwise(name, fn, rows, params, row_out, acc_out=(), tr=None):
    T = rows[0].shape[0]
    row_bytes = (sum(r.shape[1] * r.dtype.itemsize for r in rows)
                 + sum(c * jnp.dtype(dt).itemsize for c, dt in row_out))
    tr = _rows_for(T, row_bytes, cap=tr or 1024)
    nr, npar, nro = len(rows), len(params), len(row_out)

    def body(*refs):
        ins = [r[...] for r in refs[:nr + npar]]
        outs = refs[nr + npar:]
        res = fn(*ins)
        if not isinstance(res, (tuple, list)):
            res = (res,)
        for k in range(nro):
            outs[k][...] = res[k].astype(outs[k].dtype)
        if acc_out:
            @pl.when(pl.program_id(0) == 0)
            def _():
                for k in range(nro, len(outs)):
                    outs[k][...] = jnp.zeros_like(outs[k])
            for k in range(nro, len(outs)):
                outs[k][...] += res[k].astype(outs[k].dtype)

    in_specs = [pl.BlockSpec((tr, r.shape[1]), lambda i: (i, 0)) for r in rows]
    in_specs += [pl.BlockSpec(p.shape, lambda i, nd=p.ndim: (0,) * nd) for p in params]
    out_shape = [jax.ShapeDtypeStruct((T, c), dt) for c, dt in row_out]
    out_specs = [pl.BlockSpec((tr, c), lambda i: (i, 0)) for c, dt in row_out]
    out_shape += [jax.ShapeDtypeStruct(s, dt) for s, dt in acc_out]
    out_specs += [pl.BlockSpec(s, lambda i, nd=len(s): (0,) * nd) for s, dt in acc_out]
    res = pl.pallas_call(body, grid=(T // tr,), in_specs=in_specs, out_specs=out_specs, out_shape=out_shape,
                         name=name, compiler_params=_cparams(("arbitrary",)))(*rows, *params)
    return res


def _tile_m(M, K):
    tm = 2048
    while tm > 256 and tm * K * 2 > (4 << 20):
        tm //= 2
    return min(tm, M)


def matmul(name, a, b, *, mode='nn', epi=None, extras=(), out_dtypes=(F32,), out3=False, whole_rows=False):
    M, K = a.shape
    if mode == 'cb':
        nblk, _, tn = b.shape
        N = nblk * tn
    else:
        N = b.shape[0] if mode == 'nt' else b.shape[1]
        tn = N if whole_rows else min(512, N)
    row_bytes = (K * 2 + sum(N * jnp.dtype(dt).itemsize for dt in out_dtypes)
                 + sum(N * arr.dtype.itemsize for arr, kind in extras if kind == 'tile'))
    tm = _rows_for(M, row_bytes)
    assert N % tn == 0, (M, N, tm, tn)
    nex = len(extras)

    def body(a_ref, b_ref, *rest):
        ex_refs, outs = rest[:nex], rest[nex:]
        av = a_ref[...]
        for c in range(N // tn):
            cs = slice(c * tn, (c + 1) * tn)
            if mode == 'cb':
                acc = _dot(av, b_ref[c])
            elif mode == 'nt':
                acc = _dot_nt(av, b_ref[cs, :])
            else:
                acc = _dot(av, b_ref[:, cs])
            res = epi(acc, *[e[:, cs] for e in ex_refs]) if epi is not None else (acc,)
            for o, r in zip(outs, res):
                if out3:
                    o[c] = r.astype(o.dtype)
                else:
                    o[:, cs] = r.astype(o.dtype)

    in_specs = [pl.BlockSpec((tm, K), lambda i: (i, 0)), pl.BlockSpec(b.shape, lambda i, nd=b.ndim: (0,) * nd)]
    for arr, kind in extras:
        in_specs.append(pl.BlockSpec((tm, N), lambda i: (i, 0)) if kind == 'tile'
                        else pl.BlockSpec((1, N), lambda i: (0, 0)))
    if out3:
        out_shape = [jax.ShapeDtypeStruct((N // tn, M, tn), dt) for dt in out_dtypes]
        out_specs = [pl.BlockSpec((N // tn, tm, tn), lambda i: (0, i, 0)) for dt in out_dtypes]
    else:
        out_shape = [jax.ShapeDtypeStruct((M, N), dt) for dt in out_dtypes]
        out_specs = [pl.BlockSpec((tm, N), lambda i: (i, 0)) for dt in out_dtypes]
    return pl.pallas_call(body, grid=(M // tm,), in_specs=in_specs, out_specs=out_specs,
                          out_shape=out_shape, name=name,
                          compiler_params=_cparams(("arbitrary",)))(a, b, *[e[0] for e in extras])


def matmul_nt_cb(name, a, b, *, a3=False, epi=None, extras=(), params=(), out_dtypes=(F32,), acc_out=()):
    nblk, K, n = b.shape
    M = a.shape[1] if a3 else a.shape[0]
    tm = _tile_m(M, nblk * n)
    assert M % tm == 0
    nex, npar, nro = len(extras), len(params), len(out_dtypes)

    def body(a_ref, b_ref, *rest):
        ex, outs = rest[:nex + npar], rest[nex + npar:]
        acc = None
        for j in range(nblk):
            aj = a_ref[j] if a3 else a_ref[:, j * n:(j + 1) * n]
            part = _dot_nt(aj, b_ref[j])
            acc = part if acc is None else acc + part
        res = epi(acc, *[e[...] for e in ex]) if epi is not None else (acc,)
        for o, r in zip(outs[:nro], res[:nro]):
            o[...] = r.astype(o.dtype)
        if acc_out:
            @pl.when(pl.program_id(0) == 0)
            def _():
                for o in outs[nro:]:
                    o[...] = jnp.zeros_like(o)
            for o, r in zip(outs[nro:], res[nro:]):
                o[...] += r.astype(o.dtype)

    a_spec = (pl.BlockSpec((nblk, tm, n), lambda i: (0, i, 0)) if a3
              else pl.BlockSpec((tm, nblk * n), lambda i: (i, 0)))
    row = pl.BlockSpec((tm, K), lambda i: (i, 0))
    const = lambda shp: pl.BlockSpec(shp, lambda i, nd=len(shp): (0,) * nd)
    return pl.pallas_call(
        body, grid=(M // tm,),
        in_specs=[a_spec, pl.BlockSpec((nblk, K, n), lambda i: (0, 0, 0))] + [row] * nex
        + [const(p.shape) for p in params],
        out_specs=[row] * nro + [const(s) for s, dt in acc_out],
        out_shape=[jax.ShapeDtypeStruct((M, K), dt) for dt in out_dtypes]
        + [jax.ShapeDtypeStruct(s, dt) for s, dt in acc_out],
        name=name, compiler_params=_cparams(("arbitrary",)))(a, b, *extras, *params)


def wgrad(name, a, g, *, cb=False, g3=False):
    M, K = a.shape
    tm, tk = min(M, 1024), min(K, 1024)
    if cb:
        n = g.shape[2] if g3 else g.shape[1] // N_DEV
        nj = N_DEV
        while nj > 1 and nj * tk * n * 6 > (14 << 20):
            nj //= 2
        grid = (K // tk, N_DEV // nj, M // tm)
        g_spec = (pl.BlockSpec((nj, tm, n), lambda k, j, m: (j, m, 0)) if g3
                  else pl.BlockSpec((tm, nj * n), lambda k, j, m: (m, j)))
        o_spec = pl.BlockSpec((nj, tk, n), lambda k, j, m: (j, k, 0))
        o_shape = (N_DEV, K, n)
    else:
        N = g.shape[1]
        tn = min(N, 1024)
        nj = 1
        grid = (K // tk, N // tn, M // tm)
        g_spec = pl.BlockSpec((tm, tn), lambda k, j, m: (m, j))
        o_spec = pl.BlockSpec((tk, tn), lambda k, j, m: (k, j))
        o_shape = (K, N)
    nm = M // tm

    def body(a_ref, g_ref, o_ref, o16_ref):
        m = pl.program_id(2)

        @pl.when(m == 0)
        def _():
            o_ref[...] = jnp.zeros_like(o_ref)
        at = a_ref[...].T
        if cb:
            for jj in range(nj):
                gj = g_ref[jj] if g3 else g_ref[:, jj * n:(jj + 1) * n]
                o_ref[jj] += _dot(at, gj)
        else:
            o_ref[...] += _dot(at, g_ref[...])

        @pl.when(m == nm - 1)
        def _():
            o16_ref[...] = o_ref[...].astype(BF16)

    return pl.pallas_call(
        body, grid=grid, in_specs=[pl.BlockSpec((tm, tk), lambda k, j, m: (m, k)), g_spec],
        out_specs=[o_spec, o_spec],
        out_shape=[jax.ShapeDtypeStruct(o_shape, F32), jax.ShapeDtypeStruct(o_shape, BF16)], name=name,
        compiler_params=_cparams(("arbitrary", "arbitrary", "arbitrary")))(a, g)


def _rms(x, g):
    x = x.astype(F32)
    return x * lax.rsqrt(jnp.mean(x * x, axis=-1, keepdims=True) + EPS) * g


def _ln(x, g, b):
    mu = jnp.mean(x, axis=-1, keepdims=True)
    var = jnp.mean(jnp.square(x - mu), axis=-1, keepdims=True)
    return (x - mu) * lax.rsqrt(var + EPS) * g + b


def _glu(z):
    d = z.shape[1] // 2
    return z[:, :d] * jax.nn.sigmoid(z[:, d:])


def _glu_bwd(z, dy):
    d = z.shape[1] // 2
    a, s = z[:, :d], jax.nn.sigmoid(z[:, d:])
    return jnp.concatenate([dy * s, dy * a * s * (1.0 - s)], axis=1)


def _colsum(v):
    return jnp.sum(v.astype(F32), axis=0, keepdims=True)


def rms_fwd(name, x, g, want_f32=False):
    def fn(xt, gt):
        h = _rms(xt, gt)
        return (h, h) if want_f32 else (h,)
    D = x.shape[1]
    outs = [(D, BF16)] + ([(D, F32)] if want_f32 else [])
    return rowwise(name, fn, [x], [g], outs)


def rms_bwd(name, x, dh, dres, g):
    def fn(xt, dht, drt, gt):
        _, vjp = jax.vjp(_rms, xt, gt)
        dx, dg = vjp(dht.astype(F32))
        dx = dx + drt
        return dx, dx, dg
    D = x.shape[1]
    return rowwise(name, fn, [x, dh, dres], [g], [(D, F32), (D, BF16)], [((1, D), F32)])


def s5_disc(a_re, a_im, log_dt, b_re, b_im):
    dt = jnp.exp(log_dt)[:, None]
    er = jnp.exp(a_re * dt)
    lam_re = er * jnp.cos(a_im * dt)
    lam_im = er * jnp.sin(a_im * dt)
    nr, ni = lam_re - 1.0, lam_im
    den = a_re * a_re + a_im * a_im
    f_re = (nr * a_re + ni * a_im) / den
    f_im = (ni * a_re - nr * a_im) / den
    bb_re = f_re[..., None] * b_re - f_im[..., None] * b_im
    bb_im = f_re[..., None] * b_im + f_im[..., None] * b_re
    return lam_re, lam_im, bb_re, bb_im


def _s5_blockdiag_b(bb):
    t = bb.reshape(S5_NGB, S5_GB, SSM_STATE, SSM_GROUP).transpose(0, 1, 3, 2)
    eye = jnp.eye(S5_GB, dtype=bb.dtype)
    return jnp.einsum('bgpn,gh->bgphn', t, eye).reshape(S5_NGB, S5_CH, S5_ST)


def _s5_blockdiag_b_inv(x):
    t = x.reshape(S5_NGB, S5_GB, SSM_GROUP, S5_GB, SSM_STATE)
    eye = jnp.eye(S5_GB, dtype=x.dtype)
    d = jnp.einsum('bgphn,gh->bgpn', t, eye)
    return d.transpose(0, 1, 3, 2).reshape(SSM_GROUPS, SSM_STATE, SSM_GROUP)


def _s5_blockdiag_c(c):
    t = c.reshape(S5_NGB, S5_GB, SSM_GROUP, SSM_STATE).transpose(0, 1, 3, 2)
    eye = jnp.eye(S5_GB, dtype=c.dtype)
    return jnp.einsum('bgnp,gh->bgnhp', t, eye).reshape(S5_NGB, S5_ST, S5_CH)


def _s5_blockdiag_c_inv(x):
    t = x.reshape(S5_NGB, S5_GB, SSM_STATE, S5_GB, SSM_GROUP)
    eye = jnp.eye(S5_GB, dtype=x.dtype)
    d = jnp.einsum('bgnhp,gh->bgnp', t, eye)
    return d.transpose(0, 1, 3, 2).reshape(SSM_GROUPS, SSM_GROUP, SSM_STATE)


def s5_tables(lam_re, lam_im, L):
    pr, pi = lam_re.reshape(1, -1), lam_im.reshape(1, -1)
    n = 1
    while n < L:
        lr, li = pr[n - 1:n], pi[n - 1:n]
        pr, pi = (jnp.concatenate([pr, pr * lr - pi * li], 0), jnp.concatenate([pi, pr * li + pi * lr], 0))
        n *= 2
    nk = int(math.log2(L))
    idx = [2 ** k - 1 for k in range(nk)] + [0] * (8 - nk)

    def blk(t):
        return t.reshape(t.shape[0], S5_NGB, S5_ST).transpose(1, 0, 2)

    def rows(t):
        return jnp.concatenate([t[j:j + 1] for j in idx], axis=0)
    return blk(pr), blk(pi), blk(rows(pr)), blk(rows(pi))


S5_SUB = 8


def _scan_tiles(br, bi, a2r, a2i, reverse):
    L = br.shape[0]
    sub = lax.broadcasted_iota(jnp.int32, br.shape, 0) & (S5_SUB - 1)
    xr, xi = br, bi
    for k in range(3):
        s = 1 << k
        ar, ai = a2r[k:k + 1, :], a2i[k:k + 1, :]
        if reverse:
            sr, si = pltpu.roll(xr, L - s, 0), pltpu.roll(xi, L - s, 0)
            m = sub < S5_SUB - s
        else:
            sr, si = pltpu.roll(xr, s, 0), pltpu.roll(xi, s, 0)
            m = sub >= s
        sr, si = jnp.where(m, sr, 0.0), jnp.where(m, si, 0.0)
        xr, xi = xr + ar * sr - ai * si, xi + ar * si + ai * sr
    return xr, xi


def _scan_chain(xr, xi, pr, pi, cr, ci, out_r, out_i, reverse):
    ntile = xr.shape[0] // S5_SUB
    for g in (reversed(range(ntile)) if reverse else range(ntile)):
        rs = slice(g * S5_SUB, (g + 1) * S5_SUB)
        if reverse:
            nr = xr[rs] + pr * cr + pi * ci
            ni = xi[rs] + pr * ci - pi * cr
            cr, ci = nr[0:1], ni[0:1]
        else:
            nr = xr[rs] + pr * cr - pi * ci
            ni = xi[rs] + pr * ci + pi * cr
            cr, ci = nr[S5_SUB - 1:S5_SUB], ni[S5_SUB - 1:S5_SUB]
        out_r[rs, :] = nr
        out_i[rs, :] = ni
    return cr, ci


def _grid_step(shape):
    s = 0
    for ax, n in enumerate(shape):
        s = s * n + pl.program_id(ax)
    return s


def s5_fwd(h, bre, bim, cre, cim, pwr, pwi, l2r, l2i, dskip, bsz, gather=()):
    T, D = h.shape
    L = S5_L
    S = T // bsz
    NC = S // L
    ng = len(gather)
    grid = (S5_NGB, bsz, NC)
    nsteps = S5_NGB * bsz * NC
    fwd_step = nsteps - max(1, nsteps // 32)

    def body(*refs):
        (h_ref, bre_ref, bim_ref, cre_ref, cim_ref, pwr_ref, pwi_ref, l2r_ref, l2i_ref, d_ref) = refs[:10]
        x_refs = refs[10:10 + ng]
        y_ref, gy_ref, xs_ref, xr_s, xi_s = refs[10 + ng:15 + ng]
        g_refs = refs[15 + ng:15 + 2 * ng]
        car_r, car_i = refs[15 + 2 * ng:17 + 2 * ng]
        if ng:
            start, forward, finish = _gather_phases(x_refs, g_refs, *refs[17 + 2 * ng:])
            step = _grid_step(grid)
            pl.when(step == 0)(start)
            pl.when(step == fwd_step)(forward)

        @pl.when(pl.program_id(2) == 0)
        def _():
            car_r[...] = jnp.zeros_like(car_r)
            car_i[...] = jnp.zeros_like(car_i)
        u = h_ref[...]
        ub = u.astype(BF16)
        cr, ci = car_r[0:1, :], car_i[0:1, :]
        xs_ref[...] = jnp.zeros_like(xs_ref)
        xs_ref[0:1, :] = cr
        xs_ref[1:2, :] = ci
        xr, xi = _scan_tiles(_dot(ub, bre_ref[...]), _dot(ub, bim_ref[...]), l2r_ref[...], l2i_ref[...], False)
        cr, ci = _scan_chain(xr, xi, pwr_ref[...], pwi_ref[...], cr, ci, xr_s, xi_s, False)
        car_r[...] = jnp.broadcast_to(cr, car_r.shape)
        car_i[...] = jnp.broadcast_to(ci, car_i.shape)
        y = (_dot(xr_s[...].astype(BF16), cre_ref[...]) - _dot(xi_s[...].astype(BF16), cim_ref[...])
             + d_ref[...] * u)
        y_ref[...] = y
        gy_ref[...] = jax.nn.gelu(y).astype(BF16)
        if ng:
            pl.when(step == nsteps - 1)(finish)

    tok = lambda g, b, c: (b * NC + c, g)
    par = lambda g, b, c: (g, 0, 0)
    anyspec = pl.BlockSpec(memory_space=pl.ANY)
    return pl.pallas_call(
        body, grid=grid,
        in_specs=[pl.BlockSpec((L, S5_CH), tok),
                  pl.BlockSpec((None, S5_CH, S5_ST), par), pl.BlockSpec((None, S5_CH, S5_ST), par),
                  pl.BlockSpec((None, S5_ST, S5_CH), par), pl.BlockSpec((None, S5_ST, S5_CH), par),
                  pl.BlockSpec((None, 8, S5_ST), par), pl.BlockSpec((None, 8, S5_ST), par),
                  pl.BlockSpec((None, 8, S5_ST), par), pl.BlockSpec((None, 8, S5_ST), par),
                  pl.BlockSpec((1, S5_CH), lambda g, b, c: (0, g))] + [anyspec] * ng,
        out_specs=[pl.BlockSpec((L, S5_CH), tok), pl.BlockSpec((L, S5_CH), tok),
                   pl.BlockSpec((None, 8, S5_ST), lambda g, b, c: (b * NC + c, 0, g)),
                   pl.BlockSpec((L, S5_ST), tok), pl.BlockSpec((L, S5_ST), tok)] + [anyspec] * ng,
        out_shape=[jax.ShapeDtypeStruct((T, D), F32), jax.ShapeDtypeStruct((T, D), BF16),
                   jax.ShapeDtypeStruct((bsz * NC, 8, S5_NGB * S5_ST), F32),
                   jax.ShapeDtypeStruct((T, S5_NGB * S5_ST), F32), jax.ShapeDtypeStruct((T, S5_NGB * S5_ST), F32)]
        + _gather_out_shapes(gather),
        scratch_shapes=[pltpu.VMEM((8, S5_ST), F32), pltpu.VMEM((8, S5_ST), F32)]
        + (_gather_sems(ng) if ng else []),
        name="s5_fwd", compiler_params=_cparams(("arbitrary", "arbitrary", "arbitrary")),
    )(h, bre, bim, cre, cim, pwr, pwi, l2r, l2i, dskip, *gather)


def s5_bwd(h, dy, xs, xr, xi, bre, bim, cre, cim, pwr_rev, pwi_rev, l2r, l2i, dskip, bsz, chips=()):
    T, D = h.shape
    L = S5_L
    S = T // bsz
    NC = S // L
    nc = len(chips)
    grid = (S5_NGB, bsz, NC)
    nsteps = S5_NGB * bsz * NC

    def body(*refs):
        (h_ref, dy_ref, xs_ref, xr_ref, xi_ref, bre_ref, bim_ref, cre_ref, cim_ref, qr_ref, qi_ref,
         l2r_ref, l2i_ref, d_ref) = refs[:14]
        p_refs = refs[14:14 + nc]
        du_ref, dbr_ref, dbi_ref, dcr_ref, dci_ref, dl_ref, dd_ref = refs[14 + nc:21 + nc]
        r_refs = refs[21 + nc:21 + 2 * nc]
        car_r, car_i, dr_s, di_s = refs[21 + 2 * nc:25 + 2 * nc]
        if nc:
            start, finish = _chips_phases(p_refs, r_refs, *refs[25 + 2 * nc:])
            step = _grid_step(grid)
            pl.when(step == 0)(start)
        first = (pl.program_id(1) == 0) & (pl.program_id(2) == 0)

        @pl.when(first)
        def _():
            for r in (dbr_ref, dbi_ref, dcr_ref, dci_ref, dl_ref, dd_ref):
                r[...] = jnp.zeros_like(r)

        @pl.when(pl.program_id(2) == 0)
        def _():
            car_r[...] = jnp.zeros_like(car_r)
            car_i[...] = jnp.zeros_like(car_i)

        u = h_ref[...]
        ub = u.astype(BF16)
        dyv = dy_ref[...]
        dyb = dyv.astype(BF16)
        l2r_v, l2i_v = l2r_ref[...], l2i_ref[...]
        x0r, x0i = xs_ref[0:1, :], xs_ref[1:2, :]
        xr, xi = xr_ref[...], xi_ref[...]
        gr = _dot_nt(dyb, cre_ref[...])
        gi = -_dot_nt(dyb, cim_ref[...])
        dr, di = _scan_tiles(gr, gi, l2r_v, -l2i_v, True)
        cr, ci = _scan_chain(dr, di, qr_ref[...], qi_ref[...], car_r[0:1, :], car_i[0:1, :], dr_s, di_s, True)
        dr, di = dr_s[...], di_s[...]
        car_r[...] = jnp.broadcast_to(cr, car_r.shape)
        car_i[...] = jnp.broadcast_to(ci, car_i.shape)
        row = lax.broadcasted_iota(jnp.int32, xr.shape, 0)
        xpr = jnp.where(row >= 1, pltpu.roll(xr, 1, 0), x0r)
        xpi = jnp.where(row >= 1, pltpu.roll(xi, 1, 0), x0i)
        dl_ref[0:1, :] += _colsum(dr * xpr + di * xpi)
        dl_ref[1:2, :] += _colsum(di * xpr - dr * xpi)
        drb, dib = dr.astype(BF16), di.astype(BF16)
        dcr_ref[...] += _dot_tn(xr.astype(BF16), dyb)
        dci_ref[...] -= _dot_tn(xi.astype(BF16), dyb)
        dbr_ref[...] += _dot_tn(ub, drb)
        dbi_ref[...] += _dot_tn(ub, dib)
        du_ref[...] = _dot_nt(drb, bre_ref[...]) + _dot_nt(dib, bim_ref[...]) + d_ref[...] * dyv
        dd_ref[0:1, :] += _colsum(dyv * u)
        if nc:
            pl.when(step == nsteps - 1)(finish)

    tok = lambda g, b, c: (b * NC + (NC - 1 - c), g)
    par = lambda g, b, c: (g, 0, 0)
    anyspec = pl.BlockSpec(memory_space=pl.ANY)
    return pl.pallas_call(
        body, grid=grid,
        in_specs=[pl.BlockSpec((L, S5_CH), tok), pl.BlockSpec((L, S5_CH), tok),
                  pl.BlockSpec((None, 8, S5_ST), lambda g, b, c: (b * NC + (NC - 1 - c), 0, g)),
                  pl.BlockSpec((L, S5_ST), tok), pl.BlockSpec((L, S5_ST), tok),
                  pl.BlockSpec((None, S5_CH, S5_ST), par), pl.BlockSpec((None, S5_CH, S5_ST), par),
                  pl.BlockSpec((None, S5_ST, S5_CH), par), pl.BlockSpec((None, S5_ST, S5_CH), par),
                  pl.BlockSpec((None, 8, S5_ST), par), pl.BlockSpec((None, 8, S5_ST), par),
                  pl.BlockSpec((None, 8, S5_ST), par), pl.BlockSpec((None, 8, S5_ST), par),
                  pl.BlockSpec((1, S5_CH), lambda g, b, c: (0, g))] + [anyspec] * nc,
        out_specs=[pl.BlockSpec((L, S5_CH), tok),
                   pl.BlockSpec((None, S5_CH, S5_ST), par), pl.BlockSpec((None, S5_CH, S5_ST), par),
                   pl.BlockSpec((None, S5_ST, S5_CH), par), pl.BlockSpec((None, S5_ST, S5_CH), par),
                   pl.BlockSpec((None, 8, S5_ST), par),
                   pl.BlockSpec((8, S5_CH), lambda g, b, c: (0, g))] + [anyspec] * nc,
        out_shape=[jax.ShapeDtypeStruct((T, D), F32),
                   jax.ShapeDtypeStruct((S5_NGB, S5_CH, S5_ST), F32), jax.ShapeDtypeStruct((S5_NGB, S5_CH, S5_ST), F32),
                   jax.ShapeDtypeStruct((S5_NGB, S5_ST, S5_CH), F32), jax.ShapeDtypeStruct((S5_NGB, S5_ST, S5_CH), F32),
                   jax.ShapeDtypeStruct((S5_NGB, 8, S5_ST), F32), jax.ShapeDtypeStruct((8, D), F32)]
        + _chips_out_shapes(chips),
        scratch_shapes=[pltpu.VMEM((8, S5_ST), F32), pltpu.VMEM((8, S5_ST), F32)]
        + [pltpu.VMEM((L, S5_ST), F32)] * 2 + (_chips_sems(nc) if nc else []),
        name="s5_bwd", compiler_params=_cparams(("arbitrary", "arbitrary", "arbitrary")),
    )(h, dy, xs, xr, xi, bre, bim, cre, cim, pwr_rev, pwi_rev, l2r, l2i, dskip, *chips)


def _shift_rows(win, off, n):
    if off == 0:
        return win[:n]
    return pltpu.roll(win, win.shape[0] - off, 0)[:n]


def dwconv_fwd(z, w, b, bsz):
    T, D = z.shape
    S = T // bsz
    TS, CW, PAD = CONV_TS, CONV_CW, CONV_PAD

    def body(z_ref, w_ref, b_ref, y_ref, zp):
        zp[0:PAD, :] = jnp.zeros((PAD, CW), F32)
        zp[PAD:, :] = z_ref[...]
        wv, bv = w_ref[...], b_ref[...]

        def step(t, carry):
            base = pl.multiple_of(t * TS, TS)
            win = zp[pl.ds(base, TS + PAD), :]
            acc = jnp.zeros((TS, CW), F32) + bv
            for k in range(CONV_WIDTH):
                acc = acc + wv[k:k + 1, :] * _shift_rows(win, PAD - (CONV_WIDTH - 1) + k, TS)
            y_ref[pl.ds(base, TS), :] = acc
            return carry
        lax.fori_loop(0, S // TS, step, 0)

    return pl.pallas_call(
        body, grid=(D // CW, bsz),
        in_specs=[pl.BlockSpec((S, CW), lambda c, bb: (bb, c)), pl.BlockSpec((32, CW), lambda c, bb: (0, c)),
                  pl.BlockSpec((1, CW), lambda c, bb: (0, c))],
        out_specs=pl.BlockSpec((S, CW), lambda c, bb: (bb, c)),
        out_shape=jax.ShapeDtypeStruct((T, D), F32),
        scratch_shapes=[pltpu.VMEM((S + PAD, CW), F32)],
        name="dwconv_fwd", compiler_params=_cparams(("arbitrary", "arbitrary")),
    )(z, w, b)


def dwconv_bwd(z, dy, w, bsz):
    T, D = z.shape
    S = T // bsz
    TS, CW, PAD = CONV_TS, CONV_CW, CONV_PAD

    def body(z_ref, dy_ref, w_ref, dz_ref, dw_ref, db_ref, zp, dyp):
        @pl.when(pl.program_id(1) == 0)
        def _():
            dw_ref[...] = jnp.zeros_like(dw_ref)
            db_ref[...] = jnp.zeros_like(db_ref)
        zp[0:PAD, :] = jnp.zeros((PAD, CW), F32)
        zp[PAD:, :] = z_ref[...]
        dyp[0:S, :] = dy_ref[...]
        dyp[S:, :] = jnp.zeros((PAD, CW), F32)
        wv = w_ref[...]

        def step(t, carry):
            base = pl.multiple_of(t * TS, TS)
            zwin = zp[pl.ds(base, TS + PAD), :]
            dwin = dyp[pl.ds(base, TS + PAD), :]
            dyt = dwin[:TS]
            acc = jnp.zeros((TS, CW), F32)
            for j in range(CONV_WIDTH):
                k = CONV_WIDTH - 1 - j
                acc = acc + wv[k:k + 1, :] * _shift_rows(dwin, j, TS)
            dz_ref[pl.ds(base, TS), :] = acc
            for k in range(CONV_WIDTH):
                prod = dyt * _shift_rows(zwin, PAD - (CONV_WIDTH - 1) + k, TS)
                dw_ref[8 * k:8 * k + 8, :] += jnp.sum(prod.reshape(TS // 8, 8, CW), axis=0)
            db_ref[...] += jnp.sum(dyt.reshape(TS // 8, 8, CW), axis=0)
            return carry
        lax.fori_loop(0, S // TS, step, 0)

    dz, dw, db = pl.pallas_call(
        body, grid=(D // CW, bsz),
        in_specs=[pl.BlockSpec((S, CW), lambda c, bb: (bb, c)), pl.BlockSpec((S, CW), lambda c, bb: (bb, c)),
                  pl.BlockSpec((32, CW), lambda c, bb: (0, c))],
        out_specs=[pl.BlockSpec((S, CW), lambda c, bb: (bb, c)), pl.BlockSpec((8 * 32, CW), lambda c, bb: (0, c)),
                   pl.BlockSpec((8, CW), lambda c, bb: (0, c))],
        out_shape=[jax.ShapeDtypeStruct((T, D), F32), jax.ShapeDtypeStruct((8 * 32, D), F32),
                   jax.ShapeDtypeStruct((8, D), F32)],
        scratch_shapes=[pltpu.VMEM((S + PAD, CW), F32), pltpu.VMEM((S + PAD, CW), F32)],
        name="dwconv_bwd", compiler_params=_cparams(("arbitrary", "arbitrary")),
    )(z, dy, w)
    return dz, dw.reshape(32, 8, D).sum(axis=1), db.sum(axis=0, keepdims=True)


def spatial_fwd(u, vln, ws, bias):
    T, E = u.shape
    C, H = GMLP_CHUNK, GMLP_HEADS
    hw = E // H

    def body(u_ref, v_ref, ws_ref, b_ref, o_ref):
        for hh in range(H):
            sl = slice(hh * hw, (hh + 1) * hw)
            vp = _dot(ws_ref[hh], v_ref[:, sl]) + b_ref[:, sl]
            o_ref[:, sl] = (u_ref[:, sl] * vp).astype(o_ref.dtype)

    return pl.pallas_call(
        body, grid=(T // C,),
        in_specs=[pl.BlockSpec((C, E), lambda i: (i, 0)), pl.BlockSpec((C, E), lambda i: (i, 0)),
                  pl.BlockSpec((H, C, C), lambda i: (0, 0, 0)), pl.BlockSpec((C, E), lambda i: (0, 0))],
        out_specs=pl.BlockSpec((C, E), lambda i: (i, 0)),
        out_shape=jax.ShapeDtypeStruct((T, E), BF16),
        name="spatial_fwd", compiler_params=_cparams(("arbitrary",)),
    )(u, vln, ws, bias)


def spatial_bwd(u, vln, dg, ws, bias):
    T, E = u.shape
    C, H = GMLP_CHUNK, GMLP_HEADS
    hw = E // H

    def body(u_ref, v_ref, dg_ref, ws_ref, b_ref, du_ref, dv_ref, dws_ref, db_ref):
        @pl.when(pl.program_id(0) == 0)
        def _():
            dws_ref[...] = jnp.zeros_like(dws_ref)
            db_ref[...] = jnp.zeros_like(db_ref)
        tril = (lax.broadcasted_iota(jnp.int32, (C, C), 1) <= lax.broadcasted_iota(jnp.int32, (C, C), 0))
        for hh in range(H):
            sl = slice(hh * hw, (hh + 1) * hw)
            v = v_ref[:, sl]
            w = ws_ref[hh]
            dgv = dg_ref[:, sl].astype(F32)
            vp = _dot(w, v) + b_ref[:, sl]
            du_ref[:, sl] = dgv * vp
            dvp = dgv * u_ref[:, sl]
            dvpb = dvp.astype(BF16)
            dv_ref[:, sl] = _dot_tn(w, dvpb)
            dws_ref[hh] += jnp.where(tril, _dot_nt(dvpb, v), 0.0)
            db_ref[:, sl] += dvp

    return pl.pallas_call(
        body, grid=(T // C,),
        in_specs=[pl.BlockSpec((C, E), lambda i: (i, 0)), pl.BlockSpec((C, E), lambda i: (i, 0)),
                  pl.BlockSpec((C, E), lambda i: (i, 0)),
                  pl.BlockSpec((H, C, C), lambda i: (0, 0, 0)), pl.BlockSpec((C, E), lambda i: (0, 0))],
        out_specs=[pl.BlockSpec((C, E), lambda i: (i, 0)), pl.BlockSpec((C, E), lambda i: (i, 0)),
                   pl.BlockSpec((H, C, C), lambda i: (0, 0, 0)), pl.BlockSpec((C, E), lambda i: (0, 0))],
        out_shape=[jax.ShapeDtypeStruct((T, E), F32), jax.ShapeDtypeStruct((T, E), F32),
                   jax.ShapeDtypeStruct((H, C, C), F32), jax.ShapeDtypeStruct((C, E), F32)],
        name="spatial_bwd", compiler_params=_cparams(("arbitrary",)),
    )(u, vln, dg, ws, bias)


def _att_masks():
    r = lax.broadcasted_iota(jnp.int32, (ATT_BLK, ATT_BLK), 0)
    c = lax.broadcasted_iota(jnp.int32, (ATT_BLK, ATT_BLK), 1)
    return c <= r, c >= r


NEG = -1e30
ATT_SCALE = HEAD_DIM ** -0.5


def _att_view(t, dil):
    return t.reshape(t.shape[0] // dil, dil * t.shape[1])


def attn_fwd(name, q, k, v, dil, bsz):
    T, Wd = q.shape
    nb = T // (bsz * dil * ATT_BLK)
    TB = min(nb, ATT_TB)
    nsteps = nb // TB

    def body(q_ref, k_ref, v_ref, kp_ref, vp_ref, o_ref, l_ref):
        n = pl.program_id(2)
        mc, mp = _att_masks()
        for j in range(TB):
            rows = slice(j * ATT_BLK, (j + 1) * ATT_BLK)
            prow = slice((j - 1) * ATT_BLK, j * ATT_BLK)
            hp = (n * TB + j) > 0
            H = range(ATT_HEADS)
            ls = [slice(hh * HEAD_DIM, (hh + 1) * HEAD_DIM) for hh in H]
            qj = [q_ref[rows, ls[hh]] for hh in H]
            kc = [k_ref[rows, ls[hh]] for hh in H]
            kp = [k_ref[prow, ls[hh]] if j > 0 else kp_ref[:, ls[hh]] for hh in H]
            sc = [jnp.where(mc, _dot_nt(qj[hh], kc[hh]) * ATT_SCALE, NEG) for hh in H]
            sp = [jnp.where(mp & hp, _dot_nt(qj[hh], kp[hh]) * ATT_SCALE, NEG) for hh in H]
            m = [jnp.maximum(jnp.max(sc[hh], axis=1, keepdims=True), jnp.max(sp[hh], axis=1, keepdims=True))
                 for hh in H]
            pc = [jnp.exp(sc[hh] - m[hh]) for hh in H]
            pp = [jnp.exp(sp[hh] - m[hh]) for hh in H]
            l = [jnp.sum(pc[hh], axis=1, keepdims=True) + jnp.sum(pp[hh], axis=1, keepdims=True) for hh in H]
            vc = [v_ref[rows, ls[hh]] for hh in H]
            vp = [v_ref[prow, ls[hh]] if j > 0 else vp_ref[:, ls[hh]] for hh in H]
            for hh in H:
                o_ref[rows, ls[hh]] = (_dot(pc[hh].astype(BF16), vc[hh]) + _dot(pp[hh].astype(BF16), vp[hh])) / l[hh]
                l_ref[rows, ls[hh]] = jnp.broadcast_to(m[hh] + jnp.log(l[hh]), (ATT_BLK, HEAD_DIM))

    blk = pl.BlockSpec((TB * ATT_BLK, Wd), lambda b, r, n: (b * nsteps + n, r))
    prev = pl.BlockSpec((ATT_BLK, Wd), lambda b, r, n: (jnp.maximum(b * nb + n * TB - 1, 0), r))
    qv, kv, vv = (_att_view(t, dil) for t in (q, k, v))
    o, l = pl.pallas_call(
        body, grid=(bsz, dil, nsteps), in_specs=[blk, blk, blk, prev, prev], out_specs=[blk, blk],
        out_shape=[jax.ShapeDtypeStruct(qv.shape, F32), jax.ShapeDtypeStruct(qv.shape, F32)],
        name=name, compiler_params=_cparams(("arbitrary", "arbitrary", "arbitrary")),
    )(qv, kv, vv, kv, vv)
    return o.reshape(T, Wd), l.reshape(T, Wd)


def attn_bwd(name, q, k, v, do, mg, lse, dil, bsz):
    T, Wd = q.shape
    nb = T // (bsz * dil * ATT_BLK)
    TB = min(nb, ATT_TB)
    nsteps = nb // TB

    def body(q_ref, k_ref, v_ref, do_ref, mg_ref, l_ref, kp_ref, vp_ref, qn_ref, don_ref, mgn_ref, ln_ref,
             dq_ref, dk_ref, dv_ref):
        n = pl.program_id(2)
        mc, mp = _att_masks()

        def probs_all(qs, ks, lse_cols, mask):
            s = [_dot_nt(qh, kh) * ATT_SCALE for qh, kh in zip(qs, ks)]
            return [jnp.where(mask, jnp.exp(sh - lc), 0.0) for sh, lc in zip(s, lse_cols)]

        def ds_all(ps, dos, vs, deltas):
            dp = [_dot_nt(dh, vh) for dh, vh in zip(dos, vs)]
            return [(ph * (dph - dl) * ATT_SCALE).astype(BF16) for ph, dph, dl in zip(ps, dp, deltas)]

        H = range(ATT_HEADS)
        ls = [slice(hh * HEAD_DIM, (hh + 1) * HEAD_DIM) for hh in H]
        dk = [[None] * TB for _ in H]
        dv = [[None] * TB for _ in H]
        for j in range(TB + 1):
            rows = slice(j * ATT_BLK, (j + 1) * ATT_BLK)
            prow = slice((j - 1) * ATT_BLK, j * ATT_BLK)
            if j < TB:
                srcs = (q_ref, do_ref, mg_ref, l_ref)
                qj, doj, mgj, lj = ([r[rows, ls[hh]] for hh in H] for r in srcs)
                hp = (n * TB + j) > 0
            else:
                srcs = (qn_ref, don_ref, mgn_ref, ln_ref)
                qj, doj, mgj, lj = ([r[:, ls[hh]] for hh in H] for r in srcs)
                hp = (n + 1) * TB < nb
            lse_col = [lj[hh][:, 0:1] for hh in H]
            delta = [jnp.sum(doj[hh].astype(F32) * mgj[hh].astype(F32), axis=1, keepdims=True) for hh in H]
            if j > 0:
                kp = [k_ref[prow, ls[hh]] for hh in H]
                vp = [v_ref[prow, ls[hh]] for hh in H]
            else:
                kp = [kp_ref[:, ls[hh]] for hh in H]
                vp = [vp_ref[:, ls[hh]] for hh in H]
            pp = probs_all(qj, kp, lse_col, mp & hp)
            dsp = ds_all(pp, doj, vp, delta)
            if j > 0:
                for hh in H:
                    dk[hh][j - 1] = dk[hh][j - 1] + _dot_tn(dsp[hh], qj[hh])
                    dv[hh][j - 1] = dv[hh][j - 1] + _dot_tn(pp[hh].astype(BF16), doj[hh])
            if j < TB:
                kc = [k_ref[rows, ls[hh]] for hh in H]
                vc = [v_ref[rows, ls[hh]] for hh in H]
                pc = probs_all(qj, kc, lse_col, mc)
                dsc = ds_all(pc, doj, vc, delta)
                for hh in H:
                    dq_ref[rows, ls[hh]] = (_dot(dsc[hh], kc[hh]) + _dot(dsp[hh], kp[hh])).astype(dq_ref.dtype)
                for hh in H:
                    dk[hh][j] = _dot_tn(dsc[hh], qj[hh])
                    dv[hh][j] = _dot_tn(pc[hh].astype(BF16), doj[hh])
        for j in range(TB):
            rows = slice(j * ATT_BLK, (j + 1) * ATT_BLK)
            for hh in H:
                dk_ref[rows, ls[hh]] = dk[hh][j].astype(dk_ref.dtype)
                dv_ref[rows, ls[hh]] = dv[hh][j].astype(dv_ref.dtype)

    blk = pl.BlockSpec((TB * ATT_BLK, Wd), lambda b, r, n: (b * nsteps + n, r))
    prev = pl.BlockSpec((ATT_BLK, Wd), lambda b, r, n: (jnp.maximum(b * nb + n * TB - 1, 0), r))
    nxt = pl.BlockSpec((ATT_BLK, Wd), lambda b, r, n: (b * nb + jnp.minimum((n + 1) * TB, nb - 1), r))
    qv, kv, vv, dov, mgv, lv = (_att_view(t, dil) for t in (q, k, v, do, mg, lse))
    res = pl.pallas_call(
        body, grid=(bsz, dil, nsteps), in_specs=[blk] * 6 + [prev, prev, nxt, nxt, nxt, nxt],
        out_specs=[blk, blk, blk], out_shape=[jax.ShapeDtypeStruct(qv.shape, BF16)] * 3,
        name=name, compiler_params=_cparams(("arbitrary", "arbitrary", "arbitrary")),
    )(qv, kv, vv, dov, mgv, lv, kv, vv, qv, dov, mgv, lv)
    return [t.reshape(T, Wd) for t in res]


QKV_SLOTS = 3 * len(ATT_CONFIGS) * ATT_HEADS
SLOTS_PER_DEV = QKV_SLOTS // N_DEV


def qkv_matmul(name, a, b):
    M, K = a.shape
    nblk, _, n = b.shape
    nout = QKV_SLOTS // ATT_HEADS
    tm = _rows_for(M, K * 2 + nout * ATT_W * 2)

    def body(a_ref, b_ref, *outs):
        av = a_ref[...]
        for c in range(nblk):
            acc = _dot(av, b_ref[c]).astype(BF16)
            for r in range(SLOTS_PER_DEV):
                k, hh = divmod(c * SLOTS_PER_DEV + r, ATT_HEADS)
                outs[k][:, hh * HEAD_DIM:(hh + 1) * HEAD_DIM] = acc[:, r * HEAD_DIM:(r + 1) * HEAD_DIM]

    return pl.pallas_call(
        body, grid=(M // tm,),
        in_specs=[pl.BlockSpec((tm, K), lambda i: (i, 0)), pl.BlockSpec(b.shape, lambda i: (0, 0, 0))],
        out_specs=[pl.BlockSpec((tm, ATT_W), lambda i: (i, 0))] * nout,
        out_shape=[jax.ShapeDtypeStruct((M, ATT_W), BF16)] * nout,
        name=name, compiler_params=_cparams(("arbitrary",)))(a, b)


def _slots_to_blocked(slots):
    return jnp.stack([jnp.concatenate(slots[b * SLOTS_PER_DEV:(b + 1) * SLOTS_PER_DEV], axis=1)
                      for b in range(N_DEV)])


def _coords():
    return lax.axis_index("x"), lax.axis_index("y"), lax.axis_index("c")


def all_gather(name, xs):
    n = len(xs)

    def body(*refs):
        start, forward, finish = _gather_phases(refs[:n], refs[n:2 * n], *refs[2 * n:])
        start()
        forward()
        finish()

    anyspec = pl.BlockSpec(memory_space=pl.ANY)
    return pl.pallas_call(
        body, out_shape=_gather_out_shapes(xs), in_specs=[anyspec] * n, out_specs=[anyspec] * n,
        scratch_shapes=_gather_sems(n), name=name,
    )(*xs)


def _gather_out_shapes(xs):
    return [jax.ShapeDtypeStruct((N_DEV,) + t.shape, t.dtype) for t in xs]


def _gather_sems(n):
    return [pltpu.SemaphoreType.DMA((7 * n,)), pltpu.SemaphoreType.DMA((7 * n,)), pltpu.SemaphoreType.DMA((n,))]


def _gather_phases(x_refs, out_refs, send_sems, recv_sems, local_sems):
    n = len(x_refs)

    def parts():
        x, y, c = _coords()
        return (x, y, c), (x, y, 1 - c), [(1 - x, y), (x, 1 - y), (1 - x, 1 - y)], c

    def slot(a, px, py, pc):
        return out_refs[a].at[4 * px + 2 * py + pc]

    def copy(a, k, block, to, src=None):
        return pltpu.make_async_remote_copy(
            src_ref=slot(a, *block) if src is None else src, dst_ref=slot(a, *block),
            send_sem=send_sems.at[7 * a + k], recv_sem=recv_sems.at[7 * a + k],
            device_id=to, device_id_type=MESH)

    def mine(a, me):
        return pltpu.make_async_copy(x_refs[a], slot(a, *me), local_sems.at[a])

    def first(a, me, sibling, chips, c):
        return ([copy(a, 0, me, sibling, src=x_refs[a])]
                + [copy(a, 1 + j, me, (*chip, c), src=x_refs[a]) for j, chip in enumerate(chips)])

    def start():
        me, sibling, chips, c = parts()
        for a in range(n):
            mine(a, me).start()
        for a in range(n):
            for cp in first(a, me, sibling, chips, c):
                cp.start()

    def forward():
        me, sibling, chips, c = parts()
        for j, chip in enumerate(chips):
            for a in range(n):
                copy(a, 1 + j, (*chip, c), me).wait_recv()
                copy(a, 4 + j, (*chip, c), sibling).start()

    def finish():
        me, sibling, chips, c = parts()
        for a in range(n):
            copy(a, 0, sibling, me).wait_recv()
            for j, chip in enumerate(chips):
                copy(a, 4 + j, (*chip, 1 - c), me).wait_recv()
        for a in range(n):
            for cp in first(a, me, sibling, chips, c):
                cp.wait_send()
            for j, chip in enumerate(chips):
                copy(a, 4 + j, (*chip, c), sibling).wait_send()
            mine(a, me).wait()

    return start, forward, finish


def exchange_sibling(name, gs):
    n = len(gs)

    def body(*refs):
        g_refs, out_refs = refs[:n], refs[n:2 * n]
        send_sems, recv_sems = refs[2 * n:]
        x, y, c = _coords()
        sibling = (x, y, 1 - c)
        cps = []
        for a in range(n):
            for q in range(4):
                cps.append(pltpu.make_async_remote_copy(
                    src_ref=g_refs[a].at[2 * q + (1 - c)], dst_ref=out_refs[a].at[q],
                    send_sem=send_sems.at[4 * a + q], recv_sem=recv_sems.at[4 * a + q],
                    device_id=sibling, device_id_type=MESH))
        for cp in cps:
            cp.start()
        for cp in cps:
            cp.wait_recv()
        for cp in cps:
            cp.wait_send()

    anyspec = pl.BlockSpec(memory_space=pl.ANY)
    return pl.pallas_call(
        body, out_shape=[jax.ShapeDtypeStruct((4,) + g.shape[1:], g.dtype) for g in gs],
        in_specs=[anyspec] * n, out_specs=[anyspec] * n,
        scratch_shapes=[pltpu.SemaphoreType.DMA((4 * n,)), pltpu.SemaphoreType.DMA((4 * n,))],
        name=name,
    )(*gs)


def exchange_chips(name, ps):
    n = len(ps)

    def body(*refs):
        start, finish = _chips_phases(refs[:n], refs[n:2 * n], *refs[2 * n:])
        start()
        finish()

    anyspec = pl.BlockSpec(memory_space=pl.ANY)
    return pl.pallas_call(
        body, out_shape=_chips_out_shapes(ps), in_specs=[anyspec] * n, out_specs=[anyspec] * n,
        scratch_shapes=_chips_sems(n), name=name,
    )(*ps)


def _chips_out_shapes(ps):
    return [jax.ShapeDtypeStruct((3,) + p.shape[1:], p.dtype) for p in ps]


def _chips_sems(n):
    return [pltpu.SemaphoreType.DMA((3 * n,)), pltpu.SemaphoreType.DMA((3 * n,))]


def _chips_phases(p_refs, out_refs, send_sems, recv_sems):
    n = len(p_refs)

    def copies():
        x, y, c = _coords()
        chips = [(1 - x, y), (x, 1 - y), (1 - x, 1 - y)]
        return [pltpu.make_async_remote_copy(
            src_ref=p_refs[a].at[2 * px + py], dst_ref=out_refs[a].at[k],
            send_sem=send_sems.at[3 * a + k], recv_sem=recv_sems.at[3 * a + k],
            device_id=(px, py, c), device_id_type=MESH)
            for a in range(n) for k, (px, py) in enumerate(chips)]

    def start():
        for cp in copies():
            cp.start()

    def finish():
        cps = copies()
        for cp in cps:
            cp.wait_recv()
        for cp in cps:
            cp.wait_send()

    return start, finish


def _row_tile(R):
    tr = 256
    while R % tr:
        tr //= 2
    assert tr % 8 == 0
    return tr


def add_sibling(name, g, recv, c_idx):
    _, R, C = g.shape
    tr = _row_tile(R)

    def body(c_ref, g_ref, r_ref, o_ref, o16_ref):
        s = g_ref[...] + r_ref[...].astype(F32)
        o_ref[...] = s
        o16_ref[...] = s.astype(BF16)

    out = pl.BlockSpec((None, tr, C), lambda q, i, cr: (q, i, 0))
    return pl.pallas_call(
        body,
        grid_spec=pltpu.PrefetchScalarGridSpec(
            num_scalar_prefetch=1, grid=(4, R // tr),
            in_specs=[pl.BlockSpec((None, tr, C), lambda q, i, cr: (2 * q + cr[0], i, 0)), out],
            out_specs=[out, out]),
        out_shape=[jax.ShapeDtypeStruct((4, R, C), F32), jax.ShapeDtypeStruct((4, R, C), BF16)], name=name,
        compiler_params=_cparams(("arbitrary", "arbitrary")),
    )(c_idx, g, recv)


def _adam_math(w, g, m, v):
    m = ADAM_B1 * m + (1.0 - ADAM_B1) * g
    v = ADAM_B2 * v + (1.0 - ADAM_B2) * jnp.square(g)
    m_hat = m / (1.0 - ADAM_B1 ** ADAM_STEP)
    v_hat = v / (1.0 - ADAM_B2 ** ADAM_STEP)
    delta = -ADAM_LR * (m_hat / (jnp.sqrt(v_hat) + ADAM_EPS) + ADAM_WD * w)
    return delta, m, v


def adam_big(name, p1, recv, w, m, v, chip_idx, layer=0):
    _, R, C = p1.shape
    tr = _row_tile(R)
    nt = R // tr

    def body(q_ref, p_ref, r_ref, w_ref, m_ref, v_ref, g_ref, d_ref, nm_ref, nv_ref):
        g = ((p_ref[...] + r_ref[0].astype(F32)) + r_ref[1].astype(F32)) + r_ref[2].astype(F32)
        d, nm, nv = _adam_math(w_ref[...], g, m_ref[...], v_ref[...])
        g_ref[...] = g
        d_ref[...] = d
        nm_ref[...] = nm
        nv_ref[...] = nv

    row_in = pl.BlockSpec((tr, C), lambda i, qr: (layer * nt + i, 0))
    row = pl.BlockSpec((tr, C), lambda i, qr: (i, 0))
    return pl.pallas_call(
        body,
        grid_spec=pltpu.PrefetchScalarGridSpec(
            num_scalar_prefetch=1, grid=(nt,),
            in_specs=[pl.BlockSpec((None, tr, C), lambda i, qr: (qr[0], i, 0)),
                      pl.BlockSpec((3, tr, C), lambda i, qr: (0, i, 0)), row_in, row_in, row_in],
            out_specs=[row, row, row, row]),
        out_shape=[jax.ShapeDtypeStruct((R, C), F32)] * 4, name=name,
        compiler_params=_cparams(("arbitrary",)),
    )(chip_idx, p1, recv, w, m, v)


def sum8(parts):
    _, R, C = parts.shape

    def body(p_ref, o_ref):
        acc = p_ref[0]
        for k in range(1, N_DEV):
            acc = acc + p_ref[k]
        o_ref[...] = acc

    tr = 128
    while R % tr:
        tr //= 2
    assert tr % 8 == 0
    return pl.pallas_call(
        body, grid=(R // tr,), in_specs=[pl.BlockSpec((N_DEV, tr, C), lambda i: (0, i, 0))],
        out_specs=pl.BlockSpec((tr, C), lambda i: (i, 0)), out_shape=jax.ShapeDtypeStruct((R, C), F32),
        name="sum8", compiler_params=_cparams(("arbitrary",)),
    )(parts)


def adam_small(w, g, m, v):
    def fn(wt, gt, mt, vt):
        return _adam_math(wt, gt, mt, vt)
    C = w.shape[1]
    return rowwise("adam_small", fn, [w, g, m, v], [], [(C, F32)] * 3, tr=128)


def _pack(arrs, rows_mult=8):
    flat = jnp.concatenate([a.reshape(-1) for a in arrs])
    n = flat.shape[0]
    per = PACK_C * rows_mult
    pad = (-n) % per
    if pad:
        flat = jnp.concatenate([flat, jnp.zeros((pad,), flat.dtype)])
    return flat.reshape(-1, PACK_C)


def _unpack(buf, shapes):
    flat = buf.reshape(-1)
    out, off = [], 0
    for s in shapes:
        n = math.prod(s)
        out.append(flat[off:off + n].reshape(s))
        off += n
    return out


def _blocked(gfull, axis):
    shp = gfull.shape
    n = shp[axis] // N_DEV
    t = gfull.reshape(shp[:axis] + (N_DEV, n) + shp[axis + 1:])
    t = jnp.moveaxis(t, axis, 0)
    return t.reshape(N_DEV, -1)


def _unblocked(gathered, shard_shape, axis):
    t = jnp.moveaxis(gathered, 0, axis)
    shp = shard_shape[:axis] + (N_DEV * shard_shape[axis],) + shard_shape[axis + 1:]
    return t.reshape(shp)


def _relu2_epi(acc):
    r = jnp.maximum(acc, 0.0)
    return acc, r * r


def _step(x3, target3, W, comm):
    bsz, S, D = x3.shape
    T = bsz * S
    x = x3.reshape(T, D)
    target = target3.reshape(T, D)
    row = lambda v: v.reshape(1, -1)
    grads = {}

    s5p = (W['ssm_a_re'][0], W['ssm_a_im'][0], W['ssm_log_dt'][0], W['ssm_b_re'][0], W['ssm_b_im'][0])
    (lam_re, lam_im, bb_re, bb_im), s5_disc_vjp = jax.vjp(s5_disc, *s5p)
    pwr, pwi, l2r, l2i = s5_tables(lam_re, lam_im, S5_SUB)
    bre, bim = _s5_blockdiag_b(bb_re).astype(BF16), _s5_blockdiag_b(bb_im).astype(BF16)
    cre, cim = _s5_blockdiag_c(W['ssm_c_re'][0]).astype(BF16), _s5_blockdiag_c(W['ssm_c_im'][0]).astype(BF16)
    dskip = W['ssm_d']

    tril = jnp.tril(jnp.ones((GMLP_CHUNK, GMLP_CHUNK), bool))
    ws = jnp.where(tril[None], W['gmlp_w_s'][0], 0.0).astype(BF16)
    hw = D // GMLP_HEADS
    sbias = jnp.repeat(W['gmlp_b_s'][0].T, hw, axis=1)

    saved = []
    def add_norm(acc, *ex):
        xn = acc + ex[0] + ex[1] if len(ex) == 3 else acc + ex[0]
        return xn, _rms(xn, ex[-1])

    for i in range(DEPTH):
        sv = {'x': x}
        nm = W['norm_mix'][i:i + 1]
        nl = W['norm_mlp'][i:i + 1]
        if i == 0:
            h, hf = rms_fwd("rms_mix0", x, nm, want_f32=True)
            res = s5_fwd(hf, bre, bim, cre, cim, pwr, pwi, l2r, l2i, dskip, bsz, gather=comm.gather_list)
            ypre, gy, xs, xr_all, xi_all = res[:5]
            Wfull, Wsh = comm.weights(res[5:])
            W = {**W, **Wsh}
            conv_w = jnp.concatenate([W['conv_w_dw'][0], jnp.zeros((1, D), F32)], axis=0)
            z, = matmul("s5_glu_mm", gy, Wfull['ssm_w_glu'], mode='cb')
            def s5_glu(zt, xt, g):
                xn = xt + _glu(zt)
                return xn, _rms(xn, g)
            x1, h2 = rowwise("s5_glu", s5_glu, [z, x], [nl], [(D, F32), (D, BF16)])
            sv.update(hf=hf, ypre=ypre, gy=gy, xs=xs, xr=xr_all, xi=xi_all, z=z)
        elif i == 1:
            z, = matmul("conv_pw1", h, Wfull['conv_w_pw1'], mode='cb', epi=lambda acc, b: (acc + b,),
                        extras=[(W['conv_b_pw1'], 'row')])
            zg, = rowwise("conv_glu", _glu, [z], [], [(D, F32)])
            yc = dwconv_fwd(zg, conv_w, W['conv_b_dw'], bsz)
            y2, = rowwise("conv_ln_silu", lambda t, g, b: jax.nn.silu(_ln(t, g, b)), [yc],
                          [W['conv_ln_g'], W['conv_ln_b']], [(D, BF16)])
            x1, h2 = matmul("conv_pw2", y2, Wfull['conv_w_pw2'], epi=add_norm, whole_rows=True,
                            extras=[(W['conv_b_pw2'], 'row'), (x, 'tile'), (nl, 'row')], out_dtypes=(F32, BF16))
            sv.update(h=h, z=z, zg=zg, yc=yc, y2=y2)
        elif i == 2:
            zp, = matmul("gmlp_in", h, Wfull['gmlp_w_in'], mode='cb')

            def gm_pre(zt, g, b):
                a = jax.nn.gelu(zt)
                return a[:, :D], _ln(a[:, D:], g, b)
            u, vln = rowwise("gmlp_pre", gm_pre, [zp], [W['gmlp_ln_g'], W['gmlp_ln_b']], [(D, F32), (D, BF16)])
            gated = spatial_fwd(u, vln, ws, sbias)
            x1, h2 = matmul("gmlp_out", gated, Wfull['gmlp_w_out'], epi=add_norm, whole_rows=True,
                            extras=[(x, 'tile'), (nl, 'row')], out_dtypes=(F32, BF16))
            sv.update(h=h, zp=zp, u=u, vln=vln, gated=gated)
        else:
            qkv = qkv_matmul("attn_qkv", h, Wfull['attn_w_qkv'])
            ng = len(ATT_CONFIGS)
            outs, lses, blocks = [], [], []
            for gi, (window, dil) in enumerate(ATT_CONFIGS):
                qb, kb, vb = (qkv[j * ng + gi] for j in range(3))
                ob, lb = attn_fwd("attn_fwd%d" % gi, qb, kb, vb, dil, bsz)
                blocks.append((qb, kb, vb, lb, dil))
                outs.append(ob)
                lses.append(lb)

            def merge(o0, o1, o2, l0, l1, l2):
                m = jnp.maximum(jnp.maximum(l0, l1), l2)
                e0, e1, e2 = jnp.exp(l0 - m), jnp.exp(l1 - m), jnp.exp(l2 - m)
                inv = 1.0 / (e0 + e1 + e2)
                w0, w1, w2 = e0 * inv, e1 * inv, e2 * inv
                return w0 * o0 + w1 * o1 + w2 * o2, w0, w1, w2
            merged, w0, w1, w2 = rowwise("attn_merge", merge, outs + lses, [],
                                         [(ATT_W, BF16), (ATT_W, F32), (ATT_W, F32), (ATT_W, F32)])
            wo = Wfull['attn_w_o']
            wo_nat = wo.transpose(1, 0, 2).reshape(wo.shape[1], N_DEV * wo.shape[2])
            x1, h2 = matmul("attn_o", merged, wo_nat, epi=add_norm, whole_rows=True,
                            extras=[(x, 'tile'), (nl, 'row')], out_dtypes=(F32, BF16))
            sv.update(h=h, blocks=blocks, merged=merged, wts=(w0, w1, w2))
        a, act = matmul("mlp_in%d" % i, h2, Wfull['mlp_w_in'][i], mode='cb', epi=_relu2_epi, out_dtypes=(BF16, BF16))
        if i + 1 < DEPTH:
            x2, h = matmul("mlp_out%d" % i, act, Wfull['mlp_w_out'][i], epi=add_norm, whole_rows=True,
                           extras=[(x1, 'tile'), (W['norm_mix'][i + 1:i + 2], 'row')], out_dtypes=(F32, BF16))
        else:
            x2, = matmul("mlp_out%d" % i, act, Wfull['mlp_w_out'][i], epi=lambda acc, r: (acc + r,),
                         extras=[(x1, 'tile')])
        sv.update(x1=x1, h2=h2, a=a, act=act)
        saved.append(sv)
        x = x2

    def loss_fn(xt, tt, g):
        y, vjp = jax.vjp(_rms, xt, g)
        err = y - tt
        dxx, dg = vjp(err * (1.0 / D))
        lval = jnp.sum(jnp.sum(err * err, axis=1, keepdims=True), axis=0, keepdims=True) * (0.5 / D)
        return dxx, dxx, jnp.broadcast_to(lval, (1, 128)), dg
    dx, dxb, lacc, dnf = rowwise("loss_head", loss_fn, [x, target], [row(W['norm_final'])],
                                 [(D, F32), (D, BF16)], [((1, 128), F32), ((1, D), F32)])
    loss_local = lacc[0, 0]
    grads['norm_final'] = dnf.reshape(-1)

    g_norm_mix, g_norm_mlp = [None] * DEPTH, [None] * DEPTH
    g_mlp_in, g_mlp_out = [None] * DEPTH, [None] * DEPTH
    nl_all = [W['norm_mlp'][i:i + 1] for i in range(DEPTH)]
    nm_all = [W['norm_mix'][i:i + 1] for i in range(DEPTH)]

    def norm_bwd(xt, dres, g):
        def epi(dh, xv, dr, gv):
            _, vjp = jax.vjp(_rms, xv, gv)
            dxv, dgv = vjp(dh)
            dxv = dxv + dr
            return dxv, dxv, dgv
        return dict(epi=epi, extras=[xt, dres], params=[g], out_dtypes=(F32, BF16), acc_out=[((1, D), F32)])

    for i in reversed(range(DEPTH)):
        sv = saved[i]
        da, = matmul("mlp_out_bwd%d" % i, dxb, Wfull['mlp_w_out'][i], mode='nt',
                     epi=lambda acc, av: (acc * (2.0 * jnp.maximum(av.astype(F32), 0.0)),),
                     extras=[(sv['a'], 'tile')], out_dtypes=(BF16,))
        g_mlp_out[i] = _rows_blocked(wgrad("mlp_out_wg%d" % i, sv['act'], dxb))
        dx, dxb, dg = matmul_nt_cb("mlp_in_bwd%d" % i, da, Wfull['mlp_w_in'][i], **norm_bwd(sv['x1'], dx, nl_all[i]))
        g_mlp_in[i] = wgrad("mlp_in_wg%d" % i, sv['h2'], da, cb=True)
        g_norm_mlp[i] = dg.reshape(-1)
        xin = sv['x']
        if i == 0:
            dz, = rowwise("s5_glu_bwd", _glu_bwd, [sv['z'], dx], [], [(2 * D, BF16)])
            dgy, = matmul_nt_cb("s5_glu_mm_bwd", dz, Wfull['ssm_w_glu'])
            grads['ssm_w_glu'] = wgrad("s5_glu_wg", sv['gy'], dz, cb=True)

            def gelu_bwd(yt, dt):
                _, vjp = jax.vjp(jax.nn.gelu, yt)
                return vjp(dt)[0]
            dypre, = rowwise("s5_gelu_bwd", gelu_bwd, [sv['ypre'], dgy], [], [(D, F32)])
            grads['mlp_w_in'], grads['mlp_w_out'] = g_mlp_in, g_mlp_out
            res = s5_bwd(sv['hf'], dypre, sv['xs'], sv['xr'], sv['xi'], bre, bim, cre, cim,
                         pwr[:, ::-1], pwi[:, ::-1], l2r, l2i, dskip, bsz, chips=comm.rs_front(grads))
            du, dbr, dbi, dcr, dci, dl, dd = res[:7]
            comm.recv2 = res[7:]
            dlam_re = dl[:, 0, :].reshape(SSM_GROUPS, SSM_STATE)
            dlam_im = dl[:, 1, :].reshape(SSM_GROUPS, SSM_STATE)
            s5_cot = (dlam_re, dlam_im, _s5_blockdiag_b_inv(dbr), _s5_blockdiag_b_inv(dbi))
            grads['ssm_c_re'] = _s5_blockdiag_c_inv(dcr)[None]
            grads['ssm_c_im'] = _s5_blockdiag_c_inv(dci)[None]
            grads['ssm_d'] = dd[0:1]
            dx, dxb, dg = rms_bwd("rms_mix_bwd0", xin, du, dx, nm_all[0])
        elif i == 1:
            dy2, = matmul("conv_pw2_bwd", dxb, Wfull['conv_w_pw2'], mode='nt')
            grads['conv_w_pw2'] = _rows_blocked(wgrad("conv_pw2_wg", sv['y2'], dxb))

            def ln_silu_bwd(yt, dt, dxt, g, b):
                _, vjp = jax.vjp(lambda t, gg, bb: jax.nn.silu(_ln(t, gg, bb)), yt, g, b)
                dyc, dgg, dbb = vjp(dt)
                return dyc, dgg, dbb, _colsum(dxt)
            dyc, dlg, dlb, dbp2 = rowwise("conv_ln_silu_bwd", ln_silu_bwd, [sv['yc'], dy2, dx],
                                          [W['conv_ln_g'], W['conv_ln_b']], [(D, F32)],
                                          [((1, D), F32), ((1, D), F32), ((1, D), F32)])
            grads['conv_ln_g'], grads['conv_ln_b'], grads['conv_b_pw2'] = dlg, dlb, dbp2
            dzg, dwd, dbd = dwconv_bwd(sv['zg'], dyc, conv_w, bsz)
            grads['conv_w_dw'] = dwd[None, :CONV_WIDTH]
            grads['conv_b_dw'] = dbd

            def glu_bwd1(zt, dyt):
                dzt = _glu_bwd(zt, dyt)
                return dzt, _colsum(dzt)
            dz, dbp1 = rowwise("conv_glu_bwd", glu_bwd1, [sv['z'], dzg], [], [(2 * D, BF16)], [((1, 2 * D), F32)])
            grads['conv_b_pw1'] = dbp1
            dx, dxb, dg = matmul_nt_cb("conv_pw1_bwd", dz, Wfull['conv_w_pw1'], **norm_bwd(xin, dx, nm_all[i]))
            grads['conv_w_pw1'] = wgrad("conv_pw1_wg", sv['h'], dz, cb=True)
        elif i == 2:
            dgt, = matmul("gmlp_out_bwd", dxb, Wfull['gmlp_w_out'], mode='nt', out_dtypes=(BF16,))
            grads['gmlp_w_out'] = _rows_blocked(wgrad("gmlp_out_wg", sv['gated'], dxb))
            du, dvln, dws, dsb = spatial_bwd(sv['u'], sv['vln'], dgt, ws, sbias)
            grads['gmlp_w_s'] = dws[None]
            grads['gmlp_b_s'] = dsb.reshape(GMLP_CHUNK, GMLP_HEADS, hw).sum(-1).T[None]

            def gm_pre_bwd(zt, dut, dvt, g, b):
                _, vjp_u = jax.vjp(jax.nn.gelu, zt[:, :D])
                _, vjp_v = jax.vjp(lambda zz, gg, bb: _ln(jax.nn.gelu(zz), gg, bb), zt[:, D:], g, b)
                dz2, dgg, dbb = vjp_v(dvt)
                return jnp.concatenate([vjp_u(dut)[0], dz2], axis=1), dgg, dbb
            dzp, dlg, dlb = rowwise("gmlp_pre_bwd", gm_pre_bwd, [sv['zp'], du, dvln],
                                    [W['gmlp_ln_g'], W['gmlp_ln_b']], [(2 * D, BF16)], [((1, D), F32), ((1, D), F32)])
            grads['gmlp_ln_g'], grads['gmlp_ln_b'] = dlg, dlb
            dx, dxb, dg = matmul_nt_cb("gmlp_in_bwd", dzp, Wfull['gmlp_w_in'], **norm_bwd(xin, dx, nm_all[i]))
            grads['gmlp_w_in'] = wgrad("gmlp_in_wg", sv['h'], dzp, cb=True)
        else:
            dm, = matmul_nt_cb("attn_o_bwd", dxb, Wfull['attn_w_o'])
            grads['attn_w_o'] = wgrad("attn_o_wg", sv['merged'], dxb, cb=True)
            w0, w1, w2 = sv['wts']
            do0, do1, do2 = rowwise("attn_merge_bwd", lambda d, a, b, c: (a * d, b * d, c * d), [dm, w0, w1, w2], [],
                                    [(ATT_W, BF16)] * 3)
            dparts = [[None] * 3 for _ in range(3)]
            for gi, (dog, (qb, kb, vb, lb, dil)) in enumerate(zip((do0, do1, do2), sv['blocks'])):
                dqb, dkb, dvb = attn_bwd("attn_bwd%d" % gi, qb, kb, vb, dog, sv['merged'], lb, dil, bsz)
                for j, t in enumerate((dqb, dkb, dvb)):
                    dparts[j][gi] = t
            dslots = [dparts[j][gi][:, hh * HEAD_DIM:(hh + 1) * HEAD_DIM]
                      for j in range(3) for gi in range(3) for hh in range(ATT_HEADS)]
            dqkv3 = _slots_to_blocked(dslots)
            dx, dxb, dg = matmul_nt_cb("attn_qkv_bwd", dqkv3, Wfull['attn_w_qkv'], a3=True,
                                       **norm_bwd(xin, dx, nm_all[i]))
            grads['attn_w_qkv'] = wgrad("attn_qkv_wg", sv['h'], dqkv3, cb=True, g3=True)
        g_norm_mix[i] = dg.reshape(-1)

    grads['norm_mix'] = jnp.stack(g_norm_mix)
    grads['norm_mlp'] = jnp.stack(g_norm_mlp)
    grads['mlp_w_in'] = g_mlp_in
    grads['mlp_w_out'] = g_mlp_out
    return loss_local, dx.reshape(bsz, S, D), grads, (s5_disc_vjp, s5_cot)


class _StepComm:
    def __init__(self, Wl, c_idx):
        self.Wl, self.c_idx = Wl, c_idx
        self.units = []
        for n in BIG:
            self.units += [(n, i) for i in range(DEPTH)] if Wl[n].shape[0] == DEPTH else [(n, None)]
        self.ss_names = list(SMALL_SHARDED)
        spack = _pack([Wl[n] for n in self.ss_names])
        self.gather_list = [Wl[n][0 if i is None else i].astype(BF16) for n, i in self.units] + [spack]
        self.p1 = self.recv2 = None

    @staticmethod
    def tag(n, i):
        return n if i is None else "%s%d" % (n, i)

    def weights(self, gathered):
        Wl = self.Wl
        Wfull = {}
        for (n, i), g in zip(self.units, gathered):
            w = g if BIG[n] == 2 else g.reshape(N_DEV * g.shape[1], g.shape[2])
            if i is None:
                Wfull[n] = w
            else:
                Wfull.setdefault(n, []).append(w)
        sparts = _unpack_gathered(gathered[-1], [Wl[n].shape for n in self.ss_names])
        Wsh = {n: _unblocked(p, Wl[n].shape, SMALL_SHARDED[n]) for n, p in zip(self.ss_names, sparts)}
        return Wfull, Wsh

    def rs_front(self, grads):
        pairs = [grads[n] if i is None else grads[n][i] for n, i in self.units]
        recv1 = exchange_sibling("rs_sibling", [p[1] for p in pairs])
        self.p1 = [add_sibling("add_sibling_" + self.tag(n, i), p[0], r, self.c_idx)
                   for (n, i), p, r in zip(self.units, pairs, recv1)]
        return [p[1] for p in self.p1]


def _rows_blocked(pair):
    return tuple(t.reshape(N_DEV, t.shape[0] // N_DEV, t.shape[1]) for t in pair)


def kernel(x, norm_mix, norm_mlp, norm_final, ssm_a_re, ssm_a_im, ssm_b_re, ssm_b_im, ssm_c_re, ssm_c_im, ssm_d, ssm_log_dt, ssm_w_glu, conv_w_pw1, conv_b_pw1, conv_w_dw, conv_b_dw, conv_ln_g, conv_ln_b, conv_w_pw2, conv_b_pw2, gmlp_w_in, gmlp_ln_g, gmlp_ln_b, gmlp_w_s, gmlp_b_s, gmlp_w_out, attn_w_qkv, attn_w_o, mlp_w_in, mlp_w_out, loss_target, m_norm_mix, m_norm_mlp, m_norm_final, m_ssm_a_re, m_ssm_a_im, m_ssm_b_re, m_ssm_b_im, m_ssm_c_re, m_ssm_c_im, m_ssm_d, m_ssm_log_dt, m_ssm_w_glu, m_conv_w_pw1, m_conv_b_pw1, m_conv_w_dw, m_conv_b_dw, m_conv_ln_g, m_conv_ln_b, m_conv_w_pw2, m_conv_b_pw2, m_gmlp_w_in, m_gmlp_ln_g, m_gmlp_ln_b, m_gmlp_w_s, m_gmlp_b_s, m_gmlp_w_out, m_attn_w_qkv, m_attn_w_o, m_mlp_w_in, m_mlp_w_out, v_norm_mix, v_norm_mlp, v_norm_final, v_ssm_a_re, v_ssm_a_im, v_ssm_b_re, v_ssm_b_im, v_ssm_c_re, v_ssm_c_im, v_ssm_d, v_ssm_log_dt, v_ssm_w_glu, v_conv_w_pw1, v_conv_b_pw1, v_conv_w_dw, v_conv_b_dw, v_conv_ln_g, v_conv_ln_b, v_conv_w_pw2, v_conv_b_pw2, v_gmlp_w_in, v_gmlp_ln_g, v_gmlp_ln_b, v_gmlp_w_s, v_gmlp_b_s, v_gmlp_w_out, v_attn_w_qkv, v_attn_w_o, v_mlp_w_in, v_mlp_w_out):
    args = (norm_mix, norm_mlp, norm_final, ssm_a_re, ssm_a_im, ssm_b_re, ssm_b_im, ssm_c_re, ssm_c_im, ssm_d,
            ssm_log_dt, ssm_w_glu, conv_w_pw1, conv_b_pw1, conv_w_dw, conv_b_dw, conv_ln_g, conv_ln_b, conv_w_pw2,
            conv_b_pw2, gmlp_w_in, gmlp_ln_g, gmlp_ln_b, gmlp_w_s, gmlp_b_s, gmlp_w_out, attn_w_qkv, attn_w_o,
            mlp_w_in, mlp_w_out)
    margs = (m_norm_mix, m_norm_mlp, m_norm_final, m_ssm_a_re, m_ssm_a_im, m_ssm_b_re, m_ssm_b_im, m_ssm_c_re,
             m_ssm_c_im, m_ssm_d, m_ssm_log_dt, m_ssm_w_glu, m_conv_w_pw1, m_conv_b_pw1, m_conv_w_dw, m_conv_b_dw,
             m_conv_ln_g, m_conv_ln_b, m_conv_w_pw2, m_conv_b_pw2, m_gmlp_w_in, m_gmlp_ln_g, m_gmlp_ln_b,
             m_gmlp_w_s, m_gmlp_b_s, m_gmlp_w_out, m_attn_w_qkv, m_attn_w_o, m_mlp_w_in, m_mlp_w_out)
    vargs = (v_norm_mix, v_norm_mlp, v_norm_final, v_ssm_a_re, v_ssm_a_im, v_ssm_b_re, v_ssm_b_im, v_ssm_c_re,
             v_ssm_c_im, v_ssm_d, v_ssm_log_dt, v_ssm_w_glu, v_conv_w_pw1, v_conv_b_pw1, v_conv_w_dw, v_conv_b_dw,
             v_conv_ln_g, v_conv_ln_b, v_conv_w_pw2, v_conv_b_pw2, v_gmlp_w_in, v_gmlp_ln_g, v_gmlp_ln_b,
             v_gmlp_w_s, v_gmlp_b_s, v_gmlp_w_out, v_attn_w_qkv, v_attn_w_o, v_mlp_w_in, v_mlp_w_out)
    Wl = dict(zip(WEIGHT_NAMES, args))
    Ml = dict(zip(WEIGHT_NAMES, margs))
    Vl = dict(zip(WEIGHT_NAMES, vargs))
    cx, cy, cc = _coords()
    my_idx = 4 * cx + 2 * cy + cc

    c_idx = cc.reshape(1).astype(jnp.int32)
    chip_idx = (2 * cx + cy).reshape(1).astype(jnp.int32)
    comm = _StepComm(Wl, c_idx)
    units, tag = comm.units, comm.tag
    W = {n: Wl[n] for n in SMALL if n not in SMALL_SHARDED}
    loss_local, grad_x, grads, (s5_disc_vjp, s5_cot) = _step(x, loss_target, W, comm)
    loss = lax.psum(loss_local, MESH_AXES)

    outs4 = {}
    for (n, i), p, r in zip(units, comm.p1, comm.recv2):
        w2, m2, v2 = (d[n].reshape(-1, d[n].shape[-1]) for d in (Wl, Ml, Vl))
        res = adam_big("adam_" + tag(n, i), p[0], r, w2, m2, v2, chip_idx, layer=0 if i is None else i)
        if i is None:
            outs4[n] = [t.reshape(Wl[n].shape) for t in res]
        else:
            outs4.setdefault(n, []).append(res)
    for n in BIG:
        if Wl[n].shape[0] == DEPTH:
            outs4[n] = [jnp.stack([layer[k] for layer in outs4[n]]) for k in range(4)]
    out_g = {n: outs4[n][0] for n in BIG}
    out_d = {n: outs4[n][1] for n in BIG}
    out_m = {n: outs4[n][2] for n in BIG}
    out_v = {n: outs4[n][3] for n in BIG}

    s5_lin = ['ssm_a_re', 'ssm_a_im', 'ssm_log_dt', 'ssm_b_re', 'ssm_b_im']
    direct = [n for n in SMALL if n not in s5_lin]
    def full_shape(n):
        shp = list(Wl[n].shape)
        if n in SMALL_SHARDED:
            shp[SMALL_SHARDED[n]] *= N_DEV
        return tuple(shp)
    small_parts = [grads[n].reshape(full_shape(n)) for n in direct] + list(s5_cot)
    gsum = sum8(all_gather("gather_small_grads", [_pack(small_parts)])[0])
    summed = _unpack(gsum, [p.shape for p in small_parts])
    gsmall = dict(zip(direct, summed[:len(direct)]))
    s5g = s5_disc_vjp(tuple(summed[len(direct):]))
    for n, gval in zip(s5_lin, s5g):
        gsmall[n] = gval[None]
    for n, ax in SMALL_SHARDED.items():
        gsmall[n] = lax.dynamic_slice_in_dim(gsmall[n], my_idx * Wl[n].shape[ax], Wl[n].shape[ax], axis=ax)
    sm_shapes = [Wl[n].shape for n in SMALL]
    dS, mS, vS = adam_small(_pack([Wl[n] for n in SMALL]), _pack([gsmall[n] for n in SMALL]),
                            _pack([Ml[n] for n in SMALL]), _pack([Vl[n] for n in SMALL]))
    for n, gval in zip(SMALL, [gsmall[n] for n in SMALL]):
        out_g[n] = gval.reshape(Wl[n].shape)
    out_d.update(zip(SMALL, _unpack(dS, sm_shapes)))
    out_m.update(zip(SMALL, _unpack(mS, sm_shapes)))
    out_v.update(zip(SMALL, _unpack(vS, sm_shapes)))

    return (loss, grad_x, *[out_g[n] for n in WEIGHT_NAMES], *[out_d[n] for n in WEIGHT_NAMES],
            *[out_m[n] for n in WEIGHT_NAMES], *[out_v[n] for n in WEIGHT_NAMES])


def _unpack_gathered(g, shard_shapes):
    flat = g.reshape(N_DEV, -1)
    out, off = [], 0
    for s in shard_shapes:
        n = math.prod(s)
        out.append(flat[:, off:off + n].reshape((N_DEV,) + tuple(s)))
        off += n
    return out
```

```python
import functools
import math

import jax
import jax.numpy as jnp
from jax import lax
from jax.experimental import pallas as pl
from jax.experimental.pallas import tpu as pltpu

F32 = jnp.float32
BF16 = jnp.bfloat16

D_MODEL = 1024
DEPTH = 4
EPS = 1e-6
SSM_GROUP = 16
SSM_GROUPS = 64
SSM_STATE = 64
S5_GB = 8
S5_NGB = SSM_GROUPS // S5_GB
S5_CH = S5_GB * SSM_GROUP
S5_ST = S5_GB * SSM_STATE
S5_L = 256
CONV_WIDTH = 31
CONV_PAD = 32
CONV_TS = 256
CONV_CW = 256
GMLP_CHUNK = 128
GMLP_HEADS = 4
ATT_CONFIGS = ((128, 1), (512, 4), (2048, 16))
ATT_HEADS = 8
HEAD_DIM = 64
ATT_BLK = 128
ATT_TB = 2
ATT_W = ATT_HEADS * HEAD_DIM
N_DEV = 8
ADAM_LR = 0.001
ADAM_B1 = 0.9
ADAM_B2 = 0.999
ADAM_EPS = 1e-08
ADAM_WD = 0.01
ADAM_STEP = 10
VMEM_LIMIT = 56 * 1024 * 1024
PACK_C = 1024
MESH_AXES = ("x", "y", "c")
MESH = pl.DeviceIdType.MESH

WEIGHT_NAMES = ['norm_mix', 'norm_mlp', 'norm_final', 'ssm_a_re', 'ssm_a_im', 'ssm_b_re', 'ssm_b_im',
                'ssm_c_re', 'ssm_c_im', 'ssm_d', 'ssm_log_dt', 'ssm_w_glu', 'conv_w_pw1', 'conv_b_pw1',
                'conv_w_dw', 'conv_b_dw', 'conv_ln_g', 'conv_ln_b', 'conv_w_pw2', 'conv_b_pw2',
                'gmlp_w_in', 'gmlp_ln_g', 'gmlp_ln_b', 'gmlp_w_s', 'gmlp_b_s', 'gmlp_w_out',
                'attn_w_qkv', 'attn_w_o', 'mlp_w_in', 'mlp_w_out']
BIG = {'ssm_w_glu': 2, 'conv_w_pw1': 2, 'conv_w_pw2': 1, 'gmlp_w_in': 2, 'gmlp_w_out': 1,
       'attn_w_qkv': 2, 'attn_w_o': 2, 'mlp_w_in': 2, 'mlp_w_out': 1}
SMALL_SHARDED = {'conv_b_pw1': 1, 'conv_w_dw': 2, 'conv_b_dw': 1, 'conv_ln_g': 1, 'conv_ln_b': 1,
                 'conv_b_pw2': 1, 'gmlp_ln_g': 1, 'gmlp_ln_b': 1}
SMALL = [n for n in WEIGHT_NAMES if n not in BIG]


def _cparams(sem=None):
    return pltpu.CompilerParams(dimension_semantics=sem, vmem_limit_bytes=VMEM_LIMIT)


def _dot(a, b):
    return jnp.dot(a, b, preferred_element_type=F32)


def _dot_nt(a, b):
    return lax.dot_general(a, b, (((1,), (1,)), ((), ())), preferred_element_type=F32)


def _dot_tn(a, b):
    return lax.dot_general(a, b, (((0,), (0,)), ((), ())), preferred_element_type=F32)


ROW_TILE_BYTES = 10 << 20


def _rows_for(T, row_bytes, cap=1024):
    tr = min(cap, T)
    while tr > 8 and (T % tr or tr * row_bytes > ROW_TILE_BYTES):
        tr //= 2
    assert T % tr == 0 and tr % 8 == 0
    return tr


def rowwise(name, fn, rows, params, row_out, acc_out=(), tr=None):
    T = rows[0].shape[0]
    row_bytes = (sum(r.shape[1] * r.dtype.itemsize for r in rows)
                 + sum(c * jnp.dtype(dt).itemsize for c, dt in row_out))
    tr = _rows_for(T, row_bytes, cap=tr or 1024)
    nr, npar, nro = len(rows), len(params), len(row_out)

    def body(*refs):
        ins = [r[...] for r in refs[:nr + npar]]
        outs = refs[nr + npar:]
        res = fn(*ins)
        if not isinstance(res, (tuple, list)):
            res = (res,)
        for k in range(nro):
            outs[k][...] = res[k].astype(outs[k].dtype)
        if acc_out:
            @pl.when(pl.program_id(0) == 0)
            def _():
                for k in range(nro, len(outs)):
                    outs[k][...] = jnp.zeros_like(outs[k])
            for k in range(nro, len(outs)):
                outs[k][...] += res[k].astype(outs[k].dtype)

    in_specs = [pl.BlockSpec((tr, r.shape[1]), lambda i: (i, 0)) for r in rows]
    in_specs += [pl.BlockSpec(p.shape, lambda i, nd=p.ndim: (0,) * nd) for p in params]
    out_shape = [jax.ShapeDtypeStruct((T, c), dt) for c, dt in row_out]
    out_specs = [pl.BlockSpec((tr, c), lambda i: (i, 0)) for c, dt in row_out]
    out_shape += [jax.ShapeDtypeStruct(s, dt) for s, dt in acc_out]
    out_specs += [pl.BlockSpec(s, lambda i, nd=len(s): (0,) * nd) for s, dt in acc_out]
    res = pl.pallas_call(body, grid=(T // tr,), in_specs=in_specs, out_specs=out_specs, out_shape=out_shape,
                         name=name, compiler_params=_cparams(("arbitrary",)))(*rows, *params)
    return res


def _tile_m(M, K):
    tm = 2048
    while tm > 256 and tm * K * 2 > (4 << 20):
        tm //= 2
    return min(tm, M)


def matmul(name, a, b, *, mode='nn', epi=None, extras=(), out_dtypes=(F32,), out3=False, whole_rows=False):
    M, K = a.shape
    if mode == 'cb':
        nblk, _, tn = b.shape
        N = nblk * tn
    else:
        N = b.shape[0] if mode == 'nt' else b.shape[1]
        tn = N if whole_rows else min(512, N)
    row_bytes = (K * 2 + sum(N * jnp.dtype(dt).itemsize for dt in out_dtypes)
                 + sum(N * arr.dtype.itemsize for arr, kind in extras if kind == 'tile'))
    tm = _rows_for(M, row_bytes)
    assert N % tn == 0, (M, N, tm, tn)
    nex = len(extras)

    def body(a_ref, b_ref, *rest):
        ex_refs, outs = rest[:nex], rest[nex:]
        av = a_ref[...]
        for c in range(N // tn):
            cs = slice(c * tn, (c + 1) * tn)
            if mode == 'cb':
                acc = _dot(av, b_ref[c])
            elif mode == 'nt':
                acc = _dot_nt(av, b_ref[cs, :])
            else:
                acc = _dot(av, b_ref[:, cs])
            res = epi(acc, *[e[:, cs] for e in ex_refs]) if epi is not None else (acc,)
            for o, r in zip(outs, res):
                if out3:
                    o[c] = r.astype(o.dtype)
                else:
                    o[:, cs] = r.astype(o.dtype)

    in_specs = [pl.BlockSpec((tm, K), lambda i: (i, 0)), pl.BlockSpec(b.shape, lambda i, nd=b.ndim: (0,) * nd)]
    for arr, kind in extras:
        in_specs.append(pl.BlockSpec((tm, N), lambda i: (i, 0)) if kind == 'tile'
                        else pl.BlockSpec((1, N), lambda i: (0, 0)))
    if out3:
        out_shape = [jax.ShapeDtypeStruct((N // tn, M, tn), dt) for dt in out_dtypes]
        out_specs = [pl.BlockSpec((N // tn, tm, tn), lambda i: (0, i, 0)) for dt in out_dtypes]
    else:
        out_shape = [jax.ShapeDtypeStruct((M, N), dt) for dt in out_dtypes]
        out_specs = [pl.BlockSpec((tm, N), lambda i: (i, 0)) for dt in out_dtypes]
    return pl.pallas_call(body, grid=(M // tm,), in_specs=in_specs, out_specs=out_specs,
                          out_shape=out_shape, name=name,
                          compiler_params=_cparams(("arbitrary",)))(a, b, *[e[0] for e in extras])


def _gather_heads(refs, j, scr):
    for r in range(SLOTS_PER_DEV):
        k, hh = divmod(j * SLOTS_PER_DEV + r, ATT_HEADS)
        scr[:, r * HEAD_DIM:(r + 1) * HEAD_DIM] = refs[k][:, hh * HEAD_DIM:(hh + 1) * HEAD_DIM]
    return scr[...]


def matmul_nt_cb(name, a, b, *, heads=False, epi=None, extras=(), params=(), out_dtypes=(F32,), acc_out=()):
    nblk, K, n = b.shape
    a_list = list(a) if heads else [a]
    na = len(a_list)
    M = a_list[0].shape[0]
    tm = _tile_m(M, nblk * n)
    assert M % tm == 0
    nex, npar, nro = len(extras), len(params), len(out_dtypes)

    def body(*refs):
        a_refs, b_ref, rest = refs[:na], refs[na], refs[na + 1:]
        if heads:
            rest, scr = rest[:-2], rest[-2:]
        ex, outs = rest[:nex + npar], rest[nex + npar:]
        acc = None
        for j in range(nblk):
            aj = _gather_heads(a_refs, j, scr[j % 2]) if heads else a_refs[0][:, j * n:(j + 1) * n]
            part = _dot_nt(aj, b_ref[j])
            acc = part if acc is None else acc + part
        res = epi(acc, *[e[...] for e in ex]) if epi is not None else (acc,)
        for o, r in zip(outs[:nro], res[:nro]):
            o[...] = r.astype(o.dtype)
        if acc_out:
            @pl.when(pl.program_id(0) == 0)
            def _():
                for o in outs[nro:]:
                    o[...] = jnp.zeros_like(o)
            for o, r in zip(outs[nro:], res[nro:]):
                o[...] += r.astype(o.dtype)

    a_specs = [pl.BlockSpec((tm, t.shape[1]), lambda i: (i, 0)) for t in a_list]
    row = pl.BlockSpec((tm, K), lambda i: (i, 0))
    const = lambda shp: pl.BlockSpec(shp, lambda i, nd=len(shp): (0,) * nd)
    return pl.pallas_call(
        body, grid=(M // tm,),
        in_specs=a_specs + [pl.BlockSpec((nblk, K, n), lambda i: (0, 0, 0))] + [row] * nex
        + [const(p.shape) for p in params],
        out_specs=[row] * nro + [const(s) for s, dt in acc_out],
        out_shape=[jax.ShapeDtypeStruct((M, K), dt) for dt in out_dtypes]
        + [jax.ShapeDtypeStruct(s, dt) for s, dt in acc_out],
        scratch_shapes=[pltpu.VMEM((tm, n), BF16)] * 2 if heads else [],
        name=name, compiler_params=_cparams(("arbitrary",)))(*a_list, b, *extras, *params)


def wgrad(name, a, g, *, cb=False, heads=False):
    M, K = a.shape
    g_list = list(g) if heads else [g]
    tm, tk = min(M, 512 if heads else 1024), min(K, 512 if heads else 1024)
    if cb:
        n = SLOTS_PER_DEV * HEAD_DIM if heads else g.shape[1] // N_DEV
        nj = N_DEV
        while nj > 1 and nj * tk * n * 6 > (14 << 20):
            nj //= 2
        assert nj == N_DEV or not heads
        grid = (K // tk, N_DEV // nj, M // tm)
        g_specs = ([pl.BlockSpec((tm, t.shape[1]), lambda k, j, m: (m, 0)) for t in g_list] if heads
                   else [pl.BlockSpec((tm, nj * n), lambda k, j, m: (m, j))])
        o_spec = pl.BlockSpec((nj, tk, n), lambda k, j, m: (j, k, 0))
        o_shape = (N_DEV, K, n)
    else:
        N = g.shape[1]
        tn = min(N, 1024)
        nj = 1
        grid = (K // tk, N // tn, M // tm)
        g_specs = [pl.BlockSpec((tm, tn), lambda k, j, m: (m, j))]
        o_spec = pl.BlockSpec((tk, tn), lambda k, j, m: (k, j))
        o_shape = (K, N)
    nm = M // tm
    ng = len(g_list)

    def body(a_ref, *rest):
        g_refs, o_ref, o16_ref, scr = rest[:ng], rest[ng], rest[ng + 1], rest[ng + 2:]
        m = pl.program_id(2)

        @pl.when(m == 0)
        def _():
            o_ref[...] = jnp.zeros_like(o_ref)
        at = a_ref[...].T
        if cb:
            for jj in range(nj):
                gj = (_gather_heads(g_refs, jj, scr[jj % 2]) if heads
                      else g_refs[0][:, jj * n:(jj + 1) * n])
                o_ref[jj] += _dot(at, gj)
        else:
            o_ref[...] += _dot(at, g_refs[0][...])

        @pl.when(m == nm - 1)
        def _():
            o16_ref[...] = o_ref[...].astype(BF16)

    return pl.pallas_call(
        body, grid=grid, in_specs=[pl.BlockSpec((tm, tk), lambda k, j, m: (m, k))] + g_specs,
        out_specs=[o_spec, o_spec],
        out_shape=[jax.ShapeDtypeStruct(o_shape, F32), jax.ShapeDtypeStruct(o_shape, BF16)],
        scratch_shapes=[pltpu.VMEM((tm, SLOTS_PER_DEV * HEAD_DIM), BF16)] * 2 if heads else [],
        name=name, compiler_params=_cparams(("arbitrary", "arbitrary", "arbitrary")))(a, *g_list)


def _rms(x, g):
    x = x.astype(F32)
    return x * lax.rsqrt(jnp.mean(x * x, axis=-1, keepdims=True) + EPS) * g


def _ln(x, g, b):
    mu = jnp.mean(x, axis=-1, keepdims=True)
    var = jnp.mean(jnp.square(x - mu), axis=-1, keepdims=True)
    return (x - mu) * lax.rsqrt(var + EPS) * g + b


def _glu(z):
    d = z.shape[1] // 2
    return z[:, :d] * jax.nn.sigmoid(z[:, d:])


def _glu_bwd(z, dy):
    d = z.shape[1] // 2
    a, s = z[:, :d], jax.nn.sigmoid(z[:, d:])
    return jnp.concatenate([dy * s, dy * a * s * (1.0 - s)], axis=1)


def _colsum(v):
    return jnp.sum(v.astype(F32), axis=0, keepdims=True)


def rms_fwd(name, x, g, want_f32=False):
    def fn(xt, gt):
        h = _rms(xt, gt)
        return (h, h) if want_f32 else (h,)
    D = x.shape[1]
    outs = [(D, BF16)] + ([(D, F32)] if want_f32 else [])
    return rowwise(name, fn, [x], [g], outs)


def rms_bwd(name, x, dh, dres, g):
    def fn(xt, dht, drt, gt):
        _, vjp = jax.vjp(_rms, xt, gt)
        dx, dg = vjp(dht.astype(F32))
        dx = dx + drt
        return dx, dx, dg
    D = x.shape[1]
    return rowwise(name, fn, [x, dh, dres], [g], [(D, F32), (D, BF16)], [((1, D), F32)])


def s5_disc(a_re, a_im, log_dt, b_re, b_im):
    dt = jnp.exp(log_dt)[:, None]
    er = jnp.exp(a_re * dt)
    lam_re = er * jnp.cos(a_im * dt)
    lam_im = er * jnp.sin(a_im * dt)
    nr, ni = lam_re - 1.0, lam_im
    den = a_re * a_re + a_im * a_im
    f_re = (nr * a_re + ni * a_im) / den
    f_im = (ni * a_re - nr * a_im) / den
    bb_re = f_re[..., None] * b_re - f_im[..., None] * b_im
    bb_im = f_re[..., None] * b_im + f_im[..., None] * b_re
    return lam_re, lam_im, bb_re, bb_im


def _s5_blockdiag_b(bb):
    t = bb.reshape(S5_NGB, S5_GB, SSM_STATE, SSM_GROUP).transpose(0, 1, 3, 2)
    eye = jnp.eye(S5_GB, dtype=bb.dtype)
    return jnp.einsum('bgpn,gh->bgphn', t, eye).reshape(S5_NGB, S5_CH, S5_ST)


def _s5_blockdiag_b_inv(x):
    t = x.reshape(S5_NGB, S5_GB, SSM_GROUP, S5_GB, SSM_STATE)
    eye = jnp.eye(S5_GB, dtype=x.dtype)
    d = jnp.einsum('bgphn,gh->bgpn', t, eye)
    return d.transpose(0, 1, 3, 2).reshape(SSM_GROUPS, SSM_STATE, SSM_GROUP)


def _s5_blockdiag_c(c):
    t = c.reshape(S5_NGB, S5_GB, SSM_GROUP, SSM_STATE).transpose(0, 1, 3, 2)
    eye = jnp.eye(S5_GB, dtype=c.dtype)
    return jnp.einsum('bgnp,gh->bgnhp', t, eye).reshape(S5_NGB, S5_ST, S5_CH)


def _s5_blockdiag_c_inv(x):
    t = x.reshape(S5_NGB, S5_GB, SSM_STATE, S5_GB, SSM_GROUP)
    eye = jnp.eye(S5_GB, dtype=x.dtype)
    d = jnp.einsum('bgnhp,gh->bgnp', t, eye)
    return d.transpose(0, 1, 3, 2).reshape(SSM_GROUPS, SSM_GROUP, SSM_STATE)


def s5_tables(lam_re, lam_im, L):
    pr, pi = lam_re.reshape(1, -1), lam_im.reshape(1, -1)
    n = 1
    while n < L:
        lr, li = pr[n - 1:n], pi[n - 1:n]
        pr, pi = (jnp.concatenate([pr, pr * lr - pi * li], 0), jnp.concatenate([pi, pr * li + pi * lr], 0))
        n *= 2
    nk = int(math.log2(L))
    idx = [2 ** k - 1 for k in range(nk)] + [0] * (8 - nk)

    def blk(t):
        return t.reshape(t.shape[0], S5_NGB, S5_ST).transpose(1, 0, 2)

    def rows(t):
        return jnp.concatenate([t[j:j + 1] for j in idx], axis=0)
    return blk(pr), blk(pi), blk(rows(pr)), blk(rows(pi))


S5_SUB = 8


def _scan_tiles(br, bi, a2r, a2i, reverse):
    L = br.shape[0]
    sub = lax.broadcasted_iota(jnp.int32, br.shape, 0) & (S5_SUB - 1)
    xr, xi = br, bi
    for k in range(3):
        s = 1 << k
        ar, ai = a2r[k:k + 1, :], a2i[k:k + 1, :]
        if reverse:
            sr, si = pltpu.roll(xr, L - s, 0), pltpu.roll(xi, L - s, 0)
            m = sub < S5_SUB - s
        else:
            sr, si = pltpu.roll(xr, s, 0), pltpu.roll(xi, s, 0)
            m = sub >= s
        sr, si = jnp.where(m, sr, 0.0), jnp.where(m, si, 0.0)
        xr, xi = xr + ar * sr - ai * si, xi + ar * si + ai * sr
    return xr, xi


def _scan_chain(xr, xi, pr, pi, cr, ci, out_r, out_i, reverse):
    ntile = xr.shape[0] // S5_SUB
    for g in (reversed(range(ntile)) if reverse else range(ntile)):
        rs = slice(g * S5_SUB, (g + 1) * S5_SUB)
        if reverse:
            nr = xr[rs] + pr * cr + pi * ci
            ni = xi[rs] + pr * ci - pi * cr
            cr, ci = nr[0:1], ni[0:1]
        else:
            nr = xr[rs] + pr * cr - pi * ci
            ni = xi[rs] + pr * ci + pi * cr
            cr, ci = nr[S5_SUB - 1:S5_SUB], ni[S5_SUB - 1:S5_SUB]
        out_r[rs, :] = nr
        out_i[rs, :] = ni
    return cr, ci


def _grid_step(shape):
    s = 0
    for ax, n in enumerate(shape):
        s = s * n + pl.program_id(ax)
    return s


def s5_fwd(h, bre, bim, cre, cim, pwr, pwi, l2r, l2i, dskip, bsz, gather=()):
    T, D = h.shape
    L = S5_L
    S = T // bsz
    NC = S // L
    ng = len(gather)
    grid = (S5_NGB, bsz, NC)
    nsteps = S5_NGB * bsz * NC
    fwd_step = nsteps - max(1, nsteps // 32)

    def body(*refs):
        (h_ref, bre_ref, bim_ref, cre_ref, cim_ref, pwr_ref, pwi_ref, l2r_ref, l2i_ref, d_ref) = refs[:10]
        x_refs = refs[10:10 + ng]
        y_ref, gy_ref, xs_ref, xr_s, xi_s = refs[10 + ng:15 + ng]
        g_refs = refs[15 + ng:15 + 2 * ng]
        car_r, car_i = refs[15 + 2 * ng:17 + 2 * ng]
        if ng:
            start, forward, finish = _gather_phases(x_refs, g_refs, *refs[17 + 2 * ng:])
            step = _grid_step(grid)
            pl.when(step == 0)(start)
            pl.when(step == fwd_step)(forward)

        @pl.when(pl.program_id(2) == 0)
        def _():
            car_r[...] = jnp.zeros_like(car_r)
            car_i[...] = jnp.zeros_like(car_i)
        u = h_ref[...]
        ub = u.astype(BF16)
        cr, ci = car_r[0:1, :], car_i[0:1, :]
        xs_ref[...] = jnp.zeros_like(xs_ref)
        xs_ref[0:1, :] = cr
        xs_ref[1:2, :] = ci
        xr, xi = _scan_tiles(_dot(ub, bre_ref[...]), _dot(ub, bim_ref[...]), l2r_ref[...], l2i_ref[...], False)
        cr, ci = _scan_chain(xr, xi, pwr_ref[...], pwi_ref[...], cr, ci, xr_s, xi_s, False)
        car_r[...] = jnp.broadcast_to(cr, car_r.shape)
        car_i[...] = jnp.broadcast_to(ci, car_i.shape)
        y = (_dot(xr_s[...].astype(BF16), cre_ref[...]) - _dot(xi_s[...].astype(BF16), cim_ref[...])
             + d_ref[...] * u)
        y_ref[...] = y
        gy_ref[...] = jax.nn.gelu(y).astype(BF16)
        if ng:
            pl.when(step == nsteps - 1)(finish)

    tok = lambda g, b, c: (b * NC + c, g)
    par = lambda g, b, c: (g, 0, 0)
    anyspec = pl.BlockSpec(memory_space=pl.ANY)
    return pl.pallas_call(
        body, grid=grid,
        in_specs=[pl.BlockSpec((L, S5_CH), tok),
                  pl.BlockSpec((None, S5_CH, S5_ST), par), pl.BlockSpec((None, S5_CH, S5_ST), par),
                  pl.BlockSpec((None, S5_ST, S5_CH), par), pl.BlockSpec((None, S5_ST, S5_CH), par),
                  pl.BlockSpec((None, 8, S5_ST), par), pl.BlockSpec((None, 8, S5_ST), par),
                  pl.BlockSpec((None, 8, S5_ST), par), pl.BlockSpec((None, 8, S5_ST), par),
                  pl.BlockSpec((1, S5_CH), lambda g, b, c: (0, g))] + [anyspec] * ng,
        out_specs=[pl.BlockSpec((L, S5_CH), tok), pl.BlockSpec((L, S5_CH), tok),
                   pl.BlockSpec((None, 8, S5_ST), lambda g, b, c: (b * NC + c, 0, g)),
                   pl.BlockSpec((L, S5_ST), tok), pl.BlockSpec((L, S5_ST), tok)] + [anyspec] * ng,
        out_shape=[jax.ShapeDtypeStruct((T, D), F32), jax.ShapeDtypeStruct((T, D), BF16),
                   jax.ShapeDtypeStruct((bsz * NC, 8, S5_NGB * S5_ST), F32),
                   jax.ShapeDtypeStruct((T, S5_NGB * S5_ST), F32), jax.ShapeDtypeStruct((T, S5_NGB * S5_ST), F32)]
        + _gather_out_shapes(gather),
        scratch_shapes=[pltpu.VMEM((8, S5_ST), F32), pltpu.VMEM((8, S5_ST), F32)]
        + (_gather_sems(ng) if ng else []),
        name="s5_fwd", compiler_params=_cparams(("arbitrary", "arbitrary", "arbitrary")),
    )(h, bre, bim, cre, cim, pwr, pwi, l2r, l2i, dskip, *gather)


def s5_bwd(h, dy, xs, xr, xi, bre, bim, cre, cim, pwr_rev, pwi_rev, l2r, l2i, dskip, bsz, chips=()):
    T, D = h.shape
    L = S5_L
    S = T // bsz
    NC = S // L
    nc = len(chips)
    grid = (S5_NGB, bsz, NC)
    nsteps = S5_NGB * bsz * NC

    def body(*refs):
        (h_ref, dy_ref, xs_ref, xr_ref, xi_ref, bre_ref, bim_ref, cre_ref, cim_ref, qr_ref, qi_ref,
         l2r_ref, l2i_ref, d_ref) = refs[:14]
        p_refs = refs[14:14 + nc]
        du_ref, dbr_ref, dbi_ref, dcr_ref, dci_ref, dl_ref, dd_ref = refs[14 + nc:21 + nc]
        r_refs = refs[21 + nc:21 + 2 * nc]
        car_r, car_i, dr_s, di_s = refs[21 + 2 * nc:25 + 2 * nc]
        if nc:
            start, finish = _chips_phases(p_refs, r_refs, *refs[25 + 2 * nc:])
            step = _grid_step(grid)
            pl.when(step == 0)(start)
        first = (pl.program_id(1) == 0) & (pl.program_id(2) == 0)

        @pl.when(first)
        def _():
            for r in (dbr_ref, dbi_ref, dcr_ref, dci_ref, dl_ref, dd_ref):
                r[...] = jnp.zeros_like(r)

        @pl.when(pl.program_id(2) == 0)
        def _():
            car_r[...] = jnp.zeros_like(car_r)
            car_i[...] = jnp.zeros_like(car_i)

        u = h_ref[...]
        ub = u.astype(BF16)
        dyv = dy_ref[...]
        dyb = dyv.astype(BF16)
        l2r_v, l2i_v = l2r_ref[...], l2i_ref[...]
        x0r, x0i = xs_ref[0:1, :], xs_ref[1:2, :]
        xr, xi = xr_ref[...], xi_ref[...]
        gr = _dot_nt(dyb, cre_ref[...])
        gi = -_dot_nt(dyb, cim_ref[...])
        dr, di = _scan_tiles(gr, gi, l2r_v, -l2i_v, True)
        cr, ci = _scan_chain(dr, di, qr_ref[...], qi_ref[...], car_r[0:1, :], car_i[0:1, :], dr_s, di_s, True)
        dr, di = dr_s[...], di_s[...]
        car_r[...] = jnp.broadcast_to(cr, car_r.shape)
        car_i[...] = jnp.broadcast_to(ci, car_i.shape)
        row = lax.broadcasted_iota(jnp.int32, xr.shape, 0)
        xpr = jnp.where(row >= 1, pltpu.roll(xr, 1, 0), x0r)
        xpi = jnp.where(row >= 1, pltpu.roll(xi, 1, 0), x0i)
        dl_ref[0:1, :] += _colsum(dr * xpr + di * xpi)
        dl_ref[1:2, :] += _colsum(di * xpr - dr * xpi)
        drb, dib = dr.astype(BF16), di.astype(BF16)
        dcr_ref[...] += _dot_tn(xr.astype(BF16), dyb)
        dci_ref[...] -= _dot_tn(xi.astype(BF16), dyb)
        dbr_ref[...] += _dot_tn(ub, drb)
        dbi_ref[...] += _dot_tn(ub, dib)
        du_ref[...] = _dot_nt(drb, bre_ref[...]) + _dot_nt(dib, bim_ref[...]) + d_ref[...] * dyv
        dd_ref[0:1, :] += _colsum(dyv * u)
        if nc:
            pl.when(step == nsteps - 1)(finish)

    tok = lambda g, b, c: (b * NC + (NC - 1 - c), g)
    par = lambda g, b, c: (g, 0, 0)
    anyspec = pl.BlockSpec(memory_space=pl.ANY)
    return pl.pallas_call(
        body, grid=grid,
        in_specs=[pl.BlockSpec((L, S5_CH), tok), pl.BlockSpec((L, S5_CH), tok),
                  pl.BlockSpec((None, 8, S5_ST), lambda g, b, c: (b * NC + (NC - 1 - c), 0, g)),
                  pl.BlockSpec((L, S5_ST), tok), pl.BlockSpec((L, S5_ST), tok),
                  pl.BlockSpec((None, S5_CH, S5_ST), par), pl.BlockSpec((None, S5_CH, S5_ST), par),
                  pl.BlockSpec((None, S5_ST, S5_CH), par), pl.BlockSpec((None, S5_ST, S5_CH), par),
                  pl.BlockSpec((None, 8, S5_ST), par), pl.BlockSpec((None, 8, S5_ST), par),
                  pl.BlockSpec((None, 8, S5_ST), par), pl.BlockSpec((None, 8, S5_ST), par),
                  pl.BlockSpec((1, S5_CH), lambda g, b, c: (0, g))] + [anyspec] * nc,
        out_specs=[pl.BlockSpec((L, S5_CH), tok),
                   pl.BlockSpec((None, S5_CH, S5_ST), par), pl.BlockSpec((None, S5_CH, S5_ST), par),
                   pl.BlockSpec((None, S5_ST, S5_CH), par), pl.BlockSpec((None, S5_ST, S5_CH), par),
                   pl.BlockSpec((None, 8, S5_ST), par),
                   pl.BlockSpec((8, S5_CH), lambda g, b, c: (0, g))] + [anyspec] * nc,
        out_shape=[jax.ShapeDtypeStruct((T, D), F32),
                   jax.ShapeDtypeStruct((S5_NGB, S5_CH, S5_ST), F32), jax.ShapeDtypeStruct((S5_NGB, S5_CH, S5_ST), F32),
                   jax.ShapeDtypeStruct((S5_NGB, S5_ST, S5_CH), F32), jax.ShapeDtypeStruct((S5_NGB, S5_ST, S5_CH), F32),
                   jax.ShapeDtypeStruct((S5_NGB, 8, S5_ST), F32), jax.ShapeDtypeStruct((8, D), F32)]
        + _chips_out_shapes(chips),
        scratch_shapes=[pltpu.VMEM((8, S5_ST), F32), pltpu.VMEM((8, S5_ST), F32)]
        + [pltpu.VMEM((L, S5_ST), F32)] * 2 + (_chips_sems(nc) if nc else []),
        name="s5_bwd", compiler_params=_cparams(("arbitrary", "arbitrary", "arbitrary")),
    )(h, dy, xs, xr, xi, bre, bim, cre, cim, pwr_rev, pwi_rev, l2r, l2i, dskip, *chips)


def _shift_rows(win, off, n):
    if off == 0:
        return win[:n]
    return pltpu.roll(win, win.shape[0] - off, 0)[:n]


def dwconv_fwd(z, w, b, bsz):
    T, D = z.shape
    S = T // bsz
    TS, CW, PAD = CONV_TS, CONV_CW, CONV_PAD

    def body(z_ref, w_ref, b_ref, y_ref, zp):
        zp[0:PAD, :] = jnp.zeros((PAD, CW), F32)
        zp[PAD:, :] = z_ref[...]
        wv, bv = w_ref[...], b_ref[...]

        def step(t, carry):
            base = pl.multiple_of(t * TS, TS)
            win = zp[pl.ds(base, TS + PAD), :]
            acc = jnp.zeros((TS, CW), F32) + bv
            for k in range(CONV_WIDTH):
                acc = acc + wv[k:k + 1, :] * _shift_rows(win, PAD - (CONV_WIDTH - 1) + k, TS)
            y_ref[pl.ds(base, TS), :] = acc
            return carry
        lax.fori_loop(0, S // TS, step, 0)

    return pl.pallas_call(
        body, grid=(D // CW, bsz),
        in_specs=[pl.BlockSpec((S, CW), lambda c, bb: (bb, c)), pl.BlockSpec((32, CW), lambda c, bb: (0, c)),
                  pl.BlockSpec((1, CW), lambda c, bb: (0, c))],
        out_specs=pl.BlockSpec((S, CW), lambda c, bb: (bb, c)),
        out_shape=jax.ShapeDtypeStruct((T, D), F32),
        scratch_shapes=[pltpu.VMEM((S + PAD, CW), F32)],
        name="dwconv_fwd", compiler_params=_cparams(("arbitrary", "arbitrary")),
    )(z, w, b)


def dwconv_bwd(z, dy, w, bsz):
    T, D = z.shape
    S = T // bsz
    TS, CW, PAD = CONV_TS, CONV_CW, CONV_PAD

    def body(z_ref, dy_ref, w_ref, dz_ref, dw_ref, db_ref, zp, dyp):
        @pl.when(pl.program_id(1) == 0)
        def _():
            dw_ref[...] = jnp.zeros_like(dw_ref)
            db_ref[...] = jnp.zeros_like(db_ref)
        zp[0:PAD, :] = jnp.zeros((PAD, CW), F32)
        zp[PAD:, :] = z_ref[...]
        dyp[0:S, :] = dy_ref[...]
        dyp[S:, :] = jnp.zeros((PAD, CW), F32)
        wv = w_ref[...]

        def step(t, carry):
            base = pl.multiple_of(t * TS, TS)
            zwin = zp[pl.ds(base, TS + PAD), :]
            dwin = dyp[pl.ds(base, TS + PAD), :]
            dyt = dwin[:TS]
            acc = jnp.zeros((TS, CW), F32)
            for j in range(CONV_WIDTH):
                k = CONV_WIDTH - 1 - j
                acc = acc + wv[k:k + 1, :] * _shift_rows(dwin, j, TS)
            dz_ref[pl.ds(base, TS), :] = acc
            for k in range(CONV_WIDTH):
                prod = dyt * _shift_rows(zwin, PAD - (CONV_WIDTH - 1) + k, TS)
                dw_ref[8 * k:8 * k + 8, :] += jnp.sum(prod.reshape(TS // 8, 8, CW), axis=0)
            db_ref[...] += jnp.sum(dyt.reshape(TS // 8, 8, CW), axis=0)
            return carry
        lax.fori_loop(0, S // TS, step, 0)

    dz, dw, db = pl.pallas_call(
        body, grid=(D // CW, bsz),
        in_specs=[pl.BlockSpec((S, CW), lambda c, bb: (bb, c)), pl.BlockSpec((S, CW), lambda c, bb: (bb, c)),
                  pl.BlockSpec((32, CW), lambda c, bb: (0, c))],
        out_specs=[pl.BlockSpec((S, CW), lambda c, bb: (bb, c)), pl.BlockSpec((8 * 32, CW), lambda c, bb: (0, c)),
                   pl.BlockSpec((8, CW), lambda c, bb: (0, c))],
        out_shape=[jax.ShapeDtypeStruct((T, D), F32), jax.ShapeDtypeStruct((8 * 32, D), F32),
                   jax.ShapeDtypeStruct((8, D), F32)],
        scratch_shapes=[pltpu.VMEM((S + PAD, CW), F32), pltpu.VMEM((S + PAD, CW), F32)],
        name="dwconv_bwd", compiler_params=_cparams(("arbitrary", "arbitrary")),
    )(z, dy, w)
    return dz, dw.reshape(32, 8, D).sum(axis=1), db.sum(axis=0, keepdims=True)


def spatial_fwd(u, vln, ws, bias):
    T, E = u.shape
    C, H = GMLP_CHUNK, GMLP_HEADS
    hw = E // H

    def body(u_ref, v_ref, ws_ref, b_ref, o_ref):
        for hh in range(H):
            sl = slice(hh * hw, (hh + 1) * hw)
            vp = _dot(ws_ref[hh], v_ref[:, sl]) + b_ref[:, sl]
            o_ref[:, sl] = (u_ref[:, sl] * vp).astype(o_ref.dtype)

    return pl.pallas_call(
        body, grid=(T // C,),
        in_specs=[pl.BlockSpec((C, E), lambda i: (i, 0)), pl.BlockSpec((C, E), lambda i: (i, 0)),
                  pl.BlockSpec((H, C, C), lambda i: (0, 0, 0)), pl.BlockSpec((C, E), lambda i: (0, 0))],
        out_specs=pl.BlockSpec((C, E), lambda i: (i, 0)),
        out_shape=jax.ShapeDtypeStruct((T, E), BF16),
        name="spatial_fwd", compiler_params=_cparams(("arbitrary",)),
    )(u, vln, ws, bias)


def spatial_bwd(u, vln, dg, ws, bias):
    T, E = u.shape
    C, H = GMLP_CHUNK, GMLP_HEADS
    hw = E // H

    def body(u_ref, v_ref, dg_ref, ws_ref, b_ref, du_ref, dv_ref, dws_ref, db_ref):
        @pl.when(pl.program_id(0) == 0)
        def _():
            dws_ref[...] = jnp.zeros_like(dws_ref)
            db_ref[...] = jnp.zeros_like(db_ref)
        tril = (lax.broadcasted_iota(jnp.int32, (C, C), 1) <= lax.broadcasted_iota(jnp.int32, (C, C), 0))
        for hh in range(H):
            sl = slice(hh * hw, (hh + 1) * hw)
            v = v_ref[:, sl]
            w = ws_ref[hh]
            dgv = dg_ref[:, sl].astype(F32)
            vp = _dot(w, v) + b_ref[:, sl]
            du_ref[:, sl] = dgv * vp
            dvp = dgv * u_ref[:, sl]
            dvpb = dvp.astype(BF16)
            dv_ref[:, sl] = _dot_tn(w, dvpb)
            dws_ref[hh] += jnp.where(tril, _dot_nt(dvpb, v), 0.0)
            db_ref[:, sl] += dvp

    return pl.pallas_call(
        body, grid=(T // C,),
        in_specs=[pl.BlockSpec((C, E), lambda i: (i, 0)), pl.BlockSpec((C, E), lambda i: (i, 0)),
                  pl.BlockSpec((C, E), lambda i: (i, 0)),
                  pl.BlockSpec((H, C, C), lambda i: (0, 0, 0)), pl.BlockSpec((C, E), lambda i: (0, 0))],
        out_specs=[pl.BlockSpec((C, E), lambda i: (i, 0)), pl.BlockSpec((C, E), lambda i: (i, 0)),
                   pl.BlockSpec((H, C, C), lambda i: (0, 0, 0)), pl.BlockSpec((C, E), lambda i: (0, 0))],
        out_shape=[jax.ShapeDtypeStruct((T, E), F32), jax.ShapeDtypeStruct((T, E), F32),
                   jax.ShapeDtypeStruct((H, C, C), F32), jax.ShapeDtypeStruct((C, E), F32)],
        name="spatial_bwd", compiler_params=_cparams(("arbitrary",)),
    )(u, vln, dg, ws, bias)


def _att_masks():
    r = lax.broadcasted_iota(jnp.int32, (ATT_BLK, ATT_BLK), 0)
    c = lax.broadcasted_iota(jnp.int32, (ATT_BLK, ATT_BLK), 1)
    return c <= r, c >= r


NEG = -1e30
ATT_SCALE = HEAD_DIM ** -0.5


def _att_view(t, dil):
    return t.reshape(t.shape[0] // dil, dil * t.shape[1])


def attn_fwd(name, q, k, v, dil, bsz):
    T, Wd = q.shape
    nb = T // (bsz * dil * ATT_BLK)
    TB = min(nb, ATT_TB)
    nsteps = nb // TB

    def body(q_ref, k_ref, v_ref, kp_ref, vp_ref, o_ref, l_ref):
        n = pl.program_id(2)
        mc, mp = _att_masks()
        for j in range(TB):
            rows = slice(j * ATT_BLK, (j + 1) * ATT_BLK)
            prow = slice((j - 1) * ATT_BLK, j * ATT_BLK)
            hp = (n * TB + j) > 0
            H = range(ATT_HEADS)
            ls = [slice(hh * HEAD_DIM, (hh + 1) * HEAD_DIM) for hh in H]
            qj = [q_ref[rows, ls[hh]] for hh in H]
            kc = [k_ref[rows, ls[hh]] for hh in H]
            kp = [k_ref[prow, ls[hh]] if j > 0 else kp_ref[:, ls[hh]] for hh in H]
            sc = [jnp.where(mc, _dot_nt(qj[hh], kc[hh]) * ATT_SCALE, NEG) for hh in H]
            sp = [jnp.where(mp & hp, _dot_nt(qj[hh], kp[hh]) * ATT_SCALE, NEG) for hh in H]
            m = [jnp.maximum(jnp.max(sc[hh], axis=1, keepdims=True), jnp.max(sp[hh], axis=1, keepdims=True))
                 for hh in H]
            pc = [jnp.exp(sc[hh] - m[hh]) for hh in H]
            pp = [jnp.exp(sp[hh] - m[hh]) for hh in H]
            l = [jnp.sum(pc[hh], axis=1, keepdims=True) + jnp.sum(pp[hh], axis=1, keepdims=True) for hh in H]
            vc = [v_ref[rows, ls[hh]] for hh in H]
            vp = [v_ref[prow, ls[hh]] if j > 0 else vp_ref[:, ls[hh]] for hh in H]
            for hh in H:
                o_ref[rows, ls[hh]] = (_dot(pc[hh].astype(BF16), vc[hh]) + _dot(pp[hh].astype(BF16), vp[hh])) / l[hh]
                l_ref[rows, ls[hh]] = jnp.broadcast_to(m[hh] + jnp.log(l[hh]), (ATT_BLK, HEAD_DIM))

    blk = pl.BlockSpec((TB * ATT_BLK, Wd), lambda b, r, n: (b * nsteps + n, r))
    prev = pl.BlockSpec((ATT_BLK, Wd), lambda b, r, n: (jnp.maximum(b * nb + n * TB - 1, 0), r))
    qv, kv, vv = (_att_view(t, dil) for t in (q, k, v))
    o, l = pl.pallas_call(
        body, grid=(bsz, dil, nsteps), in_specs=[blk, blk, blk, prev, prev], out_specs=[blk, blk],
        out_shape=[jax.ShapeDtypeStruct(qv.shape, F32), jax.ShapeDtypeStruct(qv.shape, F32)],
        name=name, compiler_params=_cparams(("arbitrary", "arbitrary", "arbitrary")),
    )(qv, kv, vv, kv, vv)
    return o.reshape(T, Wd), l.reshape(T, Wd)


def attn_bwd(name, q, k, v, do, mg, lse, dil, bsz):
    T, Wd = q.shape
    nb = T // (bsz * dil * ATT_BLK)
    TB = min(nb, ATT_TB)
    nsteps = nb // TB

    def body(q_ref, k_ref, v_ref, do_ref, mg_ref, l_ref, kp_ref, vp_ref, qn_ref, don_ref, mgn_ref, ln_ref,
             dq_ref, dk_ref, dv_ref):
        n = pl.program_id(2)
        mc, mp = _att_masks()

        def probs_all(qs, ks, lse_cols, mask):
            s = [_dot_nt(qh, kh) * ATT_SCALE for qh, kh in zip(qs, ks)]
            return [jnp.where(mask, jnp.exp(sh - lc), 0.0) for sh, lc in zip(s, lse_cols)]

        def ds_all(ps, dos, vs, deltas):
            dp = [_dot_nt(dh, vh) for dh, vh in zip(dos, vs)]
            return [(ph * (dph - dl) * ATT_SCALE).astype(BF16) for ph, dph, dl in zip(ps, dp, deltas)]

        H = range(ATT_HEADS)
        ls = [slice(hh * HEAD_DIM, (hh + 1) * HEAD_DIM) for hh in H]
        dk = [[None] * TB for _ in H]
        dv = [[None] * TB for _ in H]
        for j in range(TB + 1):
            rows = slice(j * ATT_BLK, (j + 1) * ATT_BLK)
            prow = slice((j - 1) * ATT_BLK, j * ATT_BLK)
            if j < TB:
                srcs = (q_ref, do_ref, mg_ref, l_ref)
                qj, doj, mgj, lj = ([r[rows, ls[hh]] for hh in H] for r in srcs)
                hp = (n * TB + j) > 0
            else:
                srcs = (qn_ref, don_ref, mgn_ref, ln_ref)
                qj, doj, mgj, lj = ([r[:, ls[hh]] for hh in H] for r in srcs)
                hp = (n + 1) * TB < nb
            lse_col = [lj[hh][:, 0:1] for hh in H]
            delta = [jnp.sum(doj[hh].astype(F32) * mgj[hh].astype(F32), axis=1, keepdims=True) for hh in H]
            if j > 0:
                kp = [k_ref[prow, ls[hh]] for hh in H]
                vp = [v_ref[prow, ls[hh]] for hh in H]
            else:
                kp = [kp_ref[:, ls[hh]] for hh in H]
                vp = [vp_ref[:, ls[hh]] for hh in H]
            pp = probs_all(qj, kp, lse_col, mp & hp)
            dsp = ds_all(pp, doj, vp, delta)
            if j > 0:
                for hh in H:
                    dk[hh][j - 1] = dk[hh][j - 1] + _dot_tn(dsp[hh], qj[hh])
                    dv[hh][j - 1] = dv[hh][j - 1] + _dot_tn(pp[hh].astype(BF16), doj[hh])
            if j < TB:
                kc = [k_ref[rows, ls[hh]] for hh in H]
                vc = [v_ref[rows, ls[hh]] for hh in H]
                pc = probs_all(qj, kc, lse_col, mc)
                dsc = ds_all(pc, doj, vc, delta)
                for hh in H:
                    dq_ref[rows, ls[hh]] = (_dot(dsc[hh], kc[hh]) + _dot(dsp[hh], kp[hh])).astype(dq_ref.dtype)
                for hh in H:
                    dk[hh][j] = _dot_tn(dsc[hh], qj[hh])
                    dv[hh][j] = _dot_tn(pc[hh].astype(BF16), doj[hh])
        for j in range(TB):
            rows = slice(j * ATT_BLK, (j + 1) * ATT_BLK)
            for hh in H:
                dk_ref[rows, ls[hh]] = dk[hh][j].astype(dk_ref.dtype)
                dv_ref[rows, ls[hh]] = dv[hh][j].astype(dv_ref.dtype)

    blk = pl.BlockSpec((TB * ATT_BLK, Wd), lambda b, r, n: (b * nsteps + n, r))
    prev = pl.BlockSpec((ATT_BLK, Wd), lambda b, r, n: (jnp.maximum(b * nb + n * TB - 1, 0), r))
    nxt = pl.BlockSpec((ATT_BLK, Wd), lambda b, r, n: (b * nb + jnp.minimum((n + 1) * TB, nb - 1), r))
    qv, kv, vv, dov, mgv, lv = (_att_view(t, dil) for t in (q, k, v, do, mg, lse))
    res = pl.pallas_call(
        body, grid=(bsz, dil, nsteps), in_specs=[blk] * 6 + [prev, prev, nxt, nxt, nxt, nxt],
        out_specs=[blk, blk, blk], out_shape=[jax.ShapeDtypeStruct(qv.shape, BF16)] * 3,
        name=name, compiler_params=_cparams(("arbitrary", "arbitrary", "arbitrary")),
    )(qv, kv, vv, dov, mgv, lv, kv, vv, qv, dov, mgv, lv)
    return [t.reshape(T, Wd) for t in res]


QKV_SLOTS = 3 * len(ATT_CONFIGS) * ATT_HEADS
SLOTS_PER_DEV = QKV_SLOTS // N_DEV


def qkv_matmul(name, a, b):
    M, K = a.shape
    nblk, _, n = b.shape
    nout = QKV_SLOTS // ATT_HEADS
    tm = _rows_for(M, K * 2 + nout * ATT_W * 2)

    def body(a_ref, b_ref, *outs):
        av = a_ref[...]
        for c in range(nblk):
            acc = _dot(av, b_ref[c]).astype(BF16)
            for r in range(SLOTS_PER_DEV):
                k, hh = divmod(c * SLOTS_PER_DEV + r, ATT_HEADS)
                outs[k][:, hh * HEAD_DIM:(hh + 1) * HEAD_DIM] = acc[:, r * HEAD_DIM:(r + 1) * HEAD_DIM]

    return pl.pallas_call(
        body, grid=(M // tm,),
        in_specs=[pl.BlockSpec((tm, K), lambda i: (i, 0)), pl.BlockSpec(b.shape, lambda i: (0, 0, 0))],
        out_specs=[pl.BlockSpec((tm, ATT_W), lambda i: (i, 0))] * nout,
        out_shape=[jax.ShapeDtypeStruct((M, ATT_W), BF16)] * nout,
        name=name, compiler_params=_cparams(("arbitrary",)))(a, b)


def _coords():
    return lax.axis_index("x"), lax.axis_index("y"), lax.axis_index("c")


def all_gather(name, xs):
    n = len(xs)

    def body(*refs):
        start, forward, finish = _gather_phases(refs[:n], refs[n:2 * n], *refs[2 * n:])
        start()
        forward()
        finish()

    anyspec = pl.BlockSpec(memory_space=pl.ANY)
    return pl.pallas_call(
        body, out_shape=_gather_out_shapes(xs), in_specs=[anyspec] * n, out_specs=[anyspec] * n,
        scratch_shapes=_gather_sems(n), name=name,
    )(*xs)


def _gather_out_shapes(xs):
    return [jax.ShapeDtypeStruct((N_DEV,) + t.shape, t.dtype) for t in xs]


def _gather_sems(n):
    return [pltpu.SemaphoreType.DMA((7 * n,)), pltpu.SemaphoreType.DMA((7 * n,)), pltpu.SemaphoreType.DMA((n,))]


def _gather_phases(x_refs, out_refs, send_sems, recv_sems, local_sems):
    n = len(x_refs)

    def parts():
        x, y, c = _coords()
        return (x, y, c), (x, y, 1 - c), [(1 - x, y), (x, 1 - y), (1 - x, 1 - y)], c

    def slot(a, px, py, pc):
        return out_refs[a].at[4 * px + 2 * py + pc]

    def copy(a, k, block, to, src=None):
        return pltpu.make_async_remote_copy(
            src_ref=slot(a, *block) if src is None else src, dst_ref=slot(a, *block),
            send_sem=send_sems.at[7 * a + k], recv_sem=recv_sems.at[7 * a + k],
            device_id=to, device_id_type=MESH)

    def mine(a, me):
        return pltpu.make_async_copy(x_refs[a], slot(a, *me), local_sems.at[a])

    def first(a, me, sibling, chips, c):
        return ([copy(a, 0, me, sibling, src=x_refs[a])]
                + [copy(a, 1 + j, me, (*chip, c), src=x_refs[a]) for j, chip in enumerate(chips)])

    def start():
        me, sibling, chips, c = parts()
        for a in range(n):
            mine(a, me).start()
        for a in range(n):
            for cp in first(a, me, sibling, chips, c):
                cp.start()

    def forward():
        me, sibling, chips, c = parts()
        for j, chip in enumerate(chips):
            for a in range(n):
                copy(a, 1 + j, (*chip, c), me).wait_recv()
                copy(a, 4 + j, (*chip, c), sibling).start()

    def finish():
        me, sibling, chips, c = parts()
        for a in range(n):
            copy(a, 0, sibling, me).wait_recv()
            for j, chip in enumerate(chips):
                copy(a, 4 + j, (*chip, 1 - c), me).wait_recv()
        for a in range(n):
            for cp in first(a, me, sibling, chips, c):
                cp.wait_send()
            for j, chip in enumerate(chips):
                copy(a, 4 + j, (*chip, c), sibling).wait_send()
            mine(a, me).wait()

    return start, forward, finish


def exchange_sibling(name, gs):
    n = len(gs)

    def body(*refs):
        g_refs, out_refs = refs[:n], refs[n:2 * n]
        send_sems, recv_sems = refs[2 * n:]
        x, y, c = _coords()
        sibling = (x, y, 1 - c)
        cps = []
        for a in range(n):
            for q in range(4):
                cps.append(pltpu.make_async_remote_copy(
                    src_ref=g_refs[a].at[2 * q + (1 - c)], dst_ref=out_refs[a].at[q],
                    send_sem=send_sems.at[4 * a + q], recv_sem=recv_sems.at[4 * a + q],
                    device_id=sibling, device_id_type=MESH))
        for cp in cps:
            cp.start()
        for cp in cps:
            cp.wait_recv()
        for cp in cps:
            cp.wait_send()

    anyspec = pl.BlockSpec(memory_space=pl.ANY)
    return pl.pallas_call(
        body, out_shape=[jax.ShapeDtypeStruct((4,) + g.shape[1:], g.dtype) for g in gs],
        in_specs=[anyspec] * n, out_specs=[anyspec] * n,
        scratch_shapes=[pltpu.SemaphoreType.DMA((4 * n,)), pltpu.SemaphoreType.DMA((4 * n,))],
        name=name,
    )(*gs)


def exchange_chips(name, ps):
    n = len(ps)

    def body(*refs):
        start, finish = _chips_phases(refs[:n], refs[n:2 * n], *refs[2 * n:])
        start()
        finish()

    anyspec = pl.BlockSpec(memory_space=pl.ANY)
    return pl.pallas_call(
        body, out_shape=_chips_out_shapes(ps), in_specs=[anyspec] * n, out_specs=[anyspec] * n,
        scratch_shapes=_chips_sems(n), name=name,
    )(*ps)


def _chips_out_shapes(ps):
    return [jax.ShapeDtypeStruct((3,) + p.shape[1:], p.dtype) for p in ps]


def _chips_sems(n):
    return [pltpu.SemaphoreType.DMA((3 * n,)), pltpu.SemaphoreType.DMA((3 * n,))]


def _chips_phases(p_refs, out_refs, send_sems, recv_sems):
    n = len(p_refs)

    def copies():
        x, y, c = _coords()
        chips = [(1 - x, y), (x, 1 - y), (1 - x, 1 - y)]
        return [pltpu.make_async_remote_copy(
            src_ref=p_refs[a].at[2 * px + py], dst_ref=out_refs[a].at[k],
            send_sem=send_sems.at[3 * a + k], recv_sem=recv_sems.at[3 * a + k],
            device_id=(px, py, c), device_id_type=MESH)
            for a in range(n) for k, (px, py) in enumerate(chips)]

    def start():
        for cp in copies():
            cp.start()

    def finish():
        cps = copies()
        for cp in cps:
            cp.wait_recv()
        for cp in cps:
            cp.wait_send()

    return start, finish


def _row_tile(R):
    tr = 256
    while R % tr:
        tr //= 2
    assert tr % 8 == 0
    return tr


def add_sibling(name, g, recv, c_idx):
    _, R, C = g.shape
    tr = _row_tile(R)

    def body(c_ref, g_ref, r_ref, o_ref, o16_ref):
        s = g_ref[...] + r_ref[...].astype(F32)
        o_ref[...] = s
        o16_ref[...] = s.astype(BF16)

    out = pl.BlockSpec((None, tr, C), lambda q, i, cr: (q, i, 0))
    return pl.pallas_call(
        body,
        grid_spec=pltpu.PrefetchScalarGridSpec(
            num_scalar_prefetch=1, grid=(4, R // tr),
            in_specs=[pl.BlockSpec((None, tr, C), lambda q, i, cr: (2 * q + cr[0], i, 0)), out],
            out_specs=[out, out]),
        out_shape=[jax.ShapeDtypeStruct((4, R, C), F32), jax.ShapeDtypeStruct((4, R, C), BF16)], name=name,
        compiler_params=_cparams(("arbitrary", "arbitrary")),
    )(c_idx, g, recv)


def _adam_math(w, g, m, v):
    m = ADAM_B1 * m + (1.0 - ADAM_B1) * g
    v = ADAM_B2 * v + (1.0 - ADAM_B2) * jnp.square(g)
    m_hat = m / (1.0 - ADAM_B1 ** ADAM_STEP)
    v_hat = v / (1.0 - ADAM_B2 ** ADAM_STEP)
    delta = -ADAM_LR * (m_hat / (jnp.sqrt(v_hat) + ADAM_EPS) + ADAM_WD * w)
    return delta, m, v


def adam_big(name, p1, recv, w, m, v, chip_idx, layer=0):
    _, R, C = p1.shape
    tr = _row_tile(R)
    nt = R // tr

    def body(q_ref, p_ref, r_ref, w_ref, m_ref, v_ref, g_ref, d_ref, nm_ref, nv_ref):
        g = ((p_ref[...] + r_ref[0].astype(F32)) + r_ref[1].astype(F32)) + r_ref[2].astype(F32)
        d, nm, nv = _adam_math(w_ref[...], g, m_ref[...], v_ref[...])
        g_ref[...] = g
        d_ref[...] = d
        nm_ref[...] = nm
        nv_ref[...] = nv

    row_in = pl.BlockSpec((tr, C), lambda i, qr: (layer * nt + i, 0))
    row = pl.BlockSpec((tr, C), lambda i, qr: (i, 0))
    return pl.pallas_call(
        body,
        grid_spec=pltpu.PrefetchScalarGridSpec(
            num_scalar_prefetch=1, grid=(nt,),
            in_specs=[pl.BlockSpec((None, tr, C), lambda i, qr: (qr[0], i, 0)),
                      pl.BlockSpec((3, tr, C), lambda i, qr: (0, i, 0)), row_in, row_in, row_in],
            out_specs=[row, row, row, row]),
        out_shape=[jax.ShapeDtypeStruct((R, C), F32)] * 4, name=name,
        compiler_params=_cparams(("arbitrary",)),
    )(chip_idx, p1, recv, w, m, v)


def sum8(parts):
    _, R, C = parts.shape

    def body(p_ref, o_ref):
        acc = p_ref[0]
        for k in range(1, N_DEV):
            acc = acc + p_ref[k]
        o_ref[...] = acc

    tr = 128
    while R % tr:
        tr //= 2
    assert tr % 8 == 0
    return pl.pallas_call(
        body, grid=(R // tr,), in_specs=[pl.BlockSpec((N_DEV, tr, C), lambda i: (0, i, 0))],
        out_specs=pl.BlockSpec((tr, C), lambda i: (i, 0)), out_shape=jax.ShapeDtypeStruct((R, C), F32),
        name="sum8", compiler_params=_cparams(("arbitrary",)),
    )(parts)


def adam_small(w, g, m, v):
    def fn(wt, gt, mt, vt):
        return _adam_math(wt, gt, mt, vt)
    C = w.shape[1]
    return rowwise("adam_small", fn, [w, g, m, v], [], [(C, F32)] * 3, tr=128)


def _pack(arrs, rows_mult=8):
    flat = jnp.concatenate([a.reshape(-1) for a in arrs])
    n = flat.shape[0]
    per = PACK_C * rows_mult
    pad = (-n) % per
    if pad:
        flat = jnp.concatenate([flat, jnp.zeros((pad,), flat.dtype)])
    return flat.reshape(-1, PACK_C)


def _unpack(buf, shapes):
    flat = buf.reshape(-1)
    out, off = [], 0
    for s in shapes:
        n = math.prod(s)
        out.append(flat[off:off + n].reshape(s))
        off += n
    return out


def _blocked(gfull, axis):
    shp = gfull.shape
    n = shp[axis] // N_DEV
    t = gfull.reshape(shp[:axis] + (N_DEV, n) + shp[axis + 1:])
    t = jnp.moveaxis(t, axis, 0)
    return t.reshape(N_DEV, -1)


def _unblocked(gathered, shard_shape, axis):
    t = jnp.moveaxis(gathered, 0, axis)
    shp = shard_shape[:axis] + (N_DEV * shard_shape[axis],) + shard_shape[axis + 1:]
    return t.reshape(shp)


def _relu2_epi(acc):
    r = jnp.maximum(acc, 0.0)
    return acc, r * r


def _step(x3, target3, W, comm):
    bsz, S, D = x3.shape
    T = bsz * S
    x = x3.reshape(T, D)
    target = target3.reshape(T, D)
    row = lambda v: v.reshape(1, -1)
    grads = {}

    s5p = (W['ssm_a_re'][0], W['ssm_a_im'][0], W['ssm_log_dt'][0], W['ssm_b_re'][0], W['ssm_b_im'][0])
    (lam_re, lam_im, bb_re, bb_im), s5_disc_vjp = jax.vjp(s5_disc, *s5p)
    pwr, pwi, l2r, l2i = s5_tables(lam_re, lam_im, S5_SUB)
    bre, bim = _s5_blockdiag_b(bb_re).astype(BF16), _s5_blockdiag_b(bb_im).astype(BF16)
    cre, cim = _s5_blockdiag_c(W['ssm_c_re'][0]).astype(BF16), _s5_blockdiag_c(W['ssm_c_im'][0]).astype(BF16)
    dskip = W['ssm_d']

    tril = jnp.tril(jnp.ones((GMLP_CHUNK, GMLP_CHUNK), bool))
    ws = jnp.where(tril[None], W['gmlp_w_s'][0], 0.0).astype(BF16)
    hw = D // GMLP_HEADS
    sbias = jnp.repeat(W['gmlp_b_s'][0].T, hw, axis=1)

    saved = []
    def add_norm(acc, *ex):
        xn = acc + ex[0] + ex[1] if len(ex) == 3 else acc + ex[0]
        return xn, _rms(xn, ex[-1])

    for i in range(DEPTH):
        sv = {'x': x}
        nm = W['norm_mix'][i:i + 1]
        nl = W['norm_mlp'][i:i + 1]
        if i == 0:
            h, hf = rms_fwd("rms_mix0", x, nm, want_f32=True)
            res = s5_fwd(hf, bre, bim, cre, cim, pwr, pwi, l2r, l2i, dskip, bsz, gather=comm.gather_list)
            ypre, gy, xs, xr_all, xi_all = res[:5]
            Wfull, Wsh = comm.weights(res[5:])
            W = {**W, **Wsh}
            conv_w = jnp.concatenate([W['conv_w_dw'][0], jnp.zeros((1, D), F32)], axis=0)
            z, = matmul("s5_glu_mm", gy, Wfull['ssm_w_glu'], mode='cb')
            def s5_glu(zt, xt, g):
                xn = xt + _glu(zt)
                return xn, _rms(xn, g)
            x1, h2 = rowwise("s5_glu", s5_glu, [z, x], [nl], [(D, F32), (D, BF16)])
            sv.update(hf=hf, ypre=ypre, gy=gy, xs=xs, xr=xr_all, xi=xi_all, z=z)
        elif i == 1:
            z, = matmul("conv_pw1", h, Wfull['conv_w_pw1'], mode='cb', epi=lambda acc, b: (acc + b,),
                        extras=[(W['conv_b_pw1'], 'row')])
            zg, = rowwise("conv_glu", _glu, [z], [], [(D, F32)])
            yc = dwconv_fwd(zg, conv_w, W['conv_b_dw'], bsz)
            y2, = rowwise("conv_ln_silu", lambda t, g, b: jax.nn.silu(_ln(t, g, b)), [yc],
                          [W['conv_ln_g'], W['conv_ln_b']], [(D, BF16)])
            x1, h2 = matmul("conv_pw2", y2, Wfull['conv_w_pw2'], epi=add_norm, whole_rows=True,
                            extras=[(W['conv_b_pw2'], 'row'), (x, 'tile'), (nl, 'row')], out_dtypes=(F32, BF16))
            sv.update(h=h, z=z, zg=zg, yc=yc, y2=y2)
        elif i == 2:
            zp, = matmul("gmlp_in", h, Wfull['gmlp_w_in'], mode='cb')

            def gm_pre(zt, g, b):
                a = jax.nn.gelu(zt)
                return a[:, :D], _ln(a[:, D:], g, b)
            u, vln = rowwise("gmlp_pre", gm_pre, [zp], [W['gmlp_ln_g'], W['gmlp_ln_b']], [(D, F32), (D, BF16)])
            gated = spatial_fwd(u, vln, ws, sbias)
            x1, h2 = matmul("gmlp_out", gated, Wfull['gmlp_w_out'], epi=add_norm, whole_rows=True,
                            extras=[(x, 'tile'), (nl, 'row')], out_dtypes=(F32, BF16))
            sv.update(h=h, zp=zp, u=u, vln=vln, gated=gated)
        else:
            qkv = qkv_matmul("attn_qkv", h, Wfull['attn_w_qkv'])
            ng = len(ATT_CONFIGS)
            outs, lses, blocks = [], [], []
            for gi, (window, dil) in enumerate(ATT_CONFIGS):
                qb, kb, vb = (qkv[j * ng + gi] for j in range(3))
                ob, lb = attn_fwd("attn_fwd%d" % gi, qb, kb, vb, dil, bsz)
                blocks.append((qb, kb, vb, lb, dil))
                outs.append(ob)
                lses.append(lb)

            def merge(o0, o1, o2, l0, l1, l2):
                m = jnp.maximum(jnp.maximum(l0, l1), l2)
                e0, e1, e2 = jnp.exp(l0 - m), jnp.exp(l1 - m), jnp.exp(l2 - m)
                inv = 1.0 / (e0 + e1 + e2)
                w0, w1, w2 = e0 * inv, e1 * inv, e2 * inv
                return w0 * o0 + w1 * o1 + w2 * o2, w0, w1, w2
            merged, w0, w1, w2 = rowwise("attn_merge", merge, outs + lses, [],
                                         [(ATT_W, BF16), (ATT_W, F32), (ATT_W, F32), (ATT_W, F32)])
            wo = Wfull['attn_w_o']
            wo_nat = wo.transpose(1, 0, 2).reshape(wo.shape[1], N_DEV * wo.shape[2])
            x1, h2 = matmul("attn_o", merged, wo_nat, epi=add_norm, whole_rows=True,
                            extras=[(x, 'tile'), (nl, 'row')], out_dtypes=(F32, BF16))
            sv.update(h=h, blocks=blocks, merged=merged, wts=(w0, w1, w2))
        a, act = matmul("mlp_in%d" % i, h2, Wfull['mlp_w_in'][i], mode='cb', epi=_relu2_epi, out_dtypes=(BF16, BF16))
        if i + 1 < DEPTH:
            x2, h = matmul("mlp_out%d" % i, act, Wfull['mlp_w_out'][i], epi=add_norm, whole_rows=True,
                           extras=[(x1, 'tile'), (W['norm_mix'][i + 1:i + 2], 'row')], out_dtypes=(F32, BF16))
        else:
            x2, = matmul("mlp_out%d" % i, act, Wfull['mlp_w_out'][i], epi=lambda acc, r: (acc + r,),
                         extras=[(x1, 'tile')])
        sv.update(x1=x1, h2=h2, a=a, act=act)
        saved.append(sv)
        x = x2

    def loss_fn(xt, tt, g):
        y, vjp = jax.vjp(_rms, xt, g)
        err = y - tt
        dxx, dg = vjp(err * (1.0 / D))
        lval = jnp.sum(jnp.sum(err * err, axis=1, keepdims=True), axis=0, keepdims=True) * (0.5 / D)
        return dxx, dxx, jnp.broadcast_to(lval, (1, 128)), dg
    dx, dxb, lacc, dnf = rowwise("loss_head", loss_fn, [x, target], [row(W['norm_final'])],
                                 [(D, F32), (D, BF16)], [((1, 128), F32), ((1, D), F32)])
    loss_local = lacc[0, 0]
    grads['norm_final'] = dnf.reshape(-1)

    g_norm_mix, g_norm_mlp = [None] * DEPTH, [None] * DEPTH
    g_mlp_in, g_mlp_out = [None] * DEPTH, [None] * DEPTH
    nl_all = [W['norm_mlp'][i:i + 1] for i in range(DEPTH)]
    nm_all = [W['norm_mix'][i:i + 1] for i in range(DEPTH)]

    def norm_bwd(xt, dres, g):
        def epi(dh, xv, dr, gv):
            _, vjp = jax.vjp(_rms, xv, gv)
            dxv, dgv = vjp(dh)
            dxv = dxv + dr
            return dxv, dxv, dgv
        return dict(epi=epi, extras=[xt, dres], params=[g], out_dtypes=(F32, BF16), acc_out=[((1, D), F32)])

    for i in reversed(range(DEPTH)):
        sv = saved[i]
        da, = matmul("mlp_out_bwd%d" % i, dxb, Wfull['mlp_w_out'][i], mode='nt',
                     epi=lambda acc, av: (acc * (2.0 * jnp.maximum(av.astype(F32), 0.0)),),
                     extras=[(sv['a'], 'tile')], out_dtypes=(BF16,))
        g_mlp_out[i] = _rows_blocked(wgrad("mlp_out_wg%d" % i, sv['act'], dxb))
        dx, dxb, dg = matmul_nt_cb("mlp_in_bwd%d" % i, da, Wfull['mlp_w_in'][i], **norm_bwd(sv['x1'], dx, nl_all[i]))
        g_mlp_in[i] = wgrad("mlp_in_wg%d" % i, sv['h2'], da, cb=True)
        g_norm_mlp[i] = dg.reshape(-1)
        xin = sv['x']
        if i == 0:
            dz, = rowwise("s5_glu_bwd", _glu_bwd, [sv['z'], dx], [], [(2 * D, BF16)])
            dgy, = matmul_nt_cb("s5_glu_mm_bwd", dz, Wfull['ssm_w_glu'])
            grads['ssm_w_glu'] = wgrad("s5_glu_wg", sv['gy'], dz, cb=True)

            def gelu_bwd(yt, dt):
                _, vjp = jax.vjp(jax.nn.gelu, yt)
                return vjp(dt)[0]
            dypre, = rowwise("s5_gelu_bwd", gelu_bwd, [sv['ypre'], dgy], [], [(D, F32)])
            grads['mlp_w_in'], grads['mlp_w_out'] = g_mlp_in, g_mlp_out
            res = s5_bwd(sv['hf'], dypre, sv['xs'], sv['xr'], sv['xi'], bre, bim, cre, cim,
                         pwr[:, ::-1], pwi[:, ::-1], l2r, l2i, dskip, bsz, chips=comm.rs_front(grads))
            du, dbr, dbi, dcr, dci, dl, dd = res[:7]
            comm.recv2 = res[7:]
            dlam_re = dl[:, 0, :].reshape(SSM_GROUPS, SSM_STATE)
            dlam_im = dl[:, 1, :].reshape(SSM_GROUPS, SSM_STATE)
            s5_cot = (dlam_re, dlam_im, _s5_blockdiag_b_inv(dbr), _s5_blockdiag_b_inv(dbi))
            grads['ssm_c_re'] = _s5_blockdiag_c_inv(dcr)[None]
            grads['ssm_c_im'] = _s5_blockdiag_c_inv(dci)[None]
            grads['ssm_d'] = dd[0:1]
            dx, dxb, dg = rms_bwd("rms_mix_bwd0", xin, du, dx, nm_all[0])
        elif i == 1:
            dy2, = matmul("conv_pw2_bwd", dxb, Wfull['conv_w_pw2'], mode='nt')
            grads['conv_w_pw2'] = _rows_blocked(wgrad("conv_pw2_wg", sv['y2'], dxb))

            def ln_silu_bwd(yt, dt, dxt, g, b):
                _, vjp = jax.vjp(lambda t, gg, bb: jax.nn.silu(_ln(t, gg, bb)), yt, g, b)
                dyc, dgg, dbb = vjp(dt)
                return dyc, dgg, dbb, _colsum(dxt)
            dyc, dlg, dlb, dbp2 = rowwise("conv_ln_silu_bwd", ln_silu_bwd, [sv['yc'], dy2, dx],
                                          [W['conv_ln_g'], W['conv_ln_b']], [(D, F32)],
                                          [((1, D), F32), ((1, D), F32), ((1, D), F32)])
            grads['conv_ln_g'], grads['conv_ln_b'], grads['conv_b_pw2'] = dlg, dlb, dbp2
            dzg, dwd, dbd = dwconv_bwd(sv['zg'], dyc, conv_w, bsz)
            grads['conv_w_dw'] = dwd[None, :CONV_WIDTH]
            grads['conv_b_dw'] = dbd

            def glu_bwd1(zt, dyt):
                dzt = _glu_bwd(zt, dyt)
                return dzt, _colsum(dzt)
            dz, dbp1 = rowwise("conv_glu_bwd", glu_bwd1, [sv['z'], dzg], [], [(2 * D, BF16)], [((1, 2 * D), F32)])
            grads['conv_b_pw1'] = dbp1
            dx, dxb, dg = matmul_nt_cb("conv_pw1_bwd", dz, Wfull['conv_w_pw1'], **norm_bwd(xin, dx, nm_all[i]))
            grads['conv_w_pw1'] = wgrad("conv_pw1_wg", sv['h'], dz, cb=True)
        elif i == 2:
            dgt, = matmul("gmlp_out_bwd", dxb, Wfull['gmlp_w_out'], mode='nt', out_dtypes=(BF16,))
            grads['gmlp_w_out'] = _rows_blocked(wgrad("gmlp_out_wg", sv['gated'], dxb))
            du, dvln, dws, dsb = spatial_bwd(sv['u'], sv['vln'], dgt, ws, sbias)
            grads['gmlp_w_s'] = dws[None]
            grads['gmlp_b_s'] = dsb.reshape(GMLP_CHUNK, GMLP_HEADS, hw).sum(-1).T[None]

            def gm_pre_bwd(zt, dut, dvt, g, b):
                _, vjp_u = jax.vjp(jax.nn.gelu, zt[:, :D])
                _, vjp_v = jax.vjp(lambda zz, gg, bb: _ln(jax.nn.gelu(zz), gg, bb), zt[:, D:], g, b)
                dz2, dgg, dbb = vjp_v(dvt)
                return jnp.concatenate([vjp_u(dut)[0], dz2], axis=1), dgg, dbb
            dzp, dlg, dlb = rowwise("gmlp_pre_bwd", gm_pre_bwd, [sv['zp'], du, dvln],
                                    [W['gmlp_ln_g'], W['gmlp_ln_b']], [(2 * D, BF16)], [((1, D), F32), ((1, D), F32)])
            grads['gmlp_ln_g'], grads['gmlp_ln_b'] = dlg, dlb
            dx, dxb, dg = matmul_nt_cb("gmlp_in_bwd", dzp, Wfull['gmlp_w_in'], **norm_bwd(xin, dx, nm_all[i]))
            grads['gmlp_w_in'] = wgrad("gmlp_in_wg", sv['h'], dzp, cb=True)
        else:
            dm, = matmul_nt_cb("attn_o_bwd", dxb, Wfull['attn_w_o'])
            grads['attn_w_o'] = wgrad("attn_o_wg", sv['merged'], dxb, cb=True)
            w0, w1, w2 = sv['wts']
            do0, do1, do2 = rowwise("attn_merge_bwd", lambda d, a, b, c: (a * d, b * d, c * d), [dm, w0, w1, w2], [],
                                    [(ATT_W, BF16)] * 3)
            dparts = [[None] * 3 for _ in range(3)]
            for gi, (dog, (qb, kb, vb, lb, dil)) in enumerate(zip((do0, do1, do2), sv['blocks'])):
                dqb, dkb, dvb = attn_bwd("attn_bwd%d" % gi, qb, kb, vb, dog, sv['merged'], lb, dil, bsz)
                for j, t in enumerate((dqb, dkb, dvb)):
                    dparts[j][gi] = t
            dqkv = [dparts[j][gi] for j in range(3) for gi in range(3)]
            dx, dxb, dg = matmul_nt_cb("attn_qkv_bwd", dqkv, Wfull['attn_w_qkv'], heads=True,
                                       **norm_bwd(xin, dx, nm_all[i]))
            grads['attn_w_qkv'] = wgrad("attn_qkv_wg", sv['h'], dqkv, cb=True, heads=True)
        g_norm_mix[i] = dg.reshape(-1)

    grads['norm_mix'] = jnp.stack(g_norm_mix)
    grads['norm_mlp'] = jnp.stack(g_norm_mlp)
    grads['mlp_w_in'] = g_mlp_in
    grads['mlp_w_out'] = g_mlp_out
    return loss_local, dx.reshape(bsz, S, D), grads, (s5_disc_vjp, s5_cot)


class _StepComm:
    def __init__(self, Wl, c_idx):
        self.Wl, self.c_idx = Wl, c_idx
        self.units = []
        for n in BIG:
            self.units += [(n, i) for i in range(DEPTH)] if Wl[n].shape[0] == DEPTH else [(n, None)]
        self.ss_names = list(SMALL_SHARDED)
        spack = _pack([Wl[n] for n in self.ss_names])
        self.gather_list = [Wl[n][0 if i is None else i].astype(BF16) for n, i in self.units] + [spack]
        self.p1 = self.recv2 = None

    @staticmethod
    def tag(n, i):
        return n if i is None else "%s%d" % (n, i)

    def weights(self, gathered):
        Wl = self.Wl
        Wfull = {}
        for (n, i), g in zip(self.units, gathered):
            w = g if BIG[n] == 2 else g.reshape(N_DEV * g.shape[1], g.shape[2])
            if i is None:
                Wfull[n] = w
            else:
                Wfull.setdefault(n, []).append(w)
        sparts = _unpack_gathered(gathered[-1], [Wl[n].shape for n in self.ss_names])
        Wsh = {n: _unblocked(p, Wl[n].shape, SMALL_SHARDED[n]) for n, p in zip(self.ss_names, sparts)}
        return Wfull, Wsh

    def rs_front(self, grads):
        pairs = [grads[n] if i is None else grads[n][i] for n, i in self.units]
        recv1 = exchange_sibling("rs_sibling", [p[1] for p in pairs])
        self.p1 = [add_sibling("add_sibling_" + self.tag(n, i), p[0], r, self.c_idx)
                   for (n, i), p, r in zip(self.units, pairs, recv1)]
        return [p[1] for p in self.p1]


def _rows_blocked(pair):
    return tuple(t.reshape(N_DEV, t.shape[0] // N_DEV, t.shape[1]) for t in pair)


def kernel(x, norm_mix, norm_mlp, norm_final, ssm_a_re, ssm_a_im, ssm_b_re, ssm_b_im, ssm_c_re, ssm_c_im, ssm_d, ssm_log_dt, ssm_w_glu, conv_w_pw1, conv_b_pw1, conv_w_dw, conv_b_dw, conv_ln_g, conv_ln_b, conv_w_pw2, conv_b_pw2, gmlp_w_in, gmlp_ln_g, gmlp_ln_b, gmlp_w_s, gmlp_b_s, gmlp_w_out, attn_w_qkv, attn_w_o, mlp_w_in, mlp_w_out, loss_target, m_norm_mix, m_norm_mlp, m_norm_final, m_ssm_a_re, m_ssm_a_im, m_ssm_b_re, m_ssm_b_im, m_ssm_c_re, m_ssm_c_im, m_ssm_d, m_ssm_log_dt, m_ssm_w_glu, m_conv_w_pw1, m_conv_b_pw1, m_conv_w_dw, m_conv_b_dw, m_conv_ln_g, m_conv_ln_b, m_conv_w_pw2, m_conv_b_pw2, m_gmlp_w_in, m_gmlp_ln_g, m_gmlp_ln_b, m_gmlp_w_s, m_gmlp_b_s, m_gmlp_w_out, m_attn_w_qkv, m_attn_w_o, m_mlp_w_in, m_mlp_w_out, v_norm_mix, v_norm_mlp, v_norm_final, v_ssm_a_re, v_ssm_a_im, v_ssm_b_re, v_ssm_b_im, v_ssm_c_re, v_ssm_c_im, v_ssm_d, v_ssm_log_dt, v_ssm_w_glu, v_conv_w_pw1, v_conv_b_pw1, v_conv_w_dw, v_conv_b_dw, v_conv_ln_g, v_conv_ln_b, v_conv_w_pw2, v_conv_b_pw2, v_gmlp_w_in, v_gmlp_ln_g, v_gmlp_ln_b, v_gmlp_w_s, v_gmlp_b_s, v_gmlp_w_out, v_attn_w_qkv, v_attn_w_o, v_mlp_w_in, v_mlp_w_out):
    args = (norm_mix, norm_mlp, norm_final, ssm_a_re, ssm_a_im, ssm_b_re, ssm_b_im, ssm_c_re, ssm_c_im, ssm_d,
            ssm_log_dt, ssm_w_glu, conv_w_pw1, conv_b_pw1, conv_w_dw, conv_b_dw, conv_ln_g, conv_ln_b, conv_w_pw2,
            conv_b_pw2, gmlp_w_in, gmlp_ln_g, gmlp_ln_b, gmlp_w_s, gmlp_b_s, gmlp_w_out, attn_w_qkv, attn_w_o,
            mlp_w_in, mlp_w_out)
    margs = (m_norm_mix, m_norm_mlp, m_norm_final, m_ssm_a_re, m_ssm_a_im, m_ssm_b_re, m_ssm_b_im, m_ssm_c_re,
             m_ssm_c_im, m_ssm_d, m_ssm_log_dt, m_ssm_w_glu, m_conv_w_pw1, m_conv_b_pw1, m_conv_w_dw, m_conv_b_dw,
             m_conv_ln_g, m_conv_ln_b, m_conv_w_pw2, m_conv_b_pw2, m_gmlp_w_in, m_gmlp_ln_g, m_gmlp_ln_b,
             m_gmlp_w_s, m_gmlp_b_s, m_gmlp_w_out, m_attn_w_qkv, m_attn_w_o, m_mlp_w_in, m_mlp_w_out)
    vargs = (v_norm_mix, v_norm_mlp, v_norm_final, v_ssm_a_re, v_ssm_a_im, v_ssm_b_re, v_ssm_b_im, v_ssm_c_re,
             v_ssm_c_im, v_ssm_d, v_ssm_log_dt, v_ssm_w_glu, v_conv_w_pw1, v_conv_b_pw1, v_conv_w_dw, v_conv_b_dw,
             v_conv_ln_g, v_conv_ln_b, v_conv_w_pw2, v_conv_b_pw2, v_gmlp_w_in, v_gmlp_ln_g, v_gmlp_ln_b,
             v_gmlp_w_s, v_gmlp_b_s, v_gmlp_w_out, v_attn_w_qkv, v_attn_w_o, v_mlp_w_in, v_mlp_w_out)
    Wl = dict(zip(WEIGHT_NAMES, args))
    Ml = dict(zip(WEIGHT_NAMES, margs))
    Vl = dict(zip(WEIGHT_NAMES, vargs))
    cx, cy, cc = _coords()
    my_idx = 4 * cx + 2 * cy + cc

    c_idx = cc.reshape(1).astype(jnp.int32)
    chip_idx = (2 * cx + cy).reshape(1).astype(jnp.int32)
    comm = _StepComm(Wl, c_idx)
    units, tag = comm.units, comm.tag
    W = {n: Wl[n] for n in SMALL if n not in SMALL_SHARDED}
    loss_local, grad_x, grads, (s5_disc_vjp, s5_cot) = _step(x, loss_target, W, comm)
    loss = lax.psum(loss_local, MESH_AXES)

    outs4 = {}
    for (n, i), p, r in zip(units, comm.p1, comm.recv2):
        w2, m2, v2 = (d[n].reshape(-1, d[n].shape[-1]) for d in (Wl, Ml, Vl))
        res = adam_big("adam_" + tag(n, i), p[0], r, w2, m2, v2, chip_idx, layer=0 if i is None else i)
        if i is None:
            outs4[n] = [t.reshape(Wl[n].shape) for t in res]
        else:
            outs4.setdefault(n, []).append(res)
    for n in BIG:
        if Wl[n].shape[0] == DEPTH:
            outs4[n] = [jnp.stack([layer[k] for layer in outs4[n]]) for k in range(4)]
    out_g = {n: outs4[n][0] for n in BIG}
    out_d = {n: outs4[n][1] for n in BIG}
    out_m = {n: outs4[n][2] for n in BIG}
    out_v = {n: outs4[n][3] for n in BIG}

    s5_lin = ['ssm_a_re', 'ssm_a_im', 'ssm_log_dt', 'ssm_b_re', 'ssm_b_im']
    direct = [n for n in SMALL if n not in s5_lin]
    def full_shape(n):
        shp = list(Wl[n].shape)
        if n in SMALL_SHARDED:
            shp[SMALL_SHARDED[n]] *= N_DEV
        return tuple(shp)
    small_parts = [grads[n].reshape(full_shape(n)) for n in direct] + list(s5_cot)
    gsum = sum8(all_gather("gather_small_grads", [_pack(small_parts)])[0])
    summed = _unpack(gsum, [p.shape for p in small_parts])
    gsmall = dict(zip(direct, summed[:len(direct)]))
    s5g = s5_disc_vjp(tuple(summed[len(direct):]))
    for n, gval in zip(s5_lin, s5g):
        gsmall[n] = gval[None]
    for n, ax in SMALL_SHARDED.items():
        gsmall[n] = lax.dynamic_slice_in_dim(gsmall[n], my_idx * Wl[n].shape[ax], Wl[n].shape[ax], axis=ax)
    sm_shapes = [Wl[n].shape for n in SMALL]
    dS, mS, vS = adam_small(_pack([Wl[n] for n in SMALL]), _pack([gsmall[n] for n in SMALL]),
                            _pack([Ml[n] for n in SMALL]), _pack([Vl[n] for n in SMALL]))
    for n, gval in zip(SMALL, [gsmall[n] for n in SMALL]):
        out_g[n] = gval.reshape(Wl[n].shape)
    out_d.update(zip(SMALL, _unpack(dS, sm_shapes)))
    out_m.update(zip(SMALL, _unpack(mS, sm_shapes)))
    out_v.update(zip(SMALL, _unpack(vS, sm_shapes)))

    return (loss, grad_x, *[out_g[n] for n in WEIGHT_NAMES], *[out_d[n] for n in WEIGHT_NAMES],
            *[out_m[n] for n in WEIGHT_NAMES], *[out_v[n] for n in WEIGHT_NAMES])


def _unpack_gathered(g, shard_shapes):
    flat = g.reshape(N_DEV, -1)
    out, off = [], 0
    for s in shard_shapes:
        n = math.prod(s)
        out.append(flat[:, off:off + n].reshape((N_DEV,) + tuple(s)))
        off += n
    return out
```

```python
import functools
import math

import jax
import jax.numpy as jnp
from jax import lax
from jax.experimental import pallas as pl
from jax.experimental.pallas import tpu as pltpu

F32 = jnp.float32
BF16 = jnp.bfloat16

D_MODEL = 1024
DEPTH = 4
EPS = 1e-6
SSM_GROUP = 16
SSM_GROUPS = 64
SSM_STATE = 64
S5_GB = 8
S5_NGB = SSM_GROUPS // S5_GB
S5_CH = S5_GB * SSM_GROUP
S5_ST = S5_GB * SSM_STATE
S5_L = 256
CONV_WIDTH = 31
CONV_PAD = 32
CONV_TS = 256
CONV_CW = 256
GMLP_CHUNK = 128
GMLP_HEADS = 4
ATT_CONFIGS = ((128, 1), (512, 4), (2048, 16))
ATT_HEADS = 8
HEAD_DIM = 64
ATT_BLK = 128
ATT_TB = 2
ATT_W = ATT_HEADS * HEAD_DIM
N_DEV = 8
ADAM_LR = 0.001
ADAM_B1 = 0.9
ADAM_B2 = 0.999
ADAM_EPS = 1e-08
ADAM_WD = 0.01
ADAM_STEP = 10
VMEM_LIMIT = 56 * 1024 * 1024
PACK_C = 1024
MESH_AXES = ("x", "y", "c")
MESH = pl.DeviceIdType.MESH

WEIGHT_NAMES = ['norm_mix', 'norm_mlp', 'norm_final', 'ssm_a_re', 'ssm_a_im', 'ssm_b_re', 'ssm_b_im',
                'ssm_c_re', 'ssm_c_im', 'ssm_d', 'ssm_log_dt', 'ssm_w_glu', 'conv_w_pw1', 'conv_b_pw1',
                'conv_w_dw', 'conv_b_dw', 'conv_ln_g', 'conv_ln_b', 'conv_w_pw2', 'conv_b_pw2',
                'gmlp_w_in', 'gmlp_ln_g', 'gmlp_ln_b', 'gmlp_w_s', 'gmlp_b_s', 'gmlp_w_out',
                'attn_w_qkv', 'attn_w_o', 'mlp_w_in', 'mlp_w_out']
BIG = {'ssm_w_glu': 2, 'conv_w_pw1': 2, 'conv_w_pw2': 1, 'gmlp_w_in': 2, 'gmlp_w_out': 1,
       'attn_w_qkv': 2, 'attn_w_o': 2, 'mlp_w_in': 2, 'mlp_w_out': 1}
SMALL_SHARDED = {'conv_b_pw1': 1, 'conv_w_dw': 2, 'conv_b_dw': 1, 'conv_ln_g': 1, 'conv_ln_b': 1,
                 'conv_b_pw2': 1, 'gmlp_ln_g': 1, 'gmlp_ln_b': 1}
SMALL = [n for n in WEIGHT_NAMES if n not in BIG]


def _cparams(sem=None):
    return pltpu.CompilerParams(dimension_semantics=sem, vmem_limit_bytes=VMEM_LIMIT)


def _dot(a, b):
    return jnp.dot(a, b, preferred_element_type=F32)


def _dot_nt(a, b):
    return lax.dot_general(a, b, (((1,), (1,)), ((), ())), preferred_element_type=F32)


def _dot_tn(a, b):
    return lax.dot_general(a, b, (((0,), (0,)), ((), ())), preferred_element_type=F32)


ROW_TILE_BYTES = 10 << 20


def _rows_for(T, row_bytes, cap=1024):
    tr = min(cap, T)
    while tr > 8 and (T % tr or tr * row_bytes > ROW_TILE_BYTES):
        tr //= 2
    assert T % tr == 0 and tr % 8 == 0
    return tr


def rowwise(name, fn, rows, params, row_out, acc_out=(), tr=None):
    T = rows[0].shape[0]
    row_bytes = (sum(r.shape[1] * r.dtype.itemsize for r in rows)
                 + sum(c * jnp.dtype(dt).itemsize for c, dt in row_out))
    tr = _rows_for(T, row_bytes, cap=tr or 1024)
    nr, npar, nro = len(rows), len(params), len(row_out)

    def body(*refs):
        ins = [r[...] for r in refs[:nr + npar]]
        outs = refs[nr + npar:]
        res = fn(*ins)
        if not isinstance(res, (tuple, list)):
            res = (res,)
        for k in range(nro):
            outs[k][...] = res[k].astype(outs[k].dtype)
        if acc_out:
            @pl.when(pl.program_id(0) == 0)
            def _():
                for k in range(nro, len(outs)):
                    outs[k][...] = jnp.zeros_like(outs[k])
            for k in range(nro, len(outs)):
                outs[k][...] += res[k].astype(outs[k].dtype)

    in_specs = [pl.BlockSpec((tr, r.shape[1]), lambda i: (i, 0)) for r in rows]
    in_specs += [pl.BlockSpec(p.shape, lambda i, nd=p.ndim: (0,) * nd) for p in params]
    out_shape = [jax.ShapeDtypeStruct((T, c), dt) for c, dt in row_out]
    out_specs = [pl.BlockSpec((tr, c), lambda i: (i, 0)) for c, dt in row_out]
    out_shape += [jax.ShapeDtypeStruct(s, dt) for s, dt in acc_out]
    out_specs += [pl.BlockSpec(s, lambda i, nd=len(s): (0,) * nd) for s, dt in acc_out]
    res = pl.pallas_call(body, grid=(T // tr,), in_specs=in_specs, out_specs=out_specs, out_shape=out_shape,
                         name=name, compiler_params=_cparams(("arbitrary",)))(*rows, *params)
    return res


def _tile_m(M, K):
    tm = 2048
    while tm > 256 and tm * K * 2 > (4 << 20):
        tm //= 2
    return min(tm, M)


def matmul(name, a, b, *, mode='nn', epi=None, extras=(), out_dtypes=(F32,), out_cols=None, whole_rows=False):
    M, K = a.shape
    if mode == 'cb':
        nblk, _, tn = b.shape
        N = nblk * tn
    else:
        N = b.shape[0] if mode == 'nt' else b.shape[1]
        tn = min(512, N)
    out_cols = list(out_cols) if out_cols is not None else [N] * len(out_dtypes)
    row_bytes = (K * 2 + sum(c * jnp.dtype(dt).itemsize for c, dt in zip(out_cols, out_dtypes))
                 + sum(arr.shape[1] * arr.dtype.itemsize for arr, kind in extras if kind == 'tile'))
    tm = _rows_for(M, row_bytes)
    assert N % tn == 0, (M, N, tm, tn)
    nex = len(extras)

    def body(a_ref, b_ref, *rest):
        ex_refs, outs = rest[:nex], rest[nex:]
        av = a_ref[...]

        def product(c):
            cs = slice(c * tn, (c + 1) * tn)
            if mode == 'cb':
                return _dot(av, b_ref[c])
            return _dot_nt(av, b_ref[cs, :]) if mode == 'nt' else _dot(av, b_ref[:, cs])

        if whole_rows:
            parts = [product(c) for c in range(N // tn)]
            acc = parts[0] if len(parts) == 1 else jnp.concatenate(parts, axis=1)
            res = epi(acc, *[e[...] for e in ex_refs])
            for o, r in zip(outs, res):
                o[...] = r.astype(o.dtype)
        else:
            for c in range(N // tn):
                cs = slice(c * tn, (c + 1) * tn)
                acc = product(c)
                res = epi(acc, *[e[:, cs] for e in ex_refs]) if epi is not None else (acc,)
                for o, r in zip(outs, res):
                    o[:, cs] = r.astype(o.dtype)

    in_specs = [pl.BlockSpec((tm, K), lambda i: (i, 0)), pl.BlockSpec(b.shape, lambda i, nd=b.ndim: (0,) * nd)]
    for arr, kind in extras:
        in_specs.append(pl.BlockSpec((tm, arr.shape[1]), lambda i: (i, 0)) if kind == 'tile'
                        else pl.BlockSpec((1, arr.shape[1]), lambda i: (0, 0)))
    out_shape = [jax.ShapeDtypeStruct((M, c), dt) for c, dt in zip(out_cols, out_dtypes)]
    out_specs = [pl.BlockSpec((tm, c), lambda i: (i, 0)) for c in out_cols]
    return pl.pallas_call(body, grid=(M // tm,), in_specs=in_specs, out_specs=out_specs,
                          out_shape=out_shape, name=name,
                          compiler_params=_cparams(("arbitrary",)))(a, b, *[e[0] for e in extras])


def _gather_heads(refs, j, scr):
    for r in range(SLOTS_PER_DEV):
        k, hh = divmod(j * SLOTS_PER_DEV + r, ATT_HEADS)
        scr[:, r * HEAD_DIM:(r + 1) * HEAD_DIM] = refs[k][:, hh * HEAD_DIM:(hh + 1) * HEAD_DIM]
    return scr[...]


def matmul_nt_cb(name, a, b, *, heads=False, epi=None, extras=(), params=(), out_dtypes=(F32,), acc_out=()):
    nblk, K, n = b.shape
    a_list = list(a) if heads else [a]
    na = len(a_list)
    M = a_list[0].shape[0]
    tm = _tile_m(M, nblk * n)
    assert M % tm == 0
    nex, npar, nro = len(extras), len(params), len(out_dtypes)

    def body(*refs):
        a_refs, b_ref, rest = refs[:na], refs[na], refs[na + 1:]
        if heads:
            rest, scr = rest[:-2], rest[-2:]
        ex, outs = rest[:nex + npar], rest[nex + npar:]
        acc = None
        for j in range(nblk):
            aj = _gather_heads(a_refs, j, scr[j % 2]) if heads else a_refs[0][:, j * n:(j + 1) * n]
            part = _dot_nt(aj, b_ref[j])
            acc = part if acc is None else acc + part
        res = epi(acc, *[e[...] for e in ex]) if epi is not None else (acc,)
        for o, r in zip(outs[:nro], res[:nro]):
            o[...] = r.astype(o.dtype)
        if acc_out:
            @pl.when(pl.program_id(0) == 0)
            def _():
                for o in outs[nro:]:
                    o[...] = jnp.zeros_like(o)
            for o, r in zip(outs[nro:], res[nro:]):
                o[...] += r.astype(o.dtype)

    a_specs = [pl.BlockSpec((tm, t.shape[1]), lambda i: (i, 0)) for t in a_list]
    row = pl.BlockSpec((tm, K), lambda i: (i, 0))
    const = lambda shp: pl.BlockSpec(shp, lambda i, nd=len(shp): (0,) * nd)
    return pl.pallas_call(
        body, grid=(M // tm,),
        in_specs=a_specs + [pl.BlockSpec((nblk, K, n), lambda i: (0, 0, 0))] + [row] * nex
        + [const(p.shape) for p in params],
        out_specs=[row] * nro + [const(s) for s, dt in acc_out],
        out_shape=[jax.ShapeDtypeStruct((M, K), dt) for dt in out_dtypes]
        + [jax.ShapeDtypeStruct(s, dt) for s, dt in acc_out],
        scratch_shapes=[pltpu.VMEM((tm, n), BF16)] * 2 if heads else [],
        name=name, compiler_params=_cparams(("arbitrary",)))(*a_list, b, *extras, *params)


def wgrad(name, a, g, *, cb=False, heads=False):
    M, K = a.shape
    g_list = list(g) if heads else [g]
    tm, tk = min(M, 512 if heads else 1024), min(K, 512 if heads else 1024)
    if cb:
        n = SLOTS_PER_DEV * HEAD_DIM if heads else g.shape[1] // N_DEV
        nj = N_DEV
        while nj > 1 and nj * tk * n * 6 > (14 << 20):
            nj //= 2
        assert nj == N_DEV or not heads
        grid = (K // tk, N_DEV // nj, M // tm)
        g_specs = ([pl.BlockSpec((tm, t.shape[1]), lambda k, j, m: (m, 0)) for t in g_list] if heads
                   else [pl.BlockSpec((tm, nj * n), lambda k, j, m: (m, j))])
        o_spec = pl.BlockSpec((nj, tk, n), lambda k, j, m: (j, k, 0))
        o_shape = (N_DEV, K, n)
    else:
        N = g.shape[1]
        tn = min(N, 1024)
        nj = 1
        grid = (K // tk, N // tn, M // tm)
        g_specs = [pl.BlockSpec((tm, tn), lambda k, j, m: (m, j))]
        o_spec = pl.BlockSpec((tk, tn), lambda k, j, m: (k, j))
        o_shape = (K, N)
    nm = M // tm
    ng = len(g_list)

    def body(a_ref, *rest):
        g_refs, o_ref, o16_ref, scr = rest[:ng], rest[ng], rest[ng + 1], rest[ng + 2:]
        m = pl.program_id(2)

        @pl.when(m == 0)
        def _():
            o_ref[...] = jnp.zeros_like(o_ref)
        at = a_ref[...].T
        if cb:
            for jj in range(nj):
                gj = (_gather_heads(g_refs, jj, scr[jj % 2]) if heads
                      else g_refs[0][:, jj * n:(jj + 1) * n])
                o_ref[jj] += _dot(at, gj)
        else:
            o_ref[...] += _dot(at, g_refs[0][...])

        @pl.when(m == nm - 1)
        def _():
            o16_ref[...] = o_ref[...].astype(BF16)

    return pl.pallas_call(
        body, grid=grid, in_specs=[pl.BlockSpec((tm, tk), lambda k, j, m: (m, k))] + g_specs,
        out_specs=[o_spec, o_spec],
        out_shape=[jax.ShapeDtypeStruct(o_shape, F32), jax.ShapeDtypeStruct(o_shape, BF16)],
        scratch_shapes=[pltpu.VMEM((tm, SLOTS_PER_DEV * HEAD_DIM), BF16)] * 2 if heads else [],
        name=name, compiler_params=_cparams(("arbitrary", "arbitrary", "arbitrary")))(a, *g_list)


def _rms(x, g):
    x = x.astype(F32)
    return x * lax.rsqrt(jnp.mean(x * x, axis=-1, keepdims=True) + EPS) * g


def _ln(x, g, b):
    mu = jnp.mean(x, axis=-1, keepdims=True)
    var = jnp.mean(jnp.square(x - mu), axis=-1, keepdims=True)
    return (x - mu) * lax.rsqrt(var + EPS) * g + b


def _glu(z):
    d = z.shape[1] // 2
    return z[:, :d] * jax.nn.sigmoid(z[:, d:])


def _glu_bwd(z, dy):
    d = z.shape[1] // 2
    a, s = z[:, :d], jax.nn.sigmoid(z[:, d:])
    return jnp.concatenate([dy * s, dy * a * s * (1.0 - s)], axis=1)


def _colsum(v):
    return jnp.sum(v.astype(F32), axis=0, keepdims=True)


def rms_fwd(name, x, g, want_f32=False):
    def fn(xt, gt):
        h = _rms(xt, gt)
        return (h, h) if want_f32 else (h,)
    D = x.shape[1]
    outs = [(D, BF16)] + ([(D, F32)] if want_f32 else [])
    return rowwise(name, fn, [x], [g], outs)


def rms_bwd(name, x, dh, dres, g):
    def fn(xt, dht, drt, gt):
        _, vjp = jax.vjp(_rms, xt, gt)
        dx, dg = vjp(dht.astype(F32))
        dx = dx + drt
        return dx, dx, dg
    D = x.shape[1]
    return rowwise(name, fn, [x, dh, dres], [g], [(D, F32), (D, BF16)], [((1, D), F32)])


def s5_disc(a_re, a_im, log_dt, b_re, b_im):
    dt = jnp.exp(log_dt)[:, None]
    er = jnp.exp(a_re * dt)
    lam_re = er * jnp.cos(a_im * dt)
    lam_im = er * jnp.sin(a_im * dt)
    nr, ni = lam_re - 1.0, lam_im
    den = a_re * a_re + a_im * a_im
    f_re = (nr * a_re + ni * a_im) / den
    f_im = (ni * a_re - nr * a_im) / den
    bb_re = f_re[..., None] * b_re - f_im[..., None] * b_im
    bb_im = f_re[..., None] * b_im + f_im[..., None] * b_re
    return lam_re, lam_im, bb_re, bb_im


def _s5_blockdiag_b(bb):
    t = bb.reshape(S5_NGB, S5_GB, SSM_STATE, SSM_GROUP).transpose(0, 1, 3, 2)
    eye = jnp.eye(S5_GB, dtype=bb.dtype)
    return jnp.einsum('bgpn,gh->bgphn', t, eye).reshape(S5_NGB, S5_CH, S5_ST)


def _s5_blockdiag_b_inv(x):
    t = x.reshape(S5_NGB, S5_GB, SSM_GROUP, S5_GB, SSM_STATE)
    eye = jnp.eye(S5_GB, dtype=x.dtype)
    d = jnp.einsum('bgphn,gh->bgpn', t, eye)
    return d.transpose(0, 1, 3, 2).reshape(SSM_GROUPS, SSM_STATE, SSM_GROUP)


def _s5_blockdiag_c(c):
    t = c.reshape(S5_NGB, S5_GB, SSM_GROUP, SSM_STATE).transpose(0, 1, 3, 2)
    eye = jnp.eye(S5_GB, dtype=c.dtype)
    return jnp.einsum('bgnp,gh->bgnhp', t, eye).reshape(S5_NGB, S5_ST, S5_CH)


def _s5_blockdiag_c_inv(x):
    t = x.reshape(S5_NGB, S5_GB, SSM_STATE, S5_GB, SSM_GROUP)
    eye = jnp.eye(S5_GB, dtype=x.dtype)
    d = jnp.einsum('bgnhp,gh->bgnp', t, eye)
    return d.transpose(0, 1, 3, 2).reshape(SSM_GROUPS, SSM_GROUP, SSM_STATE)


def s5_tables(lam_re, lam_im, L):
    pr, pi = lam_re.reshape(1, -1), lam_im.reshape(1, -1)
    n = 1
    while n < L:
        lr, li = pr[n - 1:n], pi[n - 1:n]
        pr, pi = (jnp.concatenate([pr, pr * lr - pi * li], 0), jnp.concatenate([pi, pr * li + pi * lr], 0))
        n *= 2
    nk = int(math.log2(L))
    idx = [2 ** k - 1 for k in range(nk)] + [0] * (8 - nk)

    def blk(t):
        return t.reshape(t.shape[0], S5_NGB, S5_ST).transpose(1, 0, 2)

    def rows(t):
        return jnp.concatenate([t[j:j + 1] for j in idx], axis=0)
    return blk(pr), blk(pi), blk(rows(pr)), blk(rows(pi))


S5_SUB = 8


def _scan_tiles(br, bi, a2r, a2i, reverse):
    L = br.shape[0]
    sub = lax.broadcasted_iota(jnp.int32, br.shape, 0) & (S5_SUB - 1)
    xr, xi = br, bi
    for k in range(3):
        s = 1 << k
        ar, ai = a2r[k:k + 1, :], a2i[k:k + 1, :]
        if reverse:
            sr, si = pltpu.roll(xr, L - s, 0), pltpu.roll(xi, L - s, 0)
            m = sub < S5_SUB - s
        else:
            sr, si = pltpu.roll(xr, s, 0), pltpu.roll(xi, s, 0)
            m = sub >= s
        sr, si = jnp.where(m, sr, 0.0), jnp.where(m, si, 0.0)
        xr, xi = xr + ar * sr - ai * si, xi + ar * si + ai * sr
    return xr, xi


def _scan_chain(xr, xi, pr, pi, cr, ci, out_r, out_i, reverse):
    ntile = xr.shape[0] // S5_SUB
    for g in (reversed(range(ntile)) if reverse else range(ntile)):
        rs = slice(g * S5_SUB, (g + 1) * S5_SUB)
        if reverse:
            nr = xr[rs] + pr * cr + pi * ci
            ni = xi[rs] + pr * ci - pi * cr
            cr, ci = nr[0:1], ni[0:1]
        else:
            nr = xr[rs] + pr * cr - pi * ci
            ni = xi[rs] + pr * ci + pi * cr
            cr, ci = nr[S5_SUB - 1:S5_SUB], ni[S5_SUB - 1:S5_SUB]
        out_r[rs, :] = nr
        out_i[rs, :] = ni
    return cr, ci


def _grid_step(shape):
    s = 0
    for ax, n in enumerate(shape):
        s = s * n + pl.program_id(ax)
    return s


def s5_fwd(h, bre, bim, cre, cim, pwr, pwi, l2r, l2i, dskip, bsz, gather=()):
    T, D = h.shape
    L = S5_L
    S = T // bsz
    NC = S // L
    ng = len(gather)
    grid = (S5_NGB, bsz, NC)
    nsteps = S5_NGB * bsz * NC
    fwd_step = nsteps - max(1, nsteps // 32)

    def body(*refs):
        (h_ref, bre_ref, bim_ref, cre_ref, cim_ref, pwr_ref, pwi_ref, l2r_ref, l2i_ref, d_ref) = refs[:10]
        x_refs = refs[10:10 + ng]
        y_ref, gy_ref, xs_ref, xr_s, xi_s = refs[10 + ng:15 + ng]
        g_refs = refs[15 + ng:15 + 2 * ng]
        car_r, car_i = refs[15 + 2 * ng:17 + 2 * ng]
        if ng:
            start, forward, finish = _gather_phases(x_refs, g_refs, *refs[17 + 2 * ng:])
            step = _grid_step(grid)
            pl.when(step == 0)(start)
            pl.when(step == fwd_step)(forward)

        @pl.when(pl.program_id(2) == 0)
        def _():
            car_r[...] = jnp.zeros_like(car_r)
            car_i[...] = jnp.zeros_like(car_i)
        u = h_ref[...]
        ub = u.astype(BF16)
        cr, ci = car_r[0:1, :], car_i[0:1, :]
        xs_ref[...] = jnp.zeros_like(xs_ref)
        xs_ref[0:1, :] = cr
        xs_ref[1:2, :] = ci
        xr, xi = _scan_tiles(_dot(ub, bre_ref[...]), _dot(ub, bim_ref[...]), l2r_ref[...], l2i_ref[...], False)
        cr, ci = _scan_chain(xr, xi, pwr_ref[...], pwi_ref[...], cr, ci, xr_s, xi_s, False)
        car_r[...] = jnp.broadcast_to(cr, car_r.shape)
        car_i[...] = jnp.broadcast_to(ci, car_i.shape)
        y = (_dot(xr_s[...].astype(BF16), cre_ref[...]) - _dot(xi_s[...].astype(BF16), cim_ref[...])
             + d_ref[...] * u)
        y_ref[...] = y
        gy_ref[...] = jax.nn.gelu(y).astype(BF16)
        if ng:
            pl.when(step == nsteps - 1)(finish)

    tok = lambda g, b, c: (b * NC + c, g)
    par = lambda g, b, c: (g, 0, 0)
    anyspec = pl.BlockSpec(memory_space=pl.ANY)
    return pl.pallas_call(
        body, grid=grid,
        in_specs=[pl.BlockSpec((L, S5_CH), tok),
                  pl.BlockSpec((None, S5_CH, S5_ST), par), pl.BlockSpec((None, S5_CH, S5_ST), par),
                  pl.BlockSpec((None, S5_ST, S5_CH), par), pl.BlockSpec((None, S5_ST, S5_CH), par),
                  pl.BlockSpec((None, 8, S5_ST), par), pl.BlockSpec((None, 8, S5_ST), par),
                  pl.BlockSpec((None, 8, S5_ST), par), pl.BlockSpec((None, 8, S5_ST), par),
                  pl.BlockSpec((1, S5_CH), lambda g, b, c: (0, g))] + [anyspec] * ng,
        out_specs=[pl.BlockSpec((L, S5_CH), tok), pl.BlockSpec((L, S5_CH), tok),
                   pl.BlockSpec((None, 8, S5_ST), lambda g, b, c: (b * NC + c, 0, g)),
                   pl.BlockSpec((L, S5_ST), tok), pl.BlockSpec((L, S5_ST), tok)] + [anyspec] * ng,
        out_shape=[jax.ShapeDtypeStruct((T, D), F32), jax.ShapeDtypeStruct((T, D), BF16),
                   jax.ShapeDtypeStruct((bsz * NC, 8, S5_NGB * S5_ST), F32),
                   jax.ShapeDtypeStruct((T, S5_NGB * S5_ST), F32), jax.ShapeDtypeStruct((T, S5_NGB * S5_ST), F32)]
        + _gather_out_shapes(gather),
        scratch_shapes=[pltpu.VMEM((8, S5_ST), F32), pltpu.VMEM((8, S5_ST), F32)]
        + (_gather_sems(ng) if ng else []),
        name="s5_fwd", compiler_params=_cparams(("arbitrary", "arbitrary", "arbitrary")),
    )(h, bre, bim, cre, cim, pwr, pwi, l2r, l2i, dskip, *gather)


def s5_bwd(h, ypre, dgy, xs, xr, xi, bre, bim, cre, cim, pwr_rev, pwi_rev, l2r, l2i, dskip, bsz, chips=()):
    T, D = h.shape
    L = S5_L
    S = T // bsz
    NC = S // L
    nc = len(chips)
    grid = (S5_NGB, bsz, NC)
    nsteps = S5_NGB * bsz * NC

    def body(*refs):
        (h_ref, yp_ref, dg_ref, xs_ref, xr_ref, xi_ref, bre_ref, bim_ref, cre_ref, cim_ref, qr_ref, qi_ref,
         l2r_ref, l2i_ref, d_ref) = refs[:15]
        p_refs = refs[15:15 + nc]
        du_ref, dbr_ref, dbi_ref, dcr_ref, dci_ref, dl_ref, dd_ref = refs[15 + nc:22 + nc]
        r_refs = refs[22 + nc:22 + 2 * nc]
        car_r, car_i, dr_s, di_s = refs[22 + 2 * nc:26 + 2 * nc]
        if nc:
            start, finish = _chips_phases(p_refs, r_refs, *refs[26 + 2 * nc:])
            step = _grid_step(grid)
            pl.when(step == 0)(start)
        first = (pl.program_id(1) == 0) & (pl.program_id(2) == 0)

        @pl.when(first)
        def _():
            for r in (dbr_ref, dbi_ref, dcr_ref, dci_ref, dl_ref, dd_ref):
                r[...] = jnp.zeros_like(r)

        @pl.when(pl.program_id(2) == 0)
        def _():
            car_r[...] = jnp.zeros_like(car_r)
            car_i[...] = jnp.zeros_like(car_i)

        u = h_ref[...]
        ub = u.astype(BF16)
        dyv = jax.vjp(jax.nn.gelu, yp_ref[...])[1](dg_ref[...])[0]
        dyb = dyv.astype(BF16)
        l2r_v, l2i_v = l2r_ref[...], l2i_ref[...]
        x0r, x0i = xs_ref[0:1, :], xs_ref[1:2, :]
        xr, xi = xr_ref[...], xi_ref[...]
        gr = _dot_nt(dyb, cre_ref[...])
        gi = -_dot_nt(dyb, cim_ref[...])
        dr, di = _scan_tiles(gr, gi, l2r_v, -l2i_v, True)
        cr, ci = _scan_chain(dr, di, qr_ref[...], qi_ref[...], car_r[0:1, :], car_i[0:1, :], dr_s, di_s, True)
        dr, di = dr_s[...], di_s[...]
        car_r[...] = jnp.broadcast_to(cr, car_r.shape)
        car_i[...] = jnp.broadcast_to(ci, car_i.shape)
        row = lax.broadcasted_iota(jnp.int32, xr.shape, 0)
        xpr = jnp.where(row >= 1, pltpu.roll(xr, 1, 0), x0r)
        xpi = jnp.where(row >= 1, pltpu.roll(xi, 1, 0), x0i)
        dl_ref[0:1, :] += _colsum(dr * xpr + di * xpi)
        dl_ref[1:2, :] += _colsum(di * xpr - dr * xpi)
        drb, dib = dr.astype(BF16), di.astype(BF16)
        dcr_ref[...] += _dot_tn(xr.astype(BF16), dyb)
        dci_ref[...] -= _dot_tn(xi.astype(BF16), dyb)
        dbr_ref[...] += _dot_tn(ub, drb)
        dbi_ref[...] += _dot_tn(ub, dib)
        du_ref[...] = _dot_nt(drb, bre_ref[...]) + _dot_nt(dib, bim_ref[...]) + d_ref[...] * dyv
        dd_ref[0:1, :] += _colsum(dyv * u)
        if nc:
            pl.when(step == nsteps - 1)(finish)

    tok = lambda g, b, c: (b * NC + (NC - 1 - c), g)
    par = lambda g, b, c: (g, 0, 0)
    anyspec = pl.BlockSpec(memory_space=pl.ANY)
    return pl.pallas_call(
        body, grid=grid,
        in_specs=[pl.BlockSpec((L, S5_CH), tok), pl.BlockSpec((L, S5_CH), tok), pl.BlockSpec((L, S5_CH), tok),
                  pl.BlockSpec((None, 8, S5_ST), lambda g, b, c: (b * NC + (NC - 1 - c), 0, g)),
                  pl.BlockSpec((L, S5_ST), tok), pl.BlockSpec((L, S5_ST), tok),
                  pl.BlockSpec((None, S5_CH, S5_ST), par), pl.BlockSpec((None, S5_CH, S5_ST), par),
                  pl.BlockSpec((None, S5_ST, S5_CH), par), pl.BlockSpec((None, S5_ST, S5_CH), par),
                  pl.BlockSpec((None, 8, S5_ST), par), pl.BlockSpec((None, 8, S5_ST), par),
                  pl.BlockSpec((None, 8, S5_ST), par), pl.BlockSpec((None, 8, S5_ST), par),
                  pl.BlockSpec((1, S5_CH), lambda g, b, c: (0, g))] + [anyspec] * nc,
        out_specs=[pl.BlockSpec((L, S5_CH), tok),
                   pl.BlockSpec((None, S5_CH, S5_ST), par), pl.BlockSpec((None, S5_CH, S5_ST), par),
                   pl.BlockSpec((None, S5_ST, S5_CH), par), pl.BlockSpec((None, S5_ST, S5_CH), par),
                   pl.BlockSpec((None, 8, S5_ST), par),
                   pl.BlockSpec((8, S5_CH), lambda g, b, c: (0, g))] + [anyspec] * nc,
        out_shape=[jax.ShapeDtypeStruct((T, D), F32),
                   jax.ShapeDtypeStruct((S5_NGB, S5_CH, S5_ST), F32), jax.ShapeDtypeStruct((S5_NGB, S5_CH, S5_ST), F32),
                   jax.ShapeDtypeStruct((S5_NGB, S5_ST, S5_CH), F32), jax.ShapeDtypeStruct((S5_NGB, S5_ST, S5_CH), F32),
                   jax.ShapeDtypeStruct((S5_NGB, 8, S5_ST), F32), jax.ShapeDtypeStruct((8, D), F32)]
        + _chips_out_shapes(chips),
        scratch_shapes=[pltpu.VMEM((8, S5_ST), F32), pltpu.VMEM((8, S5_ST), F32)]
        + [pltpu.VMEM((L, S5_ST), F32)] * 2 + (_chips_sems(nc) if nc else []),
        name="s5_bwd", compiler_params=_cparams(("arbitrary", "arbitrary", "arbitrary")),
    )(h, ypre, dgy, xs, xr, xi, bre, bim, cre, cim, pwr_rev, pwi_rev, l2r, l2i, dskip, *chips)


def _shift_rows(win, off, n):
    if off == 0:
        return win[:n]
    return pltpu.roll(win, win.shape[0] - off, 0)[:n]


def dwconv_fwd(z, w, b, bsz):
    T, D = z.shape
    S = T // bsz
    TS, CW, PAD = CONV_TS, CONV_CW, CONV_PAD

    def body(z_ref, w_ref, b_ref, y_ref, zp):
        zp[0:PAD, :] = jnp.zeros((PAD, CW), F32)
        zp[PAD:, :] = z_ref[...]
        wv, bv = w_ref[...], b_ref[...]

        def step(t, carry):
            base = pl.multiple_of(t * TS, TS)
            win = zp[pl.ds(base, TS + PAD), :]
            acc = jnp.zeros((TS, CW), F32) + bv
            for k in range(CONV_WIDTH):
                acc = acc + wv[k:k + 1, :] * _shift_rows(win, PAD - (CONV_WIDTH - 1) + k, TS)
            y_ref[pl.ds(base, TS), :] = acc
            return carry
        lax.fori_loop(0, S // TS, step, 0)

    return pl.pallas_call(
        body, grid=(D // CW, bsz),
        in_specs=[pl.BlockSpec((S, CW), lambda c, bb: (bb, c)), pl.BlockSpec((32, CW), lambda c, bb: (0, c)),
                  pl.BlockSpec((1, CW), lambda c, bb: (0, c))],
        out_specs=pl.BlockSpec((S, CW), lambda c, bb: (bb, c)),
        out_shape=jax.ShapeDtypeStruct((T, D), F32),
        scratch_shapes=[pltpu.VMEM((S + PAD, CW), F32)],
        name="dwconv_fwd", compiler_params=_cparams(("arbitrary", "arbitrary")),
    )(z, w, b)


def dwconv_bwd(z, dy, w, bsz):
    T, D = z.shape
    S = T // bsz
    TS, CW, PAD = CONV_TS, CONV_CW, CONV_PAD

    def body(z_ref, dy_ref, w_ref, dz_ref, dw_ref, db_ref, zp, dyp):
        @pl.when(pl.program_id(1) == 0)
        def _():
            dw_ref[...] = jnp.zeros_like(dw_ref)
            db_ref[...] = jnp.zeros_like(db_ref)
        zp[0:PAD, :] = jnp.zeros((PAD, CW), F32)
        zp[PAD:, :] = z_ref[...]
        dyp[0:S, :] = dy_ref[...]
        dyp[S:, :] = jnp.zeros((PAD, CW), F32)
        wv = w_ref[...]

        def step(t, carry):
            base = pl.multiple_of(t * TS, TS)
            zwin = zp[pl.ds(base, TS + PAD), :]
            dwin = dyp[pl.ds(base, TS + PAD), :]
            dyt = dwin[:TS]
            acc = jnp.zeros((TS, CW), F32)
            for j in range(CONV_WIDTH):
                k = CONV_WIDTH - 1 - j
                acc = acc + wv[k:k + 1, :] * _shift_rows(dwin, j, TS)
            dz_ref[pl.ds(base, TS), :] = acc
            for k in range(CONV_WIDTH):
                prod = dyt * _shift_rows(zwin, PAD - (CONV_WIDTH - 1) + k, TS)
                dw_ref[8 * k:8 * k + 8, :] += jnp.sum(prod.reshape(TS // 8, 8, CW), axis=0)
            db_ref[...] += jnp.sum(dyt.reshape(TS // 8, 8, CW), axis=0)
            return carry
        lax.fori_loop(0, S // TS, step, 0)

    dz, dw, db = pl.pallas_call(
        body, grid=(D // CW, bsz),
        in_specs=[pl.BlockSpec((S, CW), lambda c, bb: (bb, c)), pl.BlockSpec((S, CW), lambda c, bb: (bb, c)),
                  pl.BlockSpec((32, CW), lambda c, bb: (0, c))],
        out_specs=[pl.BlockSpec((S, CW), lambda c, bb: (bb, c)), pl.BlockSpec((8 * 32, CW), lambda c, bb: (0, c)),
                   pl.BlockSpec((8, CW), lambda c, bb: (0, c))],
        out_shape=[jax.ShapeDtypeStruct((T, D), F32), jax.ShapeDtypeStruct((8 * 32, D), F32),
                   jax.ShapeDtypeStruct((8, D), F32)],
        scratch_shapes=[pltpu.VMEM((S + PAD, CW), F32), pltpu.VMEM((S + PAD, CW), F32)],
        name="dwconv_bwd", compiler_params=_cparams(("arbitrary", "arbitrary")),
    )(z, dy, w)
    return dz, dw.reshape(32, 8, D).sum(axis=1), db.sum(axis=0, keepdims=True)


def spatial_fwd(u, vln, ws, bias):
    T, E = u.shape
    C, H = GMLP_CHUNK, GMLP_HEADS
    hw = E // H

    def body(u_ref, v_ref, ws_ref, b_ref, o_ref):
        for hh in range(H):
            sl = slice(hh * hw, (hh + 1) * hw)
            vp = _dot(ws_ref[hh], v_ref[:, sl]) + b_ref[:, sl]
            o_ref[:, sl] = (u_ref[:, sl] * vp).astype(o_ref.dtype)

    return pl.pallas_call(
        body, grid=(T // C,),
        in_specs=[pl.BlockSpec((C, E), lambda i: (i, 0)), pl.BlockSpec((C, E), lambda i: (i, 0)),
                  pl.BlockSpec((H, C, C), lambda i: (0, 0, 0)), pl.BlockSpec((C, E), lambda i: (0, 0))],
        out_specs=pl.BlockSpec((C, E), lambda i: (i, 0)),
        out_shape=jax.ShapeDtypeStruct((T, E), BF16),
        name="spatial_fwd", compiler_params=_cparams(("arbitrary",)),
    )(u, vln, ws, bias)


def spatial_bwd(u, vln, dg, ws, bias):
    T, E = u.shape
    C, H = GMLP_CHUNK, GMLP_HEADS
    hw = E // H

    def body(u_ref, v_ref, dg_ref, ws_ref, b_ref, du_ref, dv_ref, dws_ref, db_ref):
        @pl.when(pl.program_id(0) == 0)
        def _():
            dws_ref[...] = jnp.zeros_like(dws_ref)
            db_ref[...] = jnp.zeros_like(db_ref)
        tril = (lax.broadcasted_iota(jnp.int32, (C, C), 1) <= lax.broadcasted_iota(jnp.int32, (C, C), 0))
        for hh in range(H):
            sl = slice(hh * hw, (hh + 1) * hw)
            v = v_ref[:, sl]
            w = ws_ref[hh]
            dgv = dg_ref[:, sl].astype(F32)
            vp = _dot(w, v) + b_ref[:, sl]
            du_ref[:, sl] = dgv * vp
            dvp = dgv * u_ref[:, sl]
            dvpb = dvp.astype(BF16)
            dv_ref[:, sl] = _dot_tn(w, dvpb)
            dws_ref[hh] += jnp.where(tril, _dot_nt(dvpb, v), 0.0)
            db_ref[:, sl] += dvp

    return pl.pallas_call(
        body, grid=(T // C,),
        in_specs=[pl.BlockSpec((C, E), lambda i: (i, 0)), pl.BlockSpec((C, E), lambda i: (i, 0)),
                  pl.BlockSpec((C, E), lambda i: (i, 0)),
                  pl.BlockSpec((H, C, C), lambda i: (0, 0, 0)), pl.BlockSpec((C, E), lambda i: (0, 0))],
        out_specs=[pl.BlockSpec((C, E), lambda i: (i, 0)), pl.BlockSpec((C, E), lambda i: (i, 0)),
                   pl.BlockSpec((H, C, C), lambda i: (0, 0, 0)), pl.BlockSpec((C, E), lambda i: (0, 0))],
        out_shape=[jax.ShapeDtypeStruct((T, E), F32), jax.ShapeDtypeStruct((T, E), F32),
                   jax.ShapeDtypeStruct((H, C, C), F32), jax.ShapeDtypeStruct((C, E), F32)],
        name="spatial_bwd", compiler_params=_cparams(("arbitrary",)),
    )(u, vln, dg, ws, bias)


def _att_masks():
    r = lax.broadcasted_iota(jnp.int32, (ATT_BLK, ATT_BLK), 0)
    c = lax.broadcasted_iota(jnp.int32, (ATT_BLK, ATT_BLK), 1)
    return c <= r, c >= r


NEG = -1e30
ATT_SCALE = HEAD_DIM ** -0.5


def _att_view(t, dil):
    return t.reshape(t.shape[0] // dil, dil * t.shape[1])


def attn_fwd(name, q, k, v, dil, bsz):
    T, Wd = q.shape
    nb = T // (bsz * dil * ATT_BLK)
    TB = min(nb, ATT_TB)
    nsteps = nb // TB

    def body(q_ref, k_ref, v_ref, kp_ref, vp_ref, o_ref, l_ref):
        n = pl.program_id(2)
        mc, mp = _att_masks()
        for j in range(TB):
            rows = slice(j * ATT_BLK, (j + 1) * ATT_BLK)
            prow = slice((j - 1) * ATT_BLK, j * ATT_BLK)
            hp = (n * TB + j) > 0
            H = range(ATT_HEADS)
            ls = [slice(hh * HEAD_DIM, (hh + 1) * HEAD_DIM) for hh in H]
            qj = [q_ref[rows, ls[hh]] for hh in H]
            kc = [k_ref[rows, ls[hh]] for hh in H]
            kp = [k_ref[prow, ls[hh]] if j > 0 else kp_ref[:, ls[hh]] for hh in H]
            sc = [jnp.where(mc, _dot_nt(qj[hh], kc[hh]) * ATT_SCALE, NEG) for hh in H]
            sp = [jnp.where(mp & hp, _dot_nt(qj[hh], kp[hh]) * ATT_SCALE, NEG) for hh in H]
            m = [jnp.maximum(jnp.max(sc[hh], axis=1, keepdims=True), jnp.max(sp[hh], axis=1, keepdims=True))
                 for hh in H]
            pc = [jnp.exp(sc[hh] - m[hh]) for hh in H]
            pp = [jnp.exp(sp[hh] - m[hh]) for hh in H]
            l = [jnp.sum(pc[hh], axis=1, keepdims=True) + jnp.sum(pp[hh], axis=1, keepdims=True) for hh in H]
            vc = [v_ref[rows, ls[hh]] for hh in H]
            vp = [v_ref[prow, ls[hh]] if j > 0 else vp_ref[:, ls[hh]] for hh in H]
            for hh in H:
                o_ref[rows, ls[hh]] = (_dot(pc[hh].astype(BF16), vc[hh]) + _dot(pp[hh].astype(BF16), vp[hh])) / l[hh]
                l_ref[rows, ls[hh]] = jnp.broadcast_to(m[hh] + jnp.log(l[hh]), (ATT_BLK, HEAD_DIM))

    blk = pl.BlockSpec((TB * ATT_BLK, Wd), lambda b, r, n: (b * nsteps + n, r))
    prev = pl.BlockSpec((ATT_BLK, Wd), lambda b, r, n: (jnp.maximum(b * nb + n * TB - 1, 0), r))
    qv, kv, vv = (_att_view(t, dil) for t in (q, k, v))
    o, l = pl.pallas_call(
        body, grid=(bsz, dil, nsteps), in_specs=[blk, blk, blk, prev, prev], out_specs=[blk, blk],
        out_shape=[jax.ShapeDtypeStruct(qv.shape, F32), jax.ShapeDtypeStruct(qv.shape, F32)],
        name=name, compiler_params=_cparams(("arbitrary", "arbitrary", "arbitrary")),
    )(qv, kv, vv, kv, vv)
    return o.reshape(T, Wd), l.reshape(T, Wd)


def attn_bwd(name, q, k, v, do, mg, lse, dil, bsz):
    T, Wd = q.shape
    nb = T // (bsz * dil * ATT_BLK)
    TB = min(nb, ATT_TB)
    nsteps = nb // TB

    def body(q_ref, k_ref, v_ref, do_ref, mg_ref, l_ref, kp_ref, vp_ref, qn_ref, don_ref, mgn_ref, ln_ref,
             dq_ref, dk_ref, dv_ref):
        n = pl.program_id(2)
        mc, mp = _att_masks()

        def probs_all(qs, ks, lse_cols, mask):
            s = [_dot_nt(qh, kh) * ATT_SCALE for qh, kh in zip(qs, ks)]
            return [jnp.where(mask, jnp.exp(sh - lc), 0.0) for sh, lc in zip(s, lse_cols)]

        def ds_all(ps, dos, vs, deltas):
            dp = [_dot_nt(dh, vh) for dh, vh in zip(dos, vs)]
            return [(ph * (dph - dl) * ATT_SCALE).astype(BF16) for ph, dph, dl in zip(ps, dp, deltas)]

        H = range(ATT_HEADS)
        ls = [slice(hh * HEAD_DIM, (hh + 1) * HEAD_DIM) for hh in H]
        dk = [[None] * TB for _ in H]
        dv = [[None] * TB for _ in H]
        for j in range(TB + 1):
            rows = slice(j * ATT_BLK, (j + 1) * ATT_BLK)
            prow = slice((j - 1) * ATT_BLK, j * ATT_BLK)
            if j < TB:
                srcs = (q_ref, do_ref, mg_ref, l_ref)
                qj, doj, mgj, lj = ([r[rows, ls[hh]] for hh in H] for r in srcs)
                hp = (n * TB + j) > 0
            else:
                srcs = (qn_ref, don_ref, mgn_ref, ln_ref)
                qj, doj, mgj, lj = ([r[:, ls[hh]] for hh in H] for r in srcs)
                hp = (n + 1) * TB < nb
            lse_col = [lj[hh][:, 0:1] for hh in H]
            delta = [jnp.sum(doj[hh].astype(F32) * mgj[hh].astype(F32), axis=1, keepdims=True) for hh in H]
            if j > 0:
                kp = [k_ref[prow, ls[hh]] for hh in H]
                vp = [v_ref[prow, ls[hh]] for hh in H]
            else:
                kp = [kp_ref[:, ls[hh]] for hh in H]
                vp = [vp_ref[:, ls[hh]] for hh in H]
            pp = probs_all(qj, kp, lse_col, mp & hp)
            dsp = ds_all(pp, doj, vp, delta)
            if j > 0:
                for hh in H:
                    dk[hh][j - 1] = dk[hh][j - 1] + _dot_tn(dsp[hh], qj[hh])
                    dv[hh][j - 1] = dv[hh][j - 1] + _dot_tn(pp[hh].astype(BF16), doj[hh])
            if j < TB:
                kc = [k_ref[rows, ls[hh]] for hh in H]
                vc = [v_ref[rows, ls[hh]] for hh in H]
                pc = probs_all(qj, kc, lse_col, mc)
                dsc = ds_all(pc, doj, vc, delta)
                for hh in H:
                    dq_ref[rows, ls[hh]] = (_dot(dsc[hh], kc[hh]) + _dot(dsp[hh], kp[hh])).astype(dq_ref.dtype)
                for hh in H:
                    dk[hh][j] = _dot_tn(dsc[hh], qj[hh])
                    dv[hh][j] = _dot_tn(pc[hh].astype(BF16), doj[hh])
        for j in range(TB):
            rows = slice(j * ATT_BLK, (j + 1) * ATT_BLK)
            for hh in H:
                dk_ref[rows, ls[hh]] = dk[hh][j].astype(dk_ref.dtype)
                dv_ref[rows, ls[hh]] = dv[hh][j].astype(dv_ref.dtype)

    blk = pl.BlockSpec((TB * ATT_BLK, Wd), lambda b, r, n: (b * nsteps + n, r))
    prev = pl.BlockSpec((ATT_BLK, Wd), lambda b, r, n: (jnp.maximum(b * nb + n * TB - 1, 0), r))
    nxt = pl.BlockSpec((ATT_BLK, Wd), lambda b, r, n: (b * nb + jnp.minimum((n + 1) * TB, nb - 1), r))
    qv, kv, vv, dov, mgv, lv = (_att_view(t, dil) for t in (q, k, v, do, mg, lse))
    res = pl.pallas_call(
        body, grid=(bsz, dil, nsteps), in_specs=[blk] * 6 + [prev, prev, nxt, nxt, nxt, nxt],
        out_specs=[blk, blk, blk], out_shape=[jax.ShapeDtypeStruct(qv.shape, BF16)] * 3,
        name=name, compiler_params=_cparams(("arbitrary", "arbitrary", "arbitrary")),
    )(qv, kv, vv, dov, mgv, lv, kv, vv, qv, dov, mgv, lv)
    return [t.reshape(T, Wd) for t in res]


QKV_SLOTS = 3 * len(ATT_CONFIGS) * ATT_HEADS
SLOTS_PER_DEV = QKV_SLOTS // N_DEV


def qkv_matmul(name, a, b):
    M, K = a.shape
    nblk, _, n = b.shape
    nout = QKV_SLOTS // ATT_HEADS
    tm = _rows_for(M, K * 2 + nout * ATT_W * 2)

    def body(a_ref, b_ref, *outs):
        av = a_ref[...]
        for c in range(nblk):
            acc = _dot(av, b_ref[c]).astype(BF16)
            for r in range(SLOTS_PER_DEV):
                k, hh = divmod(c * SLOTS_PER_DEV + r, ATT_HEADS)
                outs[k][:, hh * HEAD_DIM:(hh + 1) * HEAD_DIM] = acc[:, r * HEAD_DIM:(r + 1) * HEAD_DIM]

    return pl.pallas_call(
        body, grid=(M // tm,),
        in_specs=[pl.BlockSpec((tm, K), lambda i: (i, 0)), pl.BlockSpec(b.shape, lambda i: (0, 0, 0))],
        out_specs=[pl.BlockSpec((tm, ATT_W), lambda i: (i, 0))] * nout,
        out_shape=[jax.ShapeDtypeStruct((M, ATT_W), BF16)] * nout,
        name=name, compiler_params=_cparams(("arbitrary",)))(a, b)


def _coords():
    return lax.axis_index("x"), lax.axis_index("y"), lax.axis_index("c")


def all_gather(name, xs):
    n = len(xs)

    def body(*refs):
        start, forward, finish = _gather_phases(refs[:n], refs[n:2 * n], *refs[2 * n:])
        start()
        forward()
        finish()

    anyspec = pl.BlockSpec(memory_space=pl.ANY)
    return pl.pallas_call(
        body, out_shape=_gather_out_shapes(xs), in_specs=[anyspec] * n, out_specs=[anyspec] * n,
        scratch_shapes=_gather_sems(n), name=name,
    )(*xs)


def _gather_out_shapes(xs):
    return [jax.ShapeDtypeStruct((N_DEV,) + t.shape, t.dtype) for t in xs]


def _gather_sems(n):
    return [pltpu.SemaphoreType.DMA((7 * n,)), pltpu.SemaphoreType.DMA((7 * n,)), pltpu.SemaphoreType.DMA((n,))]


def _gather_phases(x_refs, out_refs, send_sems, recv_sems, local_sems):
    n = len(x_refs)

    def parts():
        x, y, c = _coords()
        return (x, y, c), (x, y, 1 - c), [(1 - x, y), (x, 1 - y), (1 - x, 1 - y)], c

    def slot(a, px, py, pc):
        return out_refs[a].at[4 * px + 2 * py + pc]

    def copy(a, k, block, to, src=None):
        return pltpu.make_async_remote_copy(
            src_ref=slot(a, *block) if src is None else src, dst_ref=slot(a, *block),
            send_sem=send_sems.at[7 * a + k], recv_sem=recv_sems.at[7 * a + k],
            device_id=to, device_id_type=MESH)

    def mine(a, me):
        return pltpu.make_async_copy(x_refs[a], slot(a, *me), local_sems.at[a])

    def first(a, me, sibling, chips, c):
        return ([copy(a, 0, me, sibling, src=x_refs[a])]
                + [copy(a, 1 + j, me, (*chip, c), src=x_refs[a]) for j, chip in enumerate(chips)])

    def start():
        me, sibling, chips, c = parts()
        for a in range(n):
            mine(a, me).start()
        for a in range(n):
            for cp in first(a, me, sibling, chips, c):
                cp.start()

    def forward():
        me, sibling, chips, c = parts()
        for j, chip in enumerate(chips):
            for a in range(n):
                copy(a, 1 + j, (*chip, c), me).wait_recv()
                copy(a, 4 + j, (*chip, c), sibling).start()

    def finish():
        me, sibling, chips, c = parts()
        for a in range(n):
            copy(a, 0, sibling, me).wait_recv()
            for j, chip in enumerate(chips):
                copy(a, 4 + j, (*chip, 1 - c), me).wait_recv()
        for a in range(n):
            for cp in first(a, me, sibling, chips, c):
                cp.wait_send()
            for j, chip in enumerate(chips):
                copy(a, 4 + j, (*chip, c), sibling).wait_send()
            mine(a, me).wait()

    return start, forward, finish


def exchange_sibling(name, gs):
    n = len(gs)

    def body(*refs):
        g_refs, out_refs = refs[:n], refs[n:2 * n]
        send_sems, recv_sems = refs[2 * n:]
        x, y, c = _coords()
        sibling = (x, y, 1 - c)
        cps = []
        for a in range(n):
            for q in range(4):
                cps.append(pltpu.make_async_remote_copy(
                    src_ref=g_refs[a].at[2 * q + (1 - c)], dst_ref=out_refs[a].at[q],
                    send_sem=send_sems.at[4 * a + q], recv_sem=recv_sems.at[4 * a + q],
                    device_id=sibling, device_id_type=MESH))
        for cp in cps:
            cp.start()
        for cp in cps:
            cp.wait_recv()
        for cp in cps:
            cp.wait_send()

    anyspec = pl.BlockSpec(memory_space=pl.ANY)
    return pl.pallas_call(
        body, out_shape=[jax.ShapeDtypeStruct((4,) + g.shape[1:], g.dtype) for g in gs],
        in_specs=[anyspec] * n, out_specs=[anyspec] * n,
        scratch_shapes=[pltpu.SemaphoreType.DMA((4 * n,)), pltpu.SemaphoreType.DMA((4 * n,))],
        name=name,
    )(*gs)


def exchange_chips(name, ps):
    n = len(ps)

    def body(*refs):
        start, finish = _chips_phases(refs[:n], refs[n:2 * n], *refs[2 * n:])
        start()
        finish()

    anyspec = pl.BlockSpec(memory_space=pl.ANY)
    return pl.pallas_call(
        body, out_shape=_chips_out_shapes(ps), in_specs=[anyspec] * n, out_specs=[anyspec] * n,
        scratch_shapes=_chips_sems(n), name=name,
    )(*ps)


def _chips_out_shapes(ps):
    return [jax.ShapeDtypeStruct((3,) + p.shape[1:], p.dtype) for p in ps]


def _chips_sems(n):
    return [pltpu.SemaphoreType.DMA((3 * n,)), pltpu.SemaphoreType.DMA((3 * n,))]


def _chips_phases(p_refs, out_refs, send_sems, recv_sems):
    n = len(p_refs)

    def copies():
        x, y, c = _coords()
        chips = [(1 - x, y), (x, 1 - y), (1 - x, 1 - y)]
        return [pltpu.make_async_remote_copy(
            src_ref=p_refs[a].at[2 * px + py], dst_ref=out_refs[a].at[k],
            send_sem=send_sems.at[3 * a + k], recv_sem=recv_sems.at[3 * a + k],
            device_id=(px, py, c), device_id_type=MESH)
            for a in range(n) for k, (px, py) in enumerate(chips)]

    def start():
        for cp in copies():
            cp.start()

    def finish():
        cps = copies()
        for cp in cps:
            cp.wait_recv()
        for cp in cps:
            cp.wait_send()

    return start, finish


def _row_tile(R):
    tr = 256
    while R % tr:
        tr //= 2
    assert tr % 8 == 0
    return tr


def add_sibling(name, g, recv, c_idx):
    _, R, C = g.shape
    tr = _row_tile(R)

    def body(c_ref, g_ref, r_ref, o_ref, o16_ref):
        s = g_ref[...] + r_ref[...].astype(F32)
        o_ref[...] = s
        o16_ref[...] = s.astype(BF16)

    out = pl.BlockSpec((None, tr, C), lambda q, i, cr: (q, i, 0))
    return pl.pallas_call(
        body,
        grid_spec=pltpu.PrefetchScalarGridSpec(
            num_scalar_prefetch=1, grid=(4, R // tr),
            in_specs=[pl.BlockSpec((None, tr, C), lambda q, i, cr: (2 * q + cr[0], i, 0)), out],
            out_specs=[out, out]),
        out_shape=[jax.ShapeDtypeStruct((4, R, C), F32), jax.ShapeDtypeStruct((4, R, C), BF16)], name=name,
        compiler_params=_cparams(("arbitrary", "arbitrary")),
    )(c_idx, g, recv)


def _adam_math(w, g, m, v):
    m = ADAM_B1 * m + (1.0 - ADAM_B1) * g
    v = ADAM_B2 * v + (1.0 - ADAM_B2) * jnp.square(g)
    m_hat = m / (1.0 - ADAM_B1 ** ADAM_STEP)
    v_hat = v / (1.0 - ADAM_B2 ** ADAM_STEP)
    delta = -ADAM_LR * (m_hat / (jnp.sqrt(v_hat) + ADAM_EPS) + ADAM_WD * w)
    return delta, m, v


def adam_big(name, p1, recv, w, m, v, chip_idx, layer=0):
    _, R, C = p1.shape
    tr = _row_tile(R)
    nt = R // tr

    def body(q_ref, p_ref, r_ref, w_ref, m_ref, v_ref, g_ref, d_ref, nm_ref, nv_ref):
        g = ((p_ref[...] + r_ref[0].astype(F32)) + r_ref[1].astype(F32)) + r_ref[2].astype(F32)
        d, nm, nv = _adam_math(w_ref[...], g, m_ref[...], v_ref[...])
        g_ref[...] = g
        d_ref[...] = d
        nm_ref[...] = nm
        nv_ref[...] = nv

    row_in = pl.BlockSpec((tr, C), lambda i, qr: (layer * nt + i, 0))
    row = pl.BlockSpec((tr, C), lambda i, qr: (i, 0))
    return pl.pallas_call(
        body,
        grid_spec=pltpu.PrefetchScalarGridSpec(
            num_scalar_prefetch=1, grid=(nt,),
            in_specs=[pl.BlockSpec((None, tr, C), lambda i, qr: (qr[0], i, 0)),
                      pl.BlockSpec((3, tr, C), lambda i, qr: (0, i, 0)), row_in, row_in, row_in],
            out_specs=[row, row, row, row]),
        out_shape=[jax.ShapeDtypeStruct((R, C), F32)] * 4, name=name,
        compiler_params=_cparams(("arbitrary",)),
    )(chip_idx, p1, recv, w, m, v)


def sum8(parts):
    _, R, C = parts.shape

    def body(p_ref, o_ref):
        acc = p_ref[0]
        for k in range(1, N_DEV):
            acc = acc + p_ref[k]
        o_ref[...] = acc

    tr = 128
    while R % tr:
        tr //= 2
    assert tr % 8 == 0
    return pl.pallas_call(
        body, grid=(R // tr,), in_specs=[pl.BlockSpec((N_DEV, tr, C), lambda i: (0, i, 0))],
        out_specs=pl.BlockSpec((tr, C), lambda i: (i, 0)), out_shape=jax.ShapeDtypeStruct((R, C), F32),
        name="sum8", compiler_params=_cparams(("arbitrary",)),
    )(parts)


def adam_small(w, g, m, v):
    def fn(wt, gt, mt, vt):
        return _adam_math(wt, gt, mt, vt)
    C = w.shape[1]
    return rowwise("adam_small", fn, [w, g, m, v], [], [(C, F32)] * 3, tr=128)


def _pack(arrs, rows_mult=8):
    flat = jnp.concatenate([a.reshape(-1) for a in arrs])
    n = flat.shape[0]
    per = PACK_C * rows_mult
    pad = (-n) % per
    if pad:
        flat = jnp.concatenate([flat, jnp.zeros((pad,), flat.dtype)])
    return flat.reshape(-1, PACK_C)


def _unpack(buf, shapes):
    flat = buf.reshape(-1)
    out, off = [], 0
    for s in shapes:
        n = math.prod(s)
        out.append(flat[off:off + n].reshape(s))
        off += n
    return out


def _blocked(gfull, axis):
    shp = gfull.shape
    n = shp[axis] // N_DEV
    t = gfull.reshape(shp[:axis] + (N_DEV, n) + shp[axis + 1:])
    t = jnp.moveaxis(t, axis, 0)
    return t.reshape(N_DEV, -1)


def _unblocked(gathered, shard_shape, axis):
    t = jnp.moveaxis(gathered, 0, axis)
    shp = shard_shape[:axis] + (N_DEV * shard_shape[axis],) + shard_shape[axis + 1:]
    return t.reshape(shp)


def _relu2_epi(acc):
    r = jnp.maximum(acc, 0.0)
    return acc, r * r


def _step(x3, target3, W, comm):
    bsz, S, D = x3.shape
    T = bsz * S
    x = x3.reshape(T, D)
    target = target3.reshape(T, D)
    row = lambda v: v.reshape(1, -1)
    grads = {}

    s5p = (W['ssm_a_re'][0], W['ssm_a_im'][0], W['ssm_log_dt'][0], W['ssm_b_re'][0], W['ssm_b_im'][0])
    (lam_re, lam_im, bb_re, bb_im), s5_disc_vjp = jax.vjp(s5_disc, *s5p)
    pwr, pwi, l2r, l2i = s5_tables(lam_re, lam_im, S5_SUB)
    bre, bim = _s5_blockdiag_b(bb_re).astype(BF16), _s5_blockdiag_b(bb_im).astype(BF16)
    cre, cim = _s5_blockdiag_c(W['ssm_c_re'][0]).astype(BF16), _s5_blockdiag_c(W['ssm_c_im'][0]).astype(BF16)
    dskip = W['ssm_d']

    tril = jnp.tril(jnp.ones((GMLP_CHUNK, GMLP_CHUNK), bool))
    ws = jnp.where(tril[None], W['gmlp_w_s'][0], 0.0).astype(BF16)
    hw = D // GMLP_HEADS
    sbias = jnp.repeat(W['gmlp_b_s'][0].T, hw, axis=1)

    saved = []
    def add_norm(acc, *ex):
        xn = acc + ex[0] + ex[1] if len(ex) == 3 else acc + ex[0]
        return xn, _rms(xn, ex[-1])

    for i in range(DEPTH):
        sv = {'x': x}
        nm = W['norm_mix'][i:i + 1]
        nl = W['norm_mlp'][i:i + 1]
        if i == 0:
            h, hf = rms_fwd("rms_mix0", x, nm, want_f32=True)
            res = s5_fwd(hf, bre, bim, cre, cim, pwr, pwi, l2r, l2i, dskip, bsz, gather=comm.gather_list)
            ypre, gy, xs, xr_all, xi_all = res[:5]
            Wfull, Wsh = comm.weights(res[5:])
            W = {**W, **Wsh}
            conv_w = jnp.concatenate([W['conv_w_dw'][0], jnp.zeros((1, D), F32)], axis=0)
            def s5_out(acc, xt, g):
                xn = xt + _glu(acc)
                return acc, xn, _rms(xn, g)
            z, x1, h2 = matmul("s5_glu_mm", gy, Wfull['ssm_w_glu'], mode='cb', whole_rows=True, epi=s5_out,
                               extras=[(x, 'tile'), (nl, 'row')], out_dtypes=(F32, F32, BF16), out_cols=(2 * D, D, D))
            sv.update(hf=hf, ypre=ypre, gy=gy, xs=xs, xr=xr_all, xi=xi_all, z=z)
        elif i == 1:
            def pw1_out(acc, b):
                zt = acc + b
                return zt, _glu(zt)
            z, zg = matmul("conv_pw1", h, Wfull['conv_w_pw1'], mode='cb', whole_rows=True, epi=pw1_out,
                           extras=[(W['conv_b_pw1'], 'row')], out_dtypes=(F32, F32), out_cols=(2 * D, D))
            yc = dwconv_fwd(zg, conv_w, W['conv_b_dw'], bsz)
            y2, = rowwise("conv_ln_silu", lambda t, g, b: jax.nn.silu(_ln(t, g, b)), [yc],
                          [W['conv_ln_g'], W['conv_ln_b']], [(D, BF16)])
            x1, h2 = matmul("conv_pw2", y2, Wfull['conv_w_pw2'], epi=add_norm, whole_rows=True,
                            extras=[(W['conv_b_pw2'], 'row'), (x, 'tile'), (nl, 'row')], out_dtypes=(F32, BF16))
            sv.update(h=h, z=z, zg=zg, yc=yc, y2=y2)
        elif i == 2:
            def gm_pre(acc, g, b):
                act = jax.nn.gelu(acc)
                return acc, act[:, :D], _ln(act[:, D:], g, b)
            zp, u, vln = matmul("gmlp_in", h, Wfull['gmlp_w_in'], mode='cb', whole_rows=True, epi=gm_pre,
                                extras=[(W['gmlp_ln_g'], 'row'), (W['gmlp_ln_b'], 'row')],
                                out_dtypes=(F32, F32, BF16), out_cols=(2 * D, D, D))
            gated = spatial_fwd(u, vln, ws, sbias)
            x1, h2 = matmul("gmlp_out", gated, Wfull['gmlp_w_out'], epi=add_norm, whole_rows=True,
                            extras=[(x, 'tile'), (nl, 'row')], out_dtypes=(F32, BF16))
            sv.update(h=h, zp=zp, u=u, vln=vln, gated=gated)
        else:
            qkv = qkv_matmul("attn_qkv", h, Wfull['attn_w_qkv'])
            ng = len(ATT_CONFIGS)
            outs, lses, blocks = [], [], []
            for gi, (window, dil) in enumerate(ATT_CONFIGS):
                qb, kb, vb = (qkv[j * ng + gi] for j in range(3))
                ob, lb = attn_fwd("attn_fwd%d" % gi, qb, kb, vb, dil, bsz)
                blocks.append((qb, kb, vb, lb, dil))
                outs.append(ob)
                lses.append(lb)

            def merge(o0, o1, o2, l0, l1, l2):
                m = jnp.maximum(jnp.maximum(l0, l1), l2)
                e0, e1, e2 = jnp.exp(l0 - m), jnp.exp(l1 - m), jnp.exp(l2 - m)
                inv = 1.0 / (e0 + e1 + e2)
                w0, w1, w2 = e0 * inv, e1 * inv, e2 * inv
                return w0 * o0 + w1 * o1 + w2 * o2, w0, w1, w2
            merged, w0, w1, w2 = rowwise("attn_merge", merge, outs + lses, [],
                                         [(ATT_W, BF16), (ATT_W, F32), (ATT_W, F32), (ATT_W, F32)])
            wo = Wfull['attn_w_o']
            wo_nat = wo.transpose(1, 0, 2).reshape(wo.shape[1], N_DEV * wo.shape[2])
            x1, h2 = matmul("attn_o", merged, wo_nat, epi=add_norm, whole_rows=True,
                            extras=[(x, 'tile'), (nl, 'row')], out_dtypes=(F32, BF16))
            sv.update(h=h, blocks=blocks, merged=merged, wts=(w0, w1, w2))
        a, act = matmul("mlp_in%d" % i, h2, Wfull['mlp_w_in'][i], mode='cb', epi=_relu2_epi, out_dtypes=(BF16, BF16))
        if i + 1 < DEPTH:
            x2, h = matmul("mlp_out%d" % i, act, Wfull['mlp_w_out'][i], epi=add_norm, whole_rows=True,
                           extras=[(x1, 'tile'), (W['norm_mix'][i + 1:i + 2], 'row')], out_dtypes=(F32, BF16))
        else:
            x2, = matmul("mlp_out%d" % i, act, Wfull['mlp_w_out'][i], epi=lambda acc, r: (acc + r,),
                         extras=[(x1, 'tile')])
        sv.update(x1=x1, h2=h2, a=a, act=act)
        saved.append(sv)
        x = x2

    def loss_fn(xt, tt, g):
        y, vjp = jax.vjp(_rms, xt, g)
        err = y - tt
        dxx, dg = vjp(err * (1.0 / D))
        lval = jnp.sum(jnp.sum(err * err, axis=1, keepdims=True), axis=0, keepdims=True) * (0.5 / D)
        return dxx, dxx, jnp.broadcast_to(lval, (1, 128)), dg
    dx, dxb, lacc, dnf = rowwise("loss_head", loss_fn, [x, target], [row(W['norm_final'])],
                                 [(D, F32), (D, BF16)], [((1, 128), F32), ((1, D), F32)])
    loss_local = lacc[0, 0]
    grads['norm_final'] = dnf.reshape(-1)

    g_norm_mix, g_norm_mlp = [None] * DEPTH, [None] * DEPTH
    g_mlp_in, g_mlp_out = [None] * DEPTH, [None] * DEPTH
    nl_all = [W['norm_mlp'][i:i + 1] for i in range(DEPTH)]
    nm_all = [W['norm_mix'][i:i + 1] for i in range(DEPTH)]

    def norm_bwd(xt, dres, g):
        def epi(dh, xv, dr, gv):
            _, vjp = jax.vjp(_rms, xv, gv)
            dxv, dgv = vjp(dh)
            dxv = dxv + dr
            return dxv, dxv, dgv
        return dict(epi=epi, extras=[xt, dres], params=[g], out_dtypes=(F32, BF16), acc_out=[((1, D), F32)])

    for i in reversed(range(DEPTH)):
        sv = saved[i]
        da, = matmul("mlp_out_bwd%d" % i, dxb, Wfull['mlp_w_out'][i], mode='nt',
                     epi=lambda acc, av: (acc * (2.0 * jnp.maximum(av.astype(F32), 0.0)),),
                     extras=[(sv['a'], 'tile')], out_dtypes=(BF16,))
        g_mlp_out[i] = _rows_blocked(wgrad("mlp_out_wg%d" % i, sv['act'], dxb))
        dx, dxb, dg = matmul_nt_cb("mlp_in_bwd%d" % i, da, Wfull['mlp_w_in'][i], **norm_bwd(sv['x1'], dx, nl_all[i]))
        g_mlp_in[i] = wgrad("mlp_in_wg%d" % i, sv['h2'], da, cb=True)
        g_norm_mlp[i] = dg.reshape(-1)
        xin = sv['x']
        if i == 0:
            dz, = rowwise("s5_glu_bwd", _glu_bwd, [sv['z'], dx], [], [(2 * D, BF16)])
            dgy, = matmul_nt_cb("s5_glu_mm_bwd", dz, Wfull['ssm_w_glu'])
            grads['ssm_w_glu'] = wgrad("s5_glu_wg", sv['gy'], dz, cb=True)
            grads['mlp_w_in'], grads['mlp_w_out'] = g_mlp_in, g_mlp_out
            res = s5_bwd(sv['hf'], sv['ypre'], dgy, sv['xs'], sv['xr'], sv['xi'], bre, bim, cre, cim,
                         pwr[:, ::-1], pwi[:, ::-1], l2r, l2i, dskip, bsz, chips=comm.rs_front(grads))
            du, dbr, dbi, dcr, dci, dl, dd = res[:7]
            comm.recv2 = res[7:]
            dlam_re = dl[:, 0, :].reshape(SSM_GROUPS, SSM_STATE)
            dlam_im = dl[:, 1, :].reshape(SSM_GROUPS, SSM_STATE)
            s5_cot = (dlam_re, dlam_im, _s5_blockdiag_b_inv(dbr), _s5_blockdiag_b_inv(dbi))
            grads['ssm_c_re'] = _s5_blockdiag_c_inv(dcr)[None]
            grads['ssm_c_im'] = _s5_blockdiag_c_inv(dci)[None]
            grads['ssm_d'] = dd[0:1]
            dx, dxb, dg = rms_bwd("rms_mix_bwd0", xin, du, dx, nm_all[0])
        elif i == 1:
            dy2, = matmul("conv_pw2_bwd", dxb, Wfull['conv_w_pw2'], mode='nt')
            grads['conv_w_pw2'] = _rows_blocked(wgrad("conv_pw2_wg", sv['y2'], dxb))

            def ln_silu_bwd(yt, dt, dxt, g, b):
                _, vjp = jax.vjp(lambda t, gg, bb: jax.nn.silu(_ln(t, gg, bb)), yt, g, b)
                dyc, dgg, dbb = vjp(dt)
                return dyc, dgg, dbb, _colsum(dxt)
            dyc, dlg, dlb, dbp2 = rowwise("conv_ln_silu_bwd", ln_silu_bwd, [sv['yc'], dy2, dx],
                                          [W['conv_ln_g'], W['conv_ln_b']], [(D, F32)],
                                          [((1, D), F32), ((1, D), F32), ((1, D), F32)])
            grads['conv_ln_g'], grads['conv_ln_b'], grads['conv_b_pw2'] = dlg, dlb, dbp2
            dzg, dwd, dbd = dwconv_bwd(sv['zg'], dyc, conv_w, bsz)
            grads['conv_w_dw'] = dwd[None, :CONV_WIDTH]
            grads['conv_b_dw'] = dbd

            def glu_bwd1(zt, dyt):
                dzt = _glu_bwd(zt, dyt)
                return dzt, _colsum(dzt)
            dz, dbp1 = rowwise("conv_glu_bwd", glu_bwd1, [sv['z'], dzg], [], [(2 * D, BF16)], [((1, 2 * D), F32)])
            grads['conv_b_pw1'] = dbp1
            dx, dxb, dg = matmul_nt_cb("conv_pw1_bwd", dz, Wfull['conv_w_pw1'], **norm_bwd(xin, dx, nm_all[i]))
            grads['conv_w_pw1'] = wgrad("conv_pw1_wg", sv['h'], dz, cb=True)
        elif i == 2:
            dgt, = matmul("gmlp_out_bwd", dxb, Wfull['gmlp_w_out'], mode='nt', out_dtypes=(BF16,))
            grads['gmlp_w_out'] = _rows_blocked(wgrad("gmlp_out_wg", sv['gated'], dxb))
            du, dvln, dws, dsb = spatial_bwd(sv['u'], sv['vln'], dgt, ws, sbias)
            grads['gmlp_w_s'] = dws[None]
            grads['gmlp_b_s'] = dsb.reshape(GMLP_CHUNK, GMLP_HEADS, hw).sum(-1).T[None]

            def gm_pre_bwd(zt, dut, dvt, g, b):
                _, vjp_u = jax.vjp(jax.nn.gelu, zt[:, :D])
                _, vjp_v = jax.vjp(lambda zz, gg, bb: _ln(jax.nn.gelu(zz), gg, bb), zt[:, D:], g, b)
                dz2, dgg, dbb = vjp_v(dvt)
                return jnp.concatenate([vjp_u(dut)[0], dz2], axis=1), dgg, dbb
            dzp, dlg, dlb = rowwise("gmlp_pre_bwd", gm_pre_bwd, [sv['zp'], du, dvln],
                                    [W['gmlp_ln_g'], W['gmlp_ln_b']], [(2 * D, BF16)], [((1, D), F32), ((1, D), F32)])
            grads['gmlp_ln_g'], grads['gmlp_ln_b'] = dlg, dlb
            dx, dxb, dg = matmul_nt_cb("gmlp_in_bwd", dzp, Wfull['gmlp_w_in'], **norm_bwd(xin, dx, nm_all[i]))
            grads['gmlp_w_in'] = wgrad("gmlp_in_wg", sv['h'], dzp, cb=True)
        else:
            dm, = matmul_nt_cb("attn_o_bwd", dxb, Wfull['attn_w_o'])
            grads['attn_w_o'] = wgrad("attn_o_wg", sv['merged'], dxb, cb=True)
            w0, w1, w2 = sv['wts']
            do0, do1, do2 = rowwise("attn_merge_bwd", lambda d, a, b, c: (a * d, b * d, c * d), [dm, w0, w1, w2], [],
                                    [(ATT_W, BF16)] * 3)
            dparts = [[None] * 3 for _ in range(3)]
            for gi, (dog, (qb, kb, vb, lb, dil)) in enumerate(zip((do0, do1, do2), sv['blocks'])):
                dqb, dkb, dvb = attn_bwd("attn_bwd%d" % gi, qb, kb, vb, dog, sv['merged'], lb, dil, bsz)
                for j, t in enumerate((dqb, dkb, dvb)):
                    dparts[j][gi] = t
            dqkv = [dparts[j][gi] for j in range(3) for gi in range(3)]
            dx, dxb, dg = matmul_nt_cb("attn_qkv_bwd", dqkv, Wfull['attn_w_qkv'], heads=True,
                                       **norm_bwd(xin, dx, nm_all[i]))
            grads['attn_w_qkv'] = wgrad("attn_qkv_wg", sv['h'], dqkv, cb=True, heads=True)
        g_norm_mix[i] = dg.reshape(-1)

    grads['norm_mix'] = jnp.stack(g_norm_mix)
    grads['norm_mlp'] = jnp.stack(g_norm_mlp)
    grads['mlp_w_in'] = g_mlp_in
    grads['mlp_w_out'] = g_mlp_out
    return loss_local, dx.reshape(bsz, S, D), grads, (s5_disc_vjp, s5_cot)


class _StepComm:
    def __init__(self, Wl, c_idx):
        self.Wl, self.c_idx = Wl, c_idx
        self.units = []
        for n in BIG:
            self.units += [(n, i) for i in range(DEPTH)] if Wl[n].shape[0] == DEPTH else [(n, None)]
        self.ss_names = list(SMALL_SHARDED)
        spack = _pack([Wl[n] for n in self.ss_names])
        self.gather_list = [Wl[n][0 if i is None else i].astype(BF16) for n, i in self.units] + [spack]
        self.p1 = self.recv2 = None

    @staticmethod
    def tag(n, i):
        return n if i is None else "%s%d" % (n, i)

    def weights(self, gathered):
        Wl = self.Wl
        Wfull = {}
        for (n, i), g in zip(self.units, gathered):
            w = g if BIG[n] == 2 else g.reshape(N_DEV * g.shape[1], g.shape[2])
            if i is None:
                Wfull[n] = w
            else:
                Wfull.setdefault(n, []).append(w)
        sparts = _unpack_gathered(gathered[-1], [Wl[n].shape for n in self.ss_names])
        Wsh = {n: _unblocked(p, Wl[n].shape, SMALL_SHARDED[n]) for n, p in zip(self.ss_names, sparts)}
        return Wfull, Wsh

    def rs_front(self, grads):
        pairs = [grads[n] if i is None else grads[n][i] for n, i in self.units]
        recv1 = exchange_sibling("rs_sibling", [p[1] for p in pairs])
        self.p1 = [add_sibling("add_sibling_" + self.tag(n, i), p[0], r, self.c_idx)
                   for (n, i), p, r in zip(self.units, pairs, recv1)]
        return [p[1] for p in self.p1]


def _rows_blocked(pair):
    return tuple(t.reshape(N_DEV, t.shape[0] // N_DEV, t.shape[1]) for t in pair)


def kernel(x, norm_mix, norm_mlp, norm_final, ssm_a_re, ssm_a_im, ssm_b_re, ssm_b_im, ssm_c_re, ssm_c_im, ssm_d, ssm_log_dt, ssm_w_glu, conv_w_pw1, conv_b_pw1, conv_w_dw, conv_b_dw, conv_ln_g, conv_ln_b, conv_w_pw2, conv_b_pw2, gmlp_w_in, gmlp_ln_g, gmlp_ln_b, gmlp_w_s, gmlp_b_s, gmlp_w_out, attn_w_qkv, attn_w_o, mlp_w_in, mlp_w_out, loss_target, m_norm_mix, m_norm_mlp, m_norm_final, m_ssm_a_re, m_ssm_a_im, m_ssm_b_re, m_ssm_b_im, m_ssm_c_re, m_ssm_c_im, m_ssm_d, m_ssm_log_dt, m_ssm_w_glu, m_conv_w_pw1, m_conv_b_pw1, m_conv_w_dw, m_conv_b_dw, m_conv_ln_g, m_conv_ln_b, m_conv_w_pw2, m_conv_b_pw2, m_gmlp_w_in, m_gmlp_ln_g, m_gmlp_ln_b, m_gmlp_w_s, m_gmlp_b_s, m_gmlp_w_out, m_attn_w_qkv, m_attn_w_o, m_mlp_w_in, m_mlp_w_out, v_norm_mix, v_norm_mlp, v_norm_final, v_ssm_a_re, v_ssm_a_im, v_ssm_b_re, v_ssm_b_im, v_ssm_c_re, v_ssm_c_im, v_ssm_d, v_ssm_log_dt, v_ssm_w_glu, v_conv_w_pw1, v_conv_b_pw1, v_conv_w_dw, v_conv_b_dw, v_conv_ln_g, v_conv_ln_b, v_conv_w_pw2, v_conv_b_pw2, v_gmlp_w_in, v_gmlp_ln_g, v_gmlp_ln_b, v_gmlp_w_s, v_gmlp_b_s, v_gmlp_w_out, v_attn_w_qkv, v_attn_w_o, v_mlp_w_in, v_mlp_w_out):
    args = (norm_mix, norm_mlp, norm_final, ssm_a_re, ssm_a_im, ssm_b_re, ssm_b_im, ssm_c_re, ssm_c_im, ssm_d,
            ssm_log_dt, ssm_w_glu, conv_w_pw1, conv_b_pw1, conv_w_dw, conv_b_dw, conv_ln_g, conv_ln_b, conv_w_pw2,
            conv_b_pw2, gmlp_w_in, gmlp_ln_g, gmlp_ln_b, gmlp_w_s, gmlp_b_s, gmlp_w_out, attn_w_qkv, attn_w_o,
            mlp_w_in, mlp_w_out)
    margs = (m_norm_mix, m_norm_mlp, m_norm_final, m_ssm_a_re, m_ssm_a_im, m_ssm_b_re, m_ssm_b_im, m_ssm_c_re,
             m_ssm_c_im, m_ssm_d, m_ssm_log_dt, m_ssm_w_glu, m_conv_w_pw1, m_conv_b_pw1, m_conv_w_dw, m_conv_b_dw,
             m_conv_ln_g, m_conv_ln_b, m_conv_w_pw2, m_conv_b_pw2, m_gmlp_w_in, m_gmlp_ln_g, m_gmlp_ln_b,
             m_gmlp_w_s, m_gmlp_b_s, m_gmlp_w_out, m_attn_w_qkv, m_attn_w_o, m_mlp_w_in, m_mlp_w_out)
    vargs = (v_norm_mix, v_norm_mlp, v_norm_final, v_ssm_a_re, v_ssm_a_im, v_ssm_b_re, v_ssm_b_im, v_ssm_c_re,
             v_ssm_c_im, v_ssm_d, v_ssm_log_dt, v_ssm_w_glu, v_conv_w_pw1, v_conv_b_pw1, v_conv_w_dw, v_conv_b_dw,
             v_conv_ln_g, v_conv_ln_b, v_conv_w_pw2, v_conv_b_pw2, v_gmlp_w_in, v_gmlp_ln_g, v_gmlp_ln_b,
             v_gmlp_w_s, v_gmlp_b_s, v_gmlp_w_out, v_attn_w_qkv, v_attn_w_o, v_mlp_w_in, v_mlp_w_out)
    Wl = dict(zip(WEIGHT_NAMES, args))
    Ml = dict(zip(WEIGHT_NAMES, margs))
    Vl = dict(zip(WEIGHT_NAMES, vargs))
    cx, cy, cc = _coords()
    my_idx = 4 * cx + 2 * cy + cc

    c_idx = cc.reshape(1).astype(jnp.int32)
    chip_idx = (2 * cx + cy).reshape(1).astype(jnp.int32)
    comm = _StepComm(Wl, c_idx)
    units, tag = comm.units, comm.tag
    W = {n: Wl[n] for n in SMALL if n not in SMALL_SHARDED}
    loss_local, grad_x, grads, (s5_disc_vjp, s5_cot) = _step(x, loss_target, W, comm)
    loss = lax.psum(loss_local, MESH_AXES)

    outs4 = {}
    for (n, i), p, r in zip(units, comm.p1, comm.recv2):
        w2, m2, v2 = (d[n].reshape(-1, d[n].shape[-1]) for d in (Wl, Ml, Vl))
        res = adam_big("adam_" + tag(n, i), p[0], r, w2, m2, v2, chip_idx, layer=0 if i is None else i)
        if i is None:
            outs4[n] = [t.reshape(Wl[n].shape) for t in res]
        else:
            outs4.setdefault(n, []).append(res)
    for n in BIG:
        if Wl[n].shape[0] == DEPTH:
            outs4[n] = [jnp.stack([layer[k] for layer in outs4[n]]) for k in range(4)]
    out_g = {n: outs4[n][0] for n in BIG}
    out_d = {n: outs4[n][1] for n in BIG}
    out_m = {n: outs4[n][2] for n in BIG}
    out_v = {n: outs4[n][3] for n in BIG}

    s5_lin = ['ssm_a_re', 'ssm_a_im', 'ssm_log_dt', 'ssm_b_re', 'ssm_b_im']
    direct = [n for n in SMALL if n not in s5_lin]
    def full_shape(n):
        shp = list(Wl[n].shape)
        if n in SMALL_SHARDED:
            shp[SMALL_SHARDED[n]] *= N_DEV
        return tuple(shp)
    small_parts = [grads[n].reshape(full_shape(n)) for n in direct] + list(s5_cot)
    gsum = sum8(all_gather("gather_small_grads", [_pack(small_parts)])[0])
    summed = _unpack(gsum, [p.shape for p in small_parts])
    gsmall = dict(zip(direct, summed[:len(direct)]))
    s5g = s5_disc_vjp(tuple(summed[len(direct):]))
    for n, gval in zip(s5_lin, s5g):
        gsmall[n] = gval[None]
    for n, ax in SMALL_SHARDED.items():
        gsmall[n] = lax.dynamic_slice_in_dim(gsmall[n], my_idx * Wl[n].shape[ax], Wl[n].shape[ax], axis=ax)
    sm_shapes = [Wl[n].shape for n in SMALL]
    dS, mS, vS = adam_small(_pack([Wl[n] for n in SMALL]), _pack([gsmall[n] for n in SMALL]),
                            _pack([Ml[n] for n in SMALL]), _pack([Vl[n] for n in SMALL]))
    for n, gval in zip(SMALL, [gsmall[n] for n in SMALL]):
        out_g[n] = gval.reshape(Wl[n].shape)
    out_d.update(zip(SMALL, _unpack(dS, sm_shapes)))
    out_m.update(zip(SMALL, _unpack(mS, sm_shapes)))
    out_v.update(zip(SMALL, _unpack(vS, sm_shapes)))

    return (loss, grad_x, *[out_g[n] for n in WEIGHT_NAMES], *[out_d[n] for n in WEIGHT_NAMES],
            *[out_m[n] for n in WEIGHT_NAMES], *[out_v[n] for n in WEIGHT_NAMES])


def _unpack_gathered(g, shard_shapes):
    flat = g.reshape(N_DEV, -1)
    out, off = [], 0
    for s in shard_shapes:
        n = math.prod(s)
        out.append(flat[:, off:off + n].reshape((N_DEV,) + tuple(s)))
        off += n
    return out
```

```python
import functools
import math

import jax
import jax.numpy as jnp
from jax import lax
from jax.experimental import pallas as pl
from jax.experimental.pallas import tpu as pltpu

F32 = jnp.float32
BF16 = jnp.bfloat16

D_MODEL = 1024
DEPTH = 4
EPS = 1e-6
SSM_GROUP = 16
SSM_GROUPS = 64
SSM_STATE = 64
S5_GB = 8
S5_NGB = SSM_GROUPS // S5_GB
S5_CH = S5_GB * SSM_GROUP
S5_ST = S5_GB * SSM_STATE
S5_L = 256
CONV_WIDTH = 31
CONV_PAD = 32
CONV_TS = 256
CONV_CW = 256
GMLP_CHUNK = 128
GMLP_HEADS = 4
ATT_CONFIGS = ((128, 1), (512, 4), (2048, 16))
ATT_HEADS = 8
HEAD_DIM = 64
ATT_BLK = 128
ATT_TB = 2
ATT_TB_BWD = 4
ATT_W = ATT_HEADS * HEAD_DIM
N_DEV = 8
ADAM_LR = 0.001
ADAM_B1 = 0.9
ADAM_B2 = 0.999
ADAM_EPS = 1e-08
ADAM_WD = 0.01
ADAM_STEP = 10
VMEM_LIMIT = 56 * 1024 * 1024
PACK_C = 1024
MESH_AXES = ("x", "y", "c")
MESH = pl.DeviceIdType.MESH

WEIGHT_NAMES = ['norm_mix', 'norm_mlp', 'norm_final', 'ssm_a_re', 'ssm_a_im', 'ssm_b_re', 'ssm_b_im',
                'ssm_c_re', 'ssm_c_im', 'ssm_d', 'ssm_log_dt', 'ssm_w_glu', 'conv_w_pw1', 'conv_b_pw1',
                'conv_w_dw', 'conv_b_dw', 'conv_ln_g', 'conv_ln_b', 'conv_w_pw2', 'conv_b_pw2',
                'gmlp_w_in', 'gmlp_ln_g', 'gmlp_ln_b', 'gmlp_w_s', 'gmlp_b_s', 'gmlp_w_out',
                'attn_w_qkv', 'attn_w_o', 'mlp_w_in', 'mlp_w_out']
BIG = {'ssm_w_glu': 2, 'conv_w_pw1': 2, 'conv_w_pw2': 1, 'gmlp_w_in': 2, 'gmlp_w_out': 1,
       'attn_w_qkv': 2, 'attn_w_o': 2, 'mlp_w_in': 2, 'mlp_w_out': 1}
SMALL_SHARDED = {'conv_b_pw1': 1, 'conv_w_dw': 2, 'conv_b_dw': 1, 'conv_ln_g': 1, 'conv_ln_b': 1,
                 'conv_b_pw2': 1, 'gmlp_ln_g': 1, 'gmlp_ln_b': 1}
SMALL = [n for n in WEIGHT_NAMES if n not in BIG]


def _cparams(sem=None):
    return pltpu.CompilerParams(dimension_semantics=sem, vmem_limit_bytes=VMEM_LIMIT)


def _dot(a, b):
    return jnp.dot(a, b, preferred_element_type=F32)


def _dot_nt(a, b):
    return lax.dot_general(a, b, (((1,), (1,)), ((), ())), preferred_element_type=F32)


def _dot_tn(a, b):
    return lax.dot_general(a, b, (((0,), (0,)), ((), ())), preferred_element_type=F32)


ROW_TILE_BYTES = 10 << 20


def _rows_for(T, row_bytes, cap=1024):
    tr = min(cap, T)
    while tr > 8 and (T % tr or tr * row_bytes > ROW_TILE_BYTES):
        tr //= 2
    assert T % tr == 0 and tr % 8 == 0
    return tr


def rowwise(name, fn, rows, params, row_out, acc_out=(), tr=None):
    T = rows[0].shape[0]
    row_bytes = (sum(r.shape[1] * r.dtype.itemsize for r in rows)
                 + sum(c * jnp.dtype(dt).itemsize for c, dt in row_out))
    tr = _rows_for(T, row_bytes, cap=tr or 1024)
    nr, npar, nro = len(rows), len(params), len(row_out)

    def body(*refs):
        ins = [r[...] for r in refs[:nr + npar]]
        outs = refs[nr + npar:]
        res = fn(*ins)
        if not isinstance(res, (tuple, list)):
            res = (res,)
        for k in range(nro):
            outs[k][...] = res[k].astype(outs[k].dtype)
        if acc_out:
            @pl.when(pl.program_id(0) == 0)
            def _():
                for k in range(nro, len(outs)):
                    outs[k][...] = jnp.zeros_like(outs[k])
            for k in range(nro, len(outs)):
                outs[k][...] += res[k].astype(outs[k].dtype)

    in_specs = [pl.BlockSpec((tr, r.shape[1]), lambda i: (i, 0)) for r in rows]
    in_specs += [pl.BlockSpec(p.shape, lambda i, nd=p.ndim: (0,) * nd) for p in params]
    out_shape = [jax.ShapeDtypeStruct((T, c), dt) for c, dt in row_out]
    out_specs = [pl.BlockSpec((tr, c), lambda i: (i, 0)) for c, dt in row_out]
    out_shape += [jax.ShapeDtypeStruct(s, dt) for s, dt in acc_out]
    out_specs += [pl.BlockSpec(s, lambda i, nd=len(s): (0,) * nd) for s, dt in acc_out]
    res = pl.pallas_call(body, grid=(T // tr,), in_specs=in_specs, out_specs=out_specs, out_shape=out_shape,
                         name=name, compiler_params=_cparams(("arbitrary",)))(*rows, *params)
    return res


def _tile_m(M, K):
    tm = 2048
    while tm > 256 and tm * K * 2 > (4 << 20):
        tm //= 2
    return min(tm, M)


def matmul(name, a, b, *, mode='nn', epi=None, extras=(), out_dtypes=(F32,), out_cols=None, whole_rows=False):
    M, K = a.shape
    if mode == 'cb':
        nblk, _, tn = b.shape
        N = nblk * tn
    else:
        N = b.shape[0] if mode == 'nt' else b.shape[1]
        tn = min(512, N)
    out_cols = list(out_cols) if out_cols is not None else [N] * len(out_dtypes)
    row_bytes = (K * 2 + sum(c * jnp.dtype(dt).itemsize for c, dt in zip(out_cols, out_dtypes))
                 + sum(arr.shape[1] * arr.dtype.itemsize for arr, kind in extras if kind == 'tile'))
    tm = _rows_for(M, row_bytes)
    assert N % tn == 0, (M, N, tm, tn)
    nex = len(extras)

    def body(a_ref, b_ref, *rest):
        ex_refs, outs = rest[:nex], rest[nex:]
        av = a_ref[...]

        def product(c):
            cs = slice(c * tn, (c + 1) * tn)
            if mode == 'cb':
                return _dot(av, b_ref[c])
            return _dot_nt(av, b_ref[cs, :]) if mode == 'nt' else _dot(av, b_ref[:, cs])

        if whole_rows:
            parts = [product(c) for c in range(N // tn)]
            acc = parts[0] if len(parts) == 1 else jnp.concatenate(parts, axis=1)
            res = epi(acc, *[e[...] for e in ex_refs])
            for o, r in zip(outs, res):
                o[...] = r.astype(o.dtype)
        else:
            for c in range(N // tn):
                cs = slice(c * tn, (c + 1) * tn)
                acc = product(c)
                res = epi(acc, *[e[:, cs] for e in ex_refs]) if epi is not None else (acc,)
                for o, r in zip(outs, res):
                    o[:, cs] = r.astype(o.dtype)

    in_specs = [pl.BlockSpec((tm, K), lambda i: (i, 0)), pl.BlockSpec(b.shape, lambda i, nd=b.ndim: (0,) * nd)]
    for arr, kind in extras:
        in_specs.append(pl.BlockSpec((tm, arr.shape[1]), lambda i: (i, 0)) if kind == 'tile'
                        else pl.BlockSpec((1, arr.shape[1]), lambda i: (0, 0)))
    out_shape = [jax.ShapeDtypeStruct((M, c), dt) for c, dt in zip(out_cols, out_dtypes)]
    out_specs = [pl.BlockSpec((tm, c), lambda i: (i, 0)) for c in out_cols]
    return pl.pallas_call(body, grid=(M // tm,), in_specs=in_specs, out_specs=out_specs,
                          out_shape=out_shape, name=name,
                          compiler_params=_cparams(("arbitrary",)))(a, b, *[e[0] for e in extras])


def _gather_heads(refs, j, scr):
    for r in range(SLOTS_PER_DEV):
        k, hh = divmod(j * SLOTS_PER_DEV + r, ATT_HEADS)
        scr[:, r * HEAD_DIM:(r + 1) * HEAD_DIM] = refs[k][:, hh * HEAD_DIM:(hh + 1) * HEAD_DIM]
    return scr[...]


def matmul_nt_cb(name, a, b, *, heads=False, epi=None, extras=(), params=(), out_dtypes=(F32,), acc_out=()):
    nblk, K, n = b.shape
    a_list = list(a) if heads else [a]
    na = len(a_list)
    M = a_list[0].shape[0]
    tm = _tile_m(M, nblk * n)
    assert M % tm == 0
    nex, npar, nro = len(extras), len(params), len(out_dtypes)

    def body(*refs):
        a_refs, b_ref, rest = refs[:na], refs[na], refs[na + 1:]
        if heads:
            rest, scr = rest[:-2], rest[-2:]
        ex, outs = rest[:nex + npar], rest[nex + npar:]
        acc = None
        for j in range(nblk):
            aj = _gather_heads(a_refs, j, scr[j % 2]) if heads else a_refs[0][:, j * n:(j + 1) * n]
            part = _dot_nt(aj, b_ref[j])
            acc = part if acc is None else acc + part
        res = epi(acc, *[e[...] for e in ex]) if epi is not None else (acc,)
        for o, r in zip(outs[:nro], res[:nro]):
            o[...] = r.astype(o.dtype)
        if acc_out:
            @pl.when(pl.program_id(0) == 0)
            def _():
                for o in outs[nro:]:
                    o[...] = jnp.zeros_like(o)
            for o, r in zip(outs[nro:], res[nro:]):
                o[...] += r.astype(o.dtype)

    a_specs = [pl.BlockSpec((tm, t.shape[1]), lambda i: (i, 0)) for t in a_list]
    row = pl.BlockSpec((tm, K), lambda i: (i, 0))
    const = lambda shp: pl.BlockSpec(shp, lambda i, nd=len(shp): (0,) * nd)
    return pl.pallas_call(
        body, grid=(M // tm,),
        in_specs=a_specs + [pl.BlockSpec((nblk, K, n), lambda i: (0, 0, 0))] + [row] * nex
        + [const(p.shape) for p in params],
        out_specs=[row] * nro + [const(s) for s, dt in acc_out],
        out_shape=[jax.ShapeDtypeStruct((M, K), dt) for dt in out_dtypes]
        + [jax.ShapeDtypeStruct(s, dt) for s, dt in acc_out],
        scratch_shapes=[pltpu.VMEM((tm, n), BF16)] * 2 if heads else [],
        name=name, compiler_params=_cparams(("arbitrary",)))(*a_list, b, *extras, *params)


def wgrad(name, a, g, *, cb=False, heads=False):
    M, K = a.shape
    g_list = list(g) if heads else [g]
    tm, tk = min(M, 512 if heads else 1024), min(K, 512 if heads else 1024)
    if cb:
        n = SLOTS_PER_DEV * HEAD_DIM if heads else g.shape[1] // N_DEV
        nj = N_DEV
        while nj > 1 and nj * tk * n * 6 > (14 << 20):
            nj //= 2
        assert nj == N_DEV or not heads
        grid = (K // tk, N_DEV // nj, M // tm)
        g_specs = ([pl.BlockSpec((tm, t.shape[1]), lambda k, j, m: (m, 0)) for t in g_list] if heads
                   else [pl.BlockSpec((tm, nj * n), lambda k, j, m: (m, j))])
        o_spec = pl.BlockSpec((nj, tk, n), lambda k, j, m: (j, k, 0))
        o_shape = (N_DEV, K, n)
    else:
        N = g.shape[1]
        tn = min(N, 1024)
        nj = 1
        grid = (K // tk, N // tn, M // tm)
        g_specs = [pl.BlockSpec((tm, tn), lambda k, j, m: (m, j))]
        o_spec = pl.BlockSpec((tk, tn), lambda k, j, m: (k, j))
        o_shape = (K, N)
    nm = M // tm
    ng = len(g_list)

    def body(a_ref, *rest):
        g_refs, o_ref, o16_ref, scr = rest[:ng], rest[ng], rest[ng + 1], rest[ng + 2:]
        m = pl.program_id(2)

        @pl.when(m == 0)
        def _():
            o_ref[...] = jnp.zeros_like(o_ref)
        at = a_ref[...].T
        if cb:
            for jj in range(nj):
                gj = (_gather_heads(g_refs, jj, scr[jj % 2]) if heads
                      else g_refs[0][:, jj * n:(jj + 1) * n])
                o_ref[jj] += _dot(at, gj)
        else:
            o_ref[...] += _dot(at, g_refs[0][...])

        @pl.when(m == nm - 1)
        def _():
            o16_ref[...] = o_ref[...].astype(BF16)

    return pl.pallas_call(
        body, grid=grid, in_specs=[pl.BlockSpec((tm, tk), lambda k, j, m: (m, k))] + g_specs,
        out_specs=[o_spec, o_spec],
        out_shape=[jax.ShapeDtypeStruct(o_shape, F32), jax.ShapeDtypeStruct(o_shape, BF16)],
        scratch_shapes=[pltpu.VMEM((tm, SLOTS_PER_DEV * HEAD_DIM), BF16)] * 2 if heads else [],
        name=name, compiler_params=_cparams(("arbitrary", "arbitrary", "arbitrary")))(a, *g_list)


def _rms(x, g):
    x = x.astype(F32)
    return x * lax.rsqrt(jnp.mean(x * x, axis=-1, keepdims=True) + EPS) * g


def _ln(x, g, b):
    mu = jnp.mean(x, axis=-1, keepdims=True)
    var = jnp.mean(jnp.square(x - mu), axis=-1, keepdims=True)
    return (x - mu) * lax.rsqrt(var + EPS) * g + b


def _glu(z):
    d = z.shape[1] // 2
    return z[:, :d] * jax.nn.sigmoid(z[:, d:])


def _glu_bwd(z, dy):
    d = z.shape[1] // 2
    a, s = z[:, :d], jax.nn.sigmoid(z[:, d:])
    return jnp.concatenate([dy * s, dy * a * s * (1.0 - s)], axis=1)


def _colsum(v):
    return jnp.sum(v.astype(F32), axis=0, keepdims=True)


def rms_fwd(name, x, g, want_f32=False):
    def fn(xt, gt):
        h = _rms(xt, gt)
        return (h, h) if want_f32 else (h,)
    D = x.shape[1]
    outs = [(D, BF16)] + ([(D, F32)] if want_f32 else [])
    return rowwise(name, fn, [x], [g], outs)


def rms_bwd(name, x, dh, dres, g):
    def fn(xt, dht, drt, gt):
        _, vjp = jax.vjp(_rms, xt, gt)
        dx, dg = vjp(dht.astype(F32))
        dx = dx + drt
        return dx, dx, dg
    D = x.shape[1]
    return rowwise(name, fn, [x, dh, dres], [g], [(D, F32), (D, BF16)], [((1, D), F32)])


def s5_disc(a_re, a_im, log_dt, b_re, b_im):
    dt = jnp.exp(log_dt)[:, None]
    er = jnp.exp(a_re * dt)
    lam_re = er * jnp.cos(a_im * dt)
    lam_im = er * jnp.sin(a_im * dt)
    nr, ni = lam_re - 1.0, lam_im
    den = a_re * a_re + a_im * a_im
    f_re = (nr * a_re + ni * a_im) / den
    f_im = (ni * a_re - nr * a_im) / den
    bb_re = f_re[..., None] * b_re - f_im[..., None] * b_im
    bb_im = f_re[..., None] * b_im + f_im[..., None] * b_re
    return lam_re, lam_im, bb_re, bb_im


def _s5_blockdiag_b(bb):
    t = bb.reshape(S5_NGB, S5_GB, SSM_STATE, SSM_GROUP).transpose(0, 1, 3, 2)
    eye = jnp.eye(S5_GB, dtype=bb.dtype)
    return jnp.einsum('bgpn,gh->bgphn', t, eye).reshape(S5_NGB, S5_CH, S5_ST)


def _s5_blockdiag_b_inv(x):
    t = x.reshape(S5_NGB, S5_GB, SSM_GROUP, S5_GB, SSM_STATE)
    eye = jnp.eye(S5_GB, dtype=x.dtype)
    d = jnp.einsum('bgphn,gh->bgpn', t, eye)
    return d.transpose(0, 1, 3, 2).reshape(SSM_GROUPS, SSM_STATE, SSM_GROUP)


def _s5_blockdiag_c(c):
    t = c.reshape(S5_NGB, S5_GB, SSM_GROUP, SSM_STATE).transpose(0, 1, 3, 2)
    eye = jnp.eye(S5_GB, dtype=c.dtype)
    return jnp.einsum('bgnp,gh->bgnhp', t, eye).reshape(S5_NGB, S5_ST, S5_CH)


def _s5_blockdiag_c_inv(x):
    t = x.reshape(S5_NGB, S5_GB, SSM_STATE, S5_GB, SSM_GROUP)
    eye = jnp.eye(S5_GB, dtype=x.dtype)
    d = jnp.einsum('bgnhp,gh->bgnp', t, eye)
    return d.transpose(0, 1, 3, 2).reshape(SSM_GROUPS, SSM_GROUP, SSM_STATE)


def s5_tables(lam_re, lam_im, L):
    pr, pi = lam_re.reshape(1, -1), lam_im.reshape(1, -1)
    n = 1
    while n < L:
        lr, li = pr[n - 1:n], pi[n - 1:n]
        pr, pi = (jnp.concatenate([pr, pr * lr - pi * li], 0), jnp.concatenate([pi, pr * li + pi * lr], 0))
        n *= 2
    nk = int(math.log2(L))
    idx = [2 ** k - 1 for k in range(nk)] + [0] * (8 - nk)

    def blk(t):
        return t.reshape(t.shape[0], S5_NGB, S5_ST).transpose(1, 0, 2)

    def rows(t):
        return jnp.concatenate([t[j:j + 1] for j in idx], axis=0)
    return blk(pr), blk(pi), blk(rows(pr)), blk(rows(pi))


S5_SUB = 8


def _scan_tiles(br, bi, a2r, a2i, reverse):
    L = br.shape[0]
    sub = lax.broadcasted_iota(jnp.int32, br.shape, 0) & (S5_SUB - 1)
    xr, xi = br, bi
    for k in range(3):
        s = 1 << k
        ar, ai = a2r[k:k + 1, :], a2i[k:k + 1, :]
        if reverse:
            sr, si = pltpu.roll(xr, L - s, 0), pltpu.roll(xi, L - s, 0)
            m = sub < S5_SUB - s
        else:
            sr, si = pltpu.roll(xr, s, 0), pltpu.roll(xi, s, 0)
            m = sub >= s
        sr, si = jnp.where(m, sr, 0.0), jnp.where(m, si, 0.0)
        xr, xi = xr + ar * sr - ai * si, xi + ar * si + ai * sr
    return xr, xi


def _scan_chain(xr, xi, pr, pi, cr, ci, out_r, out_i, reverse):
    ntile = xr.shape[0] // S5_SUB
    for g in (reversed(range(ntile)) if reverse else range(ntile)):
        rs = slice(g * S5_SUB, (g + 1) * S5_SUB)
        if reverse:
            nr = xr[rs] + pr * cr + pi * ci
            ni = xi[rs] + pr * ci - pi * cr
            cr, ci = nr[0:1], ni[0:1]
        else:
            nr = xr[rs] + pr * cr - pi * ci
            ni = xi[rs] + pr * ci + pi * cr
            cr, ci = nr[S5_SUB - 1:S5_SUB], ni[S5_SUB - 1:S5_SUB]
        out_r[rs, :] = nr
        out_i[rs, :] = ni
    return cr, ci


def _grid_step(shape):
    s = 0
    for ax, n in enumerate(shape):
        s = s * n + pl.program_id(ax)
    return s


def s5_fwd(h, bre, bim, cre, cim, pwr, pwi, l2r, l2i, dskip, bsz, gather=()):
    T, D = h.shape
    L = S5_L
    S = T // bsz
    NC = S // L
    ng = len(gather)
    grid = (S5_NGB, bsz, NC)
    nsteps = S5_NGB * bsz * NC
    fwd_step = nsteps - max(1, nsteps // 32)

    def body(*refs):
        (h_ref, bre_ref, bim_ref, cre_ref, cim_ref, pwr_ref, pwi_ref, l2r_ref, l2i_ref, d_ref) = refs[:10]
        x_refs = refs[10:10 + ng]
        y_ref, gy_ref, xs_ref, xr_s, xi_s = refs[10 + ng:15 + ng]
        g_refs = refs[15 + ng:15 + 2 * ng]
        car_r, car_i = refs[15 + 2 * ng:17 + 2 * ng]
        if ng:
            start, forward, finish = _gather_phases(x_refs, g_refs, *refs[17 + 2 * ng:])
            step = _grid_step(grid)
            pl.when(step == 0)(start)
            pl.when(step == fwd_step)(forward)

        @pl.when(pl.program_id(2) == 0)
        def _():
            car_r[...] = jnp.zeros_like(car_r)
            car_i[...] = jnp.zeros_like(car_i)
        u = h_ref[...]
        ub = u.astype(BF16)
        cr, ci = car_r[0:1, :], car_i[0:1, :]
        xs_ref[...] = jnp.zeros_like(xs_ref)
        xs_ref[0:1, :] = cr
        xs_ref[1:2, :] = ci
        xr, xi = _scan_tiles(_dot(ub, bre_ref[...]), _dot(ub, bim_ref[...]), l2r_ref[...], l2i_ref[...], False)
        cr, ci = _scan_chain(xr, xi, pwr_ref[...], pwi_ref[...], cr, ci, xr_s, xi_s, False)
        car_r[...] = jnp.broadcast_to(cr, car_r.shape)
        car_i[...] = jnp.broadcast_to(ci, car_i.shape)
        y = (_dot(xr_s[...].astype(BF16), cre_ref[...]) - _dot(xi_s[...].astype(BF16), cim_ref[...])
             + d_ref[...] * u)
        y_ref[...] = y
        gy_ref[...] = jax.nn.gelu(y).astype(BF16)
        if ng:
            pl.when(step == nsteps - 1)(finish)

    tok = lambda g, b, c: (b * NC + c, g)
    par = lambda g, b, c: (g, 0, 0)
    anyspec = pl.BlockSpec(memory_space=pl.ANY)
    return pl.pallas_call(
        body, grid=grid,
        in_specs=[pl.BlockSpec((L, S5_CH), tok),
                  pl.BlockSpec((None, S5_CH, S5_ST), par), pl.BlockSpec((None, S5_CH, S5_ST), par),
                  pl.BlockSpec((None, S5_ST, S5_CH), par), pl.BlockSpec((None, S5_ST, S5_CH), par),
                  pl.BlockSpec((None, 8, S5_ST), par), pl.BlockSpec((None, 8, S5_ST), par),
                  pl.BlockSpec((None, 8, S5_ST), par), pl.BlockSpec((None, 8, S5_ST), par),
                  pl.BlockSpec((1, S5_CH), lambda g, b, c: (0, g))] + [anyspec] * ng,
        out_specs=[pl.BlockSpec((L, S5_CH), tok), pl.BlockSpec((L, S5_CH), tok),
                   pl.BlockSpec((None, 8, S5_ST), lambda g, b, c: (b * NC + c, 0, g)),
                   pl.BlockSpec((L, S5_ST), tok), pl.BlockSpec((L, S5_ST), tok)] + [anyspec] * ng,
        out_shape=[jax.ShapeDtypeStruct((T, D), F32), jax.ShapeDtypeStruct((T, D), BF16),
                   jax.ShapeDtypeStruct((bsz * NC, 8, S5_NGB * S5_ST), F32),
                   jax.ShapeDtypeStruct((T, S5_NGB * S5_ST), F32), jax.ShapeDtypeStruct((T, S5_NGB * S5_ST), F32)]
        + _gather_out_shapes(gather),
        scratch_shapes=[pltpu.VMEM((8, S5_ST), F32), pltpu.VMEM((8, S5_ST), F32)]
        + (_gather_sems(ng) if ng else []),
        name="s5_fwd", compiler_params=_cparams(("arbitrary", "arbitrary", "arbitrary")),
    )(h, bre, bim, cre, cim, pwr, pwi, l2r, l2i, dskip, *gather)


def s5_bwd(h, ypre, dgy, xs, xr, xi, bre, bim, cre, cim, pwr_rev, pwi_rev, l2r, l2i, dskip, bsz, chips=()):
    T, D = h.shape
    L = S5_L
    S = T // bsz
    NC = S // L
    nc = len(chips)
    grid = (S5_NGB, bsz, NC)
    nsteps = S5_NGB * bsz * NC

    def body(*refs):
        (h_ref, yp_ref, dg_ref, xs_ref, xr_ref, xi_ref, bre_ref, bim_ref, cre_ref, cim_ref, qr_ref, qi_ref,
         l2r_ref, l2i_ref, d_ref) = refs[:15]
        p_refs = refs[15:15 + nc]
        du_ref, dbr_ref, dbi_ref, dcr_ref, dci_ref, dl_ref, dd_ref = refs[15 + nc:22 + nc]
        r_refs = refs[22 + nc:22 + 2 * nc]
        car_r, car_i, dr_s, di_s = refs[22 + 2 * nc:26 + 2 * nc]
        if nc:
            start, finish = _chips_phases(p_refs, r_refs, *refs[26 + 2 * nc:])
            step = _grid_step(grid)
            pl.when(step == 0)(start)
        first = (pl.program_id(1) == 0) & (pl.program_id(2) == 0)

        @pl.when(first)
        def _():
            for r in (dbr_ref, dbi_ref, dcr_ref, dci_ref, dl_ref, dd_ref):
                r[...] = jnp.zeros_like(r)

        @pl.when(pl.program_id(2) == 0)
        def _():
            car_r[...] = jnp.zeros_like(car_r)
            car_i[...] = jnp.zeros_like(car_i)

        u = h_ref[...]
        ub = u.astype(BF16)
        dyv = jax.vjp(jax.nn.gelu, yp_ref[...])[1](dg_ref[...])[0]
        dyb = dyv.astype(BF16)
        l2r_v, l2i_v = l2r_ref[...], l2i_ref[...]
        x0r, x0i = xs_ref[0:1, :], xs_ref[1:2, :]
        xr, xi = xr_ref[...], xi_ref[...]
        gr = _dot_nt(dyb, cre_ref[...])
        gi = -_dot_nt(dyb, cim_ref[...])
        dr, di = _scan_tiles(gr, gi, l2r_v, -l2i_v, True)
        cr, ci = _scan_chain(dr, di, qr_ref[...], qi_ref[...], car_r[0:1, :], car_i[0:1, :], dr_s, di_s, True)
        dr, di = dr_s[...], di_s[...]
        car_r[...] = jnp.broadcast_to(cr, car_r.shape)
        car_i[...] = jnp.broadcast_to(ci, car_i.shape)
        row = lax.broadcasted_iota(jnp.int32, xr.shape, 0)
        xpr = jnp.where(row >= 1, pltpu.roll(xr, 1, 0), x0r)
        xpi = jnp.where(row >= 1, pltpu.roll(xi, 1, 0), x0i)
        dl_ref[0:1, :] += _colsum(dr * xpr + di * xpi)
        dl_ref[1:2, :] += _colsum(di * xpr - dr * xpi)
        drb, dib = dr.astype(BF16), di.astype(BF16)
        dcr_ref[...] += _dot_tn(xr.astype(BF16), dyb)
        dci_ref[...] -= _dot_tn(xi.astype(BF16), dyb)
        dbr_ref[...] += _dot_tn(ub, drb)
        dbi_ref[...] += _dot_tn(ub, dib)
        du_ref[...] = _dot_nt(drb, bre_ref[...]) + _dot_nt(dib, bim_ref[...]) + d_ref[...] * dyv
        dd_ref[0:1, :] += _colsum(dyv * u)
        if nc:
            pl.when(step == nsteps - 1)(finish)

    tok = lambda g, b, c: (b * NC + (NC - 1 - c), g)
    par = lambda g, b, c: (g, 0, 0)
    anyspec = pl.BlockSpec(memory_space=pl.ANY)
    return pl.pallas_call(
        body, grid=grid,
        in_specs=[pl.BlockSpec((L, S5_CH), tok), pl.BlockSpec((L, S5_CH), tok), pl.BlockSpec((L, S5_CH), tok),
                  pl.BlockSpec((None, 8, S5_ST), lambda g, b, c: (b * NC + (NC - 1 - c), 0, g)),
                  pl.BlockSpec((L, S5_ST), tok), pl.BlockSpec((L, S5_ST), tok),
                  pl.BlockSpec((None, S5_CH, S5_ST), par), pl.BlockSpec((None, S5_CH, S5_ST), par),
                  pl.BlockSpec((None, S5_ST, S5_CH), par), pl.BlockSpec((None, S5_ST, S5_CH), par),
                  pl.BlockSpec((None, 8, S5_ST), par), pl.BlockSpec((None, 8, S5_ST), par),
                  pl.BlockSpec((None, 8, S5_ST), par), pl.BlockSpec((None, 8, S5_ST), par),
                  pl.BlockSpec((1, S5_CH), lambda g, b, c: (0, g))] + [anyspec] * nc,
        out_specs=[pl.BlockSpec((L, S5_CH), tok),
                   pl.BlockSpec((None, S5_CH, S5_ST), par), pl.BlockSpec((None, S5_CH, S5_ST), par),
                   pl.BlockSpec((None, S5_ST, S5_CH), par), pl.BlockSpec((None, S5_ST, S5_CH), par),
                   pl.BlockSpec((None, 8, S5_ST), par),
                   pl.BlockSpec((8, S5_CH), lambda g, b, c: (0, g))] + [anyspec] * nc,
        out_shape=[jax.ShapeDtypeStruct((T, D), F32),
                   jax.ShapeDtypeStruct((S5_NGB, S5_CH, S5_ST), F32), jax.ShapeDtypeStruct((S5_NGB, S5_CH, S5_ST), F32),
                   jax.ShapeDtypeStruct((S5_NGB, S5_ST, S5_CH), F32), jax.ShapeDtypeStruct((S5_NGB, S5_ST, S5_CH), F32),
                   jax.ShapeDtypeStruct((S5_NGB, 8, S5_ST), F32), jax.ShapeDtypeStruct((8, D), F32)]
        + _chips_out_shapes(chips),
        scratch_shapes=[pltpu.VMEM((8, S5_ST), F32), pltpu.VMEM((8, S5_ST), F32)]
        + [pltpu.VMEM((L, S5_ST), F32)] * 2 + (_chips_sems(nc) if nc else []),
        name="s5_bwd", compiler_params=_cparams(("arbitrary", "arbitrary", "arbitrary")),
    )(h, ypre, dgy, xs, xr, xi, bre, bim, cre, cim, pwr_rev, pwi_rev, l2r, l2i, dskip, *chips)


def _shift_rows(win, off, n):
    if off == 0:
        return win[:n]
    return pltpu.roll(win, win.shape[0] - off, 0)[:n]


def dwconv_fwd(z, w, b, bsz):
    T, D = z.shape
    S = T // bsz
    TS, CW, PAD = CONV_TS, CONV_CW, CONV_PAD

    def body(z_ref, w_ref, b_ref, y_ref, zp):
        zp[0:PAD, :] = jnp.zeros((PAD, CW), F32)
        zp[PAD:, :] = z_ref[...]
        wv, bv = w_ref[...], b_ref[...]

        def step(t, carry):
            base = pl.multiple_of(t * TS, TS)
            win = zp[pl.ds(base, TS + PAD), :]
            acc = jnp.zeros((TS, CW), F32) + bv
            for k in range(CONV_WIDTH):
                acc = acc + wv[k:k + 1, :] * _shift_rows(win, PAD - (CONV_WIDTH - 1) + k, TS)
            y_ref[pl.ds(base, TS), :] = acc
            return carry
        lax.fori_loop(0, S // TS, step, 0)

    return pl.pallas_call(
        body, grid=(D // CW, bsz),
        in_specs=[pl.BlockSpec((S, CW), lambda c, bb: (bb, c)), pl.BlockSpec((32, CW), lambda c, bb: (0, c)),
                  pl.BlockSpec((1, CW), lambda c, bb: (0, c))],
        out_specs=pl.BlockSpec((S, CW), lambda c, bb: (bb, c)),
        out_shape=jax.ShapeDtypeStruct((T, D), F32),
        scratch_shapes=[pltpu.VMEM((S + PAD, CW), F32)],
        name="dwconv_fwd", compiler_params=_cparams(("arbitrary", "arbitrary")),
    )(z, w, b)


def dwconv_bwd(z, dy, w, bsz):
    T, D = z.shape
    S = T // bsz
    TS, CW, PAD = CONV_TS, CONV_CW, CONV_PAD

    def body(z_ref, dy_ref, w_ref, dz_ref, dw_ref, db_ref, zp, dyp):
        @pl.when(pl.program_id(1) == 0)
        def _():
            dw_ref[...] = jnp.zeros_like(dw_ref)
            db_ref[...] = jnp.zeros_like(db_ref)
        zp[0:PAD, :] = jnp.zeros((PAD, CW), F32)
        zp[PAD:, :] = z_ref[...]
        dyp[0:S, :] = dy_ref[...]
        dyp[S:, :] = jnp.zeros((PAD, CW), F32)
        wv = w_ref[...]

        def step(t, carry):
            base = pl.multiple_of(t * TS, TS)
            zwin = zp[pl.ds(base, TS + PAD), :]
            dwin = dyp[pl.ds(base, TS + PAD), :]
            dyt = dwin[:TS]
            acc = jnp.zeros((TS, CW), F32)
            for j in range(CONV_WIDTH):
                k = CONV_WIDTH - 1 - j
                acc = acc + wv[k:k + 1, :] * _shift_rows(dwin, j, TS)
            dz_ref[pl.ds(base, TS), :] = acc
            for k in range(CONV_WIDTH):
                prod = dyt * _shift_rows(zwin, PAD - (CONV_WIDTH - 1) + k, TS)
                dw_ref[8 * k:8 * k + 8, :] += jnp.sum(prod.reshape(TS // 8, 8, CW), axis=0)
            db_ref[...] += jnp.sum(dyt.reshape(TS // 8, 8, CW), axis=0)
            return carry
        lax.fori_loop(0, S // TS, step, 0)

    dz, dw, db = pl.pallas_call(
        body, grid=(D // CW, bsz),
        in_specs=[pl.BlockSpec((S, CW), lambda c, bb: (bb, c)), pl.BlockSpec((S, CW), lambda c, bb: (bb, c)),
                  pl.BlockSpec((32, CW), lambda c, bb: (0, c))],
        out_specs=[pl.BlockSpec((S, CW), lambda c, bb: (bb, c)), pl.BlockSpec((8 * 32, CW), lambda c, bb: (0, c)),
                   pl.BlockSpec((8, CW), lambda c, bb: (0, c))],
        out_shape=[jax.ShapeDtypeStruct((T, D), F32), jax.ShapeDtypeStruct((8 * 32, D), F32),
                   jax.ShapeDtypeStruct((8, D), F32)],
        scratch_shapes=[pltpu.VMEM((S + PAD, CW), F32), pltpu.VMEM((S + PAD, CW), F32)],
        name="dwconv_bwd", compiler_params=_cparams(("arbitrary", "arbitrary")),
    )(z, dy, w)
    return dz, dw.reshape(32, 8, D).sum(axis=1), db.sum(axis=0, keepdims=True)


def spatial_fwd(u, vln, ws, bias):
    T, E = u.shape
    C, H = GMLP_CHUNK, GMLP_HEADS
    hw = E // H

    def body(u_ref, v_ref, ws_ref, b_ref, o_ref):
        for hh in range(H):
            sl = slice(hh * hw, (hh + 1) * hw)
            vp = _dot(ws_ref[hh], v_ref[:, sl]) + b_ref[:, sl]
            o_ref[:, sl] = (u_ref[:, sl] * vp).astype(o_ref.dtype)

    return pl.pallas_call(
        body, grid=(T // C,),
        in_specs=[pl.BlockSpec((C, E), lambda i: (i, 0)), pl.BlockSpec((C, E), lambda i: (i, 0)),
                  pl.BlockSpec((H, C, C), lambda i: (0, 0, 0)), pl.BlockSpec((C, E), lambda i: (0, 0))],
        out_specs=pl.BlockSpec((C, E), lambda i: (i, 0)),
        out_shape=jax.ShapeDtypeStruct((T, E), BF16),
        name="spatial_fwd", compiler_params=_cparams(("arbitrary",)),
    )(u, vln, ws, bias)


def spatial_bwd(u, vln, dg, ws, bias):
    T, E = u.shape
    C, H = GMLP_CHUNK, GMLP_HEADS
    hw = E // H

    def body(u_ref, v_ref, dg_ref, ws_ref, b_ref, du_ref, dv_ref, dws_ref, db_ref):
        @pl.when(pl.program_id(0) == 0)
        def _():
            dws_ref[...] = jnp.zeros_like(dws_ref)
            db_ref[...] = jnp.zeros_like(db_ref)
        tril = (lax.broadcasted_iota(jnp.int32, (C, C), 1) <= lax.broadcasted_iota(jnp.int32, (C, C), 0))
        for hh in range(H):
            sl = slice(hh * hw, (hh + 1) * hw)
            v = v_ref[:, sl]
            w = ws_ref[hh]
            dgv = dg_ref[:, sl].astype(F32)
            vp = _dot(w, v) + b_ref[:, sl]
            du_ref[:, sl] = dgv * vp
            dvp = dgv * u_ref[:, sl]
            dvpb = dvp.astype(BF16)
            dv_ref[:, sl] = _dot_tn(w, dvpb)
            dws_ref[hh] += jnp.where(tril, _dot_nt(dvpb, v), 0.0)
            db_ref[:, sl] += dvp

    return pl.pallas_call(
        body, grid=(T // C,),
        in_specs=[pl.BlockSpec((C, E), lambda i: (i, 0)), pl.BlockSpec((C, E), lambda i: (i, 0)),
                  pl.BlockSpec((C, E), lambda i: (i, 0)),
                  pl.BlockSpec((H, C, C), lambda i: (0, 0, 0)), pl.BlockSpec((C, E), lambda i: (0, 0))],
        out_specs=[pl.BlockSpec((C, E), lambda i: (i, 0)), pl.BlockSpec((C, E), lambda i: (i, 0)),
                   pl.BlockSpec((H, C, C), lambda i: (0, 0, 0)), pl.BlockSpec((C, E), lambda i: (0, 0))],
        out_shape=[jax.ShapeDtypeStruct((T, E), F32), jax.ShapeDtypeStruct((T, E), F32),
                   jax.ShapeDtypeStruct((H, C, C), F32), jax.ShapeDtypeStruct((C, E), F32)],
        name="spatial_bwd", compiler_params=_cparams(("arbitrary",)),
    )(u, vln, dg, ws, bias)


def _att_masks():
    r = lax.broadcasted_iota(jnp.int32, (ATT_BLK, ATT_BLK), 0)
    c = lax.broadcasted_iota(jnp.int32, (ATT_BLK, ATT_BLK), 1)
    return c <= r, c >= r


NEG = -1e30
ATT_SCALE = HEAD_DIM ** -0.5


def _att_view(t, dil):
    return t.reshape(t.shape[0] // dil, dil * t.shape[1])


def attn_fwd(name, q, k, v, dil, bsz):
    T, Wd = q.shape
    nb = T // (bsz * dil * ATT_BLK)
    TB = min(nb, ATT_TB)
    nsteps = nb // TB

    def body(q_ref, k_ref, v_ref, kp_ref, vp_ref, o_ref, l_ref):
        n = pl.program_id(2)
        mc, mp = _att_masks()
        for j in range(TB):
            rows = slice(j * ATT_BLK, (j + 1) * ATT_BLK)
            prow = slice((j - 1) * ATT_BLK, j * ATT_BLK)
            hp = (n * TB + j) > 0
            H = range(ATT_HEADS)
            ls = [slice(hh * HEAD_DIM, (hh + 1) * HEAD_DIM) for hh in H]
            qj = [q_ref[rows, ls[hh]] for hh in H]
            kc = [k_ref[rows, ls[hh]] for hh in H]
            kp = [k_ref[prow, ls[hh]] if j > 0 else kp_ref[:, ls[hh]] for hh in H]
            sc = [jnp.where(mc, _dot_nt(qj[hh], kc[hh]) * ATT_SCALE, NEG) for hh in H]
            sp = [jnp.where(mp & hp, _dot_nt(qj[hh], kp[hh]) * ATT_SCALE, NEG) for hh in H]
            m = [jnp.maximum(jnp.max(sc[hh], axis=1, keepdims=True), jnp.max(sp[hh], axis=1, keepdims=True))
                 for hh in H]
            pc = [jnp.exp(sc[hh] - m[hh]) for hh in H]
            pp = [jnp.exp(sp[hh] - m[hh]) for hh in H]
            l = [jnp.sum(pc[hh], axis=1, keepdims=True) + jnp.sum(pp[hh], axis=1, keepdims=True) for hh in H]
            vc = [v_ref[rows, ls[hh]] for hh in H]
            vp = [v_ref[prow, ls[hh]] if j > 0 else vp_ref[:, ls[hh]] for hh in H]
            for hh in H:
                o_ref[rows, ls[hh]] = (_dot(pc[hh].astype(BF16), vc[hh]) + _dot(pp[hh].astype(BF16), vp[hh])) / l[hh]
                l_ref[rows, ls[hh]] = jnp.broadcast_to(m[hh] + jnp.log(l[hh]), (ATT_BLK, HEAD_DIM))

    blk = pl.BlockSpec((TB * ATT_BLK, Wd), lambda b, r, n: (b * nsteps + n, r))
    prev = pl.BlockSpec((ATT_BLK, Wd), lambda b, r, n: (jnp.maximum(b * nb + n * TB - 1, 0), r))
    qv, kv, vv = (_att_view(t, dil) for t in (q, k, v))
    o, l = pl.pallas_call(
        body, grid=(bsz, dil, nsteps), in_specs=[blk, blk, blk, prev, prev], out_specs=[blk, blk],
        out_shape=[jax.ShapeDtypeStruct(qv.shape, F32), jax.ShapeDtypeStruct(qv.shape, F32)],
        name=name, compiler_params=_cparams(("arbitrary", "arbitrary", "arbitrary")),
    )(qv, kv, vv, kv, vv)
    return o.reshape(T, Wd), l.reshape(T, Wd)


def attn_bwd(name, q, k, v, do, mg, lse, dil, bsz):
    T, Wd = q.shape
    nb = T // (bsz * dil * ATT_BLK)
    TB = min(nb, ATT_TB_BWD)
    nsteps = nb // TB

    def body(q_ref, k_ref, v_ref, do_ref, mg_ref, l_ref, kp_ref, vp_ref, qn_ref, don_ref, mgn_ref, ln_ref,
             dq_ref, dk_ref, dv_ref):
        n = pl.program_id(2)
        mc, mp = _att_masks()

        def probs_all(qs, ks, lse_cols, mask):
            s = [_dot_nt(qh, kh) * ATT_SCALE for qh, kh in zip(qs, ks)]
            return [jnp.where(mask, jnp.exp(sh - lc), 0.0) for sh, lc in zip(s, lse_cols)]

        def ds_all(ps, dos, vs, deltas):
            dp = [_dot_nt(dh, vh) for dh, vh in zip(dos, vs)]
            return [(ph * (dph - dl) * ATT_SCALE).astype(BF16) for ph, dph, dl in zip(ps, dp, deltas)]

        H = range(ATT_HEADS)
        ls = [slice(hh * HEAD_DIM, (hh + 1) * HEAD_DIM) for hh in H]
        dk = [[None] * TB for _ in H]
        dv = [[None] * TB for _ in H]
        for j in range(TB + 1):
            rows = slice(j * ATT_BLK, (j + 1) * ATT_BLK)
            prow = slice((j - 1) * ATT_BLK, j * ATT_BLK)
            if j < TB:
                srcs = (q_ref, do_ref, mg_ref, l_ref)
                qj, doj, mgj, lj = ([r[rows, ls[hh]] for hh in H] for r in srcs)
                hp = (n * TB + j) > 0
            else:
                srcs = (qn_ref, don_ref, mgn_ref, ln_ref)
                qj, doj, mgj, lj = ([r[:, ls[hh]] for hh in H] for r in srcs)
                hp = (n + 1) * TB < nb
            lse_col = [lj[hh][:, 0:1] for hh in H]
            delta = [jnp.sum(doj[hh].astype(F32) * mgj[hh].astype(F32), axis=1, keepdims=True) for hh in H]
            if j > 0:
                kp = [k_ref[prow, ls[hh]] for hh in H]
                vp = [v_ref[prow, ls[hh]] for hh in H]
            else:
                kp = [kp_ref[:, ls[hh]] for hh in H]
                vp = [vp_ref[:, ls[hh]] for hh in H]
            pp = probs_all(qj, kp, lse_col, mp & hp)
            dsp = ds_all(pp, doj, vp, delta)
            if j > 0:
                for hh in H:
                    dk[hh][j - 1] = dk[hh][j - 1] + _dot_tn(dsp[hh], qj[hh])
                    dv[hh][j - 1] = dv[hh][j - 1] + _dot_tn(pp[hh].astype(BF16), doj[hh])
            if j < TB:
                kc = [k_ref[rows, ls[hh]] for hh in H]
                vc = [v_ref[rows, ls[hh]] for hh in H]
                pc = probs_all(qj, kc, lse_col, mc)
                dsc = ds_all(pc, doj, vc, delta)
                for hh in H:
                    dq_ref[rows, ls[hh]] = (_dot(dsc[hh], kc[hh]) + _dot(dsp[hh], kp[hh])).astype(dq_ref.dtype)
                for hh in H:
                    dk[hh][j] = _dot_tn(dsc[hh], qj[hh])
                    dv[hh][j] = _dot_tn(pc[hh].astype(BF16), doj[hh])
        for j in range(TB):
            rows = slice(j * ATT_BLK, (j + 1) * ATT_BLK)
            for hh in H:
                dk_ref[rows, ls[hh]] = dk[hh][j].astype(dk_ref.dtype)
                dv_ref[rows, ls[hh]] = dv[hh][j].astype(dv_ref.dtype)

    blk = pl.BlockSpec((TB * ATT_BLK, Wd), lambda b, r, n: (b * nsteps + n, r))
    prev = pl.BlockSpec((ATT_BLK, Wd), lambda b, r, n: (jnp.maximum(b * nb + n * TB - 1, 0), r))
    nxt = pl.BlockSpec((ATT_BLK, Wd), lambda b, r, n: (b * nb + jnp.minimum((n + 1) * TB, nb - 1), r))
    qv, kv, vv, dov, mgv, lv = (_att_view(t, dil) for t in (q, k, v, do, mg, lse))
    res = pl.pallas_call(
        body, grid=(bsz, dil, nsteps), in_specs=[blk] * 6 + [prev, prev, nxt, nxt, nxt, nxt],
        out_specs=[blk, blk, blk], out_shape=[jax.ShapeDtypeStruct(qv.shape, BF16)] * 3,
        name=name, compiler_params=_cparams(("arbitrary", "arbitrary", "arbitrary")),
    )(qv, kv, vv, dov, mgv, lv, kv, vv, qv, dov, mgv, lv)
    return [t.reshape(T, Wd) for t in res]


QKV_SLOTS = 3 * len(ATT_CONFIGS) * ATT_HEADS
SLOTS_PER_DEV = QKV_SLOTS // N_DEV


def qkv_matmul(name, a, b):
    M, K = a.shape
    nblk, _, n = b.shape
    nout = QKV_SLOTS // ATT_HEADS
    tm = _rows_for(M, K * 2 + nout * ATT_W * 2)

    def body(a_ref, b_ref, *outs):
        av = a_ref[...]
        for c in range(nblk):
            acc = _dot(av, b_ref[c]).astype(BF16)
            for r in range(SLOTS_PER_DEV):
                k, hh = divmod(c * SLOTS_PER_DEV + r, ATT_HEADS)
                outs[k][:, hh * HEAD_DIM:(hh + 1) * HEAD_DIM] = acc[:, r * HEAD_DIM:(r + 1) * HEAD_DIM]

    return pl.pallas_call(
        body, grid=(M // tm,),
        in_specs=[pl.BlockSpec((tm, K), lambda i: (i, 0)), pl.BlockSpec(b.shape, lambda i: (0, 0, 0))],
        out_specs=[pl.BlockSpec((tm, ATT_W), lambda i: (i, 0))] * nout,
        out_shape=[jax.ShapeDtypeStruct((M, ATT_W), BF16)] * nout,
        name=name, compiler_params=_cparams(("arbitrary",)))(a, b)


def _coords():
    return lax.axis_index("x"), lax.axis_index("y"), lax.axis_index("c")


def all_gather(name, xs):
    n = len(xs)

    def body(*refs):
        start, forward, finish = _gather_phases(refs[:n], refs[n:2 * n], *refs[2 * n:])
        start()
        forward()
        finish()

    anyspec = pl.BlockSpec(memory_space=pl.ANY)
    return pl.pallas_call(
        body, out_shape=_gather_out_shapes(xs), in_specs=[anyspec] * n, out_specs=[anyspec] * n,
        scratch_shapes=_gather_sems(n), name=name,
    )(*xs)


def _gather_out_shapes(xs):
    return [jax.ShapeDtypeStruct((N_DEV,) + t.shape, t.dtype) for t in xs]


def _gather_sems(n):
    return [pltpu.SemaphoreType.DMA((7 * n,)), pltpu.SemaphoreType.DMA((7 * n,)), pltpu.SemaphoreType.DMA((n,))]


def _gather_phases(x_refs, out_refs, send_sems, recv_sems, local_sems):
    n = len(x_refs)

    def parts():
        x, y, c = _coords()
        return (x, y, c), (x, y, 1 - c), [(1 - x, y), (x, 1 - y), (1 - x, 1 - y)], c

    def slot(a, px, py, pc):
        return out_refs[a].at[4 * px + 2 * py + pc]

    def copy(a, k, block, to, src=None):
        return pltpu.make_async_remote_copy(
            src_ref=slot(a, *block) if src is None else src, dst_ref=slot(a, *block),
            send_sem=send_sems.at[7 * a + k], recv_sem=recv_sems.at[7 * a + k],
            device_id=to, device_id_type=MESH)

    def mine(a, me):
        return pltpu.make_async_copy(x_refs[a], slot(a, *me), local_sems.at[a])

    def first(a, me, sibling, chips, c):
        return ([copy(a, 0, me, sibling, src=x_refs[a])]
                + [copy(a, 1 + j, me, (*chip, c), src=x_refs[a]) for j, chip in enumerate(chips)])

    def start():
        me, sibling, chips, c = parts()
        for a in range(n):
            mine(a, me).start()
        for a in range(n):
            for cp in first(a, me, sibling, chips, c):
                cp.start()

    def forward():
        me, sibling, chips, c = parts()
        for j, chip in enumerate(chips):
            for a in range(n):
                copy(a, 1 + j, (*chip, c), me).wait_recv()
                copy(a, 4 + j, (*chip, c), sibling).start()

    def finish():
        me, sibling, chips, c = parts()
        for a in range(n):
            copy(a, 0, sibling, me).wait_recv()
            for j, chip in enumerate(chips):
                copy(a, 4 + j, (*chip, 1 - c), me).wait_recv()
        for a in range(n):
            for cp in first(a, me, sibling, chips, c):
                cp.wait_send()
            for j, chip in enumerate(chips):
                copy(a, 4 + j, (*chip, c), sibling).wait_send()
            mine(a, me).wait()

    return start, forward, finish


def exchange_sibling(name, gs):
    n = len(gs)

    def body(*refs):
        g_refs, out_refs = refs[:n], refs[n:2 * n]
        send_sems, recv_sems = refs[2 * n:]
        x, y, c = _coords()
        sibling = (x, y, 1 - c)
        cps = []
        for a in range(n):
            for q in range(4):
                cps.append(pltpu.make_async_remote_copy(
                    src_ref=g_refs[a].at[2 * q + (1 - c)], dst_ref=out_refs[a].at[q],
                    send_sem=send_sems.at[4 * a + q], recv_sem=recv_sems.at[4 * a + q],
                    device_id=sibling, device_id_type=MESH))
        for cp in cps:
            cp.start()
        for cp in cps:
            cp.wait_recv()
        for cp in cps:
            cp.wait_send()

    anyspec = pl.BlockSpec(memory_space=pl.ANY)
    return pl.pallas_call(
        body, out_shape=[jax.ShapeDtypeStruct((4,) + g.shape[1:], g.dtype) for g in gs],
        in_specs=[anyspec] * n, out_specs=[anyspec] * n,
        scratch_shapes=[pltpu.SemaphoreType.DMA((4 * n,)), pltpu.SemaphoreType.DMA((4 * n,))],
        name=name,
    )(*gs)


def exchange_chips(name, ps):
    n = len(ps)

    def body(*refs):
        start, finish = _chips_phases(refs[:n], refs[n:2 * n], *refs[2 * n:])
        start()
        finish()

    anyspec = pl.BlockSpec(memory_space=pl.ANY)
    return pl.pallas_call(
        body, out_shape=_chips_out_shapes(ps), in_specs=[anyspec] * n, out_specs=[anyspec] * n,
        scratch_shapes=_chips_sems(n), name=name,
    )(*ps)


def _chips_out_shapes(ps):
    return [jax.ShapeDtypeStruct((3,) + p.shape[1:], p.dtype) for p in ps]


def _chips_sems(n):
    return [pltpu.SemaphoreType.DMA((3 * n,)), pltpu.SemaphoreType.DMA((3 * n,))]


def _chips_phases(p_refs, out_refs, send_sems, recv_sems):
    n = len(p_refs)

    def copies():
        x, y, c = _coords()
        chips = [(1 - x, y), (x, 1 - y), (1 - x, 1 - y)]
        return [pltpu.make_async_remote_copy(
            src_ref=p_refs[a].at[2 * px + py], dst_ref=out_refs[a].at[k],
            send_sem=send_sems.at[3 * a + k], recv_sem=recv_sems.at[3 * a + k],
            device_id=(px, py, c), device_id_type=MESH)
            for a in range(n) for k, (px, py) in enumerate(chips)]

    def start():
        for cp in copies():
            cp.start()

    def finish():
        cps = copies()
        for cp in cps:
            cp.wait_recv()
        for cp in cps:
            cp.wait_send()

    return start, finish


def _row_tile(R):
    tr = 256
    while R % tr:
        tr //= 2
    assert tr % 8 == 0
    return tr


def add_sibling(name, g, recv, c_idx):
    _, R, C = g.shape
    tr = _row_tile(R)

    def body(c_ref, g_ref, r_ref, o_ref, o16_ref):
        s = g_ref[...] + r_ref[...].astype(F32)
        o_ref[...] = s
        o16_ref[...] = s.astype(BF16)

    out = pl.BlockSpec((None, tr, C), lambda q, i, cr: (q, i, 0))
    return pl.pallas_call(
        body,
        grid_spec=pltpu.PrefetchScalarGridSpec(
            num_scalar_prefetch=1, grid=(4, R // tr),
            in_specs=[pl.BlockSpec((None, tr, C), lambda q, i, cr: (2 * q + cr[0], i, 0)), out],
            out_specs=[out, out]),
        out_shape=[jax.ShapeDtypeStruct((4, R, C), F32), jax.ShapeDtypeStruct((4, R, C), BF16)], name=name,
        compiler_params=_cparams(("arbitrary", "arbitrary")),
    )(c_idx, g, recv)


def _adam_math(w, g, m, v):
    m = ADAM_B1 * m + (1.0 - ADAM_B1) * g
    v = ADAM_B2 * v + (1.0 - ADAM_B2) * jnp.square(g)
    m_hat = m / (1.0 - ADAM_B1 ** ADAM_STEP)
    v_hat = v / (1.0 - ADAM_B2 ** ADAM_STEP)
    delta = -ADAM_LR * (m_hat / (jnp.sqrt(v_hat) + ADAM_EPS) + ADAM_WD * w)
    return delta, m, v


def adam_big(name, p1, recv, w, m, v, chip_idx, layer=0):
    _, R, C = p1.shape
    tr = _row_tile(R)
    nt = R // tr

    def body(q_ref, p_ref, r_ref, w_ref, m_ref, v_ref, g_ref, d_ref, nm_ref, nv_ref):
        g = ((p_ref[...] + r_ref[0].astype(F32)) + r_ref[1].astype(F32)) + r_ref[2].astype(F32)
        d, nm, nv = _adam_math(w_ref[...], g, m_ref[...], v_ref[...])
        g_ref[...] = g
        d_ref[...] = d
        nm_ref[...] = nm
        nv_ref[...] = nv

    row_in = pl.BlockSpec((tr, C), lambda i, qr: (layer * nt + i, 0))
    row = pl.BlockSpec((tr, C), lambda i, qr: (i, 0))
    return pl.pallas_call(
        body,
        grid_spec=pltpu.PrefetchScalarGridSpec(
            num_scalar_prefetch=1, grid=(nt,),
            in_specs=[pl.BlockSpec((None, tr, C), lambda i, qr: (qr[0], i, 0)),
                      pl.BlockSpec((3, tr, C), lambda i, qr: (0, i, 0)), row_in, row_in, row_in],
            out_specs=[row, row, row, row]),
        out_shape=[jax.ShapeDtypeStruct((R, C), F32)] * 4, name=name,
        compiler_params=_cparams(("arbitrary",)),
    )(chip_idx, p1, recv, w, m, v)


def sum8(parts):
    _, R, C = parts.shape

    def body(p_ref, o_ref):
        acc = p_ref[0]
        for k in range(1, N_DEV):
            acc = acc + p_ref[k]
        o_ref[...] = acc

    tr = 128
    while R % tr:
        tr //= 2
    assert tr % 8 == 0
    return pl.pallas_call(
        body, grid=(R // tr,), in_specs=[pl.BlockSpec((N_DEV, tr, C), lambda i: (0, i, 0))],
        out_specs=pl.BlockSpec((tr, C), lambda i: (i, 0)), out_shape=jax.ShapeDtypeStruct((R, C), F32),
        name="sum8", compiler_params=_cparams(("arbitrary",)),
    )(parts)


def adam_small(w, g, m, v):
    def fn(wt, gt, mt, vt):
        return _adam_math(wt, gt, mt, vt)
    C = w.shape[1]
    return rowwise("adam_small", fn, [w, g, m, v], [], [(C, F32)] * 3, tr=128)


def _pack(arrs, rows_mult=8):
    flat = jnp.concatenate([a.reshape(-1) for a in arrs])
    n = flat.shape[0]
    per = PACK_C * rows_mult
    pad = (-n) % per
    if pad:
        flat = jnp.concatenate([flat, jnp.zeros((pad,), flat.dtype)])
    return flat.reshape(-1, PACK_C)


def _unpack(buf, shapes):
    flat = buf.reshape(-1)
    out, off = [], 0
    for s in shapes:
        n = math.prod(s)
        out.append(flat[off:off + n].reshape(s))
        off += n
    return out


def _blocked(gfull, axis):
    shp = gfull.shape
    n = shp[axis] // N_DEV
    t = gfull.reshape(shp[:axis] + (N_DEV, n) + shp[axis + 1:])
    t = jnp.moveaxis(t, axis, 0)
    return t.reshape(N_DEV, -1)


def _unblocked(gathered, shard_shape, axis):
    t = jnp.moveaxis(gathered, 0, axis)
    shp = shard_shape[:axis] + (N_DEV * shard_shape[axis],) + shard_shape[axis + 1:]
    return t.reshape(shp)


def _relu2_epi(acc):
    r = jnp.maximum(acc, 0.0)
    return acc, r * r


def _step(x3, target3, W, comm):
    bsz, S, D = x3.shape
    T = bsz * S
    x = x3.reshape(T, D)
    target = target3.reshape(T, D)
    row = lambda v: v.reshape(1, -1)
    grads = {}

    s5p = (W['ssm_a_re'][0], W['ssm_a_im'][0], W['ssm_log_dt'][0], W['ssm_b_re'][0], W['ssm_b_im'][0])
    (lam_re, lam_im, bb_re, bb_im), s5_disc_vjp = jax.vjp(s5_disc, *s5p)
    pwr, pwi, l2r, l2i = s5_tables(lam_re, lam_im, S5_SUB)
    bre, bim = _s5_blockdiag_b(bb_re).astype(BF16), _s5_blockdiag_b(bb_im).astype(BF16)
    cre, cim = _s5_blockdiag_c(W['ssm_c_re'][0]).astype(BF16), _s5_blockdiag_c(W['ssm_c_im'][0]).astype(BF16)
    dskip = W['ssm_d']

    tril = jnp.tril(jnp.ones((GMLP_CHUNK, GMLP_CHUNK), bool))
    ws = jnp.where(tril[None], W['gmlp_w_s'][0], 0.0).astype(BF16)
    hw = D // GMLP_HEADS
    sbias = jnp.repeat(W['gmlp_b_s'][0].T, hw, axis=1)

    saved = []
    def add_norm(acc, *ex):
        xn = acc + ex[0] + ex[1] if len(ex) == 3 else acc + ex[0]
        return xn, _rms(xn, ex[-1])

    for i in range(DEPTH):
        sv = {'x': x}
        nm = W['norm_mix'][i:i + 1]
        nl = W['norm_mlp'][i:i + 1]
        if i == 0:
            h, hf = rms_fwd("rms_mix0", x, nm, want_f32=True)
            res = s5_fwd(hf, bre, bim, cre, cim, pwr, pwi, l2r, l2i, dskip, bsz, gather=comm.gather_list)
            ypre, gy, xs, xr_all, xi_all = res[:5]
            Wfull, Wsh = comm.weights(res[5:])
            W = {**W, **Wsh}
            conv_w = jnp.concatenate([W['conv_w_dw'][0], jnp.zeros((1, D), F32)], axis=0)
            def s5_out(acc, xt, g):
                xn = xt + _glu(acc)
                return acc, xn, _rms(xn, g)
            z, x1, h2 = matmul("s5_glu_mm", gy, Wfull['ssm_w_glu'], mode='cb', whole_rows=True, epi=s5_out,
                               extras=[(x, 'tile'), (nl, 'row')], out_dtypes=(F32, F32, BF16), out_cols=(2 * D, D, D))
            sv.update(hf=hf, ypre=ypre, gy=gy, xs=xs, xr=xr_all, xi=xi_all, z=z)
        elif i == 1:
            def pw1_out(acc, b):
                zt = acc + b
                return zt, _glu(zt)
            z, zg = matmul("conv_pw1", h, Wfull['conv_w_pw1'], mode='cb', whole_rows=True, epi=pw1_out,
                           extras=[(W['conv_b_pw1'], 'row')], out_dtypes=(F32, F32), out_cols=(2 * D, D))
            yc = dwconv_fwd(zg, conv_w, W['conv_b_dw'], bsz)
            y2, = rowwise("conv_ln_silu", lambda t, g, b: jax.nn.silu(_ln(t, g, b)), [yc],
                          [W['conv_ln_g'], W['conv_ln_b']], [(D, BF16)])
            x1, h2 = matmul("conv_pw2", y2, Wfull['conv_w_pw2'], epi=add_norm, whole_rows=True,
                            extras=[(W['conv_b_pw2'], 'row'), (x, 'tile'), (nl, 'row')], out_dtypes=(F32, BF16))
            sv.update(h=h, z=z, zg=zg, yc=yc, y2=y2)
        elif i == 2:
            def gm_pre(acc, g, b):
                act = jax.nn.gelu(acc)
                return acc, act[:, :D], _ln(act[:, D:], g, b)
            zp, u, vln = matmul("gmlp_in", h, Wfull['gmlp_w_in'], mode='cb', whole_rows=True, epi=gm_pre,
                                extras=[(W['gmlp_ln_g'], 'row'), (W['gmlp_ln_b'], 'row')],
                                out_dtypes=(F32, F32, BF16), out_cols=(2 * D, D, D))
            gated = spatial_fwd(u, vln, ws, sbias)
            x1, h2 = matmul("gmlp_out", gated, Wfull['gmlp_w_out'], epi=add_norm, whole_rows=True,
                            extras=[(x, 'tile'), (nl, 'row')], out_dtypes=(F32, BF16))
            sv.update(h=h, zp=zp, u=u, vln=vln, gated=gated)
        else:
            qkv = qkv_matmul("attn_qkv", h, Wfull['attn_w_qkv'])
            ng = len(ATT_CONFIGS)
            outs, lses, blocks = [], [], []
            for gi, (window, dil) in enumerate(ATT_CONFIGS):
                qb, kb, vb = (qkv[j * ng + gi] for j in range(3))
                ob, lb = attn_fwd("attn_fwd%d" % gi, qb, kb, vb, dil, bsz)
                blocks.append((qb, kb, vb, lb, dil))
                outs.append(ob)
                lses.append(lb)

            def merge(o0, o1, o2, l0, l1, l2):
                m = jnp.maximum(jnp.maximum(l0, l1), l2)
                e0, e1, e2 = jnp.exp(l0 - m), jnp.exp(l1 - m), jnp.exp(l2 - m)
                inv = 1.0 / (e0 + e1 + e2)
                w0, w1, w2 = e0 * inv, e1 * inv, e2 * inv
                return w0 * o0 + w1 * o1 + w2 * o2, w0, w1, w2
            merged, w0, w1, w2 = rowwise("attn_merge", merge, outs + lses, [],
                                         [(ATT_W, BF16), (ATT_W, F32), (ATT_W, F32), (ATT_W, F32)])
            wo = Wfull['attn_w_o']
            wo_nat = wo.transpose(1, 0, 2).reshape(wo.shape[1], N_DEV * wo.shape[2])
            x1, h2 = matmul("attn_o", merged, wo_nat, epi=add_norm, whole_rows=True,
                            extras=[(x, 'tile'), (nl, 'row')], out_dtypes=(F32, BF16))
            sv.update(h=h, blocks=blocks, merged=merged, wts=(w0, w1, w2))
        a, act = matmul("mlp_in%d" % i, h2, Wfull['mlp_w_in'][i], mode='cb', epi=_relu2_epi, out_dtypes=(BF16, BF16))
        if i + 1 < DEPTH:
            x2, h = matmul("mlp_out%d" % i, act, Wfull['mlp_w_out'][i], epi=add_norm, whole_rows=True,
                           extras=[(x1, 'tile'), (W['norm_mix'][i + 1:i + 2], 'row')], out_dtypes=(F32, BF16))
        else:
            x2, = matmul("mlp_out%d" % i, act, Wfull['mlp_w_out'][i], epi=lambda acc, r: (acc + r,),
                         extras=[(x1, 'tile')])
        sv.update(x1=x1, h2=h2, a=a, act=act)
        saved.append(sv)
        x = x2

    def loss_fn(xt, tt, g):
        y, vjp = jax.vjp(_rms, xt, g)
        err = y - tt
        dxx, dg = vjp(err * (1.0 / D))
        lval = jnp.sum(jnp.sum(err * err, axis=1, keepdims=True), axis=0, keepdims=True) * (0.5 / D)
        return dxx, dxx, jnp.broadcast_to(lval, (1, 128)), dg
    dx, dxb, lacc, dnf = rowwise("loss_head", loss_fn, [x, target], [row(W['norm_final'])],
                                 [(D, F32), (D, BF16)], [((1, 128), F32), ((1, D), F32)])
    loss_local = lacc[0, 0]
    grads['norm_final'] = dnf.reshape(-1)

    g_norm_mix, g_norm_mlp = [None] * DEPTH, [None] * DEPTH
    g_mlp_in, g_mlp_out = [None] * DEPTH, [None] * DEPTH
    nl_all = [W['norm_mlp'][i:i + 1] for i in range(DEPTH)]
    nm_all = [W['norm_mix'][i:i + 1] for i in range(DEPTH)]

    def norm_bwd(xt, dres, g):
        def epi(dh, xv, dr, gv):
            _, vjp = jax.vjp(_rms, xv, gv)
            dxv, dgv = vjp(dh)
            dxv = dxv + dr
            return dxv, dxv, dgv
        return dict(epi=epi, extras=[xt, dres], params=[g], out_dtypes=(F32, BF16), acc_out=[((1, D), F32)])

    for i in reversed(range(DEPTH)):
        sv = saved[i]
        da, = matmul("mlp_out_bwd%d" % i, dxb, Wfull['mlp_w_out'][i], mode='nt',
                     epi=lambda acc, av: (acc * (2.0 * jnp.maximum(av.astype(F32), 0.0)),),
                     extras=[(sv['a'], 'tile')], out_dtypes=(BF16,))
        g_mlp_out[i] = _rows_blocked(wgrad("mlp_out_wg%d" % i, sv['act'], dxb))
        dx, dxb, dg = matmul_nt_cb("mlp_in_bwd%d" % i, da, Wfull['mlp_w_in'][i], **norm_bwd(sv['x1'], dx, nl_all[i]))
        g_mlp_in[i] = wgrad("mlp_in_wg%d" % i, sv['h2'], da, cb=True)
        g_norm_mlp[i] = dg.reshape(-1)
        xin = sv['x']
        if i == 0:
            dz, = rowwise("s5_glu_bwd", _glu_bwd, [sv['z'], dx], [], [(2 * D, BF16)])
            dgy, = matmul_nt_cb("s5_glu_mm_bwd", dz, Wfull['ssm_w_glu'])
            grads['ssm_w_glu'] = wgrad("s5_glu_wg", sv['gy'], dz, cb=True)
            grads['mlp_w_in'], grads['mlp_w_out'] = g_mlp_in, g_mlp_out
            res = s5_bwd(sv['hf'], sv['ypre'], dgy, sv['xs'], sv['xr'], sv['xi'], bre, bim, cre, cim,
                         pwr[:, ::-1], pwi[:, ::-1], l2r, l2i, dskip, bsz, chips=comm.rs_front(grads))
            du, dbr, dbi, dcr, dci, dl, dd = res[:7]
            comm.recv2 = res[7:]
            dlam_re = dl[:, 0, :].reshape(SSM_GROUPS, SSM_STATE)
            dlam_im = dl[:, 1, :].reshape(SSM_GROUPS, SSM_STATE)
            s5_cot = (dlam_re, dlam_im, _s5_blockdiag_b_inv(dbr), _s5_blockdiag_b_inv(dbi))
            grads['ssm_c_re'] = _s5_blockdiag_c_inv(dcr)[None]
            grads['ssm_c_im'] = _s5_blockdiag_c_inv(dci)[None]
            grads['ssm_d'] = dd[0:1]
            dx, dxb, dg = rms_bwd("rms_mix_bwd0", xin, du, dx, nm_all[0])
        elif i == 1:
            dy2, = matmul("conv_pw2_bwd", dxb, Wfull['conv_w_pw2'], mode='nt')
            grads['conv_w_pw2'] = _rows_blocked(wgrad("conv_pw2_wg", sv['y2'], dxb))

            def ln_silu_bwd(yt, dt, dxt, g, b):
                _, vjp = jax.vjp(lambda t, gg, bb: jax.nn.silu(_ln(t, gg, bb)), yt, g, b)
                dyc, dgg, dbb = vjp(dt)
                return dyc, dgg, dbb, _colsum(dxt)
            dyc, dlg, dlb, dbp2 = rowwise("conv_ln_silu_bwd", ln_silu_bwd, [sv['yc'], dy2, dx],
                                          [W['conv_ln_g'], W['conv_ln_b']], [(D, F32)],
                                          [((1, D), F32), ((1, D), F32), ((1, D), F32)])
            grads['conv_ln_g'], grads['conv_ln_b'], grads['conv_b_pw2'] = dlg, dlb, dbp2
            dzg, dwd, dbd = dwconv_bwd(sv['zg'], dyc, conv_w, bsz)
            grads['conv_w_dw'] = dwd[None, :CONV_WIDTH]
            grads['conv_b_dw'] = dbd

            def glu_bwd1(zt, dyt):
                dzt = _glu_bwd(zt, dyt)
                return dzt, _colsum(dzt)
            dz, dbp1 = rowwise("conv_glu_bwd", glu_bwd1, [sv['z'], dzg], [], [(2 * D, BF16)], [((1, 2 * D), F32)])
            grads['conv_b_pw1'] = dbp1
            dx, dxb, dg = matmul_nt_cb("conv_pw1_bwd", dz, Wfull['conv_w_pw1'], **norm_bwd(xin, dx, nm_all[i]))
            grads['conv_w_pw1'] = wgrad("conv_pw1_wg", sv['h'], dz, cb=True)
        elif i == 2:
            dgt, = matmul("gmlp_out_bwd", dxb, Wfull['gmlp_w_out'], mode='nt', out_dtypes=(BF16,))
            grads['gmlp_w_out'] = _rows_blocked(wgrad("gmlp_out_wg", sv['gated'], dxb))
            du, dvln, dws, dsb = spatial_bwd(sv['u'], sv['vln'], dgt, ws, sbias)
            grads['gmlp_w_s'] = dws[None]
            grads['gmlp_b_s'] = dsb.reshape(GMLP_CHUNK, GMLP_HEADS, hw).sum(-1).T[None]

            def gm_pre_bwd(zt, dut, dvt, g, b):
                _, vjp_u = jax.vjp(jax.nn.gelu, zt[:, :D])
                _, vjp_v = jax.vjp(lambda zz, gg, bb: _ln(jax.nn.gelu(zz), gg, bb), zt[:, D:], g, b)
                dz2, dgg, dbb = vjp_v(dvt)
                return jnp.concatenate([vjp_u(dut)[0], dz2], axis=1), dgg, dbb
            dzp, dlg, dlb = rowwise("gmlp_pre_bwd", gm_pre_bwd, [sv['zp'], du, dvln],
                                    [W['gmlp_ln_g'], W['gmlp_ln_b']], [(2 * D, BF16)], [((1, D), F32), ((1, D), F32)])
            grads['gmlp_ln_g'], grads['gmlp_ln_b'] = dlg, dlb
            dx, dxb, dg = matmul_nt_cb("gmlp_in_bwd", dzp, Wfull['gmlp_w_in'], **norm_bwd(xin, dx, nm_all[i]))
            grads['gmlp_w_in'] = wgrad("gmlp_in_wg", sv['h'], dzp, cb=True)
        else:
            dm, = matmul_nt_cb("attn_o_bwd", dxb, Wfull['attn_w_o'])
            grads['attn_w_o'] = wgrad("attn_o_wg", sv['merged'], dxb, cb=True)
            w0, w1, w2 = sv['wts']
            do0, do1, do2 = rowwise("attn_merge_bwd", lambda d, a, b, c: (a * d, b * d, c * d), [dm, w0, w1, w2], [],
                                    [(ATT_W, BF16)] * 3)
            dparts = [[None] * 3 for _ in range(3)]
            for gi, (dog, (qb, kb, vb, lb, dil)) in enumerate(zip((do0, do1, do2), sv['blocks'])):
                dqb, dkb, dvb = attn_bwd("attn_bwd%d" % gi, qb, kb, vb, dog, sv['merged'], lb, dil, bsz)
                for j, t in enumerate((dqb, dkb, dvb)):
                    dparts[j][gi] = t
            dqkv = [dparts[j][gi] for j in range(3) for gi in range(3)]
            dx, dxb, dg = matmul_nt_cb("attn_qkv_bwd", dqkv, Wfull['attn_w_qkv'], heads=True,
                                       **norm_bwd(xin, dx, nm_all[i]))
            grads['attn_w_qkv'] = wgrad("attn_qkv_wg", sv['h'], dqkv, cb=True, heads=True)
        g_norm_mix[i] = dg.reshape(-1)

    grads['norm_mix'] = jnp.stack(g_norm_mix)
    grads['norm_mlp'] = jnp.stack(g_norm_mlp)
    grads['mlp_w_in'] = g_mlp_in
    grads['mlp_w_out'] = g_mlp_out
    return loss_local, dx.reshape(bsz, S, D), grads, (s5_disc_vjp, s5_cot)


class _StepComm:
    def __init__(self, Wl, c_idx):
        self.Wl, self.c_idx = Wl, c_idx
        self.units = []
        for n in BIG:
            self.units += [(n, i) for i in range(DEPTH)] if Wl[n].shape[0] == DEPTH else [(n, None)]
        self.ss_names = list(SMALL_SHARDED)
        spack = _pack([Wl[n] for n in self.ss_names])
        self.gather_list = [Wl[n][0 if i is None else i].astype(BF16) for n, i in self.units] + [spack]
        self.p1 = self.recv2 = None

    @staticmethod
    def tag(n, i):
        return n if i is None else "%s%d" % (n, i)

    def weights(self, gathered):
        Wl = self.Wl
        Wfull = {}
        for (n, i), g in zip(self.units, gathered):
            w = g if BIG[n] == 2 else g.reshape(N_DEV * g.shape[1], g.shape[2])
            if i is None:
                Wfull[n] = w
            else:
                Wfull.setdefault(n, []).append(w)
        sparts = _unpack_gathered(gathered[-1], [Wl[n].shape for n in self.ss_names])
        Wsh = {n: _unblocked(p, Wl[n].shape, SMALL_SHARDED[n]) for n, p in zip(self.ss_names, sparts)}
        return Wfull, Wsh

    def rs_front(self, grads):
        pairs = [grads[n] if i is None else grads[n][i] for n, i in self.units]
        recv1 = exchange_sibling("rs_sibling", [p[1] for p in pairs])
        self.p1 = [add_sibling("add_sibling_" + self.tag(n, i), p[0], r, self.c_idx)
                   for (n, i), p, r in zip(self.units, pairs, recv1)]
        return [p[1] for p in self.p1]


def _rows_blocked(pair):
    return tuple(t.reshape(N_DEV, t.shape[0] // N_DEV, t.shape[1]) for t in pair)


def kernel(x, norm_mix, norm_mlp, norm_final, ssm_a_re, ssm_a_im, ssm_b_re, ssm_b_im, ssm_c_re, ssm_c_im, ssm_d, ssm_log_dt, ssm_w_glu, conv_w_pw1, conv_b_pw1, conv_w_dw, conv_b_dw, conv_ln_g, conv_ln_b, conv_w_pw2, conv_b_pw2, gmlp_w_in, gmlp_ln_g, gmlp_ln_b, gmlp_w_s, gmlp_b_s, gmlp_w_out, attn_w_qkv, attn_w_o, mlp_w_in, mlp_w_out, loss_target, m_norm_mix, m_norm_mlp, m_norm_final, m_ssm_a_re, m_ssm_a_im, m_ssm_b_re, m_ssm_b_im, m_ssm_c_re, m_ssm_c_im, m_ssm_d, m_ssm_log_dt, m_ssm_w_glu, m_conv_w_pw1, m_conv_b_pw1, m_conv_w_dw, m_conv_b_dw, m_conv_ln_g, m_conv_ln_b, m_conv_w_pw2, m_conv_b_pw2, m_gmlp_w_in, m_gmlp_ln_g, m_gmlp_ln_b, m_gmlp_w_s, m_gmlp_b_s, m_gmlp_w_out, m_attn_w_qkv, m_attn_w_o, m_mlp_w_in, m_mlp_w_out, v_norm_mix, v_norm_mlp, v_norm_final, v_ssm_a_re, v_ssm_a_im, v_ssm_b_re, v_ssm_b_im, v_ssm_c_re, v_ssm_c_im, v_ssm_d, v_ssm_log_dt, v_ssm_w_glu, v_conv_w_pw1, v_conv_b_pw1, v_conv_w_dw, v_conv_b_dw, v_conv_ln_g, v_conv_ln_b, v_conv_w_pw2, v_conv_b_pw2, v_gmlp_w_in, v_gmlp_ln_g, v_gmlp_ln_b, v_gmlp_w_s, v_gmlp_b_s, v_gmlp_w_out, v_attn_w_qkv, v_attn_w_o, v_mlp_w_in, v_mlp_w_out):
    args = (norm_mix, norm_mlp, norm_final, ssm_a_re, ssm_a_im, ssm_b_re, ssm_b_im, ssm_c_re, ssm_c_im, ssm_d,
            ssm_log_dt, ssm_w_glu, conv_w_pw1, conv_b_pw1, conv_w_dw, conv_b_dw, conv_ln_g, conv_ln_b, conv_w_pw2,
            conv_b_pw2, gmlp_w_in, gmlp_ln_g, gmlp_ln_b, gmlp_w_s, gmlp_b_s, gmlp_w_out, attn_w_qkv, attn_w_o,
            mlp_w_in, mlp_w_out)
    margs = (m_norm_mix, m_norm_mlp, m_norm_final, m_ssm_a_re, m_ssm_a_im, m_ssm_b_re, m_ssm_b_im, m_ssm_c_re,
             m_ssm_c_im, m_ssm_d, m_ssm_log_dt, m_ssm_w_glu, m_conv_w_pw1, m_conv_b_pw1, m_conv_w_dw, m_conv_b_dw,
             m_conv_ln_g, m_conv_ln_b, m_conv_w_pw2, m_conv_b_pw2, m_gmlp_w_in, m_gmlp_ln_g, m_gmlp_ln_b,
             m_gmlp_w_s, m_gmlp_b_s, m_gmlp_w_out, m_attn_w_qkv, m_attn_w_o, m_mlp_w_in, m_mlp_w_out)
    vargs = (v_norm_mix, v_norm_mlp, v_norm_final, v_ssm_a_re, v_ssm_a_im, v_ssm_b_re, v_ssm_b_im, v_ssm_c_re,
             v_ssm_c_im, v_ssm_d, v_ssm_log_dt, v_ssm_w_glu, v_conv_w_pw1, v_conv_b_pw1, v_conv_w_dw, v_conv_b_dw,
             v_conv_ln_g, v_conv_ln_b, v_conv_w_pw2, v_conv_b_pw2, v_gmlp_w_in, v_gmlp_ln_g, v_gmlp_ln_b,
             v_gmlp_w_s, v_gmlp_b_s, v_gmlp_w_out, v_attn_w_qkv, v_attn_w_o, v_mlp_w_in, v_mlp_w_out)
    Wl = dict(zip(WEIGHT_NAMES, args))
    Ml = dict(zip(WEIGHT_NAMES, margs))
    Vl = dict(zip(WEIGHT_NAMES, vargs))
    cx, cy, cc = _coords()
    my_idx = 4 * cx + 2 * cy + cc

    c_idx = cc.reshape(1).astype(jnp.int32)
    chip_idx = (2 * cx + cy).reshape(1).astype(jnp.int32)
    comm = _StepComm(Wl, c_idx)
    units, tag = comm.units, comm.tag
    W = {n: Wl[n] for n in SMALL if n not in SMALL_SHARDED}
    loss_local, grad_x, grads, (s5_disc_vjp, s5_cot) = _step(x, loss_target, W, comm)
    loss = lax.psum(loss_local, MESH_AXES)

    outs4 = {}
    for (n, i), p, r in zip(units, comm.p1, comm.recv2):
        w2, m2, v2 = (d[n].reshape(-1, d[n].shape[-1]) for d in (Wl, Ml, Vl))
        res = adam_big("adam_" + tag(n, i), p[0], r, w2, m2, v2, chip_idx, layer=0 if i is None else i)
        if i is None:
            outs4[n] = [t.reshape(Wl[n].shape) for t in res]
        else:
            outs4.setdefault(n, []).append(res)
    for n in BIG:
        if Wl[n].shape[0] == DEPTH:
            outs4[n] = [jnp.stack([layer[k] for layer in outs4[n]]) for k in range(4)]
    out_g = {n: outs4[n][0] for n in BIG}
    out_d = {n: outs4[n][1] for n in BIG}
    out_m = {n: outs4[n][2] for n in BIG}
    out_v = {n: outs4[n][3] for n in BIG}

    s5_lin = ['ssm_a_re', 'ssm_a_im', 'ssm_log_dt', 'ssm_b_re', 'ssm_b_im']
    direct = [n for n in SMALL if n not in s5_lin]
    def full_shape(n):
        shp = list(Wl[n].shape)
        if n in SMALL_SHARDED:
            shp[SMALL_SHARDED[n]] *= N_DEV
        return tuple(shp)
    small_parts = [grads[n].reshape(full_shape(n)) for n in direct] + list(s5_cot)
    gsum = sum8(all_gather("gather_small_grads", [_pack(small_parts)])[0])
    summed = _unpack(gsum, [p.shape for p in small_parts])
    gsmall = dict(zip(direct, summed[:len(direct)]))
    s5g = s5_disc_vjp(tuple(summed[len(direct):]))
    for n, gval in zip(s5_lin, s5g):
        gsmall[n] = gval[None]
    for n, ax in SMALL_SHARDED.items():
        gsmall[n] = lax.dynamic_slice_in_dim(gsmall[n], my_idx * Wl[n].shape[ax], Wl[n].shape[ax], axis=ax)
    sm_shapes = [Wl[n].shape for n in SMALL]
    dS, mS, vS = adam_small(_pack([Wl[n] for n in SMALL]), _pack([gsmall[n] for n in SMALL]),
                            _pack([Ml[n] for n in SMALL]), _pack([Vl[n] for n in SMALL]))
    for n, gval in zip(SMALL, [gsmall[n] for n in SMALL]):
        out_g[n] = gval.reshape(Wl[n].shape)
    out_d.update(zip(SMALL, _unpack(dS, sm_shapes)))
    out_m.update(zip(SMALL, _unpack(mS, sm_shapes)))
    out_v.update(zip(SMALL, _unpack(vS, sm_shapes)))

    return (loss, grad_x, *[out_g[n] for n in WEIGHT_NAMES], *[out_d[n] for n in WEIGHT_NAMES],
            *[out_m[n] for n in WEIGHT_NAMES], *[out_v[n] for n in WEIGHT_NAMES])


def _unpack_gathered(g, shard_shapes):
    flat = g.reshape(N_DEV, -1)
    out, off = [], 0
    for s in shard_shapes:
        n = math.prod(s)
        out.append(flat[:, off:off + n].reshape((N_DEV,) + tuple(s)))
        off += n
    return out
```

```python
import functools
import math

import jax
import jax.numpy as jnp
from jax import lax
from jax.experimental import pallas as pl
from jax.experimental.pallas import tpu as pltpu

F32 = jnp.float32
BF16 = jnp.bfloat16

D_MODEL = 1024
DEPTH = 4
EPS = 1e-6
SSM_GROUP = 16
SSM_GROUPS = 64
SSM_STATE = 64
S5_GB = 8
S5_NGB = SSM_GROUPS // S5_GB
S5_CH = S5_GB * SSM_GROUP
S5_ST = S5_GB * SSM_STATE
S5_L = 256
CONV_WIDTH = 31
CONV_PAD = 32
CONV_TS = 256
CONV_CW = 256
GMLP_CHUNK = 128
GMLP_HEADS = 4
ATT_CONFIGS = ((128, 1), (512, 4), (2048, 16))
ATT_HEADS = 8
HEAD_DIM = 64
ATT_BLK = 128
ATT_TB = 2
ATT_TB_BWD = 4
ATT_W = ATT_HEADS * HEAD_DIM
N_DEV = 8
ADAM_LR = 0.001
ADAM_B1 = 0.9
ADAM_B2 = 0.999
ADAM_EPS = 1e-08
ADAM_WD = 0.01
ADAM_STEP = 10
VMEM_LIMIT = 56 * 1024 * 1024
PACK_C = 1024
MESH_AXES = ("x", "y", "c")
MESH = pl.DeviceIdType.MESH

WEIGHT_NAMES = ['norm_mix', 'norm_mlp', 'norm_final', 'ssm_a_re', 'ssm_a_im', 'ssm_b_re', 'ssm_b_im',
                'ssm_c_re', 'ssm_c_im', 'ssm_d', 'ssm_log_dt', 'ssm_w_glu', 'conv_w_pw1', 'conv_b_pw1',
                'conv_w_dw', 'conv_b_dw', 'conv_ln_g', 'conv_ln_b', 'conv_w_pw2', 'conv_b_pw2',
                'gmlp_w_in', 'gmlp_ln_g', 'gmlp_ln_b', 'gmlp_w_s', 'gmlp_b_s', 'gmlp_w_out',
                'attn_w_qkv', 'attn_w_o', 'mlp_w_in', 'mlp_w_out']
BIG = {'ssm_w_glu': 2, 'conv_w_pw1': 2, 'conv_w_pw2': 1, 'gmlp_w_in': 2, 'gmlp_w_out': 1,
       'attn_w_qkv': 2, 'attn_w_o': 2, 'mlp_w_in': 2, 'mlp_w_out': 1}
SMALL_SHARDED = {'conv_b_pw1': 1, 'conv_w_dw': 2, 'conv_b_dw': 1, 'conv_ln_g': 1, 'conv_ln_b': 1,
                 'conv_b_pw2': 1, 'gmlp_ln_g': 1, 'gmlp_ln_b': 1}
SMALL = [n for n in WEIGHT_NAMES if n not in BIG]


def _cparams(sem=None):
    return pltpu.CompilerParams(dimension_semantics=sem, vmem_limit_bytes=VMEM_LIMIT)


def _dot(a, b):
    return jnp.dot(a, b, preferred_element_type=F32)


def _dot_nt(a, b):
    return lax.dot_general(a, b, (((1,), (1,)), ((), ())), preferred_element_type=F32)


def _dot_tn(a, b):
    return lax.dot_general(a, b, (((0,), (0,)), ((), ())), preferred_element_type=F32)


ROW_TILE_BYTES = 10 << 20


def _rows_for(T, row_bytes, cap=1024):
    tr = min(cap, T)
    while tr > 8 and (T % tr or tr * row_bytes > ROW_TILE_BYTES):
        tr //= 2
    assert T % tr == 0 and tr % 8 == 0
    return tr


def rowwise(name, fn, rows, params, row_out, acc_out=(), tr=None):
    T = rows[0].shape[0]
    row_bytes = (sum(r.shape[1] * r.dtype.itemsize for r in rows)
                 + sum(c * jnp.dtype(dt).itemsize for c, dt in row_out))
    tr = _rows_for(T, row_bytes, cap=tr or 1024)
    nr, npar, nro = len(rows), len(params), len(row_out)

    def body(*refs):
        ins = [r[...] for r in refs[:nr + npar]]
        outs = refs[nr + npar:]
        res = fn(*ins)
        if not isinstance(res, (tuple, list)):
            res = (res,)
        for k in range(nro):
            outs[k][...] = res[k].astype(outs[k].dtype)
        if acc_out:
            @pl.when(pl.program_id(0) == 0)
            def _():
                for k in range(nro, len(outs)):
                    outs[k][...] = jnp.zeros_like(outs[k])
            for k in range(nro, len(outs)):
                outs[k][...] += res[k].astype(outs[k].dtype)

    in_specs = [pl.BlockSpec((tr, r.shape[1]), lambda i: (i, 0)) for r in rows]
    in_specs += [pl.BlockSpec(p.shape, lambda i, nd=p.ndim: (0,) * nd) for p in params]
    out_shape = [jax.ShapeDtypeStruct((T, c), dt) for c, dt in row_out]
    out_specs = [pl.BlockSpec((tr, c), lambda i: (i, 0)) for c, dt in row_out]
    out_shape += [jax.ShapeDtypeStruct(s, dt) for s, dt in acc_out]
    out_specs += [pl.BlockSpec(s, lambda i, nd=len(s): (0,) * nd) for s, dt in acc_out]
    res = pl.pallas_call(body, grid=(T // tr,), in_specs=in_specs, out_specs=out_specs, out_shape=out_shape,
                         name=name, compiler_params=_cparams(("arbitrary",)))(*rows, *params)
    return res


def _tile_m(M, K):
    tm = 2048
    while tm > 256 and tm * K * 2 > (4 << 20):
        tm //= 2
    return min(tm, M)


def matmul(name, a, b, *, mode='nn', epi=None, extras=(), out_dtypes=(F32,), out_cols=None, whole_rows=False):
    M, K = a.shape
    if mode == 'cb':
        nblk, _, tn = b.shape
        N = nblk * tn
    else:
        N = b.shape[0] if mode == 'nt' else b.shape[1]
        tn = min(512, N)
    out_cols = list(out_cols) if out_cols is not None else [N] * len(out_dtypes)
    row_bytes = (K * 2 + sum(c * jnp.dtype(dt).itemsize for c, dt in zip(out_cols, out_dtypes))
                 + sum(arr.shape[1] * arr.dtype.itemsize for arr, kind in extras if kind == 'tile'))
    tm = _rows_for(M, row_bytes)
    assert N % tn == 0, (M, N, tm, tn)
    nex = len(extras)

    def body(a_ref, b_ref, *rest):
        ex_refs, outs = rest[:nex], rest[nex:]
        av = a_ref[...]

        def product(c):
            cs = slice(c * tn, (c + 1) * tn)
            if mode == 'cb':
                return _dot(av, b_ref[c])
            return _dot_nt(av, b_ref[cs, :]) if mode == 'nt' else _dot(av, b_ref[:, cs])

        if whole_rows:
            parts = [product(c) for c in range(N // tn)]
            acc = parts[0] if len(parts) == 1 else jnp.concatenate(parts, axis=1)
            res = epi(acc, *[e[...] for e in ex_refs])
            for o, r in zip(outs, res):
                o[...] = r.astype(o.dtype)
        else:
            for c in range(N // tn):
                cs = slice(c * tn, (c + 1) * tn)
                acc = product(c)
                res = epi(acc, *[e[:, cs] for e in ex_refs]) if epi is not None else (acc,)
                for o, r in zip(outs, res):
                    o[:, cs] = r.astype(o.dtype)

    in_specs = [pl.BlockSpec((tm, K), lambda i: (i, 0)), pl.BlockSpec(b.shape, lambda i, nd=b.ndim: (0,) * nd)]
    for arr, kind in extras:
        in_specs.append(pl.BlockSpec((tm, arr.shape[1]), lambda i: (i, 0)) if kind == 'tile'
                        else pl.BlockSpec((1, arr.shape[1]), lambda i: (0, 0)))
    out_shape = [jax.ShapeDtypeStruct((M, c), dt) for c, dt in zip(out_cols, out_dtypes)]
    out_specs = [pl.BlockSpec((tm, c), lambda i: (i, 0)) for c in out_cols]
    return pl.pallas_call(body, grid=(M // tm,), in_specs=in_specs, out_specs=out_specs,
                          out_shape=out_shape, name=name,
                          compiler_params=_cparams(("arbitrary",)))(a, b, *[e[0] for e in extras])


def _gather_heads(refs, j, scr):
    for r in range(SLOTS_PER_DEV):
        k, hh = divmod(j * SLOTS_PER_DEV + r, ATT_HEADS)
        scr[:, r * HEAD_DIM:(r + 1) * HEAD_DIM] = refs[k][:, hh * HEAD_DIM:(hh + 1) * HEAD_DIM]
    return scr[...]


def matmul_nt_cb(name, a, b, *, heads=False, epi=None, extras=(), params=(), out_dtypes=(F32,), acc_out=(),
                 sibling=()):
    nblk, K, n = b.shape
    a_list = list(a) if heads else [a]
    na = len(a_list)
    M = a_list[0].shape[0]
    tm = _tile_m(M, nblk * n)
    assert M % tm == 0
    nex, npar, nro, ns = len(extras), len(params), len(out_dtypes), len(sibling)
    nsteps = M // tm

    def body(*refs):
        a_refs, b_ref, rest = refs[:na], refs[na], refs[na + 1:]
        if ns:
            rest, sems = rest[:-2], rest[-2:]
        if heads:
            rest, scr = rest[:-2], rest[-2:]
        ex, sib_in, rest = rest[:nex + npar], rest[nex + npar:nex + npar + ns], rest[nex + npar + ns:]
        outs, sib_out = rest[:nro + len(acc_out)], rest[nro + len(acc_out):]
        if ns:
            sib_start, sib_finish = _sibling_phases(sib_in, sib_out, *sems)
            pl.when(pl.program_id(0) == 0)(sib_start)
        acc = None
        for j in range(nblk):
            aj = _gather_heads(a_refs, j, scr[j % 2]) if heads else a_refs[0][:, j * n:(j + 1) * n]
            part = _dot_nt(aj, b_ref[j])
            acc = part if acc is None else acc + part
        res = epi(acc, *[e[...] for e in ex]) if epi is not None else (acc,)
        for o, r in zip(outs[:nro], res[:nro]):
            o[...] = r.astype(o.dtype)
        if acc_out:
            @pl.when(pl.program_id(0) == 0)
            def _():
                for o in outs[nro:]:
                    o[...] = jnp.zeros_like(o)
            for o, r in zip(outs[nro:], res[nro:]):
                o[...] += r.astype(o.dtype)
        if ns:
            pl.when(pl.program_id(0) == nsteps - 1)(sib_finish)

    a_specs = [pl.BlockSpec((tm, t.shape[1]), lambda i: (i, 0)) for t in a_list]
    row = pl.BlockSpec((tm, K), lambda i: (i, 0))
    const = lambda shp: pl.BlockSpec(shp, lambda i, nd=len(shp): (0,) * nd)
    anyspec = pl.BlockSpec(memory_space=pl.ANY)
    return pl.pallas_call(
        body, grid=(nsteps,),
        in_specs=a_specs + [pl.BlockSpec((nblk, K, n), lambda i: (0, 0, 0))] + [row] * nex
        + [const(p.shape) for p in params] + [anyspec] * ns,
        out_specs=[row] * nro + [const(s) for s, dt in acc_out] + [anyspec] * ns,
        out_shape=[jax.ShapeDtypeStruct((M, K), dt) for dt in out_dtypes]
        + [jax.ShapeDtypeStruct(s, dt) for s, dt in acc_out] + _sibling_out_shapes(sibling),
        scratch_shapes=([pltpu.VMEM((tm, n), BF16)] * 2 if heads else []) + (_sibling_sems(ns) if ns else []),
        name=name, compiler_params=_cparams(("arbitrary",)))(*a_list, b, *extras, *params, *sibling)


def wgrad(name, a, g, *, cb=False, heads=False):
    M, K = a.shape
    g_list = list(g) if heads else [g]
    tm, tk = min(M, 512 if heads else 1024), min(K, 512 if heads else 1024)
    if cb:
        n = SLOTS_PER_DEV * HEAD_DIM if heads else g.shape[1] // N_DEV
        nj = N_DEV
        while nj > 1 and nj * tk * n * 6 > (14 << 20):
            nj //= 2
        assert nj == N_DEV or not heads
        grid = (K // tk, N_DEV // nj, M // tm)
        g_specs = ([pl.BlockSpec((tm, t.shape[1]), lambda k, j, m: (m, 0)) for t in g_list] if heads
                   else [pl.BlockSpec((tm, nj * n), lambda k, j, m: (m, j))])
        o_spec = pl.BlockSpec((nj, tk, n), lambda k, j, m: (j, k, 0))
        o_shape = (N_DEV, K, n)
    else:
        N = g.shape[1]
        tn = min(N, 1024)
        nj = 1
        grid = (K // tk, N // tn, M // tm)
        g_specs = [pl.BlockSpec((tm, tn), lambda k, j, m: (m, j))]
        o_spec = pl.BlockSpec((tk, tn), lambda k, j, m: (k, j))
        o_shape = (K, N)
    nm = M // tm
    ng = len(g_list)

    def body(a_ref, *rest):
        g_refs, o_ref, o16_ref, scr = rest[:ng], rest[ng], rest[ng + 1], rest[ng + 2:]
        m = pl.program_id(2)

        @pl.when(m == 0)
        def _():
            o_ref[...] = jnp.zeros_like(o_ref)
        at = a_ref[...].T
        if cb:
            for jj in range(nj):
                gj = (_gather_heads(g_refs, jj, scr[jj % 2]) if heads
                      else g_refs[0][:, jj * n:(jj + 1) * n])
                o_ref[jj] += _dot(at, gj)
        else:
            o_ref[...] += _dot(at, g_refs[0][...])

        @pl.when(m == nm - 1)
        def _():
            o16_ref[...] = o_ref[...].astype(BF16)

    return pl.pallas_call(
        body, grid=grid, in_specs=[pl.BlockSpec((tm, tk), lambda k, j, m: (m, k))] + g_specs,
        out_specs=[o_spec, o_spec],
        out_shape=[jax.ShapeDtypeStruct(o_shape, F32), jax.ShapeDtypeStruct(o_shape, BF16)],
        scratch_shapes=[pltpu.VMEM((tm, SLOTS_PER_DEV * HEAD_DIM), BF16)] * 2 if heads else [],
        name=name, compiler_params=_cparams(("arbitrary", "arbitrary", "arbitrary")))(a, *g_list)


def _rms(x, g):
    x = x.astype(F32)
    return x * lax.rsqrt(jnp.mean(x * x, axis=-1, keepdims=True) + EPS) * g


def _ln(x, g, b):
    mu = jnp.mean(x, axis=-1, keepdims=True)
    var = jnp.mean(jnp.square(x - mu), axis=-1, keepdims=True)
    return (x - mu) * lax.rsqrt(var + EPS) * g + b


def _glu(z):
    d = z.shape[1] // 2
    return z[:, :d] * jax.nn.sigmoid(z[:, d:])


def _glu_bwd(z, dy):
    d = z.shape[1] // 2
    a, s = z[:, :d], jax.nn.sigmoid(z[:, d:])
    return jnp.concatenate([dy * s, dy * a * s * (1.0 - s)], axis=1)


def _colsum(v):
    return jnp.sum(v.astype(F32), axis=0, keepdims=True)


def rms_fwd(name, x, g, want_f32=False):
    def fn(xt, gt):
        h = _rms(xt, gt)
        return (h, h) if want_f32 else (h,)
    D = x.shape[1]
    outs = [(D, BF16)] + ([(D, F32)] if want_f32 else [])
    return rowwise(name, fn, [x], [g], outs)


def rms_bwd(name, x, dh, dres, g):
    def fn(xt, dht, drt, gt):
        _, vjp = jax.vjp(_rms, xt, gt)
        dx, dg = vjp(dht.astype(F32))
        dx = dx + drt
        return dx, dx, dg
    D = x.shape[1]
    return rowwise(name, fn, [x, dh, dres], [g], [(D, F32), (D, BF16)], [((1, D), F32)])


def s5_disc(a_re, a_im, log_dt, b_re, b_im):
    dt = jnp.exp(log_dt)[:, None]
    er = jnp.exp(a_re * dt)
    lam_re = er * jnp.cos(a_im * dt)
    lam_im = er * jnp.sin(a_im * dt)
    nr, ni = lam_re - 1.0, lam_im
    den = a_re * a_re + a_im * a_im
    f_re = (nr * a_re + ni * a_im) / den
    f_im = (ni * a_re - nr * a_im) / den
    bb_re = f_re[..., None] * b_re - f_im[..., None] * b_im
    bb_im = f_re[..., None] * b_im + f_im[..., None] * b_re
    return lam_re, lam_im, bb_re, bb_im


def _s5_blockdiag_b(bb):
    t = bb.reshape(S5_NGB, S5_GB, SSM_STATE, SSM_GROUP).transpose(0, 1, 3, 2)
    eye = jnp.eye(S5_GB, dtype=bb.dtype)
    return jnp.einsum('bgpn,gh->bgphn', t, eye).reshape(S5_NGB, S5_CH, S5_ST)


def _s5_blockdiag_b_inv(x):
    t = x.reshape(S5_NGB, S5_GB, SSM_GROUP, S5_GB, SSM_STATE)
    eye = jnp.eye(S5_GB, dtype=x.dtype)
    d = jnp.einsum('bgphn,gh->bgpn', t, eye)
    return d.transpose(0, 1, 3, 2).reshape(SSM_GROUPS, SSM_STATE, SSM_GROUP)


def _s5_blockdiag_c(c):
    t = c.reshape(S5_NGB, S5_GB, SSM_GROUP, SSM_STATE).transpose(0, 1, 3, 2)
    eye = jnp.eye(S5_GB, dtype=c.dtype)
    return jnp.einsum('bgnp,gh->bgnhp', t, eye).reshape(S5_NGB, S5_ST, S5_CH)


def _s5_blockdiag_c_inv(x):
    t = x.reshape(S5_NGB, S5_GB, SSM_STATE, S5_GB, SSM_GROUP)
    eye = jnp.eye(S5_GB, dtype=x.dtype)
    d = jnp.einsum('bgnhp,gh->bgnp', t, eye)
    return d.transpose(0, 1, 3, 2).reshape(SSM_GROUPS, SSM_GROUP, SSM_STATE)


def s5_tables(lam_re, lam_im, L):
    pr, pi = lam_re.reshape(1, -1), lam_im.reshape(1, -1)
    n = 1
    while n < L:
        lr, li = pr[n - 1:n], pi[n - 1:n]
        pr, pi = (jnp.concatenate([pr, pr * lr - pi * li], 0), jnp.concatenate([pi, pr * li + pi * lr], 0))
        n *= 2
    nk = int(math.log2(L))
    idx = [2 ** k - 1 for k in range(nk)] + [0] * (8 - nk)

    def blk(t):
        return t.reshape(t.shape[0], S5_NGB, S5_ST).transpose(1, 0, 2)

    def rows(t):
        return jnp.concatenate([t[j:j + 1] for j in idx], axis=0)
    return blk(pr), blk(pi), blk(rows(pr)), blk(rows(pi))


S5_SUB = 8


def _scan_tiles(br, bi, a2r, a2i, reverse):
    L = br.shape[0]
    sub = lax.broadcasted_iota(jnp.int32, br.shape, 0) & (S5_SUB - 1)
    xr, xi = br, bi
    for k in range(3):
        s = 1 << k
        ar, ai = a2r[k:k + 1, :], a2i[k:k + 1, :]
        if reverse:
            sr, si = pltpu.roll(xr, L - s, 0), pltpu.roll(xi, L - s, 0)
            m = sub < S5_SUB - s
        else:
            sr, si = pltpu.roll(xr, s, 0), pltpu.roll(xi, s, 0)
            m = sub >= s
        sr, si = jnp.where(m, sr, 0.0), jnp.where(m, si, 0.0)
        xr, xi = xr + ar * sr - ai * si, xi + ar * si + ai * sr
    return xr, xi


def _scan_chain(xr, xi, pr, pi, cr, ci, out_r, out_i, reverse):
    ntile = xr.shape[0] // S5_SUB
    for g in (reversed(range(ntile)) if reverse else range(ntile)):
        rs = slice(g * S5_SUB, (g + 1) * S5_SUB)
        if reverse:
            nr = xr[rs] + pr * cr + pi * ci
            ni = xi[rs] + pr * ci - pi * cr
            cr, ci = nr[0:1], ni[0:1]
        else:
            nr = xr[rs] + pr * cr - pi * ci
            ni = xi[rs] + pr * ci + pi * cr
            cr, ci = nr[S5_SUB - 1:S5_SUB], ni[S5_SUB - 1:S5_SUB]
        out_r[rs, :] = nr
        out_i[rs, :] = ni
    return cr, ci


def _grid_step(shape):
    s = 0
    for ax, n in enumerate(shape):
        s = s * n + pl.program_id(ax)
    return s


def s5_fwd(h, bre, bim, cre, cim, pwr, pwi, l2r, l2i, dskip, bsz, gather=()):
    T, D = h.shape
    L = S5_L
    S = T // bsz
    NC = S // L
    ng = len(gather)
    grid = (S5_NGB, bsz, NC)
    nsteps = S5_NGB * bsz * NC
    fwd_step = nsteps - max(1, nsteps // 32)

    def body(*refs):
        (h_ref, bre_ref, bim_ref, cre_ref, cim_ref, pwr_ref, pwi_ref, l2r_ref, l2i_ref, d_ref) = refs[:10]
        x_refs = refs[10:10 + ng]
        y_ref, gy_ref, xs_ref, xr_s, xi_s = refs[10 + ng:15 + ng]
        g_refs = refs[15 + ng:15 + 2 * ng]
        car_r, car_i = refs[15 + 2 * ng:17 + 2 * ng]
        if ng:
            start, forward, finish = _gather_phases(x_refs, g_refs, *refs[17 + 2 * ng:])
            step = _grid_step(grid)
            pl.when(step == 0)(start)
            pl.when(step == fwd_step)(forward)

        @pl.when(pl.program_id(2) == 0)
        def _():
            car_r[...] = jnp.zeros_like(car_r)
            car_i[...] = jnp.zeros_like(car_i)
        u = h_ref[...]
        ub = u.astype(BF16)
        cr, ci = car_r[0:1, :], car_i[0:1, :]
        xs_ref[...] = jnp.zeros_like(xs_ref)
        xs_ref[0:1, :] = cr
        xs_ref[1:2, :] = ci
        xr, xi = _scan_tiles(_dot(ub, bre_ref[...]), _dot(ub, bim_ref[...]), l2r_ref[...], l2i_ref[...], False)
        cr, ci = _scan_chain(xr, xi, pwr_ref[...], pwi_ref[...], cr, ci, xr_s, xi_s, False)
        car_r[...] = jnp.broadcast_to(cr, car_r.shape)
        car_i[...] = jnp.broadcast_to(ci, car_i.shape)
        y = (_dot(xr_s[...].astype(BF16), cre_ref[...]) - _dot(xi_s[...].astype(BF16), cim_ref[...])
             + d_ref[...] * u)
        y_ref[...] = y
        gy_ref[...] = jax.nn.gelu(y).astype(BF16)
        if ng:
            pl.when(step == nsteps - 1)(finish)

    tok = lambda g, b, c: (b * NC + c, g)
    par = lambda g, b, c: (g, 0, 0)
    anyspec = pl.BlockSpec(memory_space=pl.ANY)
    return pl.pallas_call(
        body, grid=grid,
        in_specs=[pl.BlockSpec((L, S5_CH), tok),
                  pl.BlockSpec((None, S5_CH, S5_ST), par), pl.BlockSpec((None, S5_CH, S5_ST), par),
                  pl.BlockSpec((None, S5_ST, S5_CH), par), pl.BlockSpec((None, S5_ST, S5_CH), par),
                  pl.BlockSpec((None, 8, S5_ST), par), pl.BlockSpec((None, 8, S5_ST), par),
                  pl.BlockSpec((None, 8, S5_ST), par), pl.BlockSpec((None, 8, S5_ST), par),
                  pl.BlockSpec((1, S5_CH), lambda g, b, c: (0, g))] + [anyspec] * ng,
        out_specs=[pl.BlockSpec((L, S5_CH), tok), pl.BlockSpec((L, S5_CH), tok),
                   pl.BlockSpec((None, 8, S5_ST), lambda g, b, c: (b * NC + c, 0, g)),
                   pl.BlockSpec((L, S5_ST), tok), pl.BlockSpec((L, S5_ST), tok)] + [anyspec] * ng,
        out_shape=[jax.ShapeDtypeStruct((T, D), F32), jax.ShapeDtypeStruct((T, D), BF16),
                   jax.ShapeDtypeStruct((bsz * NC, 8, S5_NGB * S5_ST), F32),
                   jax.ShapeDtypeStruct((T, S5_NGB * S5_ST), F32), jax.ShapeDtypeStruct((T, S5_NGB * S5_ST), F32)]
        + _gather_out_shapes(gather),
        scratch_shapes=[pltpu.VMEM((8, S5_ST), F32), pltpu.VMEM((8, S5_ST), F32)]
        + (_gather_sems(ng) if ng else []),
        name="s5_fwd", compiler_params=_cparams(("arbitrary", "arbitrary", "arbitrary")),
    )(h, bre, bim, cre, cim, pwr, pwi, l2r, l2i, dskip, *gather)


def s5_bwd(h, ypre, dgy, xs, xr, xi, bre, bim, cre, cim, pwr_rev, pwi_rev, l2r, l2i, dskip, bsz, chips=()):
    T, D = h.shape
    L = S5_L
    S = T // bsz
    NC = S // L
    nc = len(chips)
    grid = (S5_NGB, bsz, NC)
    nsteps = S5_NGB * bsz * NC

    def body(*refs):
        (h_ref, yp_ref, dg_ref, xs_ref, xr_ref, xi_ref, bre_ref, bim_ref, cre_ref, cim_ref, qr_ref, qi_ref,
         l2r_ref, l2i_ref, d_ref) = refs[:15]
        p_refs = refs[15:15 + nc]
        du_ref, dbr_ref, dbi_ref, dcr_ref, dci_ref, dl_ref, dd_ref = refs[15 + nc:22 + nc]
        r_refs = refs[22 + nc:22 + 2 * nc]
        car_r, car_i, dr_s, di_s = refs[22 + 2 * nc:26 + 2 * nc]
        if nc:
            start, finish = _chips_phases(p_refs, r_refs, *refs[26 + 2 * nc:])
            step = _grid_step(grid)
            pl.when(step == 0)(start)
        first = (pl.program_id(1) == 0) & (pl.program_id(2) == 0)

        @pl.when(first)
        def _():
            for r in (dbr_ref, dbi_ref, dcr_ref, dci_ref, dl_ref, dd_ref):
                r[...] = jnp.zeros_like(r)

        @pl.when(pl.program_id(2) == 0)
        def _():
            car_r[...] = jnp.zeros_like(car_r)
            car_i[...] = jnp.zeros_like(car_i)

        u = h_ref[...]
        ub = u.astype(BF16)
        dyv = jax.vjp(jax.nn.gelu, yp_ref[...])[1](dg_ref[...])[0]
        dyb = dyv.astype(BF16)
        l2r_v, l2i_v = l2r_ref[...], l2i_ref[...]
        x0r, x0i = xs_ref[0:1, :], xs_ref[1:2, :]
        xr, xi = xr_ref[...], xi_ref[...]
        gr = _dot_nt(dyb, cre_ref[...])
        gi = -_dot_nt(dyb, cim_ref[...])
        dr, di = _scan_tiles(gr, gi, l2r_v, -l2i_v, True)
        cr, ci = _scan_chain(dr, di, qr_ref[...], qi_ref[...], car_r[0:1, :], car_i[0:1, :], dr_s, di_s, True)
        dr, di = dr_s[...], di_s[...]
        car_r[...] = jnp.broadcast_to(cr, car_r.shape)
        car_i[...] = jnp.broadcast_to(ci, car_i.shape)
        row = lax.broadcasted_iota(jnp.int32, xr.shape, 0)
        xpr = jnp.where(row >= 1, pltpu.roll(xr, 1, 0), x0r)
        xpi = jnp.where(row >= 1, pltpu.roll(xi, 1, 0), x0i)
        dl_ref[0:1, :] += _colsum(dr * xpr + di * xpi)
        dl_ref[1:2, :] += _colsum(di * xpr - dr * xpi)
        drb, dib = dr.astype(BF16), di.astype(BF16)
        dcr_ref[...] += _dot_tn(xr.astype(BF16), dyb)
        dci_ref[...] -= _dot_tn(xi.astype(BF16), dyb)
        dbr_ref[...] += _dot_tn(ub, drb)
        dbi_ref[...] += _dot_tn(ub, dib)
        du_ref[...] = _dot_nt(drb, bre_ref[...]) + _dot_nt(dib, bim_ref[...]) + d_ref[...] * dyv
        dd_ref[0:1, :] += _colsum(dyv * u)
        if nc:
            pl.when(step == nsteps - 1)(finish)

    tok = lambda g, b, c: (b * NC + (NC - 1 - c), g)
    par = lambda g, b, c: (g, 0, 0)
    anyspec = pl.BlockSpec(memory_space=pl.ANY)
    return pl.pallas_call(
        body, grid=grid,
        in_specs=[pl.BlockSpec((L, S5_CH), tok), pl.BlockSpec((L, S5_CH), tok), pl.BlockSpec((L, S5_CH), tok),
                  pl.BlockSpec((None, 8, S5_ST), lambda g, b, c: (b * NC + (NC - 1 - c), 0, g)),
                  pl.BlockSpec((L, S5_ST), tok), pl.BlockSpec((L, S5_ST), tok),
                  pl.BlockSpec((None, S5_CH, S5_ST), par), pl.BlockSpec((None, S5_CH, S5_ST), par),
                  pl.BlockSpec((None, S5_ST, S5_CH), par), pl.BlockSpec((None, S5_ST, S5_CH), par),
                  pl.BlockSpec((None, 8, S5_ST), par), pl.BlockSpec((None, 8, S5_ST), par),
                  pl.BlockSpec((None, 8, S5_ST), par), pl.BlockSpec((None, 8, S5_ST), par),
                  pl.BlockSpec((1, S5_CH), lambda g, b, c: (0, g))] + [anyspec] * nc,
        out_specs=[pl.BlockSpec((L, S5_CH), tok),
                   pl.BlockSpec((None, S5_CH, S5_ST), par), pl.BlockSpec((None, S5_CH, S5_ST), par),
                   pl.BlockSpec((None, S5_ST, S5_CH), par), pl.BlockSpec((None, S5_ST, S5_CH), par),
                   pl.BlockSpec((None, 8, S5_ST), par),
                   pl.BlockSpec((8, S5_CH), lambda g, b, c: (0, g))] + [anyspec] * nc,
        out_shape=[jax.ShapeDtypeStruct((T, D), F32),
                   jax.ShapeDtypeStruct((S5_NGB, S5_CH, S5_ST), F32), jax.ShapeDtypeStruct((S5_NGB, S5_CH, S5_ST), F32),
                   jax.ShapeDtypeStruct((S5_NGB, S5_ST, S5_CH), F32), jax.ShapeDtypeStruct((S5_NGB, S5_ST, S5_CH), F32),
                   jax.ShapeDtypeStruct((S5_NGB, 8, S5_ST), F32), jax.ShapeDtypeStruct((8, D), F32)]
        + _chips_out_shapes(chips),
        scratch_shapes=[pltpu.VMEM((8, S5_ST), F32), pltpu.VMEM((8, S5_ST), F32)]
        + [pltpu.VMEM((L, S5_ST), F32)] * 2 + (_chips_sems(nc) if nc else []),
        name="s5_bwd", compiler_params=_cparams(("arbitrary", "arbitrary", "arbitrary")),
    )(h, ypre, dgy, xs, xr, xi, bre, bim, cre, cim, pwr_rev, pwi_rev, l2r, l2i, dskip, *chips)


def _shift_rows(win, off, n):
    if off == 0:
        return win[:n]
    return pltpu.roll(win, win.shape[0] - off, 0)[:n]


def dwconv_fwd(z, w, b, bsz):
    T, D = z.shape
    S = T // bsz
    TS, CW, PAD = CONV_TS, CONV_CW, CONV_PAD

    def body(z_ref, w_ref, b_ref, y_ref, zp):
        zp[0:PAD, :] = jnp.zeros((PAD, CW), F32)
        zp[PAD:, :] = z_ref[...]
        wv, bv = w_ref[...], b_ref[...]

        def step(t, carry):
            base = pl.multiple_of(t * TS, TS)
            win = zp[pl.ds(base, TS + PAD), :]
            acc = jnp.zeros((TS, CW), F32) + bv
            for k in range(CONV_WIDTH):
                acc = acc + wv[k:k + 1, :] * _shift_rows(win, PAD - (CONV_WIDTH - 1) + k, TS)
            y_ref[pl.ds(base, TS), :] = acc
            return carry
        lax.fori_loop(0, S // TS, step, 0)

    return pl.pallas_call(
        body, grid=(D // CW, bsz),
        in_specs=[pl.BlockSpec((S, CW), lambda c, bb: (bb, c)), pl.BlockSpec((32, CW), lambda c, bb: (0, c)),
                  pl.BlockSpec((1, CW), lambda c, bb: (0, c))],
        out_specs=pl.BlockSpec((S, CW), lambda c, bb: (bb, c)),
        out_shape=jax.ShapeDtypeStruct((T, D), F32),
        scratch_shapes=[pltpu.VMEM((S + PAD, CW), F32)],
        name="dwconv_fwd", compiler_params=_cparams(("arbitrary", "arbitrary")),
    )(z, w, b)


def dwconv_bwd(z, dy, w, bsz):
    T, D = z.shape
    S = T // bsz
    TS, CW, PAD = CONV_TS, CONV_CW, CONV_PAD

    def body(z_ref, dy_ref, w_ref, dz_ref, dw_ref, db_ref, zp, dyp):
        @pl.when(pl.program_id(1) == 0)
        def _():
            dw_ref[...] = jnp.zeros_like(dw_ref)
            db_ref[...] = jnp.zeros_like(db_ref)
        zp[0:PAD, :] = jnp.zeros((PAD, CW), F32)
        zp[PAD:, :] = z_ref[...]
        dyp[0:S, :] = dy_ref[...]
        dyp[S:, :] = jnp.zeros((PAD, CW), F32)
        wv = w_ref[...]

        def step(t, carry):
            base = pl.multiple_of(t * TS, TS)
            zwin = zp[pl.ds(base, TS + PAD), :]
            dwin = dyp[pl.ds(base, TS + PAD), :]
            dyt = dwin[:TS]
            acc = jnp.zeros((TS, CW), F32)
            for j in range(CONV_WIDTH):
                k = CONV_WIDTH - 1 - j
                acc = acc + wv[k:k + 1, :] * _shift_rows(dwin, j, TS)
            dz_ref[pl.ds(base, TS), :] = acc
            for k in range(CONV_WIDTH):
                prod = dyt * _shift_rows(zwin, PAD - (CONV_WIDTH - 1) + k, TS)
                dw_ref[8 * k:8 * k + 8, :] += jnp.sum(prod.reshape(TS // 8, 8, CW), axis=0)
            db_ref[...] += jnp.sum(dyt.reshape(TS // 8, 8, CW), axis=0)
            return carry
        lax.fori_loop(0, S // TS, step, 0)

    dz, dw, db = pl.pallas_call(
        body, grid=(D // CW, bsz),
        in_specs=[pl.BlockSpec((S, CW), lambda c, bb: (bb, c)), pl.BlockSpec((S, CW), lambda c, bb: (bb, c)),
                  pl.BlockSpec((32, CW), lambda c, bb: (0, c))],
        out_specs=[pl.BlockSpec((S, CW), lambda c, bb: (bb, c)), pl.BlockSpec((8 * 32, CW), lambda c, bb: (0, c)),
                   pl.BlockSpec((8, CW), lambda c, bb: (0, c))],
        out_shape=[jax.ShapeDtypeStruct((T, D), F32), jax.ShapeDtypeStruct((8 * 32, D), F32),
                   jax.ShapeDtypeStruct((8, D), F32)],
        scratch_shapes=[pltpu.VMEM((S + PAD, CW), F32), pltpu.VMEM((S + PAD, CW), F32)],
        name="dwconv_bwd", compiler_params=_cparams(("arbitrary", "arbitrary")),
    )(z, dy, w)
    return dz, dw.reshape(32, 8, D).sum(axis=1), db.sum(axis=0, keepdims=True)


def spatial_fwd(u, vln, ws, bias):
    T, E = u.shape
    C, H = GMLP_CHUNK, GMLP_HEADS
    hw = E // H

    def body(u_ref, v_ref, ws_ref, b_ref, o_ref):
        for hh in range(H):
            sl = slice(hh * hw, (hh + 1) * hw)
            vp = _dot(ws_ref[hh], v_ref[:, sl]) + b_ref[:, sl]
            o_ref[:, sl] = (u_ref[:, sl] * vp).astype(o_ref.dtype)

    return pl.pallas_call(
        body, grid=(T // C,),
        in_specs=[pl.BlockSpec((C, E), lambda i: (i, 0)), pl.BlockSpec((C, E), lambda i: (i, 0)),
                  pl.BlockSpec((H, C, C), lambda i: (0, 0, 0)), pl.BlockSpec((C, E), lambda i: (0, 0))],
        out_specs=pl.BlockSpec((C, E), lambda i: (i, 0)),
        out_shape=jax.ShapeDtypeStruct((T, E), BF16),
        name="spatial_fwd", compiler_params=_cparams(("arbitrary",)),
    )(u, vln, ws, bias)


def spatial_bwd(u, vln, dg, ws, bias):
    T, E = u.shape
    C, H = GMLP_CHUNK, GMLP_HEADS
    hw = E // H

    def body(u_ref, v_ref, dg_ref, ws_ref, b_ref, du_ref, dv_ref, dws_ref, db_ref):
        @pl.when(pl.program_id(0) == 0)
        def _():
            dws_ref[...] = jnp.zeros_like(dws_ref)
            db_ref[...] = jnp.zeros_like(db_ref)
        tril = (lax.broadcasted_iota(jnp.int32, (C, C), 1) <= lax.broadcasted_iota(jnp.int32, (C, C), 0))
        for hh in range(H):
            sl = slice(hh * hw, (hh + 1) * hw)
            v = v_ref[:, sl]
            w = ws_ref[hh]
            dgv = dg_ref[:, sl].astype(F32)
            vp = _dot(w, v) + b_ref[:, sl]
            du_ref[:, sl] = dgv * vp
            dvp = dgv * u_ref[:, sl]
            dvpb = dvp.astype(BF16)
            dv_ref[:, sl] = _dot_tn(w, dvpb)
            dws_ref[hh] += jnp.where(tril, _dot_nt(dvpb, v), 0.0)
            db_ref[:, sl] += dvp

    return pl.pallas_call(
        body, grid=(T // C,),
        in_specs=[pl.BlockSpec((C, E), lambda i: (i, 0)), pl.BlockSpec((C, E), lambda i: (i, 0)),
                  pl.BlockSpec((C, E), lambda i: (i, 0)),
                  pl.BlockSpec((H, C, C), lambda i: (0, 0, 0)), pl.BlockSpec((C, E), lambda i: (0, 0))],
        out_specs=[pl.BlockSpec((C, E), lambda i: (i, 0)), pl.BlockSpec((C, E), lambda i: (i, 0)),
                   pl.BlockSpec((H, C, C), lambda i: (0, 0, 0)), pl.BlockSpec((C, E), lambda i: (0, 0))],
        out_shape=[jax.ShapeDtypeStruct((T, E), F32), jax.ShapeDtypeStruct((T, E), F32),
                   jax.ShapeDtypeStruct((H, C, C), F32), jax.ShapeDtypeStruct((C, E), F32)],
        name="spatial_bwd", compiler_params=_cparams(("arbitrary",)),
    )(u, vln, dg, ws, bias)


def _att_masks():
    r = lax.broadcasted_iota(jnp.int32, (ATT_BLK, ATT_BLK), 0)
    c = lax.broadcasted_iota(jnp.int32, (ATT_BLK, ATT_BLK), 1)
    return c <= r, c >= r


NEG = -1e30
ATT_SCALE = HEAD_DIM ** -0.5


def _att_view(t, dil):
    return t.reshape(t.shape[0] // dil, dil * t.shape[1])


def attn_fwd(name, q, k, v, dil, bsz):
    T, Wd = q.shape
    nb = T // (bsz * dil * ATT_BLK)
    TB = min(nb, ATT_TB)
    nsteps = nb // TB

    def body(q_ref, k_ref, v_ref, kp_ref, vp_ref, o_ref, l_ref):
        n = pl.program_id(2)
        mc, mp = _att_masks()
        for j in range(TB):
            rows = slice(j * ATT_BLK, (j + 1) * ATT_BLK)
            prow = slice((j - 1) * ATT_BLK, j * ATT_BLK)
            hp = (n * TB + j) > 0
            H = range(ATT_HEADS)
            ls = [slice(hh * HEAD_DIM, (hh + 1) * HEAD_DIM) for hh in H]
            qj = [q_ref[rows, ls[hh]] for hh in H]
            kc = [k_ref[rows, ls[hh]] for hh in H]
            kp = [k_ref[prow, ls[hh]] if j > 0 else kp_ref[:, ls[hh]] for hh in H]
            sc = [jnp.where(mc, _dot_nt(qj[hh], kc[hh]) * ATT_SCALE, NEG) for hh in H]
            sp = [jnp.where(mp & hp, _dot_nt(qj[hh], kp[hh]) * ATT_SCALE, NEG) for hh in H]
            m = [jnp.maximum(jnp.max(sc[hh], axis=1, keepdims=True), jnp.max(sp[hh], axis=1, keepdims=True))
                 for hh in H]
            pc = [jnp.exp(sc[hh] - m[hh]) for hh in H]
            pp = [jnp.exp(sp[hh] - m[hh]) for hh in H]
            l = [jnp.sum(pc[hh], axis=1, keepdims=True) + jnp.sum(pp[hh], axis=1, keepdims=True) for hh in H]
            vc = [v_ref[rows, ls[hh]] for hh in H]
            vp = [v_ref[prow, ls[hh]] if j > 0 else vp_ref[:, ls[hh]] for hh in H]
            for hh in H:
                o_ref[rows, ls[hh]] = (_dot(pc[hh].astype(BF16), vc[hh]) + _dot(pp[hh].astype(BF16), vp[hh])) / l[hh]
                l_ref[rows, ls[hh]] = jnp.broadcast_to(m[hh] + jnp.log(l[hh]), (ATT_BLK, HEAD_DIM))

    blk = pl.BlockSpec((TB * ATT_BLK, Wd), lambda b, r, n: (b * nsteps + n, r))
    prev = pl.BlockSpec((ATT_BLK, Wd), lambda b, r, n: (jnp.maximum(b * nb + n * TB - 1, 0), r))
    qv, kv, vv = (_att_view(t, dil) for t in (q, k, v))
    o, l = pl.pallas_call(
        body, grid=(bsz, dil, nsteps), in_specs=[blk, blk, blk, prev, prev], out_specs=[blk, blk],
        out_shape=[jax.ShapeDtypeStruct(qv.shape, F32), jax.ShapeDtypeStruct(qv.shape, F32)],
        name=name, compiler_params=_cparams(("arbitrary", "arbitrary", "arbitrary")),
    )(qv, kv, vv, kv, vv)
    return o.reshape(T, Wd), l.reshape(T, Wd)


def attn_bwd(name, q, k, v, do, mg, lse, dil, bsz):
    T, Wd = q.shape
    nb = T // (bsz * dil * ATT_BLK)
    TB = min(nb, ATT_TB_BWD)
    nsteps = nb // TB

    def body(q_ref, k_ref, v_ref, do_ref, mg_ref, l_ref, kp_ref, vp_ref, qn_ref, don_ref, mgn_ref, ln_ref,
             dq_ref, dk_ref, dv_ref):
        n = pl.program_id(2)
        mc, mp = _att_masks()

        def probs_all(qs, ks, lse_cols, mask):
            s = [_dot_nt(qh, kh) * ATT_SCALE for qh, kh in zip(qs, ks)]
            return [jnp.where(mask, jnp.exp(sh - lc), 0.0) for sh, lc in zip(s, lse_cols)]

        def ds_all(ps, dos, vs, deltas):
            dp = [_dot_nt(dh, vh) for dh, vh in zip(dos, vs)]
            return [(ph * (dph - dl) * ATT_SCALE).astype(BF16) for ph, dph, dl in zip(ps, dp, deltas)]

        H = range(ATT_HEADS)
        ls = [slice(hh * HEAD_DIM, (hh + 1) * HEAD_DIM) for hh in H]
        dk = [[None] * TB for _ in H]
        dv = [[None] * TB for _ in H]
        for j in range(TB + 1):
            rows = slice(j * ATT_BLK, (j + 1) * ATT_BLK)
            prow = slice((j - 1) * ATT_BLK, j * ATT_BLK)
            if j < TB:
                srcs = (q_ref, do_ref, mg_ref, l_ref)
                qj, doj, mgj, lj = ([r[rows, ls[hh]] for hh in H] for r in srcs)
                hp = (n * TB + j) > 0
            else:
                srcs = (qn_ref, don_ref, mgn_ref, ln_ref)
                qj, doj, mgj, lj = ([r[:, ls[hh]] for hh in H] for r in srcs)
                hp = (n + 1) * TB < nb
            lse_col = [lj[hh][:, 0:1] for hh in H]
            delta = [jnp.sum(doj[hh].astype(F32) * mgj[hh].astype(F32), axis=1, keepdims=True) for hh in H]
            if j > 0:
                kp = [k_ref[prow, ls[hh]] for hh in H]
                vp = [v_ref[prow, ls[hh]] for hh in H]
            else:
                kp = [kp_ref[:, ls[hh]] for hh in H]
                vp = [vp_ref[:, ls[hh]] for hh in H]
            pp = probs_all(qj, kp, lse_col, mp & hp)
            dsp = ds_all(pp, doj, vp, delta)
            if j > 0:
                for hh in H:
                    dk[hh][j - 1] = dk[hh][j - 1] + _dot_tn(dsp[hh], qj[hh])
                    dv[hh][j - 1] = dv[hh][j - 1] + _dot_tn(pp[hh].astype(BF16), doj[hh])
            if j < TB:
                kc = [k_ref[rows, ls[hh]] for hh in H]
                vc = [v_ref[rows, ls[hh]] for hh in H]
                pc = probs_all(qj, kc, lse_col, mc)
                dsc = ds_all(pc, doj, vc, delta)
                for hh in H:
                    dq_ref[rows, ls[hh]] = (_dot(dsc[hh], kc[hh]) + _dot(dsp[hh], kp[hh])).astype(dq_ref.dtype)
                for hh in H:
                    dk[hh][j] = _dot_tn(dsc[hh], qj[hh])
                    dv[hh][j] = _dot_tn(pc[hh].astype(BF16), doj[hh])
        for j in range(TB):
            rows = slice(j * ATT_BLK, (j + 1) * ATT_BLK)
            for hh in H:
                dk_ref[rows, ls[hh]] = dk[hh][j].astype(dk_ref.dtype)
                dv_ref[rows, ls[hh]] = dv[hh][j].astype(dv_ref.dtype)

    blk = pl.BlockSpec((TB * ATT_BLK, Wd), lambda b, r, n: (b * nsteps + n, r))
    prev = pl.BlockSpec((ATT_BLK, Wd), lambda b, r, n: (jnp.maximum(b * nb + n * TB - 1, 0), r))
    nxt = pl.BlockSpec((ATT_BLK, Wd), lambda b, r, n: (b * nb + jnp.minimum((n + 1) * TB, nb - 1), r))
    qv, kv, vv, dov, mgv, lv = (_att_view(t, dil) for t in (q, k, v, do, mg, lse))
    res = pl.pallas_call(
        body, grid=(bsz, dil, nsteps), in_specs=[blk] * 6 + [prev, prev, nxt, nxt, nxt, nxt],
        out_specs=[blk, blk, blk], out_shape=[jax.ShapeDtypeStruct(qv.shape, BF16)] * 3,
        name=name, compiler_params=_cparams(("arbitrary", "arbitrary", "arbitrary")),
    )(qv, kv, vv, dov, mgv, lv, kv, vv, qv, dov, mgv, lv)
    return [t.reshape(T, Wd) for t in res]


QKV_SLOTS = 3 * len(ATT_CONFIGS) * ATT_HEADS
SLOTS_PER_DEV = QKV_SLOTS // N_DEV


def qkv_matmul(name, a, b):
    M, K = a.shape
    nblk, _, n = b.shape
    nout = QKV_SLOTS // ATT_HEADS
    tm = _rows_for(M, K * 2 + nout * ATT_W * 2)

    def body(a_ref, b_ref, *outs):
        av = a_ref[...]
        for c in range(nblk):
            acc = _dot(av, b_ref[c]).astype(BF16)
            for r in range(SLOTS_PER_DEV):
                k, hh = divmod(c * SLOTS_PER_DEV + r, ATT_HEADS)
                outs[k][:, hh * HEAD_DIM:(hh + 1) * HEAD_DIM] = acc[:, r * HEAD_DIM:(r + 1) * HEAD_DIM]

    return pl.pallas_call(
        body, grid=(M // tm,),
        in_specs=[pl.BlockSpec((tm, K), lambda i: (i, 0)), pl.BlockSpec(b.shape, lambda i: (0, 0, 0))],
        out_specs=[pl.BlockSpec((tm, ATT_W), lambda i: (i, 0))] * nout,
        out_shape=[jax.ShapeDtypeStruct((M, ATT_W), BF16)] * nout,
        name=name, compiler_params=_cparams(("arbitrary",)))(a, b)


def _coords():
    return lax.axis_index("x"), lax.axis_index("y"), lax.axis_index("c")


def all_gather(name, xs):
    n = len(xs)

    def body(*refs):
        start, forward, finish = _gather_phases(refs[:n], refs[n:2 * n], *refs[2 * n:])
        start()
        forward()
        finish()

    anyspec = pl.BlockSpec(memory_space=pl.ANY)
    return pl.pallas_call(
        body, out_shape=_gather_out_shapes(xs), in_specs=[anyspec] * n, out_specs=[anyspec] * n,
        scratch_shapes=_gather_sems(n), name=name,
    )(*xs)


def _gather_out_shapes(xs):
    return [jax.ShapeDtypeStruct((N_DEV,) + t.shape, t.dtype) for t in xs]


def _gather_sems(n):
    return [pltpu.SemaphoreType.DMA((7 * n,)), pltpu.SemaphoreType.DMA((7 * n,)), pltpu.SemaphoreType.DMA((n,))]


def _gather_phases(x_refs, out_refs, send_sems, recv_sems, local_sems):
    n = len(x_refs)

    def parts():
        x, y, c = _coords()
        return (x, y, c), (x, y, 1 - c), [(1 - x, y), (x, 1 - y), (1 - x, 1 - y)], c

    def slot(a, px, py, pc):
        return out_refs[a].at[4 * px + 2 * py + pc]

    def copy(a, k, block, to, src=None):
        return pltpu.make_async_remote_copy(
            src_ref=slot(a, *block) if src is None else src, dst_ref=slot(a, *block),
            send_sem=send_sems.at[7 * a + k], recv_sem=recv_sems.at[7 * a + k],
            device_id=to, device_id_type=MESH)

    def mine(a, me):
        return pltpu.make_async_copy(x_refs[a], slot(a, *me), local_sems.at[a])

    def first(a, me, sibling, chips, c):
        return ([copy(a, 0, me, sibling, src=x_refs[a])]
                + [copy(a, 1 + j, me, (*chip, c), src=x_refs[a]) for j, chip in enumerate(chips)])

    def start():
        me, sibling, chips, c = parts()
        for a in range(n):
            mine(a, me).start()
        for a in range(n):
            for cp in first(a, me, sibling, chips, c):
                cp.start()

    def forward():
        me, sibling, chips, c = parts()
        for j, chip in enumerate(chips):
            for a in range(n):
                copy(a, 1 + j, (*chip, c), me).wait_recv()
                copy(a, 4 + j, (*chip, c), sibling).start()

    def finish():
        me, sibling, chips, c = parts()
        for a in range(n):
            copy(a, 0, sibling, me).wait_recv()
            for j, chip in enumerate(chips):
                copy(a, 4 + j, (*chip, 1 - c), me).wait_recv()
        for a in range(n):
            for cp in first(a, me, sibling, chips, c):
                cp.wait_send()
            for j, chip in enumerate(chips):
                copy(a, 4 + j, (*chip, c), sibling).wait_send()
            mine(a, me).wait()

    return start, forward, finish


def exchange_sibling(name, gs):
    n = len(gs)

    def body(*refs):
        start, finish = _sibling_phases(refs[:n], refs[n:2 * n], *refs[2 * n:])
        start()
        finish()

    anyspec = pl.BlockSpec(memory_space=pl.ANY)
    return pl.pallas_call(
        body, out_shape=_sibling_out_shapes(gs), in_specs=[anyspec] * n, out_specs=[anyspec] * n,
        scratch_shapes=_sibling_sems(n), name=name,
    )(*gs)


def _sibling_out_shapes(gs):
    return [jax.ShapeDtypeStruct((4,) + g.shape[1:], g.dtype) for g in gs]


def _sibling_sems(n):
    return [pltpu.SemaphoreType.DMA((4 * n,)), pltpu.SemaphoreType.DMA((4 * n,))]


def _sibling_phases(g_refs, out_refs, send_sems, recv_sems):
    n = len(g_refs)

    def copies():
        x, y, c = _coords()
        return [pltpu.make_async_remote_copy(
            src_ref=g_refs[a].at[2 * q + (1 - c)], dst_ref=out_refs[a].at[q],
            send_sem=send_sems.at[4 * a + q], recv_sem=recv_sems.at[4 * a + q],
            device_id=(x, y, 1 - c), device_id_type=MESH)
            for a in range(n) for q in range(4)]

    def start():
        for cp in copies():
            cp.start()

    def finish():
        cps = copies()
        for cp in cps:
            cp.wait_recv()
        for cp in cps:
            cp.wait_send()

    return start, finish


def exchange_chips(name, ps):
    n = len(ps)

    def body(*refs):
        start, finish = _chips_phases(refs[:n], refs[n:2 * n], *refs[2 * n:])
        start()
        finish()

    anyspec = pl.BlockSpec(memory_space=pl.ANY)
    return pl.pallas_call(
        body, out_shape=_chips_out_shapes(ps), in_specs=[anyspec] * n, out_specs=[anyspec] * n,
        scratch_shapes=_chips_sems(n), name=name,
    )(*ps)


def _chips_out_shapes(ps):
    return [jax.ShapeDtypeStruct((3,) + p.shape[1:], p.dtype) for p in ps]


def _chips_sems(n):
    return [pltpu.SemaphoreType.DMA((3 * n,)), pltpu.SemaphoreType.DMA((3 * n,))]


def _chips_phases(p_refs, out_refs, send_sems, recv_sems):
    n = len(p_refs)

    def copies():
        x, y, c = _coords()
        chips = [(1 - x, y), (x, 1 - y), (1 - x, 1 - y)]
        return [pltpu.make_async_remote_copy(
            src_ref=p_refs[a].at[2 * px + py], dst_ref=out_refs[a].at[k],
            send_sem=send_sems.at[3 * a + k], recv_sem=recv_sems.at[3 * a + k],
            device_id=(px, py, c), device_id_type=MESH)
            for a in range(n) for k, (px, py) in enumerate(chips)]

    def start():
        for cp in copies():
            cp.start()

    def finish():
        cps = copies()
        for cp in cps:
            cp.wait_recv()
        for cp in cps:
            cp.wait_send()

    return start, finish


def _row_tile(R):
    tr = 256
    while R % tr:
        tr //= 2
    assert tr % 8 == 0
    return tr


def add_sibling(name, g, recv, c_idx):
    _, R, C = g.shape
    tr = _row_tile(R)

    def body(c_ref, g_ref, r_ref, o_ref, o16_ref):
        s = g_ref[...] + r_ref[...].astype(F32)
        o_ref[...] = s
        o16_ref[...] = s.astype(BF16)

    out = pl.BlockSpec((None, tr, C), lambda q, i, cr: (q, i, 0))
    return pl.pallas_call(
        body,
        grid_spec=pltpu.PrefetchScalarGridSpec(
            num_scalar_prefetch=1, grid=(4, R // tr),
            in_specs=[pl.BlockSpec((None, tr, C), lambda q, i, cr: (2 * q + cr[0], i, 0)), out],
            out_specs=[out, out]),
        out_shape=[jax.ShapeDtypeStruct((4, R, C), F32), jax.ShapeDtypeStruct((4, R, C), BF16)], name=name,
        compiler_params=_cparams(("arbitrary", "arbitrary")),
    )(c_idx, g, recv)


def _adam_math(w, g, m, v):
    m = ADAM_B1 * m + (1.0 - ADAM_B1) * g
    v = ADAM_B2 * v + (1.0 - ADAM_B2) * jnp.square(g)
    m_hat = m / (1.0 - ADAM_B1 ** ADAM_STEP)
    v_hat = v / (1.0 - ADAM_B2 ** ADAM_STEP)
    delta = -ADAM_LR * (m_hat / (jnp.sqrt(v_hat) + ADAM_EPS) + ADAM_WD * w)
    return delta, m, v


def adam_big(name, p1, recv, w, m, v, chip_idx, layer=0):
    _, R, C = p1.shape
    tr = _row_tile(R)
    nt = R // tr

    def body(q_ref, p_ref, r_ref, w_ref, m_ref, v_ref, g_ref, d_ref, nm_ref, nv_ref):
        g = ((p_ref[...] + r_ref[0].astype(F32)) + r_ref[1].astype(F32)) + r_ref[2].astype(F32)
        d, nm, nv = _adam_math(w_ref[...], g, m_ref[...], v_ref[...])
        g_ref[...] = g
        d_ref[...] = d
        nm_ref[...] = nm
        nv_ref[...] = nv

    row_in = pl.BlockSpec((tr, C), lambda i, qr: (layer * nt + i, 0))
    row = pl.BlockSpec((tr, C), lambda i, qr: (i, 0))
    return pl.pallas_call(
        body,
        grid_spec=pltpu.PrefetchScalarGridSpec(
            num_scalar_prefetch=1, grid=(nt,),
            in_specs=[pl.BlockSpec((None, tr, C), lambda i, qr: (qr[0], i, 0)),
                      pl.BlockSpec((3, tr, C), lambda i, qr: (0, i, 0)), row_in, row_in, row_in],
            out_specs=[row, row, row, row]),
        out_shape=[jax.ShapeDtypeStruct((R, C), F32)] * 4, name=name,
        compiler_params=_cparams(("arbitrary",)),
    )(chip_idx, p1, recv, w, m, v)


def sum8(parts):
    _, R, C = parts.shape

    def body(p_ref, o_ref):
        acc = p_ref[0]
        for k in range(1, N_DEV):
            acc = acc + p_ref[k]
        o_ref[...] = acc

    tr = 128
    while R % tr:
        tr //= 2
    assert tr % 8 == 0
    return pl.pallas_call(
        body, grid=(R // tr,), in_specs=[pl.BlockSpec((N_DEV, tr, C), lambda i: (0, i, 0))],
        out_specs=pl.BlockSpec((tr, C), lambda i: (i, 0)), out_shape=jax.ShapeDtypeStruct((R, C), F32),
        name="sum8", compiler_params=_cparams(("arbitrary",)),
    )(parts)


def adam_small(w, g, m, v):
    def fn(wt, gt, mt, vt):
        return _adam_math(wt, gt, mt, vt)
    C = w.shape[1]
    return rowwise("adam_small", fn, [w, g, m, v], [], [(C, F32)] * 3, tr=128)


def _pack(arrs, rows_mult=8):
    flat = jnp.concatenate([a.reshape(-1) for a in arrs])
    n = flat.shape[0]
    per = PACK_C * rows_mult
    pad = (-n) % per
    if pad:
        flat = jnp.concatenate([flat, jnp.zeros((pad,), flat.dtype)])
    return flat.reshape(-1, PACK_C)


def _unpack(buf, shapes):
    flat = buf.reshape(-1)
    out, off = [], 0
    for s in shapes:
        n = math.prod(s)
        out.append(flat[off:off + n].reshape(s))
        off += n
    return out


def _blocked(gfull, axis):
    shp = gfull.shape
    n = shp[axis] // N_DEV
    t = gfull.reshape(shp[:axis] + (N_DEV, n) + shp[axis + 1:])
    t = jnp.moveaxis(t, axis, 0)
    return t.reshape(N_DEV, -1)


def _unblocked(gathered, shard_shape, axis):
    t = jnp.moveaxis(gathered, 0, axis)
    shp = shard_shape[:axis] + (N_DEV * shard_shape[axis],) + shard_shape[axis + 1:]
    return t.reshape(shp)


def _relu2_epi(acc):
    r = jnp.maximum(acc, 0.0)
    return acc, r * r


def _step(x3, target3, W, comm):
    bsz, S, D = x3.shape
    T = bsz * S
    x = x3.reshape(T, D)
    target = target3.reshape(T, D)
    row = lambda v: v.reshape(1, -1)
    grads = {}

    s5p = (W['ssm_a_re'][0], W['ssm_a_im'][0], W['ssm_log_dt'][0], W['ssm_b_re'][0], W['ssm_b_im'][0])
    (lam_re, lam_im, bb_re, bb_im), s5_disc_vjp = jax.vjp(s5_disc, *s5p)
    pwr, pwi, l2r, l2i = s5_tables(lam_re, lam_im, S5_SUB)
    bre, bim = _s5_blockdiag_b(bb_re).astype(BF16), _s5_blockdiag_b(bb_im).astype(BF16)
    cre, cim = _s5_blockdiag_c(W['ssm_c_re'][0]).astype(BF16), _s5_blockdiag_c(W['ssm_c_im'][0]).astype(BF16)
    dskip = W['ssm_d']

    tril = jnp.tril(jnp.ones((GMLP_CHUNK, GMLP_CHUNK), bool))
    ws = jnp.where(tril[None], W['gmlp_w_s'][0], 0.0).astype(BF16)
    hw = D // GMLP_HEADS
    sbias = jnp.repeat(W['gmlp_b_s'][0].T, hw, axis=1)

    saved = []
    def add_norm(acc, *ex):
        xn = acc + ex[0] + ex[1] if len(ex) == 3 else acc + ex[0]
        return xn, _rms(xn, ex[-1])

    for i in range(DEPTH):
        sv = {'x': x}
        nm = W['norm_mix'][i:i + 1]
        nl = W['norm_mlp'][i:i + 1]
        if i == 0:
            h, hf = rms_fwd("rms_mix0", x, nm, want_f32=True)
            res = s5_fwd(hf, bre, bim, cre, cim, pwr, pwi, l2r, l2i, dskip, bsz, gather=comm.gather_list)
            ypre, gy, xs, xr_all, xi_all = res[:5]
            Wfull, Wsh = comm.weights(res[5:])
            W = {**W, **Wsh}
            conv_w = jnp.concatenate([W['conv_w_dw'][0], jnp.zeros((1, D), F32)], axis=0)
            def s5_out(acc, xt, g):
                xn = xt + _glu(acc)
                return acc, xn, _rms(xn, g)
            z, x1, h2 = matmul("s5_glu_mm", gy, Wfull['ssm_w_glu'], mode='cb', whole_rows=True, epi=s5_out,
                               extras=[(x, 'tile'), (nl, 'row')], out_dtypes=(F32, F32, BF16), out_cols=(2 * D, D, D))
            sv.update(hf=hf, ypre=ypre, gy=gy, xs=xs, xr=xr_all, xi=xi_all, z=z)
        elif i == 1:
            def pw1_out(acc, b):
                zt = acc + b
                return zt, _glu(zt)
            z, zg = matmul("conv_pw1", h, Wfull['conv_w_pw1'], mode='cb', whole_rows=True, epi=pw1_out,
                           extras=[(W['conv_b_pw1'], 'row')], out_dtypes=(F32, F32), out_cols=(2 * D, D))
            yc = dwconv_fwd(zg, conv_w, W['conv_b_dw'], bsz)
            y2, = rowwise("conv_ln_silu", lambda t, g, b: jax.nn.silu(_ln(t, g, b)), [yc],
                          [W['conv_ln_g'], W['conv_ln_b']], [(D, BF16)])
            x1, h2 = matmul("conv_pw2", y2, Wfull['conv_w_pw2'], epi=add_norm, whole_rows=True,
                            extras=[(W['conv_b_pw2'], 'row'), (x, 'tile'), (nl, 'row')], out_dtypes=(F32, BF16))
            sv.update(h=h, z=z, zg=zg, yc=yc, y2=y2)
        elif i == 2:
            def gm_pre(acc, g, b):
                act = jax.nn.gelu(acc)
                return acc, act[:, :D], _ln(act[:, D:], g, b)
            zp, u, vln = matmul("gmlp_in", h, Wfull['gmlp_w_in'], mode='cb', whole_rows=True, epi=gm_pre,
                                extras=[(W['gmlp_ln_g'], 'row'), (W['gmlp_ln_b'], 'row')],
                                out_dtypes=(F32, F32, BF16), out_cols=(2 * D, D, D))
            gated = spatial_fwd(u, vln, ws, sbias)
            x1, h2 = matmul("gmlp_out", gated, Wfull['gmlp_w_out'], epi=add_norm, whole_rows=True,
                            extras=[(x, 'tile'), (nl, 'row')], out_dtypes=(F32, BF16))
            sv.update(h=h, zp=zp, u=u, vln=vln, gated=gated)
        else:
            qkv = qkv_matmul("attn_qkv", h, Wfull['attn_w_qkv'])
            ng = len(ATT_CONFIGS)
            outs, lses, blocks = [], [], []
            for gi, (window, dil) in enumerate(ATT_CONFIGS):
                qb, kb, vb = (qkv[j * ng + gi] for j in range(3))
                ob, lb = attn_fwd("attn_fwd%d" % gi, qb, kb, vb, dil, bsz)
                blocks.append((qb, kb, vb, lb, dil))
                outs.append(ob)
                lses.append(lb)

            def merge(o0, o1, o2, l0, l1, l2):
                m = jnp.maximum(jnp.maximum(l0, l1), l2)
                e0, e1, e2 = jnp.exp(l0 - m), jnp.exp(l1 - m), jnp.exp(l2 - m)
                inv = 1.0 / (e0 + e1 + e2)
                w0, w1, w2 = e0 * inv, e1 * inv, e2 * inv
                return w0 * o0 + w1 * o1 + w2 * o2, w0, w1, w2
            merged, w0, w1, w2 = rowwise("attn_merge", merge, outs + lses, [],
                                         [(ATT_W, BF16), (ATT_W, F32), (ATT_W, F32), (ATT_W, F32)])
            wo = Wfull['attn_w_o']
            wo_nat = wo.transpose(1, 0, 2).reshape(wo.shape[1], N_DEV * wo.shape[2])
            x1, h2 = matmul("attn_o", merged, wo_nat, epi=add_norm, whole_rows=True,
                            extras=[(x, 'tile'), (nl, 'row')], out_dtypes=(F32, BF16))
            sv.update(h=h, blocks=blocks, merged=merged, wts=(w0, w1, w2))
        a, act = matmul("mlp_in%d" % i, h2, Wfull['mlp_w_in'][i], mode='cb', epi=_relu2_epi, out_dtypes=(BF16, BF16))
        if i + 1 < DEPTH:
            x2, h = matmul("mlp_out%d" % i, act, Wfull['mlp_w_out'][i], epi=add_norm, whole_rows=True,
                           extras=[(x1, 'tile'), (W['norm_mix'][i + 1:i + 2], 'row')], out_dtypes=(F32, BF16))
        else:
            x2, = matmul("mlp_out%d" % i, act, Wfull['mlp_w_out'][i], epi=lambda acc, r: (acc + r,),
                         extras=[(x1, 'tile')])
        sv.update(x1=x1, h2=h2, a=a, act=act)
        saved.append(sv)
        x = x2

    def loss_fn(xt, tt, g):
        y, vjp = jax.vjp(_rms, xt, g)
        err = y - tt
        dxx, dg = vjp(err * (1.0 / D))
        lval = jnp.sum(jnp.sum(err * err, axis=1, keepdims=True), axis=0, keepdims=True) * (0.5 / D)
        return dxx, dxx, jnp.broadcast_to(lval, (1, 128)), dg
    dx, dxb, lacc, dnf = rowwise("loss_head", loss_fn, [x, target], [row(W['norm_final'])],
                                 [(D, F32), (D, BF16)], [((1, 128), F32), ((1, D), F32)])
    loss_local = lacc[0, 0]
    grads['norm_final'] = dnf.reshape(-1)

    g_norm_mix, g_norm_mlp = [None] * DEPTH, [None] * DEPTH
    g_mlp_in, g_mlp_out = [None] * DEPTH, [None] * DEPTH
    nl_all = [W['norm_mlp'][i:i + 1] for i in range(DEPTH)]
    nm_all = [W['norm_mix'][i:i + 1] for i in range(DEPTH)]

    def norm_bwd(xt, dres, g):
        def epi(dh, xv, dr, gv):
            _, vjp = jax.vjp(_rms, xv, gv)
            dxv, dgv = vjp(dh)
            dxv = dxv + dr
            return dxv, dxv, dgv
        return dict(epi=epi, extras=[xt, dres], params=[g], out_dtypes=(F32, BF16), acc_out=[((1, D), F32)])

    for i in reversed(range(DEPTH)):
        sv = saved[i]
        da, = matmul("mlp_out_bwd%d" % i, dxb, Wfull['mlp_w_out'][i], mode='nt',
                     epi=lambda acc, av: (acc * (2.0 * jnp.maximum(av.astype(F32), 0.0)),),
                     extras=[(sv['a'], 'tile')], out_dtypes=(BF16,))
        g_mlp_out[i] = _rows_blocked(wgrad("mlp_out_wg%d" % i, sv['act'], dxb))
        sib = comm.sibling_early(grads, g_mlp_in, g_mlp_out) if i == 0 else []
        res = matmul_nt_cb("mlp_in_bwd%d" % i, da, Wfull['mlp_w_in'][i], sibling=sib,
                           **norm_bwd(sv['x1'], dx, nl_all[i]))
        dx, dxb, dg = res[:3]
        if i == 0:
            comm.recv_early = res[3:]
        g_mlp_in[i] = wgrad("mlp_in_wg%d" % i, sv['h2'], da, cb=True)
        g_norm_mlp[i] = dg.reshape(-1)
        xin = sv['x']
        if i == 0:
            dz, = rowwise("s5_glu_bwd", _glu_bwd, [sv['z'], dx], [], [(2 * D, BF16)])
            dgy, = matmul_nt_cb("s5_glu_mm_bwd", dz, Wfull['ssm_w_glu'])
            grads['ssm_w_glu'] = wgrad("s5_glu_wg", sv['gy'], dz, cb=True)
            grads['mlp_w_in'], grads['mlp_w_out'] = g_mlp_in, g_mlp_out
            res = s5_bwd(sv['hf'], sv['ypre'], dgy, sv['xs'], sv['xr'], sv['xi'], bre, bim, cre, cim,
                         pwr[:, ::-1], pwi[:, ::-1], l2r, l2i, dskip, bsz, chips=comm.rs_front(grads))
            du, dbr, dbi, dcr, dci, dl, dd = res[:7]
            comm.recv2 = res[7:]
            dlam_re = dl[:, 0, :].reshape(SSM_GROUPS, SSM_STATE)
            dlam_im = dl[:, 1, :].reshape(SSM_GROUPS, SSM_STATE)
            s5_cot = (dlam_re, dlam_im, _s5_blockdiag_b_inv(dbr), _s5_blockdiag_b_inv(dbi))
            grads['ssm_c_re'] = _s5_blockdiag_c_inv(dcr)[None]
            grads['ssm_c_im'] = _s5_blockdiag_c_inv(dci)[None]
            grads['ssm_d'] = dd[0:1]
            dx, dxb, dg = rms_bwd("rms_mix_bwd0", xin, du, dx, nm_all[0])
        elif i == 1:
            dy2, = matmul("conv_pw2_bwd", dxb, Wfull['conv_w_pw2'], mode='nt')
            grads['conv_w_pw2'] = _rows_blocked(wgrad("conv_pw2_wg", sv['y2'], dxb))

            def ln_silu_bwd(yt, dt, dxt, g, b):
                _, vjp = jax.vjp(lambda t, gg, bb: jax.nn.silu(_ln(t, gg, bb)), yt, g, b)
                dyc, dgg, dbb = vjp(dt)
                return dyc, dgg, dbb, _colsum(dxt)
            dyc, dlg, dlb, dbp2 = rowwise("conv_ln_silu_bwd", ln_silu_bwd, [sv['yc'], dy2, dx],
                                          [W['conv_ln_g'], W['conv_ln_b']], [(D, F32)],
                                          [((1, D), F32), ((1, D), F32), ((1, D), F32)])
            grads['conv_ln_g'], grads['conv_ln_b'], grads['conv_b_pw2'] = dlg, dlb, dbp2
            dzg, dwd, dbd = dwconv_bwd(sv['zg'], dyc, conv_w, bsz)
            grads['conv_w_dw'] = dwd[None, :CONV_WIDTH]
            grads['conv_b_dw'] = dbd

            def glu_bwd1(zt, dyt):
                dzt = _glu_bwd(zt, dyt)
                return dzt, _colsum(dzt)
            dz, dbp1 = rowwise("conv_glu_bwd", glu_bwd1, [sv['z'], dzg], [], [(2 * D, BF16)], [((1, 2 * D), F32)])
            grads['conv_b_pw1'] = dbp1
            dx, dxb, dg = matmul_nt_cb("conv_pw1_bwd", dz, Wfull['conv_w_pw1'], **norm_bwd(xin, dx, nm_all[i]))
            grads['conv_w_pw1'] = wgrad("conv_pw1_wg", sv['h'], dz, cb=True)
        elif i == 2:
            dgt, = matmul("gmlp_out_bwd", dxb, Wfull['gmlp_w_out'], mode='nt', out_dtypes=(BF16,))
            grads['gmlp_w_out'] = _rows_blocked(wgrad("gmlp_out_wg", sv['gated'], dxb))
            du, dvln, dws, dsb = spatial_bwd(sv['u'], sv['vln'], dgt, ws, sbias)
            grads['gmlp_w_s'] = dws[None]
            grads['gmlp_b_s'] = dsb.reshape(GMLP_CHUNK, GMLP_HEADS, hw).sum(-1).T[None]

            def gm_pre_bwd(zt, dut, dvt, g, b):
                _, vjp_u = jax.vjp(jax.nn.gelu, zt[:, :D])
                _, vjp_v = jax.vjp(lambda zz, gg, bb: _ln(jax.nn.gelu(zz), gg, bb), zt[:, D:], g, b)
                dz2, dgg, dbb = vjp_v(dvt)
                return jnp.concatenate([vjp_u(dut)[0], dz2], axis=1), dgg, dbb
            dzp, dlg, dlb = rowwise("gmlp_pre_bwd", gm_pre_bwd, [sv['zp'], du, dvln],
                                    [W['gmlp_ln_g'], W['gmlp_ln_b']], [(2 * D, BF16)], [((1, D), F32), ((1, D), F32)])
            grads['gmlp_ln_g'], grads['gmlp_ln_b'] = dlg, dlb
            dx, dxb, dg = matmul_nt_cb("gmlp_in_bwd", dzp, Wfull['gmlp_w_in'], **norm_bwd(xin, dx, nm_all[i]))
            grads['gmlp_w_in'] = wgrad("gmlp_in_wg", sv['h'], dzp, cb=True)
        else:
            dm, = matmul_nt_cb("attn_o_bwd", dxb, Wfull['attn_w_o'])
            grads['attn_w_o'] = wgrad("attn_o_wg", sv['merged'], dxb, cb=True)
            w0, w1, w2 = sv['wts']
            do0, do1, do2 = rowwise("attn_merge_bwd", lambda d, a, b, c: (a * d, b * d, c * d), [dm, w0, w1, w2], [],
                                    [(ATT_W, BF16)] * 3)
            dparts = [[None] * 3 for _ in range(3)]
            for gi, (dog, (qb, kb, vb, lb, dil)) in enumerate(zip((do0, do1, do2), sv['blocks'])):
                dqb, dkb, dvb = attn_bwd("attn_bwd%d" % gi, qb, kb, vb, dog, sv['merged'], lb, dil, bsz)
                for j, t in enumerate((dqb, dkb, dvb)):
                    dparts[j][gi] = t
            dqkv = [dparts[j][gi] for j in range(3) for gi in range(3)]
            dx, dxb, dg = matmul_nt_cb("attn_qkv_bwd", dqkv, Wfull['attn_w_qkv'], heads=True,
                                       **norm_bwd(xin, dx, nm_all[i]))
            grads['attn_w_qkv'] = wgrad("attn_qkv_wg", sv['h'], dqkv, cb=True, heads=True)
        g_norm_mix[i] = dg.reshape(-1)

    grads['norm_mix'] = jnp.stack(g_norm_mix)
    grads['norm_mlp'] = jnp.stack(g_norm_mlp)
    grads['mlp_w_in'] = g_mlp_in
    grads['mlp_w_out'] = g_mlp_out
    return loss_local, dx.reshape(bsz, S, D), grads, (s5_disc_vjp, s5_cot)


class _StepComm:
    def __init__(self, Wl, c_idx):
        self.Wl, self.c_idx = Wl, c_idx
        self.units = []
        for n in BIG:
            self.units += [(n, i) for i in range(DEPTH)] if Wl[n].shape[0] == DEPTH else [(n, None)]
        self.ss_names = list(SMALL_SHARDED)
        spack = _pack([Wl[n] for n in self.ss_names])
        self.gather_list = [Wl[n][0 if i is None else i].astype(BF16) for n, i in self.units] + [spack]
        self.p1 = self.recv2 = None

    @staticmethod
    def tag(n, i):
        return n if i is None else "%s%d" % (n, i)

    def weights(self, gathered):
        Wl = self.Wl
        Wfull = {}
        for (n, i), g in zip(self.units, gathered):
            w = g if BIG[n] == 2 else g.reshape(N_DEV * g.shape[1], g.shape[2])
            if i is None:
                Wfull[n] = w
            else:
                Wfull.setdefault(n, []).append(w)
        sparts = _unpack_gathered(gathered[-1], [Wl[n].shape for n in self.ss_names])
        Wsh = {n: _unblocked(p, Wl[n].shape, SMALL_SHARDED[n]) for n, p in zip(self.ss_names, sparts)}
        return Wfull, Wsh

    LATE = (('mlp_w_in', 0), ('ssm_w_glu', None))

    def sibling_early(self, grads, g_mlp_in, g_mlp_out):
        src = {**grads, 'mlp_w_in': g_mlp_in, 'mlp_w_out': g_mlp_out}
        return [(src[n] if i is None else src[n][i])[1] for n, i in self.units if (n, i) not in self.LATE]

    def rs_front(self, grads):
        pairs = [grads[n] if i is None else grads[n][i] for n, i in self.units]
        late = [k for k, u in enumerate(self.units) if u in self.LATE]
        got = exchange_sibling("rs_sibling", [pairs[k][1] for k in late])
        early = iter(self.recv_early)
        recv1 = [got[late.index(k)] if k in late else next(early) for k in range(len(self.units))]
        self.p1 = [add_sibling("add_sibling_" + self.tag(n, i), p[0], r, self.c_idx)
                   for (n, i), p, r in zip(self.units, pairs, recv1)]
        return [p[1] for p in self.p1]


def _rows_blocked(pair):
    return tuple(t.reshape(N_DEV, t.shape[0] // N_DEV, t.shape[1]) for t in pair)


def kernel(x, norm_mix, norm_mlp, norm_final, ssm_a_re, ssm_a_im, ssm_b_re, ssm_b_im, ssm_c_re, ssm_c_im, ssm_d, ssm_log_dt, ssm_w_glu, conv_w_pw1, conv_b_pw1, conv_w_dw, conv_b_dw, conv_ln_g, conv_ln_b, conv_w_pw2, conv_b_pw2, gmlp_w_in, gmlp_ln_g, gmlp_ln_b, gmlp_w_s, gmlp_b_s, gmlp_w_out, attn_w_qkv, attn_w_o, mlp_w_in, mlp_w_out, loss_target, m_norm_mix, m_norm_mlp, m_norm_final, m_ssm_a_re, m_ssm_a_im, m_ssm_b_re, m_ssm_b_im, m_ssm_c_re, m_ssm_c_im, m_ssm_d, m_ssm_log_dt, m_ssm_w_glu, m_conv_w_pw1, m_conv_b_pw1, m_conv_w_dw, m_conv_b_dw, m_conv_ln_g, m_conv_ln_b, m_conv_w_pw2, m_conv_b_pw2, m_gmlp_w_in, m_gmlp_ln_g, m_gmlp_ln_b, m_gmlp_w_s, m_gmlp_b_s, m_gmlp_w_out, m_attn_w_qkv, m_attn_w_o, m_mlp_w_in, m_mlp_w_out, v_norm_mix, v_norm_mlp, v_norm_final, v_ssm_a_re, v_ssm_a_im, v_ssm_b_re, v_ssm_b_im, v_ssm_c_re, v_ssm_c_im, v_ssm_d, v_ssm_log_dt, v_ssm_w_glu, v_conv_w_pw1, v_conv_b_pw1, v_conv_w_dw, v_conv_b_dw, v_conv_ln_g, v_conv_ln_b, v_conv_w_pw2, v_conv_b_pw2, v_gmlp_w_in, v_gmlp_ln_g, v_gmlp_ln_b, v_gmlp_w_s, v_gmlp_b_s, v_gmlp_w_out, v_attn_w_qkv, v_attn_w_o, v_mlp_w_in, v_mlp_w_out):
    args = (norm_mix, norm_mlp, norm_final, ssm_a_re, ssm_a_im, ssm_b_re, ssm_b_im, ssm_c_re, ssm_c_im, ssm_d,
            ssm_log_dt, ssm_w_glu, conv_w_pw1, conv_b_pw1, conv_w_dw, conv_b_dw, conv_ln_g, conv_ln_b, conv_w_pw2,
            conv_b_pw2, gmlp_w_in, gmlp_ln_g, gmlp_ln_b, gmlp_w_s, gmlp_b_s, gmlp_w_out, attn_w_qkv, attn_w_o,
            mlp_w_in, mlp_w_out)
    margs = (m_norm_mix, m_norm_mlp, m_norm_final, m_ssm_a_re, m_ssm_a_im, m_ssm_b_re, m_ssm_b_im, m_ssm_c_re,
             m_ssm_c_im, m_ssm_d, m_ssm_log_dt, m_ssm_w_glu, m_conv_w_pw1, m_conv_b_pw1, m_conv_w_dw, m_conv_b_dw,
             m_conv_ln_g, m_conv_ln_b, m_conv_w_pw2, m_conv_b_pw2, m_gmlp_w_in, m_gmlp_ln_g, m_gmlp_ln_b,
             m_gmlp_w_s, m_gmlp_b_s, m_gmlp_w_out, m_attn_w_qkv, m_attn_w_o, m_mlp_w_in, m_mlp_w_out)
    vargs = (v_norm_mix, v_norm_mlp, v_norm_final, v_ssm_a_re, v_ssm_a_im, v_ssm_b_re, v_ssm_b_im, v_ssm_c_re,
             v_ssm_c_im, v_ssm_d, v_ssm_log_dt, v_ssm_w_glu, v_conv_w_pw1, v_conv_b_pw1, v_conv_w_dw, v_conv_b_dw,
             v_conv_ln_g, v_conv_ln_b, v_conv_w_pw2, v_conv_b_pw2, v_gmlp_w_in, v_gmlp_ln_g, v_gmlp_ln_b,
             v_gmlp_w_s, v_gmlp_b_s, v_gmlp_w_out, v_attn_w_qkv, v_attn_w_o, v_mlp_w_in, v_mlp_w_out)
    Wl = dict(zip(WEIGHT_NAMES, args))
    Ml = dict(zip(WEIGHT_NAMES, margs))
    Vl = dict(zip(WEIGHT_NAMES, vargs))
    cx, cy, cc = _coords()
    my_idx = 4 * cx + 2 * cy + cc

    c_idx = cc.reshape(1).astype(jnp.int32)
    chip_idx = (2 * cx + cy).reshape(1).astype(jnp.int32)
    comm = _StepComm(Wl, c_idx)
    units, tag = comm.units, comm.tag
    W = {n: Wl[n] for n in SMALL if n not in SMALL_SHARDED}
    loss_local, grad_x, grads, (s5_disc_vjp, s5_cot) = _step(x, loss_target, W, comm)
    loss = lax.psum(loss_local, MESH_AXES)

    outs4 = {}
    for (n, i), p, r in zip(units, comm.p1, comm.recv2):
        w2, m2, v2 = (d[n].reshape(-1, d[n].shape[-1]) for d in (Wl, Ml, Vl))
        res = adam_big("adam_" + tag(n, i), p[0], r, w2, m2, v2, chip_idx, layer=0 if i is None else i)
        if i is None:
            outs4[n] = [t.reshape(Wl[n].shape) for t in res]
        else:
            outs4.setdefault(n, []).append(res)
    for n in BIG:
        if Wl[n].shape[0] == DEPTH:
            outs4[n] = [jnp.stack([layer[k] for layer in outs4[n]]) for k in range(4)]
    out_g = {n: outs4[n][0] for n in BIG}
    out_d = {n: outs4[n][1] for n in BIG}
    out_m = {n: outs4[n][2] for n in BIG}
    out_v = {n: outs4[n][3] for n in BIG}

    s5_lin = ['ssm_a_re', 'ssm_a_im', 'ssm_log_dt', 'ssm_b_re', 'ssm_b_im']
    direct = [n for n in SMALL if n not in s5_lin]
    def full_shape(n):
        shp = list(Wl[n].shape)
        if n in SMALL_SHARDED:
            shp[SMALL_SHARDED[n]] *= N_DEV
        return tuple(shp)
    small_parts = [grads[n].reshape(full_shape(n)) for n in direct] + list(s5_cot)
    gsum = sum8(all_gather("gather_small_grads", [_pack(small_parts)])[0])
    summed = _unpack(gsum, [p.shape for p in small_parts])
    gsmall = dict(zip(direct, summed[:len(direct)]))
    s5g = s5_disc_vjp(tuple(summed[len(direct):]))
    for n, gval in zip(s5_lin, s5g):
        gsmall[n] = gval[None]
    for n, ax in SMALL_SHARDED.items():
        gsmall[n] = lax.dynamic_slice_in_dim(gsmall[n], my_idx * Wl[n].shape[ax], Wl[n].shape[ax], axis=ax)
    sm_shapes = [Wl[n].shape for n in SMALL]
    dS, mS, vS = adam_small(_pack([Wl[n] for n in SMALL]), _pack([gsmall[n] for n in SMALL]),
                            _pack([Ml[n] for n in SMALL]), _pack([Vl[n] for n in SMALL]))
    for n, gval in zip(SMALL, [gsmall[n] for n in SMALL]):
        out_g[n] = gval.reshape(Wl[n].shape)
    out_d.update(zip(SMALL, _unpack(dS, sm_shapes)))
    out_m.update(zip(SMALL, _unpack(mS, sm_shapes)))
    out_v.update(zip(SMALL, _unpack(vS, sm_shapes)))

    return (loss, grad_x, *[out_g[n] for n in WEIGHT_NAMES], *[out_d[n] for n in WEIGHT_NAMES],
            *[out_m[n] for n in WEIGHT_NAMES], *[out_v[n] for n in WEIGHT_NAMES])


def _unpack_gathered(g, shard_shapes):
    flat = g.reshape(N_DEV, -1)
    out, off = [], 0
    for s in shard_shapes:
        n = math.prod(s)
        out.append(flat[:, off:off + n].reshape((N_DEV,) + tuple(s)))
        off += n
    return out
```
